```python
import math
import jax, jax.numpy as jnp
from jax import lax
import numpy as np

D_MODEL = 1024
BATCH = 4
SEQ = 4096
DEPTH = 2
DEC_BATCH = 8
DEC_SEQ = 16
PAST_LEN = 1024

CHUNK = 64
D_SSM = 512
SSM_GROUP = 16
N_SSM_GROUPS = D_SSM // SSM_GROUP
SSM_STATE = 64
D_CONV = D_MODEL - D_SSM
CONV_WIDTH = 31
D_IN = D_SSM + 2 * D_CONV
N_EXPERT_GROUPS = 4
EXPERTS_PER_GROUP = 8
N_EXPERTS = N_EXPERT_GROUPS * EXPERTS_PER_GROUP
TOP_K = 2
D_EXPERT = 256
D_PLE = 256
EPS = 1e-6
DT_MIN = 1e-3
DT_MAX = 1e-1

kernel_name = 'hymba_s5_conformer_hmoe_stream_step'


def rms_norm(x, g):
    x32 = x.astype(jnp.float32)
    y = x32 * lax.rsqrt(jnp.mean(x32 * x32, axis=-1, keepdims=True) + EPS)
    return (y * g.astype(jnp.float32)).astype(x.dtype)


def layer_norm(x, g, b):
    x32 = x.astype(jnp.float32)
    mu = jnp.mean(x32, axis=-1, keepdims=True)
    xc = x32 - mu
    var = jnp.mean(xc * xc, axis=-1, keepdims=True)
    y = xc * lax.rsqrt(var + EPS) * g.astype(jnp.float32) + b.astype(jnp.float32)
    return y.astype(x.dtype)


def _cmul(ar, ai, br, bi):
    return ar * br - ai * bi, ar * bi + ai * br


def s5_mixer(u, h0_re, h0_im, a_re, a_im, b_re, b_im, c_re, c_im, d, log_dt, w_glu):
    f32 = jnp.float32
    bsz, t_len, _ = u.shape
    u32 = u.astype(f32).reshape(bsz, t_len, N_SSM_GROUPS, SSM_GROUP)
    dt = jnp.exp(log_dt.astype(f32))[:, None]
    ar, ai = a_re.astype(f32), a_im.astype(f32)
    mag = jnp.exp(dt * ar)
    abar_re, abar_im = mag * jnp.cos(dt * ai), mag * jnp.sin(dt * ai)
    inv = 1.0 / (ar * ar + ai * ai)
    coef_re, coef_im = _cmul(abar_re - 1.0, abar_im, ar * inv, -ai * inv)
    bbar_re, bbar_im = _cmul(coef_re[..., None], coef_im[..., None],
                             b_re.astype(f32), b_im.astype(f32))
    x_re = jnp.einsum('gnc,btgc->btgn', bbar_re, u32)
    x_im = jnp.einsum('gnc,btgc->btgn', bbar_im, u32)
    s_re, s_im = _cmul(abar_re, abar_im, h0_re.astype(f32), h0_im.astype(f32))
    x_re = x_re.at[:, 0].add(s_re)
    x_im = x_im.at[:, 0].add(s_im)
    a_seq_re = jnp.broadcast_to(abar_re, x_re.shape)
    a_seq_im = jnp.broadcast_to(abar_im, x_im.shape)

    def combine(left, right):
        a1r, a1i, b1r, b1i = left
        a2r, a2i, b2r, b2i = right
        anr, ani = _cmul(a2r, a2i, a1r, a1i)
        bnr, bni = _cmul(a2r, a2i, b1r, b1i)
        return anr, ani, bnr + b2r, bni + b2i

    _, _, h_re, h_im = lax.associative_scan(combine, (a_seq_re, a_seq_im, x_re, x_im), axis=1)
    y = (jnp.einsum('gcn,btgn->btgc', c_re.astype(f32), h_re)
         - jnp.einsum('gcn,btgn->btgc', c_im.astype(f32), h_im))
    y = y.reshape(bsz, t_len, D_SSM) + d.astype(f32) * u32.reshape(bsz, t_len, D_SSM)
    z = jax.nn.gelu(y)
    out = z * jax.nn.sigmoid(z @ w_glu.astype(f32))
    return out.astype(u.dtype), h_re[:, -1].astype(u.dtype), h_im[:, -1].astype(u.dtype)


def conv_mixer(z, left, conv_w, conv_b, ln_g, ln_b):
    v = z[..., :D_CONV] * jax.nn.sigmoid(z[..., D_CONV:])
    padded = jnp.concatenate([left.astype(v.dtype), v], axis=1)
    y = lax.conv_general_dilated(padded, conv_w[:, None, :].astype(v.dtype), window_strides=(1,),
                                 padding='VALID', dimension_numbers=('NWC', 'WIO', 'NWC'),
                                 feature_group_count=D_CONV)
    y = jax.nn.silu(layer_norm(y + conv_b, ln_g, ln_b))
    return y, padded[:, -(CONV_WIDTH - 1):]


def hier_moe(x, rg_w, rg_b, re_w, re_b, w_gate, w_up, w_down):
    f32 = jnp.float32
    xf = x.astype(f32)
    g_logits = xf @ rg_w.astype(f32) + rg_b.astype(f32)
    g_prob = jax.nn.softmax(g_logits, axis=-1)
    g_idx = jnp.argmax(g_logits, axis=-1)
    g_gate = jnp.take_along_axis(g_prob, g_idx[..., None], axis=-1)
    e_logits = (xf @ re_w.astype(f32) + re_b.astype(f32)).reshape(
        x.shape[:-1] + (N_EXPERT_GROUPS, EXPERTS_PER_GROUP))
    e_in_group = jnp.take_along_axis(e_logits, g_idx[..., None, None], axis=-2)[..., 0, :]
    top_v, top_i = lax.top_k(e_in_group, TOP_K)
    top_w = jax.nn.softmax(top_v, axis=-1) * g_gate
    e_idx = g_idx[..., None] * EXPERTS_PER_GROUP + top_i
    combine = jnp.sum(jax.nn.one_hot(e_idx, N_EXPERTS, dtype=f32) * top_w[..., None], axis=-2)
    hg = jnp.einsum('btd,edh->bteh', x, w_gate)
    hu = jnp.einsum('btd,edh->bteh', x, w_up)
    act = jax.nn.silu(hg) * hu * combine[..., None].astype(hu.dtype)
    return jnp.einsum('bteh,ehd->btd', act, w_down).astype(x.dtype)


def hybrid_layer(x, p, h0_re, h0_im, conv_left, norm_mix, w_in, a_re, a_im, b_re, b_im, c_re, c_im,
                 d, log_dt, w_glu, conv_w, conv_b, ln_g, ln_b, w_out, norm_ffn, rg_w, rg_b, re_w, re_b,
                 we_g, we_u, we_d, norm_ple, ple_w, ple_gate_w):
    hn = rms_norm(x, norm_mix)
    proj = hn @ w_in
    y_a, h_re, h_im = s5_mixer(proj[..., :D_SSM], h0_re, h0_im, a_re, a_im, b_re, b_im,
                               c_re, c_im, d, log_dt, w_glu)
    y_b, conv_new = conv_mixer(proj[..., D_SSM:], conv_left, conv_w, conv_b, ln_g, ln_b)
    x = x + jnp.concatenate([y_a, y_b], axis=-1) @ w_out
    x = x + hier_moe(rms_norm(x, norm_ffn), rg_w, rg_b, re_w, re_b, we_g, we_u, we_d)
    gate = jax.nn.sigmoid(rms_norm(x, norm_ple) @ ple_gate_w)
    x = x + (p @ ple_w) * gate
    return x, h_re, h_im, conv_new


def setup_inputs(seed: int = 0) -> dict:
    key = jax.random.key(seed)
    ks = iter(jax.random.split(key, 48))
    f32 = jnp.float32

    def nrm(shape, scale):
        return scale * jax.random.normal(next(ks), shape, f32)

    G, N = N_SSM_GROUPS, SSM_STATE
    return {
        'x_prompt': nrm((BATCH, SEQ, D_MODEL), 1.0),
        'x_sample': nrm((DEC_BATCH, DEC_SEQ, D_MODEL), 1.0),
        'p_prompt': nrm((DEPTH, BATCH, SEQ, D_PLE), 1.0),
        'p_sample': nrm((DEPTH, DEC_BATCH, DEC_SEQ, D_PLE), 1.0),
        'state_ssm_re': nrm((DEPTH, DEC_BATCH, G, N), 0.1),
        'state_ssm_im': nrm((DEPTH, DEC_BATCH, G, N), 0.1),
        'cache_conv': nrm((DEPTH, DEC_BATCH, CONV_WIDTH - 1, D_CONV), 1.0),
        'norm_mix': 1.0 + nrm((DEPTH, D_MODEL), 0.02),
        'w_in': nrm((DEPTH, D_MODEL, D_IN), D_MODEL ** -0.5),
        'ssm_a_re': -0.5 + nrm((DEPTH, G, N), 0.01),
        'ssm_a_im': math.pi * jnp.arange(N, dtype=f32) + nrm((DEPTH, G, N), 0.01),
        'ssm_b_re': nrm((DEPTH, G, N, SSM_GROUP), (2 * SSM_GROUP) ** -0.5),
        'ssm_b_im': nrm((DEPTH, G, N, SSM_GROUP), (2 * SSM_GROUP) ** -0.5),
        'ssm_c_re': nrm((DEPTH, G, SSM_GROUP, N), (2 * N) ** -0.5),
        'ssm_c_im': nrm((DEPTH, G, SSM_GROUP, N), (2 * N) ** -0.5),
        'ssm_d': nrm((DEPTH, D_SSM), 1.0),
        'ssm_log_dt': jax.random.uniform(next(ks), (DEPTH, G), f32, math.log(DT_MIN), math.log(DT_MAX)),
        'w_ssm_glu': nrm((DEPTH, D_SSM, D_SSM), D_SSM ** -0.5),
        'conv_w': nrm((DEPTH, CONV_WIDTH, D_CONV), CONV_WIDTH ** -0.5),
        'conv_b': nrm((DEPTH, D_CONV), 0.02),
        'conv_ln_g': 1.0 + nrm((DEPTH, D_CONV), 0.02),
        'conv_ln_b': nrm((DEPTH, D_CONV), 0.02),
        'w_out': nrm((DEPTH, D_MODEL, D_MODEL), D_MODEL ** -0.5),
        'norm_ffn': 1.0 + nrm((DEPTH, D_MODEL), 0.02),
        'router_group_w': nrm((DEPTH, D_MODEL, N_EXPERT_GROUPS), D_MODEL ** -0.5),
        'router_group_b': nrm((DEPTH, N_EXPERT_GROUPS), 0.01),
        'router_expert_w': nrm((DEPTH, D_MODEL, N_EXPERTS), D_MODEL ** -0.5),
        'router_expert_b': nrm((DEPTH, N_EXPERTS), 0.01),
        'expert_w_gate': nrm((DEPTH, N_EXPERTS, D_MODEL, D_EXPERT), D_MODEL ** -0.5),
        'expert_w_up': nrm((DEPTH, N_EXPERTS, D_MODEL, D_EXPERT), D_MODEL ** -0.5),
        'expert_w_down': nrm((DEPTH, N_EXPERTS, D_EXPERT, D_MODEL), D_EXPERT ** -0.5),
        'norm_ple': 1.0 + nrm((DEPTH, D_MODEL), 0.02),
        'ple_w': nrm((DEPTH, D_PLE, D_MODEL), D_PLE ** -0.5),
        'ple_gate_w': nrm((DEPTH, D_MODEL, D_MODEL), D_MODEL ** -0.5),
        'norm_final': 1.0 + nrm((D_MODEL,), 0.02),
    }


def reference(x_prompt, x_sample, p_prompt, p_sample, state_ssm_re, state_ssm_im, cache_conv,
              norm_mix, w_in, ssm_a_re, ssm_a_im, ssm_b_re, ssm_b_im, ssm_c_re, ssm_c_im, ssm_d,
              ssm_log_dt, w_ssm_glu, conv_w, conv_b, conv_ln_g, conv_ln_b, w_out, norm_ffn,
              router_group_w, router_group_b, router_expert_w, router_expert_b,
              expert_w_gate, expert_w_up, expert_w_down, norm_ple, ple_w, ple_gate_w, norm_final):
    bsz = x_prompt.shape[0]
    h0_zero = jnp.zeros((bsz, N_SSM_GROUPS, SSM_STATE), jnp.float32)
    conv_zero = jnp.zeros((bsz, CONV_WIDTH - 1, D_CONV), x_prompt.dtype)
    xp, xs = x_prompt, x_sample
    pr_re, pr_im, pr_conv, sm_re, sm_im, sm_conv = [], [], [], [], [], []
    for i in range(DEPTH):
        w_i = (norm_mix[i], w_in[i], ssm_a_re[i], ssm_a_im[i], ssm_b_re[i], ssm_b_im[i],
               ssm_c_re[i], ssm_c_im[i], ssm_d[i], ssm_log_dt[i], w_ssm_glu[i], conv_w[i], conv_b[i],
               conv_ln_g[i], conv_ln_b[i], w_out[i], norm_ffn[i], router_group_w[i], router_group_b[i],
               router_expert_w[i], router_expert_b[i], expert_w_gate[i], expert_w_up[i],
               expert_w_down[i], norm_ple[i], ple_w[i], ple_gate_w[i])
        xp, hr, hi, cv = hybrid_layer(xp, p_prompt[i], h0_zero, h0_zero, conv_zero, *w_i)
        pr_re.append(hr); pr_im.append(hi); pr_conv.append(cv)
        xs, hr, hi, cv = hybrid_layer(xs, p_sample[i], state_ssm_re[i], state_ssm_im[i], cache_conv[i], *w_i)
        sm_re.append(hr); sm_im.append(hi); sm_conv.append(cv)
    y_prompt = rms_norm(xp, norm_final)
    y_sample = rms_norm(xs, norm_final)
    return (y_prompt, y_sample, jnp.stack(pr_re), jnp.stack(pr_im), jnp.stack(pr_conv),
            jnp.stack(sm_re), jnp.stack(sm_im), jnp.stack(sm_conv))
```

```python
import functools

import jax
import jax.numpy as jnp
from jax import lax
from jax.experimental import pallas as pl
from jax.experimental.pallas import tpu as pltpu

F32 = jnp.float32
BF16 = jnp.bfloat16

D_MODEL = 1024
D_SSM = 512
SSM_GROUP = 16
N_GROUPS = D_SSM // SSM_GROUP
N_PAIRS = N_GROUPS // 2
SSM_STATE = 64
D_CONV = 512
CONV_WIDTH = 31
CONV_HALO = 32
N_EXPERT_GROUPS = 4
EXPERTS_PER_GROUP = 8
N_EXPERTS = 32
D_EXPERT = 256
D_PLE = 256
EPS = 1e-6
S5_CHUNK = 16
LANES = 128
SUBLANES = 8
VMEM_LIMIT = 56 * 1024 * 1024


def _params(*sem):
    return pltpu.CompilerParams(dimension_semantics=sem, vmem_limit_bytes=VMEM_LIMIT)


def _rms(x, g):
    return x * lax.rsqrt(jnp.mean(x * x, axis=-1, keepdims=True) + EPS) * g


def _bdot(a, b):
    return jnp.dot(a.astype(BF16), b, preferred_element_type=F32)


def _row_spec(tm, width):
    return pl.BlockSpec((tm, width), lambda i: (i, 0))


def _const_spec(shape):
    return pl.BlockSpec(shape, lambda i: (0,) * len(shape))


def _inproj_kernel(x_ref, g_ref, w_ref, u_ref, z_ref):
    hn = _rms(x_ref[...], g_ref[...])
    proj = _bdot(hn, w_ref[...])
    u_ref[...] = proj[:, :D_SSM]
    z_ref[...] = proj[:, D_SSM:]


def _inproj(x, g, w_bf, tm):
    n = x.shape[0]
    d_in = w_bf.shape[1]
    return pl.pallas_call(
        _inproj_kernel,
        grid=(n // tm,),
        in_specs=[_row_spec(tm, D_MODEL), _const_spec((1, D_MODEL)), _const_spec((D_MODEL, d_in))],
        out_specs=[_row_spec(tm, D_SSM), _row_spec(tm, d_in - D_SSM)],
        out_shape=[jax.ShapeDtypeStruct((n, D_SSM), F32), jax.ShapeDtypeStruct((n, d_in - D_SSM), F32)],
        compiler_params=_params("parallel"),
        name="inproj",
    )(x, g.reshape(1, D_MODEL), w_bf)


def _s5_prep_kernel(ar_ref, ai_ref, ldt_ref, bre_ref, bim_ref, cre_ref, cim_ref,
                    wtre_ref, wtim_ref, brre_ref, brim_ref, kcat_ref, are_ref, aim_ref):
    ar, ai = ar_ref[...], ai_ref[...]
    dt = jnp.exp(ldt_ref[...])
    n_pow = S5_CHUNK + 8
    k = lax.broadcasted_iota(jnp.int32, (n_pow, SSM_STATE), 0).astype(F32)
    mag = jnp.exp(k * (dt * ar))
    ang = k * (dt * ai)
    p_re, p_im = mag * jnp.cos(ang), mag * jnp.sin(ang)
    inv = 1.0 / (ar * ar + ai * ai)
    ab_re, ab_im = p_re[1:2], p_im[1:2]
    ia_re, ia_im = ar * inv, -ai * inv
    coef_re = (ab_re - 1.0) * ia_re - ab_im * ia_im
    coef_im = (ab_re - 1.0) * ia_im + ab_im * ia_re
    bre, bim = bre_ref[...], bim_ref[...]
    bb_re = coef_re * bre - coef_im * bim
    bb_im = coef_re * bim + coef_im * bre
    cre, cim = cre_ref[...], cim_ref[...]
    for kk in range(S5_CHUNK + 1):
        pr, pi = p_re[kk:kk + 1], p_im[kk:kk + 1]
        rows = slice(kk * SSM_GROUP, (kk + 1) * SSM_GROUP)
        wtre_ref[rows, :] = pr * cre - pi * cim
        wtim_ref[rows, :] = -pi * cre - pr * cim
        if kk < S5_CHUNK:
            brre_ref[rows, :] = pr * bb_re - pi * bb_im
            brim_ref[rows, :] = pi * bb_re + pr * bb_im
    n_tap = S5_CHUNK * SSM_GROUP
    nt = (((1,), (1,)), ((), ()))
    kcat_ref[...] = (
        lax.dot_general(bb_re, wtre_ref[0:n_tap, :], nt, precision=lax.Precision.HIGHEST,
                        preferred_element_type=F32)
        + lax.dot_general(bb_im, wtim_ref[0:n_tap, :], nt, precision=lax.Precision.HIGHEST,
                          preferred_element_type=F32))
    are_ref[...] = p_re[S5_CHUNK:S5_CHUNK + 1]
    aim_ref[...] = p_im[S5_CHUNK:S5_CHUNK + 1]


def _s5_prep(a_re, a_im, log_dt, b_re, b_im, c_re, c_im):
    g, n, c = N_GROUPS, SSM_STATE, SSM_GROUP
    n_tap = S5_CHUNK * c

    def gspec(*shape):
        return pl.BlockSpec((None,) + shape, lambda i: (i,) + (0,) * len(shape))

    outs = pl.pallas_call(
        _s5_prep_kernel,
        grid=(g,),
        in_specs=[gspec(1, n), gspec(1, n), gspec(1, 1), gspec(c, n), gspec(c, n), gspec(c, n), gspec(c, n)],
        out_specs=[gspec(n_tap + c, n), gspec(n_tap + c, n), gspec(n_tap, n), gspec(n_tap, n),
                   gspec(c, n_tap), gspec(1, n), gspec(1, n)],
        out_shape=[jax.ShapeDtypeStruct((g, n_tap + c, n), F32), jax.ShapeDtypeStruct((g, n_tap + c, n), F32),
                   jax.ShapeDtypeStruct((g, n_tap, n), F32), jax.ShapeDtypeStruct((g, n_tap, n), F32),
                   jax.ShapeDtypeStruct((g, c, n_tap), F32),
                   jax.ShapeDtypeStruct((g, 1, n), F32), jax.ShapeDtypeStruct((g, 1, n), F32)],
        compiler_params=_params("parallel"),
        name="s5_prep",
    )(a_re.reshape(g, 1, n), a_im.reshape(g, 1, n), log_dt.reshape(g, 1, 1),
      jnp.swapaxes(b_re, 1, 2), jnp.swapaxes(b_im, 1, 2), c_re, c_im)
    wt_re, wt_im, br_re, br_im, kcat, a_re16, a_im16 = outs

    L = S5_CHUNK
    kk = kcat.reshape(g, c, L, c)
    lag = jnp.arange(L)[None, :] - jnp.arange(L)[:, None]
    m = kk[:, :, jnp.clip(lag, 0, L - 1), :]
    m = jnp.where((lag >= 0)[None, None, :, :, None], m, 0.0)
    m = m.transpose(0, 2, 1, 3, 4).reshape(g, n_tap, n_tap)

    def flip_k(b):
        return b.reshape(g, L, c, n)[:, ::-1].reshape(g, n_tap, n)

    ws_re, ws_im = flip_k(br_re), flip_k(br_im)
    z = jnp.zeros((N_PAIRS, n_tap, n), F32)
    e, o = slice(0, None, 2), slice(1, None, 2)
    wsp = jnp.concatenate([
        jnp.concatenate([ws_re[e], z, ws_im[e], z], axis=2),
        jnp.concatenate([z, ws_re[o], z, ws_im[o]], axis=2)], axis=1)
    wo_re = jnp.swapaxes(wt_re[:, c:], 1, 2)
    wo_im = jnp.swapaxes(wt_im[:, c:], 1, 2)
    zo = jnp.zeros((N_PAIRS, n, n_tap), F32)
    wop = jnp.concatenate([
        jnp.concatenate([wo_re[e], zo], axis=2), jnp.concatenate([zo, wo_re[o]], axis=2),
        jnp.concatenate([wo_im[e], zo], axis=2), jnp.concatenate([zo, wo_im[o]], axis=2)], axis=1)
    a16 = jnp.concatenate([a_re16[e], a_re16[o], a_im16[e], a_im16[o]], axis=2)
    return (m.reshape(N_PAIRS, 2, n_tap, n_tap).astype(BF16), wsp.astype(BF16), wop.astype(BF16), a16)


def _s5_kernel(x_ref, h0_ref, m_ref, ws_ref, wo_ref, a_ref, y_ref, hf_ref, s_scr, hp_scr, *, nj, nb):
    half = 2 * SSM_STATE
    x0, x1 = x_ref[0], x_ref[1]
    xp = jnp.concatenate([x0, x1], axis=1)
    s_scr[...] = jnp.dot(xp, ws_ref[...], preferred_element_type=F32)
    a = a_ref[...]
    a_re, a_im = a[:, :half], a[:, half:]
    h0 = h0_ref[...]

    def step(h_re, h_im, s):
        return (a_re * h_re - a_im * h_im + s[:, :half],
                a_re * h_im + a_im * h_re + s[:, half:])

    top = lax.broadcasted_iota(jnp.int32, (SUBLANES, half), 0) < nb

    def body(i, carry):
        h_re, h_im = carry
        r0 = pl.multiple_of(i * SUBLANES, SUBLANES)
        s = s_scr[pl.ds(r0, SUBLANES), :]
        if nb == SUBLANES:
            hp_re, hp_im = h_re, h_im
            n_re, n_im = step(h_re, h_im, s)
        else:
            t_re, t_im = step(h_re, h_im, s)
            hp_re = jnp.where(top, h_re, pltpu.roll(t_re, nb, 0))
            hp_im = jnp.where(top, h_im, pltpu.roll(t_im, nb, 0))
            t_re, t_im = step(hp_re, hp_im, s)
            n_re, n_im = pltpu.roll(t_re, nb, 0), pltpu.roll(t_im, nb, 0)
        hp_scr[pl.ds(r0, SUBLANES), :] = jnp.concatenate([hp_re, hp_im], axis=1)
        return n_re, n_im

    h_re, h_im = lax.fori_loop(0, nj * nb // SUBLANES, body, (h0[:, :half], h0[:, half:]))
    hf_ref[...] = jnp.concatenate([h_re, h_im], axis=1)
    yc = _bdot(hp_scr[...], wo_ref[...])
    n_tap = S5_CHUNK * SSM_GROUP
    y_ref[0] = jnp.dot(x0, m_ref[0], preferred_element_type=F32) + yc[:, :n_tap]
    y_ref[1] = jnp.dot(x1, m_ref[1], preferred_element_type=F32) + yc[:, n_tap:]


def _s5_core(xg, h0p, m, wsp, wop, a16, nj, nb):
    assert nb in (SUBLANES // 2, SUBLANES) and (nj * nb) % SUBLANES == 0, (nj, nb)
    r = nj * nb
    n_tap = S5_CHUNK * SSM_GROUP
    st = 4 * SSM_STATE

    def pspec(*shape):
        return pl.BlockSpec((None,) + shape, lambda p: (p,) + (0,) * len(shape))

    return pl.pallas_call(
        functools.partial(_s5_kernel, nj=nj, nb=nb),
        grid=(N_PAIRS,),
        in_specs=[pspec(2, r, n_tap), pspec(SUBLANES, st), pspec(2, n_tap, n_tap), pspec(2 * n_tap, st),
                  pspec(st, 2 * n_tap), pspec(1, st)],
        out_specs=[pspec(2, r, n_tap), pspec(SUBLANES, st)],
        out_shape=[jax.ShapeDtypeStruct((N_PAIRS, 2, r, n_tap), F32),
                   jax.ShapeDtypeStruct((N_PAIRS, SUBLANES, st), F32)],
        scratch_shapes=[pltpu.VMEM((r, st), F32), pltpu.VMEM((r, st), F32)],
        compiler_params=_params("parallel"),
        name="s5_core",
    )(xg, h0p, m, wsp, wop, a16)


def _s5_mixer(u, h0_re, h0_im, prep, bsz, t_len):
    m, wsp, wop, a16 = prep
    nj = t_len // S5_CHUNK
    xg = u.astype(BF16).reshape(bsz, nj, S5_CHUNK, N_PAIRS, 2, SSM_GROUP)
    xg = xg.transpose(3, 4, 1, 0, 2, 5).reshape(N_PAIRS, 2, nj * bsz, S5_CHUNK * SSM_GROUP)
    h0p = jnp.concatenate([h0_re.reshape(bsz, N_PAIRS, 2 * SSM_STATE),
                           h0_im.reshape(bsz, N_PAIRS, 2 * SSM_STATE)], axis=2).transpose(1, 0, 2)
    h0p = jnp.pad(h0p.astype(F32), ((0, 0), (0, SUBLANES - bsz), (0, 0)))
    y, hf = _s5_core(xg, h0p, m, wsp, wop, a16, nj, bsz)
    hf = hf[:, :bsz]
    y = y.reshape(N_PAIRS, 2, nj, bsz, S5_CHUNK, SSM_GROUP).transpose(3, 2, 4, 0, 1, 5)
    y = y.reshape(bsz * t_len, D_SSM)
    hf = hf.transpose(1, 0, 2)
    hf_re = hf[:, :, :2 * SSM_STATE].reshape(bsz, N_GROUPS, SSM_STATE)
    hf_im = hf[:, :, 2 * SSM_STATE:].reshape(bsz, N_GROUPS, SSM_STATE)
    return y, hf_re, hf_im


def _conv_kernel(z_ref, left_ref, w_ref, b_ref, g_ref, beta_ref, y_ref, cn_ref, vbuf, *, tm):
    @pl.when(pl.program_id(1) == 0)
    def _():
        vbuf[0:CONV_HALO, :] = left_ref[...]

    z = z_ref[...]
    vbuf[CONV_HALO:CONV_HALO + tm, :] = z[:, :D_CONV] * jax.nn.sigmoid(z[:, D_CONV:])
    first = CONV_HALO - (CONV_WIDTH - 1)
    acc = jnp.zeros((tm, D_CONV), F32)
    for k in range(CONV_WIDTH):
        acc = acc + w_ref[k:k + 1, :] * vbuf[first + k:first + k + tm, :]
    y = acc + b_ref[...]
    mu = jnp.mean(y, axis=-1, keepdims=True)
    yc = y - mu
    var = jnp.mean(yc * yc, axis=-1, keepdims=True)
    yn = yc * lax.rsqrt(var + EPS) * g_ref[...] + beta_ref[...]
    y_ref[...] = yn * jax.nn.sigmoid(yn)
    cn_ref[...] = vbuf[tm + first:tm + CONV_HALO, :]
    vbuf[0:CONV_HALO, :] = vbuf[tm:tm + CONV_HALO, :]


def _conv_mixer(z, left, w, b, g, beta, bsz, t_len, tm):
    left = jnp.pad(left.astype(F32), ((0, 0), (CONV_HALO - (CONV_WIDTH - 1), 0), (0, 0)))
    vec = pl.BlockSpec((1, D_CONV), lambda bi, j: (0, 0))
    return pl.pallas_call(
        functools.partial(_conv_kernel, tm=tm),
        grid=(bsz, t_len // tm),
        in_specs=[pl.BlockSpec((None, tm, 2 * D_CONV), lambda bi, j: (bi, j, 0)),
                  pl.BlockSpec((None, CONV_HALO, D_CONV), lambda bi, j: (bi, 0, 0)),
                  pl.BlockSpec((CONV_WIDTH, D_CONV), lambda bi, j: (0, 0)), vec, vec, vec],
        out_specs=[pl.BlockSpec((None, tm, D_CONV), lambda bi, j: (bi, j, 0)),
                   pl.BlockSpec((None, CONV_WIDTH - 1, D_CONV), lambda bi, j: (bi, 0, 0))],
        out_shape=[jax.ShapeDtypeStruct((bsz, t_len, D_CONV), F32),
                   jax.ShapeDtypeStruct((bsz, CONV_WIDTH - 1, D_CONV), F32)],
        scratch_shapes=[pltpu.VMEM((tm + CONV_HALO, D_CONV), F32)],
        compiler_params=_params("parallel", "arbitrary"),
        name="conv_mixer",
    )(z, left, w, b.reshape(1, D_CONV), g.reshape(1, D_CONV), beta.reshape(1, D_CONV))


def _route(logits):
    lane_i = lax.broadcasted_iota(jnp.int32, logits.shape, 1)
    lane = lane_i.astype(F32)
    group_of_lane = (lane_i >> 3).astype(F32)
    neg = -jnp.inf
    far = float(LANES)
    is_g = (lane_i >= N_EXPERTS) & (lane_i < N_EXPERTS + N_EXPERT_GROUPS)
    gl = jnp.where(is_g, logits, neg)
    g_max = jnp.max(gl, axis=-1, keepdims=True)
    g_lane = jnp.min(jnp.where(gl == g_max, lane, far), axis=-1, keepdims=True)
    g_gate = 1.0 / jnp.sum(jnp.exp(gl - g_max), axis=-1, keepdims=True)
    g_idx = g_lane - float(N_EXPERTS)
    in_group = (lane_i < N_EXPERTS) & (group_of_lane == g_idx)
    el = jnp.where(in_group, logits, neg)
    v1 = jnp.max(el, axis=-1, keepdims=True)
    i1 = jnp.min(jnp.where(el == v1, lane, far), axis=-1, keepdims=True)
    el2 = jnp.where(lane == i1, neg, el)
    v2 = jnp.max(el2, axis=-1, keepdims=True)
    i2 = jnp.min(jnp.where(el2 == v2, lane, far), axis=-1, keepdims=True)
    e2 = jnp.exp(v2 - v1)
    w1 = g_gate / (1.0 + e2)
    w2 = g_gate * e2 / (1.0 + e2)
    return jnp.where(lane == i1, w1, 0.0) + jnp.where(lane == i2, w2, 0.0)


def _outproj_kernel(x_ref, ys_ref, u_ref, yb_ref, d_ref, wglu_ref, wout_ref, nf_ref, rw_ref, rb_ref,
                    x1_ref, hn_ref, comb_ref):
    z = jax.nn.gelu(ys_ref[...] + d_ref[...] * u_ref[...])
    ya = z * jax.nn.sigmoid(_bdot(z, wglu_ref[...]))
    mix = _bdot(ya, wout_ref[0:D_SSM, :]) + _bdot(yb_ref[...], wout_ref[D_SSM:, :])
    x1 = x_ref[...] + mix
    x1_ref[...] = x1
    hn = _rms(x1, nf_ref[...])
    hn_ref[...] = hn.astype(BF16)
    logits = jnp.dot(hn, rw_ref[...], precision=lax.Precision.HIGHEST,
                     preferred_element_type=F32) + rb_ref[...]
    comb_ref[...] = _route(logits)


def _outproj(x, ys, u, yb, d, wglu_bf, wout_bf, nf, rw, rb, tm):
    n = x.shape[0]
    return pl.pallas_call(
        _outproj_kernel,
        grid=(n // tm,),
        in_specs=[_row_spec(tm, D_MODEL), _row_spec(tm, D_SSM), _row_spec(tm, D_SSM), _row_spec(tm, D_CONV),
                  _const_spec((1, D_SSM)), _const_spec((D_SSM, D_SSM)), _const_spec((D_MODEL, D_MODEL)),
                  _const_spec((1, D_MODEL)), _const_spec((D_MODEL, LANES)), _const_spec((1, LANES))],
        out_specs=[_row_spec(tm, D_MODEL), _row_spec(tm, D_MODEL), _row_spec(tm, LANES)],
        out_shape=[jax.ShapeDtypeStruct((n, D_MODEL), F32), jax.ShapeDtypeStruct((n, D_MODEL), BF16),
                   jax.ShapeDtypeStruct((n, LANES), F32)],
        compiler_params=_params("parallel"),
        name="outproj_router",
    )(x, ys, u, yb, d.reshape(1, D_SSM), wglu_bf, wout_bf, nf.reshape(1, D_MODEL), rw, rb)


def _moe_kernel(x1_ref, hn_ref, comb_ref, wg_ref, wu_ref, wd_ref, o_ref, acc_ref):
    e = pl.program_id(1)

    @pl.when(e == 0)
    def _():
        acc_ref[...] = jnp.zeros_like(acc_ref)

    h = hn_ref[...]
    hg = jnp.dot(h, wg_ref[...], preferred_element_type=F32)
    hu = jnp.dot(h, wu_ref[...], preferred_element_type=F32)
    comb = comb_ref[...]
    lane = lax.broadcasted_iota(jnp.int32, comb.shape, 1)
    c = jnp.sum(jnp.where(lane == e, comb, 0.0), axis=-1, keepdims=True)
    act = hg * jax.nn.sigmoid(hg) * hu * c
    acc_ref[...] += _bdot(act, wd_ref[...])

    @pl.when(e == N_EXPERTS - 1)
    def _():
        o_ref[...] = x1_ref[...] + acc_ref[...]


def _moe(x1, hn, comb, wg_bf, wu_bf, wd_bf, tm):
    n = x1.shape[0]
    row = lambda w: pl.BlockSpec((tm, w), lambda i, e: (i, 0))
    return pl.pallas_call(
        _moe_kernel,
        grid=(n // tm, N_EXPERTS),
        in_specs=[row(D_MODEL), row(D_MODEL), row(LANES),
                  pl.BlockSpec((None, D_MODEL, D_EXPERT), lambda i, e: (e, 0, 0)),
                  pl.BlockSpec((None, D_MODEL, D_EXPERT), lambda i, e: (e, 0, 0)),
                  pl.BlockSpec((None, D_EXPERT, D_MODEL), lambda i, e: (e, 0, 0))],
        out_specs=row(D_MODEL),
        out_shape=jax.ShapeDtypeStruct((n, D_MODEL), F32),
        scratch_shapes=[pltpu.VMEM((tm, D_MODEL), F32)],
        compiler_params=_params("parallel", "arbitrary"),
        name="moe",
    )(x1, hn, comb, wg_bf, wu_bf, wd_bf)


def _ple_kernel(x_ref, p_ref, np_ref, wple_ref, wgate_ref, nfin_ref, o_ref, *, final):
    x = x_ref[...]
    gate = jax.nn.sigmoid(_bdot(_rms(x, np_ref[...]), wgate_ref[...]))
    out = x + _bdot(p_ref[...], wple_ref[...]) * gate
    if final:
        out = _rms(out, nfin_ref[...])
    o_ref[...] = out


def _ple(x, p, npl, wple_bf, wgate_bf, nfin, tm, final):
    n = x.shape[0]
    return pl.pallas_call(
        functools.partial(_ple_kernel, final=final),
        grid=(n // tm,),
        in_specs=[_row_spec(tm, D_MODEL), _row_spec(tm, D_PLE), _const_spec((1, D_MODEL)),
                  _const_spec((D_PLE, D_MODEL)), _const_spec((D_MODEL, D_MODEL)), _const_spec((1, D_MODEL))],
        out_specs=_row_spec(tm, D_MODEL),
        out_shape=jax.ShapeDtypeStruct((n, D_MODEL), F32),
        compiler_params=_params("parallel"),
        name="ple",
    )(x, p, npl.reshape(1, D_MODEL), wple_bf, wgate_bf, nfin.reshape(1, D_MODEL))


def _layer(x, p, h0_re, h0_im, conv_left, w, bsz, t_len, tm, tm_conv, tm_moe, final):
    n = bsz * t_len
    u, z = _inproj(x, w["norm_mix"], w["w_in"], tm)
    ys, hf_re, hf_im = _s5_mixer(u, h0_re, h0_im, w["s5"], bsz, t_len)
    yb, conv_new = _conv_mixer(z.reshape(bsz, t_len, 2 * D_CONV), conv_left, w["conv_w"], w["conv_b"],
                               w["conv_ln_g"], w["conv_ln_b"], bsz, t_len, tm_conv)
    x1, hn, comb = _outproj(x, ys, u, yb.reshape(n, D_CONV), w["ssm_d"], w["w_ssm_glu"], w["w_out"],
                            w["norm_ffn"], w["router_w"], w["router_b"], tm)
    x2 = _moe(x1, hn, comb, w["expert_w_gate"], w["expert_w_up"], w["expert_w_down"], tm_moe)
    x3 = _ple(x2, p, w["norm_ple"], w["ple_w"], w["ple_gate_w"], w["norm_final"], tm, final)
    return x3, hf_re, hf_im, conv_new


def kernel(x_prompt, x_sample, p_prompt, p_sample, state_ssm_re, state_ssm_im, cache_conv, norm_mix, w_in, ssm_a_re, ssm_a_im, ssm_b_re, ssm_b_im, ssm_c_re, ssm_c_im, ssm_d, ssm_log_dt, w_ssm_glu, conv_w, conv_b, conv_ln_g, conv_ln_b, w_out, norm_ffn, router_group_w, router_group_b, router_expert_w, router_expert_b, expert_w_gate, expert_w_up, expert_w_down, norm_ple, ple_w, ple_gate_w, norm_final):
    depth = w_in.shape[0]
    bp, tp, _ = x_prompt.shape
    bs, ts, _ = x_sample.shape
    xp = x_prompt.reshape(bp * tp, D_MODEL)
    xs = x_sample.reshape(bs * ts, D_MODEL)
    zero_state = jnp.zeros((bp, N_GROUPS, SSM_STATE), F32)
    zero_conv = jnp.zeros((bp, CONV_WIDTH - 1, D_CONV), F32)
    pad_lanes = LANES - N_EXPERTS - N_EXPERT_GROUPS
    outs = {k: [] for k in ("pr_re", "pr_im", "pr_conv", "sm_re", "sm_im", "sm_conv")}
    for i in range(depth):
        w = {
            "norm_mix": norm_mix[i], "w_in": w_in[i].astype(BF16),
            "s5": _s5_prep(ssm_a_re[i], ssm_a_im[i], ssm_log_dt[i], ssm_b_re[i], ssm_b_im[i],
                           ssm_c_re[i], ssm_c_im[i]),
            "ssm_d": ssm_d[i], "w_ssm_glu": w_ssm_glu[i].astype(BF16),
            "conv_w": conv_w[i], "conv_b": conv_b[i], "conv_ln_g": conv_ln_g[i], "conv_ln_b": conv_ln_b[i],
            "w_out": w_out[i].astype(BF16), "norm_ffn": norm_ffn[i],
            "router_w": jnp.pad(jnp.concatenate([router_expert_w[i], router_group_w[i]], axis=1),
                                ((0, 0), (0, pad_lanes))),
            "router_b": jnp.pad(jnp.concatenate([router_expert_b[i], router_group_b[i]]),
                                (0, pad_lanes)).reshape(1, LANES),
            "expert_w_gate": expert_w_gate[i].astype(BF16), "expert_w_up": expert_w_up[i].astype(BF16),
            "expert_w_down": expert_w_down[i].astype(BF16),
            "norm_ple": norm_ple[i], "ple_w": ple_w[i].astype(BF16), "ple_gate_w": ple_gate_w[i].astype(BF16),
            "norm_final": norm_final,
        }
        final = i == depth - 1
        xp, hr, hi, cv = _layer(xp, p_prompt[i].reshape(bp * tp, D_PLE), zero_state, zero_state, zero_conv, w,
                                bp, tp, tm=512, tm_conv=512, tm_moe=1024, final=final)
        outs["pr_re"].append(hr); outs["pr_im"].append(hi); outs["pr_conv"].append(cv)
        xs, hr, hi, cv = _layer(xs, p_sample[i].reshape(bs * ts, D_PLE), state_ssm_re[i], state_ssm_im[i],
                                cache_conv[i], w, bs, ts, tm=bs * ts, tm_conv=ts, tm_moe=bs * ts, final=final)
        outs["sm_re"].append(hr); outs["sm_im"].append(hi); outs["sm_conv"].append(cv)
    return (xp.reshape(bp, tp, D_MODEL), xs.reshape(bs, ts, D_MODEL),
            jnp.stack(outs["pr_re"]), jnp.stack(outs["pr_im"]), jnp.stack(outs["pr_conv"]),
            jnp.stack(outs["sm_re"]), jnp.stack(outs["sm_im"]), jnp.stack(outs["sm_conv"]))
```

```python
import functools

import jax
import jax.numpy as jnp
from jax import lax
from jax.experimental import pallas as pl
from jax.experimental.pallas import tpu as pltpu

F32 = jnp.float32
BF16 = jnp.bfloat16

D_MODEL = 1024
D_SSM = 512
SSM_GROUP = 16
N_GROUPS = D_SSM // SSM_GROUP
N_PAIRS = N_GROUPS // 2
SSM_STATE = 64
D_CONV = 512
CONV_WIDTH = 31
CONV_HALO = 32
N_EXPERT_GROUPS = 4
EXPERTS_PER_GROUP = 8
N_EXPERTS = 32
D_EXPERT = 256
D_PLE = 256
EPS = 1e-6
S5_CHUNK = 16
LANES = 128
SUBLANES = 8
N_SLABS = D_SSM // LANES
GROUPS_PER_SLAB = LANES // SSM_GROUP
PAIRS_PER_SLAB = GROUPS_PER_SLAB // 2
VMEM_LIMIT = 56 * 1024 * 1024


def _params(*sem):
    return pltpu.CompilerParams(dimension_semantics=sem, vmem_limit_bytes=VMEM_LIMIT)


def _rms(x, g):
    return x * lax.rsqrt(jnp.mean(x * x, axis=-1, keepdims=True) + EPS) * g


def _bdot(a, b):
    return jnp.dot(a.astype(BF16), b, preferred_element_type=F32)


def _row_spec(tm, width):
    return pl.BlockSpec((tm, width), lambda i: (i, 0))


def _slab_spec(tm):
    return pl.BlockSpec((N_SLABS, tm, LANES), lambda i: (0, i, 0))


def _const_spec(shape):
    return pl.BlockSpec(shape, lambda i: (0,) * len(shape))


def _inproj_kernel(x_ref, g_ref, w_ref, u_ref, z_ref):
    hn = _rms(x_ref[...], g_ref[...])
    proj = _bdot(hn, w_ref[...])
    for q in range(N_SLABS):
        u_ref[q] = proj[:, q * LANES:(q + 1) * LANES]
    z_ref[...] = proj[:, D_SSM:]


def _inproj(x, g, w_bf, tm):
    n = x.shape[0]
    d_in = w_bf.shape[1]
    return pl.pallas_call(
        _inproj_kernel,
        grid=(n // tm,),
        in_specs=[_row_spec(tm, D_MODEL), _const_spec((1, D_MODEL)), _const_spec((D_MODEL, d_in))],
        out_specs=[_slab_spec(tm), _row_spec(tm, d_in - D_SSM)],
        out_shape=[jax.ShapeDtypeStruct((N_SLABS, n, LANES), F32),
                   jax.ShapeDtypeStruct((n, d_in - D_SSM), F32)],
        compiler_params=_params("parallel"),
        name="inproj",
    )(x, g.reshape(1, D_MODEL), w_bf)


def _s5_prep_kernel(ar_ref, ai_ref, ldt_ref, bre_ref, bim_ref, cre_ref, cim_ref,
                    wtre_ref, wtim_ref, brre_ref, brim_ref, kcat_ref, are_ref, aim_ref):
    ar, ai = ar_ref[...], ai_ref[...]
    dt = jnp.exp(ldt_ref[...])
    n_pow = S5_CHUNK + 8
    k = lax.broadcasted_iota(jnp.int32, (n_pow, SSM_STATE), 0).astype(F32)
    mag = jnp.exp(k * (dt * ar))
    ang = k * (dt * ai)
    p_re, p_im = mag * jnp.cos(ang), mag * jnp.sin(ang)
    inv = 1.0 / (ar * ar + ai * ai)
    ab_re, ab_im = p_re[1:2], p_im[1:2]
    ia_re, ia_im = ar * inv, -ai * inv
    coef_re = (ab_re - 1.0) * ia_re - ab_im * ia_im
    coef_im = (ab_re - 1.0) * ia_im + ab_im * ia_re
    bre, bim = bre_ref[...], bim_ref[...]
    bb_re = coef_re * bre - coef_im * bim
    bb_im = coef_re * bim + coef_im * bre
    cre, cim = cre_ref[...], cim_ref[...]
    for kk in range(S5_CHUNK + 1):
        pr, pi = p_re[kk:kk + 1], p_im[kk:kk + 1]
        rows = slice(kk * SSM_GROUP, (kk + 1) * SSM_GROUP)
        wtre_ref[rows, :] = pr * cre - pi * cim
        wtim_ref[rows, :] = -pi * cre - pr * cim
        if kk < S5_CHUNK:
            brre_ref[rows, :] = pr * bb_re - pi * bb_im
            brim_ref[rows, :] = pi * bb_re + pr * bb_im
    n_tap = S5_CHUNK * SSM_GROUP
    nt = (((1,), (1,)), ((), ()))
    kcat_ref[...] = (
        lax.dot_general(bb_re, wtre_ref[0:n_tap, :], nt, precision=lax.Precision.HIGHEST,
                        preferred_element_type=F32)
        + lax.dot_general(bb_im, wtim_ref[0:n_tap, :], nt, precision=lax.Precision.HIGHEST,
                          preferred_element_type=F32))
    kc = float(S5_CHUNK) * lax.broadcasted_iota(jnp.int32, (2 * SUBLANES, SSM_STATE), 0).astype(F32)
    magc = jnp.exp(kc * (dt * ar))
    angc = kc * (dt * ai)
    are_ref[...] = magc * jnp.cos(angc)
    aim_ref[...] = magc * jnp.sin(angc)


def _s5_prep(a_re, a_im, log_dt, b_re, b_im, c_re, c_im):
    g, n, c = N_GROUPS, SSM_STATE, SSM_GROUP
    n_tap = S5_CHUNK * c

    def gspec(*shape):
        return pl.BlockSpec((None,) + shape, lambda i: (i,) + (0,) * len(shape))

    outs = pl.pallas_call(
        _s5_prep_kernel,
        grid=(g,),
        in_specs=[gspec(1, n), gspec(1, n), gspec(1, 1), gspec(c, n), gspec(c, n), gspec(c, n), gspec(c, n)],
        out_specs=[gspec(n_tap + c, n), gspec(n_tap + c, n), gspec(n_tap, n), gspec(n_tap, n),
                   gspec(c, n_tap), gspec(2 * SUBLANES, n), gspec(2 * SUBLANES, n)],
        out_shape=[jax.ShapeDtypeStruct((g, n_tap + c, n), F32), jax.ShapeDtypeStruct((g, n_tap + c, n), F32),
                   jax.ShapeDtypeStruct((g, n_tap, n), F32), jax.ShapeDtypeStruct((g, n_tap, n), F32),
                   jax.ShapeDtypeStruct((g, c, n_tap), F32),
                   jax.ShapeDtypeStruct((g, 2 * SUBLANES, n), F32),
                   jax.ShapeDtypeStruct((g, 2 * SUBLANES, n), F32)],
        compiler_params=_params("parallel"),
        name="s5_prep",
    )(a_re.reshape(g, 1, n), a_im.reshape(g, 1, n), log_dt.reshape(g, 1, 1),
      jnp.swapaxes(b_re, 1, 2), jnp.swapaxes(b_im, 1, 2), c_re, c_im)
    wt_re, wt_im, br_re, br_im, kcat, a_re16, a_im16 = outs

    L = S5_CHUNK
    kk = kcat.reshape(g, c, L, c)
    lag = jnp.arange(L)[None, :] - jnp.arange(L)[:, None]
    m = kk[:, :, jnp.clip(lag, 0, L - 1), :]
    m = jnp.where((lag >= 0)[None, None, :, :, None], m, 0.0)
    m = m.transpose(0, 2, 1, 3, 4).reshape(g, n_tap, n_tap)

    def flip_k(b):
        return b.reshape(g, L, c, n)[:, ::-1].reshape(g, n_tap, n)

    ws_re, ws_im = flip_k(br_re), flip_k(br_im)
    z = jnp.zeros((N_PAIRS, n_tap, n), F32)
    e, o = slice(0, None, 2), slice(1, None, 2)
    wsp = jnp.concatenate([
        jnp.concatenate([ws_re[e], z, ws_im[e], z], axis=2),
        jnp.concatenate([z, ws_re[o], z, ws_im[o]], axis=2)], axis=1)
    wo_re = jnp.swapaxes(wt_re[:, c:], 1, 2)
    wo_im = jnp.swapaxes(wt_im[:, c:], 1, 2)
    zo = jnp.zeros((N_PAIRS, n, n_tap), F32)
    wop = jnp.concatenate([
        jnp.concatenate([wo_re[e], zo], axis=2), jnp.concatenate([zo, wo_re[o]], axis=2),
        jnp.concatenate([wo_im[e], zo], axis=2), jnp.concatenate([zo, wo_im[o]], axis=2)], axis=1)
    a16 = jnp.concatenate([a_re16[e], a_re16[o], a_im16[e], a_im16[o]], axis=2)
    return (m.reshape(N_PAIRS, 2, n_tap, n_tap).astype(BF16), wsp.astype(BF16), wop.astype(BF16), a16)


def _block_transpose8(vs):
    lane = lax.broadcasted_iota(jnp.int32, vs[0].shape, 1)
    blk = lane >> 4
    for d in (4, 2, 1):
        keep = (blk & d) == 0
        new = list(vs)
        for i in range(GROUPS_PER_SLAB):
            if i & d == 0:
                a, b = vs[i], vs[i + d]
                new[i] = jnp.where(keep, a, pltpu.roll(b, d * SSM_GROUP, 1))
                new[i + d] = jnp.where(keep, pltpu.roll(a, LANES - d * SSM_GROUP, 1), b)
        vs = new
    return vs


def _cmul(ar, ai, xr, xi):
    return ar * xr - ai * xi, ar * xi + ai * xr


def _s5_kernel(u_ref, h0_ref, m_ref, ws_ref, wo_ref, a_ref, y_ref, hf_ref, x_scr, yg_scr, s_scr, hp_scr,
               *, rows, independent):
    half = 2 * SSM_STATE
    rt = min(rows, 4 * SUBLANES)
    half_chunk = S5_CHUNK // 2

    def gather_tile(t, carry):
        r0 = pl.multiple_of(t * rt, rt)
        for hf in range(2):
            vs = [u_ref[pl.ds(r0 * S5_CHUNK + hf * half_chunk + i, rt, stride=S5_CHUNK), :]
                  for i in range(half_chunk)]
            outs = _block_transpose8(vs)
            for g in range(GROUPS_PER_SLAB):
                x_scr[g, pl.ds(r0, rt), hf * LANES:(hf + 1) * LANES] = outs[g]
        return carry

    lax.fori_loop(0, rows // rt, gather_tile, 0)

    row = lax.broadcasted_iota(jnp.int32, (SUBLANES, half), 0)
    n_tap = S5_CHUNK * SSM_GROUP
    for pi in range(PAIRS_PER_SLAB):
        x0 = x_scr[2 * pi].astype(BF16)
        x1 = x_scr[2 * pi + 1].astype(BF16)
        s_scr[...] = jnp.dot(jnp.concatenate([x0, x1], axis=1), ws_ref[pi], preferred_element_type=F32)
        ap = a_ref[pi]
        h0 = h0_ref[pi]
        if independent:
            hp_scr[...] = h0
            s = s_scr[...]
            n_re, n_im = _cmul(ap[1:2, :half], ap[1:2, half:], h0[:, :half], h0[:, half:])
            hf_ref[pi] = jnp.concatenate([n_re + s[:, :half], n_im + s[:, half:]], axis=1)
        else:
            pw_re, pw_im = ap[0:SUBLANES, :half], ap[0:SUBLANES, half:]

            def scan_tile(t, carry):
                h_re, h_im = carry
                r0 = pl.multiple_of(t * SUBLANES, SUBLANES)
                s = s_scr[pl.ds(r0, SUBLANES), :]
                t_re, t_im = s[:, :half], s[:, half:]
                for d in (1, 2, 4):
                    sh_re = jnp.where(row >= d, pltpu.roll(t_re, d, 0), 0.0)
                    sh_im = jnp.where(row >= d, pltpu.roll(t_im, d, 0), 0.0)
                    m_re, m_im = _cmul(ap[d:d + 1, :half], ap[d:d + 1, half:], sh_re, sh_im)
                    t_re, t_im = t_re + m_re, t_im + m_im
                e_re = jnp.where(row >= 1, pltpu.roll(t_re, 1, 0), 0.0)
                e_im = jnp.where(row >= 1, pltpu.roll(t_im, 1, 0), 0.0)
                c_re, c_im = _cmul(pw_re, pw_im, h_re, h_im)
                hp_scr[pl.ds(r0, SUBLANES), :] = jnp.concatenate([e_re + c_re, e_im + c_im], axis=1)
                o_re, o_im = _cmul(ap[SUBLANES:SUBLANES + 1, :half], ap[SUBLANES:SUBLANES + 1, half:],
                                   h_re, h_im)
                last = SUBLANES - 1
                n_re = jnp.broadcast_to(t_re[last:last + 1], h_re.shape) + o_re
                n_im = jnp.broadcast_to(t_im[last:last + 1], h_im.shape) + o_im
                return n_re, n_im

            init = (jnp.broadcast_to(h0[:, :half], (SUBLANES, half)),
                    jnp.broadcast_to(h0[:, half:], (SUBLANES, half)))
            h_re, h_im = lax.fori_loop(0, rows // SUBLANES, scan_tile, init)
            hf_ref[pi] = jnp.concatenate([h_re[0:1], h_im[0:1]], axis=1)
        yc = _bdot(hp_scr[...], wo_ref[pi])
        yg_scr[2 * pi] = jnp.dot(x0, m_ref[pi, 0], preferred_element_type=F32) + yc[:, :n_tap]
        yg_scr[2 * pi + 1] = jnp.dot(x1, m_ref[pi, 1], preferred_element_type=F32) + yc[:, n_tap:]

    def scatter_tile(t, carry):
        r0 = pl.multiple_of(t * rt, rt)
        for hf in range(2):
            vs = [yg_scr[g, pl.ds(r0, rt), hf * LANES:(hf + 1) * LANES] for g in range(GROUPS_PER_SLAB)]
            outs = _block_transpose8(vs)
            for i in range(half_chunk):
                y_ref[pl.ds(r0 * S5_CHUNK + hf * half_chunk + i, rt, stride=S5_CHUNK), :] = outs[i]
        return carry

    lax.fori_loop(0, rows // rt, scatter_tile, 0)


def _s5_mixer(u, h0_re, h0_im, prep, bsz, t_len):
    m, wsp, wop, a16 = prep
    n_tap = S5_CHUNK * SSM_GROUP
    st = 4 * SSM_STATE
    independent = t_len == S5_CHUNK
    if independent:
        nblk, rows, hrows = 1, bsz, bsz
    else:
        nblk, rows, hrows = bsz, t_len // S5_CHUNK, 1
    assert t_len % S5_CHUNK == 0 and rows % SUBLANES == 0, (bsz, t_len)
    h0p = jnp.concatenate([h0_re.reshape(bsz, N_PAIRS, 2 * SSM_STATE),
                           h0_im.reshape(bsz, N_PAIRS, 2 * SSM_STATE)], axis=2).astype(F32)
    h0p = h0p.transpose(1, 0, 2)[None] if independent else h0p[:, :, None, :]
    pp = PAIRS_PER_SLAB

    def wspec(*shape):
        return pl.BlockSpec((pp,) + shape, lambda q, b: (q,) + (0,) * len(shape))

    frames = rows * S5_CHUNK
    y, hf = pl.pallas_call(
        functools.partial(_s5_kernel, rows=rows, independent=independent),
        grid=(N_SLABS, nblk),
        in_specs=[pl.BlockSpec((None, frames, LANES), lambda q, b: (q, b, 0)),
                  pl.BlockSpec((None, pp, hrows, st), lambda q, b: (b, q, 0, 0)),
                  wspec(2, n_tap, n_tap), wspec(2 * n_tap, st), wspec(st, 2 * n_tap), wspec(2 * SUBLANES, st)],
        out_specs=[pl.BlockSpec((None, frames, LANES), lambda q, b: (q, b, 0)),
                   pl.BlockSpec((None, pp, hrows, st), lambda q, b: (b, q, 0, 0))],
        out_shape=[jax.ShapeDtypeStruct(u.shape, F32),
                   jax.ShapeDtypeStruct((nblk, N_PAIRS, hrows, st), F32)],
        scratch_shapes=[pltpu.VMEM((GROUPS_PER_SLAB, rows, n_tap), F32),
                        pltpu.VMEM((GROUPS_PER_SLAB, rows, n_tap), F32),
                        pltpu.VMEM((rows, st), F32), pltpu.VMEM((rows, st), F32)],
        compiler_params=_params("parallel", "parallel"),
        name="s5_core",
    )(u, h0p, m, wsp, wop, a16)
    hf = hf[0].transpose(1, 0, 2) if independent else hf[:, :, 0, :]
    hf_re = hf[:, :, :2 * SSM_STATE].reshape(bsz, N_GROUPS, SSM_STATE)
    hf_im = hf[:, :, 2 * SSM_STATE:].reshape(bsz, N_GROUPS, SSM_STATE)
    return y, hf_re, hf_im


def _conv_kernel(z_ref, left_ref, w_ref, b_ref, g_ref, beta_ref, y_ref, cn_ref, vbuf, *, tm):
    @pl.when(pl.program_id(1) == 0)
    def _():
        vbuf[0:CONV_HALO, :] = left_ref[...]

    z = z_ref[...]
    vbuf[CONV_HALO:CONV_HALO + tm, :] = z[:, :D_CONV] * jax.nn.sigmoid(z[:, D_CONV:])
    first = CONV_HALO - (CONV_WIDTH - 1)
    acc = jnp.zeros((tm, D_CONV), F32)
    for k in range(CONV_WIDTH):
        acc = acc + w_ref[k:k + 1, :] * vbuf[first + k:first + k + tm, :]
    y = acc + b_ref[...]
    mu = jnp.mean(y, axis=-1, keepdims=True)
    yc = y - mu
    var = jnp.mean(yc * yc, axis=-1, keepdims=True)
    yn = yc * lax.rsqrt(var + EPS) * g_ref[...] + beta_ref[...]
    y_ref[...] = yn * jax.nn.sigmoid(yn)
    cn_ref[...] = vbuf[tm + first:tm + CONV_HALO, :]
    vbuf[0:CONV_HALO, :] = vbuf[tm:tm + CONV_HALO, :]


def _conv_mixer(z, left, w, b, g, beta, bsz, t_len, tm):
    left = jnp.pad(left.astype(F32), ((0, 0), (CONV_HALO - (CONV_WIDTH - 1), 0), (0, 0)))
    vec = pl.BlockSpec((1, D_CONV), lambda bi, j: (0, 0))
    return pl.pallas_call(
        functools.partial(_conv_kernel, tm=tm),
        grid=(bsz, t_len // tm),
        in_specs=[pl.BlockSpec((None, tm, 2 * D_CONV), lambda bi, j: (bi, j, 0)),
                  pl.BlockSpec((None, CONV_HALO, D_CONV), lambda bi, j: (bi, 0, 0)),
                  pl.BlockSpec((CONV_WIDTH, D_CONV), lambda bi, j: (0, 0)), vec, vec, vec],
        out_specs=[pl.BlockSpec((None, tm, D_CONV), lambda bi, j: (bi, j, 0)),
                   pl.BlockSpec((None, CONV_WIDTH - 1, D_CONV), lambda bi, j: (bi, 0, 0))],
        out_shape=[jax.ShapeDtypeStruct((bsz, t_len, D_CONV), F32),
                   jax.ShapeDtypeStruct((bsz, CONV_WIDTH - 1, D_CONV), F32)],
        scratch_shapes=[pltpu.VMEM((tm + CONV_HALO, D_CONV), F32)],
        compiler_params=_params("parallel", "arbitrary"),
        name="conv_mixer",
    )(z, left, w, b.reshape(1, D_CONV), g.reshape(1, D_CONV), beta.reshape(1, D_CONV))


def _route(logits):
    lane_i = lax.broadcasted_iota(jnp.int32, logits.shape, 1)
    lane = lane_i.astype(F32)
    group_of_lane = (lane_i >> 3).astype(F32)
    neg = -jnp.inf
    far = float(LANES)
    is_g = (lane_i >= N_EXPERTS) & (lane_i < N_EXPERTS + N_EXPERT_GROUPS)
    gl = jnp.where(is_g, logits, neg)
    g_max = jnp.max(gl, axis=-1, keepdims=True)
    g_lane = jnp.min(jnp.where(gl == g_max, lane, far), axis=-1, keepdims=True)
    g_gate = 1.0 / jnp.sum(jnp.exp(gl - g_max), axis=-1, keepdims=True)
    g_idx = g_lane - float(N_EXPERTS)
    in_group = (lane_i < N_EXPERTS) & (group_of_lane == g_idx)
    el = jnp.where(in_group, logits, neg)
    v1 = jnp.max(el, axis=-1, keepdims=True)
    i1 = jnp.min(jnp.where(el == v1, lane, far), axis=-1, keepdims=True)
    el2 = jnp.where(lane == i1, neg, el)
    v2 = jnp.max(el2, axis=-1, keepdims=True)
    i2 = jnp.min(jnp.where(el2 == v2, lane, far), axis=-1, keepdims=True)
    e2 = jnp.exp(v2 - v1)
    w1 = g_gate / (1.0 + e2)
    w2 = g_gate * e2 / (1.0 + e2)
    return jnp.where(lane == i1, w1, 0.0) + jnp.where(lane == i2, w2, 0.0)


def _outproj_kernel(x_ref, ys_ref, u_ref, yb_ref, d_ref, wglu_ref, wout_ref, nf_ref, rw_ref, rb_ref,
                    x1_ref, hn_ref, comb_ref):
    ys = jnp.concatenate([ys_ref[q] for q in range(N_SLABS)], axis=1)
    u = jnp.concatenate([u_ref[q] for q in range(N_SLABS)], axis=1)
    z = jax.nn.gelu(ys + d_ref[...] * u)
    ya = z * jax.nn.sigmoid(_bdot(z, wglu_ref[...]))
    mix = _bdot(ya, wout_ref[0:D_SSM, :]) + _bdot(yb_ref[...], wout_ref[D_SSM:, :])
    x1 = x_ref[...] + mix
    x1_ref[...] = x1
    hn = _rms(x1, nf_ref[...])
    hn_ref[...] = hn.astype(BF16)
    logits = jnp.dot(hn, rw_ref[...], precision=lax.Precision.HIGHEST,
                     preferred_element_type=F32) + rb_ref[...]
    comb_ref[...] = _route(logits)


def _outproj(x, ys, u, yb, d, wglu_bf, wout_bf, nf, rw, rb, tm):
    n = x.shape[0]
    return pl.pallas_call(
        _outproj_kernel,
        grid=(n // tm,),
        in_specs=[_row_spec(tm, D_MODEL), _slab_spec(tm), _slab_spec(tm), _row_spec(tm, D_CONV),
                  _const_spec((1, D_SSM)), _const_spec((D_SSM, D_SSM)), _const_spec((D_MODEL, D_MODEL)),
                  _const_spec((1, D_MODEL)), _const_spec((D_MODEL, LANES)), _const_spec((1, LANES))],
        out_specs=[_row_spec(tm, D_MODEL), _row_spec(tm, D_MODEL), _row_spec(tm, LANES)],
        out_shape=[jax.ShapeDtypeStruct((n, D_MODEL), F32), jax.ShapeDtypeStruct((n, D_MODEL), BF16),
                   jax.ShapeDtypeStruct((n, LANES), F32)],
        compiler_params=_params("parallel"),
        name="outproj_router",
    )(x, ys, u, yb, d.reshape(1, D_SSM), wglu_bf, wout_bf, nf.reshape(1, D_MODEL), rw, rb)


def _moe_kernel(x1_ref, hn_ref, comb_ref, wg_ref, wu_ref, wd_ref, o_ref, acc_ref):
    e = pl.program_id(1)

    @pl.when(e == 0)
    def _():
        acc_ref[...] = jnp.zeros_like(acc_ref)

    h = hn_ref[...]
    hg = jnp.dot(h, wg_ref[...], preferred_element_type=F32)
    hu = jnp.dot(h, wu_ref[...], preferred_element_type=F32)
    comb = comb_ref[...]
    lane = lax.broadcasted_iota(jnp.int32, comb.shape, 1)
    c = jnp.sum(jnp.where(lane == e, comb, 0.0), axis=-1, keepdims=True)
    act = hg * jax.nn.sigmoid(hg) * hu * c
    acc_ref[...] += _bdot(act, wd_ref[...])

    @pl.when(e == N_EXPERTS - 1)
    def _():
        o_ref[...] = x1_ref[...] + acc_ref[...]


def _moe(x1, hn, comb, wg_bf, wu_bf, wd_bf, tm):
    n = x1.shape[0]
    row = lambda w: pl.BlockSpec((tm, w), lambda i, e: (i, 0))
    return pl.pallas_call(
        _moe_kernel,
        grid=(n // tm, N_EXPERTS),
        in_specs=[row(D_MODEL), row(D_MODEL), row(LANES),
                  pl.BlockSpec((None, D_MODEL, D_EXPERT), lambda i, e: (e, 0, 0)),
                  pl.BlockSpec((None, D_MODEL, D_EXPERT), lambda i, e: (e, 0, 0)),
                  pl.BlockSpec((None, D_EXPERT, D_MODEL), lambda i, e: (e, 0, 0))],
        out_specs=row(D_MODEL),
        out_shape=jax.ShapeDtypeStruct((n, D_MODEL), F32),
        scratch_shapes=[pltpu.VMEM((tm, D_MODEL), F32)],
        compiler_params=_params("parallel", "arbitrary"),
        name="moe",
    )(x1, hn, comb, wg_bf, wu_bf, wd_bf)


def _ple_kernel(x_ref, p_ref, np_ref, wple_ref, wgate_ref, nfin_ref, o_ref, *, final):
    x = x_ref[...]
    gate = jax.nn.sigmoid(_bdot(_rms(x, np_ref[...]), wgate_ref[...]))
    out = x + _bdot(p_ref[...], wple_ref[...]) * gate
    if final:
        out = _rms(out, nfin_ref[...])
    o_ref[...] = out


def _ple(x, p, npl, wple_bf, wgate_bf, nfin, tm, final):
    n = x.shape[0]
    return pl.pallas_call(
        functools.partial(_ple_kernel, final=final),
        grid=(n // tm,),
        in_specs=[_row_spec(tm, D_MODEL), _row_spec(tm, D_PLE), _const_spec((1, D_MODEL)),
                  _const_spec((D_PLE, D_MODEL)), _const_spec((D_MODEL, D_MODEL)), _const_spec((1, D_MODEL))],
        out_specs=_row_spec(tm, D_MODEL),
        out_shape=jax.ShapeDtypeStruct((n, D_MODEL), F32),
        compiler_params=_params("parallel"),
        name="ple",
    )(x, p, npl.reshape(1, D_MODEL), wple_bf, wgate_bf, nfin.reshape(1, D_MODEL))


def _layer(x, p, h0_re, h0_im, conv_left, w, bsz, t_len, tm, tm_conv, tm_moe, final):
    n = bsz * t_len
    u, z = _inproj(x, w["norm_mix"], w["w_in"], tm)
    ys, hf_re, hf_im = _s5_mixer(u, h0_re, h0_im, w["s5"], bsz, t_len)
    yb, conv_new = _conv_mixer(z.reshape(bsz, t_len, 2 * D_CONV), conv_left, w["conv_w"], w["conv_b"],
                               w["conv_ln_g"], w["conv_ln_b"], bsz, t_len, tm_conv)
    x1, hn, comb = _outproj(x, ys, u, yb.reshape(n, D_CONV), w["ssm_d"], w["w_ssm_glu"], w["w_out"],
                            w["norm_ffn"], w["router_w"], w["router_b"], tm)
    x2 = _moe(x1, hn, comb, w["expert_w_gate"], w["expert_w_up"], w["expert_w_down"], tm_moe)
    x3 = _ple(x2, p, w["norm_ple"], w["ple_w"], w["ple_gate_w"], w["norm_final"], tm, final)
    return x3, hf_re, hf_im, conv_new


def kernel(x_prompt, x_sample, p_prompt, p_sample, state_ssm_re, state_ssm_im, cache_conv, norm_mix, w_in, ssm_a_re, ssm_a_im, ssm_b_re, ssm_b_im, ssm_c_re, ssm_c_im, ssm_d, ssm_log_dt, w_ssm_glu, conv_w, conv_b, conv_ln_g, conv_ln_b, w_out, norm_ffn, router_group_w, router_group_b, router_expert_w, router_expert_b, expert_w_gate, expert_w_up, expert_w_down, norm_ple, ple_w, ple_gate_w, norm_final):
    depth = w_in.shape[0]
    bp, tp, _ = x_prompt.shape
    bs, ts, _ = x_sample.shape
    xp = x_prompt.reshape(bp * tp, D_MODEL)
    xs = x_sample.reshape(bs * ts, D_MODEL)
    zero_state = jnp.zeros((bp, N_GROUPS, SSM_STATE), F32)
    zero_conv = jnp.zeros((bp, CONV_WIDTH - 1, D_CONV), F32)
    pad_lanes = LANES - N_EXPERTS - N_EXPERT_GROUPS
    outs = {k: [] for k in ("pr_re", "pr_im", "pr_conv", "sm_re", "sm_im", "sm_conv")}
    for i in range(depth):
        w = {
            "norm_mix": norm_mix[i], "w_in": w_in[i].astype(BF16),
            "s5": _s5_prep(ssm_a_re[i], ssm_a_im[i], ssm_log_dt[i], ssm_b_re[i], ssm_b_im[i],
                           ssm_c_re[i], ssm_c_im[i]),
            "ssm_d": ssm_d[i], "w_ssm_glu": w_ssm_glu[i].astype(BF16),
            "conv_w": conv_w[i], "conv_b": conv_b[i], "conv_ln_g": conv_ln_g[i], "conv_ln_b": conv_ln_b[i],
            "w_out": w_out[i].astype(BF16), "norm_ffn": norm_ffn[i],
            "router_w": jnp.pad(jnp.concatenate([router_expert_w[i], router_group_w[i]], axis=1),
                                ((0, 0), (0, pad_lanes))),
            "router_b": jnp.pad(jnp.concatenate([router_expert_b[i], router_group_b[i]]),
                                (0, pad_lanes)).reshape(1, LANES),
            "expert_w_gate": expert_w_gate[i].astype(BF16), "expert_w_up": expert_w_up[i].astype(BF16),
            "expert_w_down": expert_w_down[i].astype(BF16),
            "norm_ple": norm_ple[i], "ple_w": ple_w[i].astype(BF16), "ple_gate_w": ple_gate_w[i].astype(BF16),
            "norm_final": norm_final,
        }
        final = i == depth - 1
        xp, hr, hi, cv = _layer(xp, p_prompt[i].reshape(bp * tp, D_PLE), zero_state, zero_state, zero_conv, w,
                                bp, tp, tm=512, tm_conv=512, tm_moe=1024, final=final)
        outs["pr_re"].append(hr); outs["pr_im"].append(hi); outs["pr_conv"].append(cv)
        xs, hr, hi, cv = _layer(xs, p_sample[i].reshape(bs * ts, D_PLE), state_ssm_re[i], state_ssm_im[i],
                                cache_conv[i], w, bs, ts, tm=bs * ts, tm_conv=ts, tm_moe=bs * ts, final=final)
        outs["sm_re"].append(hr); outs["sm_im"].append(hi); outs["sm_conv"].append(cv)
    return (xp.reshape(bp, tp, D_MODEL), xs.reshape(bs, ts, D_MODEL),
            jnp.stack(outs["pr_re"]), jnp.stack(outs["pr_im"]), jnp.stack(outs["pr_conv"]),
            jnp.stack(outs["sm_re"]), jnp.stack(outs["sm_im"]), jnp.stack(outs["sm_conv"]))
```

```python
import functools

import jax
import jax.numpy as jnp
from jax import lax
from jax.experimental import pallas as pl
from jax.experimental.pallas import tpu as pltpu

F32 = jnp.float32
BF16 = jnp.bfloat16

D_MODEL = 1024
D_SSM = 512
SSM_GROUP = 16
N_GROUPS = D_SSM // SSM_GROUP
N_PAIRS = N_GROUPS // 2
SSM_STATE = 64
D_CONV = 512
CONV_WIDTH = 31
CONV_HALO = 32
N_EXPERT_GROUPS = 4
EXPERTS_PER_GROUP = 8
N_EXPERTS = 32
D_EXPERT = 256
D_PLE = 256
EPS = 1e-6
S5_CHUNK = 16
LANES = 128
SUBLANES = 8
N_SLABS = D_SSM // LANES
GROUPS_PER_SLAB = LANES // SSM_GROUP
PAIRS_PER_SLAB = GROUPS_PER_SLAB // 2
VMEM_LIMIT = 56 * 1024 * 1024


def _params(*sem):
    return pltpu.CompilerParams(dimension_semantics=sem, vmem_limit_bytes=VMEM_LIMIT)


def _rms(x, g):
    return x * lax.rsqrt(jnp.mean(x * x, axis=-1, keepdims=True) + EPS) * g


def _bdot(a, b):
    return jnp.dot(a.astype(BF16), b, preferred_element_type=F32)


def _row_spec(tm, width):
    return pl.BlockSpec((tm, width), lambda i: (i, 0))


def _slab_spec(tm):
    return pl.BlockSpec((N_SLABS, tm, LANES), lambda i: (0, i, 0))


def _const_spec(shape):
    return pl.BlockSpec(shape, lambda i: (0,) * len(shape))


def _inproj_kernel(x_ref, g_ref, w_ref, u_ref, z_ref):
    hn = _rms(x_ref[...], g_ref[...])
    proj = _bdot(hn, w_ref[...])
    for q in range(N_SLABS):
        u_ref[q] = proj[:, q * LANES:(q + 1) * LANES]
    z_ref[...] = proj[:, D_SSM:]


def _inproj(x, g, w_bf, tm):
    n = x.shape[0]
    d_in = w_bf.shape[1]
    return pl.pallas_call(
        _inproj_kernel,
        grid=(n // tm,),
        in_specs=[_row_spec(tm, D_MODEL), _const_spec((1, D_MODEL)), _const_spec((D_MODEL, d_in))],
        out_specs=[_slab_spec(tm), _row_spec(tm, d_in - D_SSM)],
        out_shape=[jax.ShapeDtypeStruct((N_SLABS, n, LANES), F32),
                   jax.ShapeDtypeStruct((n, d_in - D_SSM), F32)],
        compiler_params=_params("parallel"),
        name="inproj",
    )(x, g.reshape(1, D_MODEL), w_bf)


def _s5_prep_kernel(ar_ref, ai_ref, ldt_ref, bre_ref, bim_ref, cre_ref, cim_ref,
                    wtre_ref, wtim_ref, brre_ref, brim_ref, kcat_ref, are_ref, aim_ref):
    ar, ai = ar_ref[...], ai_ref[...]
    dt = jnp.exp(ldt_ref[...])
    n_pow = S5_CHUNK + 8
    k = lax.broadcasted_iota(jnp.int32, (n_pow, SSM_STATE), 0).astype(F32)
    mag = jnp.exp(k * (dt * ar))
    ang = k * (dt * ai)
    p_re, p_im = mag * jnp.cos(ang), mag * jnp.sin(ang)
    inv = 1.0 / (ar * ar + ai * ai)
    ab_re, ab_im = p_re[1:2], p_im[1:2]
    ia_re, ia_im = ar * inv, -ai * inv
    coef_re = (ab_re - 1.0) * ia_re - ab_im * ia_im
    coef_im = (ab_re - 1.0) * ia_im + ab_im * ia_re
    bre, bim = bre_ref[...], bim_ref[...]
    bb_re = coef_re * bre - coef_im * bim
    bb_im = coef_re * bim + coef_im * bre
    cre, cim = cre_ref[...], cim_ref[...]
    for kk in range(S5_CHUNK + 1):
        pr, pi = p_re[kk:kk + 1], p_im[kk:kk + 1]
        rows = slice(kk * SSM_GROUP, (kk + 1) * SSM_GROUP)
        wtre_ref[rows, :] = pr * cre - pi * cim
        wtim_ref[rows, :] = -pi * cre - pr * cim
        if kk < S5_CHUNK:
            brre_ref[rows, :] = pr * bb_re - pi * bb_im
            brim_ref[rows, :] = pi * bb_re + pr * bb_im
    n_tap = S5_CHUNK * SSM_GROUP
    nt = (((1,), (1,)), ((), ()))
    kcat_ref[...] = (
        lax.dot_general(bb_re, wtre_ref[0:n_tap, :], nt, precision=lax.Precision.HIGHEST,
                        preferred_element_type=F32)
        + lax.dot_general(bb_im, wtim_ref[0:n_tap, :], nt, precision=lax.Precision.HIGHEST,
                          preferred_element_type=F32))
    kc = float(S5_CHUNK) * lax.broadcasted_iota(jnp.int32, (2 * SUBLANES, SSM_STATE), 0).astype(F32)
    magc = jnp.exp(kc * (dt * ar))
    angc = kc * (dt * ai)
    are_ref[...] = magc * jnp.cos(angc)
    aim_ref[...] = magc * jnp.sin(angc)


def _s5_prep(a_re, a_im, log_dt, b_re, b_im, c_re, c_im):
    g, n, c = N_GROUPS, SSM_STATE, SSM_GROUP
    n_tap = S5_CHUNK * c

    def gspec(*shape):
        return pl.BlockSpec((None,) + shape, lambda i: (i,) + (0,) * len(shape))

    outs = pl.pallas_call(
        _s5_prep_kernel,
        grid=(g,),
        in_specs=[gspec(1, n), gspec(1, n), gspec(1, 1), gspec(c, n), gspec(c, n), gspec(c, n), gspec(c, n)],
        out_specs=[gspec(n_tap + c, n), gspec(n_tap + c, n), gspec(n_tap, n), gspec(n_tap, n),
                   gspec(c, n_tap), gspec(2 * SUBLANES, n), gspec(2 * SUBLANES, n)],
        out_shape=[jax.ShapeDtypeStruct((g, n_tap + c, n), F32), jax.ShapeDtypeStruct((g, n_tap + c, n), F32),
                   jax.ShapeDtypeStruct((g, n_tap, n), F32), jax.ShapeDtypeStruct((g, n_tap, n), F32),
                   jax.ShapeDtypeStruct((g, c, n_tap), F32),
                   jax.ShapeDtypeStruct((g, 2 * SUBLANES, n), F32),
                   jax.ShapeDtypeStruct((g, 2 * SUBLANES, n), F32)],
        compiler_params=_params("parallel"),
        name="s5_prep",
    )(a_re.reshape(g, 1, n), a_im.reshape(g, 1, n), log_dt.reshape(g, 1, 1),
      jnp.swapaxes(b_re, 1, 2), jnp.swapaxes(b_im, 1, 2), c_re, c_im)
    wt_re, wt_im, br_re, br_im, kcat, a_re16, a_im16 = outs

    L = S5_CHUNK
    kk = kcat.reshape(g, c, L, c)
    lag = jnp.arange(L)[None, :] - jnp.arange(L)[:, None]
    m = kk[:, :, jnp.clip(lag, 0, L - 1), :]
    m = jnp.where((lag >= 0)[None, None, :, :, None], m, 0.0)
    m = m.transpose(0, 2, 1, 3, 4).reshape(g, n_tap, n_tap)

    def flip_k(b):
        return b.reshape(g, L, c, n)[:, ::-1].reshape(g, n_tap, n)

    ws_re, ws_im = flip_k(br_re), flip_k(br_im)
    z = jnp.zeros((N_PAIRS, n_tap, n), F32)
    e, o = slice(0, None, 2), slice(1, None, 2)
    wsp = jnp.concatenate([
        jnp.concatenate([ws_re[e], z, ws_im[e], z], axis=2),
        jnp.concatenate([z, ws_re[o], z, ws_im[o]], axis=2)], axis=1)
    wo_re = jnp.swapaxes(wt_re[:, c:], 1, 2)
    wo_im = jnp.swapaxes(wt_im[:, c:], 1, 2)
    zo = jnp.zeros((N_PAIRS, n, n_tap), F32)
    wop = jnp.concatenate([
        jnp.concatenate([wo_re[e], zo], axis=2), jnp.concatenate([zo, wo_re[o]], axis=2),
        jnp.concatenate([wo_im[e], zo], axis=2), jnp.concatenate([zo, wo_im[o]], axis=2)], axis=1)
    a16 = jnp.concatenate([a_re16[e], a_re16[o], a_im16[e], a_im16[o]], axis=2)
    return (m.reshape(N_PAIRS, 2, n_tap, n_tap).astype(BF16), wsp.astype(BF16), wop.astype(BF16), a16)


def _block_transpose8(vs):
    lane = lax.broadcasted_iota(jnp.int32, vs[0].shape, 1)
    blk = lane >> 4
    for d in (4, 2, 1):
        keep = (blk & d) == 0
        new = list(vs)
        for i in range(GROUPS_PER_SLAB):
            if i & d == 0:
                a, b = vs[i], vs[i + d]
                new[i] = jnp.where(keep, a, pltpu.roll(b, d * SSM_GROUP, 1))
                new[i + d] = jnp.where(keep, pltpu.roll(a, LANES - d * SSM_GROUP, 1), b)
        vs = new
    return vs


def _cmul(ar, ai, xr, xi):
    return ar * xr - ai * xi, ar * xi + ai * xr


def _s5_kernel(u_ref, h0_ref, m_ref, ws_ref, wo_ref, a_ref, y_ref, hf_ref, x_scr, yg_scr, s_scr, hp_scr,
               *, rows, independent):
    half = 2 * SSM_STATE
    rt = min(rows, 4 * SUBLANES)
    half_chunk = S5_CHUNK // 2

    def gather_tile(t, carry):
        r0 = pl.multiple_of(t * rt, rt)
        for hf in range(2):
            vs = [u_ref[pl.ds(r0 * S5_CHUNK + hf * half_chunk + i, rt, stride=S5_CHUNK), :]
                  for i in range(half_chunk)]
            outs = _block_transpose8(vs)
            for g in range(GROUPS_PER_SLAB):
                x_scr[g, pl.ds(r0, rt), hf * LANES:(hf + 1) * LANES] = outs[g]
        return carry

    lax.fori_loop(0, rows // rt, gather_tile, 0)

    row = lax.broadcasted_iota(jnp.int32, (SUBLANES, half), 0)
    n_tap = S5_CHUNK * SSM_GROUP
    for pi in range(PAIRS_PER_SLAB):
        x0 = x_scr[2 * pi].astype(BF16)
        x1 = x_scr[2 * pi + 1].astype(BF16)
        s_scr[...] = jnp.dot(jnp.concatenate([x0, x1], axis=1), ws_ref[pi], preferred_element_type=F32)
        ap = a_ref[pi]
        h0 = h0_ref[pi]
        if independent:
            hp_scr[...] = h0
            s = s_scr[...]
            n_re, n_im = _cmul(ap[1:2, :half], ap[1:2, half:], h0[:, :half], h0[:, half:])
            hf_ref[pi] = jnp.concatenate([n_re + s[:, :half], n_im + s[:, half:]], axis=1)
        else:
            pw_re, pw_im = ap[0:SUBLANES, :half], ap[0:SUBLANES, half:]

            def scan_tile(t, carry):
                h_re, h_im = carry
                r0 = pl.multiple_of(t * SUBLANES, SUBLANES)
                s = s_scr[pl.ds(r0, SUBLANES), :]
                t_re, t_im = s[:, :half], s[:, half:]
                for d in (1, 2, 4):
                    sh_re = jnp.where(row >= d, pltpu.roll(t_re, d, 0), 0.0)
                    sh_im = jnp.where(row >= d, pltpu.roll(t_im, d, 0), 0.0)
                    m_re, m_im = _cmul(ap[d:d + 1, :half], ap[d:d + 1, half:], sh_re, sh_im)
                    t_re, t_im = t_re + m_re, t_im + m_im
                e_re = jnp.where(row >= 1, pltpu.roll(t_re, 1, 0), 0.0)
                e_im = jnp.where(row >= 1, pltpu.roll(t_im, 1, 0), 0.0)
                c_re, c_im = _cmul(pw_re, pw_im, h_re, h_im)
                hp_scr[pl.ds(r0, SUBLANES), :] = jnp.concatenate([e_re + c_re, e_im + c_im], axis=1)
                o_re, o_im = _cmul(ap[SUBLANES:SUBLANES + 1, :half], ap[SUBLANES:SUBLANES + 1, half:],
                                   h_re, h_im)
                last = SUBLANES - 1
                n_re = jnp.broadcast_to(t_re[last:last + 1], h_re.shape) + o_re
                n_im = jnp.broadcast_to(t_im[last:last + 1], h_im.shape) + o_im
                return n_re, n_im

            init = (jnp.broadcast_to(h0[:, :half], (SUBLANES, half)),
                    jnp.broadcast_to(h0[:, half:], (SUBLANES, half)))
            h_re, h_im = lax.fori_loop(0, rows // SUBLANES, scan_tile, init)
            hf_ref[pi] = jnp.concatenate([h_re[0:1], h_im[0:1]], axis=1)
        yc = _bdot(hp_scr[...], wo_ref[pi])
        yg_scr[2 * pi] = jnp.dot(x0, m_ref[pi, 0], preferred_element_type=F32) + yc[:, :n_tap]
        yg_scr[2 * pi + 1] = jnp.dot(x1, m_ref[pi, 1], preferred_element_type=F32) + yc[:, n_tap:]

    def scatter_tile(t, carry):
        r0 = pl.multiple_of(t * rt, rt)
        for hf in range(2):
            vs = [yg_scr[g, pl.ds(r0, rt), hf * LANES:(hf + 1) * LANES] for g in range(GROUPS_PER_SLAB)]
            outs = _block_transpose8(vs)
            for i in range(half_chunk):
                y_ref[pl.ds(r0 * S5_CHUNK + hf * half_chunk + i, rt, stride=S5_CHUNK), :] = outs[i]
        return carry

    lax.fori_loop(0, rows // rt, scatter_tile, 0)


def _s5_mixer(u, h0_re, h0_im, prep, bsz, t_len):
    m, wsp, wop, a16 = prep
    n_tap = S5_CHUNK * SSM_GROUP
    st = 4 * SSM_STATE
    independent = t_len == S5_CHUNK
    if independent:
        nblk, rows, hrows = 1, bsz, bsz
    else:
        nblk, rows, hrows = bsz, t_len // S5_CHUNK, 1
    assert t_len % S5_CHUNK == 0 and rows % SUBLANES == 0, (bsz, t_len)
    h0p = jnp.concatenate([h0_re.reshape(bsz, N_PAIRS, 2 * SSM_STATE),
                           h0_im.reshape(bsz, N_PAIRS, 2 * SSM_STATE)], axis=2).astype(F32)
    h0p = h0p.transpose(1, 0, 2)[None] if independent else h0p[:, :, None, :]
    pp = PAIRS_PER_SLAB

    def wspec(*shape):
        return pl.BlockSpec((pp,) + shape, lambda q, b: (q,) + (0,) * len(shape))

    frames = rows * S5_CHUNK
    y, hf = pl.pallas_call(
        functools.partial(_s5_kernel, rows=rows, independent=independent),
        grid=(N_SLABS, nblk),
        in_specs=[pl.BlockSpec((None, frames, LANES), lambda q, b: (q, b, 0)),
                  pl.BlockSpec((None, pp, hrows, st), lambda q, b: (b, q, 0, 0)),
                  wspec(2, n_tap, n_tap), wspec(2 * n_tap, st), wspec(st, 2 * n_tap), wspec(2 * SUBLANES, st)],
        out_specs=[pl.BlockSpec((None, frames, LANES), lambda q, b: (q, b, 0)),
                   pl.BlockSpec((None, pp, hrows, st), lambda q, b: (b, q, 0, 0))],
        out_shape=[jax.ShapeDtypeStruct(u.shape, F32),
                   jax.ShapeDtypeStruct((nblk, N_PAIRS, hrows, st), F32)],
        scratch_shapes=[pltpu.VMEM((GROUPS_PER_SLAB, rows, n_tap), F32),
                        pltpu.VMEM((GROUPS_PER_SLAB, rows, n_tap), F32),
                        pltpu.VMEM((rows, st), F32), pltpu.VMEM((rows, st), F32)],
        compiler_params=_params("parallel", "parallel"),
        name="s5_core",
    )(u, h0p, m, wsp, wop, a16)
    hf = hf[0].transpose(1, 0, 2) if independent else hf[:, :, 0, :]
    hf_re = hf[:, :, :2 * SSM_STATE].reshape(bsz, N_GROUPS, SSM_STATE)
    hf_im = hf[:, :, 2 * SSM_STATE:].reshape(bsz, N_GROUPS, SSM_STATE)
    return y, hf_re, hf_im


def _conv_kernel(z_ref, left_ref, w_ref, b_ref, g_ref, beta_ref, y_ref, cn_ref, vbuf, *, tm):
    @pl.when(pl.program_id(1) == 0)
    def _():
        vbuf[0:CONV_HALO, :] = left_ref[...]

    z = z_ref[...]
    vbuf[CONV_HALO:CONV_HALO + tm, :] = z[:, :D_CONV] * jax.nn.sigmoid(z[:, D_CONV:])
    first = CONV_HALO - (CONV_WIDTH - 1)
    acc = jnp.zeros((tm, D_CONV), F32)
    for k in range(CONV_WIDTH):
        acc = acc + w_ref[k:k + 1, :] * vbuf[first + k:first + k + tm, :]
    y = acc + b_ref[...]
    mu = jnp.mean(y, axis=-1, keepdims=True)
    yc = y - mu
    var = jnp.mean(yc * yc, axis=-1, keepdims=True)
    yn = yc * lax.rsqrt(var + EPS) * g_ref[...] + beta_ref[...]
    y_ref[...] = yn * jax.nn.sigmoid(yn)
    cn_ref[...] = vbuf[tm + first:tm + CONV_HALO, :]
    vbuf[0:CONV_HALO, :] = vbuf[tm:tm + CONV_HALO, :]


def _conv_mixer(z, left, w, b, g, beta, bsz, t_len, tm):
    left = jnp.pad(left.astype(F32), ((0, 0), (CONV_HALO - (CONV_WIDTH - 1), 0), (0, 0)))
    vec = pl.BlockSpec((1, D_CONV), lambda bi, j: (0, 0))
    return pl.pallas_call(
        functools.partial(_conv_kernel, tm=tm),
        grid=(bsz, t_len // tm),
        in_specs=[pl.BlockSpec((None, tm, 2 * D_CONV), lambda bi, j: (bi, j, 0)),
                  pl.BlockSpec((None, CONV_HALO, D_CONV), lambda bi, j: (bi, 0, 0)),
                  pl.BlockSpec((CONV_WIDTH, D_CONV), lambda bi, j: (0, 0)), vec, vec, vec],
        out_specs=[pl.BlockSpec((None, tm, D_CONV), lambda bi, j: (bi, j, 0)),
                   pl.BlockSpec((None, CONV_WIDTH - 1, D_CONV), lambda bi, j: (bi, 0, 0))],
        out_shape=[jax.ShapeDtypeStruct((bsz, t_len, D_CONV), F32),
                   jax.ShapeDtypeStruct((bsz, CONV_WIDTH - 1, D_CONV), F32)],
        scratch_shapes=[pltpu.VMEM((tm + CONV_HALO, D_CONV), F32)],
        compiler_params=_params("parallel", "arbitrary"),
        name="conv_mixer",
    )(z, left, w, b.reshape(1, D_CONV), g.reshape(1, D_CONV), beta.reshape(1, D_CONV))


def _route(logits):
    lane_i = lax.broadcasted_iota(jnp.int32, logits.shape, 1)
    lane = lane_i.astype(F32)
    group_of_lane = (lane_i >> 3).astype(F32)
    neg = -jnp.inf
    far = float(LANES)
    is_g = (lane_i >= N_EXPERTS) & (lane_i < N_EXPERTS + N_EXPERT_GROUPS)
    gl = jnp.where(is_g, logits, neg)
    g_max = jnp.max(gl, axis=-1, keepdims=True)
    g_lane = jnp.min(jnp.where(gl == g_max, lane, far), axis=-1, keepdims=True)
    g_gate = 1.0 / jnp.sum(jnp.exp(gl - g_max), axis=-1, keepdims=True)
    g_idx = g_lane - float(N_EXPERTS)
    in_group = (lane_i < N_EXPERTS) & (group_of_lane == g_idx)
    el = jnp.where(in_group, logits, neg)
    v1 = jnp.max(el, axis=-1, keepdims=True)
    i1 = jnp.min(jnp.where(el == v1, lane, far), axis=-1, keepdims=True)
    el2 = jnp.where(lane == i1, neg, el)
    v2 = jnp.max(el2, axis=-1, keepdims=True)
    i2 = jnp.min(jnp.where(el2 == v2, lane, far), axis=-1, keepdims=True)
    e2 = jnp.exp(v2 - v1)
    w1 = g_gate / (1.0 + e2)
    w2 = g_gate * e2 / (1.0 + e2)
    return i1, i2, w1, w2


ROUTE_E1, ROUTE_E2, ROUTE_W1, ROUTE_W2, ROUTE_RANK1, ROUTE_RANK2 = range(6)


def _outproj_kernel(x_ref, ys_ref, u_ref, yb_ref, d_ref, wglu_ref, wout_ref, nf_ref, rw_ref, rb_ref,
                    x1_ref, hn_ref, route_ref, cnt_ref):
    @pl.when(pl.program_id(0) == 0)
    def _():
        cnt_ref[...] = jnp.zeros_like(cnt_ref)

    ys = jnp.concatenate([ys_ref[q] for q in range(N_SLABS)], axis=1)
    u = jnp.concatenate([u_ref[q] for q in range(N_SLABS)], axis=1)
    z = jax.nn.gelu(ys + d_ref[...] * u)
    ya = z * jax.nn.sigmoid(_bdot(z, wglu_ref[...]))
    mix = _bdot(ya, wout_ref[0:D_SSM, :]) + _bdot(yb_ref[...], wout_ref[D_SSM:, :])
    x1 = x_ref[...] + mix
    x1_ref[...] = x1
    hn = _rms(x1, nf_ref[...])
    hn_ref[...] = hn
    logits = jnp.dot(hn, rw_ref[...], precision=lax.Precision.HIGHEST,
                     preferred_element_type=F32) + rb_ref[...]
    i1, i2, w1, w2 = _route(logits)
    tm = logits.shape[0]
    lane_i = lax.broadcasted_iota(jnp.int32, logits.shape, 1)
    lane = lane_i.astype(F32)
    picked = jnp.where((lane == i1) | (lane == i2), 1.0, 0.0)
    earlier = (lax.broadcasted_iota(jnp.int32, (tm, tm), 0) > lax.broadcasted_iota(jnp.int32, (tm, tm), 1))
    prefix = _bdot(jnp.where(earlier, 1.0, 0.0), picked.astype(BF16)) + cnt_ref[...]
    rank1 = jnp.sum(jnp.where(lane == i1, prefix, 0.0), axis=-1, keepdims=True)
    rank2 = jnp.sum(jnp.where(lane == i2, prefix, 0.0), axis=-1, keepdims=True)
    cnt_ref[...] += jnp.sum(picked, axis=0, keepdims=True)
    rec = jnp.zeros_like(logits)
    for lane_id, val in ((ROUTE_E1, i1), (ROUTE_E2, i2), (ROUTE_W1, w1), (ROUTE_W2, w2),
                         (ROUTE_RANK1, rank1), (ROUTE_RANK2, rank2)):
        rec = jnp.where(lane_i == lane_id, val, rec)
    route_ref[...] = rec


def _outproj(x, ys, u, yb, d, wglu_bf, wout_bf, nf, rw, rb, tm):
    n = x.shape[0]
    return pl.pallas_call(
        _outproj_kernel,
        grid=(n // tm,),
        in_specs=[_row_spec(tm, D_MODEL), _slab_spec(tm), _slab_spec(tm), _row_spec(tm, D_CONV),
                  _const_spec((1, D_SSM)), _const_spec((D_SSM, D_SSM)), _const_spec((D_MODEL, D_MODEL)),
                  _const_spec((1, D_MODEL)), _const_spec((D_MODEL, LANES)), _const_spec((1, LANES))],
        out_specs=[_row_spec(tm, D_MODEL), _row_spec(tm, D_MODEL), _row_spec(tm, LANES),
                   _const_spec((1, LANES))],
        out_shape=[jax.ShapeDtypeStruct((n, D_MODEL), F32), jax.ShapeDtypeStruct((n, D_MODEL), F32),
                   jax.ShapeDtypeStruct((n, LANES), F32), jax.ShapeDtypeStruct((1, LANES), F32)],
        compiler_params=_params("arbitrary"),
        name="outproj_router",
    )(x, ys, u, yb, d.reshape(1, D_SSM), wglu_bf, wout_bf, nf.reshape(1, D_MODEL), rw, rb)


def _moe_plan(route, counts, tms, n_tiles):
    cnt = counts[0, :N_EXPERTS].astype(jnp.int32)
    padded = ((cnt + tms - 1) // tms) * tms
    ends = jnp.cumsum(padded)
    starts = ends - padded
    e1 = route[:, ROUTE_E1].astype(jnp.int32)
    e2 = route[:, ROUTE_E2].astype(jnp.int32)
    s1 = starts[e1] + route[:, ROUTE_RANK1].astype(jnp.int32)
    s2 = starts[e2] + route[:, ROUTE_RANK2].astype(jnp.int32)
    slots = jnp.stack([s1, s2], axis=1).reshape(-1)
    n_used = ends[-1] // tms
    tile = jnp.minimum(jnp.arange(n_tiles, dtype=jnp.int32), n_used - 1)
    tile_expert = jnp.minimum(jnp.searchsorted(ends, tile * tms, side="right"), N_EXPERTS - 1)
    return slots, tile_expert.astype(jnp.int32), n_used.reshape(1).astype(jnp.int32)


def _dispatch_kernel(slot_ref, hn_ref, xs_init_ref, xs_ref, sem, *, tm):
    del xs_init_ref

    def issue(r, carry):
        for k in range(2):
            pltpu.make_async_copy(hn_ref.at[pl.ds(r, 1)], xs_ref.at[pl.ds(slot_ref[2 * r + k], 1)],
                                  sem).start()
        return carry

    lax.fori_loop(0, tm, issue, 0)
    for k in range(2):
        pltpu.make_async_copy(hn_ref, xs_ref.at[pl.ds(0, tm)], sem).wait()


def _dispatch(hn, slots, n_slots, tm):
    n = hn.shape[0]
    return pl.pallas_call(
        functools.partial(_dispatch_kernel, tm=tm),
        grid=(n // tm,),
        in_specs=[pl.BlockSpec((2 * tm,), lambda i: (i,), memory_space=pltpu.SMEM),
                  _row_spec(tm, D_MODEL), pl.BlockSpec(memory_space=pl.ANY)],
        out_specs=pl.BlockSpec(memory_space=pl.ANY),
        out_shape=jax.ShapeDtypeStruct((n_slots, D_MODEL), F32),
        scratch_shapes=[pltpu.SemaphoreType.DMA],
        input_output_aliases={2: 0},
        compiler_params=_params("arbitrary"),
        name="moe_dispatch",
    )(slots, hn, jnp.zeros((n_slots, D_MODEL), F32))


def _moe_kernel(te_ref, nu_ref, x_ref, wg_ref, wu_ref, wd_ref, y_ref):
    del te_ref

    in_use = pl.program_id(0) < nu_ref[0]

    @pl.when(in_use)
    def _():
        h = x_ref[...].astype(BF16)
        hg = jnp.dot(h, wg_ref[...], preferred_element_type=F32)
        hu = jnp.dot(h, wu_ref[...], preferred_element_type=F32)
        y_ref[...] = _bdot(hg * jax.nn.sigmoid(hg) * hu, wd_ref[...])

    @pl.when(jnp.logical_not(in_use))
    def _():
        y_ref[...] = jnp.zeros_like(y_ref)


def _moe(xs, tile_expert, n_used, wg_bf, wu_bf, wd_bf, tms):
    n_slots = xs.shape[0]
    rows = pl.BlockSpec((tms, D_MODEL), lambda i, te, nu: (jnp.minimum(i, nu[0] - 1), 0))
    out_rows = pl.BlockSpec((tms, D_MODEL), lambda i, te, nu: (i, 0))

    def wspec(a, b):
        return pl.BlockSpec((None, a, b), lambda i, te, nu: (te[i], 0, 0))

    return pl.pallas_call(
        _moe_kernel,
        grid_spec=pltpu.PrefetchScalarGridSpec(
            num_scalar_prefetch=2, grid=(n_slots // tms,),
            in_specs=[rows, wspec(D_MODEL, D_EXPERT), wspec(D_MODEL, D_EXPERT), wspec(D_EXPERT, D_MODEL)],
            out_specs=out_rows),
        out_shape=jax.ShapeDtypeStruct((n_slots, D_MODEL), F32),
        compiler_params=_params("arbitrary"),
        name="moe",
    )(tile_expert, n_used, xs, wg_bf, wu_bf, wd_bf)


def _ple_kernel(slot_ref, x_ref, route_ref, p_ref, np_ref, wple_ref, wgate_ref, nfin_ref, ys_ref,
                o_ref, ybuf, sem, *, tm, final):
    def issue(r, carry):
        for k in range(2):
            pltpu.make_async_copy(ys_ref.at[pl.ds(slot_ref[2 * r + k], 1)], ybuf.at[k, pl.ds(r, 1)],
                                  sem).start()
        return carry

    lax.fori_loop(0, tm, issue, 0)
    pe = _bdot(p_ref[...], wple_ref[...])
    for k in range(2):
        pltpu.make_async_copy(ys_ref.at[pl.ds(0, tm)], ybuf.at[k], sem).wait()
    route = route_ref[...]
    x = (x_ref[...] + route[:, ROUTE_W1:ROUTE_W1 + 1] * ybuf[0]
         + route[:, ROUTE_W2:ROUTE_W2 + 1] * ybuf[1])
    gate = jax.nn.sigmoid(_bdot(_rms(x, np_ref[...]), wgate_ref[...]))
    out = x + pe * gate
    if final:
        out = _rms(out, nfin_ref[...])
    o_ref[...] = out


def _ple(x, route, slots, ys, p, npl, wple_bf, wgate_bf, nfin, tm, final):
    n = x.shape[0]
    return pl.pallas_call(
        functools.partial(_ple_kernel, tm=tm, final=final),
        grid=(n // tm,),
        in_specs=[pl.BlockSpec((2 * tm,), lambda i: (i,), memory_space=pltpu.SMEM),
                  _row_spec(tm, D_MODEL), _row_spec(tm, LANES), _row_spec(tm, D_PLE),
                  _const_spec((1, D_MODEL)), _const_spec((D_PLE, D_MODEL)), _const_spec((D_MODEL, D_MODEL)),
                  _const_spec((1, D_MODEL)), pl.BlockSpec(memory_space=pl.ANY)],
        out_specs=_row_spec(tm, D_MODEL),
        out_shape=jax.ShapeDtypeStruct((n, D_MODEL), F32),
        scratch_shapes=[pltpu.VMEM((2, tm, D_MODEL), F32), pltpu.SemaphoreType.DMA],
        compiler_params=_params("arbitrary"),
        name="combine_ple",
    )(slots, x, route, p, npl.reshape(1, D_MODEL), wple_bf, wgate_bf, nfin.reshape(1, D_MODEL), ys)


def _layer(x, p, h0_re, h0_im, conv_left, w, bsz, t_len, tm, tm_conv, tms, final):
    n = bsz * t_len
    u, z = _inproj(x, w["norm_mix"], w["w_in"], tm)
    ys, hf_re, hf_im = _s5_mixer(u, h0_re, h0_im, w["s5"], bsz, t_len)
    yb, conv_new = _conv_mixer(z.reshape(bsz, t_len, 2 * D_CONV), conv_left, w["conv_w"], w["conv_b"],
                               w["conv_ln_g"], w["conv_ln_b"], bsz, t_len, tm_conv)
    x1, hn, route, counts = _outproj(x, ys, u, yb.reshape(n, D_CONV), w["ssm_d"], w["w_ssm_glu"], w["w_out"],
                                     w["norm_ffn"], w["router_w"], w["router_b"], tm)
    n_tiles = 2 * n // tms + N_EXPERTS
    slots, tile_expert, n_used = _moe_plan(route, counts, tms, n_tiles)
    xs = _dispatch(hn, slots, n_tiles * tms, tm)
    ysort = _moe(xs, tile_expert, n_used, w["expert_w_gate"], w["expert_w_up"], w["expert_w_down"], tms)
    x3 = _ple(x1, route, slots, ysort, p, w["norm_ple"], w["ple_w"], w["ple_gate_w"], w["norm_final"],
              tm, final)
    return x3, hf_re, hf_im, conv_new


def kernel(x_prompt, x_sample, p_prompt, p_sample, state_ssm_re, state_ssm_im, cache_conv, norm_mix, w_in, ssm_a_re, ssm_a_im, ssm_b_re, ssm_b_im, ssm_c_re, ssm_c_im, ssm_d, ssm_log_dt, w_ssm_glu, conv_w, conv_b, conv_ln_g, conv_ln_b, w_out, norm_ffn, router_group_w, router_group_b, router_expert_w, router_expert_b, expert_w_gate, expert_w_up, expert_w_down, norm_ple, ple_w, ple_gate_w, norm_final):
    depth = w_in.shape[0]
    bp, tp, _ = x_prompt.shape
    bs, ts, _ = x_sample.shape
    xp = x_prompt.reshape(bp * tp, D_MODEL)
    xs = x_sample.reshape(bs * ts, D_MODEL)
    zero_state = jnp.zeros((bp, N_GROUPS, SSM_STATE), F32)
    zero_conv = jnp.zeros((bp, CONV_WIDTH - 1, D_CONV), F32)
    pad_lanes = LANES - N_EXPERTS - N_EXPERT_GROUPS
    outs = {k: [] for k in ("pr_re", "pr_im", "pr_conv", "sm_re", "sm_im", "sm_conv")}
    for i in range(depth):
        w = {
            "norm_mix": norm_mix[i], "w_in": w_in[i].astype(BF16),
            "s5": _s5_prep(ssm_a_re[i], ssm_a_im[i], ssm_log_dt[i], ssm_b_re[i], ssm_b_im[i],
                           ssm_c_re[i], ssm_c_im[i]),
            "ssm_d": ssm_d[i], "w_ssm_glu": w_ssm_glu[i].astype(BF16),
            "conv_w": conv_w[i], "conv_b": conv_b[i], "conv_ln_g": conv_ln_g[i], "conv_ln_b": conv_ln_b[i],
            "w_out": w_out[i].astype(BF16), "norm_ffn": norm_ffn[i],
            "router_w": jnp.pad(jnp.concatenate([router_expert_w[i], router_group_w[i]], axis=1),
                                ((0, 0), (0, pad_lanes))),
            "router_b": jnp.pad(jnp.concatenate([router_expert_b[i], router_group_b[i]]),
                                (0, pad_lanes)).reshape(1, LANES),
            "expert_w_gate": expert_w_gate[i].astype(BF16), "expert_w_up": expert_w_up[i].astype(BF16),
            "expert_w_down": expert_w_down[i].astype(BF16),
            "norm_ple": norm_ple[i], "ple_w": ple_w[i].astype(BF16), "ple_gate_w": ple_gate_w[i].astype(BF16),
            "norm_final": norm_final,
        }
        final = i == depth - 1
        xp, hr, hi, cv = _layer(xp, p_prompt[i].reshape(bp * tp, D_PLE), zero_state, zero_state, zero_conv, w,
                                bp, tp, tm=512, tm_conv=512, tms=256, final=final)
        outs["pr_re"].append(hr); outs["pr_im"].append(hi); outs["pr_conv"].append(cv)
        xs, hr, hi, cv = _layer(xs, p_sample[i].reshape(bs * ts, D_PLE), state_ssm_re[i], state_ssm_im[i],
                                cache_conv[i], w, bs, ts, tm=bs * ts, tm_conv=ts, tms=32, final=final)
        outs["sm_re"].append(hr); outs["sm_im"].append(hi); outs["sm_conv"].append(cv)
    return (xp.reshape(bp, tp, D_MODEL), xs.reshape(bs, ts, D_MODEL),
            jnp.stack(outs["pr_re"]), jnp.stack(outs["pr_im"]), jnp.stack(outs["pr_conv"]),
            jnp.stack(outs["sm_re"]), jnp.stack(outs["sm_im"]), jnp.stack(outs["sm_conv"]))
```

```python
import functools

import jax
import jax.numpy as jnp
from jax import lax
from jax.experimental import pallas as pl
from jax.experimental.pallas import tpu as pltpu

F32 = jnp.float32
BF16 = jnp.bfloat16

D_MODEL = 1024
D_SSM = 512
SSM_GROUP = 16
N_GROUPS = D_SSM // SSM_GROUP
N_PAIRS = N_GROUPS // 2
SSM_STATE = 64
D_CONV = 512
CONV_WIDTH = 31
CONV_HALO = 32
N_EXPERT_GROUPS = 4
EXPERTS_PER_GROUP = 8
N_EXPERTS = 32
D_EXPERT = 256
D_PLE = 256
EPS = 1e-6
S5_CHUNK = 16
LANES = 128
SUBLANES = 8
N_SLABS = D_SSM // LANES
GROUPS_PER_SLAB = LANES // SSM_GROUP
PAIRS_PER_SLAB = GROUPS_PER_SLAB // 2
VMEM_LIMIT = 56 * 1024 * 1024


def _params(*sem):
    return pltpu.CompilerParams(dimension_semantics=sem, vmem_limit_bytes=VMEM_LIMIT)


def _rms(x, g):
    return x * lax.rsqrt(jnp.mean(x * x, axis=-1, keepdims=True) + EPS) * g


def _bdot(a, b):
    return jnp.dot(a.astype(BF16), b, preferred_element_type=F32)


def _row_spec(tm, width):
    return pl.BlockSpec((tm, width), lambda i: (i, 0))


def _slab_spec(tm):
    return pl.BlockSpec((N_SLABS, tm, LANES), lambda i: (0, i, 0))


def _const_spec(shape):
    return pl.BlockSpec(shape, lambda i: (0,) * len(shape))


def _inproj_kernel(x_ref, g_ref, w_ref, u_ref, z_ref):
    hn = _rms(x_ref[...], g_ref[...])
    proj = _bdot(hn, w_ref[...])
    for q in range(N_SLABS):
        u_ref[q] = proj[:, q * LANES:(q + 1) * LANES]
    z_ref[...] = proj[:, D_SSM:]


def _inproj(x, g, w_bf, tm):
    n = x.shape[0]
    d_in = w_bf.shape[1]
    return pl.pallas_call(
        _inproj_kernel,
        grid=(n // tm,),
        in_specs=[_row_spec(tm, D_MODEL), _const_spec((1, D_MODEL)), _const_spec((D_MODEL, d_in))],
        out_specs=[_slab_spec(tm), _row_spec(tm, d_in - D_SSM)],
        out_shape=[jax.ShapeDtypeStruct((N_SLABS, n, LANES), F32),
                   jax.ShapeDtypeStruct((n, d_in - D_SSM), F32)],
        compiler_params=_params("parallel"),
        name="inproj",
    )(x, g.reshape(1, D_MODEL), w_bf)


def _s5_prep_kernel(ar_ref, ai_ref, ldt_ref, bre_ref, bim_ref, cre_ref, cim_ref,
                    wtre_ref, wtim_ref, brre_ref, brim_ref, kcat_ref, are_ref, aim_ref):
    ar, ai = ar_ref[...], ai_ref[...]
    dt = jnp.exp(ldt_ref[...])
    n_pow = S5_CHUNK + 8
    k = lax.broadcasted_iota(jnp.int32, (n_pow, SSM_STATE), 0).astype(F32)
    mag = jnp.exp(k * (dt * ar))
    ang = k * (dt * ai)
    p_re, p_im = mag * jnp.cos(ang), mag * jnp.sin(ang)
    inv = 1.0 / (ar * ar + ai * ai)
    ab_re, ab_im = p_re[1:2], p_im[1:2]
    ia_re, ia_im = ar * inv, -ai * inv
    coef_re = (ab_re - 1.0) * ia_re - ab_im * ia_im
    coef_im = (ab_re - 1.0) * ia_im + ab_im * ia_re
    bre, bim = bre_ref[...], bim_ref[...]
    bb_re = coef_re * bre - coef_im * bim
    bb_im = coef_re * bim + coef_im * bre
    cre, cim = cre_ref[...], cim_ref[...]
    for kk in range(S5_CHUNK + 1):
        pr, pi = p_re[kk:kk + 1], p_im[kk:kk + 1]
        rows = slice(kk * SSM_GROUP, (kk + 1) * SSM_GROUP)
        wtre_ref[rows, :] = pr * cre - pi * cim
        wtim_ref[rows, :] = -pi * cre - pr * cim
        if kk < S5_CHUNK:
            brre_ref[rows, :] = pr * bb_re - pi * bb_im
            brim_ref[rows, :] = pi * bb_re + pr * bb_im
    n_tap = S5_CHUNK * SSM_GROUP
    nt = (((1,), (1,)), ((), ()))
    kcat_ref[...] = (
        lax.dot_general(bb_re, wtre_ref[0:n_tap, :], nt, precision=lax.Precision.HIGHEST,
                        preferred_element_type=F32)
        + lax.dot_general(bb_im, wtim_ref[0:n_tap, :], nt, precision=lax.Precision.HIGHEST,
                          preferred_element_type=F32))
    kc = float(S5_CHUNK) * lax.broadcasted_iota(jnp.int32, (2 * SUBLANES, SSM_STATE), 0).astype(F32)
    magc = jnp.exp(kc * (dt * ar))
    angc = kc * (dt * ai)
    are_ref[...] = magc * jnp.cos(angc)
    aim_ref[...] = magc * jnp.sin(angc)


def _s5_prep(a_re, a_im, log_dt, b_re, b_im, c_re, c_im):
    g, n, c = N_GROUPS, SSM_STATE, SSM_GROUP
    n_tap = S5_CHUNK * c

    def gspec(*shape):
        return pl.BlockSpec((None,) + shape, lambda i: (i,) + (0,) * len(shape))

    outs = pl.pallas_call(
        _s5_prep_kernel,
        grid=(g,),
        in_specs=[gspec(1, n), gspec(1, n), gspec(1, 1), gspec(c, n), gspec(c, n), gspec(c, n), gspec(c, n)],
        out_specs=[gspec(n_tap + c, n), gspec(n_tap + c, n), gspec(n_tap, n), gspec(n_tap, n),
                   gspec(c, n_tap), gspec(2 * SUBLANES, n), gspec(2 * SUBLANES, n)],
        out_shape=[jax.ShapeDtypeStruct((g, n_tap + c, n), F32), jax.ShapeDtypeStruct((g, n_tap + c, n), F32),
                   jax.ShapeDtypeStruct((g, n_tap, n), F32), jax.ShapeDtypeStruct((g, n_tap, n), F32),
                   jax.ShapeDtypeStruct((g, c, n_tap), F32),
                   jax.ShapeDtypeStruct((g, 2 * SUBLANES, n), F32),
                   jax.ShapeDtypeStruct((g, 2 * SUBLANES, n), F32)],
        compiler_params=_params("parallel"),
        name="s5_prep",
    )(a_re.reshape(g, 1, n), a_im.reshape(g, 1, n), log_dt.reshape(g, 1, 1),
      jnp.swapaxes(b_re, 1, 2), jnp.swapaxes(b_im, 1, 2), c_re, c_im)
    wt_re, wt_im, br_re, br_im, kcat, a_re16, a_im16 = outs

    L = S5_CHUNK
    kk = kcat.reshape(g, c, L, c)
    lag = jnp.arange(L)[None, :] - jnp.arange(L)[:, None]
    m = kk[:, :, jnp.clip(lag, 0, L - 1), :]
    m = jnp.where((lag >= 0)[None, None, :, :, None], m, 0.0)
    m = m.transpose(0, 2, 1, 3, 4).reshape(g, n_tap, n_tap)

    def flip_k(b):
        return b.reshape(g, L, c, n)[:, ::-1].reshape(g, n_tap, n)

    ws_re, ws_im = flip_k(br_re), flip_k(br_im)
    z = jnp.zeros((N_PAIRS, n_tap, n), F32)
    e, o = slice(0, None, 2), slice(1, None, 2)
    wsp = jnp.concatenate([
        jnp.concatenate([ws_re[e], z, ws_im[e], z], axis=2),
        jnp.concatenate([z, ws_re[o], z, ws_im[o]], axis=2)], axis=1)
    wo_re = jnp.swapaxes(wt_re[:, c:], 1, 2)
    wo_im = jnp.swapaxes(wt_im[:, c:], 1, 2)
    zo = jnp.zeros((N_PAIRS, n, n_tap), F32)
    wop = jnp.concatenate([
        jnp.concatenate([wo_re[e], zo], axis=2), jnp.concatenate([zo, wo_re[o]], axis=2),
        jnp.concatenate([wo_im[e], zo], axis=2), jnp.concatenate([zo, wo_im[o]], axis=2)], axis=1)
    a16 = jnp.concatenate([a_re16[e], a_re16[o], a_im16[e], a_im16[o]], axis=2)
    return (m.reshape(N_PAIRS, 2, n_tap, n_tap).astype(BF16), wsp.astype(BF16), wop.astype(BF16), a16)


def _block_transpose8(vs):
    lane = lax.broadcasted_iota(jnp.int32, vs[0].shape, 1)
    blk = lane >> 4
    for d in (4, 2, 1):
        keep = (blk & d) == 0
        new = list(vs)
        for i in range(GROUPS_PER_SLAB):
            if i & d == 0:
                a, b = vs[i], vs[i + d]
                new[i] = jnp.where(keep, a, pltpu.roll(b, d * SSM_GROUP, 1))
                new[i + d] = jnp.where(keep, pltpu.roll(a, LANES - d * SSM_GROUP, 1), b)
        vs = new
    return vs


def _cmul(ar, ai, xr, xi):
    return ar * xr - ai * xi, ar * xi + ai * xr


def _s5_kernel(u_ref, h0_ref, m_ref, ws_ref, wo_ref, a_ref, y_ref, hf_ref, x_scr, yg_scr, s_scr, hp_scr,
               *, rows, independent):
    half = 2 * SSM_STATE
    rt = min(rows, 4 * SUBLANES)
    half_chunk = S5_CHUNK // 2

    def gather_tile(t, carry):
        r0 = pl.multiple_of(t * rt, rt)
        for hf in range(2):
            vs = [u_ref[pl.ds(r0 * S5_CHUNK + hf * half_chunk + i, rt, stride=S5_CHUNK), :]
                  for i in range(half_chunk)]
            outs = _block_transpose8(vs)
            for g in range(GROUPS_PER_SLAB):
                x_scr[g, pl.ds(r0, rt), hf * LANES:(hf + 1) * LANES] = outs[g]
        return carry

    lax.fori_loop(0, rows // rt, gather_tile, 0)

    row = lax.broadcasted_iota(jnp.int32, (SUBLANES, half), 0)
    n_tap = S5_CHUNK * SSM_GROUP
    for pi in range(PAIRS_PER_SLAB):
        x0 = x_scr[2 * pi].astype(BF16)
        x1 = x_scr[2 * pi + 1].astype(BF16)
        s_scr[...] = jnp.dot(jnp.concatenate([x0, x1], axis=1), ws_ref[pi], preferred_element_type=F32)
        ap = a_ref[pi]
        h0 = h0_ref[pi]
        if independent:
            hp_scr[...] = h0
            s = s_scr[...]
            n_re, n_im = _cmul(ap[1:2, :half], ap[1:2, half:], h0[:, :half], h0[:, half:])
            hf_ref[pi] = jnp.concatenate([n_re + s[:, :half], n_im + s[:, half:]], axis=1)
        else:
            pw_re, pw_im = ap[0:SUBLANES, :half], ap[0:SUBLANES, half:]

            def scan_tile(t, carry):
                h_re, h_im = carry
                r0 = pl.multiple_of(t * SUBLANES, SUBLANES)
                s = s_scr[pl.ds(r0, SUBLANES), :]
                t_re, t_im = s[:, :half], s[:, half:]
                for d in (1, 2, 4):
                    sh_re = jnp.where(row >= d, pltpu.roll(t_re, d, 0), 0.0)
                    sh_im = jnp.where(row >= d, pltpu.roll(t_im, d, 0), 0.0)
                    m_re, m_im = _cmul(ap[d:d + 1, :half], ap[d:d + 1, half:], sh_re, sh_im)
                    t_re, t_im = t_re + m_re, t_im + m_im
                e_re = jnp.where(row >= 1, pltpu.roll(t_re, 1, 0), 0.0)
                e_im = jnp.where(row >= 1, pltpu.roll(t_im, 1, 0), 0.0)
                c_re, c_im = _cmul(pw_re, pw_im, h_re, h_im)
                hp_scr[pl.ds(r0, SUBLANES), :] = jnp.concatenate([e_re + c_re, e_im + c_im], axis=1)
                o_re, o_im = _cmul(ap[SUBLANES:SUBLANES + 1, :half], ap[SUBLANES:SUBLANES + 1, half:],
                                   h_re, h_im)
                last = SUBLANES - 1
                n_re = jnp.broadcast_to(t_re[last:last + 1], h_re.shape) + o_re
                n_im = jnp.broadcast_to(t_im[last:last + 1], h_im.shape) + o_im
                return n_re, n_im

            init = (jnp.broadcast_to(h0[:, :half], (SUBLANES, half)),
                    jnp.broadcast_to(h0[:, half:], (SUBLANES, half)))
            h_re, h_im = lax.fori_loop(0, rows // SUBLANES, scan_tile, init)
            hf_ref[pi] = jnp.concatenate([h_re[0:1], h_im[0:1]], axis=1)
        yc = _bdot(hp_scr[...], wo_ref[pi])
        yg_scr[2 * pi] = jnp.dot(x0, m_ref[pi, 0], preferred_element_type=F32) + yc[:, :n_tap]
        yg_scr[2 * pi + 1] = jnp.dot(x1, m_ref[pi, 1], preferred_element_type=F32) + yc[:, n_tap:]

    def scatter_tile(t, carry):
        r0 = pl.multiple_of(t * rt, rt)
        for hf in range(2):
            vs = [yg_scr[g, pl.ds(r0, rt), hf * LANES:(hf + 1) * LANES] for g in range(GROUPS_PER_SLAB)]
            outs = _block_transpose8(vs)
            for i in range(half_chunk):
                y_ref[pl.ds(r0 * S5_CHUNK + hf * half_chunk + i, rt, stride=S5_CHUNK), :] = outs[i]
        return carry

    lax.fori_loop(0, rows // rt, scatter_tile, 0)


def _s5_mixer(u, h0_re, h0_im, prep, bsz, t_len):
    m, wsp, wop, a16 = prep
    n_tap = S5_CHUNK * SSM_GROUP
    st = 4 * SSM_STATE
    independent = t_len == S5_CHUNK
    if independent:
        nblk, rows, hrows = 1, bsz, bsz
    else:
        nblk, rows, hrows = bsz, t_len // S5_CHUNK, 1
    assert t_len % S5_CHUNK == 0 and rows % SUBLANES == 0, (bsz, t_len)
    h0p = jnp.concatenate([h0_re.reshape(bsz, N_PAIRS, 2 * SSM_STATE),
                           h0_im.reshape(bsz, N_PAIRS, 2 * SSM_STATE)], axis=2).astype(F32)
    h0p = h0p.transpose(1, 0, 2)[None] if independent else h0p[:, :, None, :]
    pp = PAIRS_PER_SLAB

    def wspec(*shape):
        return pl.BlockSpec((pp,) + shape, lambda q, b: (q,) + (0,) * len(shape))

    frames = rows * S5_CHUNK
    y, hf = pl.pallas_call(
        functools.partial(_s5_kernel, rows=rows, independent=independent),
        grid=(N_SLABS, nblk),
        in_specs=[pl.BlockSpec((None, frames, LANES), lambda q, b: (q, b, 0)),
                  pl.BlockSpec((None, pp, hrows, st), lambda q, b: (b, q, 0, 0)),
                  wspec(2, n_tap, n_tap), wspec(2 * n_tap, st), wspec(st, 2 * n_tap), wspec(2 * SUBLANES, st)],
        out_specs=[pl.BlockSpec((None, frames, LANES), lambda q, b: (q, b, 0)),
                   pl.BlockSpec((None, pp, hrows, st), lambda q, b: (b, q, 0, 0))],
        out_shape=[jax.ShapeDtypeStruct(u.shape, F32),
                   jax.ShapeDtypeStruct((nblk, N_PAIRS, hrows, st), F32)],
        scratch_shapes=[pltpu.VMEM((GROUPS_PER_SLAB, rows, n_tap), F32),
                        pltpu.VMEM((GROUPS_PER_SLAB, rows, n_tap), F32),
                        pltpu.VMEM((rows, st), F32), pltpu.VMEM((rows, st), F32)],
        compiler_params=_params("parallel", "parallel"),
        name="s5_core",
    )(u, h0p, m, wsp, wop, a16)
    hf = hf[0].transpose(1, 0, 2) if independent else hf[:, :, 0, :]
    hf_re = hf[:, :, :2 * SSM_STATE].reshape(bsz, N_GROUPS, SSM_STATE)
    hf_im = hf[:, :, 2 * SSM_STATE:].reshape(bsz, N_GROUPS, SSM_STATE)
    return y, hf_re, hf_im


def _conv_kernel(z_ref, left_ref, w_ref, b_ref, g_ref, beta_ref, y_ref, cn_ref, vbuf, *, tm):
    @pl.when(pl.program_id(1) == 0)
    def _():
        vbuf[0:CONV_HALO, :] = left_ref[...]

    z = z_ref[...]
    vbuf[CONV_HALO:CONV_HALO + tm, :] = z[:, :D_CONV] * jax.nn.sigmoid(z[:, D_CONV:])
    first = CONV_HALO - (CONV_WIDTH - 1)
    acc = jnp.zeros((tm, D_CONV), F32)
    for k in range(CONV_WIDTH):
        acc = acc + w_ref[k:k + 1, :] * vbuf[first + k:first + k + tm, :]
    y = acc + b_ref[...]
    mu = jnp.mean(y, axis=-1, keepdims=True)
    yc = y - mu
    var = jnp.mean(yc * yc, axis=-1, keepdims=True)
    yn = yc * lax.rsqrt(var + EPS) * g_ref[...] + beta_ref[...]
    y_ref[...] = yn * jax.nn.sigmoid(yn)
    cn_ref[...] = vbuf[tm + first:tm + CONV_HALO, :]
    vbuf[0:CONV_HALO, :] = vbuf[tm:tm + CONV_HALO, :]


def _conv_mixer(z, left, w, b, g, beta, bsz, t_len, tm):
    left = jnp.pad(left.astype(F32), ((0, 0), (CONV_HALO - (CONV_WIDTH - 1), 0), (0, 0)))
    vec = pl.BlockSpec((1, D_CONV), lambda bi, j: (0, 0))
    return pl.pallas_call(
        functools.partial(_conv_kernel, tm=tm),
        grid=(bsz, t_len // tm),
        in_specs=[pl.BlockSpec((None, tm, 2 * D_CONV), lambda bi, j: (bi, j, 0)),
                  pl.BlockSpec((None, CONV_HALO, D_CONV), lambda bi, j: (bi, 0, 0)),
                  pl.BlockSpec((CONV_WIDTH, D_CONV), lambda bi, j: (0, 0)), vec, vec, vec],
        out_specs=[pl.BlockSpec((None, tm, D_CONV), lambda bi, j: (bi, j, 0)),
                   pl.BlockSpec((None, CONV_WIDTH - 1, D_CONV), lambda bi, j: (bi, 0, 0))],
        out_shape=[jax.ShapeDtypeStruct((bsz, t_len, D_CONV), F32),
                   jax.ShapeDtypeStruct((bsz, CONV_WIDTH - 1, D_CONV), F32)],
        scratch_shapes=[pltpu.VMEM((tm + CONV_HALO, D_CONV), F32)],
        compiler_params=_params("parallel", "arbitrary"),
        name="conv_mixer",
    )(z, left, w, b.reshape(1, D_CONV), g.reshape(1, D_CONV), beta.reshape(1, D_CONV))


def _route(logits):
    lane_i = lax.broadcasted_iota(jnp.int32, logits.shape, 1)
    lane = lane_i.astype(F32)
    group_of_lane = (lane_i >> 3).astype(F32)
    neg = -jnp.inf
    far = float(LANES)
    is_g = (lane_i >= N_EXPERTS) & (lane_i < N_EXPERTS + N_EXPERT_GROUPS)
    gl = jnp.where(is_g, logits, neg)
    g_max = jnp.max(gl, axis=-1, keepdims=True)
    g_lane = jnp.min(jnp.where(gl == g_max, lane, far), axis=-1, keepdims=True)
    g_gate = 1.0 / jnp.sum(jnp.exp(gl - g_max), axis=-1, keepdims=True)
    g_idx = g_lane - float(N_EXPERTS)
    in_group = (lane_i < N_EXPERTS) & (group_of_lane == g_idx)
    el = jnp.where(in_group, logits, neg)
    v1 = jnp.max(el, axis=-1, keepdims=True)
    i1 = jnp.min(jnp.where(el == v1, lane, far), axis=-1, keepdims=True)
    el2 = jnp.where(lane == i1, neg, el)
    v2 = jnp.max(el2, axis=-1, keepdims=True)
    i2 = jnp.min(jnp.where(el2 == v2, lane, far), axis=-1, keepdims=True)
    e2 = jnp.exp(v2 - v1)
    w1 = g_gate / (1.0 + e2)
    w2 = g_gate * e2 / (1.0 + e2)
    return i1, i2, w1, w2


ROUTE_E1, ROUTE_E2, ROUTE_W1, ROUTE_W2, ROUTE_RANK1, ROUTE_RANK2 = range(6)


def _outproj_kernel(x_ref, ys_ref, u_ref, yb_ref, d_ref, wglu_ref, wout_ref, nf_ref, rw_ref, rb_ref,
                    x1_ref, hn_ref, route_ref, rt_ref, cnt_ref):
    @pl.when(pl.program_id(0) == 0)
    def _():
        cnt_ref[...] = jnp.zeros_like(cnt_ref)

    ys = jnp.concatenate([ys_ref[q] for q in range(N_SLABS)], axis=1)
    u = jnp.concatenate([u_ref[q] for q in range(N_SLABS)], axis=1)
    z = jax.nn.gelu(ys + d_ref[...] * u)
    ya = z * jax.nn.sigmoid(_bdot(z, wglu_ref[...]))
    mix = _bdot(ya, wout_ref[0:D_SSM, :]) + _bdot(yb_ref[...], wout_ref[D_SSM:, :])
    x1 = x_ref[...] + mix
    x1_ref[...] = x1
    hn = _rms(x1, nf_ref[...])
    hn_ref[...] = hn
    logits = jnp.dot(hn, rw_ref[...], precision=lax.Precision.HIGHEST,
                     preferred_element_type=F32) + rb_ref[...]
    i1, i2, w1, w2 = _route(logits)
    tm = logits.shape[0]
    lane_i = lax.broadcasted_iota(jnp.int32, logits.shape, 1)
    lane = lane_i.astype(F32)
    picked = jnp.where((lane == i1) | (lane == i2), 1.0, 0.0)
    earlier = (lax.broadcasted_iota(jnp.int32, (tm, tm), 0) > lax.broadcasted_iota(jnp.int32, (tm, tm), 1))
    prefix = _bdot(jnp.where(earlier, 1.0, 0.0), picked.astype(BF16)) + cnt_ref[...]
    rank1 = jnp.sum(jnp.where(lane == i1, prefix, 0.0), axis=-1, keepdims=True)
    rank2 = jnp.sum(jnp.where(lane == i2, prefix, 0.0), axis=-1, keepdims=True)
    cnt_ref[...] += jnp.sum(picked, axis=0, keepdims=True)
    rec = jnp.zeros_like(logits)
    for lane_id, val in ((ROUTE_E1, i1), (ROUTE_E2, i2), (ROUTE_W1, w1), (ROUTE_W2, w2),
                         (ROUTE_RANK1, rank1), (ROUTE_RANK2, rank2)):
        rec = jnp.where(lane_i == lane_id, val, rec)
    route_ref[...] = rec
    rt_ref[...] = rec.T[0:SUBLANES, :]


def _outproj(x, ys, u, yb, d, wglu_bf, wout_bf, nf, rw, rb, tm):
    n = x.shape[0]
    return pl.pallas_call(
        _outproj_kernel,
        grid=(n // tm,),
        in_specs=[_row_spec(tm, D_MODEL), _slab_spec(tm), _slab_spec(tm), _row_spec(tm, D_CONV),
                  _const_spec((1, D_SSM)), _const_spec((D_SSM, D_SSM)), _const_spec((D_MODEL, D_MODEL)),
                  _const_spec((1, D_MODEL)), _const_spec((D_MODEL, LANES)), _const_spec((1, LANES))],
        out_specs=[_row_spec(tm, D_MODEL), _row_spec(tm, D_MODEL), _row_spec(tm, LANES),
                   pl.BlockSpec((SUBLANES, tm), lambda i: (0, i)), _const_spec((1, LANES))],
        out_shape=[jax.ShapeDtypeStruct((n, D_MODEL), F32), jax.ShapeDtypeStruct((n, D_MODEL), F32),
                   jax.ShapeDtypeStruct((n, LANES), F32), jax.ShapeDtypeStruct((SUBLANES, n), F32),
                   jax.ShapeDtypeStruct((1, LANES), F32)],
        compiler_params=_params("arbitrary"),
        name="outproj_router",
    )(x, ys, u, yb, d.reshape(1, D_SSM), wglu_bf, wout_bf, nf.reshape(1, D_MODEL), rw, rb)


PLAN_TILE_LANES = 2 * LANES
PLAN_EXPERT, PLAN_ROWS, PLAN_USED = range(3)


def _moe_plan_kernel(rt_ref, cnt_ref, slot_ref, tile_ref, *, tms):
    cnt = cnt_ref[...]
    padded = jnp.ceil(cnt * (1.0 / tms)) * float(tms)
    r = lax.broadcasted_iota(jnp.int32, (LANES, LANES), 0)
    c = lax.broadcasted_iota(jnp.int32, (LANES, LANES), 1)
    ends = jnp.dot(padded, jnp.where(r <= c, 1.0, 0.0), precision=lax.Precision.HIGHEST,
                   preferred_element_type=F32)
    starts = ends - padded
    rt = rt_ref[...]
    e1, e2 = rt[ROUTE_E1:ROUTE_E1 + 1], rt[ROUTE_E2:ROUTE_E2 + 1]
    s1, s2 = rt[ROUTE_RANK1:ROUTE_RANK1 + 1], rt[ROUTE_RANK2:ROUTE_RANK2 + 1]
    tile = lax.broadcasted_iota(jnp.int32, (1, PLAN_TILE_LANES), 1).astype(F32)
    used = ends[:, N_EXPERTS - 1:N_EXPERTS] * (1.0 / tms)
    pos = jnp.minimum(tile, used - 1.0) * float(tms)
    t_exp = jnp.zeros_like(tile)
    t_fill = jnp.zeros_like(tile)
    for e in range(N_EXPERTS):
        st, en = starts[:, e:e + 1], ends[:, e:e + 1]
        s1 = s1 + jnp.where(e1 == float(e), st, 0.0)
        s2 = s2 + jnp.where(e2 == float(e), st, 0.0)
        mine = (pos >= st) & (pos < en)
        t_exp = t_exp + jnp.where(mine, float(e), 0.0)
        t_fill = t_fill + jnp.where(mine, st + cnt[:, e:e + 1], 0.0)
    t_rows = jnp.where(tile < used, jnp.clip(t_fill - pos, 0.0, float(tms)), 0.0)
    slot_ref[...] = jnp.concatenate([s1, s2], axis=0).astype(jnp.int32)
    tile_ref[...] = jnp.concatenate(
        [t_exp, t_rows, jnp.broadcast_to(used, tile.shape), jnp.zeros((SUBLANES - 3, PLAN_TILE_LANES), F32)],
        axis=0).astype(jnp.int32)


def _moe_plan(route_t, counts, tms, n_tiles):
    n = route_t.shape[1]
    assert n_tiles <= PLAN_TILE_LANES
    slots, tiles = pl.pallas_call(
        functools.partial(_moe_plan_kernel, tms=tms),
        out_shape=[jax.ShapeDtypeStruct((2, n), jnp.int32),
                   jax.ShapeDtypeStruct((SUBLANES, PLAN_TILE_LANES), jnp.int32)],
        compiler_params=pltpu.CompilerParams(vmem_limit_bytes=VMEM_LIMIT),
        name="moe_plan",
    )(route_t, counts)
    return slots, tiles[PLAN_EXPERT, :n_tiles], tiles[PLAN_ROWS, :n_tiles], tiles[PLAN_USED, :1]


DMA_UNROLL = 8


def _dispatch_kernel(tr_ref, slot_ref, hn_ref, xs_ref, zbuf, sem, zsem, *, tm, tms, n_tiles):
    @pl.when(pl.program_id(0) == 0)
    def _():
        zbuf[...] = jnp.zeros_like(zbuf)

        def fill(t, carry):
            @pl.when(tr_ref[t] < tms)
            def _():
                pltpu.make_async_copy(zbuf, xs_ref.at[pl.ds(pl.multiple_of(t * tms, tms), tms)], zsem).start()
            return carry

        def drain(t, carry):
            @pl.when(tr_ref[t] < tms)
            def _():
                pltpu.make_async_copy(zbuf, xs_ref.at[pl.ds(0, tms)], zsem).wait()
            return carry

        lax.fori_loop(0, n_tiles, fill, 0)
        lax.fori_loop(0, n_tiles, drain, 0)

    def issue(r, carry):
        for k in range(2):
            pltpu.make_async_copy(hn_ref.at[pl.ds(r, 1)], xs_ref.at[pl.ds(slot_ref[k, r], 1)], sem).start()
        return carry

    lax.fori_loop(0, tm, issue, 0, unroll=DMA_UNROLL)
    for k in range(2):
        pltpu.make_async_copy(hn_ref, xs_ref.at[pl.ds(0, tm)], sem).wait()


def _dispatch(hn, slots, tile_rows, tms, tm):
    n = hn.shape[0]
    n_tiles = tile_rows.shape[0]
    return pl.pallas_call(
        functools.partial(_dispatch_kernel, tm=tm, tms=tms, n_tiles=n_tiles),
        grid_spec=pltpu.PrefetchScalarGridSpec(
            num_scalar_prefetch=1, grid=(n // tm,),
            in_specs=[pl.BlockSpec((2, tm), lambda i, tr: (0, i), memory_space=pltpu.SMEM),
                      pl.BlockSpec((tm, D_MODEL), lambda i, tr: (i, 0))],
            out_specs=pl.BlockSpec(memory_space=pl.ANY),
            scratch_shapes=[pltpu.VMEM((tms, D_MODEL), F32), pltpu.SemaphoreType.DMA,
                            pltpu.SemaphoreType.DMA]),
        out_shape=jax.ShapeDtypeStruct((n_tiles * tms, D_MODEL), F32),
        compiler_params=_params("arbitrary"),
        name="moe_dispatch",
    )(tile_rows, slots, hn)


def _moe_kernel(te_ref, nu_ref, x_ref, wg_ref, wu_ref, wd_ref, y_ref):
    del te_ref
    in_use = pl.program_id(0) < nu_ref[0]

    @pl.when(in_use)
    def _():
        h = x_ref[...].astype(BF16)
        hg = jnp.dot(h, wg_ref[...], preferred_element_type=F32)
        hu = jnp.dot(h, wu_ref[...], preferred_element_type=F32)
        y_ref[...] = _bdot(hg * jax.nn.sigmoid(hg) * hu, wd_ref[...])

    @pl.when(jnp.logical_not(in_use))
    def _():
        y_ref[...] = jnp.zeros_like(y_ref)


def _moe(xs, tile_expert, n_used, wg_bf, wu_bf, wd_bf, tms):
    n_slots = xs.shape[0]
    rows = pl.BlockSpec((tms, D_MODEL), lambda i, te, nu: (jnp.minimum(i, nu[0] - 1), 0))
    out_rows = pl.BlockSpec((tms, D_MODEL), lambda i, te, nu: (i, 0))

    def wspec(a, b):
        return pl.BlockSpec((None, a, b), lambda i, te, nu: (te[i], 0, 0))

    return pl.pallas_call(
        _moe_kernel,
        grid_spec=pltpu.PrefetchScalarGridSpec(
            num_scalar_prefetch=2, grid=(n_slots // tms,),
            in_specs=[rows, wspec(D_MODEL, D_EXPERT), wspec(D_MODEL, D_EXPERT), wspec(D_EXPERT, D_MODEL)],
            out_specs=out_rows),
        out_shape=jax.ShapeDtypeStruct((n_slots, D_MODEL), F32),
        compiler_params=_params("arbitrary"),
        name="moe",
    )(tile_expert, n_used, xs, wg_bf, wu_bf, wd_bf)


def _ple_kernel(slot_ref, x_ref, route_ref, p_ref, np_ref, wple_ref, wgate_ref, nfin_ref, ys_ref,
                o_ref, ybuf, sem, *, tm, final):
    def issue(r, carry):
        for k in range(2):
            pltpu.make_async_copy(ys_ref.at[pl.ds(slot_ref[k, r], 1)], ybuf.at[k, pl.ds(r, 1)],
                                  sem).start()
        return carry

    lax.fori_loop(0, tm, issue, 0, unroll=DMA_UNROLL)
    pe = _bdot(p_ref[...], wple_ref[...])
    for k in range(2):
        pltpu.make_async_copy(ys_ref.at[pl.ds(0, tm)], ybuf.at[k], sem).wait()
    route = route_ref[...]
    x = (x_ref[...] + route[:, ROUTE_W1:ROUTE_W1 + 1] * ybuf[0]
         + route[:, ROUTE_W2:ROUTE_W2 + 1] * ybuf[1])
    gate = jax.nn.sigmoid(_bdot(_rms(x, np_ref[...]), wgate_ref[...]))
    out = x + pe * gate
    if final:
        out = _rms(out, nfin_ref[...])
    o_ref[...] = out


def _ple(x, route, slots, ys, p, npl, wple_bf, wgate_bf, nfin, tm, final):
    n = x.shape[0]
    return pl.pallas_call(
        functools.partial(_ple_kernel, tm=tm, final=final),
        grid=(n // tm,),
        in_specs=[pl.BlockSpec((2, tm), lambda i: (0, i), memory_space=pltpu.SMEM),
                  _row_spec(tm, D_MODEL), _row_spec(tm, LANES), _row_spec(tm, D_PLE),
                  _const_spec((1, D_MODEL)), _const_spec((D_PLE, D_MODEL)), _const_spec((D_MODEL, D_MODEL)),
                  _const_spec((1, D_MODEL)), pl.BlockSpec(memory_space=pl.ANY)],
        out_specs=_row_spec(tm, D_MODEL),
        out_shape=jax.ShapeDtypeStruct((n, D_MODEL), F32),
        scratch_shapes=[pltpu.VMEM((2, tm, D_MODEL), F32), pltpu.SemaphoreType.DMA],
        compiler_params=_params("arbitrary"),
        name="combine_ple",
    )(slots, x, route, p, npl.reshape(1, D_MODEL), wple_bf, wgate_bf, nfin.reshape(1, D_MODEL), ys)


def _layer(x, p, h0_re, h0_im, conv_left, w, bsz, t_len, tm, tm_conv, tms, final):
    n = bsz * t_len
    u, z = _inproj(x, w["norm_mix"], w["w_in"], tm)
    ys, hf_re, hf_im = _s5_mixer(u, h0_re, h0_im, w["s5"], bsz, t_len)
    yb, conv_new = _conv_mixer(z.reshape(bsz, t_len, 2 * D_CONV), conv_left, w["conv_w"], w["conv_b"],
                               w["conv_ln_g"], w["conv_ln_b"], bsz, t_len, tm_conv)
    x1, hn, route, route_t, counts = _outproj(x, ys, u, yb.reshape(n, D_CONV), w["ssm_d"], w["w_ssm_glu"],
                                              w["w_out"], w["norm_ffn"], w["router_w"], w["router_b"], tm)
    n_tiles = 2 * n // tms + N_EXPERTS
    slots, tile_expert, tile_rows, n_used = _moe_plan(route_t, counts, tms, n_tiles)
    xs = _dispatch(hn, slots, tile_rows, tms, tm)
    ysort = _moe(xs, tile_expert, n_used, w["expert_w_gate"], w["expert_w_up"], w["expert_w_down"], tms)
    x3 = _ple(x1, route, slots, ysort, p, w["norm_ple"], w["ple_w"], w["ple_gate_w"], w["norm_final"],
              tm, final)
    return x3, hf_re, hf_im, conv_new


def kernel(x_prompt, x_sample, p_prompt, p_sample, state_ssm_re, state_ssm_im, cache_conv, norm_mix, w_in, ssm_a_re, ssm_a_im, ssm_b_re, ssm_b_im, ssm_c_re, ssm_c_im, ssm_d, ssm_log_dt, w_ssm_glu, conv_w, conv_b, conv_ln_g, conv_ln_b, w_out, norm_ffn, router_group_w, router_group_b, router_expert_w, router_expert_b, expert_w_gate, expert_w_up, expert_w_down, norm_ple, ple_w, ple_gate_w, norm_final):
    depth = w_in.shape[0]
    bp, tp, _ = x_prompt.shape
    bs, ts, _ = x_sample.shape
    xp = x_prompt.reshape(bp * tp, D_MODEL)
    xs = x_sample.reshape(bs * ts, D_MODEL)
    zero_state = jnp.zeros((bp, N_GROUPS, SSM_STATE), F32)
    zero_conv = jnp.zeros((bp, CONV_WIDTH - 1, D_CONV), F32)
    pad_lanes = LANES - N_EXPERTS - N_EXPERT_GROUPS
    outs = {k: [] for k in ("pr_re", "pr_im", "pr_conv", "sm_re", "sm_im", "sm_conv")}
    for i in range(depth):
        w = {
            "norm_mix": norm_mix[i], "w_in": w_in[i].astype(BF16),
            "s5": _s5_prep(ssm_a_re[i], ssm_a_im[i], ssm_log_dt[i], ssm_b_re[i], ssm_b_im[i],
                           ssm_c_re[i], ssm_c_im[i]),
            "ssm_d": ssm_d[i], "w_ssm_glu": w_ssm_glu[i].astype(BF16),
            "conv_w": conv_w[i], "conv_b": conv_b[i], "conv_ln_g": conv_ln_g[i], "conv_ln_b": conv_ln_b[i],
            "w_out": w_out[i].astype(BF16), "norm_ffn": norm_ffn[i],
            "router_w": jnp.pad(jnp.concatenate([router_expert_w[i], router_group_w[i]], axis=1),
                                ((0, 0), (0, pad_lanes))),
            "router_b": jnp.pad(jnp.concatenate([router_expert_b[i], router_group_b[i]]),
                                (0, pad_lanes)).reshape(1, LANES),
            "expert_w_gate": expert_w_gate[i].astype(BF16), "expert_w_up": expert_w_up[i].astype(BF16),
            "expert_w_down": expert_w_down[i].astype(BF16),
            "norm_ple": norm_ple[i], "ple_w": ple_w[i].astype(BF16), "ple_gate_w": ple_gate_w[i].astype(BF16),
            "norm_final": norm_final,
        }
        final = i == depth - 1
        xp, hr, hi, cv = _layer(xp, p_prompt[i].reshape(bp * tp, D_PLE), zero_state, zero_state, zero_conv, w,
                                bp, tp, tm=512, tm_conv=512, tms=256, final=final)
        outs["pr_re"].append(hr); outs["pr_im"].append(hi); outs["pr_conv"].append(cv)
        xs, hr, hi, cv = _layer(xs, p_sample[i].reshape(bs * ts, D_PLE), state_ssm_re[i], state_ssm_im[i],
                                cache_conv[i], w, bs, ts, tm=bs * ts, tm_conv=ts, tms=32, final=final)
        outs["sm_re"].append(hr); outs["sm_im"].append(hi); outs["sm_conv"].append(cv)
    return (xp.reshape(bp, tp, D_MODEL), xs.reshape(bs, ts, D_MODEL),
            jnp.stack(outs["pr_re"]), jnp.stack(outs["pr_im"]), jnp.stack(outs["pr_conv"]),
            jnp.stack(outs["sm_re"]), jnp.stack(outs["sm_im"]), jnp.stack(outs["sm_conv"]))
```

```python
import functools

import jax
import jax.numpy as jnp
from jax import lax
from jax.experimental import pallas as pl
from jax.experimental.pallas import tpu as pltpu

F32 = jnp.float32
BF16 = jnp.bfloat16

D_MODEL = 1024
D_SSM = 512
SSM_GROUP = 16
N_GROUPS = D_SSM // SSM_GROUP
N_PAIRS = N_GROUPS // 2
SSM_STATE = 64
D_CONV = 512
CONV_WIDTH = 31
CONV_HALO = 32
N_EXPERT_GROUPS = 4
EXPERTS_PER_GROUP = 8
N_EXPERTS = 32
D_EXPERT = 256
D_PLE = 256
EPS = 1e-6
S5_CHUNK = 16
LANES = 128
SUBLANES = 8
N_SLABS = D_SSM // LANES
GROUPS_PER_SLAB = LANES // SSM_GROUP
PAIRS_PER_SLAB = GROUPS_PER_SLAB // 2
VMEM_LIMIT = 56 * 1024 * 1024


def _params(*sem):
    return pltpu.CompilerParams(dimension_semantics=sem, vmem_limit_bytes=VMEM_LIMIT)


def _rms(x, g):
    return x * lax.rsqrt(jnp.mean(x * x, axis=-1, keepdims=True) + EPS) * g


def _bdot(a, b):
    return jnp.dot(a.astype(BF16), b, preferred_element_type=F32)


def _row_spec(tm, width):
    return pl.BlockSpec((tm, width), lambda i: (i, 0))


def _slab_spec(tm):
    return pl.BlockSpec((N_SLABS, tm, LANES), lambda i: (0, i, 0))


def _const_spec(shape):
    return pl.BlockSpec(shape, lambda i: (0,) * len(shape))


def _layer_spec(layer, *shape):
    return pl.BlockSpec((None,) + shape, lambda *_: (layer,) + (0,) * len(shape))


def _vec(stacked):
    return stacked.reshape(stacked.shape[0], 1, stacked.shape[1])


def _cast_weight_once(w_ref, wbf_ref):
    @pl.when(pl.program_id(0) == 0)
    def _():
        wbf_ref[...] = w_ref[...].astype(BF16)


def _inproj_kernel(x_ref, g_ref, w_ref, u_ref, z_ref, wbf):
    _cast_weight_once(w_ref, wbf)
    hn = _rms(x_ref[...], g_ref[...])
    proj = _bdot(hn, wbf[...])
    for q in range(N_SLABS):
        u_ref[q] = proj[:, q * LANES:(q + 1) * LANES]
    z_ref[...] = proj[:, D_SSM:]


def _inproj(x, g, w, layer, tm):
    n = x.shape[0]
    d_in = w.shape[2]
    return pl.pallas_call(
        _inproj_kernel,
        grid=(n // tm,),
        in_specs=[_row_spec(tm, D_MODEL), _layer_spec(layer, 1, D_MODEL), _layer_spec(layer, D_MODEL, d_in)],
        out_specs=[_slab_spec(tm), _row_spec(tm, d_in - D_SSM)],
        out_shape=[jax.ShapeDtypeStruct((N_SLABS, n, LANES), F32),
                   jax.ShapeDtypeStruct((n, d_in - D_SSM), F32)],
        scratch_shapes=[pltpu.VMEM((D_MODEL, d_in), BF16)],
        compiler_params=_params("arbitrary"),
        name="inproj",
    )(x, _vec(g), w)


def _s5_prep_kernel(ar_ref, ai_ref, ldt_ref, bre_ref, bim_ref, cre_ref, cim_ref,
                    wtre_ref, wtim_ref, brre_ref, brim_ref, kcat_ref, are_ref, aim_ref):
    ar, ai = ar_ref[...], ai_ref[...]
    dt = jnp.exp(ldt_ref[...])
    n_pow = S5_CHUNK + 8
    k = lax.broadcasted_iota(jnp.int32, (n_pow, SSM_STATE), 0).astype(F32)
    mag = jnp.exp(k * (dt * ar))
    ang = k * (dt * ai)
    p_re, p_im = mag * jnp.cos(ang), mag * jnp.sin(ang)
    inv = 1.0 / (ar * ar + ai * ai)
    ab_re, ab_im = p_re[1:2], p_im[1:2]
    ia_re, ia_im = ar * inv, -ai * inv
    coef_re = (ab_re - 1.0) * ia_re - ab_im * ia_im
    coef_im = (ab_re - 1.0) * ia_im + ab_im * ia_re
    bre, bim = bre_ref[...], bim_ref[...]
    bb_re = coef_re * bre - coef_im * bim
    bb_im = coef_re * bim + coef_im * bre
    cre, cim = cre_ref[...], cim_ref[...]
    for kk in range(S5_CHUNK + 1):
        pr, pi = p_re[kk:kk + 1], p_im[kk:kk + 1]
        rows = slice(kk * SSM_GROUP, (kk + 1) * SSM_GROUP)
        wtre_ref[rows, :] = pr * cre - pi * cim
        wtim_ref[rows, :] = -pi * cre - pr * cim
        if kk < S5_CHUNK:
            brre_ref[rows, :] = pr * bb_re - pi * bb_im
            brim_ref[rows, :] = pi * bb_re + pr * bb_im
    n_tap = S5_CHUNK * SSM_GROUP
    nt = (((1,), (1,)), ((), ()))
    kcat_ref[...] = (
        lax.dot_general(bb_re, wtre_ref[0:n_tap, :], nt, precision=lax.Precision.HIGHEST,
                        preferred_element_type=F32)
        + lax.dot_general(bb_im, wtim_ref[0:n_tap, :], nt, precision=lax.Precision.HIGHEST,
                          preferred_element_type=F32))
    kc = float(S5_CHUNK) * lax.broadcasted_iota(jnp.int32, (2 * SUBLANES, SSM_STATE), 0).astype(F32)
    magc = jnp.exp(kc * (dt * ar))
    angc = kc * (dt * ai)
    are_ref[...] = magc * jnp.cos(angc)
    aim_ref[...] = magc * jnp.sin(angc)


def _s5_prep(a_re, a_im, log_dt, b_re, b_im, c_re, c_im):
    g, n, c = N_GROUPS, SSM_STATE, SSM_GROUP
    n_tap = S5_CHUNK * c

    def gspec(*shape):
        return pl.BlockSpec((None,) + shape, lambda i: (i,) + (0,) * len(shape))

    outs = pl.pallas_call(
        _s5_prep_kernel,
        grid=(g,),
        in_specs=[gspec(1, n), gspec(1, n), gspec(1, 1), gspec(c, n), gspec(c, n), gspec(c, n), gspec(c, n)],
        out_specs=[gspec(n_tap + c, n), gspec(n_tap + c, n), gspec(n_tap, n), gspec(n_tap, n),
                   gspec(c, n_tap), gspec(2 * SUBLANES, n), gspec(2 * SUBLANES, n)],
        out_shape=[jax.ShapeDtypeStruct((g, n_tap + c, n), F32), jax.ShapeDtypeStruct((g, n_tap + c, n), F32),
                   jax.ShapeDtypeStruct((g, n_tap, n), F32), jax.ShapeDtypeStruct((g, n_tap, n), F32),
                   jax.ShapeDtypeStruct((g, c, n_tap), F32),
                   jax.ShapeDtypeStruct((g, 2 * SUBLANES, n), F32),
                   jax.ShapeDtypeStruct((g, 2 * SUBLANES, n), F32)],
        compiler_params=_params("parallel"),
        name="s5_prep",
    )(a_re.reshape(g, 1, n), a_im.reshape(g, 1, n), log_dt.reshape(g, 1, 1),
      jnp.swapaxes(b_re, 1, 2), jnp.swapaxes(b_im, 1, 2), c_re, c_im)
    wt_re, wt_im, br_re, br_im, kcat, a_re16, a_im16 = outs

    L = S5_CHUNK
    kk = kcat.reshape(g, c, L, c)
    lag = jnp.arange(L)[None, :] - jnp.arange(L)[:, None]
    m = kk[:, :, jnp.clip(lag, 0, L - 1), :]
    m = jnp.where((lag >= 0)[None, None, :, :, None], m, 0.0)
    m = m.transpose(0, 2, 1, 3, 4).reshape(g, n_tap, n_tap)

    def flip_k(b):
        return b.reshape(g, L, c, n)[:, ::-1].reshape(g, n_tap, n)

    ws_re, ws_im = flip_k(br_re), flip_k(br_im)
    z = jnp.zeros((N_PAIRS, n_tap, n), F32)
    e, o = slice(0, None, 2), slice(1, None, 2)
    wsp = jnp.concatenate([
        jnp.concatenate([ws_re[e], z, ws_im[e], z], axis=2),
        jnp.concatenate([z, ws_re[o], z, ws_im[o]], axis=2)], axis=1)
    wo_re = jnp.swapaxes(wt_re[:, c:], 1, 2)
    wo_im = jnp.swapaxes(wt_im[:, c:], 1, 2)
    zo = jnp.zeros((N_PAIRS, n, n_tap), F32)
    wop = jnp.concatenate([
        jnp.concatenate([wo_re[e], zo], axis=2), jnp.concatenate([zo, wo_re[o]], axis=2),
        jnp.concatenate([wo_im[e], zo], axis=2), jnp.concatenate([zo, wo_im[o]], axis=2)], axis=1)
    a16 = jnp.concatenate([a_re16[e], a_re16[o], a_im16[e], a_im16[o]], axis=2)
    return (m.reshape(N_PAIRS, 2, n_tap, n_tap).astype(BF16), wsp.astype(BF16), wop.astype(BF16), a16)


def _block_transpose8(vs):
    lane = lax.broadcasted_iota(jnp.int32, vs[0].shape, 1)
    blk = lane >> 4
    for d in (4, 2, 1):
        keep = (blk & d) == 0
        new = list(vs)
        for i in range(GROUPS_PER_SLAB):
            if i & d == 0:
                a, b = vs[i], vs[i + d]
                new[i] = jnp.where(keep, a, pltpu.roll(b, d * SSM_GROUP, 1))
                new[i + d] = jnp.where(keep, pltpu.roll(a, LANES - d * SSM_GROUP, 1), b)
        vs = new
    return vs


def _cmul(ar, ai, xr, xi):
    return ar * xr - ai * xi, ar * xi + ai * xr


def _s5_kernel(u_ref, h0_ref, m_ref, ws_ref, wo_ref, a_ref, y_ref, hf_ref, x_scr, yg_scr, s_scr, hp_scr,
               *, rows, independent):
    half = 2 * SSM_STATE
    rt = min(rows, 16 * SUBLANES)
    half_chunk = S5_CHUNK // 2

    def gather_tile(t, carry):
        r0 = pl.multiple_of(t * rt, rt)
        for hf in range(2):
            vs = [u_ref[pl.ds(r0 * S5_CHUNK + hf * half_chunk + i, rt, stride=S5_CHUNK), :]
                  for i in range(half_chunk)]
            outs = _block_transpose8(vs)
            for g in range(GROUPS_PER_SLAB):
                x_scr[g, pl.ds(r0, rt), hf * LANES:(hf + 1) * LANES] = outs[g]
        return carry

    lax.fori_loop(0, rows // rt, gather_tile, 0)

    row = lax.broadcasted_iota(jnp.int32, (SUBLANES, half), 0)
    n_tap = S5_CHUNK * SSM_GROUP
    for pi in range(PAIRS_PER_SLAB):
        x0 = x_scr[2 * pi].astype(BF16)
        x1 = x_scr[2 * pi + 1].astype(BF16)
        s_scr[...] = jnp.dot(jnp.concatenate([x0, x1], axis=1), ws_ref[pi], preferred_element_type=F32)
        ap = a_ref[pi]
        h0 = h0_ref[pi]
        if independent:
            hp_scr[...] = h0
            s = s_scr[...]
            n_re, n_im = _cmul(ap[1:2, :half], ap[1:2, half:], h0[:, :half], h0[:, half:])
            hf_ref[pi] = jnp.concatenate([n_re + s[:, :half], n_im + s[:, half:]], axis=1)
        else:
            pw_re, pw_im = ap[0:SUBLANES, :half], ap[0:SUBLANES, half:]

            def scan_tile(t, carry):
                h_re, h_im = carry
                r0 = pl.multiple_of(t * SUBLANES, SUBLANES)
                s = s_scr[pl.ds(r0, SUBLANES), :]
                t_re, t_im = s[:, :half], s[:, half:]
                for d in (1, 2, 4):
                    sh_re = jnp.where(row >= d, pltpu.roll(t_re, d, 0), 0.0)
                    sh_im = jnp.where(row >= d, pltpu.roll(t_im, d, 0), 0.0)
                    m_re, m_im = _cmul(ap[d:d + 1, :half], ap[d:d + 1, half:], sh_re, sh_im)
                    t_re, t_im = t_re + m_re, t_im + m_im
                e_re = jnp.where(row >= 1, pltpu.roll(t_re, 1, 0), 0.0)
                e_im = jnp.where(row >= 1, pltpu.roll(t_im, 1, 0), 0.0)
                c_re, c_im = _cmul(pw_re, pw_im, h_re, h_im)
                hp_scr[pl.ds(r0, SUBLANES), :] = jnp.concatenate([e_re + c_re, e_im + c_im], axis=1)
                o_re, o_im = _cmul(ap[SUBLANES:SUBLANES + 1, :half], ap[SUBLANES:SUBLANES + 1, half:],
                                   h_re, h_im)
                last = SUBLANES - 1
                n_re = jnp.broadcast_to(t_re[last:last + 1], h_re.shape) + o_re
                n_im = jnp.broadcast_to(t_im[last:last + 1], h_im.shape) + o_im
                return n_re, n_im

            init = (jnp.broadcast_to(h0[:, :half], (SUBLANES, half)),
                    jnp.broadcast_to(h0[:, half:], (SUBLANES, half)))
            h_re, h_im = lax.fori_loop(0, rows // SUBLANES, scan_tile, init, unroll=4)
            hf_ref[pi] = jnp.concatenate([h_re[0:1], h_im[0:1]], axis=1)
        yc = _bdot(hp_scr[...], wo_ref[pi])
        yg_scr[2 * pi] = jnp.dot(x0, m_ref[pi, 0], preferred_element_type=F32) + yc[:, :n_tap]
        yg_scr[2 * pi + 1] = jnp.dot(x1, m_ref[pi, 1], preferred_element_type=F32) + yc[:, n_tap:]

    def scatter_tile(t, carry):
        r0 = pl.multiple_of(t * rt, rt)
        for hf in range(2):
            vs = [yg_scr[g, pl.ds(r0, rt), hf * LANES:(hf + 1) * LANES] for g in range(GROUPS_PER_SLAB)]
            outs = _block_transpose8(vs)
            for i in range(half_chunk):
                y_ref[pl.ds(r0 * S5_CHUNK + hf * half_chunk + i, rt, stride=S5_CHUNK), :] = outs[i]
        return carry

    lax.fori_loop(0, rows // rt, scatter_tile, 0)


def _s5_mixer(u, h0_re, h0_im, prep, bsz, t_len):
    m, wsp, wop, a16 = prep
    n_tap = S5_CHUNK * SSM_GROUP
    st = 4 * SSM_STATE
    independent = t_len == S5_CHUNK
    if independent:
        nblk, rows, hrows = 1, bsz, bsz
    else:
        nblk, rows, hrows = bsz, t_len // S5_CHUNK, 1
    assert t_len % S5_CHUNK == 0 and rows % SUBLANES == 0, (bsz, t_len)
    h0p = jnp.concatenate([h0_re.reshape(bsz, N_PAIRS, 2 * SSM_STATE),
                           h0_im.reshape(bsz, N_PAIRS, 2 * SSM_STATE)], axis=2).astype(F32)
    h0p = h0p.transpose(1, 0, 2)[None] if independent else h0p[:, :, None, :]
    pp = PAIRS_PER_SLAB

    def wspec(*shape):
        return pl.BlockSpec((pp,) + shape, lambda q, b: (q,) + (0,) * len(shape))

    frames = rows * S5_CHUNK
    y, hf = pl.pallas_call(
        functools.partial(_s5_kernel, rows=rows, independent=independent),
        grid=(N_SLABS, nblk),
        in_specs=[pl.BlockSpec((None, frames, LANES), lambda q, b: (q, b, 0)),
                  pl.BlockSpec((None, pp, hrows, st), lambda q, b: (b, q, 0, 0)),
                  wspec(2, n_tap, n_tap), wspec(2 * n_tap, st), wspec(st, 2 * n_tap), wspec(2 * SUBLANES, st)],
        out_specs=[pl.BlockSpec((None, frames, LANES), lambda q, b: (q, b, 0)),
                   pl.BlockSpec((None, pp, hrows, st), lambda q, b: (b, q, 0, 0))],
        out_shape=[jax.ShapeDtypeStruct(u.shape, F32),
                   jax.ShapeDtypeStruct((nblk, N_PAIRS, hrows, st), F32)],
        scratch_shapes=[pltpu.VMEM((GROUPS_PER_SLAB, rows, n_tap), F32),
                        pltpu.VMEM((GROUPS_PER_SLAB, rows, n_tap), F32),
                        pltpu.VMEM((rows, st), F32), pltpu.VMEM((rows, st), F32)],
        compiler_params=_params("parallel", "parallel"),
        name="s5_core",
    )(u, h0p, m, wsp, wop, a16)
    hf = hf[0].transpose(1, 0, 2) if independent else hf[:, :, 0, :]
    hf_re = hf[:, :, :2 * SSM_STATE].reshape(bsz, N_GROUPS, SSM_STATE)
    hf_im = hf[:, :, 2 * SSM_STATE:].reshape(bsz, N_GROUPS, SSM_STATE)
    return y, hf_re, hf_im


def _conv_kernel(z_ref, left_ref, w_ref, b_ref, g_ref, beta_ref, y_ref, cn_ref, vbuf, shifted, *, tm):
    @pl.when(pl.program_id(1) == 0)
    def _():
        vbuf[0:CONV_HALO, :] = left_ref[...]

    z = z_ref[...]
    vbuf[CONV_HALO:CONV_HALO + tm, :] = z[:, :D_CONV] * jax.nn.sigmoid(z[:, D_CONV:])
    first = CONV_HALO - (CONV_WIDTH - 1)
    span = tm + CONV_HALO - SUBLANES
    for r in range(1, SUBLANES):
        shifted[r - 1, 0:span, :] = vbuf[r:r + span, :]
    acc = jnp.zeros((tm, D_CONV), F32)
    for k in range(CONV_WIDTH):
        a, r = divmod(first + k, SUBLANES)
        src = vbuf if r == 0 else shifted.at[r - 1]
        acc = acc + w_ref[k:k + 1, :] * src[a * SUBLANES:a * SUBLANES + tm, :]
    y = acc + b_ref[...]
    mu = jnp.mean(y, axis=-1, keepdims=True)
    yc = y - mu
    var = jnp.mean(yc * yc, axis=-1, keepdims=True)
    yn = yc * lax.rsqrt(var + EPS) * g_ref[...] + beta_ref[...]
    y_ref[...] = yn * jax.nn.sigmoid(yn)
    cn_ref[...] = vbuf[tm + first:tm + CONV_HALO, :]
    vbuf[0:CONV_HALO, :] = vbuf[tm:tm + CONV_HALO, :]


def _conv_mixer(z, left, w, b, g, beta, layer, bsz, t_len, tm):
    left = jnp.pad(left.astype(F32), ((0, 0), (CONV_HALO - (CONV_WIDTH - 1), 0), (0, 0)))
    nt = t_len // tm
    vec = _layer_spec(layer, 1, D_CONV)
    return pl.pallas_call(
        functools.partial(_conv_kernel, tm=tm),
        grid=(bsz, nt),
        in_specs=[pl.BlockSpec((tm, 2 * D_CONV), lambda bi, j: (bi * nt + j, 0)),
                  pl.BlockSpec((None, CONV_HALO, D_CONV), lambda bi, j: (bi, 0, 0)),
                  _layer_spec(layer, CONV_WIDTH, D_CONV), vec, vec, vec],
        out_specs=[pl.BlockSpec((tm, D_CONV), lambda bi, j: (bi * nt + j, 0)),
                   pl.BlockSpec((None, CONV_WIDTH - 1, D_CONV), lambda bi, j: (bi, 0, 0))],
        out_shape=[jax.ShapeDtypeStruct((bsz * t_len, D_CONV), F32),
                   jax.ShapeDtypeStruct((bsz, CONV_WIDTH - 1, D_CONV), F32)],
        scratch_shapes=[pltpu.VMEM((tm + CONV_HALO, D_CONV), F32),
                        pltpu.VMEM((SUBLANES - 1, tm + CONV_HALO - SUBLANES, D_CONV), F32)],
        compiler_params=_params("parallel", "arbitrary"),
        name="conv_mixer",
    )(z, left, w, _vec(b), _vec(g), _vec(beta))


def _route(logits):
    lane_i = lax.broadcasted_iota(jnp.int32, logits.shape, 1)
    lane = lane_i.astype(F32)
    group_of_lane = (lane_i >> 3).astype(F32)
    neg = -jnp.inf
    far = float(LANES)
    is_g = (lane_i >= N_EXPERTS) & (lane_i < N_EXPERTS + N_EXPERT_GROUPS)
    gl = jnp.where(is_g, logits, neg)
    g_max = jnp.max(gl, axis=-1, keepdims=True)
    g_lane = jnp.min(jnp.where(gl == g_max, lane, far), axis=-1, keepdims=True)
    g_gate = 1.0 / jnp.sum(jnp.exp(gl - g_max), axis=-1, keepdims=True)
    g_idx = g_lane - float(N_EXPERTS)
    in_group = (lane_i < N_EXPERTS) & (group_of_lane == g_idx)
    el = jnp.where(in_group, logits, neg)
    v1 = jnp.max(el, axis=-1, keepdims=True)
    i1 = jnp.min(jnp.where(el == v1, lane, far), axis=-1, keepdims=True)
    el2 = jnp.where(lane == i1, neg, el)
    v2 = jnp.max(el2, axis=-1, keepdims=True)
    i2 = jnp.min(jnp.where(el2 == v2, lane, far), axis=-1, keepdims=True)
    e2 = jnp.exp(v2 - v1)
    w1 = g_gate / (1.0 + e2)
    w2 = g_gate * e2 / (1.0 + e2)
    return i1, i2, w1, w2


ROUTE_E1, ROUTE_E2, ROUTE_W1, ROUTE_W2, ROUTE_RANK1, ROUTE_RANK2 = range(6)


def _outproj_kernel(x_ref, ys_ref, u_ref, yb_ref, d_ref, wglu_ref, wout_ref, nf_ref, rw_ref, rb_ref,
                    x1_ref, hn_ref, route_ref, rt_ref, cnt_ref, wglu_bf, wout_bf):
    @pl.when(pl.program_id(0) == 0)
    def _():
        cnt_ref[...] = jnp.zeros_like(cnt_ref)

    _cast_weight_once(wglu_ref, wglu_bf)
    _cast_weight_once(wout_ref, wout_bf)
    ys = jnp.concatenate([ys_ref[q] for q in range(N_SLABS)], axis=1)
    u = jnp.concatenate([u_ref[q] for q in range(N_SLABS)], axis=1)
    z = jax.nn.gelu(ys + d_ref[...] * u)
    ya = z * jax.nn.sigmoid(_bdot(z, wglu_bf[...]))
    mix = _bdot(ya, wout_bf[0:D_SSM, :]) + _bdot(yb_ref[...], wout_bf[D_SSM:, :])
    x1 = x_ref[...] + mix
    x1_ref[...] = x1
    hn = _rms(x1, nf_ref[...])
    hn_ref[...] = hn
    logits = jnp.dot(hn, rw_ref[...], precision=lax.Precision.HIGHEST,
                     preferred_element_type=F32) + rb_ref[...]
    i1, i2, w1, w2 = _route(logits)
    tm = logits.shape[0]
    lane_i = lax.broadcasted_iota(jnp.int32, logits.shape, 1)
    lane = lane_i.astype(F32)
    picked = jnp.where((lane == i1) | (lane == i2), 1.0, 0.0)
    earlier = (lax.broadcasted_iota(jnp.int32, (tm, tm), 0) > lax.broadcasted_iota(jnp.int32, (tm, tm), 1))
    prefix = _bdot(jnp.where(earlier, 1.0, 0.0), picked.astype(BF16)) + cnt_ref[...]
    rank1 = jnp.sum(jnp.where(lane == i1, prefix, 0.0), axis=-1, keepdims=True)
    rank2 = jnp.sum(jnp.where(lane == i2, prefix, 0.0), axis=-1, keepdims=True)
    cnt_ref[...] += jnp.sum(picked, axis=0, keepdims=True)
    rec = jnp.zeros_like(logits)
    for lane_id, val in ((ROUTE_E1, i1), (ROUTE_E2, i2), (ROUTE_W1, w1), (ROUTE_W2, w2),
                         (ROUTE_RANK1, rank1), (ROUTE_RANK2, rank2)):
        rec = jnp.where(lane_i == lane_id, val, rec)
    route_ref[...] = rec
    rt_ref[...] = rec.T[0:SUBLANES, :]


def _outproj(x, ys, u, yb, d, wglu, wout, nf, rw, rb, layer, tm):
    n = x.shape[0]
    return pl.pallas_call(
        _outproj_kernel,
        grid=(n // tm,),
        in_specs=[_row_spec(tm, D_MODEL), _slab_spec(tm), _slab_spec(tm), _row_spec(tm, D_CONV),
                  _layer_spec(layer, 1, D_SSM), _layer_spec(layer, D_SSM, D_SSM),
                  _layer_spec(layer, D_MODEL, D_MODEL), _layer_spec(layer, 1, D_MODEL),
                  _const_spec((D_MODEL, LANES)), _const_spec((1, LANES))],
        out_specs=[_row_spec(tm, D_MODEL), _row_spec(tm, D_MODEL), _row_spec(tm, LANES),
                   pl.BlockSpec((SUBLANES, tm), lambda i: (0, i)), _const_spec((1, LANES))],
        out_shape=[jax.ShapeDtypeStruct((n, D_MODEL), F32), jax.ShapeDtypeStruct((n, D_MODEL), F32),
                   jax.ShapeDtypeStruct((n, LANES), F32), jax.ShapeDtypeStruct((SUBLANES, n), F32),
                   jax.ShapeDtypeStruct((1, LANES), F32)],
        scratch_shapes=[pltpu.VMEM((D_SSM, D_SSM), BF16), pltpu.VMEM((D_MODEL, D_MODEL), BF16)],
        compiler_params=_params("arbitrary"),
        name="outproj_router",
    )(x, ys, u, yb, _vec(d), wglu, wout, _vec(nf), rw, rb)


PLAN_TILE_LANES = 2 * LANES
PLAN_EXPERT, PLAN_ROWS, PLAN_USED = range(3)


def _moe_plan_kernel(rt_ref, cnt_ref, slot_ref, tile_ref, *, tms):
    cnt = cnt_ref[...]
    padded = jnp.ceil(cnt * (1.0 / tms)) * float(tms)
    r = lax.broadcasted_iota(jnp.int32, (LANES, LANES), 0)
    c = lax.broadcasted_iota(jnp.int32, (LANES, LANES), 1)
    ends = jnp.dot(padded, jnp.where(r <= c, 1.0, 0.0), precision=lax.Precision.HIGHEST,
                   preferred_element_type=F32)
    starts = ends - padded
    rt = rt_ref[...]
    e1, e2 = rt[ROUTE_E1:ROUTE_E1 + 1], rt[ROUTE_E2:ROUTE_E2 + 1]
    s1, s2 = rt[ROUTE_RANK1:ROUTE_RANK1 + 1], rt[ROUTE_RANK2:ROUTE_RANK2 + 1]
    tile = lax.broadcasted_iota(jnp.int32, (1, PLAN_TILE_LANES), 1).astype(F32)
    used = ends[:, N_EXPERTS - 1:N_EXPERTS] * (1.0 / tms)
    pos = jnp.minimum(tile, used - 1.0) * float(tms)
    t_exp = jnp.zeros_like(tile)
    t_fill = jnp.zeros_like(tile)
    for e in range(N_EXPERTS):
        st, en = starts[:, e:e + 1], ends[:, e:e + 1]
        s1 = s1 + jnp.where(e1 == float(e), st, 0.0)
        s2 = s2 + jnp.where(e2 == float(e), st, 0.0)
        mine = (pos >= st) & (pos < en)
        t_exp = t_exp + jnp.where(mine, float(e), 0.0)
        t_fill = t_fill + jnp.where(mine, st + cnt[:, e:e + 1], 0.0)
    t_rows = jnp.where(tile < used, jnp.clip(t_fill - pos, 0.0, float(tms)), 0.0)
    slot_ref[...] = jnp.concatenate([s1, s2], axis=0).astype(jnp.int32)
    tile_ref[...] = jnp.concatenate(
        [t_exp, t_rows, jnp.broadcast_to(used, tile.shape), jnp.zeros((SUBLANES - 3, PLAN_TILE_LANES), F32)],
        axis=0).astype(jnp.int32)


def _moe_plan(route_t, counts, tms, n_tiles):
    n = route_t.shape[1]
    assert n_tiles <= PLAN_TILE_LANES
    slots, tiles = pl.pallas_call(
        functools.partial(_moe_plan_kernel, tms=tms),
        out_shape=[jax.ShapeDtypeStruct((2, n), jnp.int32),
                   jax.ShapeDtypeStruct((SUBLANES, PLAN_TILE_LANES), jnp.int32)],
        compiler_params=pltpu.CompilerParams(vmem_limit_bytes=VMEM_LIMIT),
        name="moe_plan",
    )(route_t, counts)
    return slots, tiles[PLAN_EXPERT, :n_tiles], tiles[PLAN_ROWS, :n_tiles], tiles[PLAN_USED, :1]


DMA_UNROLL = 8


def _dispatch_kernel(tr_ref, slot_ref, hn_ref, xs_ref, zbuf, sem, zsem, *, tm, tms, n_tiles):
    @pl.when(pl.program_id(0) == 0)
    def _():
        zbuf[...] = jnp.zeros_like(zbuf)

        def fill(t, carry):
            @pl.when(tr_ref[t] < tms)
            def _():
                pltpu.make_async_copy(zbuf, xs_ref.at[pl.ds(pl.multiple_of(t * tms, tms), tms)], zsem).start()
            return carry

        def drain(t, carry):
            @pl.when(tr_ref[t] < tms)
            def _():
                pltpu.make_async_copy(zbuf, xs_ref.at[pl.ds(0, tms)], zsem).wait()
            return carry

        lax.fori_loop(0, n_tiles, fill, 0)
        lax.fori_loop(0, n_tiles, drain, 0)

    def issue(r, carry):
        for k in range(2):
            pltpu.make_async_copy(hn_ref.at[pl.ds(r, 1)], xs_ref.at[pl.ds(slot_ref[k, r], 1)],
                                  sem).start(priority=k)
        return carry

    lax.fori_loop(0, tm, issue, 0, unroll=DMA_UNROLL)
    for k in range(2):
        pltpu.make_async_copy(hn_ref, xs_ref.at[pl.ds(0, tm)], sem).wait()


def _dispatch(hn, slots, tile_rows, tms, tm):
    n = hn.shape[0]
    n_tiles = tile_rows.shape[0]
    return pl.pallas_call(
        functools.partial(_dispatch_kernel, tm=tm, tms=tms, n_tiles=n_tiles),
        grid_spec=pltpu.PrefetchScalarGridSpec(
            num_scalar_prefetch=1, grid=(n // tm,),
            in_specs=[pl.BlockSpec((2, tm), lambda i, tr: (0, i), memory_space=pltpu.SMEM),
                      pl.BlockSpec((tm, D_MODEL), lambda i, tr: (i, 0))],
            out_specs=pl.BlockSpec(memory_space=pl.ANY),
            scratch_shapes=[pltpu.VMEM((tms, D_MODEL), F32), pltpu.SemaphoreType.DMA,
                            pltpu.SemaphoreType.DMA]),
        out_shape=jax.ShapeDtypeStruct((n_tiles * tms, D_MODEL), F32),
        compiler_params=_params("arbitrary"),
        name="moe_dispatch",
    )(tile_rows, slots, hn)


def _moe_kernel(te_ref, nu_ref, x_ref, wg_ref, wu_ref, wd_ref, y_ref, wg_bf, wu_bf, wd_bf):
    i = pl.program_id(0)
    in_use = i < nu_ref[0]
    new_expert = (i == 0) | (te_ref[i] != te_ref[jnp.maximum(i - 1, 0)])

    @pl.when(in_use & new_expert)
    def _():
        wg_bf[...] = wg_ref[...].astype(BF16)
        wu_bf[...] = wu_ref[...].astype(BF16)
        wd_bf[...] = wd_ref[...].astype(BF16)

    @pl.when(in_use)
    def _():
        h = x_ref[...].astype(BF16)
        hg = jnp.dot(h, wg_bf[...], preferred_element_type=F32)
        hu = jnp.dot(h, wu_bf[...], preferred_element_type=F32)
        y_ref[...] = _bdot(hg * jax.nn.sigmoid(hg) * hu, wd_bf[...])

    @pl.when(jnp.logical_not(in_use))
    def _():
        y_ref[...] = jnp.zeros_like(y_ref)


def _moe(xs, tile_expert, n_used, wg, wu, wd, layer, tms):
    n_slots = xs.shape[0]
    rows = pl.BlockSpec((tms, D_MODEL), lambda i, te, nu: (jnp.minimum(i, nu[0] - 1), 0))
    out_rows = pl.BlockSpec((tms, D_MODEL), lambda i, te, nu: (i, 0))

    def wspec(a, b):
        return pl.BlockSpec((None, None, a, b), lambda i, te, nu: (layer, te[i], 0, 0))

    return pl.pallas_call(
        _moe_kernel,
        grid_spec=pltpu.PrefetchScalarGridSpec(
            num_scalar_prefetch=2, grid=(n_slots // tms,),
            in_specs=[rows, wspec(D_MODEL, D_EXPERT), wspec(D_MODEL, D_EXPERT), wspec(D_EXPERT, D_MODEL)],
            out_specs=out_rows,
            scratch_shapes=[pltpu.VMEM((D_MODEL, D_EXPERT), BF16), pltpu.VMEM((D_MODEL, D_EXPERT), BF16),
                            pltpu.VMEM((D_EXPERT, D_MODEL), BF16)]),
        out_shape=jax.ShapeDtypeStruct((n_slots, D_MODEL), F32),
        compiler_params=_params("arbitrary"),
        name="moe",
    )(tile_expert, n_used, xs, wg, wu, wd)


def _ple_kernel(slot_ref, x_ref, route_ref, p_ref, np_ref, wple_ref, wgate_ref, nfin_ref, ys_ref,
                o_ref, ybuf, sem, wple_bf, wgate_bf, *, tm, final):
    _cast_weight_once(wple_ref, wple_bf)
    _cast_weight_once(wgate_ref, wgate_bf)

    def issue(r, carry):
        for k in range(2):
            pltpu.make_async_copy(ys_ref.at[pl.ds(slot_ref[k, r], 1)], ybuf.at[k, pl.ds(r, 1)],
                                  sem).start(priority=k)
        return carry

    lax.fori_loop(0, tm, issue, 0, unroll=DMA_UNROLL)
    pe = _bdot(p_ref[...], wple_bf[...])
    for k in range(2):
        pltpu.make_async_copy(ys_ref.at[pl.ds(0, tm)], ybuf.at[k], sem).wait()
    route = route_ref[...]
    x = (x_ref[...] + route[:, ROUTE_W1:ROUTE_W1 + 1] * ybuf[0]
         + route[:, ROUTE_W2:ROUTE_W2 + 1] * ybuf[1])
    gate = jax.nn.sigmoid(_bdot(_rms(x, np_ref[...]), wgate_bf[...]))
    out = x + pe * gate
    if final:
        out = _rms(out, nfin_ref[...])
    o_ref[...] = out


def _ple(x, route, slots, ys, p, npl, wple, wgate, nfin, layer, tm, final):
    n = x.shape[0]
    return pl.pallas_call(
        functools.partial(_ple_kernel, tm=tm, final=final),
        grid=(n // tm,),
        in_specs=[pl.BlockSpec((2, tm), lambda i: (0, i), memory_space=pltpu.SMEM),
                  _row_spec(tm, D_MODEL), _row_spec(tm, LANES),
                  pl.BlockSpec((None, tm, D_PLE), lambda i: (layer, i, 0)),
                  _layer_spec(layer, 1, D_MODEL), _layer_spec(layer, D_PLE, D_MODEL),
                  _layer_spec(layer, D_MODEL, D_MODEL), _const_spec((1, D_MODEL)),
                  pl.BlockSpec(memory_space=pl.ANY)],
        out_specs=_row_spec(tm, D_MODEL),
        out_shape=jax.ShapeDtypeStruct((n, D_MODEL), F32),
        scratch_shapes=[pltpu.VMEM((2, tm, D_MODEL), F32), pltpu.SemaphoreType.DMA,
                        pltpu.VMEM((D_PLE, D_MODEL), BF16), pltpu.VMEM((D_MODEL, D_MODEL), BF16)],
        compiler_params=_params("arbitrary"),
        name="combine_ple",
    )(slots, x, route, p, _vec(npl), wple, wgate, nfin.reshape(1, D_MODEL), ys)


def _layer(x, p, h0_re, h0_im, conv_left, w, s5, router_w, router_b, layer, bsz, t_len, tm, tm_conv, tms, final):
    n = bsz * t_len
    u, z = _inproj(x, w["norm_mix"], w["w_in"], layer, tm)
    ys, hf_re, hf_im = _s5_mixer(u, h0_re, h0_im, s5, bsz, t_len)
    yb, conv_new = _conv_mixer(z, conv_left, w["conv_w"], w["conv_b"], w["conv_ln_g"], w["conv_ln_b"],
                               layer, bsz, t_len, tm_conv)
    x1, hn, route, route_t, counts = _outproj(x, ys, u, yb, w["ssm_d"], w["w_ssm_glu"], w["w_out"],
                                              w["norm_ffn"], router_w, router_b, layer, tm)
    n_tiles = 2 * n // tms + N_EXPERTS
    slots, tile_expert, tile_rows, n_used = _moe_plan(route_t, counts, tms, n_tiles)
    xs = _dispatch(hn, slots, tile_rows, tms, tm)
    ysort = _moe(xs, tile_expert, n_used, w["expert_w_gate"], w["expert_w_up"], w["expert_w_down"],
                 layer, tms)
    x3 = _ple(x1, route, slots, ysort, p, w["norm_ple"], w["ple_w"], w["ple_gate_w"], w["norm_final"],
              layer, tm, final)
    return x3, hf_re, hf_im, conv_new


def kernel(x_prompt, x_sample, p_prompt, p_sample, state_ssm_re, state_ssm_im, cache_conv, norm_mix, w_in, ssm_a_re, ssm_a_im, ssm_b_re, ssm_b_im, ssm_c_re, ssm_c_im, ssm_d, ssm_log_dt, w_ssm_glu, conv_w, conv_b, conv_ln_g, conv_ln_b, w_out, norm_ffn, router_group_w, router_group_b, router_expert_w, router_expert_b, expert_w_gate, expert_w_up, expert_w_down, norm_ple, ple_w, ple_gate_w, norm_final):
    depth = w_in.shape[0]
    bp, tp, _ = x_prompt.shape
    bs, ts, _ = x_sample.shape
    xp = x_prompt.reshape(bp * tp, D_MODEL)
    xs = x_sample.reshape(bs * ts, D_MODEL)
    pp = p_prompt.reshape(depth, bp * tp, D_PLE)
    ps = p_sample.reshape(depth, bs * ts, D_PLE)
    zero_state = jnp.zeros((bp, N_GROUPS, SSM_STATE), F32)
    zero_conv = jnp.zeros((bp, CONV_WIDTH - 1, D_CONV), F32)
    pad_lanes = LANES - N_EXPERTS - N_EXPERT_GROUPS
    w = {"norm_mix": norm_mix, "w_in": w_in, "ssm_d": ssm_d, "w_ssm_glu": w_ssm_glu, "conv_w": conv_w,
         "conv_b": conv_b, "conv_ln_g": conv_ln_g, "conv_ln_b": conv_ln_b, "w_out": w_out,
         "norm_ffn": norm_ffn, "expert_w_gate": expert_w_gate, "expert_w_up": expert_w_up,
         "expert_w_down": expert_w_down, "norm_ple": norm_ple, "ple_w": ple_w, "ple_gate_w": ple_gate_w,
         "norm_final": norm_final}
    outs = {k: [] for k in ("pr_re", "pr_im", "pr_conv", "sm_re", "sm_im", "sm_conv")}
    for i in range(depth):
        s5 = _s5_prep(ssm_a_re[i], ssm_a_im[i], ssm_log_dt[i], ssm_b_re[i], ssm_b_im[i], ssm_c_re[i],
                      ssm_c_im[i])
        router_w = jnp.pad(jnp.concatenate([router_expert_w[i], router_group_w[i]], axis=1),
                           ((0, 0), (0, pad_lanes)))
        router_b = jnp.pad(jnp.concatenate([router_expert_b[i], router_group_b[i]]),
                           (0, pad_lanes)).reshape(1, LANES)
        final = i == depth - 1
        xp, hr, hi, cv = _layer(xp, pp, zero_state, zero_state, zero_conv, w, s5, router_w, router_b, i,
                                bp, tp, tm=512, tm_conv=512, tms=256, final=final)
        outs["pr_re"].append(hr); outs["pr_im"].append(hi); outs["pr_conv"].append(cv)
        xs, hr, hi, cv = _layer(xs, ps, state_ssm_re[i], state_ssm_im[i], cache_conv[i], w, s5, router_w,
                                router_b, i, bs, ts, tm=bs * ts, tm_conv=ts, tms=32, final=final)
        outs["sm_re"].append(hr); outs["sm_im"].append(hi); outs["sm_conv"].append(cv)
    return (xp.reshape(bp, tp, D_MODEL), xs.reshape(bs, ts, D_MODEL),
            jnp.stack(outs["pr_re"]), jnp.stack(outs["pr_im"]), jnp.stack(outs["pr_conv"]),
            jnp.stack(outs["sm_re"]), jnp.stack(outs["sm_im"]), jnp.stack(outs["sm_conv"]))
```

```python
import functools

import jax
import jax.numpy as jnp
from jax import lax
from jax.experimental import pallas as pl
from jax.experimental.pallas import tpu as pltpu

F32 = jnp.float32
BF16 = jnp.bfloat16

D_MODEL = 1024
D_SSM = 512
SSM_GROUP = 16
N_GROUPS = D_SSM // SSM_GROUP
N_PAIRS = N_GROUPS // 2
SSM_STATE = 64
D_CONV = 512
CONV_WIDTH = 31
CONV_HALO = 32
N_EXPERT_GROUPS = 4
EXPERTS_PER_GROUP = 8
N_EXPERTS = 32
D_EXPERT = 256
D_PLE = 256
EPS = 1e-6
S5_CHUNK = 16
LANES = 128
SUBLANES = 8
N_SLABS = D_SSM // LANES
GROUPS_PER_SLAB = LANES // SSM_GROUP
PAIRS_PER_SLAB = GROUPS_PER_SLAB // 2
VMEM_LIMIT = 56 * 1024 * 1024


def _params(*sem):
    return pltpu.CompilerParams(dimension_semantics=sem, vmem_limit_bytes=VMEM_LIMIT)


def _rms(x, g):
    return x * lax.rsqrt(jnp.mean(x * x, axis=-1, keepdims=True) + EPS) * g


def _bdot(a, b):
    return jnp.dot(a.astype(BF16), b, preferred_element_type=F32)


def _row_spec(tm, width):
    return pl.BlockSpec((tm, width), lambda i: (i, 0))


def _slab_spec(tm):
    return pl.BlockSpec((N_SLABS, tm, LANES), lambda i: (0, i, 0))


def _const_spec(shape):
    return pl.BlockSpec(shape, lambda i: (0,) * len(shape))


def _layer_spec(layer, *shape):
    return pl.BlockSpec((None,) + shape, lambda *_: (layer,) + (0,) * len(shape))


def _vec(stacked):
    return stacked.reshape(stacked.shape[0], 1, stacked.shape[1])


def _cast_weight_once(w_ref, wbf_ref):
    @pl.when(pl.program_id(0) == 0)
    def _():
        wbf_ref[...] = w_ref[...].astype(BF16)


def _inproj_kernel(x_ref, g_ref, w_ref, u_ref, z_ref, wbf):
    _cast_weight_once(w_ref, wbf)
    hn = _rms(x_ref[...], g_ref[...])
    proj = _bdot(hn, wbf[...])
    for q in range(N_SLABS):
        u_ref[q] = proj[:, q * LANES:(q + 1) * LANES]
    z_ref[...] = proj[:, D_SSM:]


def _inproj(x, g, w, layer, tm):
    n = x.shape[0]
    d_in = w.shape[2]
    return pl.pallas_call(
        _inproj_kernel,
        grid=(n // tm,),
        in_specs=[_row_spec(tm, D_MODEL), _layer_spec(layer, 1, D_MODEL), _layer_spec(layer, D_MODEL, d_in)],
        out_specs=[_slab_spec(tm), _row_spec(tm, d_in - D_SSM)],
        out_shape=[jax.ShapeDtypeStruct((N_SLABS, n, LANES), F32),
                   jax.ShapeDtypeStruct((n, d_in - D_SSM), F32)],
        scratch_shapes=[pltpu.VMEM((D_MODEL, d_in), BF16)],
        compiler_params=_params("arbitrary"),
        name="inproj",
    )(x, _vec(g), w)


def _s5_prep_kernel(ar_ref, ai_ref, ldt_ref, bre_ref, bim_ref, cre_ref, cim_ref,
                    m_ref, ws_ref, wot_ref, atab_ref, wt_re, wt_im, br_re, br_im):
    n_tap = S5_CHUNK * SSM_GROUP
    st = 4 * SSM_STATE
    nt = (((1,), (1,)), ((), ()))
    hi = lax.Precision.HIGHEST
    src = lax.broadcasted_iota(jnp.int32, (SSM_STATE, st), 0)
    dst = lax.broadcasted_iota(jnp.int32, (SSM_STATE, st), 1)
    lane = lax.broadcasted_iota(jnp.int32, (SSM_GROUP, n_tap), 1)
    ws_rows, wot_rows, atab = [], [], jnp.zeros((2 * SUBLANES, st), F32)
    for gi in range(2):
        ar, ai = ar_ref[gi], ai_ref[gi]
        dt = jnp.exp(ldt_ref[gi])
        k = lax.broadcasted_iota(jnp.int32, (S5_CHUNK + SUBLANES, SSM_STATE), 0).astype(F32)
        mag = jnp.exp(k * (dt * ar))
        ang = k * (dt * ai)
        p_re, p_im = mag * jnp.cos(ang), mag * jnp.sin(ang)
        inv = 1.0 / (ar * ar + ai * ai)
        ab_re, ab_im = p_re[1:2], p_im[1:2]
        ia_re, ia_im = ar * inv, -ai * inv
        coef_re = (ab_re - 1.0) * ia_re - ab_im * ia_im
        coef_im = (ab_re - 1.0) * ia_im + ab_im * ia_re
        bre, bim = bre_ref[gi], bim_ref[gi]
        bb_re = coef_re * bre - coef_im * bim
        bb_im = coef_re * bim + coef_im * bre
        cre, cim = cre_ref[gi], cim_ref[gi]
        for kk in range(S5_CHUNK + 1):
            pr, pi = p_re[kk:kk + 1], p_im[kk:kk + 1]
            rows = slice(kk * SSM_GROUP, (kk + 1) * SSM_GROUP)
            wt_re[rows, :] = pr * cre - pi * cim
            wt_im[rows, :] = -pi * cre - pr * cim
            if kk < S5_CHUNK:
                back = slice((S5_CHUNK - 1 - kk) * SSM_GROUP, (S5_CHUNK - kk) * SSM_GROUP)
                br_re[back, :] = pr * bb_re - pi * bb_im
                br_im[back, :] = pi * bb_re + pr * bb_im
        kcat = (lax.dot_general(bb_re, wt_re[0:n_tap, :], nt, precision=hi, preferred_element_type=F32)
                + lax.dot_general(bb_im, wt_im[0:n_tap, :], nt, precision=hi, preferred_element_type=F32))
        for s in range(S5_CHUNK):
            shifted = kcat if s == 0 else pltpu.roll(kcat, s * SSM_GROUP, 1)
            m_ref[gi, s * SSM_GROUP:(s + 1) * SSM_GROUP, :] = jnp.where(
                lane >= s * SSM_GROUP, shifted, 0.0).astype(BF16)
        put_re = jnp.where(dst == src + gi * SSM_STATE, 1.0, 0.0)
        put_im = jnp.where(dst == src + (2 + gi) * SSM_STATE, 1.0, 0.0)

        def place(v_re, v_im):
            return (jnp.dot(v_re, put_re, precision=hi, preferred_element_type=F32)
                    + jnp.dot(v_im, put_im, precision=hi, preferred_element_type=F32))

        ws_rows.append(place(br_re[...], br_im[...]))
        wot_rows.append(place(wt_re[SSM_GROUP:, :], wt_im[SSM_GROUP:, :]))
        kc = float(S5_CHUNK) * lax.broadcasted_iota(jnp.int32, (2 * SUBLANES, SSM_STATE), 0).astype(F32)
        magc = jnp.exp(kc * (dt * ar))
        angc = kc * (dt * ai)
        atab = atab + place(magc * jnp.cos(angc), magc * jnp.sin(angc))
    ws_ref[...] = jnp.concatenate(ws_rows, axis=0).astype(BF16)
    wot_ref[...] = jnp.concatenate(wot_rows, axis=0).astype(BF16)
    atab_ref[...] = atab


def _s5_prep(a_re, a_im, log_dt, b_re, b_im, c_re, c_im):
    p, n, c = N_PAIRS, SSM_STATE, SSM_GROUP
    n_tap = S5_CHUNK * c
    st = 4 * n

    def pspec(*shape):
        return pl.BlockSpec((None,) + shape, lambda i: (i,) + (0,) * len(shape))

    def pairs(a, *shape):
        return a.reshape((p, 2) + shape)

    return pl.pallas_call(
        _s5_prep_kernel,
        grid=(p,),
        in_specs=[pspec(2, 1, n), pspec(2, 1, n), pspec(2, 1, 1), pspec(2, c, n), pspec(2, c, n),
                  pspec(2, c, n), pspec(2, c, n)],
        out_specs=[pspec(2, n_tap, n_tap), pspec(2 * n_tap, st), pspec(2 * n_tap, st), pspec(2 * SUBLANES, st)],
        out_shape=[jax.ShapeDtypeStruct((p, 2, n_tap, n_tap), BF16),
                   jax.ShapeDtypeStruct((p, 2 * n_tap, st), BF16),
                   jax.ShapeDtypeStruct((p, 2 * n_tap, st), BF16),
                   jax.ShapeDtypeStruct((p, 2 * SUBLANES, st), F32)],
        scratch_shapes=[pltpu.VMEM((n_tap + c, n), F32), pltpu.VMEM((n_tap + c, n), F32),
                        pltpu.VMEM((n_tap, n), F32), pltpu.VMEM((n_tap, n), F32)],
        compiler_params=_params("parallel"),
        name="s5_prep",
    )(pairs(a_re, 1, n), pairs(a_im, 1, n), pairs(log_dt, 1, 1),
      pairs(jnp.swapaxes(b_re, 1, 2), c, n), pairs(jnp.swapaxes(b_im, 1, 2), c, n),
      pairs(c_re, c, n), pairs(c_im, c, n))


def _block_transpose8(vs):
    lane = lax.broadcasted_iota(jnp.int32, vs[0].shape, 1)
    blk = lane >> 4
    for d in (4, 2, 1):
        keep = (blk & d) == 0
        new = list(vs)
        for i in range(GROUPS_PER_SLAB):
            if i & d == 0:
                a, b = vs[i], vs[i + d]
                new[i] = jnp.where(keep, a, pltpu.roll(b, d * SSM_GROUP, 1))
                new[i + d] = jnp.where(keep, pltpu.roll(a, LANES - d * SSM_GROUP, 1), b)
        vs = new
    return vs


def _cmul(ar, ai, xr, xi):
    return ar * xr - ai * xi, ar * xi + ai * xr


def _s5_kernel(u_ref, h0_ref, m_ref, ws_ref, wo_ref, a_ref, y_ref, hf_ref, x_scr, yg_scr, s_scr, hp_scr,
               *, rows, independent):
    half = 2 * SSM_STATE
    rt = min(rows, 16 * SUBLANES)
    half_chunk = S5_CHUNK // 2

    def gather_tile(t, carry):
        r0 = pl.multiple_of(t * rt, rt)
        for hf in range(2):
            vs = [u_ref[pl.ds(r0 * S5_CHUNK + hf * half_chunk + i, rt, stride=S5_CHUNK), :]
                  for i in range(half_chunk)]
            outs = _block_transpose8(vs)
            for g in range(GROUPS_PER_SLAB):
                x_scr[g, pl.ds(r0, rt), hf * LANES:(hf + 1) * LANES] = outs[g]
        return carry

    lax.fori_loop(0, rows // rt, gather_tile, 0)

    row = lax.broadcasted_iota(jnp.int32, (SUBLANES, half), 0)
    n_tap = S5_CHUNK * SSM_GROUP
    for pi in range(PAIRS_PER_SLAB):
        x0 = x_scr[2 * pi].astype(BF16)
        x1 = x_scr[2 * pi + 1].astype(BF16)
        s_scr[...] = jnp.dot(jnp.concatenate([x0, x1], axis=1), ws_ref[pi], preferred_element_type=F32)
        ap = a_ref[pi]
        h0 = h0_ref[pi]
        if independent:
            hp_scr[...] = h0
            s = s_scr[...]
            n_re, n_im = _cmul(ap[1:2, :half], ap[1:2, half:], h0[:, :half], h0[:, half:])
            hf_ref[pi] = jnp.concatenate([n_re + s[:, :half], n_im + s[:, half:]], axis=1)
        else:
            pw_re, pw_im = ap[0:SUBLANES, :half], ap[0:SUBLANES, half:]

            def scan_tile(t, carry):
                h_re, h_im = carry
                r0 = pl.multiple_of(t * SUBLANES, SUBLANES)
                s = s_scr[pl.ds(r0, SUBLANES), :]
                t_re, t_im = s[:, :half], s[:, half:]
                for d in (1, 2, 4):
                    sh_re = jnp.where(row >= d, pltpu.roll(t_re, d, 0), 0.0)
                    sh_im = jnp.where(row >= d, pltpu.roll(t_im, d, 0), 0.0)
                    m_re, m_im = _cmul(ap[d:d + 1, :half], ap[d:d + 1, half:], sh_re, sh_im)
                    t_re, t_im = t_re + m_re, t_im + m_im
                e_re = jnp.where(row >= 1, pltpu.roll(t_re, 1, 0), 0.0)
                e_im = jnp.where(row >= 1, pltpu.roll(t_im, 1, 0), 0.0)
                c_re, c_im = _cmul(pw_re, pw_im, h_re, h_im)
                hp_scr[pl.ds(r0, SUBLANES), :] = jnp.concatenate([e_re + c_re, e_im + c_im], axis=1)
                o_re, o_im = _cmul(ap[SUBLANES:SUBLANES + 1, :half], ap[SUBLANES:SUBLANES + 1, half:],
                                   h_re, h_im)
                last = SUBLANES - 1
                n_re = jnp.broadcast_to(t_re[last:last + 1], h_re.shape) + o_re
                n_im = jnp.broadcast_to(t_im[last:last + 1], h_im.shape) + o_im
                return n_re, n_im

            init = (jnp.broadcast_to(h0[:, :half], (SUBLANES, half)),
                    jnp.broadcast_to(h0[:, half:], (SUBLANES, half)))
            h_re, h_im = lax.fori_loop(0, rows // SUBLANES, scan_tile, init, unroll=4)
            hf_ref[pi] = jnp.concatenate([h_re[0:1], h_im[0:1]], axis=1)
        yc = lax.dot_general(hp_scr[...].astype(BF16), wo_ref[pi], (((1,), (1,)), ((), ())),
                             preferred_element_type=F32)
        yg_scr[2 * pi] = jnp.dot(x0, m_ref[pi, 0], preferred_element_type=F32) + yc[:, :n_tap]
        yg_scr[2 * pi + 1] = jnp.dot(x1, m_ref[pi, 1], preferred_element_type=F32) + yc[:, n_tap:]

    def scatter_tile(t, carry):
        r0 = pl.multiple_of(t * rt, rt)
        for hf in range(2):
            vs = [yg_scr[g, pl.ds(r0, rt), hf * LANES:(hf + 1) * LANES] for g in range(GROUPS_PER_SLAB)]
            outs = _block_transpose8(vs)
            for i in range(half_chunk):
                y_ref[pl.ds(r0 * S5_CHUNK + hf * half_chunk + i, rt, stride=S5_CHUNK), :] = outs[i]
        return carry

    lax.fori_loop(0, rows // rt, scatter_tile, 0)


def _s5_mixer(u, h0_re, h0_im, prep, bsz, t_len):
    m, wsp, wop, a16 = prep
    n_tap = S5_CHUNK * SSM_GROUP
    st = 4 * SSM_STATE
    independent = t_len == S5_CHUNK
    if independent:
        nblk, rows, hrows = 1, bsz, bsz
    else:
        nblk, rows, hrows = bsz, t_len // S5_CHUNK, 1
    assert t_len % S5_CHUNK == 0 and rows % SUBLANES == 0, (bsz, t_len)
    h0p = jnp.concatenate([h0_re.reshape(bsz, N_PAIRS, 2 * SSM_STATE),
                           h0_im.reshape(bsz, N_PAIRS, 2 * SSM_STATE)], axis=2).astype(F32)
    h0p = h0p.transpose(1, 0, 2)[None] if independent else h0p[:, :, None, :]
    pp = PAIRS_PER_SLAB

    def wspec(*shape):
        return pl.BlockSpec((pp,) + shape, lambda q, b: (q,) + (0,) * len(shape))

    frames = rows * S5_CHUNK
    y, hf = pl.pallas_call(
        functools.partial(_s5_kernel, rows=rows, independent=independent),
        grid=(N_SLABS, nblk),
        in_specs=[pl.BlockSpec((None, frames, LANES), lambda q, b: (q, b, 0)),
                  pl.BlockSpec((None, pp, hrows, st), lambda q, b: (b, q, 0, 0)),
                  wspec(2, n_tap, n_tap), wspec(2 * n_tap, st), wspec(2 * n_tap, st), wspec(2 * SUBLANES, st)],
        out_specs=[pl.BlockSpec((None, frames, LANES), lambda q, b: (q, b, 0)),
                   pl.BlockSpec((None, pp, hrows, st), lambda q, b: (b, q, 0, 0))],
        out_shape=[jax.ShapeDtypeStruct(u.shape, F32),
                   jax.ShapeDtypeStruct((nblk, N_PAIRS, hrows, st), F32)],
        scratch_shapes=[pltpu.VMEM((GROUPS_PER_SLAB, rows, n_tap), F32),
                        pltpu.VMEM((GROUPS_PER_SLAB, rows, n_tap), F32),
                        pltpu.VMEM((rows, st), F32), pltpu.VMEM((rows, st), F32)],
        compiler_params=_params("parallel", "parallel"),
        name="s5_core",
    )(u, h0p, m, wsp, wop, a16)
    hf = hf[0].transpose(1, 0, 2) if independent else hf[:, :, 0, :]
    hf_re = hf[:, :, :2 * SSM_STATE].reshape(bsz, N_GROUPS, SSM_STATE)
    hf_im = hf[:, :, 2 * SSM_STATE:].reshape(bsz, N_GROUPS, SSM_STATE)
    return y, hf_re, hf_im


def _conv_kernel(z_ref, left_ref, w_ref, b_ref, g_ref, beta_ref, y_ref, cn_ref, vbuf, shifted, *, tm):
    @pl.when(pl.program_id(1) == 0)
    def _():
        vbuf[0:CONV_HALO, :] = left_ref[...]

    z = z_ref[...]
    vbuf[CONV_HALO:CONV_HALO + tm, :] = z[:, :D_CONV] * jax.nn.sigmoid(z[:, D_CONV:])
    first = CONV_HALO - (CONV_WIDTH - 1)
    span = tm + CONV_HALO - SUBLANES
    for r in range(1, SUBLANES):
        shifted[r - 1, 0:span, :] = vbuf[r:r + span, :]
    acc = jnp.zeros((tm, D_CONV), F32)
    for k in range(CONV_WIDTH):
        a, r = divmod(first + k, SUBLANES)
        src = vbuf if r == 0 else shifted.at[r - 1]
        acc = acc + w_ref[k:k + 1, :] * src[a * SUBLANES:a * SUBLANES + tm, :]
    y = acc + b_ref[...]
    mu = jnp.mean(y, axis=-1, keepdims=True)
    yc = y - mu
    var = jnp.mean(yc * yc, axis=-1, keepdims=True)
    yn = yc * lax.rsqrt(var + EPS) * g_ref[...] + beta_ref[...]
    y_ref[...] = yn * jax.nn.sigmoid(yn)
    cn_ref[...] = vbuf[tm + first:tm + CONV_HALO, :]
    vbuf[0:CONV_HALO, :] = vbuf[tm:tm + CONV_HALO, :]


def _conv_mixer(z, left, w, b, g, beta, layer, bsz, t_len, tm):
    left = jnp.pad(left.astype(F32), ((0, 0), (CONV_HALO - (CONV_WIDTH - 1), 0), (0, 0)))
    nt = t_len // tm
    vec = _layer_spec(layer, 1, D_CONV)
    return pl.pallas_call(
        functools.partial(_conv_kernel, tm=tm),
        grid=(bsz, nt),
        in_specs=[pl.BlockSpec((tm, 2 * D_CONV), lambda bi, j: (bi * nt + j, 0)),
                  pl.BlockSpec((None, CONV_HALO, D_CONV), lambda bi, j: (bi, 0, 0)),
                  _layer_spec(layer, CONV_WIDTH, D_CONV), vec, vec, vec],
        out_specs=[pl.BlockSpec((tm, D_CONV), lambda bi, j: (bi * nt + j, 0)),
                   pl.BlockSpec((None, CONV_WIDTH - 1, D_CONV), lambda bi, j: (bi, 0, 0))],
        out_shape=[jax.ShapeDtypeStruct((bsz * t_len, D_CONV), F32),
                   jax.ShapeDtypeStruct((bsz, CONV_WIDTH - 1, D_CONV), F32)],
        scratch_shapes=[pltpu.VMEM((tm + CONV_HALO, D_CONV), F32),
                        pltpu.VMEM((SUBLANES - 1, tm + CONV_HALO - SUBLANES, D_CONV), F32)],
        compiler_params=_params("parallel", "arbitrary"),
        name="conv_mixer",
    )(z, left, w, _vec(b), _vec(g), _vec(beta))


def _route(logits):
    lane_i = lax.broadcasted_iota(jnp.int32, logits.shape, 1)
    lane = lane_i.astype(F32)
    group_of_lane = (lane_i >> 3).astype(F32)
    neg = -jnp.inf
    far = float(LANES)
    is_g = (lane_i >= N_EXPERTS) & (lane_i < N_EXPERTS + N_EXPERT_GROUPS)
    gl = jnp.where(is_g, logits, neg)
    g_max = jnp.max(gl, axis=-1, keepdims=True)
    g_lane = jnp.min(jnp.where(gl == g_max, lane, far), axis=-1, keepdims=True)
    g_gate = 1.0 / jnp.sum(jnp.exp(gl - g_max), axis=-1, keepdims=True)
    g_idx = g_lane - float(N_EXPERTS)
    in_group = (lane_i < N_EXPERTS) & (group_of_lane == g_idx)
    el = jnp.where(in_group, logits, neg)
    v1 = jnp.max(el, axis=-1, keepdims=True)
    i1 = jnp.min(jnp.where(el == v1, lane, far), axis=-1, keepdims=True)
    el2 = jnp.where(lane == i1, neg, el)
    v2 = jnp.max(el2, axis=-1, keepdims=True)
    i2 = jnp.min(jnp.where(el2 == v2, lane, far), axis=-1, keepdims=True)
    e2 = jnp.exp(v2 - v1)
    w1 = g_gate / (1.0 + e2)
    w2 = g_gate * e2 / (1.0 + e2)
    return i1, i2, w1, w2


ROUTE_E1, ROUTE_E2, ROUTE_W1, ROUTE_W2, ROUTE_RANK1, ROUTE_RANK2 = range(6)


def _outproj_kernel(x_ref, ys_ref, u_ref, yb_ref, d_ref, wglu_ref, wout_ref, nf_ref, rw_ref, rb_ref,
                    x1_ref, hn_ref, route_ref, rt_ref, cnt_ref, wglu_bf, wout_bf, rw_bf):
    @pl.when(pl.program_id(0) == 0)
    def _():
        cnt_ref[...] = jnp.zeros_like(cnt_ref)
        rw = rw_ref[...]
        rw_hi = rw.astype(BF16)
        rw_bf[:, 0:LANES] = rw_hi
        rw_bf[:, LANES:] = (rw - rw_hi.astype(F32)).astype(BF16)

    _cast_weight_once(wglu_ref, wglu_bf)
    _cast_weight_once(wout_ref, wout_bf)
    ys = jnp.concatenate([ys_ref[q] for q in range(N_SLABS)], axis=1)
    u = jnp.concatenate([u_ref[q] for q in range(N_SLABS)], axis=1)
    z = jax.nn.gelu(ys + d_ref[...] * u)
    ya = z * jax.nn.sigmoid(_bdot(z, wglu_bf[...]))
    mix = _bdot(ya, wout_bf[0:D_SSM, :]) + _bdot(yb_ref[...], wout_bf[D_SSM:, :])
    x1 = x_ref[...] + mix
    x1_ref[...] = x1
    hn = _rms(x1, nf_ref[...])
    hn_ref[...] = hn
    h_hi = hn.astype(BF16)
    h_lo = (hn - h_hi.astype(F32)).astype(BF16)
    hw = jnp.dot(h_hi, rw_bf[...], preferred_element_type=F32)
    logits = (hw[:, :LANES] + hw[:, LANES:]
              + jnp.dot(h_lo, rw_bf[:, 0:LANES], preferred_element_type=F32) + rb_ref[...])
    i1, i2, w1, w2 = _route(logits)
    tm = logits.shape[0]
    lane_i = lax.broadcasted_iota(jnp.int32, logits.shape, 1)
    lane = lane_i.astype(F32)
    picked = jnp.where((lane == i1) | (lane == i2), 1.0, 0.0)
    earlier = (lax.broadcasted_iota(jnp.int32, (tm, tm), 0) > lax.broadcasted_iota(jnp.int32, (tm, tm), 1))
    prefix = _bdot(jnp.where(earlier, 1.0, 0.0), picked.astype(BF16)) + cnt_ref[...]
    rank1 = jnp.sum(jnp.where(lane == i1, prefix, 0.0), axis=-1, keepdims=True)
    rank2 = jnp.sum(jnp.where(lane == i2, prefix, 0.0), axis=-1, keepdims=True)
    cnt_ref[...] += jnp.sum(picked, axis=0, keepdims=True)
    rec = jnp.zeros_like(logits)
    for lane_id, val in ((ROUTE_E1, i1), (ROUTE_E2, i2), (ROUTE_W1, w1), (ROUTE_W2, w2),
                         (ROUTE_RANK1, rank1), (ROUTE_RANK2, rank2)):
        rec = jnp.where(lane_i == lane_id, val, rec)
    route_ref[...] = rec
    rt_ref[...] = rec.T[0:SUBLANES, :]


def _outproj(x, ys, u, yb, d, wglu, wout, nf, rw, rb, layer, tm):
    n = x.shape[0]
    return pl.pallas_call(
        _outproj_kernel,
        grid=(n // tm,),
        in_specs=[_row_spec(tm, D_MODEL), _slab_spec(tm), _slab_spec(tm), _row_spec(tm, D_CONV),
                  _layer_spec(layer, 1, D_SSM), _layer_spec(layer, D_SSM, D_SSM),
                  _layer_spec(layer, D_MODEL, D_MODEL), _layer_spec(layer, 1, D_MODEL),
                  _const_spec((D_MODEL, LANES)), _const_spec((1, LANES))],
        out_specs=[_row_spec(tm, D_MODEL), _row_spec(tm, D_MODEL), _row_spec(tm, LANES),
                   pl.BlockSpec((SUBLANES, tm), lambda i: (0, i)), _const_spec((1, LANES))],
        out_shape=[jax.ShapeDtypeStruct((n, D_MODEL), F32), jax.ShapeDtypeStruct((n, D_MODEL), F32),
                   jax.ShapeDtypeStruct((n, LANES), F32), jax.ShapeDtypeStruct((SUBLANES, n), F32),
                   jax.ShapeDtypeStruct((1, LANES), F32)],
        scratch_shapes=[pltpu.VMEM((D_SSM, D_SSM), BF16), pltpu.VMEM((D_MODEL, D_MODEL), BF16),
                        pltpu.VMEM((D_MODEL, 2 * LANES), BF16)],
        compiler_params=_params("arbitrary"),
        name="outproj_router",
    )(x, ys, u, yb, _vec(d), wglu, wout, _vec(nf), rw, rb)


PLAN_TILE_LANES = 2 * LANES
PLAN_EXPERT, PLAN_ROWS, PLAN_USED = range(3)


def _moe_plan_kernel(rt_ref, cnt_ref, slot_ref, tile_ref, *, tms):
    cnt = cnt_ref[...]
    padded = jnp.ceil(cnt * (1.0 / tms)) * float(tms)
    r = lax.broadcasted_iota(jnp.int32, (LANES, LANES), 0)
    c = lax.broadcasted_iota(jnp.int32, (LANES, LANES), 1)
    ends = jnp.dot(padded, jnp.where(r <= c, 1.0, 0.0), precision=lax.Precision.HIGHEST,
                   preferred_element_type=F32)
    starts = ends - padded
    rt = rt_ref[...]
    e1, e2 = rt[ROUTE_E1:ROUTE_E1 + 1], rt[ROUTE_E2:ROUTE_E2 + 1]
    s1, s2 = rt[ROUTE_RANK1:ROUTE_RANK1 + 1], rt[ROUTE_RANK2:ROUTE_RANK2 + 1]
    tile = lax.broadcasted_iota(jnp.int32, (1, PLAN_TILE_LANES), 1).astype(F32)
    used = ends[:, N_EXPERTS - 1:N_EXPERTS] * (1.0 / tms)
    pos = jnp.minimum(tile, used - 1.0) * float(tms)
    t_exp = jnp.zeros_like(tile)
    t_fill = jnp.zeros_like(tile)
    for e in range(N_EXPERTS):
        st, en = starts[:, e:e + 1], ends[:, e:e + 1]
        s1 = s1 + jnp.where(e1 == float(e), st, 0.0)
        s2 = s2 + jnp.where(e2 == float(e), st, 0.0)
        mine = (pos >= st) & (pos < en)
        t_exp = t_exp + jnp.where(mine, float(e), 0.0)
        t_fill = t_fill + jnp.where(mine, st + cnt[:, e:e + 1], 0.0)
    t_rows = jnp.where(tile < used, jnp.clip(t_fill - pos, 0.0, float(tms)), 0.0)
    slot_ref[...] = jnp.concatenate([s1, s2], axis=0).astype(jnp.int32)
    tile_ref[...] = jnp.concatenate(
        [t_exp, t_rows, jnp.broadcast_to(used, tile.shape), jnp.zeros((SUBLANES - 3, PLAN_TILE_LANES), F32)],
        axis=0).astype(jnp.int32)


def _moe_plan(route_t, counts, tms, n_tiles):
    n = route_t.shape[1]
    assert n_tiles <= PLAN_TILE_LANES
    slots, tiles = pl.pallas_call(
        functools.partial(_moe_plan_kernel, tms=tms),
        out_shape=[jax.ShapeDtypeStruct((2, n), jnp.int32),
                   jax.ShapeDtypeStruct((SUBLANES, PLAN_TILE_LANES), jnp.int32)],
        compiler_params=pltpu.CompilerParams(vmem_limit_bytes=VMEM_LIMIT),
        name="moe_plan",
    )(route_t, counts)
    return slots, tiles[PLAN_EXPERT, :n_tiles], tiles[PLAN_ROWS, :n_tiles], tiles[PLAN_USED, :1]


DMA_UNROLL = 8


def _dispatch_kernel(tr_ref, slot_ref, hn_ref, xs_ref, zbuf, sem, zsem, *, tm, tms, n_tiles):
    @pl.when(pl.program_id(0) == 0)
    def _():
        zbuf[...] = jnp.zeros_like(zbuf)

        def fill(t, carry):
            @pl.when(tr_ref[t] < tms)
            def _():
                pltpu.make_async_copy(zbuf, xs_ref.at[pl.ds(pl.multiple_of(t * tms, tms), tms)], zsem).start()
            return carry

        def drain(t, carry):
            @pl.when(tr_ref[t] < tms)
            def _():
                pltpu.make_async_copy(zbuf, xs_ref.at[pl.ds(0, tms)], zsem).wait()
            return carry

        lax.fori_loop(0, n_tiles, fill, 0)
        lax.fori_loop(0, n_tiles, drain, 0)

    def issue(r, carry):
        for k in range(2):
            pltpu.make_async_copy(hn_ref.at[pl.ds(r, 1)], xs_ref.at[pl.ds(slot_ref[k, r], 1)],
                                  sem).start(priority=k)
        return carry

    lax.fori_loop(0, tm, issue, 0, unroll=DMA_UNROLL)
    for k in range(2):
        pltpu.make_async_copy(hn_ref, xs_ref.at[pl.ds(0, tm)], sem).wait()


def _dispatch(hn, slots, tile_rows, tms, tm):
    n = hn.shape[0]
    n_tiles = tile_rows.shape[0]
    return pl.pallas_call(
        functools.partial(_dispatch_kernel, tm=tm, tms=tms, n_tiles=n_tiles),
        grid_spec=pltpu.PrefetchScalarGridSpec(
            num_scalar_prefetch=1, grid=(n // tm,),
            in_specs=[pl.BlockSpec((2, tm), lambda i, tr: (0, i), memory_space=pltpu.SMEM),
                      pl.BlockSpec((tm, D_MODEL), lambda i, tr: (i, 0))],
            out_specs=pl.BlockSpec(memory_space=pl.ANY),
            scratch_shapes=[pltpu.VMEM((tms, D_MODEL), F32), pltpu.SemaphoreType.DMA,
                            pltpu.SemaphoreType.DMA]),
        out_shape=jax.ShapeDtypeStruct((n_tiles * tms, D_MODEL), F32),
        compiler_params=_params("arbitrary"),
        name="moe_dispatch",
    )(tile_rows, slots, hn)


def _moe_kernel(te_ref, nu_ref, x_ref, wg_ref, wu_ref, wd_ref, y_ref, wg_bf, wu_bf, wd_bf):
    i = pl.program_id(0)
    in_use = i < nu_ref[0]
    new_expert = (i == 0) | (te_ref[i] != te_ref[jnp.maximum(i - 1, 0)])

    @pl.when(in_use & new_expert)
    def _():
        wg_bf[...] = wg_ref[...].astype(BF16)
        wu_bf[...] = wu_ref[...].astype(BF16)
        wd_bf[...] = wd_ref[...].astype(BF16)

    @pl.when(in_use)
    def _():
        h = x_ref[...].astype(BF16)
        hg = jnp.dot(h, wg_bf[...], preferred_element_type=F32)
        hu = jnp.dot(h, wu_bf[...], preferred_element_type=F32)
        y_ref[...] = _bdot(hg * jax.nn.sigmoid(hg) * hu, wd_bf[...])

    @pl.when(jnp.logical_not(in_use))
    def _():
        y_ref[...] = jnp.zeros_like(y_ref)


def _moe(xs, tile_expert, n_used, wg, wu, wd, layer, tms):
    n_slots = xs.shape[0]
    rows = pl.BlockSpec((tms, D_MODEL), lambda i, te, nu: (jnp.minimum(i, nu[0] - 1), 0))
    out_rows = pl.BlockSpec((tms, D_MODEL), lambda i, te, nu: (i, 0))

    def wspec(a, b):
        return pl.BlockSpec((None, None, a, b), lambda i, te, nu: (layer, te[i], 0, 0))

    return pl.pallas_call(
        _moe_kernel,
        grid_spec=pltpu.PrefetchScalarGridSpec(
            num_scalar_prefetch=2, grid=(n_slots // tms,),
            in_specs=[rows, wspec(D_MODEL, D_EXPERT), wspec(D_MODEL, D_EXPERT), wspec(D_EXPERT, D_MODEL)],
            out_specs=out_rows,
            scratch_shapes=[pltpu.VMEM((D_MODEL, D_EXPERT), BF16), pltpu.VMEM((D_MODEL, D_EXPERT), BF16),
                            pltpu.VMEM((D_EXPERT, D_MODEL), BF16)]),
        out_shape=jax.ShapeDtypeStruct((n_slots, D_MODEL), F32),
        compiler_params=_params("arbitrary"),
        name="moe",
    )(tile_expert, n_used, xs, wg, wu, wd)


def _ple_kernel(slot_ref, x_ref, route_ref, p_ref, np_ref, wple_ref, wgate_ref, nfin_ref, ys_ref,
                o_ref, ybuf, sem, wple_bf, wgate_bf, *, tm, final):
    _cast_weight_once(wple_ref, wple_bf)
    _cast_weight_once(wgate_ref, wgate_bf)

    def issue(r, carry):
        for k in range(2):
            pltpu.make_async_copy(ys_ref.at[pl.ds(slot_ref[k, r], 1)], ybuf.at[k, pl.ds(r, 1)],
                                  sem).start(priority=k)
        return carry

    lax.fori_loop(0, tm, issue, 0, unroll=DMA_UNROLL)
    pe = _bdot(p_ref[...], wple_bf[...])
    for k in range(2):
        pltpu.make_async_copy(ys_ref.at[pl.ds(0, tm)], ybuf.at[k], sem).wait()
    route = route_ref[...]
    x = (x_ref[...] + route[:, ROUTE_W1:ROUTE_W1 + 1] * ybuf[0]
         + route[:, ROUTE_W2:ROUTE_W2 + 1] * ybuf[1])
    gate = jax.nn.sigmoid(_bdot(_rms(x, np_ref[...]), wgate_bf[...]))
    out = x + pe * gate
    if final:
        out = _rms(out, nfin_ref[...])
    o_ref[...] = out


def _ple(x, route, slots, ys, p, npl, wple, wgate, nfin, layer, tm, final):
    n = x.shape[0]
    return pl.pallas_call(
        functools.partial(_ple_kernel, tm=tm, final=final),
        grid=(n // tm,),
        in_specs=[pl.BlockSpec((2, tm), lambda i: (0, i), memory_space=pltpu.SMEM),
                  _row_spec(tm, D_MODEL), _row_spec(tm, LANES),
                  pl.BlockSpec((None, tm, D_PLE), lambda i: (layer, i, 0)),
                  _layer_spec(layer, 1, D_MODEL), _layer_spec(layer, D_PLE, D_MODEL),
                  _layer_spec(layer, D_MODEL, D_MODEL), _const_spec((1, D_MODEL)),
                  pl.BlockSpec(memory_space=pl.ANY)],
        out_specs=_row_spec(tm, D_MODEL),
        out_shape=jax.ShapeDtypeStruct((n, D_MODEL), F32),
        scratch_shapes=[pltpu.VMEM((2, tm, D_MODEL), F32), pltpu.SemaphoreType.DMA,
                        pltpu.VMEM((D_PLE, D_MODEL), BF16), pltpu.VMEM((D_MODEL, D_MODEL), BF16)],
        compiler_params=_params("arbitrary"),
        name="combine_ple",
    )(slots, x, route, p, _vec(npl), wple, wgate, nfin.reshape(1, D_MODEL), ys)


def _layer(x, p, h0_re, h0_im, conv_left, w, s5, router_w, router_b, layer, bsz, t_len, tm, tm_conv, tms, final):
    n = bsz * t_len
    u, z = _inproj(x, w["norm_mix"], w["w_in"], layer, tm)
    ys, hf_re, hf_im = _s5_mixer(u, h0_re, h0_im, s5, bsz, t_len)
    yb, conv_new = _conv_mixer(z, conv_left, w["conv_w"], w["conv_b"], w["conv_ln_g"], w["conv_ln_b"],
                               layer, bsz, t_len, tm_conv)
    x1, hn, route, route_t, counts = _outproj(x, ys, u, yb, w["ssm_d"], w["w_ssm_glu"], w["w_out"],
                                              w["norm_ffn"], router_w, router_b, layer, tm)
    n_tiles = 2 * n // tms + N_EXPERTS
    slots, tile_expert, tile_rows, n_used = _moe_plan(route_t, counts, tms, n_tiles)
    xs = _dispatch(hn, slots, tile_rows, tms, tm)
    ysort = _moe(xs, tile_expert, n_used, w["expert_w_gate"], w["expert_w_up"], w["expert_w_down"],
                 layer, tms)
    x3 = _ple(x1, route, slots, ysort, p, w["norm_ple"], w["ple_w"], w["ple_gate_w"], w["norm_final"],
              layer, tm, final)
    return x3, hf_re, hf_im, conv_new


def kernel(x_prompt, x_sample, p_prompt, p_sample, state_ssm_re, state_ssm_im, cache_conv, norm_mix, w_in, ssm_a_re, ssm_a_im, ssm_b_re, ssm_b_im, ssm_c_re, ssm_c_im, ssm_d, ssm_log_dt, w_ssm_glu, conv_w, conv_b, conv_ln_g, conv_ln_b, w_out, norm_ffn, router_group_w, router_group_b, router_expert_w, router_expert_b, expert_w_gate, expert_w_up, expert_w_down, norm_ple, ple_w, ple_gate_w, norm_final):
    depth = w_in.shape[0]
    bp, tp, _ = x_prompt.shape
    bs, ts, _ = x_sample.shape
    xp = x_prompt.reshape(bp * tp, D_MODEL)
    xs = x_sample.reshape(bs * ts, D_MODEL)
    pp = p_prompt.reshape(depth, bp * tp, D_PLE)
    ps = p_sample.reshape(depth, bs * ts, D_PLE)
    zero_state = jnp.zeros((bp, N_GROUPS, SSM_STATE), F32)
    zero_conv = jnp.zeros((bp, CONV_WIDTH - 1, D_CONV), F32)
    pad_lanes = LANES - N_EXPERTS - N_EXPERT_GROUPS
    w = {"norm_mix": norm_mix, "w_in": w_in, "ssm_d": ssm_d, "w_ssm_glu": w_ssm_glu, "conv_w": conv_w,
         "conv_b": conv_b, "conv_ln_g": conv_ln_g, "conv_ln_b": conv_ln_b, "w_out": w_out,
         "norm_ffn": norm_ffn, "expert_w_gate": expert_w_gate, "expert_w_up": expert_w_up,
         "expert_w_down": expert_w_down, "norm_ple": norm_ple, "ple_w": ple_w, "ple_gate_w": ple_gate_w,
         "norm_final": norm_final}
    outs = {k: [] for k in ("pr_re", "pr_im", "pr_conv", "sm_re", "sm_im", "sm_conv")}
    for i in range(depth):
        s5 = _s5_prep(ssm_a_re[i], ssm_a_im[i], ssm_log_dt[i], ssm_b_re[i], ssm_b_im[i], ssm_c_re[i],
                      ssm_c_im[i])
        router_w = jnp.pad(jnp.concatenate([router_expert_w[i], router_group_w[i]], axis=1),
                           ((0, 0), (0, pad_lanes)))
        router_b = jnp.pad(jnp.concatenate([router_expert_b[i], router_group_b[i]]),
                           (0, pad_lanes)).reshape(1, LANES)
        final = i == depth - 1
        xp, hr, hi, cv = _layer(xp, pp, zero_state, zero_state, zero_conv, w, s5, router_w, router_b, i,
                                bp, tp, tm=512, tm_conv=512, tms=256, final=final)
        outs["pr_re"].append(hr); outs["pr_im"].append(hi); outs["pr_conv"].append(cv)
        xs, hr, hi, cv = _layer(xs, ps, state_ssm_re[i], state_ssm_im[i], cache_conv[i], w, s5, router_w,
                                router_b, i, bs, ts, tm=bs * ts, tm_conv=ts, tms=32, final=final)
        outs["sm_re"].append(hr); outs["sm_im"].append(hi); outs["sm_conv"].append(cv)
    return (xp.reshape(bp, tp, D_MODEL), xs.reshape(bs, ts, D_MODEL),
            jnp.stack(outs["pr_re"]), jnp.stack(outs["pr_im"]), jnp.stack(outs["pr_conv"]),
            jnp.stack(outs["sm_re"]), jnp.stack(outs["sm_im"]), jnp.stack(outs["sm_conv"]))
```

```python
import functools

import jax
import jax.numpy as jnp
from jax import lax
from jax.experimental import pallas as pl
from jax.experimental.pallas import tpu as pltpu

F32 = jnp.float32
BF16 = jnp.bfloat16

D_MODEL = 1024
D_SSM = 512
SSM_GROUP = 16
N_GROUPS = D_SSM // SSM_GROUP
N_PAIRS = N_GROUPS // 2
SSM_STATE = 64
D_CONV = 512
CONV_WIDTH = 31
CONV_HALO = 32
N_EXPERT_GROUPS = 4
EXPERTS_PER_GROUP = 8
N_EXPERTS = 32
D_EXPERT = 256
D_PLE = 256
EPS = 1e-6
S5_CHUNK = 16
LANES = 128
SUBLANES = 8
N_SLABS = D_SSM // LANES
GROUPS_PER_SLAB = LANES // SSM_GROUP
PAIRS_PER_SLAB = GROUPS_PER_SLAB // 2
VMEM_LIMIT = 56 * 1024 * 1024


def _params(*sem):
    return pltpu.CompilerParams(dimension_semantics=sem, vmem_limit_bytes=VMEM_LIMIT)


def _rms(x, g):
    return x * lax.rsqrt(jnp.mean(x * x, axis=-1, keepdims=True) + EPS) * g


def _bdot(a, b):
    return jnp.dot(a.astype(BF16), b, preferred_element_type=F32)


def _row_spec(tm, width):
    return pl.BlockSpec((tm, width), lambda i: (i, 0))


def _slab_spec(tm):
    return pl.BlockSpec((N_SLABS, tm, LANES), lambda i: (0, i, 0))


def _const_spec(shape):
    return pl.BlockSpec(shape, lambda i: (0,) * len(shape))


def _layer_spec(layer, *shape):
    return pl.BlockSpec((None,) + shape, lambda *_: (layer,) + (0,) * len(shape))


def _vec(stacked):
    return stacked.reshape(stacked.shape[0], 1, stacked.shape[1])


def _cast_weight_once(w_ref, wbf_ref):
    @pl.when(pl.program_id(0) == 0)
    def _():
        wbf_ref[...] = w_ref[...].astype(BF16)


def _inproj_kernel(x_ref, g_ref, w_ref, u_ref, z_ref, wbf):
    _cast_weight_once(w_ref, wbf)
    hn = _rms(x_ref[...], g_ref[...])
    proj = _bdot(hn, wbf[...])
    for q in range(N_SLABS):
        u_ref[q] = proj[:, q * LANES:(q + 1) * LANES]
    z_ref[...] = proj[:, D_SSM:]


def _inproj(x, g, w, layer, tm):
    n = x.shape[0]
    d_in = w.shape[2]
    return pl.pallas_call(
        _inproj_kernel,
        grid=(n // tm,),
        in_specs=[_row_spec(tm, D_MODEL), _layer_spec(layer, 1, D_MODEL), _layer_spec(layer, D_MODEL, d_in)],
        out_specs=[_slab_spec(tm), _row_spec(tm, d_in - D_SSM)],
        out_shape=[jax.ShapeDtypeStruct((N_SLABS, n, LANES), F32),
                   jax.ShapeDtypeStruct((n, d_in - D_SSM), F32)],
        scratch_shapes=[pltpu.VMEM((D_MODEL, d_in), BF16)],
        compiler_params=_params("arbitrary"),
        name="inproj",
    )(x, _vec(g), w)


def _s5_prep_kernel(ar_ref, ai_ref, ldt_ref, bre_ref, bim_ref, cre_ref, cim_ref,
                    m_ref, ws_ref, wot_ref, atab_ref, wt_re, wt_im, br_re, br_im):
    n_tap = S5_CHUNK * SSM_GROUP
    st = 4 * SSM_STATE
    nt = (((1,), (1,)), ((), ()))
    hi = lax.Precision.HIGHEST
    src = lax.broadcasted_iota(jnp.int32, (SSM_STATE, st), 0)
    dst = lax.broadcasted_iota(jnp.int32, (SSM_STATE, st), 1)
    lane = lax.broadcasted_iota(jnp.int32, (SSM_GROUP, n_tap), 1)
    ws_rows, wot_rows, atab = [], [], jnp.zeros((2 * SUBLANES, st), F32)
    for gi in range(2):
        ar, ai = ar_ref[gi], ai_ref[gi]
        dt = jnp.exp(ldt_ref[gi])
        k = lax.broadcasted_iota(jnp.int32, (S5_CHUNK + SUBLANES, SSM_STATE), 0).astype(F32)
        mag = jnp.exp(k * (dt * ar))
        ang = k * (dt * ai)
        p_re, p_im = mag * jnp.cos(ang), mag * jnp.sin(ang)
        inv = 1.0 / (ar * ar + ai * ai)
        ab_re, ab_im = p_re[1:2], p_im[1:2]
        ia_re, ia_im = ar * inv, -ai * inv
        coef_re = (ab_re - 1.0) * ia_re - ab_im * ia_im
        coef_im = (ab_re - 1.0) * ia_im + ab_im * ia_re
        bre, bim = bre_ref[gi], bim_ref[gi]
        bb_re = coef_re * bre - coef_im * bim
        bb_im = coef_re * bim + coef_im * bre
        cre, cim = cre_ref[gi], cim_ref[gi]
        for kk in range(S5_CHUNK + 1):
            pr, pi = p_re[kk:kk + 1], p_im[kk:kk + 1]
            rows = slice(kk * SSM_GROUP, (kk + 1) * SSM_GROUP)
            wt_re[rows, :] = pr * cre - pi * cim
            wt_im[rows, :] = -pi * cre - pr * cim
            if kk < S5_CHUNK:
                back = slice((S5_CHUNK - 1 - kk) * SSM_GROUP, (S5_CHUNK - kk) * SSM_GROUP)
                br_re[back, :] = pr * bb_re - pi * bb_im
                br_im[back, :] = pi * bb_re + pr * bb_im
        kcat = (lax.dot_general(bb_re, wt_re[0:n_tap, :], nt, precision=hi, preferred_element_type=F32)
                + lax.dot_general(bb_im, wt_im[0:n_tap, :], nt, precision=hi, preferred_element_type=F32))
        for s in range(S5_CHUNK):
            shifted = kcat if s == 0 else pltpu.roll(kcat, s * SSM_GROUP, 1)
            m_ref[gi, s * SSM_GROUP:(s + 1) * SSM_GROUP, :] = jnp.where(
                lane >= s * SSM_GROUP, shifted, 0.0).astype(BF16)
        put_re = jnp.where(dst == src + gi * SSM_STATE, 1.0, 0.0)
        put_im = jnp.where(dst == src + (2 + gi) * SSM_STATE, 1.0, 0.0)

        def place(v_re, v_im):
            return (jnp.dot(v_re, put_re, precision=hi, preferred_element_type=F32)
                    + jnp.dot(v_im, put_im, precision=hi, preferred_element_type=F32))

        ws_rows.append(place(br_re[...], br_im[...]))
        wot_rows.append(place(wt_re[SSM_GROUP:, :], wt_im[SSM_GROUP:, :]))
        kc = float(S5_CHUNK) * lax.broadcasted_iota(jnp.int32, (2 * SUBLANES, SSM_STATE), 0).astype(F32)
        magc = jnp.exp(kc * (dt * ar))
        angc = kc * (dt * ai)
        atab = atab + place(magc * jnp.cos(angc), magc * jnp.sin(angc))
    ws_ref[...] = jnp.concatenate(ws_rows, axis=0).astype(BF16)
    wot_ref[...] = jnp.concatenate(wot_rows, axis=0).astype(BF16)
    atab_ref[...] = atab


def _s5_prep(a_re, a_im, log_dt, b_re, b_im, c_re, c_im):
    p, n, c = N_PAIRS, SSM_STATE, SSM_GROUP
    n_tap = S5_CHUNK * c
    st = 4 * n

    def pspec(*shape):
        return pl.BlockSpec((None,) + shape, lambda i: (i,) + (0,) * len(shape))

    def pairs(a, *shape):
        return a.reshape((p, 2) + shape)

    return pl.pallas_call(
        _s5_prep_kernel,
        grid=(p,),
        in_specs=[pspec(2, 1, n), pspec(2, 1, n), pspec(2, 1, 1), pspec(2, c, n), pspec(2, c, n),
                  pspec(2, c, n), pspec(2, c, n)],
        out_specs=[pspec(2, n_tap, n_tap), pspec(2 * n_tap, st), pspec(2 * n_tap, st), pspec(2 * SUBLANES, st)],
        out_shape=[jax.ShapeDtypeStruct((p, 2, n_tap, n_tap), BF16),
                   jax.ShapeDtypeStruct((p, 2 * n_tap, st), BF16),
                   jax.ShapeDtypeStruct((p, 2 * n_tap, st), BF16),
                   jax.ShapeDtypeStruct((p, 2 * SUBLANES, st), F32)],
        scratch_shapes=[pltpu.VMEM((n_tap + c, n), F32), pltpu.VMEM((n_tap + c, n), F32),
                        pltpu.VMEM((n_tap, n), F32), pltpu.VMEM((n_tap, n), F32)],
        compiler_params=_params("parallel"),
        name="s5_prep",
    )(pairs(a_re, 1, n), pairs(a_im, 1, n), pairs(log_dt, 1, 1),
      pairs(jnp.swapaxes(b_re, 1, 2), c, n), pairs(jnp.swapaxes(b_im, 1, 2), c, n),
      pairs(c_re, c, n), pairs(c_im, c, n))


def _block_transpose8(vs):
    lane = lax.broadcasted_iota(jnp.int32, vs[0].shape, 1)
    blk = lane >> 4
    for d in (4, 2, 1):
        keep = (blk & d) == 0
        new = list(vs)
        for i in range(GROUPS_PER_SLAB):
            if i & d == 0:
                a, b = vs[i], vs[i + d]
                new[i] = jnp.where(keep, a, pltpu.roll(b, d * SSM_GROUP, 1))
                new[i + d] = jnp.where(keep, pltpu.roll(a, LANES - d * SSM_GROUP, 1), b)
        vs = new
    return vs


def _cmul(ar, ai, xr, xi):
    return ar * xr - ai * xi, ar * xi + ai * xr


def _s5_kernel(u_ref, h0_ref, m_ref, ws_ref, wo_ref, a_ref, y_ref, hf_ref, x_scr, yg_scr, s_scr, hp_scr,
               *, rows, independent):
    half = 2 * SSM_STATE
    rt = min(rows, 16 * SUBLANES)
    half_chunk = S5_CHUNK // 2

    def gather_tile(t, carry):
        r0 = pl.multiple_of(t * rt, rt)
        for hf in range(2):
            vs = [u_ref[pl.ds(r0 * S5_CHUNK + hf * half_chunk + i, rt, stride=S5_CHUNK), :]
                  for i in range(half_chunk)]
            outs = _block_transpose8(vs)
            for g in range(GROUPS_PER_SLAB):
                x_scr[g, pl.ds(r0, rt), hf * LANES:(hf + 1) * LANES] = outs[g]
        return carry

    lax.fori_loop(0, rows // rt, gather_tile, 0)

    row = lax.broadcasted_iota(jnp.int32, (SUBLANES, half), 0)
    n_tap = S5_CHUNK * SSM_GROUP
    for pi in range(PAIRS_PER_SLAB):
        x0 = x_scr[2 * pi].astype(BF16)
        x1 = x_scr[2 * pi + 1].astype(BF16)
        s_scr[...] = jnp.dot(jnp.concatenate([x0, x1], axis=1), ws_ref[pi], preferred_element_type=F32)
        ap = a_ref[pi]
        h0 = h0_ref[pi]
        if independent:
            hp_scr[...] = h0
            s = s_scr[...]
            n_re, n_im = _cmul(ap[1:2, :half], ap[1:2, half:], h0[:, :half], h0[:, half:])
            hf_ref[pi] = jnp.concatenate([n_re + s[:, :half], n_im + s[:, half:]], axis=1)
        else:
            pw_re, pw_im = ap[0:SUBLANES, :half], ap[0:SUBLANES, half:]

            def scan_tile(t, carry):
                h_re, h_im = carry
                r0 = pl.multiple_of(t * SUBLANES, SUBLANES)
                s = s_scr[pl.ds(r0, SUBLANES), :]
                t_re, t_im = s[:, :half], s[:, half:]
                for d in (1, 2, 4):
                    sh_re = jnp.where(row >= d, pltpu.roll(t_re, d, 0), 0.0)
                    sh_im = jnp.where(row >= d, pltpu.roll(t_im, d, 0), 0.0)
                    m_re, m_im = _cmul(ap[d:d + 1, :half], ap[d:d + 1, half:], sh_re, sh_im)
                    t_re, t_im = t_re + m_re, t_im + m_im
                e_re = jnp.where(row >= 1, pltpu.roll(t_re, 1, 0), 0.0)
                e_im = jnp.where(row >= 1, pltpu.roll(t_im, 1, 0), 0.0)
                c_re, c_im = _cmul(pw_re, pw_im, h_re, h_im)
                hp_scr[pl.ds(r0, SUBLANES), :] = jnp.concatenate([e_re + c_re, e_im + c_im], axis=1)
                o_re, o_im = _cmul(ap[SUBLANES:SUBLANES + 1, :half], ap[SUBLANES:SUBLANES + 1, half:],
                                   h_re, h_im)
                last = SUBLANES - 1
                n_re = jnp.broadcast_to(t_re[last:last + 1], h_re.shape) + o_re
                n_im = jnp.broadcast_to(t_im[last:last + 1], h_im.shape) + o_im
                return n_re, n_im

            init = (jnp.broadcast_to(h0[:, :half], (SUBLANES, half)),
                    jnp.broadcast_to(h0[:, half:], (SUBLANES, half)))
            h_re, h_im = lax.fori_loop(0, rows // SUBLANES, scan_tile, init, unroll=4)
            hf_ref[pi] = jnp.concatenate([h_re[0:1], h_im[0:1]], axis=1)
        yc = lax.dot_general(hp_scr[...].astype(BF16), wo_ref[pi], (((1,), (1,)), ((), ())),
                             preferred_element_type=F32)
        yg_scr[2 * pi] = jnp.dot(x0, m_ref[pi, 0], preferred_element_type=F32) + yc[:, :n_tap]
        yg_scr[2 * pi + 1] = jnp.dot(x1, m_ref[pi, 1], preferred_element_type=F32) + yc[:, n_tap:]

    def scatter_tile(t, carry):
        r0 = pl.multiple_of(t * rt, rt)
        for hf in range(2):
            vs = [yg_scr[g, pl.ds(r0, rt), hf * LANES:(hf + 1) * LANES] for g in range(GROUPS_PER_SLAB)]
            outs = _block_transpose8(vs)
            for i in range(half_chunk):
                y_ref[pl.ds(r0 * S5_CHUNK + hf * half_chunk + i, rt, stride=S5_CHUNK), :] = outs[i]
        return carry

    lax.fori_loop(0, rows // rt, scatter_tile, 0)


def _s5_mixer(u, h0_re, h0_im, prep, bsz, t_len):
    m, wsp, wop, a16 = prep
    n_tap = S5_CHUNK * SSM_GROUP
    st = 4 * SSM_STATE
    independent = t_len == S5_CHUNK
    if independent:
        nblk, rows, hrows = 1, bsz, bsz
    else:
        nblk, rows, hrows = bsz, t_len // S5_CHUNK, 1
    assert t_len % S5_CHUNK == 0 and rows % SUBLANES == 0, (bsz, t_len)
    h0p = jnp.concatenate([h0_re.reshape(bsz, N_PAIRS, 2 * SSM_STATE),
                           h0_im.reshape(bsz, N_PAIRS, 2 * SSM_STATE)], axis=2).astype(F32)
    h0p = h0p.transpose(1, 0, 2)[None] if independent else h0p[:, :, None, :]
    pp = PAIRS_PER_SLAB

    def wspec(*shape):
        return pl.BlockSpec((pp,) + shape, lambda q, b: (q,) + (0,) * len(shape))

    frames = rows * S5_CHUNK
    y, hf = pl.pallas_call(
        functools.partial(_s5_kernel, rows=rows, independent=independent),
        grid=(N_SLABS, nblk),
        in_specs=[pl.BlockSpec((None, frames, LANES), lambda q, b: (q, b, 0)),
                  pl.BlockSpec((None, pp, hrows, st), lambda q, b: (b, q, 0, 0)),
                  wspec(2, n_tap, n_tap), wspec(2 * n_tap, st), wspec(2 * n_tap, st), wspec(2 * SUBLANES, st)],
        out_specs=[pl.BlockSpec((None, frames, LANES), lambda q, b: (q, b, 0)),
                   pl.BlockSpec((None, pp, hrows, st), lambda q, b: (b, q, 0, 0))],
        out_shape=[jax.ShapeDtypeStruct(u.shape, F32),
                   jax.ShapeDtypeStruct((nblk, N_PAIRS, hrows, st), F32)],
        scratch_shapes=[pltpu.VMEM((GROUPS_PER_SLAB, rows, n_tap), F32),
                        pltpu.VMEM((GROUPS_PER_SLAB, rows, n_tap), F32),
                        pltpu.VMEM((rows, st), F32), pltpu.VMEM((rows, st), F32)],
        compiler_params=_params("parallel", "parallel"),
        name="s5_core",
    )(u, h0p, m, wsp, wop, a16)
    hf = hf[0].transpose(1, 0, 2) if independent else hf[:, :, 0, :]
    hf_re = hf[:, :, :2 * SSM_STATE].reshape(bsz, N_GROUPS, SSM_STATE)
    hf_im = hf[:, :, 2 * SSM_STATE:].reshape(bsz, N_GROUPS, SSM_STATE)
    return y, hf_re, hf_im


def _conv_kernel(z_ref, left_ref, w_ref, b_ref, g_ref, beta_ref, y_ref, cn_ref, vbuf, shifted, *, tm):
    @pl.when(pl.program_id(1) == 0)
    def _():
        vbuf[0:CONV_HALO, :] = left_ref[...]

    z = z_ref[...]
    vbuf[CONV_HALO:CONV_HALO + tm, :] = z[:, :D_CONV] * jax.nn.sigmoid(z[:, D_CONV:])
    first = CONV_HALO - (CONV_WIDTH - 1)
    span = tm + CONV_HALO - SUBLANES
    for r in range(1, SUBLANES):
        shifted[r - 1, 0:span, :] = vbuf[r:r + span, :]
    acc = jnp.zeros((tm, D_CONV), F32)
    for k in range(CONV_WIDTH):
        a, r = divmod(first + k, SUBLANES)
        src = vbuf if r == 0 else shifted.at[r - 1]
        acc = acc + w_ref[k:k + 1, :] * src[a * SUBLANES:a * SUBLANES + tm, :]
    y = acc + b_ref[...]
    mu = jnp.mean(y, axis=-1, keepdims=True)
    yc = y - mu
    var = jnp.mean(yc * yc, axis=-1, keepdims=True)
    yn = yc * lax.rsqrt(var + EPS) * g_ref[...] + beta_ref[...]
    y_ref[...] = yn * jax.nn.sigmoid(yn)
    cn_ref[...] = vbuf[tm + first:tm + CONV_HALO, :]
    vbuf[0:CONV_HALO, :] = vbuf[tm:tm + CONV_HALO, :]


def _conv_mixer(z, left, w, b, g, beta, layer, bsz, t_len, tm):
    left = jnp.pad(left.astype(F32), ((0, 0), (CONV_HALO - (CONV_WIDTH - 1), 0), (0, 0)))
    nt = t_len // tm
    vec = _layer_spec(layer, 1, D_CONV)
    return pl.pallas_call(
        functools.partial(_conv_kernel, tm=tm),
        grid=(bsz, nt),
        in_specs=[pl.BlockSpec((tm, 2 * D_CONV), lambda bi, j: (bi * nt + j, 0)),
                  pl.BlockSpec((None, CONV_HALO, D_CONV), lambda bi, j: (bi, 0, 0)),
                  _layer_spec(layer, CONV_WIDTH, D_CONV), vec, vec, vec],
        out_specs=[pl.BlockSpec((tm, D_CONV), lambda bi, j: (bi * nt + j, 0)),
                   pl.BlockSpec((None, CONV_WIDTH - 1, D_CONV), lambda bi, j: (bi, 0, 0))],
        out_shape=[jax.ShapeDtypeStruct((bsz * t_len, D_CONV), F32),
                   jax.ShapeDtypeStruct((bsz, CONV_WIDTH - 1, D_CONV), F32)],
        scratch_shapes=[pltpu.VMEM((tm + CONV_HALO, D_CONV), F32),
                        pltpu.VMEM((SUBLANES - 1, tm + CONV_HALO - SUBLANES, D_CONV), F32)],
        compiler_params=_params("parallel", "arbitrary"),
        name="conv_mixer",
    )(z, left, w, _vec(b), _vec(g), _vec(beta))


def _route(logits):
    lane_i = lax.broadcasted_iota(jnp.int32, logits.shape, 1)
    lane = lane_i.astype(F32)
    group_of_lane = (lane_i >> 3).astype(F32)
    neg = -jnp.inf
    far = float(LANES)
    is_g = (lane_i >= N_EXPERTS) & (lane_i < N_EXPERTS + N_EXPERT_GROUPS)
    gl = jnp.where(is_g, logits, neg)
    g_max = jnp.max(gl, axis=-1, keepdims=True)
    g_lane = jnp.min(jnp.where(gl == g_max, lane, far), axis=-1, keepdims=True)
    g_gate = 1.0 / jnp.sum(jnp.exp(gl - g_max), axis=-1, keepdims=True)
    g_idx = g_lane - float(N_EXPERTS)
    in_group = (lane_i < N_EXPERTS) & (group_of_lane == g_idx)
    el = jnp.where(in_group, logits, neg)
    v1 = jnp.max(el, axis=-1, keepdims=True)
    i1 = jnp.min(jnp.where(el == v1, lane, far), axis=-1, keepdims=True)
    el2 = jnp.where(lane == i1, neg, el)
    v2 = jnp.max(el2, axis=-1, keepdims=True)
    i2 = jnp.min(jnp.where(el2 == v2, lane, far), axis=-1, keepdims=True)
    e2 = jnp.exp(v2 - v1)
    w1 = g_gate / (1.0 + e2)
    w2 = g_gate * e2 / (1.0 + e2)
    return i1, i2, w1, w2


ROUTE_E1, ROUTE_E2, ROUTE_W1, ROUTE_W2, ROUTE_RANK1, ROUTE_RANK2 = range(6)


def _outproj_kernel(x_ref, ys_ref, u_ref, yb_ref, d_ref, wglu_ref, wout_ref, nf_ref, rw_ref, rb_ref,
                    x1_ref, hn_ref, route_ref, rt_ref, cnt_ref, wglu_bf, wout_bf, rw_bf):
    @pl.when(pl.program_id(0) == 0)
    def _():
        cnt_ref[...] = jnp.zeros_like(cnt_ref)
        rw = rw_ref[...]
        rw_hi = rw.astype(BF16)
        rw_bf[:, 0:LANES] = rw_hi
        rw_bf[:, LANES:] = (rw - rw_hi.astype(F32)).astype(BF16)

    _cast_weight_once(wglu_ref, wglu_bf)
    _cast_weight_once(wout_ref, wout_bf)
    ys = jnp.concatenate([ys_ref[q] for q in range(N_SLABS)], axis=1)
    u = jnp.concatenate([u_ref[q] for q in range(N_SLABS)], axis=1)
    z = jax.nn.gelu(ys + d_ref[...] * u)
    ya = z * jax.nn.sigmoid(_bdot(z, wglu_bf[...]))
    mix = _bdot(ya, wout_bf[0:D_SSM, :]) + _bdot(yb_ref[...], wout_bf[D_SSM:, :])
    x1 = x_ref[...] + mix
    x1_ref[...] = x1
    hn = _rms(x1, nf_ref[...])
    hn_ref[...] = hn
    h_hi = hn.astype(BF16)
    h_lo = (hn - h_hi.astype(F32)).astype(BF16)
    hw = jnp.dot(h_hi, rw_bf[...], preferred_element_type=F32)
    logits = (hw[:, :LANES] + hw[:, LANES:]
              + jnp.dot(h_lo, rw_bf[:, 0:LANES], preferred_element_type=F32) + rb_ref[...])
    i1, i2, w1, w2 = _route(logits)
    tm = logits.shape[0]
    lane_i = lax.broadcasted_iota(jnp.int32, logits.shape, 1)
    lane = lane_i.astype(F32)
    picked = jnp.where((lane == i1) | (lane == i2), 1.0, 0.0)
    earlier = (lax.broadcasted_iota(jnp.int32, (tm, tm), 0) > lax.broadcasted_iota(jnp.int32, (tm, tm), 1))
    prefix = _bdot(jnp.where(earlier, 1.0, 0.0), picked.astype(BF16)) + cnt_ref[...]
    rank1 = jnp.sum(jnp.where(lane == i1, prefix, 0.0), axis=-1, keepdims=True)
    rank2 = jnp.sum(jnp.where(lane == i2, prefix, 0.0), axis=-1, keepdims=True)
    cnt_ref[...] += jnp.sum(picked, axis=0, keepdims=True)
    rec = jnp.zeros_like(logits)
    for lane_id, val in ((ROUTE_E1, i1), (ROUTE_E2, i2), (ROUTE_W1, w1), (ROUTE_W2, w2),
                         (ROUTE_RANK1, rank1), (ROUTE_RANK2, rank2)):
        rec = jnp.where(lane_i == lane_id, val, rec)
    route_ref[...] = rec
    rt_ref[...] = rec.T[0:SUBLANES, :]


def _outproj(x, ys, u, yb, d, wglu, wout, nf, rw, rb, layer, tm):
    n = x.shape[0]
    return pl.pallas_call(
        _outproj_kernel,
        grid=(n // tm,),
        in_specs=[_row_spec(tm, D_MODEL), _slab_spec(tm), _slab_spec(tm), _row_spec(tm, D_CONV),
                  _layer_spec(layer, 1, D_SSM), _layer_spec(layer, D_SSM, D_SSM),
                  _layer_spec(layer, D_MODEL, D_MODEL), _layer_spec(layer, 1, D_MODEL),
                  _const_spec((D_MODEL, LANES)), _const_spec((1, LANES))],
        out_specs=[_row_spec(tm, D_MODEL), _row_spec(tm, D_MODEL), _row_spec(tm, LANES),
                   pl.BlockSpec((SUBLANES, tm), lambda i: (0, i)), _const_spec((1, LANES))],
        out_shape=[jax.ShapeDtypeStruct((n, D_MODEL), F32), jax.ShapeDtypeStruct((n, D_MODEL), F32),
                   jax.ShapeDtypeStruct((n, LANES), F32), jax.ShapeDtypeStruct((SUBLANES, n), F32),
                   jax.ShapeDtypeStruct((1, LANES), F32)],
        scratch_shapes=[pltpu.VMEM((D_SSM, D_SSM), BF16), pltpu.VMEM((D_MODEL, D_MODEL), BF16),
                        pltpu.VMEM((D_MODEL, 2 * LANES), BF16)],
        compiler_params=_params("arbitrary"),
        name="outproj_router",
    )(x, ys, u, yb, _vec(d), wglu, wout, _vec(nf), rw, rb)


PLAN_TILE_LANES = 2 * LANES
PLAN_EXPERT, PLAN_ROWS, PLAN_USED = range(3)


def _moe_plan_kernel(rt_ref, cnt_ref, slot_ref, tile_ref, *, tms):
    cnt = cnt_ref[...]
    padded = jnp.ceil(cnt * (1.0 / tms)) * float(tms)
    r = lax.broadcasted_iota(jnp.int32, (LANES, LANES), 0)
    c = lax.broadcasted_iota(jnp.int32, (LANES, LANES), 1)
    ends = jnp.dot(padded, jnp.where(r <= c, 1.0, 0.0), precision=lax.Precision.HIGHEST,
                   preferred_element_type=F32)
    starts = ends - padded
    rt = rt_ref[...]
    e1, e2 = rt[ROUTE_E1:ROUTE_E1 + 1], rt[ROUTE_E2:ROUTE_E2 + 1]
    s1, s2 = rt[ROUTE_RANK1:ROUTE_RANK1 + 1], rt[ROUTE_RANK2:ROUTE_RANK2 + 1]
    tile = lax.broadcasted_iota(jnp.int32, (1, PLAN_TILE_LANES), 1).astype(F32)
    used = ends[:, N_EXPERTS - 1:N_EXPERTS] * (1.0 / tms)
    pos = jnp.minimum(tile, used - 1.0) * float(tms)
    t_exp = jnp.zeros_like(tile)
    t_fill = jnp.zeros_like(tile)
    for e in range(N_EXPERTS):
        st, en = starts[:, e:e + 1], ends[:, e:e + 1]
        s1 = s1 + jnp.where(e1 == float(e), st, 0.0)
        s2 = s2 + jnp.where(e2 == float(e), st, 0.0)
        mine = (pos >= st) & (pos < en)
        t_exp = t_exp + jnp.where(mine, float(e), 0.0)
        t_fill = t_fill + jnp.where(mine, st + cnt[:, e:e + 1], 0.0)
    t_rows = jnp.where(tile < used, jnp.clip(t_fill - pos, 0.0, float(tms)), 0.0)
    slot_ref[...] = jnp.concatenate([s1, s2], axis=0).astype(jnp.int32)
    tile_ref[...] = jnp.concatenate(
        [t_exp, t_rows, jnp.broadcast_to(used, tile.shape), jnp.zeros((SUBLANES - 3, PLAN_TILE_LANES), F32)],
        axis=0).astype(jnp.int32)


def _moe_plan(route_t, counts, tms, n_tiles):
    n = route_t.shape[1]
    assert n_tiles <= PLAN_TILE_LANES
    slots, tiles = pl.pallas_call(
        functools.partial(_moe_plan_kernel, tms=tms),
        out_shape=[jax.ShapeDtypeStruct((2, n), jnp.int32),
                   jax.ShapeDtypeStruct((SUBLANES, PLAN_TILE_LANES), jnp.int32)],
        compiler_params=pltpu.CompilerParams(vmem_limit_bytes=VMEM_LIMIT),
        name="moe_plan",
    )(route_t, counts)
    return slots, tiles[PLAN_EXPERT, :n_tiles], tiles[PLAN_ROWS, :n_tiles], tiles[PLAN_USED, :1]


DMA_UNROLL = 8


HN_BUFFERS = 3


def _dispatch_kernel(tr_ref, slot_ref, hn_ref, xs_ref, zbuf, hbuf, in_sem, out_sem, zsem,
                     *, tm, tms, n_tiles, n_steps):
    i = pl.program_id(0)

    def fetch(t):
        b = lax.rem(t, HN_BUFFERS)
        return pltpu.make_async_copy(hn_ref.at[pl.ds(pl.multiple_of(t * tm, tm), tm)], hbuf.at[b], in_sem.at[b])

    def drain_scatter(t):
        b = lax.rem(t, HN_BUFFERS)
        for k in range(2):
            pltpu.make_async_copy(hbuf.at[b], xs_ref.at[pl.ds(0, tm)], out_sem.at[b]).wait()

    @pl.when(i == 0)
    def _():
        fetch(i).start()
        zbuf[...] = jnp.zeros_like(zbuf)

        def fill(t, carry):
            @pl.when(tr_ref[t] < tms)
            def _():
                pltpu.make_async_copy(zbuf, xs_ref.at[pl.ds(pl.multiple_of(t * tms, tms), tms)], zsem).start()
            return carry

        def drain(t, carry):
            @pl.when(tr_ref[t] < tms)
            def _():
                pltpu.make_async_copy(zbuf, xs_ref.at[pl.ds(0, tms)], zsem).wait()
            return carry

        lax.fori_loop(0, n_tiles, fill, 0)
        lax.fori_loop(0, n_tiles, drain, 0)

    @pl.when(i + 1 < n_steps)
    def _():
        fetch(i + 1).start()

    fetch(i).wait()
    b = lax.rem(i, HN_BUFFERS)
    rows = hbuf.at[b]

    def issue(r, carry):
        for k in range(2):
            pltpu.make_async_copy(rows.at[pl.ds(r, 1)], xs_ref.at[pl.ds(slot_ref[k, r], 1)],
                                  out_sem.at[b]).start(priority=k)
        return carry

    lax.fori_loop(0, tm, issue, 0, unroll=DMA_UNROLL)

    @pl.when(i >= 1)
    def _():
        drain_scatter(i - 1)

    @pl.when(i == n_steps - 1)
    def _():
        drain_scatter(i)


def _dispatch(hn, slots, tile_rows, tms, tm):
    n = hn.shape[0]
    n_tiles = tile_rows.shape[0]
    return pl.pallas_call(
        functools.partial(_dispatch_kernel, tm=tm, tms=tms, n_tiles=n_tiles, n_steps=n // tm),
        grid_spec=pltpu.PrefetchScalarGridSpec(
            num_scalar_prefetch=1, grid=(n // tm,),
            in_specs=[pl.BlockSpec((2, tm), lambda i, tr: (0, i), memory_space=pltpu.SMEM),
                      pl.BlockSpec(memory_space=pl.ANY)],
            out_specs=pl.BlockSpec(memory_space=pl.ANY),
            scratch_shapes=[pltpu.VMEM((tms, D_MODEL), F32), pltpu.VMEM((HN_BUFFERS, tm, D_MODEL), F32),
                            pltpu.SemaphoreType.DMA((HN_BUFFERS,)), pltpu.SemaphoreType.DMA((HN_BUFFERS,)),
                            pltpu.SemaphoreType.DMA]),
        out_shape=jax.ShapeDtypeStruct((n_tiles * tms, D_MODEL), F32),
        compiler_params=_params("arbitrary"),
        name="moe_dispatch",
    )(tile_rows, slots, hn)


def _moe_kernel(te_ref, nu_ref, x_ref, wg_ref, wu_ref, wd_ref, y_ref, wg_bf, wu_bf, wd_bf):
    i = pl.program_id(0)
    in_use = i < nu_ref[0]
    new_expert = (i == 0) | (te_ref[i] != te_ref[jnp.maximum(i - 1, 0)])

    @pl.when(in_use & new_expert)
    def _():
        wg_bf[...] = wg_ref[...].astype(BF16)
        wu_bf[...] = wu_ref[...].astype(BF16)
        wd_bf[...] = wd_ref[...].astype(BF16)

    @pl.when(in_use)
    def _():
        h = x_ref[...].astype(BF16)
        hg = jnp.dot(h, wg_bf[...], preferred_element_type=F32)
        hu = jnp.dot(h, wu_bf[...], preferred_element_type=F32)
        y_ref[...] = _bdot(hg * jax.nn.sigmoid(hg) * hu, wd_bf[...])

    @pl.when(jnp.logical_not(in_use))
    def _():
        y_ref[...] = jnp.zeros_like(y_ref)


def _moe(xs, tile_expert, n_used, wg, wu, wd, layer, tms):
    n_slots = xs.shape[0]
    rows = pl.BlockSpec((tms, D_MODEL), lambda i, te, nu: (jnp.minimum(i, nu[0] - 1), 0))
    out_rows = pl.BlockSpec((tms, D_MODEL), lambda i, te, nu: (i, 0))

    def wspec(a, b):
        return pl.BlockSpec((None, None, a, b), lambda i, te, nu: (layer, te[i], 0, 0))

    return pl.pallas_call(
        _moe_kernel,
        grid_spec=pltpu.PrefetchScalarGridSpec(
            num_scalar_prefetch=2, grid=(n_slots // tms,),
            in_specs=[rows, wspec(D_MODEL, D_EXPERT), wspec(D_MODEL, D_EXPERT), wspec(D_EXPERT, D_MODEL)],
            out_specs=out_rows,
            scratch_shapes=[pltpu.VMEM((D_MODEL, D_EXPERT), BF16), pltpu.VMEM((D_MODEL, D_EXPERT), BF16),
                            pltpu.VMEM((D_EXPERT, D_MODEL), BF16)]),
        out_shape=jax.ShapeDtypeStruct((n_slots, D_MODEL), F32),
        compiler_params=_params("arbitrary"),
        name="moe",
    )(tile_expert, n_used, xs, wg, wu, wd)


def _ple_kernel(slot_ref, next_slot_ref, x_ref, route_ref, p_ref, np_ref, wple_ref, wgate_ref, nfin_ref,
                ys_ref, o_ref, ybuf, sem, wple_bf, wgate_bf, *, tm, n_steps, final):
    i = pl.program_id(0)
    _cast_weight_once(wple_ref, wple_bf)
    _cast_weight_once(wgate_ref, wgate_bf)

    def gather(slots, b):
        def issue(r, carry):
            for k in range(2):
                pltpu.make_async_copy(ys_ref.at[pl.ds(slots[k, r], 1)], ybuf.at[b, k, pl.ds(r, 1)],
                                      sem.at[b]).start(priority=k)
            return carry

        lax.fori_loop(0, tm, issue, 0, unroll=DMA_UNROLL)

    @pl.when(i == 0)
    def _():
        gather(slot_ref, 0)

    @pl.when(i + 1 < n_steps)
    def _():
        gather(next_slot_ref, lax.rem(i + 1, 2))

    pe = _bdot(p_ref[...], wple_bf[...])
    b = lax.rem(i, 2)
    for k in range(2):
        pltpu.make_async_copy(ys_ref.at[pl.ds(0, tm)], ybuf.at[b, k], sem.at[b]).wait()
    route = route_ref[...]
    x = (x_ref[...] + route[:, ROUTE_W1:ROUTE_W1 + 1] * ybuf[b, 0]
         + route[:, ROUTE_W2:ROUTE_W2 + 1] * ybuf[b, 1])
    gate = jax.nn.sigmoid(_bdot(_rms(x, np_ref[...]), wgate_bf[...]))
    out = x + pe * gate
    if final:
        out = _rms(out, nfin_ref[...])
    o_ref[...] = out


def _ple(x, route, slots, ys, p, npl, wple, wgate, nfin, layer, tm, final):
    n = x.shape[0]
    n_steps = n // tm
    return pl.pallas_call(
        functools.partial(_ple_kernel, tm=tm, n_steps=n_steps, final=final),
        grid=(n_steps,),
        in_specs=[pl.BlockSpec((2, tm), lambda i: (0, i), memory_space=pltpu.SMEM),
                  pl.BlockSpec((2, tm), lambda i: (0, jnp.minimum(i + 1, n_steps - 1)),
                               memory_space=pltpu.SMEM),
                  _row_spec(tm, D_MODEL), _row_spec(tm, LANES),
                  pl.BlockSpec((None, tm, D_PLE), lambda i: (layer, i, 0)),
                  _layer_spec(layer, 1, D_MODEL), _layer_spec(layer, D_PLE, D_MODEL),
                  _layer_spec(layer, D_MODEL, D_MODEL), _const_spec((1, D_MODEL)),
                  pl.BlockSpec(memory_space=pl.ANY)],
        out_specs=_row_spec(tm, D_MODEL),
        out_shape=jax.ShapeDtypeStruct((n, D_MODEL), F32),
        scratch_shapes=[pltpu.VMEM((2, 2, tm, D_MODEL), F32), pltpu.SemaphoreType.DMA((2,)),
                        pltpu.VMEM((D_PLE, D_MODEL), BF16), pltpu.VMEM((D_MODEL, D_MODEL), BF16)],
        compiler_params=_params("arbitrary"),
        name="combine_ple",
    )(slots, slots, x, route, p, _vec(npl), wple, wgate, nfin.reshape(1, D_MODEL), ys)


def _layer(x, p, h0_re, h0_im, conv_left, w, s5, router_w, router_b, layer, bsz, t_len, tm, tm_conv, tms, final):
    n = bsz * t_len
    u, z = _inproj(x, w["norm_mix"], w["w_in"], layer, tm)
    ys, hf_re, hf_im = _s5_mixer(u, h0_re, h0_im, s5, bsz, t_len)
    yb, conv_new = _conv_mixer(z, conv_left, w["conv_w"], w["conv_b"], w["conv_ln_g"], w["conv_ln_b"],
                               layer, bsz, t_len, tm_conv)
    x1, hn, route, route_t, counts = _outproj(x, ys, u, yb, w["ssm_d"], w["w_ssm_glu"], w["w_out"],
                                              w["norm_ffn"], router_w, router_b, layer, tm)
    n_tiles = 2 * n // tms + N_EXPERTS
    slots, tile_expert, tile_rows, n_used = _moe_plan(route_t, counts, tms, n_tiles)
    xs = _dispatch(hn, slots, tile_rows, tms, tm)
    ysort = _moe(xs, tile_expert, n_used, w["expert_w_gate"], w["expert_w_up"], w["expert_w_down"],
                 layer, tms)
    x3 = _ple(x1, route, slots, ysort, p, w["norm_ple"], w["ple_w"], w["ple_gate_w"], w["norm_final"],
              layer, tm, final)
    return x3, hf_re, hf_im, conv_new


def kernel(x_prompt, x_sample, p_prompt, p_sample, state_ssm_re, state_ssm_im, cache_conv, norm_mix, w_in, ssm_a_re, ssm_a_im, ssm_b_re, ssm_b_im, ssm_c_re, ssm_c_im, ssm_d, ssm_log_dt, w_ssm_glu, conv_w, conv_b, conv_ln_g, conv_ln_b, w_out, norm_ffn, router_group_w, router_group_b, router_expert_w, router_expert_b, expert_w_gate, expert_w_up, expert_w_down, norm_ple, ple_w, ple_gate_w, norm_final):
    depth = w_in.shape[0]
    bp, tp, _ = x_prompt.shape
    bs, ts, _ = x_sample.shape
    xp = x_prompt.reshape(bp * tp, D_MODEL)
    xs = x_sample.reshape(bs * ts, D_MODEL)
    pp = p_prompt.reshape(depth, bp * tp, D_PLE)
    ps = p_sample.reshape(depth, bs * ts, D_PLE)
    zero_state = jnp.zeros((bp, N_GROUPS, SSM_STATE), F32)
    zero_conv = jnp.zeros((bp, CONV_WIDTH - 1, D_CONV), F32)
    pad_lanes = LANES - N_EXPERTS - N_EXPERT_GROUPS
    w = {"norm_mix": norm_mix, "w_in": w_in, "ssm_d": ssm_d, "w_ssm_glu": w_ssm_glu, "conv_w": conv_w,
         "conv_b": conv_b, "conv_ln_g": conv_ln_g, "conv_ln_b": conv_ln_b, "w_out": w_out,
         "norm_ffn": norm_ffn, "expert_w_gate": expert_w_gate, "expert_w_up": expert_w_up,
         "expert_w_down": expert_w_down, "norm_ple": norm_ple, "ple_w": ple_w, "ple_gate_w": ple_gate_w,
         "norm_final": norm_final}
    outs = {k: [] for k in ("pr_re", "pr_im", "pr_conv", "sm_re", "sm_im", "sm_conv")}
    for i in range(depth):
        s5 = _s5_prep(ssm_a_re[i], ssm_a_im[i], ssm_log_dt[i], ssm_b_re[i], ssm_b_im[i], ssm_c_re[i],
                      ssm_c_im[i])
        router_w = jnp.pad(jnp.concatenate([router_expert_w[i], router_group_w[i]], axis=1),
                           ((0, 0), (0, pad_lanes)))
        router_b = jnp.pad(jnp.concatenate([router_expert_b[i], router_group_b[i]]),
                           (0, pad_lanes)).reshape(1, LANES)
        final = i == depth - 1
        xp, hr, hi, cv = _layer(xp, pp, zero_state, zero_state, zero_conv, w, s5, router_w, router_b, i,
                                bp, tp, tm=512, tm_conv=512, tms=256, final=final)
        outs["pr_re"].append(hr); outs["pr_im"].append(hi); outs["pr_conv"].append(cv)
        xs, hr, hi, cv = _layer(xs, ps, state_ssm_re[i], state_ssm_im[i], cache_conv[i], w, s5, router_w,
                                router_b, i, bs, ts, tm=bs * ts, tm_conv=ts, tms=32, final=final)
        outs["sm_re"].append(hr); outs["sm_im"].append(hi); outs["sm_conv"].append(cv)
    return (xp.reshape(bp, tp, D_MODEL), xs.reshape(bs, ts, D_MODEL),
            jnp.stack(outs["pr_re"]), jnp.stack(outs["pr_im"]), jnp.stack(outs["pr_conv"]),
            jnp.stack(outs["sm_re"]), jnp.stack(outs["sm_im"]), jnp.stack(outs["sm_conv"]))
```

```python
import functools

import jax
import jax.numpy as jnp
from jax import lax
from jax.experimental import pallas as pl
from jax.experimental.pallas import tpu as pltpu

F32 = jnp.float32
BF16 = jnp.bfloat16

D_MODEL = 1024
D_SSM = 512
SSM_GROUP = 16
N_GROUPS = D_SSM // SSM_GROUP
N_PAIRS = N_GROUPS // 2
SSM_STATE = 64
D_CONV = 512
CONV_WIDTH = 31
CONV_HALO = 32
N_EXPERT_GROUPS = 4
EXPERTS_PER_GROUP = 8
N_EXPERTS = 32
D_EXPERT = 256
D_PLE = 256
EPS = 1e-6
S5_CHUNK = 16
LANES = 128
SUBLANES = 8
N_SLABS = D_SSM // LANES
GROUPS_PER_SLAB = LANES // SSM_GROUP
PAIRS_PER_SLAB = GROUPS_PER_SLAB // 2
VMEM_LIMIT = 56 * 1024 * 1024


def _params(*sem):
    return pltpu.CompilerParams(dimension_semantics=sem, vmem_limit_bytes=VMEM_LIMIT)


def _rms(x, g):
    return x * lax.rsqrt(jnp.mean(x * x, axis=-1, keepdims=True) + EPS) * g


def _bdot(a, b):
    return jnp.dot(a.astype(BF16), b, preferred_element_type=F32)


def _row_spec(tm, width):
    return pl.BlockSpec((tm, width), lambda i: (i, 0))


ROW_TILE = (D_MODEL // LANES, LANES)


def _row_tile_spec(tm):
    return pl.BlockSpec((tm,) + ROW_TILE, lambda i, *_: (i, 0, 0))


def _store_row_tiles(ref, rows):
    for s in range(ROW_TILE[0]):
        ref[:, s, :] = rows[:, s * LANES:(s + 1) * LANES]


def _load_row_tiles(ref):
    return jnp.concatenate([ref[:, s, :] for s in range(ROW_TILE[0])], axis=1)


def _slab_spec(tm):
    return pl.BlockSpec((N_SLABS, tm, LANES), lambda i: (0, i, 0))


def _const_spec(shape):
    return pl.BlockSpec(shape, lambda i: (0,) * len(shape))


def _layer_spec(layer, *shape):
    return pl.BlockSpec((None,) + shape, lambda *_: (layer,) + (0,) * len(shape))


def _vec(stacked):
    return stacked.reshape(stacked.shape[0], 1, stacked.shape[1])


def _cast_weight_once(w_ref, wbf_ref):
    @pl.when(pl.program_id(0) == 0)
    def _():
        wbf_ref[...] = w_ref[...].astype(BF16)


def _inproj_kernel(x_ref, g_ref, w_ref, u_ref, z_ref, wbf):
    _cast_weight_once(w_ref, wbf)
    hn = _rms(x_ref[...], g_ref[...])
    proj = _bdot(hn, wbf[...])
    for q in range(N_SLABS):
        u_ref[q] = proj[:, q * LANES:(q + 1) * LANES]
    z_ref[...] = proj[:, D_SSM:]


def _inproj(x, g, w, layer, tm):
    n = x.shape[0]
    d_in = w.shape[2]
    return pl.pallas_call(
        _inproj_kernel,
        grid=(n // tm,),
        in_specs=[_row_spec(tm, D_MODEL), _layer_spec(layer, 1, D_MODEL), _layer_spec(layer, D_MODEL, d_in)],
        out_specs=[_slab_spec(tm), _row_spec(tm, d_in - D_SSM)],
        out_shape=[jax.ShapeDtypeStruct((N_SLABS, n, LANES), F32),
                   jax.ShapeDtypeStruct((n, d_in - D_SSM), F32)],
        scratch_shapes=[pltpu.VMEM((D_MODEL, d_in), BF16)],
        compiler_params=_params("arbitrary"),
        name="inproj",
    )(x, _vec(g), w)


def _s5_prep_kernel(ar_ref, ai_ref, ldt_ref, bre_ref, bim_ref, cre_ref, cim_ref,
                    m_ref, ws_ref, wot_ref, atab_ref, wt_re, wt_im, br_re, br_im):
    n_tap = S5_CHUNK * SSM_GROUP
    st = 4 * SSM_STATE
    nt = (((1,), (1,)), ((), ()))
    hi = lax.Precision.HIGHEST
    src = lax.broadcasted_iota(jnp.int32, (SSM_STATE, st), 0)
    dst = lax.broadcasted_iota(jnp.int32, (SSM_STATE, st), 1)
    lane = lax.broadcasted_iota(jnp.int32, (SSM_GROUP, n_tap), 1)
    ws_rows, wot_rows, atab = [], [], jnp.zeros((2 * SUBLANES, st), F32)
    for gi in range(2):
        ar, ai = ar_ref[gi], ai_ref[gi]
        dt = jnp.exp(ldt_ref[gi])
        k = lax.broadcasted_iota(jnp.int32, (S5_CHUNK + SUBLANES, SSM_STATE), 0).astype(F32)
        mag = jnp.exp(k * (dt * ar))
        ang = k * (dt * ai)
        p_re, p_im = mag * jnp.cos(ang), mag * jnp.sin(ang)
        inv = 1.0 / (ar * ar + ai * ai)
        ab_re, ab_im = p_re[1:2], p_im[1:2]
        ia_re, ia_im = ar * inv, -ai * inv
        coef_re = (ab_re - 1.0) * ia_re - ab_im * ia_im
        coef_im = (ab_re - 1.0) * ia_im + ab_im * ia_re
        bre, bim = bre_ref[gi], bim_ref[gi]
        bb_re = coef_re * bre - coef_im * bim
        bb_im = coef_re * bim + coef_im * bre
        cre, cim = cre_ref[gi], cim_ref[gi]
        for kk in range(S5_CHUNK + 1):
            pr, pi = p_re[kk:kk + 1], p_im[kk:kk + 1]
            rows = slice(kk * SSM_GROUP, (kk + 1) * SSM_GROUP)
            wt_re[rows, :] = pr * cre - pi * cim
            wt_im[rows, :] = -pi * cre - pr * cim
            if kk < S5_CHUNK:
                back = slice((S5_CHUNK - 1 - kk) * SSM_GROUP, (S5_CHUNK - kk) * SSM_GROUP)
                br_re[back, :] = pr * bb_re - pi * bb_im
                br_im[back, :] = pi * bb_re + pr * bb_im
        kcat = (lax.dot_general(bb_re, wt_re[0:n_tap, :], nt, precision=hi, preferred_element_type=F32)
                + lax.dot_general(bb_im, wt_im[0:n_tap, :], nt, precision=hi, preferred_element_type=F32))
        for s in range(S5_CHUNK):
            shifted = kcat if s == 0 else pltpu.roll(kcat, s * SSM_GROUP, 1)
            m_ref[gi, s * SSM_GROUP:(s + 1) * SSM_GROUP, :] = jnp.where(
                lane >= s * SSM_GROUP, shifted, 0.0).astype(BF16)
        put_re = jnp.where(dst == src + gi * SSM_STATE, 1.0, 0.0)
        put_im = jnp.where(dst == src + (2 + gi) * SSM_STATE, 1.0, 0.0)

        def place(v_re, v_im):
            return (jnp.dot(v_re, put_re, precision=hi, preferred_element_type=F32)
                    + jnp.dot(v_im, put_im, precision=hi, preferred_element_type=F32))

        ws_rows.append(place(br_re[...], br_im[...]))
        wot_rows.append(place(wt_re[SSM_GROUP:, :], wt_im[SSM_GROUP:, :]))
        kc = float(S5_CHUNK) * lax.broadcasted_iota(jnp.int32, (2 * SUBLANES, SSM_STATE), 0).astype(F32)
        magc = jnp.exp(kc * (dt * ar))
        angc = kc * (dt * ai)
        atab = atab + place(magc * jnp.cos(angc), magc * jnp.sin(angc))
    ws_ref[...] = jnp.concatenate(ws_rows, axis=0).astype(BF16)
    wot_ref[...] = jnp.concatenate(wot_rows, axis=0).astype(BF16)
    atab_ref[...] = atab


def _s5_prep(a_re, a_im, log_dt, b_re, b_im, c_re, c_im):
    p, n, c = N_PAIRS, SSM_STATE, SSM_GROUP
    n_tap = S5_CHUNK * c
    st = 4 * n

    def pspec(*shape):
        return pl.BlockSpec((None,) + shape, lambda i: (i,) + (0,) * len(shape))

    def pairs(a, *shape):
        return a.reshape((p, 2) + shape)

    return pl.pallas_call(
        _s5_prep_kernel,
        grid=(p,),
        in_specs=[pspec(2, 1, n), pspec(2, 1, n), pspec(2, 1, 1), pspec(2, c, n), pspec(2, c, n),
                  pspec(2, c, n), pspec(2, c, n)],
        out_specs=[pspec(2, n_tap, n_tap), pspec(2 * n_tap, st), pspec(2 * n_tap, st), pspec(2 * SUBLANES, st)],
        out_shape=[jax.ShapeDtypeStruct((p, 2, n_tap, n_tap), BF16),
                   jax.ShapeDtypeStruct((p, 2 * n_tap, st), BF16),
                   jax.ShapeDtypeStruct((p, 2 * n_tap, st), BF16),
                   jax.ShapeDtypeStruct((p, 2 * SUBLANES, st), F32)],
        scratch_shapes=[pltpu.VMEM((n_tap + c, n), F32), pltpu.VMEM((n_tap + c, n), F32),
                        pltpu.VMEM((n_tap, n), F32), pltpu.VMEM((n_tap, n), F32)],
        compiler_params=_params("parallel"),
        name="s5_prep",
    )(pairs(a_re, 1, n), pairs(a_im, 1, n), pairs(log_dt, 1, 1),
      pairs(jnp.swapaxes(b_re, 1, 2), c, n), pairs(jnp.swapaxes(b_im, 1, 2), c, n),
      pairs(c_re, c, n), pairs(c_im, c, n))


def _block_transpose8(vs):
    lane = lax.broadcasted_iota(jnp.int32, vs[0].shape, 1)
    blk = lane >> 4
    for d in (4, 2, 1):
        keep = (blk & d) == 0
        new = list(vs)
        for i in range(GROUPS_PER_SLAB):
            if i & d == 0:
                a, b = vs[i], vs[i + d]
                new[i] = jnp.where(keep, a, pltpu.roll(b, d * SSM_GROUP, 1))
                new[i + d] = jnp.where(keep, pltpu.roll(a, LANES - d * SSM_GROUP, 1), b)
        vs = new
    return vs


def _cmul(ar, ai, xr, xi):
    return ar * xr - ai * xi, ar * xi + ai * xr


def _s5_kernel(u_ref, h0_ref, m_ref, ws_ref, wo_ref, a_ref, y_ref, hf_ref, x_scr, yg_scr, s_scr, hp_scr,
               *, rows, independent):
    half = 2 * SSM_STATE
    rt = min(rows, 16 * SUBLANES)
    half_chunk = S5_CHUNK // 2

    def gather_tile(t, carry):
        r0 = pl.multiple_of(t * rt, rt)
        for hf in range(2):
            vs = [u_ref[pl.ds(r0 * S5_CHUNK + hf * half_chunk + i, rt, stride=S5_CHUNK), :]
                  for i in range(half_chunk)]
            outs = _block_transpose8(vs)
            for g in range(GROUPS_PER_SLAB):
                x_scr[g, pl.ds(r0, rt), hf * LANES:(hf + 1) * LANES] = outs[g]
        return carry

    lax.fori_loop(0, rows // rt, gather_tile, 0)

    row = lax.broadcasted_iota(jnp.int32, (SUBLANES, half), 0)
    n_tap = S5_CHUNK * SSM_GROUP
    for pi in range(PAIRS_PER_SLAB):
        x0 = x_scr[2 * pi].astype(BF16)
        x1 = x_scr[2 * pi + 1].astype(BF16)
        s_scr[...] = jnp.dot(jnp.concatenate([x0, x1], axis=1), ws_ref[pi], preferred_element_type=F32)
        ap = a_ref[pi]
        h0 = h0_ref[pi]
        if independent:
            hp_scr[...] = h0
            s = s_scr[...]
            n_re, n_im = _cmul(ap[1:2, :half], ap[1:2, half:], h0[:, :half], h0[:, half:])
            hf_ref[pi] = jnp.concatenate([n_re + s[:, :half], n_im + s[:, half:]], axis=1)
        else:
            pw_re, pw_im = ap[0:SUBLANES, :half], ap[0:SUBLANES, half:]

            def scan_tile(t, carry):
                h_re, h_im = carry
                r0 = pl.multiple_of(t * SUBLANES, SUBLANES)
                s = s_scr[pl.ds(r0, SUBLANES), :]
                t_re, t_im = s[:, :half], s[:, half:]
                for d in (1, 2, 4):
                    sh_re = jnp.where(row >= d, pltpu.roll(t_re, d, 0), 0.0)
                    sh_im = jnp.where(row >= d, pltpu.roll(t_im, d, 0), 0.0)
                    m_re, m_im = _cmul(ap[d:d + 1, :half], ap[d:d + 1, half:], sh_re, sh_im)
                    t_re, t_im = t_re + m_re, t_im + m_im
                e_re = jnp.where(row >= 1, pltpu.roll(t_re, 1, 0), 0.0)
                e_im = jnp.where(row >= 1, pltpu.roll(t_im, 1, 0), 0.0)
                c_re, c_im = _cmul(pw_re, pw_im, h_re, h_im)
                hp_scr[pl.ds(r0, SUBLANES), :] = jnp.concatenate([e_re + c_re, e_im + c_im], axis=1)
                o_re, o_im = _cmul(ap[SUBLANES:SUBLANES + 1, :half], ap[SUBLANES:SUBLANES + 1, half:],
                                   h_re, h_im)
                last = SUBLANES - 1
                n_re = jnp.broadcast_to(t_re[last:last + 1], h_re.shape) + o_re
                n_im = jnp.broadcast_to(t_im[last:last + 1], h_im.shape) + o_im
                return n_re, n_im

            init = (jnp.broadcast_to(h0[:, :half], (SUBLANES, half)),
                    jnp.broadcast_to(h0[:, half:], (SUBLANES, half)))
            h_re, h_im = lax.fori_loop(0, rows // SUBLANES, scan_tile, init, unroll=4)
            hf_ref[pi] = jnp.concatenate([h_re[0:1], h_im[0:1]], axis=1)
        yc = lax.dot_general(hp_scr[...].astype(BF16), wo_ref[pi], (((1,), (1,)), ((), ())),
                             preferred_element_type=F32)
        yg_scr[2 * pi] = jnp.dot(x0, m_ref[pi, 0], preferred_element_type=F32) + yc[:, :n_tap]
        yg_scr[2 * pi + 1] = jnp.dot(x1, m_ref[pi, 1], preferred_element_type=F32) + yc[:, n_tap:]

    def scatter_tile(t, carry):
        r0 = pl.multiple_of(t * rt, rt)
        for hf in range(2):
            vs = [yg_scr[g, pl.ds(r0, rt), hf * LANES:(hf + 1) * LANES] for g in range(GROUPS_PER_SLAB)]
            outs = _block_transpose8(vs)
            for i in range(half_chunk):
                y_ref[pl.ds(r0 * S5_CHUNK + hf * half_chunk + i, rt, stride=S5_CHUNK), :] = outs[i]
        return carry

    lax.fori_loop(0, rows // rt, scatter_tile, 0)


def _s5_mixer(u, h0_re, h0_im, prep, bsz, t_len):
    m, wsp, wop, a16 = prep
    n_tap = S5_CHUNK * SSM_GROUP
    st = 4 * SSM_STATE
    independent = t_len == S5_CHUNK
    if independent:
        nblk, rows, hrows = 1, bsz, bsz
    else:
        nblk, rows, hrows = bsz, t_len // S5_CHUNK, 1
    assert t_len % S5_CHUNK == 0 and rows % SUBLANES == 0, (bsz, t_len)
    h0p = jnp.concatenate([h0_re.reshape(bsz, N_PAIRS, 2 * SSM_STATE),
                           h0_im.reshape(bsz, N_PAIRS, 2 * SSM_STATE)], axis=2).astype(F32)
    h0p = h0p.transpose(1, 0, 2)[None] if independent else h0p[:, :, None, :]
    pp = PAIRS_PER_SLAB

    def wspec(*shape):
        return pl.BlockSpec((pp,) + shape, lambda q, b: (q,) + (0,) * len(shape))

    frames = rows * S5_CHUNK
    y, hf = pl.pallas_call(
        functools.partial(_s5_kernel, rows=rows, independent=independent),
        grid=(N_SLABS, nblk),
        in_specs=[pl.BlockSpec((None, frames, LANES), lambda q, b: (q, b, 0)),
                  pl.BlockSpec((None, pp, hrows, st), lambda q, b: (b, q, 0, 0)),
                  wspec(2, n_tap, n_tap), wspec(2 * n_tap, st), wspec(2 * n_tap, st), wspec(2 * SUBLANES, st)],
        out_specs=[pl.BlockSpec((None, frames, LANES), lambda q, b: (q, b, 0)),
                   pl.BlockSpec((None, pp, hrows, st), lambda q, b: (b, q, 0, 0))],
        out_shape=[jax.ShapeDtypeStruct(u.shape, F32),
                   jax.ShapeDtypeStruct((nblk, N_PAIRS, hrows, st), F32)],
        scratch_shapes=[pltpu.VMEM((GROUPS_PER_SLAB, rows, n_tap), F32),
                        pltpu.VMEM((GROUPS_PER_SLAB, rows, n_tap), F32),
                        pltpu.VMEM((rows, st), F32), pltpu.VMEM((rows, st), F32)],
        compiler_params=_params("parallel", "parallel"),
        name="s5_core",
    )(u, h0p, m, wsp, wop, a16)
    hf = hf[0].transpose(1, 0, 2) if independent else hf[:, :, 0, :]
    hf_re = hf[:, :, :2 * SSM_STATE].reshape(bsz, N_GROUPS, SSM_STATE)
    hf_im = hf[:, :, 2 * SSM_STATE:].reshape(bsz, N_GROUPS, SSM_STATE)
    return y, hf_re, hf_im


def _conv_kernel(z_ref, left_ref, w_ref, b_ref, g_ref, beta_ref, y_ref, cn_ref, vbuf, shifted, *, tm):
    @pl.when(pl.program_id(1) == 0)
    def _():
        vbuf[0:CONV_HALO, :] = left_ref[...]

    z = z_ref[...]
    vbuf[CONV_HALO:CONV_HALO + tm, :] = z[:, :D_CONV] * jax.nn.sigmoid(z[:, D_CONV:])
    first = CONV_HALO - (CONV_WIDTH - 1)
    span = tm + CONV_HALO - SUBLANES
    for r in range(1, SUBLANES):
        shifted[r - 1, 0:span, :] = vbuf[r:r + span, :]
    acc = jnp.zeros((tm, D_CONV), F32)
    for k in range(CONV_WIDTH):
        a, r = divmod(first + k, SUBLANES)
        src = vbuf if r == 0 else shifted.at[r - 1]
        acc = acc + w_ref[k:k + 1, :] * src[a * SUBLANES:a * SUBLANES + tm, :]
    y = acc + b_ref[...]
    mu = jnp.mean(y, axis=-1, keepdims=True)
    yc = y - mu
    var = jnp.mean(yc * yc, axis=-1, keepdims=True)
    yn = yc * lax.rsqrt(var + EPS) * g_ref[...] + beta_ref[...]
    y_ref[...] = yn * jax.nn.sigmoid(yn)
    cn_ref[...] = vbuf[tm + first:tm + CONV_HALO, :]
    vbuf[0:CONV_HALO, :] = vbuf[tm:tm + CONV_HALO, :]


def _conv_mixer(z, left, w, b, g, beta, layer, bsz, t_len, tm):
    left = jnp.pad(left.astype(F32), ((0, 0), (CONV_HALO - (CONV_WIDTH - 1), 0), (0, 0)))
    nt = t_len // tm
    vec = _layer_spec(layer, 1, D_CONV)
    return pl.pallas_call(
        functools.partial(_conv_kernel, tm=tm),
        grid=(bsz, nt),
        in_specs=[pl.BlockSpec((tm, 2 * D_CONV), lambda bi, j: (bi * nt + j, 0)),
                  pl.BlockSpec((None, CONV_HALO, D_CONV), lambda bi, j: (bi, 0, 0)),
                  _layer_spec(layer, CONV_WIDTH, D_CONV), vec, vec, vec],
        out_specs=[pl.BlockSpec((tm, D_CONV), lambda bi, j: (bi * nt + j, 0)),
                   pl.BlockSpec((None, CONV_WIDTH - 1, D_CONV), lambda bi, j: (bi, 0, 0))],
        out_shape=[jax.ShapeDtypeStruct((bsz * t_len, D_CONV), F32),
                   jax.ShapeDtypeStruct((bsz, CONV_WIDTH - 1, D_CONV), F32)],
        scratch_shapes=[pltpu.VMEM((tm + CONV_HALO, D_CONV), F32),
                        pltpu.VMEM((SUBLANES - 1, tm + CONV_HALO - SUBLANES, D_CONV), F32)],
        compiler_params=_params("parallel", "arbitrary"),
        name="conv_mixer",
    )(z, left, w, _vec(b), _vec(g), _vec(beta))


def _route(logits):
    lane_i = lax.broadcasted_iota(jnp.int32, logits.shape, 1)
    lane = lane_i.astype(F32)
    group_of_lane = (lane_i >> 3).astype(F32)
    neg = -jnp.inf
    far = float(LANES)
    is_g = (lane_i >= N_EXPERTS) & (lane_i < N_EXPERTS + N_EXPERT_GROUPS)
    gl = jnp.where(is_g, logits, neg)
    g_max = jnp.max(gl, axis=-1, keepdims=True)
    g_lane = jnp.min(jnp.where(gl == g_max, lane, far), axis=-1, keepdims=True)
    g_gate = 1.0 / jnp.sum(jnp.exp(gl - g_max), axis=-1, keepdims=True)
    g_idx = g_lane - float(N_EXPERTS)
    in_group = (lane_i < N_EXPERTS) & (group_of_lane == g_idx)
    el = jnp.where(in_group, logits, neg)
    v1 = jnp.max(el, axis=-1, keepdims=True)
    i1 = jnp.min(jnp.where(el == v1, lane, far), axis=-1, keepdims=True)
    el2 = jnp.where(lane == i1, neg, el)
    v2 = jnp.max(el2, axis=-1, keepdims=True)
    i2 = jnp.min(jnp.where(el2 == v2, lane, far), axis=-1, keepdims=True)
    e2 = jnp.exp(v2 - v1)
    w1 = g_gate / (1.0 + e2)
    w2 = g_gate * e2 / (1.0 + e2)
    return i1, i2, w1, w2


ROUTE_E1, ROUTE_E2, ROUTE_W1, ROUTE_W2, ROUTE_RANK1, ROUTE_RANK2 = range(6)


def _outproj_kernel(x_ref, ys_ref, u_ref, yb_ref, d_ref, wglu_ref, wout_ref, nf_ref, rw_ref, rb_ref,
                    x1_ref, hn_ref, route_ref, rt_ref, cnt_ref, wglu_bf, wout_bf, rw_bf):
    @pl.when(pl.program_id(0) == 0)
    def _():
        cnt_ref[...] = jnp.zeros_like(cnt_ref)
        rw = rw_ref[...]
        rw_hi = rw.astype(BF16)
        rw_bf[:, 0:LANES] = rw_hi
        rw_bf[:, LANES:] = (rw - rw_hi.astype(F32)).astype(BF16)

    _cast_weight_once(wglu_ref, wglu_bf)
    _cast_weight_once(wout_ref, wout_bf)
    ys = jnp.concatenate([ys_ref[q] for q in range(N_SLABS)], axis=1)
    u = jnp.concatenate([u_ref[q] for q in range(N_SLABS)], axis=1)
    z = jax.nn.gelu(ys + d_ref[...] * u)
    ya = z * jax.nn.sigmoid(_bdot(z, wglu_bf[...]))
    mix = _bdot(ya, wout_bf[0:D_SSM, :]) + _bdot(yb_ref[...], wout_bf[D_SSM:, :])
    x1 = x_ref[...] + mix
    x1_ref[...] = x1
    hn = _rms(x1, nf_ref[...])
    _store_row_tiles(hn_ref, hn)
    h_hi = hn.astype(BF16)
    h_lo = (hn - h_hi.astype(F32)).astype(BF16)
    hw = jnp.dot(h_hi, rw_bf[...], preferred_element_type=F32)
    logits = (hw[:, :LANES] + hw[:, LANES:]
              + jnp.dot(h_lo, rw_bf[:, 0:LANES], preferred_element_type=F32) + rb_ref[...])
    i1, i2, w1, w2 = _route(logits)
    tm = logits.shape[0]
    lane_i = lax.broadcasted_iota(jnp.int32, logits.shape, 1)
    lane = lane_i.astype(F32)
    picked = jnp.where((lane == i1) | (lane == i2), 1.0, 0.0)
    earlier = (lax.broadcasted_iota(jnp.int32, (tm, tm), 0) > lax.broadcasted_iota(jnp.int32, (tm, tm), 1))
    prefix = _bdot(jnp.where(earlier, 1.0, 0.0), picked.astype(BF16)) + cnt_ref[...]
    rank1 = jnp.sum(jnp.where(lane == i1, prefix, 0.0), axis=-1, keepdims=True)
    rank2 = jnp.sum(jnp.where(lane == i2, prefix, 0.0), axis=-1, keepdims=True)
    cnt_ref[...] += jnp.sum(picked, axis=0, keepdims=True)
    rec = jnp.zeros_like(logits)
    for lane_id, val in ((ROUTE_E1, i1), (ROUTE_E2, i2), (ROUTE_W1, w1), (ROUTE_W2, w2),
                         (ROUTE_RANK1, rank1), (ROUTE_RANK2, rank2)):
        rec = jnp.where(lane_i == lane_id, val, rec)
    route_ref[...] = rec
    rt_ref[...] = rec.T[0:SUBLANES, :]


def _outproj(x, ys, u, yb, d, wglu, wout, nf, rw, rb, layer, tm):
    n = x.shape[0]
    return pl.pallas_call(
        _outproj_kernel,
        grid=(n // tm,),
        in_specs=[_row_spec(tm, D_MODEL), _slab_spec(tm), _slab_spec(tm), _row_spec(tm, D_CONV),
                  _layer_spec(layer, 1, D_SSM), _layer_spec(layer, D_SSM, D_SSM),
                  _layer_spec(layer, D_MODEL, D_MODEL), _layer_spec(layer, 1, D_MODEL),
                  _const_spec((D_MODEL, LANES)), _const_spec((1, LANES))],
        out_specs=[_row_spec(tm, D_MODEL), _row_tile_spec(tm), _row_spec(tm, LANES),
                   pl.BlockSpec((SUBLANES, tm), lambda i: (0, i)), _const_spec((1, LANES))],
        out_shape=[jax.ShapeDtypeStruct((n, D_MODEL), F32), jax.ShapeDtypeStruct((n,) + ROW_TILE, F32),
                   jax.ShapeDtypeStruct((n, LANES), F32), jax.ShapeDtypeStruct((SUBLANES, n), F32),
                   jax.ShapeDtypeStruct((1, LANES), F32)],
        scratch_shapes=[pltpu.VMEM((D_SSM, D_SSM), BF16), pltpu.VMEM((D_MODEL, D_MODEL), BF16),
                        pltpu.VMEM((D_MODEL, 2 * LANES), BF16)],
        compiler_params=_params("arbitrary"),
        name="outproj_router",
    )(x, ys, u, yb, _vec(d), wglu, wout, _vec(nf), rw, rb)


PLAN_TILE_LANES = 2 * LANES
PLAN_EXPERT, PLAN_ROWS, PLAN_USED = range(3)


def _moe_plan_kernel(rt_ref, cnt_ref, slot_ref, tile_ref, *, tms):
    cnt = cnt_ref[...]
    padded = jnp.ceil(cnt * (1.0 / tms)) * float(tms)
    r = lax.broadcasted_iota(jnp.int32, (LANES, LANES), 0)
    c = lax.broadcasted_iota(jnp.int32, (LANES, LANES), 1)
    ends = jnp.dot(padded, jnp.where(r <= c, 1.0, 0.0), precision=lax.Precision.HIGHEST,
                   preferred_element_type=F32)
    starts = ends - padded
    rt = rt_ref[...]
    e1, e2 = rt[ROUTE_E1:ROUTE_E1 + 1], rt[ROUTE_E2:ROUTE_E2 + 1]
    s1, s2 = rt[ROUTE_RANK1:ROUTE_RANK1 + 1], rt[ROUTE_RANK2:ROUTE_RANK2 + 1]
    tile = lax.broadcasted_iota(jnp.int32, (1, PLAN_TILE_LANES), 1).astype(F32)
    used = ends[:, N_EXPERTS - 1:N_EXPERTS] * (1.0 / tms)
    pos = jnp.minimum(tile, used - 1.0) * float(tms)
    t_exp = jnp.zeros_like(tile)
    t_fill = jnp.zeros_like(tile)
    for e in range(N_EXPERTS):
        st, en = starts[:, e:e + 1], ends[:, e:e + 1]
        s1 = s1 + jnp.where(e1 == float(e), st, 0.0)
        s2 = s2 + jnp.where(e2 == float(e), st, 0.0)
        mine = (pos >= st) & (pos < en)
        t_exp = t_exp + jnp.where(mine, float(e), 0.0)
        t_fill = t_fill + jnp.where(mine, st + cnt[:, e:e + 1], 0.0)
    t_rows = jnp.where(tile < used, jnp.clip(t_fill - pos, 0.0, float(tms)), 0.0)
    slot_ref[...] = jnp.concatenate([s1, s2], axis=0).astype(jnp.int32)
    tile_ref[...] = jnp.concatenate(
        [t_exp, t_rows, jnp.broadcast_to(used, tile.shape), jnp.zeros((SUBLANES - 3, PLAN_TILE_LANES), F32)],
        axis=0).astype(jnp.int32)


def _moe_plan(route_t, counts, tms, n_tiles):
    n = route_t.shape[1]
    assert n_tiles <= PLAN_TILE_LANES
    slots, tiles = pl.pallas_call(
        functools.partial(_moe_plan_kernel, tms=tms),
        out_shape=[jax.ShapeDtypeStruct((2, n), jnp.int32),
                   jax.ShapeDtypeStruct((SUBLANES, PLAN_TILE_LANES), jnp.int32)],
        compiler_params=pltpu.CompilerParams(vmem_limit_bytes=VMEM_LIMIT),
        name="moe_plan",
    )(route_t, counts)
    return slots, tiles[PLAN_EXPERT, :n_tiles], tiles[PLAN_ROWS, :n_tiles], tiles[PLAN_USED, :1]


DMA_UNROLL = 8


HN_BUFFERS = 3


def _dispatch_kernel(tr_ref, slot_ref, hn_ref, xs_ref, zbuf, hbuf, in_sem, out_sem, zsem,
                     *, tm, tms, n_tiles, n_steps):
    i = pl.program_id(0)

    def fetch(t):
        b = lax.rem(t, HN_BUFFERS)
        return pltpu.make_async_copy(hn_ref.at[pl.ds(pl.multiple_of(t * tm, tm), tm)], hbuf.at[b], in_sem.at[b])

    def drain_scatter(t):
        b = lax.rem(t, HN_BUFFERS)
        for k in range(2):
            pltpu.make_async_copy(hbuf.at[b], xs_ref.at[pl.ds(0, tm)], out_sem.at[b]).wait()

    @pl.when(i == 0)
    def _():
        fetch(i).start()
        zbuf[...] = jnp.zeros_like(zbuf)

        def fill(t, carry):
            @pl.when(tr_ref[t] < tms)
            def _():
                pltpu.make_async_copy(zbuf, xs_ref.at[pl.ds(pl.multiple_of(t * tms, tms), tms)], zsem).start()
            return carry

        def drain(t, carry):
            @pl.when(tr_ref[t] < tms)
            def _():
                pltpu.make_async_copy(zbuf, xs_ref.at[pl.ds(0, tms)], zsem).wait()
            return carry

        lax.fori_loop(0, n_tiles, fill, 0)
        lax.fori_loop(0, n_tiles, drain, 0)

    @pl.when(i + 1 < n_steps)
    def _():
        fetch(i + 1).start()

    fetch(i).wait()
    b = lax.rem(i, HN_BUFFERS)
    rows = hbuf.at[b]

    def issue(r, carry):
        for k in range(2):
            pltpu.make_async_copy(rows.at[pl.ds(r, 1)], xs_ref.at[pl.ds(slot_ref[k, r], 1)],
                                  out_sem.at[b]).start(priority=k)
        return carry

    lax.fori_loop(0, tm, issue, 0, unroll=DMA_UNROLL)

    @pl.when(i >= 1)
    def _():
        drain_scatter(i - 1)

    @pl.when(i == n_steps - 1)
    def _():
        drain_scatter(i)


def _dispatch(hn, slots, tile_rows, tms, tm):
    n = hn.shape[0]
    n_tiles = tile_rows.shape[0]
    return pl.pallas_call(
        functools.partial(_dispatch_kernel, tm=tm, tms=tms, n_tiles=n_tiles, n_steps=n // tm),
        grid_spec=pltpu.PrefetchScalarGridSpec(
            num_scalar_prefetch=1, grid=(n // tm,),
            in_specs=[pl.BlockSpec((2, tm), lambda i, tr: (0, i), memory_space=pltpu.SMEM),
                      pl.BlockSpec(memory_space=pl.ANY)],
            out_specs=pl.BlockSpec(memory_space=pl.ANY),
            scratch_shapes=[pltpu.VMEM((tms,) + ROW_TILE, F32), pltpu.VMEM((HN_BUFFERS, tm) + ROW_TILE, F32),
                            pltpu.SemaphoreType.DMA((HN_BUFFERS,)), pltpu.SemaphoreType.DMA((HN_BUFFERS,)),
                            pltpu.SemaphoreType.DMA]),
        out_shape=jax.ShapeDtypeStruct((n_tiles * tms,) + ROW_TILE, F32),
        compiler_params=_params("arbitrary"),
        name="moe_dispatch",
    )(tile_rows, slots, hn)


def _moe_kernel(te_ref, nu_ref, x_ref, wg_ref, wu_ref, wd_ref, y_ref, wg_bf, wu_bf, wd_bf):
    i = pl.program_id(0)
    in_use = i < nu_ref[0]
    new_expert = (i == 0) | (te_ref[i] != te_ref[jnp.maximum(i - 1, 0)])

    @pl.when(in_use & new_expert)
    def _():
        wg_bf[...] = wg_ref[...].astype(BF16)
        wu_bf[...] = wu_ref[...].astype(BF16)
        wd_bf[...] = wd_ref[...].astype(BF16)

    @pl.when(in_use)
    def _():
        h = _load_row_tiles(x_ref).astype(BF16)
        hg = jnp.dot(h, wg_bf[...], preferred_element_type=F32)
        hu = jnp.dot(h, wu_bf[...], preferred_element_type=F32)
        _store_row_tiles(y_ref, _bdot(hg * jax.nn.sigmoid(hg) * hu, wd_bf[...]))

    @pl.when(jnp.logical_not(in_use))
    def _():
        y_ref[...] = jnp.zeros_like(y_ref)


def _moe(xs, tile_expert, n_used, wg, wu, wd, layer, tms):
    n_slots = xs.shape[0]
    rows = pl.BlockSpec((tms,) + ROW_TILE, lambda i, te, nu: (jnp.minimum(i, nu[0] - 1), 0, 0))
    out_rows = pl.BlockSpec((tms,) + ROW_TILE, lambda i, te, nu: (i, 0, 0))

    def wspec(a, b):
        return pl.BlockSpec((None, None, a, b), lambda i, te, nu: (layer, te[i], 0, 0))

    return pl.pallas_call(
        _moe_kernel,
        grid_spec=pltpu.PrefetchScalarGridSpec(
            num_scalar_prefetch=2, grid=(n_slots // tms,),
            in_specs=[rows, wspec(D_MODEL, D_EXPERT), wspec(D_MODEL, D_EXPERT), wspec(D_EXPERT, D_MODEL)],
            out_specs=out_rows,
            scratch_shapes=[pltpu.VMEM((D_MODEL, D_EXPERT), BF16), pltpu.VMEM((D_MODEL, D_EXPERT), BF16),
                            pltpu.VMEM((D_EXPERT, D_MODEL), BF16)]),
        out_shape=jax.ShapeDtypeStruct((n_slots,) + ROW_TILE, F32),
        compiler_params=_params("arbitrary"),
        name="moe",
    )(tile_expert, n_used, xs, wg, wu, wd)


def _ple_kernel(slot_ref, next_slot_ref, x_ref, route_ref, p_ref, np_ref, wple_ref, wgate_ref, nfin_ref,
                ys_ref, o_ref, ybuf, sem, wple_bf, wgate_bf, *, tm, n_steps, final):
    i = pl.program_id(0)
    _cast_weight_once(wple_ref, wple_bf)
    _cast_weight_once(wgate_ref, wgate_bf)

    def gather(slots, b):
        def issue(r, carry):
            for k in range(2):
                pltpu.make_async_copy(ys_ref.at[pl.ds(slots[k, r], 1)], ybuf.at[b, k, pl.ds(r, 1)],
                                      sem.at[b]).start(priority=k)
            return carry

        lax.fori_loop(0, tm, issue, 0, unroll=DMA_UNROLL)

    @pl.when(i == 0)
    def _():
        gather(slot_ref, 0)

    @pl.when(i + 1 < n_steps)
    def _():
        gather(next_slot_ref, lax.rem(i + 1, 2))

    pe = _bdot(p_ref[...], wple_bf[...])
    b = lax.rem(i, 2)
    for k in range(2):
        pltpu.make_async_copy(ys_ref.at[pl.ds(0, tm)], ybuf.at[b, k], sem.at[b]).wait()
    route = route_ref[...]
    x = (x_ref[...] + route[:, ROUTE_W1:ROUTE_W1 + 1] * _load_row_tiles(ybuf.at[b, 0])
         + route[:, ROUTE_W2:ROUTE_W2 + 1] * _load_row_tiles(ybuf.at[b, 1]))
    gate = jax.nn.sigmoid(_bdot(_rms(x, np_ref[...]), wgate_bf[...]))
    out = x + pe * gate
    if final:
        out = _rms(out, nfin_ref[...])
    o_ref[...] = out


def _ple(x, route, slots, ys, p, npl, wple, wgate, nfin, layer, tm, final):
    n = x.shape[0]
    n_steps = n // tm
    return pl.pallas_call(
        functools.partial(_ple_kernel, tm=tm, n_steps=n_steps, final=final),
        grid=(n_steps,),
        in_specs=[pl.BlockSpec((2, tm), lambda i: (0, i), memory_space=pltpu.SMEM),
                  pl.BlockSpec((2, tm), lambda i: (0, jnp.minimum(i + 1, n_steps - 1)),
                               memory_space=pltpu.SMEM),
                  _row_spec(tm, D_MODEL), _row_spec(tm, LANES),
                  pl.BlockSpec((None, tm, D_PLE), lambda i: (layer, i, 0)),
                  _layer_spec(layer, 1, D_MODEL), _layer_spec(layer, D_PLE, D_MODEL),
                  _layer_spec(layer, D_MODEL, D_MODEL), _const_spec((1, D_MODEL)),
                  pl.BlockSpec(memory_space=pl.ANY)],
        out_specs=_row_spec(tm, D_MODEL),
        out_shape=jax.ShapeDtypeStruct((n, D_MODEL), F32),
        scratch_shapes=[pltpu.VMEM((2, 2, tm) + ROW_TILE, F32), pltpu.SemaphoreType.DMA((2,)),
                        pltpu.VMEM((D_PLE, D_MODEL), BF16), pltpu.VMEM((D_MODEL, D_MODEL), BF16)],
        compiler_params=_params("arbitrary"),
        name="combine_ple",
    )(slots, slots, x, route, p, _vec(npl), wple, wgate, nfin.reshape(1, D_MODEL), ys)


def _layer(x, p, h0_re, h0_im, conv_left, w, s5, router_w, router_b, layer, bsz, t_len, tm, tm_conv, tms, final):
    n = bsz * t_len
    u, z = _inproj(x, w["norm_mix"], w["w_in"], layer, tm)
    ys, hf_re, hf_im = _s5_mixer(u, h0_re, h0_im, s5, bsz, t_len)
    yb, conv_new = _conv_mixer(z, conv_left, w["conv_w"], w["conv_b"], w["conv_ln_g"], w["conv_ln_b"],
                               layer, bsz, t_len, tm_conv)
    x1, hn, route, route_t, counts = _outproj(x, ys, u, yb, w["ssm_d"], w["w_ssm_glu"], w["w_out"],
                                              w["norm_ffn"], router_w, router_b, layer, tm)
    n_tiles = 2 * n // tms + N_EXPERTS
    slots, tile_expert, tile_rows, n_used = _moe_plan(route_t, counts, tms, n_tiles)
    xs = _dispatch(hn, slots, tile_rows, tms, tm)
    ysort = _moe(xs, tile_expert, n_used, w["expert_w_gate"], w["expert_w_up"], w["expert_w_down"],
                 layer, tms)
    x3 = _ple(x1, route, slots, ysort, p, w["norm_ple"], w["ple_w"], w["ple_gate_w"], w["norm_final"],
              layer, tm, final)
    return x3, hf_re, hf_im, conv_new


def kernel(x_prompt, x_sample, p_prompt, p_sample, state_ssm_re, state_ssm_im, cache_conv, norm_mix, w_in, ssm_a_re, ssm_a_im, ssm_b_re, ssm_b_im, ssm_c_re, ssm_c_im, ssm_d, ssm_log_dt, w_ssm_glu, conv_w, conv_b, conv_ln_g, conv_ln_b, w_out, norm_ffn, router_group_w, router_group_b, router_expert_w, router_expert_b, expert_w_gate, expert_w_up, expert_w_down, norm_ple, ple_w, ple_gate_w, norm_final):
    depth = w_in.shape[0]
    bp, tp, _ = x_prompt.shape
    bs, ts, _ = x_sample.shape
    xp = x_prompt.reshape(bp * tp, D_MODEL)
    xs = x_sample.reshape(bs * ts, D_MODEL)
    pp = p_prompt.reshape(depth, bp * tp, D_PLE)
    ps = p_sample.reshape(depth, bs * ts, D_PLE)
    zero_state = jnp.zeros((bp, N_GROUPS, SSM_STATE), F32)
    zero_conv = jnp.zeros((bp, CONV_WIDTH - 1, D_CONV), F32)
    pad_lanes = LANES - N_EXPERTS - N_EXPERT_GROUPS
    w = {"norm_mix": norm_mix, "w_in": w_in, "ssm_d": ssm_d, "w_ssm_glu": w_ssm_glu, "conv_w": conv_w,
         "conv_b": conv_b, "conv_ln_g": conv_ln_g, "conv_ln_b": conv_ln_b, "w_out": w_out,
         "norm_ffn": norm_ffn, "expert_w_gate": expert_w_gate, "expert_w_up": expert_w_up,
         "expert_w_down": expert_w_down, "norm_ple": norm_ple, "ple_w": ple_w, "ple_gate_w": ple_gate_w,
         "norm_final": norm_final}
    outs = {k: [] for k in ("pr_re", "pr_im", "pr_conv", "sm_re", "sm_im", "sm_conv")}
    for i in range(depth):
        s5 = _s5_prep(ssm_a_re[i], ssm_a_im[i], ssm_log_dt[i], ssm_b_re[i], ssm_b_im[i], ssm_c_re[i],
                      ssm_c_im[i])
        router_w = jnp.pad(jnp.concatenate([router_expert_w[i], router_group_w[i]], axis=1),
                           ((0, 0), (0, pad_lanes)))
        router_b = jnp.pad(jnp.concatenate([router_expert_b[i], router_group_b[i]]),
                           (0, pad_lanes)).reshape(1, LANES)
        final = i == depth - 1
        xp, hr, hi, cv = _layer(xp, pp, zero_state, zero_state, zero_conv, w, s5, router_w, router_b, i,
                                bp, tp, tm=512, tm_conv=512, tms=256, final=final)
        outs["pr_re"].append(hr); outs["pr_im"].append(hi); outs["pr_conv"].append(cv)
        xs, hr, hi, cv = _layer(xs, ps, state_ssm_re[i], state_ssm_im[i], cache_conv[i], w, s5, router_w,
                                router_b, i, bs, ts, tm=bs * ts, tm_conv=ts, tms=32, final=final)
        outs["sm_re"].append(hr); outs["sm_im"].append(hi); outs["sm_conv"].append(cv)
    return (xp.reshape(bp, tp, D_MODEL), xs.reshape(bs, ts, D_MODEL),
            jnp.stack(outs["pr_re"]), jnp.stack(outs["pr_im"]), jnp.stack(outs["pr_conv"]),
            jnp.stack(outs["sm_re"]), jnp.stack(outs["sm_im"]), jnp.stack(outs["sm_conv"]))
```

```python
import functools

import jax
import jax.numpy as jnp
from jax import lax
from jax.experimental import pallas as pl
from jax.experimental.pallas import tpu as pltpu

F32 = jnp.float32
BF16 = jnp.bfloat16

D_MODEL = 1024
D_SSM = 512
SSM_GROUP = 16
N_GROUPS = D_SSM // SSM_GROUP
N_PAIRS = N_GROUPS // 2
SSM_STATE = 64
D_CONV = 512
CONV_WIDTH = 31
CONV_HALO = 32
N_EXPERT_GROUPS = 4
EXPERTS_PER_GROUP = 8
N_EXPERTS = 32
D_EXPERT = 256
D_PLE = 256
EPS = 1e-6
S5_CHUNK = 16
LANES = 128
SUBLANES = 8
N_SLABS = D_SSM // LANES
GROUPS_PER_SLAB = LANES // SSM_GROUP
PAIRS_PER_SLAB = GROUPS_PER_SLAB // 2
VMEM_LIMIT = 56 * 1024 * 1024


def _params(*sem):
    return pltpu.CompilerParams(dimension_semantics=sem, vmem_limit_bytes=VMEM_LIMIT)


def _rms(x, g):
    return x * lax.rsqrt(jnp.mean(x * x, axis=-1, keepdims=True) + EPS) * g


def _bdot(a, b):
    return jnp.dot(a.astype(BF16), b, preferred_element_type=F32)


def _row_spec(tm, width):
    return pl.BlockSpec((tm, width), lambda i: (i, 0))


ROW_SUB = D_MODEL // LANES


def _tiled_rows(n):
    return (n * ROW_SUB, LANES)


def _row_tile_spec(tm):
    return pl.BlockSpec(_tiled_rows(tm), lambda i, *_: (i, 0))


def _one_row(ref, r):
    return ref.at[pl.ds(pl.multiple_of(r * ROW_SUB, ROW_SUB), ROW_SUB)]


def _store_row_tiles(ref, rows):
    n = rows.shape[0]
    for s in range(ROW_SUB):
        ref[pl.ds(s, n, stride=ROW_SUB), :] = rows[:, s * LANES:(s + 1) * LANES]


def _load_row_tiles(ref):
    n = ref.shape[0] // ROW_SUB
    return jnp.concatenate([ref[pl.ds(s, n, stride=ROW_SUB), :] for s in range(ROW_SUB)], axis=1)


def _slab_spec(tm):
    return pl.BlockSpec((N_SLABS, tm, LANES), lambda i: (0, i, 0))


def _const_spec(shape):
    return pl.BlockSpec(shape, lambda i: (0,) * len(shape))


def _layer_spec(layer, *shape):
    return pl.BlockSpec((None,) + shape, lambda *_: (layer,) + (0,) * len(shape))


def _vec(stacked):
    return stacked.reshape(stacked.shape[0], 1, stacked.shape[1])


def _cast_weight_once(w_ref, wbf_ref):
    @pl.when(pl.program_id(0) == 0)
    def _():
        wbf_ref[...] = w_ref[...].astype(BF16)


def _inproj_kernel(x_ref, g_ref, w_ref, u_ref, z_ref, wbf):
    _cast_weight_once(w_ref, wbf)
    hn = _rms(x_ref[...], g_ref[...])
    proj = _bdot(hn, wbf[...])
    for q in range(N_SLABS):
        u_ref[q] = proj[:, q * LANES:(q + 1) * LANES]
    z_ref[...] = proj[:, D_SSM:]


def _inproj(x, g, w, layer, tm):
    n = x.shape[0]
    d_in = w.shape[2]
    return pl.pallas_call(
        _inproj_kernel,
        grid=(n // tm,),
        in_specs=[_row_spec(tm, D_MODEL), _layer_spec(layer, 1, D_MODEL), _layer_spec(layer, D_MODEL, d_in)],
        out_specs=[_slab_spec(tm), _row_spec(tm, d_in - D_SSM)],
        out_shape=[jax.ShapeDtypeStruct((N_SLABS, n, LANES), F32),
                   jax.ShapeDtypeStruct((n, d_in - D_SSM), F32)],
        scratch_shapes=[pltpu.VMEM((D_MODEL, d_in), BF16)],
        compiler_params=_params("arbitrary"),
        name="inproj",
    )(x, _vec(g), w)


def _s5_prep_kernel(ar_ref, ai_ref, ldt_ref, bre_ref, bim_ref, cre_ref, cim_ref,
                    m_ref, ws_ref, wot_ref, atab_ref, wt_re, wt_im, br_re, br_im):
    n_tap = S5_CHUNK * SSM_GROUP
    st = 4 * SSM_STATE
    nt = (((1,), (1,)), ((), ()))
    hi = lax.Precision.HIGHEST
    src = lax.broadcasted_iota(jnp.int32, (SSM_STATE, st), 0)
    dst = lax.broadcasted_iota(jnp.int32, (SSM_STATE, st), 1)
    lane = lax.broadcasted_iota(jnp.int32, (SSM_GROUP, n_tap), 1)
    ws_rows, wot_rows, atab = [], [], jnp.zeros((2 * SUBLANES, st), F32)
    for gi in range(2):
        ar, ai = ar_ref[gi], ai_ref[gi]
        dt = jnp.exp(ldt_ref[gi])
        k = lax.broadcasted_iota(jnp.int32, (S5_CHUNK + SUBLANES, SSM_STATE), 0).astype(F32)
        mag = jnp.exp(k * (dt * ar))
        ang = k * (dt * ai)
        p_re, p_im = mag * jnp.cos(ang), mag * jnp.sin(ang)
        inv = 1.0 / (ar * ar + ai * ai)
        ab_re, ab_im = p_re[1:2], p_im[1:2]
        ia_re, ia_im = ar * inv, -ai * inv
        coef_re = (ab_re - 1.0) * ia_re - ab_im * ia_im
        coef_im = (ab_re - 1.0) * ia_im + ab_im * ia_re
        bre, bim = bre_ref[gi], bim_ref[gi]
        bb_re = coef_re * bre - coef_im * bim
        bb_im = coef_re * bim + coef_im * bre
        cre, cim = cre_ref[gi], cim_ref[gi]
        for kk in range(S5_CHUNK + 1):
            pr, pi = p_re[kk:kk + 1], p_im[kk:kk + 1]
            rows = slice(kk * SSM_GROUP, (kk + 1) * SSM_GROUP)
            wt_re[rows, :] = pr * cre - pi * cim
            wt_im[rows, :] = -pi * cre - pr * cim
            if kk < S5_CHUNK:
                back = slice((S5_CHUNK - 1 - kk) * SSM_GROUP, (S5_CHUNK - kk) * SSM_GROUP)
                br_re[back, :] = pr * bb_re - pi * bb_im
                br_im[back, :] = pi * bb_re + pr * bb_im
        kcat = (lax.dot_general(bb_re, wt_re[0:n_tap, :], nt, precision=hi, preferred_element_type=F32)
                + lax.dot_general(bb_im, wt_im[0:n_tap, :], nt, precision=hi, preferred_element_type=F32))
        for s in range(S5_CHUNK):
            shifted = kcat if s == 0 else pltpu.roll(kcat, s * SSM_GROUP, 1)
            m_ref[gi, s * SSM_GROUP:(s + 1) * SSM_GROUP, :] = jnp.where(
                lane >= s * SSM_GROUP, shifted, 0.0).astype(BF16)
        put_re = jnp.where(dst == src + gi * SSM_STATE, 1.0, 0.0)
        put_im = jnp.where(dst == src + (2 + gi) * SSM_STATE, 1.0, 0.0)

        def place(v_re, v_im):
            return (jnp.dot(v_re, put_re, precision=hi, preferred_element_type=F32)
                    + jnp.dot(v_im, put_im, precision=hi, preferred_element_type=F32))

        ws_rows.append(place(br_re[...], br_im[...]))
        wot_rows.append(place(wt_re[SSM_GROUP:, :], wt_im[SSM_GROUP:, :]))
        kc = float(S5_CHUNK) * lax.broadcasted_iota(jnp.int32, (2 * SUBLANES, SSM_STATE), 0).astype(F32)
        magc = jnp.exp(kc * (dt * ar))
        angc = kc * (dt * ai)
        atab = atab + place(magc * jnp.cos(angc), magc * jnp.sin(angc))
    ws_ref[...] = jnp.concatenate(ws_rows, axis=0).astype(BF16)
    wot_ref[...] = jnp.concatenate(wot_rows, axis=0).astype(BF16)
    atab_ref[...] = atab


def _s5_prep(a_re, a_im, log_dt, b_re, b_im, c_re, c_im):
    p, n, c = N_PAIRS, SSM_STATE, SSM_GROUP
    n_tap = S5_CHUNK * c
    st = 4 * n

    def pspec(*shape):
        return pl.BlockSpec((None,) + shape, lambda i: (i,) + (0,) * len(shape))

    def pairs(a, *shape):
        return a.reshape((p, 2) + shape)

    return pl.pallas_call(
        _s5_prep_kernel,
        grid=(p,),
        in_specs=[pspec(2, 1, n), pspec(2, 1, n), pspec(2, 1, 1), pspec(2, c, n), pspec(2, c, n),
                  pspec(2, c, n), pspec(2, c, n)],
        out_specs=[pspec(2, n_tap, n_tap), pspec(2 * n_tap, st), pspec(2 * n_tap, st), pspec(2 * SUBLANES, st)],
        out_shape=[jax.ShapeDtypeStruct((p, 2, n_tap, n_tap), BF16),
                   jax.ShapeDtypeStruct((p, 2 * n_tap, st), BF16),
                   jax.ShapeDtypeStruct((p, 2 * n_tap, st), BF16),
                   jax.ShapeDtypeStruct((p, 2 * SUBLANES, st), F32)],
        scratch_shapes=[pltpu.VMEM((n_tap + c, n), F32), pltpu.VMEM((n_tap + c, n), F32),
                        pltpu.VMEM((n_tap, n), F32), pltpu.VMEM((n_tap, n), F32)],
        compiler_params=_params("parallel"),
        name="s5_prep",
    )(pairs(a_re, 1, n), pairs(a_im, 1, n), pairs(log_dt, 1, 1),
      pairs(jnp.swapaxes(b_re, 1, 2), c, n), pairs(jnp.swapaxes(b_im, 1, 2), c, n),
      pairs(c_re, c, n), pairs(c_im, c, n))


def _block_transpose8(vs):
    lane = lax.broadcasted_iota(jnp.int32, vs[0].shape, 1)
    blk = lane >> 4
    for d in (4, 2, 1):
        keep = (blk & d) == 0
        new = list(vs)
        for i in range(GROUPS_PER_SLAB):
            if i & d == 0:
                a, b = vs[i], vs[i + d]
                new[i] = jnp.where(keep, a, pltpu.roll(b, d * SSM_GROUP, 1))
                new[i + d] = jnp.where(keep, pltpu.roll(a, LANES - d * SSM_GROUP, 1), b)
        vs = new
    return vs


def _cmul(ar, ai, xr, xi):
    return ar * xr - ai * xi, ar * xi + ai * xr


def _s5_kernel(u_ref, h0_ref, m_ref, ws_ref, wo_ref, a_ref, y_ref, hf_ref, x_scr, yg_scr, s_scr, hp_scr,
               *, rows, independent):
    half = 2 * SSM_STATE
    rt = min(rows, 16 * SUBLANES)
    half_chunk = S5_CHUNK // 2

    def gather_tile(t, carry):
        r0 = pl.multiple_of(t * rt, rt)
        for hf in range(2):
            vs = [u_ref[pl.ds(r0 * S5_CHUNK + hf * half_chunk + i, rt, stride=S5_CHUNK), :]
                  for i in range(half_chunk)]
            outs = _block_transpose8(vs)
            for g in range(GROUPS_PER_SLAB):
                x_scr[g, pl.ds(r0, rt), hf * LANES:(hf + 1) * LANES] = outs[g]
        return carry

    lax.fori_loop(0, rows // rt, gather_tile, 0)

    row = lax.broadcasted_iota(jnp.int32, (SUBLANES, half), 0)
    n_tap = S5_CHUNK * SSM_GROUP
    for pi in range(PAIRS_PER_SLAB):
        x0 = x_scr[2 * pi].astype(BF16)
        x1 = x_scr[2 * pi + 1].astype(BF16)
        s_scr[...] = jnp.dot(jnp.concatenate([x0, x1], axis=1), ws_ref[pi], preferred_element_type=F32)
        ap = a_ref[pi]
        h0 = h0_ref[pi]
        if independent:
            hp_scr[...] = h0
            s = s_scr[...]
            n_re, n_im = _cmul(ap[1:2, :half], ap[1:2, half:], h0[:, :half], h0[:, half:])
            hf_ref[pi] = jnp.concatenate([n_re + s[:, :half], n_im + s[:, half:]], axis=1)
        else:
            pw_re, pw_im = ap[0:SUBLANES, :half], ap[0:SUBLANES, half:]

            def scan_tile(t, carry):
                h_re, h_im = carry
                r0 = pl.multiple_of(t * SUBLANES, SUBLANES)
                s = s_scr[pl.ds(r0, SUBLANES), :]
                t_re, t_im = s[:, :half], s[:, half:]
                for d in (1, 2, 4):
                    sh_re = jnp.where(row >= d, pltpu.roll(t_re, d, 0), 0.0)
                    sh_im = jnp.where(row >= d, pltpu.roll(t_im, d, 0), 0.0)
                    m_re, m_im = _cmul(ap[d:d + 1, :half], ap[d:d + 1, half:], sh_re, sh_im)
                    t_re, t_im = t_re + m_re, t_im + m_im
                e_re = jnp.where(row >= 1, pltpu.roll(t_re, 1, 0), 0.0)
                e_im = jnp.where(row >= 1, pltpu.roll(t_im, 1, 0), 0.0)
                c_re, c_im = _cmul(pw_re, pw_im, h_re, h_im)
                hp_scr[pl.ds(r0, SUBLANES), :] = jnp.concatenate([e_re + c_re, e_im + c_im], axis=1)
                o_re, o_im = _cmul(ap[SUBLANES:SUBLANES + 1, :half], ap[SUBLANES:SUBLANES + 1, half:],
                                   h_re, h_im)
                last = SUBLANES - 1
                n_re = jnp.broadcast_to(t_re[last:last + 1], h_re.shape) + o_re
                n_im = jnp.broadcast_to(t_im[last:last + 1], h_im.shape) + o_im
                return n_re, n_im

            init = (jnp.broadcast_to(h0[:, :half], (SUBLANES, half)),
                    jnp.broadcast_to(h0[:, half:], (SUBLANES, half)))
            h_re, h_im = lax.fori_loop(0, rows // SUBLANES, scan_tile, init, unroll=4)
            hf_ref[pi] = jnp.concatenate([h_re[0:1], h_im[0:1]], axis=1)
        yc = lax.dot_general(hp_scr[...].astype(BF16), wo_ref[pi], (((1,), (1,)), ((), ())),
                             preferred_element_type=F32)
        yg_scr[2 * pi] = jnp.dot(x0, m_ref[pi, 0], preferred_element_type=F32) + yc[:, :n_tap]
        yg_scr[2 * pi + 1] = jnp.dot(x1, m_ref[pi, 1], preferred_element_type=F32) + yc[:, n_tap:]

    def scatter_tile(t, carry):
        r0 = pl.multiple_of(t * rt, rt)
        for hf in range(2):
            vs = [yg_scr[g, pl.ds(r0, rt), hf * LANES:(hf + 1) * LANES] for g in range(GROUPS_PER_SLAB)]
            outs = _block_transpose8(vs)
            for i in range(half_chunk):
                y_ref[pl.ds(r0 * S5_CHUNK + hf * half_chunk + i, rt, stride=S5_CHUNK), :] = outs[i]
        return carry

    lax.fori_loop(0, rows // rt, scatter_tile, 0)


def _s5_mixer(u, h0_re, h0_im, prep, bsz, t_len):
    m, wsp, wop, a16 = prep
    n_tap = S5_CHUNK * SSM_GROUP
    st = 4 * SSM_STATE
    independent = t_len == S5_CHUNK
    if independent:
        nblk, rows, hrows = 1, bsz, bsz
    else:
        nblk, rows, hrows = bsz, t_len // S5_CHUNK, 1
    assert t_len % S5_CHUNK == 0 and rows % SUBLANES == 0, (bsz, t_len)
    h0p = jnp.concatenate([h0_re.reshape(bsz, N_PAIRS, 2 * SSM_STATE),
                           h0_im.reshape(bsz, N_PAIRS, 2 * SSM_STATE)], axis=2).astype(F32)
    h0p = h0p.transpose(1, 0, 2)[None] if independent else h0p[:, :, None, :]
    pp = PAIRS_PER_SLAB

    def wspec(*shape):
        return pl.BlockSpec((pp,) + shape, lambda q, b: (q,) + (0,) * len(shape))

    frames = rows * S5_CHUNK
    y, hf = pl.pallas_call(
        functools.partial(_s5_kernel, rows=rows, independent=independent),
        grid=(N_SLABS, nblk),
        in_specs=[pl.BlockSpec((None, frames, LANES), lambda q, b: (q, b, 0)),
                  pl.BlockSpec((None, pp, hrows, st), lambda q, b: (b, q, 0, 0)),
                  wspec(2, n_tap, n_tap), wspec(2 * n_tap, st), wspec(2 * n_tap, st), wspec(2 * SUBLANES, st)],
        out_specs=[pl.BlockSpec((None, frames, LANES), lambda q, b: (q, b, 0)),
                   pl.BlockSpec((None, pp, hrows, st), lambda q, b: (b, q, 0, 0))],
        out_shape=[jax.ShapeDtypeStruct(u.shape, F32),
                   jax.ShapeDtypeStruct((nblk, N_PAIRS, hrows, st), F32)],
        scratch_shapes=[pltpu.VMEM((GROUPS_PER_SLAB, rows, n_tap), F32),
                        pltpu.VMEM((GROUPS_PER_SLAB, rows, n_tap), F32),
                        pltpu.VMEM((rows, st), F32), pltpu.VMEM((rows, st), F32)],
        compiler_params=_params("parallel", "parallel"),
        name="s5_core",
    )(u, h0p, m, wsp, wop, a16)
    hf = hf[0].transpose(1, 0, 2) if independent else hf[:, :, 0, :]
    hf_re = hf[:, :, :2 * SSM_STATE].reshape(bsz, N_GROUPS, SSM_STATE)
    hf_im = hf[:, :, 2 * SSM_STATE:].reshape(bsz, N_GROUPS, SSM_STATE)
    return y, hf_re, hf_im


def _conv_kernel(z_ref, left_ref, w_ref, b_ref, g_ref, beta_ref, y_ref, cn_ref, vbuf, shifted, *, tm):
    @pl.when(pl.program_id(1) == 0)
    def _():
        vbuf[0:CONV_HALO, :] = left_ref[...]

    z = z_ref[...]
    vbuf[CONV_HALO:CONV_HALO + tm, :] = z[:, :D_CONV] * jax.nn.sigmoid(z[:, D_CONV:])
    first = CONV_HALO - (CONV_WIDTH - 1)
    span = tm + CONV_HALO - SUBLANES
    for r in range(1, SUBLANES):
        shifted[r - 1, 0:span, :] = vbuf[r:r + span, :]
    acc = jnp.zeros((tm, D_CONV), F32)
    for k in range(CONV_WIDTH):
        a, r = divmod(first + k, SUBLANES)
        src = vbuf if r == 0 else shifted.at[r - 1]
        acc = acc + w_ref[k:k + 1, :] * src[a * SUBLANES:a * SUBLANES + tm, :]
    y = acc + b_ref[...]
    mu = jnp.mean(y, axis=-1, keepdims=True)
    yc = y - mu
    var = jnp.mean(yc * yc, axis=-1, keepdims=True)
    yn = yc * lax.rsqrt(var + EPS) * g_ref[...] + beta_ref[...]
    y_ref[...] = yn * jax.nn.sigmoid(yn)
    cn_ref[...] = vbuf[tm + first:tm + CONV_HALO, :]
    vbuf[0:CONV_HALO, :] = vbuf[tm:tm + CONV_HALO, :]


def _conv_mixer(z, left, w, b, g, beta, layer, bsz, t_len, tm):
    left = jnp.pad(left.astype(F32), ((0, 0), (CONV_HALO - (CONV_WIDTH - 1), 0), (0, 0)))
    nt = t_len // tm
    vec = _layer_spec(layer, 1, D_CONV)
    return pl.pallas_call(
        functools.partial(_conv_kernel, tm=tm),
        grid=(bsz, nt),
        in_specs=[pl.BlockSpec((tm, 2 * D_CONV), lambda bi, j: (bi * nt + j, 0)),
                  pl.BlockSpec((None, CONV_HALO, D_CONV), lambda bi, j: (bi, 0, 0)),
                  _layer_spec(layer, CONV_WIDTH, D_CONV), vec, vec, vec],
        out_specs=[pl.BlockSpec((tm, D_CONV), lambda bi, j: (bi * nt + j, 0)),
                   pl.BlockSpec((None, CONV_WIDTH - 1, D_CONV), lambda bi, j: (bi, 0, 0))],
        out_shape=[jax.ShapeDtypeStruct((bsz * t_len, D_CONV), F32),
                   jax.ShapeDtypeStruct((bsz, CONV_WIDTH - 1, D_CONV), F32)],
        scratch_shapes=[pltpu.VMEM((tm + CONV_HALO, D_CONV), F32),
                        pltpu.VMEM((SUBLANES - 1, tm + CONV_HALO - SUBLANES, D_CONV), F32)],
        compiler_params=_params("parallel", "arbitrary"),
        name="conv_mixer",
    )(z, left, w, _vec(b), _vec(g), _vec(beta))


def _route(logits):
    lane_i = lax.broadcasted_iota(jnp.int32, logits.shape, 1)
    lane = lane_i.astype(F32)
    group_of_lane = (lane_i >> 3).astype(F32)
    neg = -jnp.inf
    far = float(LANES)
    is_g = (lane_i >= N_EXPERTS) & (lane_i < N_EXPERTS + N_EXPERT_GROUPS)
    gl = jnp.where(is_g, logits, neg)
    g_max = jnp.max(gl, axis=-1, keepdims=True)
    g_lane = jnp.min(jnp.where(gl == g_max, lane, far), axis=-1, keepdims=True)
    g_gate = 1.0 / jnp.sum(jnp.exp(gl - g_max), axis=-1, keepdims=True)
    g_idx = g_lane - float(N_EXPERTS)
    in_group = (lane_i < N_EXPERTS) & (group_of_lane == g_idx)
    el = jnp.where(in_group, logits, neg)
    v1 = jnp.max(el, axis=-1, keepdims=True)
    i1 = jnp.min(jnp.where(el == v1, lane, far), axis=-1, keepdims=True)
    el2 = jnp.where(lane == i1, neg, el)
    v2 = jnp.max(el2, axis=-1, keepdims=True)
    i2 = jnp.min(jnp.where(el2 == v2, lane, far), axis=-1, keepdims=True)
    e2 = jnp.exp(v2 - v1)
    w1 = g_gate / (1.0 + e2)
    w2 = g_gate * e2 / (1.0 + e2)
    return i1, i2, w1, w2


ROUTE_E1, ROUTE_E2, ROUTE_W1, ROUTE_W2, ROUTE_RANK1, ROUTE_RANK2 = range(6)


def _outproj_kernel(x_ref, ys_ref, u_ref, yb_ref, d_ref, wglu_ref, wout_ref, nf_ref, rw_ref, rb_ref,
                    x1_ref, hn_ref, route_ref, rt_ref, cnt_ref, wglu_bf, wout_bf, rw_bf):
    @pl.when(pl.program_id(0) == 0)
    def _():
        cnt_ref[...] = jnp.zeros_like(cnt_ref)
        rw = rw_ref[...]
        rw_hi = rw.astype(BF16)
        rw_bf[:, 0:LANES] = rw_hi
        rw_bf[:, LANES:] = (rw - rw_hi.astype(F32)).astype(BF16)

    _cast_weight_once(wglu_ref, wglu_bf)
    _cast_weight_once(wout_ref, wout_bf)
    ys = jnp.concatenate([ys_ref[q] for q in range(N_SLABS)], axis=1)
    u = jnp.concatenate([u_ref[q] for q in range(N_SLABS)], axis=1)
    z = jax.nn.gelu(ys + d_ref[...] * u)
    ya = z * jax.nn.sigmoid(_bdot(z, wglu_bf[...]))
    mix = _bdot(ya, wout_bf[0:D_SSM, :]) + _bdot(yb_ref[...], wout_bf[D_SSM:, :])
    x1 = x_ref[...] + mix
    x1_ref[...] = x1
    hn = _rms(x1, nf_ref[...])
    _store_row_tiles(hn_ref, hn)
    h_hi = hn.astype(BF16)
    h_lo = (hn - h_hi.astype(F32)).astype(BF16)
    hw = jnp.dot(h_hi, rw_bf[...], preferred_element_type=F32)
    logits = (hw[:, :LANES] + hw[:, LANES:]
              + jnp.dot(h_lo, rw_bf[:, 0:LANES], preferred_element_type=F32) + rb_ref[...])
    i1, i2, w1, w2 = _route(logits)
    tm = logits.shape[0]
    lane_i = lax.broadcasted_iota(jnp.int32, logits.shape, 1)
    lane = lane_i.astype(F32)
    picked = jnp.where((lane == i1) | (lane == i2), 1.0, 0.0)
    earlier = (lax.broadcasted_iota(jnp.int32, (tm, tm), 0) > lax.broadcasted_iota(jnp.int32, (tm, tm), 1))
    prefix = _bdot(jnp.where(earlier, 1.0, 0.0), picked.astype(BF16)) + cnt_ref[...]
    rank1 = jnp.sum(jnp.where(lane == i1, prefix, 0.0), axis=-1, keepdims=True)
    rank2 = jnp.sum(jnp.where(lane == i2, prefix, 0.0), axis=-1, keepdims=True)
    cnt_ref[...] += jnp.sum(picked, axis=0, keepdims=True)
    rec = jnp.zeros_like(logits)
    for lane_id, val in ((ROUTE_E1, i1), (ROUTE_E2, i2), (ROUTE_W1, w1), (ROUTE_W2, w2),
                         (ROUTE_RANK1, rank1), (ROUTE_RANK2, rank2)):
        rec = jnp.where(lane_i == lane_id, val, rec)
    route_ref[...] = rec
    rt_ref[...] = rec.T[0:SUBLANES, :]


def _outproj(x, ys, u, yb, d, wglu, wout, nf, rw, rb, layer, tm):
    n = x.shape[0]
    return pl.pallas_call(
        _outproj_kernel,
        grid=(n // tm,),
        in_specs=[_row_spec(tm, D_MODEL), _slab_spec(tm), _slab_spec(tm), _row_spec(tm, D_CONV),
                  _layer_spec(layer, 1, D_SSM), _layer_spec(layer, D_SSM, D_SSM),
                  _layer_spec(layer, D_MODEL, D_MODEL), _layer_spec(layer, 1, D_MODEL),
                  _const_spec((D_MODEL, LANES)), _const_spec((1, LANES))],
        out_specs=[_row_spec(tm, D_MODEL), _row_tile_spec(tm), _row_spec(tm, LANES),
                   pl.BlockSpec((SUBLANES, tm), lambda i: (0, i)), _const_spec((1, LANES))],
        out_shape=[jax.ShapeDtypeStruct((n, D_MODEL), F32), jax.ShapeDtypeStruct(_tiled_rows(n), F32),
                   jax.ShapeDtypeStruct((n, LANES), F32), jax.ShapeDtypeStruct((SUBLANES, n), F32),
                   jax.ShapeDtypeStruct((1, LANES), F32)],
        scratch_shapes=[pltpu.VMEM((D_SSM, D_SSM), BF16), pltpu.VMEM((D_MODEL, D_MODEL), BF16),
                        pltpu.VMEM((D_MODEL, 2 * LANES), BF16)],
        compiler_params=_params("arbitrary"),
        name="outproj_router",
    )(x, ys, u, yb, _vec(d), wglu, wout, _vec(nf), rw, rb)


PLAN_TILE_LANES = 2 * LANES
PLAN_EXPERT, PLAN_ROWS, PLAN_USED = range(3)


def _moe_plan_kernel(rt_ref, cnt_ref, slot_ref, tile_ref, *, tms):
    cnt = cnt_ref[...]
    padded = jnp.ceil(cnt * (1.0 / tms)) * float(tms)
    r = lax.broadcasted_iota(jnp.int32, (LANES, LANES), 0)
    c = lax.broadcasted_iota(jnp.int32, (LANES, LANES), 1)
    ends = jnp.dot(padded, jnp.where(r <= c, 1.0, 0.0), precision=lax.Precision.HIGHEST,
                   preferred_element_type=F32)
    starts = ends - padded
    rt = rt_ref[...]
    e1, e2 = rt[ROUTE_E1:ROUTE_E1 + 1], rt[ROUTE_E2:ROUTE_E2 + 1]
    s1, s2 = rt[ROUTE_RANK1:ROUTE_RANK1 + 1], rt[ROUTE_RANK2:ROUTE_RANK2 + 1]
    tile = lax.broadcasted_iota(jnp.int32, (1, PLAN_TILE_LANES), 1).astype(F32)
    used = ends[:, N_EXPERTS - 1:N_EXPERTS] * (1.0 / tms)
    pos = jnp.minimum(tile, used - 1.0) * float(tms)
    t_exp = jnp.zeros_like(tile)
    t_fill = jnp.zeros_like(tile)
    for e in range(N_EXPERTS):
        st, en = starts[:, e:e + 1], ends[:, e:e + 1]
        s1 = s1 + jnp.where(e1 == float(e), st, 0.0)
        s2 = s2 + jnp.where(e2 == float(e), st, 0.0)
        mine = (pos >= st) & (pos < en)
        t_exp = t_exp + jnp.where(mine, float(e), 0.0)
        t_fill = t_fill + jnp.where(mine, st + cnt[:, e:e + 1], 0.0)
    t_rows = jnp.where(tile < used, jnp.clip(t_fill - pos, 0.0, float(tms)), 0.0)
    slot_ref[...] = jnp.concatenate([s1, s2], axis=0).astype(jnp.int32)
    tile_ref[...] = jnp.concatenate(
        [t_exp, t_rows, jnp.broadcast_to(used, tile.shape), jnp.zeros((SUBLANES - 3, PLAN_TILE_LANES), F32)],
        axis=0).astype(jnp.int32)


def _moe_plan(route_t, counts, tms, n_tiles):
    n = route_t.shape[1]
    assert n_tiles <= PLAN_TILE_LANES
    slots, tiles = pl.pallas_call(
        functools.partial(_moe_plan_kernel, tms=tms),
        out_shape=[jax.ShapeDtypeStruct((2, n), jnp.int32),
                   jax.ShapeDtypeStruct((SUBLANES, PLAN_TILE_LANES), jnp.int32)],
        compiler_params=pltpu.CompilerParams(vmem_limit_bytes=VMEM_LIMIT),
        name="moe_plan",
    )(route_t, counts)
    return slots, tiles[PLAN_EXPERT, :n_tiles], tiles[PLAN_ROWS, :n_tiles], tiles[PLAN_USED, :1]


DMA_UNROLL = 8


HN_BUFFERS = 3


def _dispatch_kernel(tr_ref, slot_ref, hn_ref, xs_ref, zbuf, hbuf, in_sem, out_sem, zsem,
                     *, tm, tms, n_tiles, n_steps):
    i = pl.program_id(0)

    def fetch(t):
        b = lax.rem(t, HN_BUFFERS)
        first = pl.multiple_of(t * (tm * ROW_SUB), tm * ROW_SUB)
        return pltpu.make_async_copy(hn_ref.at[pl.ds(first, tm * ROW_SUB)], hbuf.at[b], in_sem.at[b])

    def drain_scatter(t):
        b = lax.rem(t, HN_BUFFERS)
        for k in range(2):
            pltpu.make_async_copy(hbuf.at[b], xs_ref.at[pl.ds(0, tm * ROW_SUB)], out_sem.at[b]).wait()

    @pl.when(i == 0)
    def _():
        fetch(i).start()
        zbuf[...] = jnp.zeros_like(zbuf)

        def fill(t, carry):
            @pl.when(tr_ref[t] < tms)
            def _():
                first = pl.multiple_of(t * (tms * ROW_SUB), tms * ROW_SUB)
                pltpu.make_async_copy(zbuf, xs_ref.at[pl.ds(first, tms * ROW_SUB)], zsem).start()
            return carry

        def drain(t, carry):
            @pl.when(tr_ref[t] < tms)
            def _():
                pltpu.make_async_copy(zbuf, xs_ref.at[pl.ds(0, tms * ROW_SUB)], zsem).wait()
            return carry

        lax.fori_loop(0, n_tiles, fill, 0)
        lax.fori_loop(0, n_tiles, drain, 0)

    @pl.when(i + 1 < n_steps)
    def _():
        fetch(i + 1).start()

    fetch(i).wait()
    b = lax.rem(i, HN_BUFFERS)
    rows = hbuf.at[b]

    def issue(r, carry):
        for k in range(2):
            pltpu.make_async_copy(_one_row(rows, r), _one_row(xs_ref, slot_ref[k, r]),
                                  out_sem.at[b]).start(priority=k)
        return carry

    lax.fori_loop(0, tm, issue, 0, unroll=DMA_UNROLL)

    @pl.when(i >= 1)
    def _():
        drain_scatter(i - 1)

    @pl.when(i == n_steps - 1)
    def _():
        drain_scatter(i)


def _dispatch(hn, slots, tile_rows, tms, tm):
    n = hn.shape[0] // ROW_SUB
    n_tiles = tile_rows.shape[0]
    return pl.pallas_call(
        functools.partial(_dispatch_kernel, tm=tm, tms=tms, n_tiles=n_tiles, n_steps=n // tm),
        grid_spec=pltpu.PrefetchScalarGridSpec(
            num_scalar_prefetch=1, grid=(n // tm,),
            in_specs=[pl.BlockSpec((2, tm), lambda i, tr: (0, i), memory_space=pltpu.SMEM),
                      pl.BlockSpec(memory_space=pl.ANY)],
            out_specs=pl.BlockSpec(memory_space=pl.ANY),
            scratch_shapes=[pltpu.VMEM(_tiled_rows(tms), F32), pltpu.VMEM((HN_BUFFERS,) + _tiled_rows(tm), F32),
                            pltpu.SemaphoreType.DMA((HN_BUFFERS,)), pltpu.SemaphoreType.DMA((HN_BUFFERS,)),
                            pltpu.SemaphoreType.DMA]),
        out_shape=jax.ShapeDtypeStruct(_tiled_rows(n_tiles * tms), F32),
        compiler_params=_params("arbitrary"),
        name="moe_dispatch",
    )(tile_rows, slots, hn)


def _moe_kernel(te_ref, nu_ref, x_ref, wg_ref, wu_ref, wd_ref, y_ref, wg_bf, wu_bf, wd_bf):
    i = pl.program_id(0)
    in_use = i < nu_ref[0]
    new_expert = (i == 0) | (te_ref[i] != te_ref[jnp.maximum(i - 1, 0)])

    @pl.when(in_use & new_expert)
    def _():
        wg_bf[...] = wg_ref[...].astype(BF16)
        wu_bf[...] = wu_ref[...].astype(BF16)
        wd_bf[...] = wd_ref[...].astype(BF16)

    @pl.when(in_use)
    def _():
        h = _load_row_tiles(x_ref).astype(BF16)
        hg = jnp.dot(h, wg_bf[...], preferred_element_type=F32)
        hu = jnp.dot(h, wu_bf[...], preferred_element_type=F32)
        _store_row_tiles(y_ref, _bdot(hg * jax.nn.sigmoid(hg) * hu, wd_bf[...]))

    @pl.when(jnp.logical_not(in_use))
    def _():
        y_ref[...] = jnp.zeros_like(y_ref)


def _moe(xs, tile_expert, n_used, wg, wu, wd, layer, tms):
    n_slots = xs.shape[0] // ROW_SUB
    rows = pl.BlockSpec(_tiled_rows(tms), lambda i, te, nu: (jnp.minimum(i, nu[0] - 1), 0))
    out_rows = pl.BlockSpec(_tiled_rows(tms), lambda i, te, nu: (i, 0))

    def wspec(a, b):
        return pl.BlockSpec((None, None, a, b), lambda i, te, nu: (layer, te[i], 0, 0))

    return pl.pallas_call(
        _moe_kernel,
        grid_spec=pltpu.PrefetchScalarGridSpec(
            num_scalar_prefetch=2, grid=(n_slots // tms,),
            in_specs=[rows, wspec(D_MODEL, D_EXPERT), wspec(D_MODEL, D_EXPERT), wspec(D_EXPERT, D_MODEL)],
            out_specs=out_rows,
            scratch_shapes=[pltpu.VMEM((D_MODEL, D_EXPERT), BF16), pltpu.VMEM((D_MODEL, D_EXPERT), BF16),
                            pltpu.VMEM((D_EXPERT, D_MODEL), BF16)]),
        out_shape=jax.ShapeDtypeStruct(xs.shape, F32),
        compiler_params=_params("arbitrary"),
        name="moe",
    )(tile_expert, n_used, xs, wg, wu, wd)


def _ple_kernel(slot_ref, next_slot_ref, x_ref, route_ref, p_ref, np_ref, wple_ref, wgate_ref, nfin_ref,
                ys_ref, o_ref, ybuf, sem, wple_bf, wgate_bf, *, tm, n_steps, final):
    i = pl.program_id(0)
    _cast_weight_once(wple_ref, wple_bf)
    _cast_weight_once(wgate_ref, wgate_bf)

    def gather(slots, b):
        def issue(r, carry):
            for k in range(2):
                pltpu.make_async_copy(_one_row(ys_ref, slots[k, r]), _one_row(ybuf.at[b, k], r),
                                      sem.at[b]).start(priority=k)
            return carry

        lax.fori_loop(0, tm, issue, 0, unroll=DMA_UNROLL)

    @pl.when(i == 0)
    def _():
        gather(slot_ref, 0)

    @pl.when(i + 1 < n_steps)
    def _():
        gather(next_slot_ref, lax.rem(i + 1, 2))

    pe = _bdot(p_ref[...], wple_bf[...])
    b = lax.rem(i, 2)
    for k in range(2):
        pltpu.make_async_copy(ys_ref.at[pl.ds(0, tm * ROW_SUB)], ybuf.at[b, k], sem.at[b]).wait()
    route = route_ref[...]
    x = (x_ref[...] + route[:, ROUTE_W1:ROUTE_W1 + 1] * _load_row_tiles(ybuf.at[b, 0])
         + route[:, ROUTE_W2:ROUTE_W2 + 1] * _load_row_tiles(ybuf.at[b, 1]))
    gate = jax.nn.sigmoid(_bdot(_rms(x, np_ref[...]), wgate_bf[...]))
    out = x + pe * gate
    if final:
        out = _rms(out, nfin_ref[...])
    o_ref[...] = out


def _ple(x, route, slots, ys, p, npl, wple, wgate, nfin, layer, tm, final):
    n = x.shape[0]
    n_steps = n // tm
    return pl.pallas_call(
        functools.partial(_ple_kernel, tm=tm, n_steps=n_steps, final=final),
        grid=(n_steps,),
        in_specs=[pl.BlockSpec((2, tm), lambda i: (0, i), memory_space=pltpu.SMEM),
                  pl.BlockSpec((2, tm), lambda i: (0, jnp.minimum(i + 1, n_steps - 1)),
                               memory_space=pltpu.SMEM),
                  _row_spec(tm, D_MODEL), _row_spec(tm, LANES),
                  pl.BlockSpec((None, tm, D_PLE), lambda i: (layer, i, 0)),
                  _layer_spec(layer, 1, D_MODEL), _layer_spec(layer, D_PLE, D_MODEL),
                  _layer_spec(layer, D_MODEL, D_MODEL), _const_spec((1, D_MODEL)),
                  pl.BlockSpec(memory_space=pl.ANY)],
        out_specs=_row_spec(tm, D_MODEL),
        out_shape=jax.ShapeDtypeStruct((n, D_MODEL), F32),
        scratch_shapes=[pltpu.VMEM((2, 2) + _tiled_rows(tm), F32), pltpu.SemaphoreType.DMA((2,)),
                        pltpu.VMEM((D_PLE, D_MODEL), BF16), pltpu.VMEM((D_MODEL, D_MODEL), BF16)],
        compiler_params=_params("arbitrary"),
        name="combine_ple",
    )(slots, slots, x, route, p, _vec(npl), wple, wgate, nfin.reshape(1, D_MODEL), ys)


def _layer(x, p, h0_re, h0_im, conv_left, w, s5, router_w, router_b, layer, bsz, t_len, tm, tm_conv, tms, final):
    n = bsz * t_len
    u, z = _inproj(x, w["norm_mix"], w["w_in"], layer, tm)
    ys, hf_re, hf_im = _s5_mixer(u, h0_re, h0_im, s5, bsz, t_len)
    yb, conv_new = _conv_mixer(z, conv_left, w["conv_w"], w["conv_b"], w["conv_ln_g"], w["conv_ln_b"],
                               layer, bsz, t_len, tm_conv)
    x1, hn, route, route_t, counts = _outproj(x, ys, u, yb, w["ssm_d"], w["w_ssm_glu"], w["w_out"],
                                              w["norm_ffn"], router_w, router_b, layer, tm)
    n_tiles = 2 * n // tms + N_EXPERTS
    slots, tile_expert, tile_rows, n_used = _moe_plan(route_t, counts, tms, n_tiles)
    xs = _dispatch(hn, slots, tile_rows, tms, tm)
    ysort = _moe(xs, tile_expert, n_used, w["expert_w_gate"], w["expert_w_up"], w["expert_w_down"],
                 layer, tms)
    x3 = _ple(x1, route, slots, ysort, p, w["norm_ple"], w["ple_w"], w["ple_gate_w"], w["norm_final"],
              layer, tm, final)
    return x3, hf_re, hf_im, conv_new


def kernel(x_prompt, x_sample, p_prompt, p_sample, state_ssm_re, state_ssm_im, cache_conv, norm_mix, w_in, ssm_a_re, ssm_a_im, ssm_b_re, ssm_b_im, ssm_c_re, ssm_c_im, ssm_d, ssm_log_dt, w_ssm_glu, conv_w, conv_b, conv_ln_g, conv_ln_b, w_out, norm_ffn, router_group_w, router_group_b, router_expert_w, router_expert_b, expert_w_gate, expert_w_up, expert_w_down, norm_ple, ple_w, ple_gate_w, norm_final):
    depth = w_in.shape[0]
    bp, tp, _ = x_prompt.shape
    bs, ts, _ = x_sample.shape
    xp = x_prompt.reshape(bp * tp, D_MODEL)
    xs = x_sample.reshape(bs * ts, D_MODEL)
    pp = p_prompt.reshape(depth, bp * tp, D_PLE)
    ps = p_sample.reshape(depth, bs * ts, D_PLE)
    zero_state = jnp.zeros((bp, N_GROUPS, SSM_STATE), F32)
    zero_conv = jnp.zeros((bp, CONV_WIDTH - 1, D_CONV), F32)
    pad_lanes = LANES - N_EXPERTS - N_EXPERT_GROUPS
    w = {"norm_mix": norm_mix, "w_in": w_in, "ssm_d": ssm_d, "w_ssm_glu": w_ssm_glu, "conv_w": conv_w,
         "conv_b": conv_b, "conv_ln_g": conv_ln_g, "conv_ln_b": conv_ln_b, "w_out": w_out,
         "norm_ffn": norm_ffn, "expert_w_gate": expert_w_gate, "expert_w_up": expert_w_up,
         "expert_w_down": expert_w_down, "norm_ple": norm_ple, "ple_w": ple_w, "ple_gate_w": ple_gate_w,
         "norm_final": norm_final}
    outs = {k: [] for k in ("pr_re", "pr_im", "pr_conv", "sm_re", "sm_im", "sm_conv")}
    for i in range(depth):
        s5 = _s5_prep(ssm_a_re[i], ssm_a_im[i], ssm_log_dt[i], ssm_b_re[i], ssm_b_im[i], ssm_c_re[i],
                      ssm_c_im[i])
        router_w = jnp.pad(jnp.concatenate([router_expert_w[i], router_group_w[i]], axis=1),
                           ((0, 0), (0, pad_lanes)))
        router_b = jnp.pad(jnp.concatenate([router_expert_b[i], router_group_b[i]]),
                           (0, pad_lanes)).reshape(1, LANES)
        final = i == depth - 1
        xp, hr, hi, cv = _layer(xp, pp, zero_state, zero_state, zero_conv, w, s5, router_w, router_b, i,
                                bp, tp, tm=512, tm_conv=512, tms=256, final=final)
        outs["pr_re"].append(hr); outs["pr_im"].append(hi); outs["pr_conv"].append(cv)
        xs, hr, hi, cv = _layer(xs, ps, state_ssm_re[i], state_ssm_im[i], cache_conv[i], w, s5, router_w,
                                router_b, i, bs, ts, tm=bs * ts, tm_conv=ts, tms=32, final=final)
        outs["sm_re"].append(hr); outs["sm_im"].append(hi); outs["sm_conv"].append(cv)
    return (xp.reshape(bp, tp, D_MODEL), xs.reshape(bs, ts, D_MODEL),
            jnp.stack(outs["pr_re"]), jnp.stack(outs["pr_im"]), jnp.stack(outs["pr_conv"]),
            jnp.stack(outs["sm_re"]), jnp.stack(outs["sm_im"]), jnp.stack(outs["sm_conv"]))
```

```python
import functools

import jax
import jax.numpy as jnp
from jax import lax
from jax.experimental import pallas as pl
from jax.experimental.pallas import tpu as pltpu

F32 = jnp.float32
BF16 = jnp.bfloat16

D_MODEL = 1024
D_SSM = 512
SSM_GROUP = 16
N_GROUPS = D_SSM // SSM_GROUP
N_PAIRS = N_GROUPS // 2
SSM_STATE = 64
D_CONV = 512
CONV_WIDTH = 31
CONV_HALO = 32
N_EXPERT_GROUPS = 4
EXPERTS_PER_GROUP = 8
N_EXPERTS = 32
D_EXPERT = 256
D_PLE = 256
EPS = 1e-6
S5_CHUNK = 16
LANES = 128
SUBLANES = 8
N_SLABS = D_SSM // LANES
GROUPS_PER_SLAB = LANES // SSM_GROUP
PAIRS_PER_SLAB = GROUPS_PER_SLAB // 2
VMEM_LIMIT = 56 * 1024 * 1024
TOKEN_TILE = 512
SLOT_TILE = 256


def _params(*sem):
    return pltpu.CompilerParams(dimension_semantics=sem, vmem_limit_bytes=VMEM_LIMIT)


def _rms(x, g):
    return x * lax.rsqrt(jnp.mean(x * x, axis=-1, keepdims=True) + EPS) * g


def _bdot(a, b):
    return jnp.dot(a.astype(BF16), b, preferred_element_type=F32)


def _row_spec(tm, width):
    return pl.BlockSpec((tm, width), lambda i: (i, 0))


ROW_SUB = D_MODEL // LANES


def _tiled_rows(n):
    return (n * ROW_SUB, LANES)


def _row_tile_spec(tm):
    return pl.BlockSpec(_tiled_rows(tm), lambda i, *_: (i, 0))


def _one_row(ref, r):
    return ref.at[pl.ds(pl.multiple_of(r * ROW_SUB, ROW_SUB), ROW_SUB)]


def _store_row_tiles(ref, rows):
    n = rows.shape[0]
    for s in range(ROW_SUB):
        ref[pl.ds(s, n, stride=ROW_SUB), :] = rows[:, s * LANES:(s + 1) * LANES]


def _load_row_tiles(ref):
    n = ref.shape[0] // ROW_SUB
    return jnp.concatenate([ref[pl.ds(s, n, stride=ROW_SUB), :] for s in range(ROW_SUB)], axis=1)


def _slab_spec(tm):
    return pl.BlockSpec((N_SLABS, tm, LANES), lambda i: (0, i, 0))


def _const_spec(shape):
    return pl.BlockSpec(shape, lambda i: (0,) * len(shape))


def _layer_spec(layer, *shape):
    return pl.BlockSpec((None,) + shape, lambda *_: (layer,) + (0,) * len(shape))


def _vec(stacked):
    return stacked.reshape(stacked.shape[0], 1, stacked.shape[1])


def _cast_weight_once(w_ref, wbf_ref):
    @pl.when(pl.program_id(0) == 0)
    def _():
        wbf_ref[...] = w_ref[...].astype(BF16)


def _inproj_kernel(x_ref, g_ref, w_ref, u_ref, z_ref, wbf):
    _cast_weight_once(w_ref, wbf)
    hn = _rms(x_ref[...], g_ref[...])
    proj = _bdot(hn, wbf[...])
    for q in range(N_SLABS):
        u_ref[q] = proj[:, q * LANES:(q + 1) * LANES]
    z_ref[...] = proj[:, D_SSM:]


def _inproj(x, g, w, layer, tm):
    n = x.shape[0]
    d_in = w.shape[2]
    return pl.pallas_call(
        _inproj_kernel,
        grid=(n // tm,),
        in_specs=[_row_spec(tm, D_MODEL), _layer_spec(layer, 1, D_MODEL), _layer_spec(layer, D_MODEL, d_in)],
        out_specs=[_slab_spec(tm), _row_spec(tm, d_in - D_SSM)],
        out_shape=[jax.ShapeDtypeStruct((N_SLABS, n, LANES), F32),
                   jax.ShapeDtypeStruct((n, d_in - D_SSM), F32)],
        scratch_shapes=[pltpu.VMEM((D_MODEL, d_in), BF16)],
        compiler_params=_params("arbitrary"),
        name="inproj",
    )(x, _vec(g), w)


def _s5_prep_kernel(ar_ref, ai_ref, ldt_ref, bre_ref, bim_ref, cre_ref, cim_ref,
                    m_ref, ws_ref, wot_ref, atab_ref, wt_re, wt_im, br_re, br_im):
    n_tap = S5_CHUNK * SSM_GROUP
    st = 4 * SSM_STATE
    nt = (((1,), (1,)), ((), ()))
    hi = lax.Precision.HIGHEST
    lane = lax.broadcasted_iota(jnp.int32, (SSM_GROUP, n_tap), 1)
    ws_rows, wot_rows, atab = [], [], jnp.zeros((2 * SUBLANES, st), F32)
    for gi in range(2):
        ar, ai = ar_ref[gi], ai_ref[gi]
        dt = jnp.exp(ldt_ref[gi])
        k = lax.broadcasted_iota(jnp.int32, (S5_CHUNK + SUBLANES, SSM_STATE), 0).astype(F32)
        mag = jnp.exp(k * (dt * ar))
        ang = k * (dt * ai)
        p_re, p_im = mag * jnp.cos(ang), mag * jnp.sin(ang)
        inv = 1.0 / (ar * ar + ai * ai)
        ab_re, ab_im = p_re[1:2], p_im[1:2]
        ia_re, ia_im = ar * inv, -ai * inv
        coef_re = (ab_re - 1.0) * ia_re - ab_im * ia_im
        coef_im = (ab_re - 1.0) * ia_im + ab_im * ia_re
        bre, bim = bre_ref[gi], bim_ref[gi]
        bb_re = coef_re * bre - coef_im * bim
        bb_im = coef_re * bim + coef_im * bre
        cre, cim = cre_ref[gi], cim_ref[gi]
        for kk in range(S5_CHUNK + 1):
            pr, pi = p_re[kk:kk + 1], p_im[kk:kk + 1]
            rows = slice(kk * SSM_GROUP, (kk + 1) * SSM_GROUP)
            wt_re[rows, :] = pr * cre - pi * cim
            wt_im[rows, :] = -pi * cre - pr * cim
            if kk < S5_CHUNK:
                back = slice((S5_CHUNK - 1 - kk) * SSM_GROUP, (S5_CHUNK - kk) * SSM_GROUP)
                br_re[back, :] = pr * bb_re - pi * bb_im
                br_im[back, :] = pi * bb_re + pr * bb_im
        kcat = (lax.dot_general(bb_re, wt_re[0:n_tap, :], nt, precision=hi, preferred_element_type=F32)
                + lax.dot_general(bb_im, wt_im[0:n_tap, :], nt, precision=hi, preferred_element_type=F32))
        for s in range(S5_CHUNK):
            shifted = kcat if s == 0 else pltpu.roll(kcat, s * SSM_GROUP, 1)
            m_ref[gi, s * SSM_GROUP:(s + 1) * SSM_GROUP, :] = jnp.where(
                lane >= s * SSM_GROUP, shifted, 0.0).astype(BF16)
        def place(v_re, v_im):
            zero = jnp.zeros_like(v_re)
            parts = [v_re, zero, v_im, zero] if gi == 0 else [zero, v_re, zero, v_im]
            return jnp.concatenate(parts, axis=1)

        ws_rows.append(place(br_re[...], br_im[...]))
        wot_rows.append(place(wt_re[SSM_GROUP:, :], wt_im[SSM_GROUP:, :]))
        kc = float(S5_CHUNK) * lax.broadcasted_iota(jnp.int32, (2 * SUBLANES, SSM_STATE), 0).astype(F32)
        magc = jnp.exp(kc * (dt * ar))
        angc = kc * (dt * ai)
        atab = atab + place(magc * jnp.cos(angc), magc * jnp.sin(angc))
    ws_ref[...] = jnp.concatenate(ws_rows, axis=0).astype(BF16)
    wot_ref[...] = jnp.concatenate(wot_rows, axis=0).astype(BF16)
    atab_ref[...] = atab


def _s5_prep(a_re, a_im, log_dt, b_re, b_im, c_re, c_im):
    p, n, c = N_PAIRS, SSM_STATE, SSM_GROUP
    n_tap = S5_CHUNK * c
    st = 4 * n

    def pspec(*shape):
        return pl.BlockSpec((None,) + shape, lambda i: (i,) + (0,) * len(shape))

    def pairs(a, *shape):
        return a.reshape((p, 2) + shape)

    return pl.pallas_call(
        _s5_prep_kernel,
        grid=(p,),
        in_specs=[pspec(2, 1, n), pspec(2, 1, n), pspec(2, 1, 1), pspec(2, c, n), pspec(2, c, n),
                  pspec(2, c, n), pspec(2, c, n)],
        out_specs=[pspec(2, n_tap, n_tap), pspec(2 * n_tap, st), pspec(2 * n_tap, st), pspec(2 * SUBLANES, st)],
        out_shape=[jax.ShapeDtypeStruct((p, 2, n_tap, n_tap), BF16),
                   jax.ShapeDtypeStruct((p, 2 * n_tap, st), BF16),
                   jax.ShapeDtypeStruct((p, 2 * n_tap, st), BF16),
                   jax.ShapeDtypeStruct((p, 2 * SUBLANES, st), F32)],
        scratch_shapes=[pltpu.VMEM((n_tap + c, n), F32), pltpu.VMEM((n_tap + c, n), F32),
                        pltpu.VMEM((n_tap, n), F32), pltpu.VMEM((n_tap, n), F32)],
        compiler_params=_params("parallel"),
        name="s5_prep",
    )(pairs(a_re, 1, n), pairs(a_im, 1, n), pairs(log_dt, 1, 1),
      pairs(jnp.swapaxes(b_re, 1, 2), c, n), pairs(jnp.swapaxes(b_im, 1, 2), c, n),
      pairs(c_re, c, n), pairs(c_im, c, n))


def _block_transpose8(vs):
    lane = lax.broadcasted_iota(jnp.int32, vs[0].shape, 1)
    blk = lane >> 4
    for d in (4, 2, 1):
        keep = (blk & d) == 0
        new = list(vs)
        for i in range(GROUPS_PER_SLAB):
            if i & d == 0:
                a, b = vs[i], vs[i + d]
                new[i] = jnp.where(keep, a, pltpu.roll(b, d * SSM_GROUP, 1))
                new[i + d] = jnp.where(keep, pltpu.roll(a, LANES - d * SSM_GROUP, 1), b)
        vs = new
    return vs


def _cmul(ar, ai, xr, xi):
    return ar * xr - ai * xi, ar * xi + ai * xr


def _s5_kernel(u_ref, h0_ref, m_ref, ws_ref, wo_ref, a_ref, y_ref, hf_ref, x_scr, yg_scr, s_scr, hp_scr,
               *, rows, independent):
    half = 2 * SSM_STATE
    rt = min(rows, 16 * SUBLANES)
    half_chunk = S5_CHUNK // 2

    def gather_tile(t, carry):
        r0 = pl.multiple_of(t * rt, rt)
        for hf in range(2):
            vs = [u_ref[pl.ds(r0 * S5_CHUNK + hf * half_chunk + i, rt, stride=S5_CHUNK), :]
                  for i in range(half_chunk)]
            outs = _block_transpose8(vs)
            for g in range(GROUPS_PER_SLAB):
                x_scr[g, pl.ds(r0, rt), hf * LANES:(hf + 1) * LANES] = outs[g]
        return carry

    lax.fori_loop(0, rows // rt, gather_tile, 0)

    row = lax.broadcasted_iota(jnp.int32, (SUBLANES, half), 0)
    n_tap = S5_CHUNK * SSM_GROUP
    for pi in range(PAIRS_PER_SLAB):
        x0 = x_scr[2 * pi].astype(BF16)
        x1 = x_scr[2 * pi + 1].astype(BF16)
        s_scr[...] = jnp.dot(jnp.concatenate([x0, x1], axis=1), ws_ref[pi], preferred_element_type=F32)
        ap = a_ref[pi]
        h0 = h0_ref[pi]
        if independent:
            hp_scr[...] = h0
            s = s_scr[...]
            n_re, n_im = _cmul(ap[1:2, :half], ap[1:2, half:], h0[:, :half], h0[:, half:])
            hf_ref[pi] = jnp.concatenate([n_re + s[:, :half], n_im + s[:, half:]], axis=1)
        else:
            pw_re, pw_im = ap[0:SUBLANES, :half], ap[0:SUBLANES, half:]

            def scan_tile(t, carry):
                h_re, h_im = carry
                r0 = pl.multiple_of(t * SUBLANES, SUBLANES)
                s = s_scr[pl.ds(r0, SUBLANES), :]
                t_re, t_im = s[:, :half], s[:, half:]
                for d in (1, 2, 4):
                    sh_re = jnp.where(row >= d, pltpu.roll(t_re, d, 0), 0.0)
                    sh_im = jnp.where(row >= d, pltpu.roll(t_im, d, 0), 0.0)
                    m_re, m_im = _cmul(ap[d:d + 1, :half], ap[d:d + 1, half:], sh_re, sh_im)
                    t_re, t_im = t_re + m_re, t_im + m_im
                e_re = jnp.where(row >= 1, pltpu.roll(t_re, 1, 0), 0.0)
                e_im = jnp.where(row >= 1, pltpu.roll(t_im, 1, 0), 0.0)
                c_re, c_im = _cmul(pw_re, pw_im, h_re, h_im)
                hp_scr[pl.ds(r0, SUBLANES), :] = jnp.concatenate([e_re + c_re, e_im + c_im], axis=1)
                o_re, o_im = _cmul(ap[SUBLANES:SUBLANES + 1, :half], ap[SUBLANES:SUBLANES + 1, half:],
                                   h_re, h_im)
                last = SUBLANES - 1
                n_re = jnp.broadcast_to(t_re[last:last + 1], h_re.shape) + o_re
                n_im = jnp.broadcast_to(t_im[last:last + 1], h_im.shape) + o_im
                return n_re, n_im

            init = (jnp.broadcast_to(h0[:, :half], (SUBLANES, half)),
                    jnp.broadcast_to(h0[:, half:], (SUBLANES, half)))
            h_re, h_im = lax.fori_loop(0, rows // SUBLANES, scan_tile, init, unroll=4)
            hf_ref[pi] = jnp.concatenate([h_re[0:1], h_im[0:1]], axis=1)
        yc = lax.dot_general(hp_scr[...].astype(BF16), wo_ref[pi], (((1,), (1,)), ((), ())),
                             preferred_element_type=F32)
        yg_scr[2 * pi] = jnp.dot(x0, m_ref[pi, 0], preferred_element_type=F32) + yc[:, :n_tap]
        yg_scr[2 * pi + 1] = jnp.dot(x1, m_ref[pi, 1], preferred_element_type=F32) + yc[:, n_tap:]

    def scatter_tile(t, carry):
        r0 = pl.multiple_of(t * rt, rt)
        for hf in range(2):
            vs = [yg_scr[g, pl.ds(r0, rt), hf * LANES:(hf + 1) * LANES] for g in range(GROUPS_PER_SLAB)]
            outs = _block_transpose8(vs)
            for i in range(half_chunk):
                y_ref[pl.ds(r0 * S5_CHUNK + hf * half_chunk + i, rt, stride=S5_CHUNK), :] = outs[i]
        return carry

    lax.fori_loop(0, rows // rt, scatter_tile, 0)


def _s5_mixer(u, h0_re, h0_im, prep, bsz, t_len):
    m, wsp, wop, a16 = prep
    n_tap = S5_CHUNK * SSM_GROUP
    st = 4 * SSM_STATE
    independent = t_len == S5_CHUNK
    if independent:
        nblk, rows, hrows = 1, bsz, bsz
    else:
        nblk, rows, hrows = bsz, t_len // S5_CHUNK, 1
    assert t_len % S5_CHUNK == 0 and rows % SUBLANES == 0, (bsz, t_len)
    h0p = jnp.concatenate([h0_re.reshape(bsz, N_PAIRS, 2 * SSM_STATE),
                           h0_im.reshape(bsz, N_PAIRS, 2 * SSM_STATE)], axis=2).astype(F32)
    h0p = h0p.transpose(1, 0, 2)[None] if independent else h0p[:, :, None, :]
    pp = PAIRS_PER_SLAB

    def wspec(*shape):
        return pl.BlockSpec((pp,) + shape, lambda q, b: (q,) + (0,) * len(shape))

    frames = rows * S5_CHUNK
    y, hf = pl.pallas_call(
        functools.partial(_s5_kernel, rows=rows, independent=independent),
        grid=(N_SLABS, nblk),
        in_specs=[pl.BlockSpec((None, frames, LANES), lambda q, b: (q, b, 0)),
                  pl.BlockSpec((None, pp, hrows, st), lambda q, b: (b, q, 0, 0)),
                  wspec(2, n_tap, n_tap), wspec(2 * n_tap, st), wspec(2 * n_tap, st), wspec(2 * SUBLANES, st)],
        out_specs=[pl.BlockSpec((None, frames, LANES), lambda q, b: (q, b, 0)),
                   pl.BlockSpec((None, pp, hrows, st), lambda q, b: (b, q, 0, 0))],
        out_shape=[jax.ShapeDtypeStruct(u.shape, F32),
                   jax.ShapeDtypeStruct((nblk, N_PAIRS, hrows, st), F32)],
        scratch_shapes=[pltpu.VMEM((GROUPS_PER_SLAB, rows, n_tap), F32),
                        pltpu.VMEM((GROUPS_PER_SLAB, rows, n_tap), F32),
                        pltpu.VMEM((rows, st), F32), pltpu.VMEM((rows, st), F32)],
        compiler_params=_params("parallel", "parallel"),
        name="s5_core",
    )(u, h0p, m, wsp, wop, a16)
    hf = hf[0].transpose(1, 0, 2) if independent else hf[:, :, 0, :]
    hf_re = hf[:, :, :2 * SSM_STATE].reshape(bsz, N_GROUPS, SSM_STATE)
    hf_im = hf[:, :, 2 * SSM_STATE:].reshape(bsz, N_GROUPS, SSM_STATE)
    return y, hf_re, hf_im


def _conv_kernel(z_ref, left_ref, w_ref, b_ref, g_ref, beta_ref, y_ref, cn_ref, vbuf, shifted, *, tm):
    @pl.when(pl.program_id(1) == 0)
    def _():
        vbuf[0:CONV_HALO, :] = left_ref[...]

    z = z_ref[...]
    vbuf[CONV_HALO:CONV_HALO + tm, :] = z[:, :D_CONV] * jax.nn.sigmoid(z[:, D_CONV:])
    first = CONV_HALO - (CONV_WIDTH - 1)
    span = tm + CONV_HALO - SUBLANES
    for r in range(1, SUBLANES):
        shifted[r - 1, 0:span, :] = vbuf[r:r + span, :]
    acc = jnp.zeros((tm, D_CONV), F32)
    for k in range(CONV_WIDTH):
        a, r = divmod(first + k, SUBLANES)
        src = vbuf if r == 0 else shifted.at[r - 1]
        acc = acc + w_ref[k:k + 1, :] * src[a * SUBLANES:a * SUBLANES + tm, :]
    y = acc + b_ref[...]
    mu = jnp.mean(y, axis=-1, keepdims=True)
    yc = y - mu
    var = jnp.mean(yc * yc, axis=-1, keepdims=True)
    yn = yc * lax.rsqrt(var + EPS) * g_ref[...] + beta_ref[...]
    y_ref[...] = yn * jax.nn.sigmoid(yn)
    cn_ref[...] = vbuf[tm + first:tm + CONV_HALO, :]
    vbuf[0:CONV_HALO, :] = vbuf[tm:tm + CONV_HALO, :]


def _conv_mixer(z, left, w, b, g, beta, layer, bsz, t_len, tm):
    left = jnp.pad(left.astype(F32), ((0, 0), (CONV_HALO - (CONV_WIDTH - 1), 0), (0, 0)))
    nt = t_len // tm
    vec = _layer_spec(layer, 1, D_CONV)
    return pl.pallas_call(
        functools.partial(_conv_kernel, tm=tm),
        grid=(bsz, nt),
        in_specs=[pl.BlockSpec((tm, 2 * D_CONV), lambda bi, j: (bi * nt + j, 0)),
                  pl.BlockSpec((None, CONV_HALO, D_CONV), lambda bi, j: (bi, 0, 0)),
                  _layer_spec(layer, CONV_WIDTH, D_CONV), vec, vec, vec],
        out_specs=[pl.BlockSpec((tm, D_CONV), lambda bi, j: (bi * nt + j, 0)),
                   pl.BlockSpec((None, CONV_WIDTH - 1, D_CONV), lambda bi, j: (bi, 0, 0))],
        out_shape=[jax.ShapeDtypeStruct((bsz * t_len, D_CONV), F32),
                   jax.ShapeDtypeStruct((bsz, CONV_WIDTH - 1, D_CONV), F32)],
        scratch_shapes=[pltpu.VMEM((tm + CONV_HALO, D_CONV), F32),
                        pltpu.VMEM((SUBLANES - 1, tm + CONV_HALO - SUBLANES, D_CONV), F32)],
        compiler_params=_params("parallel", "arbitrary"),
        name="conv_mixer",
    )(z, left, w, _vec(b), _vec(g), _vec(beta))


def _route(logits):
    lane_i = lax.broadcasted_iota(jnp.int32, logits.shape, 1)
    lane = lane_i.astype(F32)
    group_of_lane = (lane_i >> 3).astype(F32)
    neg = -jnp.inf
    far = float(LANES)
    is_g = (lane_i >= N_EXPERTS) & (lane_i < N_EXPERTS + N_EXPERT_GROUPS)
    gl = jnp.where(is_g, logits, neg)
    g_max = jnp.max(gl, axis=-1, keepdims=True)
    g_lane = jnp.min(jnp.where(gl == g_max, lane, far), axis=-1, keepdims=True)
    g_gate = 1.0 / jnp.sum(jnp.exp(gl - g_max), axis=-1, keepdims=True)
    g_idx = g_lane - float(N_EXPERTS)
    in_group = (lane_i < N_EXPERTS) & (group_of_lane == g_idx)
    el = jnp.where(in_group, logits, neg)
    v1 = jnp.max(el, axis=-1, keepdims=True)
    i1 = jnp.min(jnp.where(el == v1, lane, far), axis=-1, keepdims=True)
    el2 = jnp.where(lane == i1, neg, el)
    v2 = jnp.max(el2, axis=-1, keepdims=True)
    i2 = jnp.min(jnp.where(el2 == v2, lane, far), axis=-1, keepdims=True)
    e2 = jnp.exp(v2 - v1)
    w1 = g_gate / (1.0 + e2)
    w2 = g_gate * e2 / (1.0 + e2)
    return i1, i2, w1, w2


ROUTE_E1, ROUTE_E2, ROUTE_W1, ROUTE_W2, ROUTE_RANK1, ROUTE_RANK2 = range(6)


def _outproj_kernel(x_ref, ys_ref, u_ref, yb_ref, d_ref, wglu_ref, wout_ref, nf_ref, rw_ref, rb_ref, cnt0_ref,
                    x1_ref, hn_ref, route_ref, rt_ref, cnt_ref, wglu_bf, wout_bf, rw_bf):
    @pl.when(pl.program_id(0) == 0)
    def _():
        cnt_ref[...] = cnt0_ref[...]
        rw = rw_ref[...]
        rw_hi = rw.astype(BF16)
        rw_bf[:, 0:LANES] = rw_hi
        rw_bf[:, LANES:] = (rw - rw_hi.astype(F32)).astype(BF16)

    _cast_weight_once(wglu_ref, wglu_bf)
    _cast_weight_once(wout_ref, wout_bf)
    ys = jnp.concatenate([ys_ref[q] for q in range(N_SLABS)], axis=1)
    u = jnp.concatenate([u_ref[q] for q in range(N_SLABS)], axis=1)
    z = jax.nn.gelu(ys + d_ref[...] * u)
    ya = z * jax.nn.sigmoid(_bdot(z, wglu_bf[...]))
    mix = _bdot(ya, wout_bf[0:D_SSM, :]) + _bdot(yb_ref[...], wout_bf[D_SSM:, :])
    x1 = x_ref[...] + mix
    x1_ref[...] = x1
    hn = _rms(x1, nf_ref[...])
    _store_row_tiles(hn_ref, hn)
    h_hi = hn.astype(BF16)
    h_lo = (hn - h_hi.astype(F32)).astype(BF16)
    hw = jnp.dot(h_hi, rw_bf[...], preferred_element_type=F32)
    logits = (hw[:, :LANES] + hw[:, LANES:]
              + jnp.dot(h_lo, rw_bf[:, 0:LANES], preferred_element_type=F32) + rb_ref[...])
    i1, i2, w1, w2 = _route(logits)
    tm = logits.shape[0]
    lane_i = lax.broadcasted_iota(jnp.int32, logits.shape, 1)
    lane = lane_i.astype(F32)
    picked = jnp.where((lane == i1) | (lane == i2), 1.0, 0.0)
    earlier = (lax.broadcasted_iota(jnp.int32, (tm, tm), 0) > lax.broadcasted_iota(jnp.int32, (tm, tm), 1))
    prefix = _bdot(jnp.where(earlier, 1.0, 0.0), picked.astype(BF16)) + cnt_ref[...]
    rank1 = jnp.sum(jnp.where(lane == i1, prefix, 0.0), axis=-1, keepdims=True)
    rank2 = jnp.sum(jnp.where(lane == i2, prefix, 0.0), axis=-1, keepdims=True)
    cnt_ref[...] += jnp.sum(picked, axis=0, keepdims=True)
    rec = jnp.zeros_like(logits)
    for lane_id, val in ((ROUTE_E1, i1), (ROUTE_E2, i2), (ROUTE_W1, w1), (ROUTE_W2, w2),
                         (ROUTE_RANK1, rank1), (ROUTE_RANK2, rank2)):
        rec = jnp.where(lane_i == lane_id, val, rec)
    route_ref[...] = rec
    rt_ref[...] = rec.T[0:SUBLANES, :]


def _outproj(x, ys, u, yb, d, wglu, wout, nf, rw, rb, counts0, layer, tm):
    n = x.shape[0]
    return pl.pallas_call(
        _outproj_kernel,
        grid=(n // tm,),
        in_specs=[_row_spec(tm, D_MODEL), _slab_spec(tm), _slab_spec(tm), _row_spec(tm, D_CONV),
                  _layer_spec(layer, 1, D_SSM), _layer_spec(layer, D_SSM, D_SSM),
                  _layer_spec(layer, D_MODEL, D_MODEL), _layer_spec(layer, 1, D_MODEL),
                  _const_spec((D_MODEL, LANES)), _const_spec((1, LANES)), _const_spec((1, LANES))],
        out_specs=[_row_spec(tm, D_MODEL), _row_tile_spec(tm), _row_spec(tm, LANES),
                   pl.BlockSpec((SUBLANES, tm), lambda i: (0, i)), _const_spec((1, LANES))],
        out_shape=[jax.ShapeDtypeStruct((n, D_MODEL), F32), jax.ShapeDtypeStruct(_tiled_rows(n), F32),
                   jax.ShapeDtypeStruct((n, LANES), F32), jax.ShapeDtypeStruct((SUBLANES, n), F32),
                   jax.ShapeDtypeStruct((1, LANES), F32)],
        scratch_shapes=[pltpu.VMEM((D_SSM, D_SSM), BF16), pltpu.VMEM((D_MODEL, D_MODEL), BF16),
                        pltpu.VMEM((D_MODEL, 2 * LANES), BF16)],
        compiler_params=_params("arbitrary"),
        name="outproj_router",
    )(x, ys, u, yb, _vec(d), wglu, wout, _vec(nf), rw, rb, counts0)


PLAN_TILE_LANES = 2 * LANES
PLAN_EXPERT, PLAN_ROWS, PLAN_USED = range(3)


def _moe_plan_kernel(rt_ref, cnt_ref, cnt_first_ref, slot_ref, tile_ref, *, tms):
    cnt = cnt_ref[...]
    padded = jnp.ceil(cnt * (1.0 / tms)) * float(tms)
    r = lax.broadcasted_iota(jnp.int32, (LANES, LANES), 0)
    c = lax.broadcasted_iota(jnp.int32, (LANES, LANES), 1)
    ends = jnp.dot(padded, jnp.where(r <= c, 1.0, 0.0), precision=lax.Precision.HIGHEST,
                   preferred_element_type=F32)
    starts = ends - padded
    rt = rt_ref[...]
    e1, e2 = rt[ROUTE_E1:ROUTE_E1 + 1], rt[ROUTE_E2:ROUTE_E2 + 1]
    s1, s2 = rt[ROUTE_RANK1:ROUTE_RANK1 + 1], rt[ROUTE_RANK2:ROUTE_RANK2 + 1]
    tile = lax.broadcasted_iota(jnp.int32, (1, PLAN_TILE_LANES), 1).astype(F32)
    used = ends[:, N_EXPERTS - 1:N_EXPERTS] * (1.0 / tms)
    pos = jnp.minimum(tile, used - 1.0) * float(tms)
    t_exp = jnp.zeros_like(tile)
    t_fill = jnp.zeros_like(tile)
    for e in range(N_EXPERTS):
        st, en = starts[:, e:e + 1], ends[:, e:e + 1]
        s1 = s1 + jnp.where(e1 == float(e), st, 0.0)
        s2 = s2 + jnp.where(e2 == float(e), st, 0.0)
        mine = (pos >= st) & (pos < en)
        t_exp = t_exp + jnp.where(mine, float(e), 0.0)
        t_fill = t_fill + jnp.where(mine, st + cnt_first_ref[:, e:e + 1], 0.0)
    t_rows = jnp.where(tile < used, jnp.clip(t_fill - pos, 0.0, float(tms)), 0.0)
    slot_ref[...] = jnp.concatenate([s1, s2], axis=0).astype(jnp.int32)
    tile_ref[...] = jnp.concatenate(
        [t_exp, t_rows, jnp.broadcast_to(used, tile.shape), jnp.zeros((SUBLANES - 3, PLAN_TILE_LANES), F32)],
        axis=0).astype(jnp.int32)


def _moe_plan(route_t, counts, counts_first, tms, n_tiles):
    n = route_t.shape[1]
    assert n_tiles <= PLAN_TILE_LANES
    slots, tiles = pl.pallas_call(
        functools.partial(_moe_plan_kernel, tms=tms),
        out_shape=[jax.ShapeDtypeStruct((2, n), jnp.int32),
                   jax.ShapeDtypeStruct((SUBLANES, PLAN_TILE_LANES), jnp.int32)],
        compiler_params=pltpu.CompilerParams(vmem_limit_bytes=VMEM_LIMIT),
        name="moe_plan",
    )(route_t, counts, counts_first)
    return slots, tiles[PLAN_EXPERT, :n_tiles], tiles[PLAN_ROWS, :n_tiles], tiles[PLAN_USED, :1]


DMA_UNROLL = 8


HN_BUFFERS = 3


def _dispatch_kernel(tr_ref, slot_ref, hn_ref, *rest, tm, tms, n_tiles, n_steps, fresh):
    xs_ref, zbuf, hbuf, in_sem, out_sem, zsem = rest if fresh else rest[1:]
    i = pl.program_id(0)

    def fetch(t):
        b = lax.rem(t, HN_BUFFERS)
        first = pl.multiple_of(t * (tm * ROW_SUB), tm * ROW_SUB)
        return pltpu.make_async_copy(hn_ref.at[pl.ds(first, tm * ROW_SUB)], hbuf.at[b], in_sem.at[b])

    def drain_scatter(t):
        b = lax.rem(t, HN_BUFFERS)
        for k in range(2):
            pltpu.make_async_copy(hbuf.at[b], xs_ref.at[pl.ds(0, tm * ROW_SUB)], out_sem.at[b]).wait()

    @pl.when(i == 0)
    def _():
        fetch(i).start()

    @pl.when(jnp.logical_and(i == 0, fresh))
    def _():
        zbuf[...] = jnp.zeros_like(zbuf)

        def fill(t, carry):
            @pl.when(tr_ref[t] < tms)
            def _():
                first = pl.multiple_of(t * (tms * ROW_SUB), tms * ROW_SUB)
                pltpu.make_async_copy(zbuf, xs_ref.at[pl.ds(first, tms * ROW_SUB)], zsem).start()
            return carry

        def drain(t, carry):
            @pl.when(tr_ref[t] < tms)
            def _():
                pltpu.make_async_copy(zbuf, xs_ref.at[pl.ds(0, tms * ROW_SUB)], zsem).wait()
            return carry

        lax.fori_loop(0, n_tiles, fill, 0)
        lax.fori_loop(0, n_tiles, drain, 0)

    @pl.when(i + 1 < n_steps)
    def _():
        fetch(i + 1).start()

    fetch(i).wait()
    b = lax.rem(i, HN_BUFFERS)
    rows = hbuf.at[b]

    def issue(r, carry):
        for k in range(2):
            pltpu.make_async_copy(_one_row(rows, r), _one_row(xs_ref, slot_ref[k, r]),
                                  out_sem.at[b]).start(priority=k)
        return carry

    lax.fori_loop(0, tm, issue, 0, unroll=DMA_UNROLL)

    @pl.when(i >= 1)
    def _():
        drain_scatter(i - 1)

    @pl.when(i == n_steps - 1)
    def _():
        drain_scatter(i)


def _dispatch(hn, slots, tile_rows, tms, tm, into=None):
    n = hn.shape[0] // ROW_SUB
    n_tiles = tile_rows.shape[0]
    fresh = into is None
    any_spec = pl.BlockSpec(memory_space=pl.ANY)
    return pl.pallas_call(
        functools.partial(_dispatch_kernel, tm=tm, tms=tms, n_tiles=n_tiles, n_steps=n // tm, fresh=fresh),
        grid_spec=pltpu.PrefetchScalarGridSpec(
            num_scalar_prefetch=1, grid=(n // tm,),
            in_specs=[pl.BlockSpec((2, tm), lambda i, tr: (0, i), memory_space=pltpu.SMEM), any_spec]
                     + ([] if fresh else [any_spec]),
            out_specs=any_spec,
            scratch_shapes=[pltpu.VMEM(_tiled_rows(tms), F32), pltpu.VMEM((HN_BUFFERS,) + _tiled_rows(tm), F32),
                            pltpu.SemaphoreType.DMA((HN_BUFFERS,)), pltpu.SemaphoreType.DMA((HN_BUFFERS,)),
                            pltpu.SemaphoreType.DMA]),
        out_shape=jax.ShapeDtypeStruct(_tiled_rows(n_tiles * tms), F32),
        input_output_aliases={} if fresh else {3: 0},
        compiler_params=_params("arbitrary"),
        name="moe_dispatch",
    )(tile_rows, slots, hn, *([] if fresh else [into]))


def _moe_kernel(te_ref, nu_ref, x_ref, wg_ref, wu_ref, wd_ref, y_ref, wg_bf, wu_bf, wd_bf):
    i = pl.program_id(0)
    in_use = i < nu_ref[0]
    new_expert = (i == 0) | (te_ref[i] != te_ref[jnp.maximum(i - 1, 0)])

    @pl.when(in_use & new_expert)
    def _():
        wg_bf[...] = wg_ref[...].astype(BF16)
        wu_bf[...] = wu_ref[...].astype(BF16)
        wd_bf[...] = wd_ref[...].astype(BF16)

    @pl.when(in_use)
    def _():
        h = _load_row_tiles(x_ref).astype(BF16)
        hg = jnp.dot(h, wg_bf[...], preferred_element_type=F32)
        hu = jnp.dot(h, wu_bf[...], preferred_element_type=F32)
        _store_row_tiles(y_ref, _bdot(hg * jax.nn.sigmoid(hg) * hu, wd_bf[...]))

    @pl.when(jnp.logical_not(in_use))
    def _():
        y_ref[...] = jnp.zeros_like(y_ref)


def _moe(xs, tile_expert, n_used, wg, wu, wd, layer, tms):
    n_slots = xs.shape[0] // ROW_SUB
    rows = pl.BlockSpec(_tiled_rows(tms), lambda i, te, nu: (jnp.minimum(i, nu[0] - 1), 0))
    out_rows = pl.BlockSpec(_tiled_rows(tms), lambda i, te, nu: (i, 0))

    def wspec(a, b):
        return pl.BlockSpec((None, None, a, b), lambda i, te, nu: (layer, te[i], 0, 0))

    return pl.pallas_call(
        _moe_kernel,
        grid_spec=pltpu.PrefetchScalarGridSpec(
            num_scalar_prefetch=2, grid=(n_slots // tms,),
            in_specs=[rows, wspec(D_MODEL, D_EXPERT), wspec(D_MODEL, D_EXPERT), wspec(D_EXPERT, D_MODEL)],
            out_specs=out_rows,
            scratch_shapes=[pltpu.VMEM((D_MODEL, D_EXPERT), BF16), pltpu.VMEM((D_MODEL, D_EXPERT), BF16),
                            pltpu.VMEM((D_EXPERT, D_MODEL), BF16)]),
        out_shape=jax.ShapeDtypeStruct(xs.shape, F32),
        compiler_params=_params("arbitrary"),
        name="moe",
    )(tile_expert, n_used, xs, wg, wu, wd)


def _ple_kernel(slot_ref, next_slot_ref, x_ref, route_ref, p_ref, np_ref, wple_ref, wgate_ref, nfin_ref,
                ys_ref, o_ref, ybuf, sem, wple_bf, wgate_bf, *, tm, n_steps, final):
    i = pl.program_id(0)
    _cast_weight_once(wple_ref, wple_bf)
    _cast_weight_once(wgate_ref, wgate_bf)

    def gather(slots, b):
        def issue(r, carry):
            for k in range(2):
                pltpu.make_async_copy(_one_row(ys_ref, slots[k, r]), _one_row(ybuf.at[b, k], r),
                                      sem.at[b]).start(priority=k)
            return carry

        lax.fori_loop(0, tm, issue, 0, unroll=DMA_UNROLL)

    @pl.when(i == 0)
    def _():
        gather(slot_ref, 0)

    @pl.when(i + 1 < n_steps)
    def _():
        gather(next_slot_ref, lax.rem(i + 1, 2))

    pe = _bdot(p_ref[...], wple_bf[...])
    b = lax.rem(i, 2)
    for k in range(2):
        pltpu.make_async_copy(ys_ref.at[pl.ds(0, tm * ROW_SUB)], ybuf.at[b, k], sem.at[b]).wait()
    route = route_ref[...]
    x = (x_ref[...] + route[:, ROUTE_W1:ROUTE_W1 + 1] * _load_row_tiles(ybuf.at[b, 0])
         + route[:, ROUTE_W2:ROUTE_W2 + 1] * _load_row_tiles(ybuf.at[b, 1]))
    gate = jax.nn.sigmoid(_bdot(_rms(x, np_ref[...]), wgate_bf[...]))
    out = x + pe * gate
    if final:
        out = _rms(out, nfin_ref[...])
    o_ref[...] = out


def _ple(x, route, slots, ys, p, npl, wple, wgate, nfin, layer, tm, final):
    n = x.shape[0]
    n_steps = n // tm
    return pl.pallas_call(
        functools.partial(_ple_kernel, tm=tm, n_steps=n_steps, final=final),
        grid=(n_steps,),
        in_specs=[pl.BlockSpec((2, tm), lambda i: (0, i), memory_space=pltpu.SMEM),
                  pl.BlockSpec((2, tm), lambda i: (0, jnp.minimum(i + 1, n_steps - 1)),
                               memory_space=pltpu.SMEM),
                  _row_spec(tm, D_MODEL), _row_spec(tm, LANES),
                  pl.BlockSpec((None, tm, D_PLE), lambda i: (layer, i, 0)),
                  _layer_spec(layer, 1, D_MODEL), _layer_spec(layer, D_PLE, D_MODEL),
                  _layer_spec(layer, D_MODEL, D_MODEL), _const_spec((1, D_MODEL)),
                  pl.BlockSpec(memory_space=pl.ANY)],
        out_specs=_row_spec(tm, D_MODEL),
        out_shape=jax.ShapeDtypeStruct((n, D_MODEL), F32),
        scratch_shapes=[pltpu.VMEM((2, 2) + _tiled_rows(tm), F32), pltpu.SemaphoreType.DMA((2,)),
                        pltpu.VMEM((D_PLE, D_MODEL), BF16), pltpu.VMEM((D_MODEL, D_MODEL), BF16)],
        compiler_params=_params("arbitrary"),
        name="combine_ple",
    )(slots, slots, x, route, p, _vec(npl), wple, wgate, nfin.reshape(1, D_MODEL), ys)


def _mixers(x, h0_re, h0_im, conv_left, w, s5, router_w, router_b, counts0, layer, bsz, t_len, tm, tm_conv):
    u, z = _inproj(x, w["norm_mix"], w["w_in"], layer, min(2 * tm, x.shape[0]))
    ys, hf_re, hf_im = _s5_mixer(u, h0_re, h0_im, s5, bsz, t_len)
    yb, conv_new = _conv_mixer(z, conv_left, w["conv_w"], w["conv_b"], w["conv_ln_g"], w["conv_ln_b"],
                               layer, bsz, t_len, tm_conv)
    x1, hn, route, route_t, counts = _outproj(x, ys, u, yb, w["ssm_d"], w["w_ssm_glu"], w["w_out"],
                                              w["norm_ffn"], router_w, router_b, counts0, layer, tm)
    return dict(x1=x1, hn=hn, route=route, route_t=route_t, counts=counts, state=(hf_re, hf_im, conv_new))


def _moe_and_ple(sets, ps, tms_rows, w, layer, tms, final):
    sizes = [s["x1"].shape[0] for s in sets]
    n_tiles = 2 * sum(sizes) // tms + N_EXPERTS
    route_t = jnp.concatenate([s["route_t"] for s in sets], axis=1)
    slots, tile_expert, tile_rows, n_used = _moe_plan(route_t, sets[-1]["counts"], sets[0]["counts"], tms,
                                                     n_tiles)
    starts = [sum(sizes[:j]) for j in range(len(sets))]
    set_slots = [slots[:, a:a + n] for a, n in zip(starts, sizes)]
    xsort = None
    for s, sl, tm in zip(sets, set_slots, tms_rows):
        xsort = _dispatch(s["hn"], sl, tile_rows, tms, tm, into=xsort)
    ysort = _moe(xsort, tile_expert, n_used, w["expert_w_gate"], w["expert_w_up"], w["expert_w_down"],
                 layer, tms)
    return [_ple(s["x1"], s["route"], sl, ysort, p, w["norm_ple"], w["ple_w"], w["ple_gate_w"],
                 w["norm_final"], layer, tm, final)
            for s, sl, p, tm in zip(sets, set_slots, ps, tms_rows)]


def kernel(x_prompt, x_sample, p_prompt, p_sample, state_ssm_re, state_ssm_im, cache_conv, norm_mix, w_in, ssm_a_re, ssm_a_im, ssm_b_re, ssm_b_im, ssm_c_re, ssm_c_im, ssm_d, ssm_log_dt, w_ssm_glu, conv_w, conv_b, conv_ln_g, conv_ln_b, w_out, norm_ffn, router_group_w, router_group_b, router_expert_w, router_expert_b, expert_w_gate, expert_w_up, expert_w_down, norm_ple, ple_w, ple_gate_w, norm_final):
    depth = w_in.shape[0]
    bp, tp, _ = x_prompt.shape
    bs, ts, _ = x_sample.shape
    xp = x_prompt.reshape(bp * tp, D_MODEL)
    xs = x_sample.reshape(bs * ts, D_MODEL)
    pp = p_prompt.reshape(depth, bp * tp, D_PLE)
    ps = p_sample.reshape(depth, bs * ts, D_PLE)
    zero_state = jnp.zeros((bp, N_GROUPS, SSM_STATE), F32)
    zero_conv = jnp.zeros((bp, CONV_WIDTH - 1, D_CONV), F32)
    pad_lanes = LANES - N_EXPERTS - N_EXPERT_GROUPS
    w = {"norm_mix": norm_mix, "w_in": w_in, "ssm_d": ssm_d, "w_ssm_glu": w_ssm_glu, "conv_w": conv_w,
         "conv_b": conv_b, "conv_ln_g": conv_ln_g, "conv_ln_b": conv_ln_b, "w_out": w_out,
         "norm_ffn": norm_ffn, "expert_w_gate": expert_w_gate, "expert_w_up": expert_w_up,
         "expert_w_down": expert_w_down, "norm_ple": norm_ple, "ple_w": ple_w, "ple_gate_w": ple_gate_w,
         "norm_final": norm_final}
    no_picks = jnp.zeros((1, LANES), F32)
    outs = {k: [] for k in ("pr_re", "pr_im", "pr_conv", "sm_re", "sm_im", "sm_conv")}
    for i in range(depth):
        s5 = _s5_prep(ssm_a_re[i], ssm_a_im[i], ssm_log_dt[i], ssm_b_re[i], ssm_b_im[i], ssm_c_re[i],
                      ssm_c_im[i])
        router_w = jnp.pad(jnp.concatenate([router_expert_w[i], router_group_w[i]], axis=1),
                           ((0, 0), (0, pad_lanes)))
        router_b = jnp.pad(jnp.concatenate([router_expert_b[i], router_group_b[i]]),
                           (0, pad_lanes)).reshape(1, LANES)
        final = i == depth - 1
        mp = _mixers(xp, zero_state, zero_state, zero_conv, w, s5, router_w, router_b, no_picks, i,
                     bp, tp, tm=TOKEN_TILE, tm_conv=TOKEN_TILE)
        ms = _mixers(xs, state_ssm_re[i], state_ssm_im[i], cache_conv[i], w, s5, router_w, router_b,
                     mp["counts"], i, bs, ts, tm=bs * ts, tm_conv=ts)
        xp, xs = _moe_and_ple([mp, ms], [pp, ps], [TOKEN_TILE, bs * ts], w, i, SLOT_TILE, final)
        for key, val in zip(("pr_re", "pr_im", "pr_conv"), mp["state"]):
            outs[key].append(val)
        for key, val in zip(("sm_re", "sm_im", "sm_conv"), ms["state"]):
            outs[key].append(val)
    return (xp.reshape(bp, tp, D_MODEL), xs.reshape(bs, ts, D_MODEL),
            jnp.stack(outs["pr_re"]), jnp.stack(outs["pr_im"]), jnp.stack(outs["pr_conv"]),
            jnp.stack(outs["sm_re"]), jnp.stack(outs["sm_im"]), jnp.stack(outs["sm_conv"]))
```

```python
import functools

import jax
import jax.numpy as jnp
from jax import lax
from jax.experimental import pallas as pl
from jax.experimental.pallas import tpu as pltpu

F32 = jnp.float32
BF16 = jnp.bfloat16

D_MODEL = 1024
D_SSM = 512
SSM_GROUP = 16
N_GROUPS = D_SSM // SSM_GROUP
N_PAIRS = N_GROUPS // 2
SSM_STATE = 64
D_CONV = 512
CONV_WIDTH = 31
CONV_HALO = 32
N_EXPERT_GROUPS = 4
EXPERTS_PER_GROUP = 8
N_EXPERTS = 32
D_EXPERT = 256
D_PLE = 256
EPS = 1e-6
S5_CHUNK = 16
LANES = 128
SUBLANES = 8
N_SLABS = D_SSM // LANES
GROUPS_PER_SLAB = LANES // SSM_GROUP
PAIRS_PER_SLAB = GROUPS_PER_SLAB // 2
VMEM_LIMIT = 56 * 1024 * 1024
TOKEN_TILE = 512
SLOT_TILE = 256


def _params(*sem):
    return pltpu.CompilerParams(dimension_semantics=sem, vmem_limit_bytes=VMEM_LIMIT)


def _rms(x, g):
    return x * lax.rsqrt(jnp.mean(x * x, axis=-1, keepdims=True) + EPS) * g


def _bdot(a, b):
    return jnp.dot(a.astype(BF16), b, preferred_element_type=F32)


def _split_bf16(a):
    hi = a.astype(BF16)
    return hi, (a - hi.astype(F32)).astype(BF16)


def _pdot(a, w_hi, w_lo=None, dims=(((1,), (0,)), ((), ()))):
    def mm(x, w):
        return lax.dot_general(x, w, dims, preferred_element_type=F32)

    if w_lo is None:
        return mm(a.astype(BF16), w_hi)
    a_hi, a_lo = _split_bf16(a)
    return mm(a_hi, w_hi) + (mm(a_hi, w_lo) + mm(a_lo, w_hi))


def _parts(wbf_ref, rows=slice(None)):
    return wbf_ref[0, rows, :], (wbf_ref[1, rows, :] if wbf_ref.shape[0] == 2 else None)


def _row_spec(tm, width):
    return pl.BlockSpec((tm, width), lambda i: (i, 0))


ROW_SUB = D_MODEL // LANES


def _tiled_rows(n):
    return (n * ROW_SUB, LANES)


def _row_tile_spec(tm):
    return pl.BlockSpec(_tiled_rows(tm), lambda i, *_: (i, 0))


def _one_row(ref, r):
    return ref.at[pl.ds(pl.multiple_of(r * ROW_SUB, ROW_SUB), ROW_SUB)]


def _store_row_tiles(ref, rows):
    n = rows.shape[0]
    for s in range(ROW_SUB):
        ref[pl.ds(s, n, stride=ROW_SUB), :] = rows[:, s * LANES:(s + 1) * LANES]


def _load_row_tiles(ref):
    n = ref.shape[0] // ROW_SUB
    return jnp.concatenate([ref[pl.ds(s, n, stride=ROW_SUB), :] for s in range(ROW_SUB)], axis=1)


def _slab_spec(tm):
    return pl.BlockSpec((N_SLABS, tm, LANES), lambda i: (0, i, 0))


def _const_spec(shape):
    return pl.BlockSpec(shape, lambda i: (0,) * len(shape))


def _layer_spec(layer, *shape):
    return pl.BlockSpec((None,) + shape, lambda *_: (layer,) + (0,) * len(shape))


def _vec(stacked):
    return stacked.reshape(stacked.shape[0], 1, stacked.shape[1])


def _cast_weight_once(w_ref, wbf_ref):
    @pl.when(pl.program_id(0) == 0)
    def _():
        w = w_ref[...]
        hi = w.astype(BF16)
        wbf_ref[0] = hi
        if wbf_ref.shape[0] == 2:
            wbf_ref[1] = (w - hi.astype(F32)).astype(BF16)


def _weight_scratch(k, n, precise):
    return pltpu.VMEM((2 if precise else 1, k, n), BF16)


def _inproj_kernel(x_ref, g_ref, w_ref, u_ref, v_ref, wbf):
    _cast_weight_once(w_ref, wbf)
    hn = _rms(x_ref[...], g_ref[...])
    proj = _pdot(hn, *_parts(wbf))
    for q in range(N_SLABS):
        u_ref[q] = proj[:, q * LANES:(q + 1) * LANES]
    v_ref[...] = proj[:, D_SSM:D_SSM + D_CONV] * jax.nn.sigmoid(proj[:, D_SSM + D_CONV:])


def _inproj(x, g, w, layer, tm, precise):
    n = x.shape[0]
    d_in = w.shape[2]
    return pl.pallas_call(
        _inproj_kernel,
        grid=(n // tm,),
        in_specs=[_row_spec(tm, D_MODEL), _layer_spec(layer, 1, D_MODEL), _layer_spec(layer, D_MODEL, d_in)],
        out_specs=[_slab_spec(tm), _row_spec(tm, D_CONV)],
        out_shape=[jax.ShapeDtypeStruct((N_SLABS, n, LANES), F32), jax.ShapeDtypeStruct((n, D_CONV), F32)],
        scratch_shapes=[_weight_scratch(D_MODEL, d_in, precise)],
        compiler_params=_params("arbitrary"),
        name="inproj",
    )(x, _vec(g), w)


def _s5_prep_kernel(ar_ref, ai_ref, ldt_ref, bre_ref, bim_ref, cre_ref, cim_ref,
                    m_ref, ws_ref, wot_ref, atab_ref, wt_re, wt_im, br_re, br_im):
    n_tap = S5_CHUNK * SSM_GROUP
    st = 4 * SSM_STATE
    nt = (((1,), (1,)), ((), ()))
    hi = lax.Precision.HIGHEST
    lane = lax.broadcasted_iota(jnp.int32, (SSM_GROUP, n_tap), 1)
    ws_rows, wot_rows, atab = [], [], jnp.zeros((2 * SUBLANES, st), F32)
    for gi in range(2):
        ar, ai = ar_ref[gi], ai_ref[gi]
        dt = jnp.exp(ldt_ref[gi])
        k = lax.broadcasted_iota(jnp.int32, (S5_CHUNK + SUBLANES, SSM_STATE), 0).astype(F32)
        mag = jnp.exp(k * (dt * ar))
        ang = k * (dt * ai)
        p_re, p_im = mag * jnp.cos(ang), mag * jnp.sin(ang)
        inv = 1.0 / (ar * ar + ai * ai)
        ab_re, ab_im = p_re[1:2], p_im[1:2]
        ia_re, ia_im = ar * inv, -ai * inv
        coef_re = (ab_re - 1.0) * ia_re - ab_im * ia_im
        coef_im = (ab_re - 1.0) * ia_im + ab_im * ia_re
        bre, bim = bre_ref[gi], bim_ref[gi]
        bb_re = coef_re * bre - coef_im * bim
        bb_im = coef_re * bim + coef_im * bre
        cre, cim = cre_ref[gi], cim_ref[gi]
        for kk in range(S5_CHUNK + 1):
            pr, pi = p_re[kk:kk + 1], p_im[kk:kk + 1]
            rows = slice(kk * SSM_GROUP, (kk + 1) * SSM_GROUP)
            wt_re[rows, :] = pr * cre - pi * cim
            wt_im[rows, :] = -pi * cre - pr * cim
            if kk < S5_CHUNK:
                back = slice((S5_CHUNK - 1 - kk) * SSM_GROUP, (S5_CHUNK - kk) * SSM_GROUP)
                br_re[back, :] = pr * bb_re - pi * bb_im
                br_im[back, :] = pi * bb_re + pr * bb_im
        kcat = (lax.dot_general(bb_re, wt_re[0:n_tap, :], nt, precision=hi, preferred_element_type=F32)
                + lax.dot_general(bb_im, wt_im[0:n_tap, :], nt, precision=hi, preferred_element_type=F32))
        for s in range(S5_CHUNK):
            shifted = kcat if s == 0 else pltpu.roll(kcat, s * SSM_GROUP, 1)
            rows = slice(s * SSM_GROUP, (s + 1) * SSM_GROUP)
            m_ref[0, gi, rows, :], m_ref[1, gi, rows, :] = _split_bf16(
                jnp.where(lane >= s * SSM_GROUP, shifted, 0.0))
        def place(v_re, v_im):
            zero = jnp.zeros_like(v_re)
            parts = [v_re, zero, v_im, zero] if gi == 0 else [zero, v_re, zero, v_im]
            return jnp.concatenate(parts, axis=1)

        ws_rows.append(place(br_re[...], br_im[...]))
        wot_rows.append(place(wt_re[SSM_GROUP:, :], wt_im[SSM_GROUP:, :]))
        kc = float(S5_CHUNK) * lax.broadcasted_iota(jnp.int32, (2 * SUBLANES, SSM_STATE), 0).astype(F32)
        magc = jnp.exp(kc * (dt * ar))
        angc = kc * (dt * ai)
        atab = atab + place(magc * jnp.cos(angc), magc * jnp.sin(angc))
    ws_ref[0], ws_ref[1] = _split_bf16(jnp.concatenate(ws_rows, axis=0))
    wot_ref[0], wot_ref[1] = _split_bf16(jnp.concatenate(wot_rows, axis=0))
    atab_ref[...] = atab


def _s5_prep(a_re, a_im, log_dt, b_re, b_im, c_re, c_im):
    p, n, c = N_PAIRS, SSM_STATE, SSM_GROUP
    n_tap = S5_CHUNK * c
    st = 4 * n

    def pspec(*shape):
        return pl.BlockSpec((None,) + shape, lambda i: (i,) + (0,) * len(shape))

    def pairs(a, *shape):
        return a.reshape((p, 2) + shape)

    return pl.pallas_call(
        _s5_prep_kernel,
        grid=(p,),
        in_specs=[pspec(2, 1, n), pspec(2, 1, n), pspec(2, 1, 1), pspec(2, c, n), pspec(2, c, n),
                  pspec(2, c, n), pspec(2, c, n)],
        out_specs=[pspec(2, 2, n_tap, n_tap), pspec(2, 2 * n_tap, st), pspec(2, 2 * n_tap, st),
                   pspec(2 * SUBLANES, st)],
        out_shape=[jax.ShapeDtypeStruct((p, 2, 2, n_tap, n_tap), BF16),
                   jax.ShapeDtypeStruct((p, 2, 2 * n_tap, st), BF16),
                   jax.ShapeDtypeStruct((p, 2, 2 * n_tap, st), BF16),
                   jax.ShapeDtypeStruct((p, 2 * SUBLANES, st), F32)],
        scratch_shapes=[pltpu.VMEM((n_tap + c, n), F32), pltpu.VMEM((n_tap + c, n), F32),
                        pltpu.VMEM((n_tap, n), F32), pltpu.VMEM((n_tap, n), F32)],
        compiler_params=_params("parallel"),
        name="s5_prep",
    )(pairs(a_re, 1, n), pairs(a_im, 1, n), pairs(log_dt, 1, 1),
      pairs(jnp.swapaxes(b_re, 1, 2), c, n), pairs(jnp.swapaxes(b_im, 1, 2), c, n),
      pairs(c_re, c, n), pairs(c_im, c, n))


def _block_transpose8(vs):
    lane = lax.broadcasted_iota(jnp.int32, vs[0].shape, 1)
    blk = lane >> 4
    for d in (4, 2, 1):
        keep = (blk & d) == 0
        new = list(vs)
        for i in range(GROUPS_PER_SLAB):
            if i & d == 0:
                a, b = vs[i], vs[i + d]
                new[i] = jnp.where(keep, a, pltpu.roll(b, d * SSM_GROUP, 1))
                new[i + d] = jnp.where(keep, pltpu.roll(a, LANES - d * SSM_GROUP, 1), b)
        vs = new
    return vs


def _cmul(ar, ai, xr, xi):
    return ar * xr - ai * xi, ar * xi + ai * xr


def _s5_kernel(u_ref, h0_ref, m_ref, ws_ref, wo_ref, a_ref, y_ref, hf_ref, x_scr, yg_scr, s_scr, hp_scr,
               *, rows, independent):
    half = 2 * SSM_STATE
    rt = min(rows, 16 * SUBLANES)
    half_chunk = S5_CHUNK // 2

    def gather_tile(t, carry):
        r0 = pl.multiple_of(t * rt, rt)
        for hf in range(2):
            vs = [u_ref[pl.ds(r0 * S5_CHUNK + hf * half_chunk + i, rt, stride=S5_CHUNK), :]
                  for i in range(half_chunk)]
            outs = _block_transpose8(vs)
            for g in range(GROUPS_PER_SLAB):
                x_scr[g, pl.ds(r0, rt), hf * LANES:(hf + 1) * LANES] = outs[g]
        return carry

    lax.fori_loop(0, rows // rt, gather_tile, 0)

    row = lax.broadcasted_iota(jnp.int32, (SUBLANES, half), 0)
    n_tap = S5_CHUNK * SSM_GROUP
    for pi in range(PAIRS_PER_SLAB):
        def part(w_ref, *idx):
            return w_ref[(pi, 0) + idx], (w_ref[(pi, 1) + idx] if w_ref.shape[1] == 2 else None)

        x0 = x_scr[2 * pi]
        x1 = x_scr[2 * pi + 1]
        s_scr[...] = _pdot(jnp.concatenate([x0, x1], axis=1), *part(ws_ref))
        ap = a_ref[pi]
        h0 = h0_ref[pi]
        if independent:
            hp_scr[...] = h0
            s = s_scr[...]
            n_re, n_im = _cmul(ap[1:2, :half], ap[1:2, half:], h0[:, :half], h0[:, half:])
            hf_ref[pi] = jnp.concatenate([n_re + s[:, :half], n_im + s[:, half:]], axis=1)
        else:
            pw_re, pw_im = ap[0:SUBLANES, :half], ap[0:SUBLANES, half:]

            def scan_tile(t, carry):
                h_re, h_im = carry
                r0 = pl.multiple_of(t * SUBLANES, SUBLANES)
                s = s_scr[pl.ds(r0, SUBLANES), :]
                t_re, t_im = s[:, :half], s[:, half:]
                for d in (1, 2, 4):
                    sh_re = jnp.where(row >= d, pltpu.roll(t_re, d, 0), 0.0)
                    sh_im = jnp.where(row >= d, pltpu.roll(t_im, d, 0), 0.0)
                    m_re, m_im = _cmul(ap[d:d + 1, :half], ap[d:d + 1, half:], sh_re, sh_im)
                    t_re, t_im = t_re + m_re, t_im + m_im
                e_re = jnp.where(row >= 1, pltpu.roll(t_re, 1, 0), 0.0)
                e_im = jnp.where(row >= 1, pltpu.roll(t_im, 1, 0), 0.0)
                c_re, c_im = _cmul(pw_re, pw_im, h_re, h_im)
                hp_scr[pl.ds(r0, SUBLANES), :] = jnp.concatenate([e_re + c_re, e_im + c_im], axis=1)
                o_re, o_im = _cmul(ap[SUBLANES:SUBLANES + 1, :half], ap[SUBLANES:SUBLANES + 1, half:],
                                   h_re, h_im)
                last = SUBLANES - 1
                n_re = jnp.broadcast_to(t_re[last:last + 1], h_re.shape) + o_re
                n_im = jnp.broadcast_to(t_im[last:last + 1], h_im.shape) + o_im
                return n_re, n_im

            init = (jnp.broadcast_to(h0[:, :half], (SUBLANES, half)),
                    jnp.broadcast_to(h0[:, half:], (SUBLANES, half)))
            h_re, h_im = lax.fori_loop(0, rows // SUBLANES, scan_tile, init, unroll=4)
            hf_ref[pi] = jnp.concatenate([h_re[0:1], h_im[0:1]], axis=1)
        yc = _pdot(hp_scr[...], *part(wo_ref), dims=(((1,), (1,)), ((), ())))
        yg_scr[2 * pi] = _pdot(x0, *part(m_ref, 0)) + yc[:, :n_tap]
        yg_scr[2 * pi + 1] = _pdot(x1, *part(m_ref, 1)) + yc[:, n_tap:]

    def scatter_tile(t, carry):
        r0 = pl.multiple_of(t * rt, rt)
        for hf in range(2):
            vs = [yg_scr[g, pl.ds(r0, rt), hf * LANES:(hf + 1) * LANES] for g in range(GROUPS_PER_SLAB)]
            outs = _block_transpose8(vs)
            for i in range(half_chunk):
                y_ref[pl.ds(r0 * S5_CHUNK + hf * half_chunk + i, rt, stride=S5_CHUNK), :] = outs[i]
        return carry

    lax.fori_loop(0, rows // rt, scatter_tile, 0)


def _s5_mixer(u, h0_re, h0_im, prep, bsz, t_len, precise):
    m, wsp, wop, a16 = prep
    parts = 2 if precise else 1
    n_tap = S5_CHUNK * SSM_GROUP
    st = 4 * SSM_STATE
    independent = t_len == S5_CHUNK
    if independent:
        nblk, rows, hrows = 1, bsz, bsz
    else:
        nblk, rows, hrows = bsz, t_len // S5_CHUNK, 1
    assert t_len % S5_CHUNK == 0 and rows % SUBLANES == 0, (bsz, t_len)
    h0p = jnp.concatenate([h0_re.reshape(bsz, N_PAIRS, 2 * SSM_STATE),
                           h0_im.reshape(bsz, N_PAIRS, 2 * SSM_STATE)], axis=2).astype(F32)
    h0p = h0p.transpose(1, 0, 2)[None] if independent else h0p[:, :, None, :]
    pp = PAIRS_PER_SLAB

    def wspec(*shape):
        return pl.BlockSpec((pp,) + shape, lambda q, b: (q,) + (0,) * len(shape))

    frames = rows * S5_CHUNK
    y, hf = pl.pallas_call(
        functools.partial(_s5_kernel, rows=rows, independent=independent),
        grid=(N_SLABS, nblk),
        in_specs=[pl.BlockSpec((None, frames, LANES), lambda q, b: (q, b, 0)),
                  pl.BlockSpec((None, pp, hrows, st), lambda q, b: (b, q, 0, 0)),
                  wspec(parts, 2, n_tap, n_tap), wspec(parts, 2 * n_tap, st), wspec(parts, 2 * n_tap, st),
                  wspec(2 * SUBLANES, st)],
        out_specs=[pl.BlockSpec((None, frames, LANES), lambda q, b: (q, b, 0)),
                   pl.BlockSpec((None, pp, hrows, st), lambda q, b: (b, q, 0, 0))],
        out_shape=[jax.ShapeDtypeStruct(u.shape, F32),
                   jax.ShapeDtypeStruct((nblk, N_PAIRS, hrows, st), F32)],
        scratch_shapes=[pltpu.VMEM((GROUPS_PER_SLAB, rows, n_tap), F32),
                        pltpu.VMEM((GROUPS_PER_SLAB, rows, n_tap), F32),
                        pltpu.VMEM((rows, st), F32), pltpu.VMEM((rows, st), F32)],
        compiler_params=_params("parallel", "parallel"),
        name="s5_core",
    )(u, h0p, m, wsp, wop, a16)
    hf = hf[0].transpose(1, 0, 2) if independent else hf[:, :, 0, :]
    hf_re = hf[:, :, :2 * SSM_STATE].reshape(bsz, N_GROUPS, SSM_STATE)
    hf_im = hf[:, :, 2 * SSM_STATE:].reshape(bsz, N_GROUPS, SSM_STATE)
    return y, hf_re, hf_im


def _conv_kernel(v_ref, left_ref, w_ref, b_ref, g_ref, beta_ref, y_ref, cn_ref, vbuf, shifted, *, tm):
    @pl.when(pl.program_id(1) == 0)
    def _():
        vbuf[0:CONV_HALO, :] = left_ref[...]

    vbuf[CONV_HALO:CONV_HALO + tm, :] = v_ref[...]
    first = CONV_HALO - (CONV_WIDTH - 1)
    span = tm + CONV_HALO - SUBLANES
    for r in range(1, SUBLANES):
        shifted[r - 1, 0:span, :] = vbuf[r:r + span, :]
    acc = jnp.zeros((tm, D_CONV), F32)
    for k in range(CONV_WIDTH):
        a, r = divmod(first + k, SUBLANES)
        src = vbuf if r == 0 else shifted.at[r - 1]
        acc = acc + w_ref[k:k + 1, :] * src[a * SUBLANES:a * SUBLANES + tm, :]
    y = acc + b_ref[...]
    mu = jnp.mean(y, axis=-1, keepdims=True)
    yc = y - mu
    var = jnp.mean(yc * yc, axis=-1, keepdims=True)
    yn = yc * lax.rsqrt(var + EPS) * g_ref[...] + beta_ref[...]
    y_ref[...] = yn * jax.nn.sigmoid(yn)
    cn_ref[...] = vbuf[tm + first:tm + CONV_HALO, :]
    vbuf[0:CONV_HALO, :] = vbuf[tm:tm + CONV_HALO, :]


def _conv_mixer(v, left, w, b, g, beta, layer, bsz, t_len, tm):
    left = jnp.pad(left.astype(F32), ((0, 0), (CONV_HALO - (CONV_WIDTH - 1), 0), (0, 0)))
    nt = t_len // tm
    vec = _layer_spec(layer, 1, D_CONV)
    return pl.pallas_call(
        functools.partial(_conv_kernel, tm=tm),
        grid=(bsz, nt),
        in_specs=[pl.BlockSpec((tm, D_CONV), lambda bi, j: (bi * nt + j, 0)),
                  pl.BlockSpec((None, CONV_HALO, D_CONV), lambda bi, j: (bi, 0, 0)),
                  _layer_spec(layer, CONV_WIDTH, D_CONV), vec, vec, vec],
        out_specs=[pl.BlockSpec((tm, D_CONV), lambda bi, j: (bi * nt + j, 0)),
                   pl.BlockSpec((None, CONV_WIDTH - 1, D_CONV), lambda bi, j: (bi, 0, 0))],
        out_shape=[jax.ShapeDtypeStruct((bsz * t_len, D_CONV), F32),
                   jax.ShapeDtypeStruct((bsz, CONV_WIDTH - 1, D_CONV), F32)],
        scratch_shapes=[pltpu.VMEM((tm + CONV_HALO, D_CONV), F32),
                        pltpu.VMEM((SUBLANES - 1, tm + CONV_HALO - SUBLANES, D_CONV), F32)],
        compiler_params=_params("parallel", "arbitrary"),
        name="conv_mixer",
    )(v, left, w, _vec(b), _vec(g), _vec(beta))


def _route(logits):
    lane_i = lax.broadcasted_iota(jnp.int32, logits.shape, 1)
    lane = lane_i.astype(F32)
    group_of_lane = (lane_i >> 3).astype(F32)
    neg = -jnp.inf
    far = float(LANES)
    is_g = (lane_i >= N_EXPERTS) & (lane_i < N_EXPERTS + N_EXPERT_GROUPS)
    gl = jnp.where(is_g, logits, neg)
    g_max = jnp.max(gl, axis=-1, keepdims=True)
    g_lane = jnp.min(jnp.where(gl == g_max, lane, far), axis=-1, keepdims=True)
    g_gate = 1.0 / jnp.sum(jnp.exp(gl - g_max), axis=-1, keepdims=True)
    g_idx = g_lane - float(N_EXPERTS)
    in_group = (lane_i < N_EXPERTS) & (group_of_lane == g_idx)
    el = jnp.where(in_group, logits, neg)
    v1 = jnp.max(el, axis=-1, keepdims=True)
    i1 = jnp.min(jnp.where(el == v1, lane, far), axis=-1, keepdims=True)
    el2 = jnp.where(lane == i1, neg, el)
    v2 = jnp.max(el2, axis=-1, keepdims=True)
    i2 = jnp.min(jnp.where(el2 == v2, lane, far), axis=-1, keepdims=True)
    e2 = jnp.exp(v2 - v1)
    w1 = g_gate / (1.0 + e2)
    w2 = g_gate * e2 / (1.0 + e2)
    return i1, i2, w1, w2


ROUTE_E1, ROUTE_E2, ROUTE_W1, ROUTE_W2, ROUTE_RANK1, ROUTE_RANK2 = range(6)


def _outproj_kernel(x_ref, ys_ref, u_ref, yb_ref, d_ref, wglu_ref, wout_ref, nf_ref, rw_ref, rb_ref, cnt0_ref,
                    x1_ref, hn_ref, route_ref, rt_ref, cnt_ref, wglu_bf, wout_bf, rw_bf):
    @pl.when(pl.program_id(0) == 0)
    def _():
        cnt_ref[...] = cnt0_ref[...]
        rw = rw_ref[...]
        rw_hi = rw.astype(BF16)
        rw_bf[:, 0:LANES] = rw_hi
        rw_bf[:, LANES:] = (rw - rw_hi.astype(F32)).astype(BF16)

    _cast_weight_once(wglu_ref, wglu_bf)
    _cast_weight_once(wout_ref, wout_bf)
    ys = jnp.concatenate([ys_ref[q] for q in range(N_SLABS)], axis=1)
    u = jnp.concatenate([u_ref[q] for q in range(N_SLABS)], axis=1)
    z = jax.nn.gelu(ys + d_ref[...] * u)
    ya = z * jax.nn.sigmoid(_pdot(z, *_parts(wglu_bf)))
    mix = (_pdot(ya, *_parts(wout_bf, slice(0, D_SSM)))
           + _pdot(yb_ref[...], *_parts(wout_bf, slice(D_SSM, D_MODEL))))
    x1 = x_ref[...] + mix
    x1_ref[...] = x1
    hn = _rms(x1, nf_ref[...])
    _store_row_tiles(hn_ref, hn)
    h_hi = hn.astype(BF16)
    h_lo = (hn - h_hi.astype(F32)).astype(BF16)
    hw = jnp.dot(h_hi, rw_bf[...], preferred_element_type=F32)
    logits = (hw[:, :LANES] + hw[:, LANES:]
              + jnp.dot(h_lo, rw_bf[:, 0:LANES], preferred_element_type=F32) + rb_ref[...])
    i1, i2, w1, w2 = _route(logits)
    tm = logits.shape[0]
    lane_i = lax.broadcasted_iota(jnp.int32, logits.shape, 1)
    lane = lane_i.astype(F32)
    picked = jnp.where((lane == i1) | (lane == i2), 1.0, 0.0)
    earlier = (lax.broadcasted_iota(jnp.int32, (tm, tm), 0) > lax.broadcasted_iota(jnp.int32, (tm, tm), 1))
    prefix = _bdot(jnp.where(earlier, 1.0, 0.0), picked.astype(BF16)) + cnt_ref[...]
    rank1 = jnp.sum(jnp.where(lane == i1, prefix, 0.0), axis=-1, keepdims=True)
    rank2 = jnp.sum(jnp.where(lane == i2, prefix, 0.0), axis=-1, keepdims=True)
    cnt_ref[...] += jnp.sum(picked, axis=0, keepdims=True)
    rec = jnp.zeros_like(logits)
    for lane_id, val in ((ROUTE_E1, i1), (ROUTE_E2, i2), (ROUTE_W1, w1), (ROUTE_W2, w2),
                         (ROUTE_RANK1, rank1), (ROUTE_RANK2, rank2)):
        rec = jnp.where(lane_i == lane_id, val, rec)
    route_ref[...] = rec
    rt_ref[...] = rec.T[0:SUBLANES, :]


def _outproj(x, ys, u, yb, d, wglu, wout, nf, rw, rb, counts0, layer, tm, precise):
    n = x.shape[0]
    return pl.pallas_call(
        _outproj_kernel,
        grid=(n // tm,),
        in_specs=[_row_spec(tm, D_MODEL), _slab_spec(tm), _slab_spec(tm), _row_spec(tm, D_CONV),
                  _layer_spec(layer, 1, D_SSM), _layer_spec(layer, D_SSM, D_SSM),
                  _layer_spec(layer, D_MODEL, D_MODEL), _layer_spec(layer, 1, D_MODEL),
                  _const_spec((D_MODEL, LANES)), _const_spec((1, LANES)), _const_spec((1, LANES))],
        out_specs=[_row_spec(tm, D_MODEL), _row_tile_spec(tm), _row_spec(tm, LANES),
                   pl.BlockSpec((SUBLANES, tm), lambda i: (0, i)), _const_spec((1, LANES))],
        out_shape=[jax.ShapeDtypeStruct((n, D_MODEL), F32), jax.ShapeDtypeStruct(_tiled_rows(n), F32),
                   jax.ShapeDtypeStruct((n, LANES), F32), jax.ShapeDtypeStruct((SUBLANES, n), F32),
                   jax.ShapeDtypeStruct((1, LANES), F32)],
        scratch_shapes=[_weight_scratch(D_SSM, D_SSM, precise), _weight_scratch(D_MODEL, D_MODEL, precise),
                        pltpu.VMEM((D_MODEL, 2 * LANES), BF16)],
        compiler_params=_params("arbitrary"),
        name="outproj_router",
    )(x, ys, u, yb, _vec(d), wglu, wout, _vec(nf), rw, rb, counts0)


PLAN_TILE_LANES = 2 * LANES
PLAN_EXPERT, PLAN_ROWS, PLAN_USED = range(3)


def _moe_plan_kernel(rt_ref, cnt_ref, cnt_first_ref, slot_ref, tile_ref, *, tms):
    cnt = cnt_ref[...]
    padded = jnp.ceil(cnt * (1.0 / tms)) * float(tms)
    r = lax.broadcasted_iota(jnp.int32, (LANES, LANES), 0)
    c = lax.broadcasted_iota(jnp.int32, (LANES, LANES), 1)
    ends = jnp.dot(padded, jnp.where(r <= c, 1.0, 0.0), precision=lax.Precision.HIGHEST,
                   preferred_element_type=F32)
    starts = ends - padded
    rt = rt_ref[...]
    e1, e2 = rt[ROUTE_E1:ROUTE_E1 + 1], rt[ROUTE_E2:ROUTE_E2 + 1]
    s1, s2 = rt[ROUTE_RANK1:ROUTE_RANK1 + 1], rt[ROUTE_RANK2:ROUTE_RANK2 + 1]
    tile = lax.broadcasted_iota(jnp.int32, (1, PLAN_TILE_LANES), 1).astype(F32)
    used = ends[:, N_EXPERTS - 1:N_EXPERTS] * (1.0 / tms)
    pos = jnp.minimum(tile, used - 1.0) * float(tms)
    t_exp = jnp.zeros_like(tile)
    t_fill = jnp.zeros_like(tile)
    for e in range(N_EXPERTS):
        st, en = starts[:, e:e + 1], ends[:, e:e + 1]
        s1 = s1 + jnp.where(e1 == float(e), st, 0.0)
        s2 = s2 + jnp.where(e2 == float(e), st, 0.0)
        mine = (pos >= st) & (pos < en)
        t_exp = t_exp + jnp.where(mine, float(e), 0.0)
        t_fill = t_fill + jnp.where(mine, st + cnt_first_ref[:, e:e + 1], 0.0)
    t_rows = jnp.where(tile < used, jnp.clip(t_fill - pos, 0.0, float(tms)), 0.0)
    slot_ref[...] = jnp.concatenate([s1, s2], axis=0).astype(jnp.int32)
    tile_ref[...] = jnp.concatenate(
        [t_exp, t_rows, jnp.broadcast_to(used, tile.shape), jnp.zeros((SUBLANES - 3, PLAN_TILE_LANES), F32)],
        axis=0).astype(jnp.int32)


def _moe_plan(route_t, counts, counts_first, tms, n_tiles):
    n = route_t.shape[1]
    assert n_tiles <= PLAN_TILE_LANES
    slots, tiles = pl.pallas_call(
        functools.partial(_moe_plan_kernel, tms=tms),
        out_shape=[jax.ShapeDtypeStruct((2, n), jnp.int32),
                   jax.ShapeDtypeStruct((SUBLANES, PLAN_TILE_LANES), jnp.int32)],
        compiler_params=pltpu.CompilerParams(vmem_limit_bytes=VMEM_LIMIT),
        name="moe_plan",
    )(route_t, counts, counts_first)
    return slots, tiles[PLAN_EXPERT, :n_tiles], tiles[PLAN_ROWS, :n_tiles], tiles[PLAN_USED, :1]


DMA_UNROLL = 8


HN_BUFFERS = 3


def _dispatch_kernel(tr_ref, slot_ref, hn_ref, *rest, tm, tms, n_tiles, n_steps, fresh):
    xs_ref, zbuf, hbuf, in_sem, out_sem, zsem = rest if fresh else rest[1:]
    i = pl.program_id(0)

    def fetch(t):
        b = lax.rem(t, HN_BUFFERS)
        first = pl.multiple_of(t * (tm * ROW_SUB), tm * ROW_SUB)
        return pltpu.make_async_copy(hn_ref.at[pl.ds(first, tm * ROW_SUB)], hbuf.at[b], in_sem.at[b])

    def drain_scatter(t):
        b = lax.rem(t, HN_BUFFERS)
        for k in range(2):
            pltpu.make_async_copy(hbuf.at[b], xs_ref.at[pl.ds(0, tm * ROW_SUB)], out_sem.at[b]).wait()

    @pl.when(i == 0)
    def _():
        fetch(i).start()

    @pl.when(jnp.logical_and(i == 0, fresh))
    def _():
        zbuf[...] = jnp.zeros_like(zbuf)

        def fill(t, carry):
            @pl.when(tr_ref[t] < tms)
            def _():
                first = pl.multiple_of(t * (tms * ROW_SUB), tms * ROW_SUB)
                pltpu.make_async_copy(zbuf, xs_ref.at[pl.ds(first, tms * ROW_SUB)], zsem).start()
            return carry

        def drain(t, carry):
            @pl.when(tr_ref[t] < tms)
            def _():
                pltpu.make_async_copy(zbuf, xs_ref.at[pl.ds(0, tms * ROW_SUB)], zsem).wait()
            return carry

        lax.fori_loop(0, n_tiles, fill, 0)
        lax.fori_loop(0, n_tiles, drain, 0)

    @pl.when(i + 1 < n_steps)
    def _():
        fetch(i + 1).start()

    fetch(i).wait()
    b = lax.rem(i, HN_BUFFERS)
    rows = hbuf.at[b]

    def issue(r, carry):
        for k in range(2):
            pltpu.make_async_copy(_one_row(rows, r), _one_row(xs_ref, slot_ref[k, r]),
                                  out_sem.at[b]).start(priority=k)
        return carry

    lax.fori_loop(0, tm, issue, 0, unroll=DMA_UNROLL)

    @pl.when(i >= 1)
    def _():
        drain_scatter(i - 1)

    @pl.when(i == n_steps - 1)
    def _():
        drain_scatter(i)


def _dispatch(hn, slots, tile_rows, tms, tm, into=None):
    n = hn.shape[0] // ROW_SUB
    n_tiles = tile_rows.shape[0]
    fresh = into is None
    any_spec = pl.BlockSpec(memory_space=pl.ANY)
    return pl.pallas_call(
        functools.partial(_dispatch_kernel, tm=tm, tms=tms, n_tiles=n_tiles, n_steps=n // tm, fresh=fresh),
        grid_spec=pltpu.PrefetchScalarGridSpec(
            num_scalar_prefetch=1, grid=(n // tm,),
            in_specs=[pl.BlockSpec((2, tm), lambda i, tr: (0, i), memory_space=pltpu.SMEM), any_spec]
                     + ([] if fresh else [any_spec]),
            out_specs=any_spec,
            scratch_shapes=[pltpu.VMEM(_tiled_rows(tms), F32), pltpu.VMEM((HN_BUFFERS,) + _tiled_rows(tm), F32),
                            pltpu.SemaphoreType.DMA((HN_BUFFERS,)), pltpu.SemaphoreType.DMA((HN_BUFFERS,)),
                            pltpu.SemaphoreType.DMA]),
        out_shape=jax.ShapeDtypeStruct(_tiled_rows(n_tiles * tms), F32),
        input_output_aliases={} if fresh else {3: 0},
        compiler_params=_params("arbitrary"),
        name="moe_dispatch",
    )(tile_rows, slots, hn, *([] if fresh else [into]))


def _moe_kernel(te_ref, nu_ref, x_ref, wg_ref, wu_ref, wd_ref, y_ref, wg_bf, wu_bf, wd_bf):
    i = pl.program_id(0)
    in_use = i < nu_ref[0]
    new_expert = (i == 0) | (te_ref[i] != te_ref[jnp.maximum(i - 1, 0)])

    @pl.when(in_use & new_expert)
    def _():
        wg_bf[...] = wg_ref[...].astype(BF16)
        wu_bf[...] = wu_ref[...].astype(BF16)
        wd_bf[...] = wd_ref[...].astype(BF16)

    @pl.when(in_use)
    def _():
        h = _load_row_tiles(x_ref).astype(BF16)
        hg = jnp.dot(h, wg_bf[...], preferred_element_type=F32)
        hu = jnp.dot(h, wu_bf[...], preferred_element_type=F32)
        _store_row_tiles(y_ref, _bdot(hg * jax.nn.sigmoid(hg) * hu, wd_bf[...]))

    @pl.when(jnp.logical_not(in_use))
    def _():
        y_ref[...] = jnp.zeros_like(y_ref)


def _moe(xs, tile_expert, n_used, wg, wu, wd, layer, tms):
    n_slots = xs.shape[0] // ROW_SUB
    rows = pl.BlockSpec(_tiled_rows(tms), lambda i, te, nu: (jnp.minimum(i, nu[0] - 1), 0))
    out_rows = pl.BlockSpec(_tiled_rows(tms), lambda i, te, nu: (i, 0))

    def wspec(a, b):
        return pl.BlockSpec((None, None, a, b), lambda i, te, nu: (layer, te[i], 0, 0))

    return pl.pallas_call(
        _moe_kernel,
        grid_spec=pltpu.PrefetchScalarGridSpec(
            num_scalar_prefetch=2, grid=(n_slots // tms,),
            in_specs=[rows, wspec(D_MODEL, D_EXPERT), wspec(D_MODEL, D_EXPERT), wspec(D_EXPERT, D_MODEL)],
            out_specs=out_rows,
            scratch_shapes=[pltpu.VMEM((D_MODEL, D_EXPERT), BF16), pltpu.VMEM((D_MODEL, D_EXPERT), BF16),
                            pltpu.VMEM((D_EXPERT, D_MODEL), BF16)]),
        out_shape=jax.ShapeDtypeStruct(xs.shape, F32),
        compiler_params=_params("arbitrary"),
        name="moe",
    )(tile_expert, n_used, xs, wg, wu, wd)


def _ple_kernel(slot_ref, next_slot_ref, x_ref, route_ref, p_ref, np_ref, wple_ref, wgate_ref, nfin_ref,
                ys_ref, o_ref, ybuf, sem, wple_bf, wgate_bf, *, tm, n_steps, final):
    i = pl.program_id(0)
    _cast_weight_once(wple_ref, wple_bf)
    _cast_weight_once(wgate_ref, wgate_bf)

    def gather(slots, b):
        def issue(r, carry):
            for k in range(2):
                pltpu.make_async_copy(_one_row(ys_ref, slots[k, r]), _one_row(ybuf.at[b, k], r),
                                      sem.at[b]).start(priority=k)
            return carry

        lax.fori_loop(0, tm, issue, 0, unroll=DMA_UNROLL)

    @pl.when(i == 0)
    def _():
        gather(slot_ref, 0)

    @pl.when(i + 1 < n_steps)
    def _():
        gather(next_slot_ref, lax.rem(i + 1, 2))

    pe = _pdot(p_ref[...], *_parts(wple_bf))
    b = lax.rem(i, 2)
    for k in range(2):
        pltpu.make_async_copy(ys_ref.at[pl.ds(0, tm * ROW_SUB)], ybuf.at[b, k], sem.at[b]).wait()
    route = route_ref[...]
    x = (x_ref[...] + route[:, ROUTE_W1:ROUTE_W1 + 1] * _load_row_tiles(ybuf.at[b, 0])
         + route[:, ROUTE_W2:ROUTE_W2 + 1] * _load_row_tiles(ybuf.at[b, 1]))
    gate = jax.nn.sigmoid(_pdot(_rms(x, np_ref[...]), *_parts(wgate_bf)))
    out = x + pe * gate
    if final:
        out = _rms(out, nfin_ref[...])
    o_ref[...] = out


def _ple(x, route, slots, ys, p, npl, wple, wgate, nfin, layer, tm, final):
    n = x.shape[0]
    n_steps = n // tm
    return pl.pallas_call(
        functools.partial(_ple_kernel, tm=tm, n_steps=n_steps, final=final),
        grid=(n_steps,),
        in_specs=[pl.BlockSpec((2, tm), lambda i: (0, i), memory_space=pltpu.SMEM),
                  pl.BlockSpec((2, tm), lambda i: (0, jnp.minimum(i + 1, n_steps - 1)),
                               memory_space=pltpu.SMEM),
                  _row_spec(tm, D_MODEL), _row_spec(tm, LANES),
                  pl.BlockSpec((None, tm, D_PLE), lambda i: (layer, i, 0)),
                  _layer_spec(layer, 1, D_MODEL), _layer_spec(layer, D_PLE, D_MODEL),
                  _layer_spec(layer, D_MODEL, D_MODEL), _const_spec((1, D_MODEL)),
                  pl.BlockSpec(memory_space=pl.ANY)],
        out_specs=_row_spec(tm, D_MODEL),
        out_shape=jax.ShapeDtypeStruct((n, D_MODEL), F32),
        scratch_shapes=[pltpu.VMEM((2, 2) + _tiled_rows(tm), F32), pltpu.SemaphoreType.DMA((2,)),
                        _weight_scratch(D_PLE, D_MODEL, False), _weight_scratch(D_MODEL, D_MODEL, False)],
        compiler_params=_params("arbitrary"),
        name="combine_ple",
    )(slots, slots, x, route, p, _vec(npl), wple, wgate, nfin.reshape(1, D_MODEL), ys)


def _mixers(x, h0_re, h0_im, conv_left, w, s5, router_w, router_b, counts0, layer, bsz, t_len, tm, tm_conv,
            precise):
    u, v = _inproj(x, w["norm_mix"], w["w_in"], layer, min(2 * tm, x.shape[0]), precise)
    ys, hf_re, hf_im = _s5_mixer(u, h0_re, h0_im, s5, bsz, t_len, precise)
    yb, conv_new = _conv_mixer(v, conv_left, w["conv_w"], w["conv_b"], w["conv_ln_g"], w["conv_ln_b"],
                               layer, bsz, t_len, tm_conv)
    x1, hn, route, route_t, counts = _outproj(x, ys, u, yb, w["ssm_d"], w["w_ssm_glu"], w["w_out"],
                                              w["norm_ffn"], router_w, router_b, counts0, layer, tm, precise)
    return dict(x1=x1, hn=hn, route=route, route_t=route_t, counts=counts, state=(hf_re, hf_im, conv_new))


def _moe_and_ple(sets, ps, tms_rows, w, layer, tms, final):
    sizes = [s["x1"].shape[0] for s in sets]
    n_tiles = 2 * sum(sizes) // tms + N_EXPERTS
    route_t = jnp.concatenate([s["route_t"] for s in sets], axis=1)
    slots, tile_expert, tile_rows, n_used = _moe_plan(route_t, sets[-1]["counts"], sets[0]["counts"], tms,
                                                     n_tiles)
    starts = [sum(sizes[:j]) for j in range(len(sets))]
    set_slots = [slots[:, a:a + n] for a, n in zip(starts, sizes)]
    xsort = None
    for s, sl, tm in zip(sets, set_slots, tms_rows):
        xsort = _dispatch(s["hn"], sl, tile_rows, tms, tm, into=xsort)
    ysort = _moe(xsort, tile_expert, n_used, w["expert_w_gate"], w["expert_w_up"], w["expert_w_down"],
                 layer, tms)
    return [_ple(s["x1"], s["route"], sl, ysort, p, w["norm_ple"], w["ple_w"], w["ple_gate_w"],
                 w["norm_final"], layer, tm, final)
            for s, sl, p, tm in zip(sets, set_slots, ps, tms_rows)]


def kernel(x_prompt, x_sample, p_prompt, p_sample, state_ssm_re, state_ssm_im, cache_conv, norm_mix, w_in, ssm_a_re, ssm_a_im, ssm_b_re, ssm_b_im, ssm_c_re, ssm_c_im, ssm_d, ssm_log_dt, w_ssm_glu, conv_w, conv_b, conv_ln_g, conv_ln_b, w_out, norm_ffn, router_group_w, router_group_b, router_expert_w, router_expert_b, expert_w_gate, expert_w_up, expert_w_down, norm_ple, ple_w, ple_gate_w, norm_final):
    depth = w_in.shape[0]
    bp, tp, _ = x_prompt.shape
    bs, ts, _ = x_sample.shape
    xp = x_prompt.reshape(bp * tp, D_MODEL)
    xs = x_sample.reshape(bs * ts, D_MODEL)
    pp = p_prompt.reshape(depth, bp * tp, D_PLE)
    ps = p_sample.reshape(depth, bs * ts, D_PLE)
    zero_state = jnp.zeros((bp, N_GROUPS, SSM_STATE), F32)
    zero_conv = jnp.zeros((bp, CONV_WIDTH - 1, D_CONV), F32)
    pad_lanes = LANES - N_EXPERTS - N_EXPERT_GROUPS
    w = {"norm_mix": norm_mix, "w_in": w_in, "ssm_d": ssm_d, "w_ssm_glu": w_ssm_glu, "conv_w": conv_w,
         "conv_b": conv_b, "conv_ln_g": conv_ln_g, "conv_ln_b": conv_ln_b, "w_out": w_out,
         "norm_ffn": norm_ffn, "expert_w_gate": expert_w_gate, "expert_w_up": expert_w_up,
         "expert_w_down": expert_w_down, "norm_ple": norm_ple, "ple_w": ple_w, "ple_gate_w": ple_gate_w,
         "norm_final": norm_final}
    no_picks = jnp.zeros((1, LANES), F32)
    outs = {k: [] for k in ("pr_re", "pr_im", "pr_conv", "sm_re", "sm_im", "sm_conv")}
    for i in range(depth):
        s5 = _s5_prep(ssm_a_re[i], ssm_a_im[i], ssm_log_dt[i], ssm_b_re[i], ssm_b_im[i], ssm_c_re[i],
                      ssm_c_im[i])
        router_w = jnp.pad(jnp.concatenate([router_expert_w[i], router_group_w[i]], axis=1),
                           ((0, 0), (0, pad_lanes)))
        router_b = jnp.pad(jnp.concatenate([router_expert_b[i], router_group_b[i]]),
                           (0, pad_lanes)).reshape(1, LANES)
        final = i == depth - 1
        precise = not final
        mp = _mixers(xp, zero_state, zero_state, zero_conv, w, s5, router_w, router_b, no_picks, i,
                     bp, tp, tm=TOKEN_TILE, tm_conv=TOKEN_TILE, precise=precise)
        ms = _mixers(xs, state_ssm_re[i], state_ssm_im[i], cache_conv[i], w, s5, router_w, router_b,
                     mp["counts"], i, bs, ts, tm=bs * ts, tm_conv=ts, precise=precise)
        xp, xs = _moe_and_ple([mp, ms], [pp, ps], [TOKEN_TILE, bs * ts], w, i, SLOT_TILE, final)
        for key, val in zip(("pr_re", "pr_im", "pr_conv"), mp["state"]):
            outs[key].append(val)
        for key, val in zip(("sm_re", "sm_im", "sm_conv"), ms["state"]):
            outs[key].append(val)
    return (xp.reshape(bp, tp, D_MODEL), xs.reshape(bs, ts, D_MODEL),
            jnp.stack(outs["pr_re"]), jnp.stack(outs["pr_im"]), jnp.stack(outs["pr_conv"]),
            jnp.stack(outs["sm_re"]), jnp.stack(outs["sm_im"]), jnp.stack(outs["sm_conv"]))
```

```python
import functools

import jax
import jax.numpy as jnp
from jax import lax
from jax.experimental import pallas as pl
from jax.experimental.pallas import tpu as pltpu

F32 = jnp.float32
BF16 = jnp.bfloat16

D_MODEL = 1024
D_SSM = 512
SSM_GROUP = 16
N_GROUPS = D_SSM // SSM_GROUP
N_PAIRS = N_GROUPS // 2
SSM_STATE = 64
D_CONV = 512
CONV_WIDTH = 31
CONV_HALO = 32
N_EXPERT_GROUPS = 4
EXPERTS_PER_GROUP = 8
N_EXPERTS = 32
D_EXPERT = 256
D_PLE = 256
EPS = 1e-6
S5_CHUNK = 16
LANES = 128
SUBLANES = 8
N_SLABS = D_SSM // LANES
GROUPS_PER_SLAB = LANES // SSM_GROUP
PAIRS_PER_SLAB = GROUPS_PER_SLAB // 2
VMEM_LIMIT = 56 * 1024 * 1024
TOKEN_TILE = 512
SLOT_TILE = 256
PRECISE_TAIL = 1024


def _params(*sem):
    return pltpu.CompilerParams(dimension_semantics=sem, vmem_limit_bytes=VMEM_LIMIT)


def _rms(x, g):
    return x * lax.rsqrt(jnp.mean(x * x, axis=-1, keepdims=True) + EPS) * g


def _bdot(a, b):
    return jnp.dot(a.astype(BF16), b, preferred_element_type=F32)


def _split_bf16(a):
    hi = a.astype(BF16)
    return hi, (a - hi.astype(F32)).astype(BF16)


def _pdot(a, w_hi, w_lo=None, dims=(((1,), (0,)), ((), ()))):
    def mm(x, w):
        return lax.dot_general(x, w, dims, preferred_element_type=F32)

    if w_lo is None:
        return mm(a.astype(BF16), w_hi)
    a_hi, a_lo = _split_bf16(a)
    return mm(a_hi, w_hi) + (mm(a_hi, w_lo) + mm(a_lo, w_hi))


def _pdot_lo_terms(a, w_hi, w_lo, dims=(((1,), (0,)), ((), ()))):
    a_hi, a_lo = _split_bf16(a)
    return (lax.dot_general(a_hi, w_lo, dims, preferred_element_type=F32)
            + lax.dot_general(a_lo, w_hi, dims, preferred_element_type=F32))


def _stream_tail_tiles(t_len, tm, tail):
    tiles = max(t_len // tm, 1)
    return tiles, (max(t_len - tail, 0) // tm if tm < t_len else 0)


def _pdot_tail(a, w_hi, w_lo, acc_ref, tail_tiles):
    if w_lo is None:
        return _pdot(a, w_hi)
    tiles, first = tail_tiles
    if first == 0:
        return _pdot(a, w_hi, w_lo)
    acc_ref[...] = _pdot(a, w_hi)

    @pl.when(lax.rem(pl.program_id(0), tiles) >= first)
    def _():
        acc_ref[...] += _pdot_lo_terms(a, w_hi, w_lo)

    return acc_ref[...]


def _parts(wbf_ref, rows=slice(None)):
    return wbf_ref[0, rows, :], (wbf_ref[1, rows, :] if wbf_ref.shape[0] == 2 else None)


def _row_spec(tm, width):
    return pl.BlockSpec((tm, width), lambda i: (i, 0))


ROW_SUB = D_MODEL // LANES


def _tiled_rows(n):
    return (n * ROW_SUB, LANES)


def _row_tile_spec(tm):
    return pl.BlockSpec(_tiled_rows(tm), lambda i, *_: (i, 0))


def _one_row(ref, r):
    return ref.at[pl.ds(pl.multiple_of(r * ROW_SUB, ROW_SUB), ROW_SUB)]


def _store_row_tiles(ref, rows):
    n = rows.shape[0]
    for s in range(ROW_SUB):
        ref[pl.ds(s, n, stride=ROW_SUB), :] = rows[:, s * LANES:(s + 1) * LANES]


def _load_row_tiles(ref):
    n = ref.shape[0] // ROW_SUB
    return jnp.concatenate([ref[pl.ds(s, n, stride=ROW_SUB), :] for s in range(ROW_SUB)], axis=1)


def _slab_spec(tm):
    return pl.BlockSpec((N_SLABS, tm, LANES), lambda i: (0, i, 0))


def _const_spec(shape):
    return pl.BlockSpec(shape, lambda i: (0,) * len(shape))


def _layer_spec(layer, *shape):
    return pl.BlockSpec((None,) + shape, lambda *_: (layer,) + (0,) * len(shape))


def _vec(stacked):
    return stacked.reshape(stacked.shape[0], 1, stacked.shape[1])


def _cast_weight_once(w_ref, wbf_ref):
    @pl.when(pl.program_id(0) == 0)
    def _():
        w = w_ref[...]
        hi = w.astype(BF16)
        wbf_ref[0] = hi
        if wbf_ref.shape[0] == 2:
            wbf_ref[1] = (w - hi.astype(F32)).astype(BF16)


def _weight_scratch(k, n, precise):
    return pltpu.VMEM((2 if precise else 1, k, n), BF16)


def _tail_acc(rows, n, precise):
    return pltpu.VMEM((rows, n) if precise else (SUBLANES, LANES), F32)


def _inproj_kernel(x_ref, g_ref, w_ref, u_ref, v_ref, wbf, acc, *, tail_tiles):
    _cast_weight_once(w_ref, wbf)
    hn = _rms(x_ref[...], g_ref[...])
    proj = _pdot_tail(hn, *_parts(wbf), acc, tail_tiles)
    for q in range(N_SLABS):
        u_ref[q] = proj[:, q * LANES:(q + 1) * LANES]
    v_ref[...] = proj[:, D_SSM:D_SSM + D_CONV] * jax.nn.sigmoid(proj[:, D_SSM + D_CONV:])


def _inproj(x, g, w, layer, tm, t_len, tail):
    n = x.shape[0]
    d_in = w.shape[2]
    precise = tail > 0
    return pl.pallas_call(
        functools.partial(_inproj_kernel, tail_tiles=_stream_tail_tiles(t_len, tm, tail)),
        grid=(n // tm,),
        in_specs=[_row_spec(tm, D_MODEL), _layer_spec(layer, 1, D_MODEL), _layer_spec(layer, D_MODEL, d_in)],
        out_specs=[_slab_spec(tm), _row_spec(tm, D_CONV)],
        out_shape=[jax.ShapeDtypeStruct((N_SLABS, n, LANES), F32), jax.ShapeDtypeStruct((n, D_CONV), F32)],
        scratch_shapes=[_weight_scratch(D_MODEL, d_in, precise), _tail_acc(tm, d_in, precise)],
        compiler_params=_params("arbitrary"),
        name="inproj",
    )(x, _vec(g), w)


def _s5_prep_kernel(ar_ref, ai_ref, ldt_ref, bre_ref, bim_ref, cre_ref, cim_ref,
                    m_ref, ws_ref, wot_ref, atab_ref, wt_re, wt_im, br_re, br_im):
    n_tap = S5_CHUNK * SSM_GROUP
    st = 4 * SSM_STATE
    nt = (((1,), (1,)), ((), ()))
    hi = lax.Precision.HIGHEST
    lane = lax.broadcasted_iota(jnp.int32, (SSM_GROUP, n_tap), 1)
    ws_rows, wot_rows, atab = [], [], jnp.zeros((2 * SUBLANES, st), F32)
    for gi in range(2):
        ar, ai = ar_ref[gi], ai_ref[gi]
        dt = jnp.exp(ldt_ref[gi])
        k = lax.broadcasted_iota(jnp.int32, (S5_CHUNK + SUBLANES, SSM_STATE), 0).astype(F32)
        mag = jnp.exp(k * (dt * ar))
        ang = k * (dt * ai)
        p_re, p_im = mag * jnp.cos(ang), mag * jnp.sin(ang)
        inv = 1.0 / (ar * ar + ai * ai)
        ab_re, ab_im = p_re[1:2], p_im[1:2]
        ia_re, ia_im = ar * inv, -ai * inv
        coef_re = (ab_re - 1.0) * ia_re - ab_im * ia_im
        coef_im = (ab_re - 1.0) * ia_im + ab_im * ia_re
        bre, bim = bre_ref[gi], bim_ref[gi]
        bb_re = coef_re * bre - coef_im * bim
        bb_im = coef_re * bim + coef_im * bre
        cre, cim = cre_ref[gi], cim_ref[gi]
        for kk in range(S5_CHUNK + 1):
            pr, pi = p_re[kk:kk + 1], p_im[kk:kk + 1]
            rows = slice(kk * SSM_GROUP, (kk + 1) * SSM_GROUP)
            wt_re[rows, :] = pr * cre - pi * cim
            wt_im[rows, :] = -pi * cre - pr * cim
            if kk < S5_CHUNK:
                back = slice((S5_CHUNK - 1 - kk) * SSM_GROUP, (S5_CHUNK - kk) * SSM_GROUP)
                br_re[back, :] = pr * bb_re - pi * bb_im
                br_im[back, :] = pi * bb_re + pr * bb_im
        kcat = (lax.dot_general(bb_re, wt_re[0:n_tap, :], nt, precision=hi, preferred_element_type=F32)
                + lax.dot_general(bb_im, wt_im[0:n_tap, :], nt, precision=hi, preferred_element_type=F32))
        for s in range(S5_CHUNK):
            shifted = kcat if s == 0 else pltpu.roll(kcat, s * SSM_GROUP, 1)
            rows = slice(s * SSM_GROUP, (s + 1) * SSM_GROUP)
            m_ref[0, gi, rows, :], m_ref[1, gi, rows, :] = _split_bf16(
                jnp.where(lane >= s * SSM_GROUP, shifted, 0.0))
        def place(v_re, v_im):
            zero = jnp.zeros_like(v_re)
            parts = [v_re, zero, v_im, zero] if gi == 0 else [zero, v_re, zero, v_im]
            return jnp.concatenate(parts, axis=1)

        ws_rows.append(place(br_re[...], br_im[...]))
        wot_rows.append(place(wt_re[SSM_GROUP:, :], wt_im[SSM_GROUP:, :]))
        kc = float(S5_CHUNK) * lax.broadcasted_iota(jnp.int32, (2 * SUBLANES, SSM_STATE), 0).astype(F32)
        magc = jnp.exp(kc * (dt * ar))
        angc = kc * (dt * ai)
        atab = atab + place(magc * jnp.cos(angc), magc * jnp.sin(angc))
    ws_ref[0], ws_ref[1] = _split_bf16(jnp.concatenate(ws_rows, axis=0))
    wot_ref[0], wot_ref[1] = _split_bf16(jnp.concatenate(wot_rows, axis=0))
    atab_ref[...] = atab


def _s5_prep(a_re, a_im, log_dt, b_re, b_im, c_re, c_im):
    p, n, c = N_PAIRS, SSM_STATE, SSM_GROUP
    n_tap = S5_CHUNK * c
    st = 4 * n

    def pspec(*shape):
        return pl.BlockSpec((None,) + shape, lambda i: (i,) + (0,) * len(shape))

    def pairs(a, *shape):
        return a.reshape((p, 2) + shape)

    return pl.pallas_call(
        _s5_prep_kernel,
        grid=(p,),
        in_specs=[pspec(2, 1, n), pspec(2, 1, n), pspec(2, 1, 1), pspec(2, c, n), pspec(2, c, n),
                  pspec(2, c, n), pspec(2, c, n)],
        out_specs=[pspec(2, 2, n_tap, n_tap), pspec(2, 2 * n_tap, st), pspec(2, 2 * n_tap, st),
                   pspec(2 * SUBLANES, st)],
        out_shape=[jax.ShapeDtypeStruct((p, 2, 2, n_tap, n_tap), BF16),
                   jax.ShapeDtypeStruct((p, 2, 2 * n_tap, st), BF16),
                   jax.ShapeDtypeStruct((p, 2, 2 * n_tap, st), BF16),
                   jax.ShapeDtypeStruct((p, 2 * SUBLANES, st), F32)],
        scratch_shapes=[pltpu.VMEM((n_tap + c, n), F32), pltpu.VMEM((n_tap + c, n), F32),
                        pltpu.VMEM((n_tap, n), F32), pltpu.VMEM((n_tap, n), F32)],
        compiler_params=_params("parallel"),
        name="s5_prep",
    )(pairs(a_re, 1, n), pairs(a_im, 1, n), pairs(log_dt, 1, 1),
      pairs(jnp.swapaxes(b_re, 1, 2), c, n), pairs(jnp.swapaxes(b_im, 1, 2), c, n),
      pairs(c_re, c, n), pairs(c_im, c, n))


def _block_transpose8(vs):
    lane = lax.broadcasted_iota(jnp.int32, vs[0].shape, 1)
    blk = lane >> 4
    for d in (4, 2, 1):
        keep = (blk & d) == 0
        new = list(vs)
        for i in range(GROUPS_PER_SLAB):
            if i & d == 0:
                a, b = vs[i], vs[i + d]
                new[i] = jnp.where(keep, a, pltpu.roll(b, d * SSM_GROUP, 1))
                new[i + d] = jnp.where(keep, pltpu.roll(a, LANES - d * SSM_GROUP, 1), b)
        vs = new
    return vs


def _cmul(ar, ai, xr, xi):
    return ar * xr - ai * xi, ar * xi + ai * xr


def _s5_kernel(u_ref, h0_ref, m_ref, ws_ref, wo_ref, a_ref, y_ref, hf_ref, x_scr, yg_scr, s_scr, hp_scr,
               *, rows, independent, tail_rows):
    half = 2 * SSM_STATE
    tail = pl.ds(rows - tail_rows, tail_rows)
    rt = min(rows, 16 * SUBLANES)
    half_chunk = S5_CHUNK // 2

    def gather_tile(t, carry):
        r0 = pl.multiple_of(t * rt, rt)
        for hf in range(2):
            vs = [u_ref[pl.ds(r0 * S5_CHUNK + hf * half_chunk + i, rt, stride=S5_CHUNK), :]
                  for i in range(half_chunk)]
            outs = _block_transpose8(vs)
            for g in range(GROUPS_PER_SLAB):
                x_scr[g, pl.ds(r0, rt), hf * LANES:(hf + 1) * LANES] = outs[g]
        return carry

    lax.fori_loop(0, rows // rt, gather_tile, 0)

    row = lax.broadcasted_iota(jnp.int32, (SUBLANES, half), 0)
    n_tap = S5_CHUNK * SSM_GROUP
    for pi in range(PAIRS_PER_SLAB):
        def part(w_ref, *idx):
            return w_ref[(pi, 0) + idx], w_ref[(pi, 1) + idx]

        x0 = x_scr[2 * pi]
        x1 = x_scr[2 * pi + 1]
        x01 = jnp.concatenate([x0, x1], axis=1)
        s_scr[...] = _pdot(x01, ws_ref[pi, 0])
        if tail_rows:
            s_scr[tail, :] += _pdot_lo_terms(x01[rows - tail_rows:], *part(ws_ref))
        ap = a_ref[pi]
        h0 = h0_ref[pi]
        if independent:
            hp_scr[...] = h0
            s = s_scr[...]
            n_re, n_im = _cmul(ap[1:2, :half], ap[1:2, half:], h0[:, :half], h0[:, half:])
            hf_ref[pi] = jnp.concatenate([n_re + s[:, :half], n_im + s[:, half:]], axis=1)
        else:
            pw_re, pw_im = ap[0:SUBLANES, :half], ap[0:SUBLANES, half:]

            def scan_tile(t, carry):
                h_re, h_im = carry
                r0 = pl.multiple_of(t * SUBLANES, SUBLANES)
                s = s_scr[pl.ds(r0, SUBLANES), :]
                t_re, t_im = s[:, :half], s[:, half:]
                for d in (1, 2, 4):
                    sh_re = jnp.where(row >= d, pltpu.roll(t_re, d, 0), 0.0)
                    sh_im = jnp.where(row >= d, pltpu.roll(t_im, d, 0), 0.0)
                    m_re, m_im = _cmul(ap[d:d + 1, :half], ap[d:d + 1, half:], sh_re, sh_im)
                    t_re, t_im = t_re + m_re, t_im + m_im
                e_re = jnp.where(row >= 1, pltpu.roll(t_re, 1, 0), 0.0)
                e_im = jnp.where(row >= 1, pltpu.roll(t_im, 1, 0), 0.0)
                c_re, c_im = _cmul(pw_re, pw_im, h_re, h_im)
                hp_scr[pl.ds(r0, SUBLANES), :] = jnp.concatenate([e_re + c_re, e_im + c_im], axis=1)
                o_re, o_im = _cmul(ap[SUBLANES:SUBLANES + 1, :half], ap[SUBLANES:SUBLANES + 1, half:],
                                   h_re, h_im)
                last = SUBLANES - 1
                n_re = jnp.broadcast_to(t_re[last:last + 1], h_re.shape) + o_re
                n_im = jnp.broadcast_to(t_im[last:last + 1], h_im.shape) + o_im
                return n_re, n_im

            init = (jnp.broadcast_to(h0[:, :half], (SUBLANES, half)),
                    jnp.broadcast_to(h0[:, half:], (SUBLANES, half)))
            h_re, h_im = lax.fori_loop(0, rows // SUBLANES, scan_tile, init, unroll=4)
            hf_ref[pi] = jnp.concatenate([h_re[0:1], h_im[0:1]], axis=1)
        nt = (((1,), (1,)), ((), ()))
        yc = _pdot(hp_scr[...], wo_ref[pi, 0], dims=nt)
        yg_scr[2 * pi] = _pdot(x0, m_ref[pi, 0, 0]) + yc[:, :n_tap]
        yg_scr[2 * pi + 1] = _pdot(x1, m_ref[pi, 0, 1]) + yc[:, n_tap:]
        if tail_rows:
            yc_lo = _pdot_lo_terms(hp_scr[tail, :], *part(wo_ref), dims=nt)
            yg_scr[2 * pi, tail, :] += _pdot_lo_terms(x0[rows - tail_rows:], *part(m_ref, 0)) + yc_lo[:, :n_tap]
            yg_scr[2 * pi + 1, tail, :] += (_pdot_lo_terms(x1[rows - tail_rows:], *part(m_ref, 1))
                                            + yc_lo[:, n_tap:])

    def scatter_tile(t, carry):
        r0 = pl.multiple_of(t * rt, rt)
        for hf in range(2):
            vs = [yg_scr[g, pl.ds(r0, rt), hf * LANES:(hf + 1) * LANES] for g in range(GROUPS_PER_SLAB)]
            outs = _block_transpose8(vs)
            for i in range(half_chunk):
                y_ref[pl.ds(r0 * S5_CHUNK + hf * half_chunk + i, rt, stride=S5_CHUNK), :] = outs[i]
        return carry

    lax.fori_loop(0, rows // rt, scatter_tile, 0)


def _s5_mixer(u, h0_re, h0_im, prep, bsz, t_len, tail):
    m, wsp, wop, a16 = prep
    parts = 2 if tail > 0 else 1
    n_tap = S5_CHUNK * SSM_GROUP
    st = 4 * SSM_STATE
    independent = t_len == S5_CHUNK
    if independent:
        nblk, rows, hrows = 1, bsz, bsz
    else:
        nblk, rows, hrows = bsz, t_len // S5_CHUNK, 1
    assert t_len % S5_CHUNK == 0 and rows % SUBLANES == 0, (bsz, t_len)
    h0p = jnp.concatenate([h0_re.reshape(bsz, N_PAIRS, 2 * SSM_STATE),
                           h0_im.reshape(bsz, N_PAIRS, 2 * SSM_STATE)], axis=2).astype(F32)
    h0p = h0p.transpose(1, 0, 2)[None] if independent else h0p[:, :, None, :]
    pp = PAIRS_PER_SLAB

    def wspec(*shape):
        return pl.BlockSpec((pp,) + shape, lambda q, b: (q,) + (0,) * len(shape))

    frames = rows * S5_CHUNK
    if tail <= 0:
        tail_rows = 0
    elif independent:
        tail_rows = rows
    else:
        tail_rows = min(rows, -(-tail // (S5_CHUNK * SUBLANES)) * SUBLANES)
    y, hf = pl.pallas_call(
        functools.partial(_s5_kernel, rows=rows, independent=independent, tail_rows=tail_rows),
        grid=(N_SLABS, nblk),
        in_specs=[pl.BlockSpec((None, frames, LANES), lambda q, b: (q, b, 0)),
                  pl.BlockSpec((None, pp, hrows, st), lambda q, b: (b, q, 0, 0)),
                  wspec(parts, 2, n_tap, n_tap), wspec(parts, 2 * n_tap, st), wspec(parts, 2 * n_tap, st),
                  wspec(2 * SUBLANES, st)],
        out_specs=[pl.BlockSpec((None, frames, LANES), lambda q, b: (q, b, 0)),
                   pl.BlockSpec((None, pp, hrows, st), lambda q, b: (b, q, 0, 0))],
        out_shape=[jax.ShapeDtypeStruct(u.shape, F32),
                   jax.ShapeDtypeStruct((nblk, N_PAIRS, hrows, st), F32)],
        scratch_shapes=[pltpu.VMEM((GROUPS_PER_SLAB, rows, n_tap), F32),
                        pltpu.VMEM((GROUPS_PER_SLAB, rows, n_tap), F32),
                        pltpu.VMEM((rows, st), F32), pltpu.VMEM((rows, st), F32)],
        compiler_params=_params("parallel", "parallel"),
        name="s5_core",
    )(u, h0p, m, wsp, wop, a16)
    hf = hf[0].transpose(1, 0, 2) if independent else hf[:, :, 0, :]
    hf_re = hf[:, :, :2 * SSM_STATE].reshape(bsz, N_GROUPS, SSM_STATE)
    hf_im = hf[:, :, 2 * SSM_STATE:].reshape(bsz, N_GROUPS, SSM_STATE)
    return y, hf_re, hf_im


def _conv_kernel(v_ref, left_ref, w_ref, b_ref, g_ref, beta_ref, y_ref, cn_ref, vbuf, shifted, *, tm):
    @pl.when(pl.program_id(1) == 0)
    def _():
        vbuf[0:CONV_HALO, :] = left_ref[...]

    vbuf[CONV_HALO:CONV_HALO + tm, :] = v_ref[...]
    first = CONV_HALO - (CONV_WIDTH - 1)
    span = tm + CONV_HALO - SUBLANES
    for r in range(1, SUBLANES):
        shifted[r - 1, 0:span, :] = vbuf[r:r + span, :]
    acc = jnp.zeros((tm, D_CONV), F32)
    for k in range(CONV_WIDTH):
        a, r = divmod(first + k, SUBLANES)
        src = vbuf if r == 0 else shifted.at[r - 1]
        acc = acc + w_ref[k:k + 1, :] * src[a * SUBLANES:a * SUBLANES + tm, :]
    y = acc + b_ref[...]
    mu = jnp.mean(y, axis=-1, keepdims=True)
    yc = y - mu
    var = jnp.mean(yc * yc, axis=-1, keepdims=True)
    yn = yc * lax.rsqrt(var + EPS) * g_ref[...] + beta_ref[...]
    y_ref[...] = yn * jax.nn.sigmoid(yn)
    cn_ref[...] = vbuf[tm + first:tm + CONV_HALO, :]
    vbuf[0:CONV_HALO, :] = vbuf[tm:tm + CONV_HALO, :]


def _conv_mixer(v, left, w, b, g, beta, layer, bsz, t_len, tm):
    left = jnp.pad(left.astype(F32), ((0, 0), (CONV_HALO - (CONV_WIDTH - 1), 0), (0, 0)))
    nt = t_len // tm
    vec = _layer_spec(layer, 1, D_CONV)
    return pl.pallas_call(
        functools.partial(_conv_kernel, tm=tm),
        grid=(bsz, nt),
        in_specs=[pl.BlockSpec((tm, D_CONV), lambda bi, j: (bi * nt + j, 0)),
                  pl.BlockSpec((None, CONV_HALO, D_CONV), lambda bi, j: (bi, 0, 0)),
                  _layer_spec(layer, CONV_WIDTH, D_CONV), vec, vec, vec],
        out_specs=[pl.BlockSpec((tm, D_CONV), lambda bi, j: (bi * nt + j, 0)),
                   pl.BlockSpec((None, CONV_WIDTH - 1, D_CONV), lambda bi, j: (bi, 0, 0))],
        out_shape=[jax.ShapeDtypeStruct((bsz * t_len, D_CONV), F32),
                   jax.ShapeDtypeStruct((bsz, CONV_WIDTH - 1, D_CONV), F32)],
        scratch_shapes=[pltpu.VMEM((tm + CONV_HALO, D_CONV), F32),
                        pltpu.VMEM((SUBLANES - 1, tm + CONV_HALO - SUBLANES, D_CONV), F32)],
        compiler_params=_params("parallel", "arbitrary"),
        name="conv_mixer",
    )(v, left, w, _vec(b), _vec(g), _vec(beta))


def _route(logits):
    lane_i = lax.broadcasted_iota(jnp.int32, logits.shape, 1)
    lane = lane_i.astype(F32)
    group_of_lane = (lane_i >> 3).astype(F32)
    neg = -jnp.inf
    far = float(LANES)
    is_g = (lane_i >= N_EXPERTS) & (lane_i < N_EXPERTS + N_EXPERT_GROUPS)
    gl = jnp.where(is_g, logits, neg)
    g_max = jnp.max(gl, axis=-1, keepdims=True)
    g_lane = jnp.min(jnp.where(gl == g_max, lane, far), axis=-1, keepdims=True)
    g_gate = 1.0 / jnp.sum(jnp.exp(gl - g_max), axis=-1, keepdims=True)
    g_idx = g_lane - float(N_EXPERTS)
    in_group = (lane_i < N_EXPERTS) & (group_of_lane == g_idx)
    el = jnp.where(in_group, logits, neg)
    v1 = jnp.max(el, axis=-1, keepdims=True)
    i1 = jnp.min(jnp.where(el == v1, lane, far), axis=-1, keepdims=True)
    el2 = jnp.where(lane == i1, neg, el)
    v2 = jnp.max(el2, axis=-1, keepdims=True)
    i2 = jnp.min(jnp.where(el2 == v2, lane, far), axis=-1, keepdims=True)
    e2 = jnp.exp(v2 - v1)
    w1 = g_gate / (1.0 + e2)
    w2 = g_gate * e2 / (1.0 + e2)
    return i1, i2, w1, w2


ROUTE_E1, ROUTE_E2, ROUTE_W1, ROUTE_W2, ROUTE_RANK1, ROUTE_RANK2 = range(6)


def _outproj_kernel(x_ref, ys_ref, u_ref, yb_ref, d_ref, wglu_ref, wout_ref, nf_ref, rw_ref, rb_ref, cnt0_ref,
                    x1_ref, hn_ref, route_ref, rt_ref, cnt_ref, wglu_bf, wout_bf, rw_bf, acc_glu, acc_mix,
                    *, tail_tiles):
    @pl.when(pl.program_id(0) == 0)
    def _():
        cnt_ref[...] = cnt0_ref[...]
        rw = rw_ref[...]
        rw_hi = rw.astype(BF16)
        rw_bf[:, 0:LANES] = rw_hi
        rw_bf[:, LANES:] = (rw - rw_hi.astype(F32)).astype(BF16)

    _cast_weight_once(wglu_ref, wglu_bf)
    _cast_weight_once(wout_ref, wout_bf)
    ys = jnp.concatenate([ys_ref[q] for q in range(N_SLABS)], axis=1)
    u = jnp.concatenate([u_ref[q] for q in range(N_SLABS)], axis=1)
    z = jax.nn.gelu(ys + d_ref[...] * u)
    ya = z * jax.nn.sigmoid(_pdot_tail(z, *_parts(wglu_bf), acc_glu, tail_tiles))
    mix = _pdot_tail(jnp.concatenate([ya, yb_ref[...]], axis=1), *_parts(wout_bf), acc_mix, tail_tiles)
    x1 = x_ref[...] + mix
    x1_ref[...] = x1
    hn = _rms(x1, nf_ref[...])
    _store_row_tiles(hn_ref, hn)
    h_hi = hn.astype(BF16)
    h_lo = (hn - h_hi.astype(F32)).astype(BF16)
    hw = jnp.dot(h_hi, rw_bf[...], preferred_element_type=F32)
    logits = (hw[:, :LANES] + hw[:, LANES:]
              + jnp.dot(h_lo, rw_bf[:, 0:LANES], preferred_element_type=F32) + rb_ref[...])
    i1, i2, w1, w2 = _route(logits)
    tm = logits.shape[0]
    lane_i = lax.broadcasted_iota(jnp.int32, logits.shape, 1)
    lane = lane_i.astype(F32)
    picked = jnp.where((lane == i1) | (lane == i2), 1.0, 0.0)
    earlier = (lax.broadcasted_iota(jnp.int32, (tm, tm), 0) > lax.broadcasted_iota(jnp.int32, (tm, tm), 1))
    prefix = _bdot(jnp.where(earlier, 1.0, 0.0), picked.astype(BF16)) + cnt_ref[...]
    rank1 = jnp.sum(jnp.where(lane == i1, prefix, 0.0), axis=-1, keepdims=True)
    rank2 = jnp.sum(jnp.where(lane == i2, prefix, 0.0), axis=-1, keepdims=True)
    cnt_ref[...] += jnp.sum(picked, axis=0, keepdims=True)
    rec = jnp.zeros_like(logits)
    for lane_id, val in ((ROUTE_E1, i1), (ROUTE_E2, i2), (ROUTE_W1, w1), (ROUTE_W2, w2),
                         (ROUTE_RANK1, rank1), (ROUTE_RANK2, rank2)):
        rec = jnp.where(lane_i == lane_id, val, rec)
    route_ref[...] = rec
    rt_ref[...] = rec.T[0:SUBLANES, :]


def _outproj(x, ys, u, yb, d, wglu, wout, nf, rw, rb, counts0, layer, tm, t_len, tail):
    n = x.shape[0]
    precise = tail > 0
    return pl.pallas_call(
        functools.partial(_outproj_kernel, tail_tiles=_stream_tail_tiles(t_len, tm, tail)),
        grid=(n // tm,),
        in_specs=[_row_spec(tm, D_MODEL), _slab_spec(tm), _slab_spec(tm), _row_spec(tm, D_CONV),
                  _layer_spec(layer, 1, D_SSM), _layer_spec(layer, D_SSM, D_SSM),
                  _layer_spec(layer, D_MODEL, D_MODEL), _layer_spec(layer, 1, D_MODEL),
                  _const_spec((D_MODEL, LANES)), _const_spec((1, LANES)), _const_spec((1, LANES))],
        out_specs=[_row_spec(tm, D_MODEL), _row_tile_spec(tm), _row_spec(tm, LANES),
                   pl.BlockSpec((SUBLANES, tm), lambda i: (0, i)), _const_spec((1, LANES))],
        out_shape=[jax.ShapeDtypeStruct((n, D_MODEL), F32), jax.ShapeDtypeStruct(_tiled_rows(n), F32),
                   jax.ShapeDtypeStruct((n, LANES), F32), jax.ShapeDtypeStruct((SUBLANES, n), F32),
                   jax.ShapeDtypeStruct((1, LANES), F32)],
        scratch_shapes=[_weight_scratch(D_SSM, D_SSM, precise), _weight_scratch(D_MODEL, D_MODEL, precise),
                        pltpu.VMEM((D_MODEL, 2 * LANES), BF16),
                        _tail_acc(tm, D_SSM, precise), _tail_acc(tm, D_MODEL, precise)],
        compiler_params=_params("arbitrary"),
        name="outproj_router",
    )(x, ys, u, yb, _vec(d), wglu, wout, _vec(nf), rw, rb, counts0)


PLAN_TILE_LANES = 2 * LANES
PLAN_EXPERT, PLAN_ROWS, PLAN_USED = range(3)


def _moe_plan_kernel(rt_ref, cnt_ref, cnt_first_ref, slot_ref, tile_ref, *, tms):
    cnt = cnt_ref[...]
    padded = jnp.ceil(cnt * (1.0 / tms)) * float(tms)
    r = lax.broadcasted_iota(jnp.int32, (LANES, LANES), 0)
    c = lax.broadcasted_iota(jnp.int32, (LANES, LANES), 1)
    ends = jnp.dot(padded, jnp.where(r <= c, 1.0, 0.0), precision=lax.Precision.HIGHEST,
                   preferred_element_type=F32)
    starts = ends - padded
    rt = rt_ref[...]
    e1, e2 = rt[ROUTE_E1:ROUTE_E1 + 1], rt[ROUTE_E2:ROUTE_E2 + 1]
    s1, s2 = rt[ROUTE_RANK1:ROUTE_RANK1 + 1], rt[ROUTE_RANK2:ROUTE_RANK2 + 1]
    tile = lax.broadcasted_iota(jnp.int32, (1, PLAN_TILE_LANES), 1).astype(F32)
    used = ends[:, N_EXPERTS - 1:N_EXPERTS] * (1.0 / tms)
    pos = jnp.minimum(tile, used - 1.0) * float(tms)
    t_exp = jnp.zeros_like(tile)
    t_fill = jnp.zeros_like(tile)
    for e in range(N_EXPERTS):
        st, en = starts[:, e:e + 1], ends[:, e:e + 1]
        s1 = s1 + jnp.where(e1 == float(e), st, 0.0)
        s2 = s2 + jnp.where(e2 == float(e), st, 0.0)
        mine = (pos >= st) & (pos < en)
        t_exp = t_exp + jnp.where(mine, float(e), 0.0)
        t_fill = t_fill + jnp.where(mine, st + cnt_first_ref[:, e:e + 1], 0.0)
    t_rows = jnp.where(tile < used, jnp.clip(t_fill - pos, 0.0, float(tms)), 0.0)
    slot_ref[...] = jnp.concatenate([s1, s2], axis=0).astype(jnp.int32)
    tile_ref[...] = jnp.concatenate(
        [t_exp, t_rows, jnp.broadcast_to(used, tile.shape), jnp.zeros((SUBLANES - 3, PLAN_TILE_LANES), F32)],
        axis=0).astype(jnp.int32)


def _moe_plan(route_t, counts, counts_first, tms, n_tiles):
    n = route_t.shape[1]
    assert n_tiles <= PLAN_TILE_LANES
    slots, tiles = pl.pallas_call(
        functools.partial(_moe_plan_kernel, tms=tms),
        out_shape=[jax.ShapeDtypeStruct((2, n), jnp.int32),
                   jax.ShapeDtypeStruct((SUBLANES, PLAN_TILE_LANES), jnp.int32)],
        compiler_params=pltpu.CompilerParams(vmem_limit_bytes=VMEM_LIMIT),
        name="moe_plan",
    )(route_t, counts, counts_first)
    return slots, tiles[PLAN_EXPERT, :n_tiles], tiles[PLAN_ROWS, :n_tiles], tiles[PLAN_USED, :1]


DMA_UNROLL = 8


HN_BUFFERS = 3


def _dispatch_kernel(tr_ref, slot_ref, hn_ref, *rest, tm, tms, n_tiles, n_steps, fresh):
    xs_ref, zbuf, hbuf, in_sem, out_sem, zsem = rest if fresh else rest[1:]
    i = pl.program_id(0)

    def fetch(t):
        b = lax.rem(t, HN_BUFFERS)
        first = pl.multiple_of(t * (tm * ROW_SUB), tm * ROW_SUB)
        return pltpu.make_async_copy(hn_ref.at[pl.ds(first, tm * ROW_SUB)], hbuf.at[b], in_sem.at[b])

    def drain_scatter(t):
        b = lax.rem(t, HN_BUFFERS)
        for k in range(2):
            pltpu.make_async_copy(hbuf.at[b], xs_ref.at[pl.ds(0, tm * ROW_SUB)], out_sem.at[b]).wait()

    @pl.when(i == 0)
    def _():
        fetch(i).start()

    @pl.when(jnp.logical_and(i == 0, fresh))
    def _():
        zbuf[...] = jnp.zeros_like(zbuf)

        def fill(t, carry):
            @pl.when(tr_ref[t] < tms)
            def _():
                first = pl.multiple_of(t * (tms * ROW_SUB), tms * ROW_SUB)
                pltpu.make_async_copy(zbuf, xs_ref.at[pl.ds(first, tms * ROW_SUB)], zsem).start()
            return carry

        def drain(t, carry):
            @pl.when(tr_ref[t] < tms)
            def _():
                pltpu.make_async_copy(zbuf, xs_ref.at[pl.ds(0, tms * ROW_SUB)], zsem).wait()
            return carry

        lax.fori_loop(0, n_tiles, fill, 0)
        lax.fori_loop(0, n_tiles, drain, 0)

    @pl.when(i + 1 < n_steps)
    def _():
        fetch(i + 1).start()

    fetch(i).wait()
    b = lax.rem(i, HN_BUFFERS)
    rows = hbuf.at[b]

    def issue(r, carry):
        for k in range(2):
            pltpu.make_async_copy(_one_row(rows, r), _one_row(xs_ref, slot_ref[k, r]),
                                  out_sem.at[b]).start(priority=k)
        return carry

    lax.fori_loop(0, tm, issue, 0, unroll=DMA_UNROLL)

    @pl.when(i >= 1)
    def _():
        drain_scatter(i - 1)

    @pl.when(i == n_steps - 1)
    def _():
        drain_scatter(i)


def _dispatch(hn, slots, tile_rows, tms, tm, into=None):
    n = hn.shape[0] // ROW_SUB
    n_tiles = tile_rows.shape[0]
    fresh = into is None
    any_spec = pl.BlockSpec(memory_space=pl.ANY)
    return pl.pallas_call(
        functools.partial(_dispatch_kernel, tm=tm, tms=tms, n_tiles=n_tiles, n_steps=n // tm, fresh=fresh),
        grid_spec=pltpu.PrefetchScalarGridSpec(
            num_scalar_prefetch=1, grid=(n // tm,),
            in_specs=[pl.BlockSpec((2, tm), lambda i, tr: (0, i), memory_space=pltpu.SMEM), any_spec]
                     + ([] if fresh else [any_spec]),
            out_specs=any_spec,
            scratch_shapes=[pltpu.VMEM(_tiled_rows(tms), F32), pltpu.VMEM((HN_BUFFERS,) + _tiled_rows(tm), F32),
                            pltpu.SemaphoreType.DMA((HN_BUFFERS,)), pltpu.SemaphoreType.DMA((HN_BUFFERS,)),
                            pltpu.SemaphoreType.DMA]),
        out_shape=jax.ShapeDtypeStruct(_tiled_rows(n_tiles * tms), F32),
        input_output_aliases={} if fresh else {3: 0},
        compiler_params=_params("arbitrary"),
        name="moe_dispatch",
    )(tile_rows, slots, hn, *([] if fresh else [into]))


def _moe_kernel(te_ref, nu_ref, x_ref, wg_ref, wu_ref, wd_ref, y_ref, wg_bf, wu_bf, wd_bf):
    i = pl.program_id(0)
    in_use = i < nu_ref[0]
    new_expert = (i == 0) | (te_ref[i] != te_ref[jnp.maximum(i - 1, 0)])

    @pl.when(in_use & new_expert)
    def _():
        wg_bf[...] = wg_ref[...].astype(BF16)
        wu_bf[...] = wu_ref[...].astype(BF16)
        wd_bf[...] = wd_ref[...].astype(BF16)

    @pl.when(in_use)
    def _():
        h = _load_row_tiles(x_ref).astype(BF16)
        hg = jnp.dot(h, wg_bf[...], preferred_element_type=F32)
        hu = jnp.dot(h, wu_bf[...], preferred_element_type=F32)
        _store_row_tiles(y_ref, _bdot(hg * jax.nn.sigmoid(hg) * hu, wd_bf[...]))

    @pl.when(jnp.logical_not(in_use))
    def _():
        y_ref[...] = jnp.zeros_like(y_ref)


def _moe(xs, tile_expert, n_used, wg, wu, wd, layer, tms):
    n_slots = xs.shape[0] // ROW_SUB
    rows = pl.BlockSpec(_tiled_rows(tms), lambda i, te, nu: (jnp.minimum(i, nu[0] - 1), 0))
    out_rows = pl.BlockSpec(_tiled_rows(tms), lambda i, te, nu: (i, 0))

    def wspec(a, b):
        return pl.BlockSpec((None, None, a, b), lambda i, te, nu: (layer, te[i], 0, 0))

    return pl.pallas_call(
        _moe_kernel,
        grid_spec=pltpu.PrefetchScalarGridSpec(
            num_scalar_prefetch=2, grid=(n_slots // tms,),
            in_specs=[rows, wspec(D_MODEL, D_EXPERT), wspec(D_MODEL, D_EXPERT), wspec(D_EXPERT, D_MODEL)],
            out_specs=out_rows,
            scratch_shapes=[pltpu.VMEM((D_MODEL, D_EXPERT), BF16), pltpu.VMEM((D_MODEL, D_EXPERT), BF16),
                            pltpu.VMEM((D_EXPERT, D_MODEL), BF16)]),
        out_shape=jax.ShapeDtypeStruct(xs.shape, F32),
        compiler_params=_params("arbitrary"),
        name="moe",
    )(tile_expert, n_used, xs, wg, wu, wd)


def _ple_kernel(slot_ref, next_slot_ref, x_ref, route_ref, p_ref, np_ref, wple_ref, wgate_ref, nfin_ref,
                ys_ref, o_ref, ybuf, sem, wple_bf, wgate_bf, *, tm, n_steps, final):
    i = pl.program_id(0)
    _cast_weight_once(wple_ref, wple_bf)
    _cast_weight_once(wgate_ref, wgate_bf)

    def gather(slots, b):
        def issue(r, carry):
            for k in range(2):
                pltpu.make_async_copy(_one_row(ys_ref, slots[k, r]), _one_row(ybuf.at[b, k], r),
                                      sem.at[b]).start(priority=k)
            return carry

        lax.fori_loop(0, tm, issue, 0, unroll=DMA_UNROLL)

    @pl.when(i == 0)
    def _():
        gather(slot_ref, 0)

    @pl.when(i + 1 < n_steps)
    def _():
        gather(next_slot_ref, lax.rem(i + 1, 2))

    pe = _pdot(p_ref[...], *_parts(wple_bf))
    b = lax.rem(i, 2)
    for k in range(2):
        pltpu.make_async_copy(ys_ref.at[pl.ds(0, tm * ROW_SUB)], ybuf.at[b, k], sem.at[b]).wait()
    route = route_ref[...]
    x = (x_ref[...] + route[:, ROUTE_W1:ROUTE_W1 + 1] * _load_row_tiles(ybuf.at[b, 0])
         + route[:, ROUTE_W2:ROUTE_W2 + 1] * _load_row_tiles(ybuf.at[b, 1]))
    gate = jax.nn.sigmoid(_pdot(_rms(x, np_ref[...]), *_parts(wgate_bf)))
    out = x + pe * gate
    if final:
        out = _rms(out, nfin_ref[...])
    o_ref[...] = out


def _ple(x, route, slots, ys, p, npl, wple, wgate, nfin, layer, tm, final):
    n = x.shape[0]
    n_steps = n // tm
    return pl.pallas_call(
        functools.partial(_ple_kernel, tm=tm, n_steps=n_steps, final=final),
        grid=(n_steps,),
        in_specs=[pl.BlockSpec((2, tm), lambda i: (0, i), memory_space=pltpu.SMEM),
                  pl.BlockSpec((2, tm), lambda i: (0, jnp.minimum(i + 1, n_steps - 1)),
                               memory_space=pltpu.SMEM),
                  _row_spec(tm, D_MODEL), _row_spec(tm, LANES),
                  pl.BlockSpec((None, tm, D_PLE), lambda i: (layer, i, 0)),
                  _layer_spec(layer, 1, D_MODEL), _layer_spec(layer, D_PLE, D_MODEL),
                  _layer_spec(layer, D_MODEL, D_MODEL), _const_spec((1, D_MODEL)),
                  pl.BlockSpec(memory_space=pl.ANY)],
        out_specs=_row_spec(tm, D_MODEL),
        out_shape=jax.ShapeDtypeStruct((n, D_MODEL), F32),
        scratch_shapes=[pltpu.VMEM((2, 2) + _tiled_rows(tm), F32), pltpu.SemaphoreType.DMA((2,)),
                        _weight_scratch(D_PLE, D_MODEL, False), _weight_scratch(D_MODEL, D_MODEL, False)],
        compiler_params=_params("arbitrary"),
        name="combine_ple",
    )(slots, slots, x, route, p, _vec(npl), wple, wgate, nfin.reshape(1, D_MODEL), ys)


def _mixers(x, h0_re, h0_im, conv_left, w, s5, router_w, router_b, counts0, layer, bsz, t_len, tm, tm_conv,
            tail):
    u, v = _inproj(x, w["norm_mix"], w["w_in"], layer, min(2 * tm, x.shape[0]), t_len, tail)
    ys, hf_re, hf_im = _s5_mixer(u, h0_re, h0_im, s5, bsz, t_len, tail)
    yb, conv_new = _conv_mixer(v, conv_left, w["conv_w"], w["conv_b"], w["conv_ln_g"], w["conv_ln_b"],
                               layer, bsz, t_len, tm_conv)
    x1, hn, route, route_t, counts = _outproj(x, ys, u, yb, w["ssm_d"], w["w_ssm_glu"], w["w_out"],
                                              w["norm_ffn"], router_w, router_b, counts0, layer, tm, t_len, tail)
    return dict(x1=x1, hn=hn, route=route, route_t=route_t, counts=counts, state=(hf_re, hf_im, conv_new))


def _moe_and_ple(sets, ps, tms_rows, w, layer, tms, final):
    sizes = [s["x1"].shape[0] for s in sets]
    n_tiles = 2 * sum(sizes) // tms + N_EXPERTS
    route_t = jnp.concatenate([s["route_t"] for s in sets], axis=1)
    slots, tile_expert, tile_rows, n_used = _moe_plan(route_t, sets[-1]["counts"], sets[0]["counts"], tms,
                                                     n_tiles)
    starts = [sum(sizes[:j]) for j in range(len(sets))]
    set_slots = [slots[:, a:a + n] for a, n in zip(starts, sizes)]
    xsort = None
    for s, sl, tm in zip(sets, set_slots, tms_rows):
        xsort = _dispatch(s["hn"], sl, tile_rows, tms, tm, into=xsort)
    ysort = _moe(xsort, tile_expert, n_used, w["expert_w_gate"], w["expert_w_up"], w["expert_w_down"],
                 layer, tms)
    return [_ple(s["x1"], s["route"], sl, ysort, p, w["norm_ple"], w["ple_w"], w["ple_gate_w"],
                 w["norm_final"], layer, tm, final)
            for s, sl, p, tm in zip(sets, set_slots, ps, tms_rows)]


def kernel(x_prompt, x_sample, p_prompt, p_sample, state_ssm_re, state_ssm_im, cache_conv, norm_mix, w_in, ssm_a_re, ssm_a_im, ssm_b_re, ssm_b_im, ssm_c_re, ssm_c_im, ssm_d, ssm_log_dt, w_ssm_glu, conv_w, conv_b, conv_ln_g, conv_ln_b, w_out, norm_ffn, router_group_w, router_group_b, router_expert_w, router_expert_b, expert_w_gate, expert_w_up, expert_w_down, norm_ple, ple_w, ple_gate_w, norm_final):
    depth = w_in.shape[0]
    bp, tp, _ = x_prompt.shape
    bs, ts, _ = x_sample.shape
    xp = x_prompt.reshape(bp * tp, D_MODEL)
    xs = x_sample.reshape(bs * ts, D_MODEL)
    pp = p_prompt.reshape(depth, bp * tp, D_PLE)
    ps = p_sample.reshape(depth, bs * ts, D_PLE)
    zero_state = jnp.zeros((bp, N_GROUPS, SSM_STATE), F32)
    zero_conv = jnp.zeros((bp, CONV_WIDTH - 1, D_CONV), F32)
    pad_lanes = LANES - N_EXPERTS - N_EXPERT_GROUPS
    w = {"norm_mix": norm_mix, "w_in": w_in, "ssm_d": ssm_d, "w_ssm_glu": w_ssm_glu, "conv_w": conv_w,
         "conv_b": conv_b, "conv_ln_g": conv_ln_g, "conv_ln_b": conv_ln_b, "w_out": w_out,
         "norm_ffn": norm_ffn, "expert_w_gate": expert_w_gate, "expert_w_up": expert_w_up,
         "expert_w_down": expert_w_down, "norm_ple": norm_ple, "ple_w": ple_w, "ple_gate_w": ple_gate_w,
         "norm_final": norm_final}
    no_picks = jnp.zeros((1, LANES), F32)
    outs = {k: [] for k in ("pr_re", "pr_im", "pr_conv", "sm_re", "sm_im", "sm_conv")}
    for i in range(depth):
        s5 = _s5_prep(ssm_a_re[i], ssm_a_im[i], ssm_log_dt[i], ssm_b_re[i], ssm_b_im[i], ssm_c_re[i],
                      ssm_c_im[i])
        router_w = jnp.pad(jnp.concatenate([router_expert_w[i], router_group_w[i]], axis=1),
                           ((0, 0), (0, pad_lanes)))
        router_b = jnp.pad(jnp.concatenate([router_expert_b[i], router_group_b[i]]),
                           (0, pad_lanes)).reshape(1, LANES)
        final = i == depth - 1
        tail = 0 if final else PRECISE_TAIL
        mp = _mixers(xp, zero_state, zero_state, zero_conv, w, s5, router_w, router_b, no_picks, i,
                     bp, tp, tm=TOKEN_TILE, tm_conv=TOKEN_TILE, tail=tail)
        ms = _mixers(xs, state_ssm_re[i], state_ssm_im[i], cache_conv[i], w, s5, router_w, router_b,
                     mp["counts"], i, bs, ts, tm=bs * ts, tm_conv=ts, tail=tail)
        xp, xs = _moe_and_ple([mp, ms], [pp, ps], [TOKEN_TILE, bs * ts], w, i, SLOT_TILE, final)
        for key, val in zip(("pr_re", "pr_im", "pr_conv"), mp["state"]):
            outs[key].append(val)
        for key, val in zip(("sm_re", "sm_im", "sm_conv"), ms["state"]):
            outs[key].append(val)
    return (xp.reshape(bp, tp, D_MODEL), xs.reshape(bs, ts, D_MODEL),
            jnp.stack(outs["pr_re"]), jnp.stack(outs["pr_im"]), jnp.stack(outs["pr_conv"]),
            jnp.stack(outs["sm_re"]), jnp.stack(outs["sm_im"]), jnp.stack(outs["sm_conv"]))
```

```python
import functools

import jax
import jax.numpy as jnp
from jax import lax
from jax.experimental import pallas as pl
from jax.experimental.pallas import tpu as pltpu

F32 = jnp.float32
BF16 = jnp.bfloat16

D_MODEL = 1024
D_SSM = 512
SSM_GROUP = 16
N_GROUPS = D_SSM // SSM_GROUP
N_PAIRS = N_GROUPS // 2
SSM_STATE = 64
D_CONV = 512
CONV_WIDTH = 31
CONV_HALO = 32
N_EXPERT_GROUPS = 4
EXPERTS_PER_GROUP = 8
N_EXPERTS = 32
D_EXPERT = 256
D_PLE = 256
EPS = 1e-6
S5_CHUNK = 16
LANES = 128
SUBLANES = 8
N_SLABS = D_SSM // LANES
GROUPS_PER_SLAB = LANES // SSM_GROUP
PAIRS_PER_SLAB = GROUPS_PER_SLAB // 2
VMEM_LIMIT = 56 * 1024 * 1024
TOKEN_TILE = 512
SLOT_TILE = 512
PRECISE_TAIL = 1024


def _params(*sem):
    return pltpu.CompilerParams(dimension_semantics=sem, vmem_limit_bytes=VMEM_LIMIT)


def _rms(x, g):
    return x * lax.rsqrt(jnp.mean(x * x, axis=-1, keepdims=True) + EPS) * g


def _bdot(a, b):
    return jnp.dot(a.astype(BF16), b, preferred_element_type=F32)


def _split_bf16(a):
    hi = a.astype(BF16)
    return hi, (a - hi.astype(F32)).astype(BF16)


def _pdot(a, w_hi, w_lo=None, dims=(((1,), (0,)), ((), ()))):
    def mm(x, w):
        return lax.dot_general(x, w, dims, preferred_element_type=F32)

    if w_lo is None:
        return mm(a.astype(BF16), w_hi)
    a_hi, a_lo = _split_bf16(a)
    return mm(a_hi, w_hi) + (mm(a_hi, w_lo) + mm(a_lo, w_hi))


def _pdot_lo_terms(a, w_hi, w_lo, dims=(((1,), (0,)), ((), ()))):
    a_hi, a_lo = _split_bf16(a)
    return (lax.dot_general(a_hi, w_lo, dims, preferred_element_type=F32)
            + lax.dot_general(a_lo, w_hi, dims, preferred_element_type=F32))


def _stream_tail_tiles(t_len, tm, tail):
    tiles = max(t_len // tm, 1)
    return tiles, (max(t_len - tail, 0) // tm if tm < t_len else 0)


def _pdot_tail(a, w_hi, w_lo, acc_ref, tail_tiles):
    if w_lo is None:
        return _pdot(a, w_hi)
    tiles, first = tail_tiles
    if first == 0:
        return _pdot(a, w_hi, w_lo)
    acc_ref[...] = _pdot(a, w_hi)

    @pl.when(lax.rem(pl.program_id(0), tiles) >= first)
    def _():
        acc_ref[...] += _pdot_lo_terms(a, w_hi, w_lo)

    return acc_ref[...]


def _parts(wbf_ref, rows=slice(None)):
    return wbf_ref[0, rows, :], (wbf_ref[1, rows, :] if wbf_ref.shape[0] == 2 else None)


def _row_spec(tm, width):
    return pl.BlockSpec((tm, width), lambda i: (i, 0))


ROW_SUB = D_MODEL // LANES


def _tiled_rows(n):
    return (n * ROW_SUB, LANES)


def _row_tile_spec(tm):
    return pl.BlockSpec(_tiled_rows(tm), lambda i, *_: (i, 0))


def _one_row(ref, r):
    return ref.at[pl.ds(pl.multiple_of(r * ROW_SUB, ROW_SUB), ROW_SUB)]


def _store_row_tiles(ref, rows):
    n = rows.shape[0]
    for s in range(ROW_SUB):
        ref[pl.ds(s, n, stride=ROW_SUB), :] = rows[:, s * LANES:(s + 1) * LANES]


def _load_row_tiles(ref):
    n = ref.shape[0] // ROW_SUB
    return jnp.concatenate([ref[pl.ds(s, n, stride=ROW_SUB), :] for s in range(ROW_SUB)], axis=1)


def _slab_spec(tm):
    return pl.BlockSpec((N_SLABS, tm, LANES), lambda i: (0, i, 0))


def _const_spec(shape):
    return pl.BlockSpec(shape, lambda i: (0,) * len(shape))


def _layer_spec(layer, *shape):
    return pl.BlockSpec((None,) + shape, lambda *_: (layer,) + (0,) * len(shape))


def _vec(stacked):
    return stacked.reshape(stacked.shape[0], 1, stacked.shape[1])


def _cast_weight_once(w_ref, wbf_ref):
    @pl.when(pl.program_id(0) == 0)
    def _():
        w = w_ref[...]
        hi = w.astype(BF16)
        wbf_ref[0] = hi
        if wbf_ref.shape[0] == 2:
            wbf_ref[1] = (w - hi.astype(F32)).astype(BF16)


def _weight_scratch(k, n, precise):
    return pltpu.VMEM((2 if precise else 1, k, n), BF16)


def _tail_acc(rows, n, precise):
    return pltpu.VMEM((rows, n) if precise else (SUBLANES, LANES), F32)


def _inproj_kernel(x_ref, g_ref, w_ref, u_ref, v_ref, wbf, acc, *, tail_tiles):
    _cast_weight_once(w_ref, wbf)
    hn = _rms(x_ref[...], g_ref[...])
    proj = _pdot_tail(hn, *_parts(wbf), acc, tail_tiles)
    for q in range(N_SLABS):
        u_ref[q] = proj[:, q * LANES:(q + 1) * LANES]
    v_ref[...] = proj[:, D_SSM:D_SSM + D_CONV] * jax.nn.sigmoid(proj[:, D_SSM + D_CONV:])


def _inproj(x, g, w, layer, tm, t_len, tail):
    n = x.shape[0]
    d_in = w.shape[2]
    precise = tail > 0
    return pl.pallas_call(
        functools.partial(_inproj_kernel, tail_tiles=_stream_tail_tiles(t_len, tm, tail)),
        grid=(n // tm,),
        in_specs=[_row_spec(tm, D_MODEL), _layer_spec(layer, 1, D_MODEL), _layer_spec(layer, D_MODEL, d_in)],
        out_specs=[_slab_spec(tm), _row_spec(tm, D_CONV)],
        out_shape=[jax.ShapeDtypeStruct((N_SLABS, n, LANES), F32), jax.ShapeDtypeStruct((n, D_CONV), F32)],
        scratch_shapes=[_weight_scratch(D_MODEL, d_in, precise), _tail_acc(tm, d_in, precise)],
        compiler_params=_params("arbitrary"),
        name="inproj",
    )(x, _vec(g), w)


def _s5_prep_kernel(ar_ref, ai_ref, ldt_ref, bre_ref, bim_ref, cre_ref, cim_ref,
                    m_ref, ws_ref, wot_ref, atab_ref, wt_re, wt_im, br_re, br_im):
    n_tap = S5_CHUNK * SSM_GROUP
    st = 4 * SSM_STATE
    nt = (((1,), (1,)), ((), ()))
    hi = lax.Precision.HIGHEST
    lane = lax.broadcasted_iota(jnp.int32, (SSM_GROUP, n_tap), 1)
    ws_rows, wot_rows, atab = [], [], jnp.zeros((2 * SUBLANES, st), F32)
    for gi in range(2):
        ar, ai = ar_ref[gi], ai_ref[gi]
        dt = jnp.exp(ldt_ref[gi])
        k = lax.broadcasted_iota(jnp.int32, (S5_CHUNK + SUBLANES, SSM_STATE), 0).astype(F32)
        mag = jnp.exp(k * (dt * ar))
        ang = k * (dt * ai)
        p_re, p_im = mag * jnp.cos(ang), mag * jnp.sin(ang)
        inv = 1.0 / (ar * ar + ai * ai)
        ab_re, ab_im = p_re[1:2], p_im[1:2]
        ia_re, ia_im = ar * inv, -ai * inv
        coef_re = (ab_re - 1.0) * ia_re - ab_im * ia_im
        coef_im = (ab_re - 1.0) * ia_im + ab_im * ia_re
        bre, bim = bre_ref[gi], bim_ref[gi]
        bb_re = coef_re * bre - coef_im * bim
        bb_im = coef_re * bim + coef_im * bre
        cre, cim = cre_ref[gi], cim_ref[gi]
        for kk in range(S5_CHUNK + 1):
            pr, pi = p_re[kk:kk + 1], p_im[kk:kk + 1]
            rows = slice(kk * SSM_GROUP, (kk + 1) * SSM_GROUP)
            wt_re[rows, :] = pr * cre - pi * cim
            wt_im[rows, :] = -pi * cre - pr * cim
            if kk < S5_CHUNK:
                back = slice((S5_CHUNK - 1 - kk) * SSM_GROUP, (S5_CHUNK - kk) * SSM_GROUP)
                br_re[back, :] = pr * bb_re - pi * bb_im
                br_im[back, :] = pi * bb_re + pr * bb_im
        kcat = (lax.dot_general(bb_re, wt_re[0:n_tap, :], nt, precision=hi, preferred_element_type=F32)
                + lax.dot_general(bb_im, wt_im[0:n_tap, :], nt, precision=hi, preferred_element_type=F32))
        for s in range(S5_CHUNK):
            shifted = kcat if s == 0 else pltpu.roll(kcat, s * SSM_GROUP, 1)
            rows = slice(s * SSM_GROUP, (s + 1) * SSM_GROUP)
            m_ref[0, gi, rows, :], m_ref[1, gi, rows, :] = _split_bf16(
                jnp.where(lane >= s * SSM_GROUP, shifted, 0.0))
        def place(v_re, v_im):
            zero = jnp.zeros_like(v_re)
            parts = [v_re, zero, v_im, zero] if gi == 0 else [zero, v_re, zero, v_im]
            return jnp.concatenate(parts, axis=1)

        ws_rows.append(place(br_re[...], br_im[...]))
        wot_rows.append(place(wt_re[SSM_GROUP:, :], wt_im[SSM_GROUP:, :]))
        kc = float(S5_CHUNK) * lax.broadcasted_iota(jnp.int32, (2 * SUBLANES, SSM_STATE), 0).astype(F32)
        magc = jnp.exp(kc * (dt * ar))
        angc = kc * (dt * ai)
        atab = atab + place(magc * jnp.cos(angc), magc * jnp.sin(angc))
    ws_ref[0], ws_ref[1] = _split_bf16(jnp.concatenate(ws_rows, axis=0))
    wot_ref[0], wot_ref[1] = _split_bf16(jnp.concatenate(wot_rows, axis=0))
    atab_ref[...] = atab


def _s5_prep(a_re, a_im, log_dt, b_re, b_im, c_re, c_im):
    p, n, c = N_PAIRS, SSM_STATE, SSM_GROUP
    n_tap = S5_CHUNK * c
    st = 4 * n

    def pspec(*shape):
        return pl.BlockSpec((None,) + shape, lambda i: (i,) + (0,) * len(shape))

    def pairs(a, *shape):
        return a.reshape((p, 2) + shape)

    return pl.pallas_call(
        _s5_prep_kernel,
        grid=(p,),
        in_specs=[pspec(2, 1, n), pspec(2, 1, n), pspec(2, 1, 1), pspec(2, c, n), pspec(2, c, n),
                  pspec(2, c, n), pspec(2, c, n)],
        out_specs=[pspec(2, 2, n_tap, n_tap), pspec(2, 2 * n_tap, st), pspec(2, 2 * n_tap, st),
                   pspec(2 * SUBLANES, st)],
        out_shape=[jax.ShapeDtypeStruct((p, 2, 2, n_tap, n_tap), BF16),
                   jax.ShapeDtypeStruct((p, 2, 2 * n_tap, st), BF16),
                   jax.ShapeDtypeStruct((p, 2, 2 * n_tap, st), BF16),
                   jax.ShapeDtypeStruct((p, 2 * SUBLANES, st), F32)],
        scratch_shapes=[pltpu.VMEM((n_tap + c, n), F32), pltpu.VMEM((n_tap + c, n), F32),
                        pltpu.VMEM((n_tap, n), F32), pltpu.VMEM((n_tap, n), F32)],
        compiler_params=_params("parallel"),
        name="s5_prep",
    )(pairs(a_re, 1, n), pairs(a_im, 1, n), pairs(log_dt, 1, 1),
      pairs(jnp.swapaxes(b_re, 1, 2), c, n), pairs(jnp.swapaxes(b_im, 1, 2), c, n),
      pairs(c_re, c, n), pairs(c_im, c, n))


def _block_transpose8(vs):
    lane = lax.broadcasted_iota(jnp.int32, vs[0].shape, 1)
    blk = lane >> 4
    for d in (4, 2, 1):
        keep = (blk & d) == 0
        new = list(vs)
        for i in range(GROUPS_PER_SLAB):
            if i & d == 0:
                a, b = vs[i], vs[i + d]
                new[i] = jnp.where(keep, a, pltpu.roll(b, d * SSM_GROUP, 1))
                new[i + d] = jnp.where(keep, pltpu.roll(a, LANES - d * SSM_GROUP, 1), b)
        vs = new
    return vs


def _cmul(ar, ai, xr, xi):
    return ar * xr - ai * xi, ar * xi + ai * xr


def _s5_kernel(u_ref, h0_ref, m_ref, ws_ref, wo_ref, a_ref, y_ref, hf_ref, x_scr, yg_scr, s_scr, hp_scr,
               *, rows, independent, tail_rows):
    half = 2 * SSM_STATE
    tail = pl.ds(rows - tail_rows, tail_rows)
    rt = min(rows, 16 * SUBLANES)
    half_chunk = S5_CHUNK // 2

    def gather_tile(t, carry):
        r0 = pl.multiple_of(t * rt, rt)
        for hf in range(2):
            vs = [u_ref[pl.ds(r0 * S5_CHUNK + hf * half_chunk + i, rt, stride=S5_CHUNK), :]
                  for i in range(half_chunk)]
            outs = _block_transpose8(vs)
            for g in range(GROUPS_PER_SLAB):
                x_scr[g, pl.ds(r0, rt), hf * LANES:(hf + 1) * LANES] = outs[g]
        return carry

    lax.fori_loop(0, rows // rt, gather_tile, 0)

    row = lax.broadcasted_iota(jnp.int32, (SUBLANES, half), 0)
    n_tap = S5_CHUNK * SSM_GROUP
    for pi in range(PAIRS_PER_SLAB):
        def part(w_ref, *idx):
            return w_ref[(pi, 0) + idx], w_ref[(pi, 1) + idx]

        x0 = x_scr[2 * pi]
        x1 = x_scr[2 * pi + 1]
        x01 = jnp.concatenate([x0, x1], axis=1)
        s_scr[...] = _pdot(x01, ws_ref[pi, 0])
        if tail_rows:
            s_scr[tail, :] += _pdot_lo_terms(x01[rows - tail_rows:], *part(ws_ref))
        ap = a_ref[pi]
        h0 = h0_ref[pi]
        if independent:
            hp_scr[...] = h0
            s = s_scr[...]
            n_re, n_im = _cmul(ap[1:2, :half], ap[1:2, half:], h0[:, :half], h0[:, half:])
            hf_ref[pi] = jnp.concatenate([n_re + s[:, :half], n_im + s[:, half:]], axis=1)
        else:
            pw_re, pw_im = ap[0:SUBLANES, :half], ap[0:SUBLANES, half:]

            def scan_tile(t, carry):
                h_re, h_im = carry
                r0 = pl.multiple_of(t * SUBLANES, SUBLANES)
                s = s_scr[pl.ds(r0, SUBLANES), :]
                t_re, t_im = s[:, :half], s[:, half:]
                for d in (1, 2, 4):
                    sh_re = jnp.where(row >= d, pltpu.roll(t_re, d, 0), 0.0)
                    sh_im = jnp.where(row >= d, pltpu.roll(t_im, d, 0), 0.0)
                    m_re, m_im = _cmul(ap[d:d + 1, :half], ap[d:d + 1, half:], sh_re, sh_im)
                    t_re, t_im = t_re + m_re, t_im + m_im
                e_re = jnp.where(row >= 1, pltpu.roll(t_re, 1, 0), 0.0)
                e_im = jnp.where(row >= 1, pltpu.roll(t_im, 1, 0), 0.0)
                c_re, c_im = _cmul(pw_re, pw_im, h_re, h_im)
                hp_scr[pl.ds(r0, SUBLANES), :] = jnp.concatenate([e_re + c_re, e_im + c_im], axis=1)
                o_re, o_im = _cmul(ap[SUBLANES:SUBLANES + 1, :half], ap[SUBLANES:SUBLANES + 1, half:],
                                   h_re, h_im)
                last = SUBLANES - 1
                n_re = jnp.broadcast_to(t_re[last:last + 1], h_re.shape) + o_re
                n_im = jnp.broadcast_to(t_im[last:last + 1], h_im.shape) + o_im
                return n_re, n_im

            init = (jnp.broadcast_to(h0[:, :half], (SUBLANES, half)),
                    jnp.broadcast_to(h0[:, half:], (SUBLANES, half)))
            h_re, h_im = lax.fori_loop(0, rows // SUBLANES, scan_tile, init, unroll=4)
            hf_ref[pi] = jnp.concatenate([h_re[0:1], h_im[0:1]], axis=1)
        nt = (((1,), (1,)), ((), ()))
        yc = _pdot(hp_scr[...], wo_ref[pi, 0], dims=nt)
        yg_scr[2 * pi] = _pdot(x0, m_ref[pi, 0, 0]) + yc[:, :n_tap]
        yg_scr[2 * pi + 1] = _pdot(x1, m_ref[pi, 0, 1]) + yc[:, n_tap:]
        if tail_rows:
            yc_lo = _pdot_lo_terms(hp_scr[tail, :], *part(wo_ref), dims=nt)
            yg_scr[2 * pi, tail, :] += _pdot_lo_terms(x0[rows - tail_rows:], *part(m_ref, 0)) + yc_lo[:, :n_tap]
            yg_scr[2 * pi + 1, tail, :] += (_pdot_lo_terms(x1[rows - tail_rows:], *part(m_ref, 1))
                                            + yc_lo[:, n_tap:])

    def scatter_tile(t, carry):
        r0 = pl.multiple_of(t * rt, rt)
        for hf in range(2):
            vs = [yg_scr[g, pl.ds(r0, rt), hf * LANES:(hf + 1) * LANES] for g in range(GROUPS_PER_SLAB)]
            outs = _block_transpose8(vs)
            for i in range(half_chunk):
                y_ref[pl.ds(r0 * S5_CHUNK + hf * half_chunk + i, rt, stride=S5_CHUNK), :] = outs[i]
        return carry

    lax.fori_loop(0, rows // rt, scatter_tile, 0)


def _s5_mixer(u, h0_re, h0_im, prep, bsz, t_len, tail):
    m, wsp, wop, a16 = prep
    parts = 2 if tail > 0 else 1
    n_tap = S5_CHUNK * SSM_GROUP
    st = 4 * SSM_STATE
    independent = t_len == S5_CHUNK
    if independent:
        nblk, rows, hrows = 1, bsz, bsz
    else:
        nblk, rows, hrows = bsz, t_len // S5_CHUNK, 1
    assert t_len % S5_CHUNK == 0 and rows % SUBLANES == 0, (bsz, t_len)
    h0p = jnp.concatenate([h0_re.reshape(bsz, N_PAIRS, 2 * SSM_STATE),
                           h0_im.reshape(bsz, N_PAIRS, 2 * SSM_STATE)], axis=2).astype(F32)
    h0p = h0p.transpose(1, 0, 2)[None] if independent else h0p[:, :, None, :]
    pp = PAIRS_PER_SLAB

    def wspec(*shape):
        return pl.BlockSpec((pp,) + shape, lambda q, b: (q,) + (0,) * len(shape))

    frames = rows * S5_CHUNK
    if tail <= 0:
        tail_rows = 0
    elif independent:
        tail_rows = rows
    else:
        tail_rows = min(rows, -(-tail // (S5_CHUNK * SUBLANES)) * SUBLANES)
    y, hf = pl.pallas_call(
        functools.partial(_s5_kernel, rows=rows, independent=independent, tail_rows=tail_rows),
        grid=(N_SLABS, nblk),
        in_specs=[pl.BlockSpec((None, frames, LANES), lambda q, b: (q, b, 0)),
                  pl.BlockSpec((None, pp, hrows, st), lambda q, b: (b, q, 0, 0)),
                  wspec(parts, 2, n_tap, n_tap), wspec(parts, 2 * n_tap, st), wspec(parts, 2 * n_tap, st),
                  wspec(2 * SUBLANES, st)],
        out_specs=[pl.BlockSpec((None, frames, LANES), lambda q, b: (q, b, 0)),
                   pl.BlockSpec((None, pp, hrows, st), lambda q, b: (b, q, 0, 0))],
        out_shape=[jax.ShapeDtypeStruct(u.shape, F32),
                   jax.ShapeDtypeStruct((nblk, N_PAIRS, hrows, st), F32)],
        scratch_shapes=[pltpu.VMEM((GROUPS_PER_SLAB, rows, n_tap), F32),
                        pltpu.VMEM((GROUPS_PER_SLAB, rows, n_tap), F32),
                        pltpu.VMEM((rows, st), F32), pltpu.VMEM((rows, st), F32)],
        compiler_params=_params("parallel", "parallel"),
        name="s5_core",
    )(u, h0p, m, wsp, wop, a16)
    hf = hf[0].transpose(1, 0, 2) if independent else hf[:, :, 0, :]
    hf_re = hf[:, :, :2 * SSM_STATE].reshape(bsz, N_GROUPS, SSM_STATE)
    hf_im = hf[:, :, 2 * SSM_STATE:].reshape(bsz, N_GROUPS, SSM_STATE)
    return y, hf_re, hf_im


def _conv_kernel(v_ref, left_ref, w_ref, b_ref, g_ref, beta_ref, y_ref, cn_ref, vbuf, shifted, *, tm):
    @pl.when(pl.program_id(1) == 0)
    def _():
        vbuf[0:CONV_HALO, :] = left_ref[...]

    vbuf[CONV_HALO:CONV_HALO + tm, :] = v_ref[...]
    first = CONV_HALO - (CONV_WIDTH - 1)
    span = tm + CONV_HALO - SUBLANES
    for r in range(1, SUBLANES):
        shifted[r - 1, 0:span, :] = vbuf[r:r + span, :]
    acc = jnp.zeros((tm, D_CONV), F32)
    for k in range(CONV_WIDTH):
        a, r = divmod(first + k, SUBLANES)
        src = vbuf if r == 0 else shifted.at[r - 1]
        acc = acc + w_ref[k:k + 1, :] * src[a * SUBLANES:a * SUBLANES + tm, :]
    y = acc + b_ref[...]
    mu = jnp.mean(y, axis=-1, keepdims=True)
    yc = y - mu
    var = jnp.mean(yc * yc, axis=-1, keepdims=True)
    yn = yc * lax.rsqrt(var + EPS) * g_ref[...] + beta_ref[...]
    y_ref[...] = yn * jax.nn.sigmoid(yn)
    cn_ref[...] = vbuf[tm + first:tm + CONV_HALO, :]
    vbuf[0:CONV_HALO, :] = vbuf[tm:tm + CONV_HALO, :]


def _conv_mixer(v, left, w, b, g, beta, layer, bsz, t_len, tm):
    left = jnp.pad(left.astype(F32), ((0, 0), (CONV_HALO - (CONV_WIDTH - 1), 0), (0, 0)))
    nt = t_len // tm
    vec = _layer_spec(layer, 1, D_CONV)
    return pl.pallas_call(
        functools.partial(_conv_kernel, tm=tm),
        grid=(bsz, nt),
        in_specs=[pl.BlockSpec((tm, D_CONV), lambda bi, j: (bi * nt + j, 0)),
                  pl.BlockSpec((None, CONV_HALO, D_CONV), lambda bi, j: (bi, 0, 0)),
                  _layer_spec(layer, CONV_WIDTH, D_CONV), vec, vec, vec],
        out_specs=[pl.BlockSpec((tm, D_CONV), lambda bi, j: (bi * nt + j, 0)),
                   pl.BlockSpec((None, CONV_WIDTH - 1, D_CONV), lambda bi, j: (bi, 0, 0))],
        out_shape=[jax.ShapeDtypeStruct((bsz * t_len, D_CONV), F32),
                   jax.ShapeDtypeStruct((bsz, CONV_WIDTH - 1, D_CONV), F32)],
        scratch_shapes=[pltpu.VMEM((tm + CONV_HALO, D_CONV), F32),
                        pltpu.VMEM((SUBLANES - 1, tm + CONV_HALO - SUBLANES, D_CONV), F32)],
        compiler_params=_params("parallel", "arbitrary"),
        name="conv_mixer",
    )(v, left, w, _vec(b), _vec(g), _vec(beta))


def _route(logits):
    lane_i = lax.broadcasted_iota(jnp.int32, logits.shape, 1)
    lane = lane_i.astype(F32)
    group_of_lane = (lane_i >> 3).astype(F32)
    neg = -jnp.inf
    far = float(LANES)
    is_g = (lane_i >= N_EXPERTS) & (lane_i < N_EXPERTS + N_EXPERT_GROUPS)
    gl = jnp.where(is_g, logits, neg)
    g_max = jnp.max(gl, axis=-1, keepdims=True)
    g_lane = jnp.min(jnp.where(gl == g_max, lane, far), axis=-1, keepdims=True)
    g_gate = 1.0 / jnp.sum(jnp.exp(gl - g_max), axis=-1, keepdims=True)
    g_idx = g_lane - float(N_EXPERTS)
    in_group = (lane_i < N_EXPERTS) & (group_of_lane == g_idx)
    el = jnp.where(in_group, logits, neg)
    v1 = jnp.max(el, axis=-1, keepdims=True)
    i1 = jnp.min(jnp.where(el == v1, lane, far), axis=-1, keepdims=True)
    el2 = jnp.where(lane == i1, neg, el)
    v2 = jnp.max(el2, axis=-1, keepdims=True)
    i2 = jnp.min(jnp.where(el2 == v2, lane, far), axis=-1, keepdims=True)
    e2 = jnp.exp(v2 - v1)
    w1 = g_gate / (1.0 + e2)
    w2 = g_gate * e2 / (1.0 + e2)
    return i1, i2, w1, w2


ROUTE_E1, ROUTE_E2, ROUTE_W1, ROUTE_W2, ROUTE_RANK1, ROUTE_RANK2 = range(6)


def _outproj_kernel(x_ref, ys_ref, u_ref, yb_ref, d_ref, wglu_ref, wout_ref, nf_ref, rw_ref, rb_ref, cnt0_ref,
                    x1_ref, hn_ref, route_ref, rt_ref, cnt_ref, wglu_bf, wout_bf, rw_bf, acc_glu, acc_mix,
                    *, tail_tiles):
    @pl.when(pl.program_id(0) == 0)
    def _():
        cnt_ref[...] = cnt0_ref[...]
        rw = rw_ref[...]
        rw_hi = rw.astype(BF16)
        rw_bf[:, 0:LANES] = rw_hi
        rw_bf[:, LANES:] = (rw - rw_hi.astype(F32)).astype(BF16)

    _cast_weight_once(wglu_ref, wglu_bf)
    _cast_weight_once(wout_ref, wout_bf)
    ys = jnp.concatenate([ys_ref[q] for q in range(N_SLABS)], axis=1)
    u = jnp.concatenate([u_ref[q] for q in range(N_SLABS)], axis=1)
    z = jax.nn.gelu(ys + d_ref[...] * u)
    ya = z * jax.nn.sigmoid(_pdot_tail(z, *_parts(wglu_bf), acc_glu, tail_tiles))
    mix = _pdot_tail(jnp.concatenate([ya, yb_ref[...]], axis=1), *_parts(wout_bf), acc_mix, tail_tiles)
    x1 = x_ref[...] + mix
    x1_ref[...] = x1
    hn = _rms(x1, nf_ref[...])
    _store_row_tiles(hn_ref, hn)
    h_hi = hn.astype(BF16)
    h_lo = (hn - h_hi.astype(F32)).astype(BF16)
    hw = jnp.dot(h_hi, rw_bf[...], preferred_element_type=F32)
    logits = (hw[:, :LANES] + hw[:, LANES:]
              + jnp.dot(h_lo, rw_bf[:, 0:LANES], preferred_element_type=F32) + rb_ref[...])
    i1, i2, w1, w2 = _route(logits)
    tm = logits.shape[0]
    lane_i = lax.broadcasted_iota(jnp.int32, logits.shape, 1)
    lane = lane_i.astype(F32)
    picked = jnp.where((lane == i1) | (lane == i2), 1.0, 0.0)
    earlier = (lax.broadcasted_iota(jnp.int32, (tm, tm), 0) > lax.broadcasted_iota(jnp.int32, (tm, tm), 1))
    prefix = _bdot(jnp.where(earlier, 1.0, 0.0), picked.astype(BF16)) + cnt_ref[...]
    rank1 = jnp.sum(jnp.where(lane == i1, prefix, 0.0), axis=-1, keepdims=True)
    rank2 = jnp.sum(jnp.where(lane == i2, prefix, 0.0), axis=-1, keepdims=True)
    cnt_ref[...] += jnp.sum(picked, axis=0, keepdims=True)
    rec = jnp.zeros_like(logits)
    for lane_id, val in ((ROUTE_E1, i1), (ROUTE_E2, i2), (ROUTE_W1, w1), (ROUTE_W2, w2),
                         (ROUTE_RANK1, rank1), (ROUTE_RANK2, rank2)):
        rec = jnp.where(lane_i == lane_id, val, rec)
    route_ref[...] = rec
    rt_ref[...] = rec.T[0:SUBLANES, :]


def _outproj(x, ys, u, yb, d, wglu, wout, nf, rw, rb, counts0, layer, tm, t_len, tail):
    n = x.shape[0]
    precise = tail > 0
    return pl.pallas_call(
        functools.partial(_outproj_kernel, tail_tiles=_stream_tail_tiles(t_len, tm, tail)),
        grid=(n // tm,),
        in_specs=[_row_spec(tm, D_MODEL), _slab_spec(tm), _slab_spec(tm), _row_spec(tm, D_CONV),
                  _layer_spec(layer, 1, D_SSM), _layer_spec(layer, D_SSM, D_SSM),
                  _layer_spec(layer, D_MODEL, D_MODEL), _layer_spec(layer, 1, D_MODEL),
                  _const_spec((D_MODEL, LANES)), _const_spec((1, LANES)), _const_spec((1, LANES))],
        out_specs=[_row_spec(tm, D_MODEL), _row_tile_spec(tm), _row_spec(tm, LANES),
                   pl.BlockSpec((SUBLANES, tm), lambda i: (0, i)), _const_spec((1, LANES))],
        out_shape=[jax.ShapeDtypeStruct((n, D_MODEL), F32), jax.ShapeDtypeStruct(_tiled_rows(n), F32),
                   jax.ShapeDtypeStruct((n, LANES), F32), jax.ShapeDtypeStruct((SUBLANES, n), F32),
                   jax.ShapeDtypeStruct((1, LANES), F32)],
        scratch_shapes=[_weight_scratch(D_SSM, D_SSM, precise), _weight_scratch(D_MODEL, D_MODEL, precise),
                        pltpu.VMEM((D_MODEL, 2 * LANES), BF16),
                        _tail_acc(tm, D_SSM, precise), _tail_acc(tm, D_MODEL, precise)],
        compiler_params=_params("arbitrary"),
        name="outproj_router",
    )(x, ys, u, yb, _vec(d), wglu, wout, _vec(nf), rw, rb, counts0)


PLAN_TILE_LANES = 2 * LANES
PLAN_EXPERT, PLAN_ROWS, PLAN_USED = range(3)


def _moe_plan_kernel(rt_ref, cnt_ref, cnt_first_ref, slot_ref, tile_ref, *, tms):
    cnt = cnt_ref[...]
    padded = jnp.ceil(cnt * (1.0 / tms)) * float(tms)
    r = lax.broadcasted_iota(jnp.int32, (LANES, LANES), 0)
    c = lax.broadcasted_iota(jnp.int32, (LANES, LANES), 1)
    ends = jnp.dot(padded, jnp.where(r <= c, 1.0, 0.0), precision=lax.Precision.HIGHEST,
                   preferred_element_type=F32)
    starts = ends - padded
    rt = rt_ref[...]
    e1, e2 = rt[ROUTE_E1:ROUTE_E1 + 1], rt[ROUTE_E2:ROUTE_E2 + 1]
    s1, s2 = rt[ROUTE_RANK1:ROUTE_RANK1 + 1], rt[ROUTE_RANK2:ROUTE_RANK2 + 1]
    tile = lax.broadcasted_iota(jnp.int32, (1, PLAN_TILE_LANES), 1).astype(F32)
    used = ends[:, N_EXPERTS - 1:N_EXPERTS] * (1.0 / tms)
    pos = jnp.minimum(tile, used - 1.0) * float(tms)
    t_exp = jnp.zeros_like(tile)
    t_fill = jnp.zeros_like(tile)
    for e in range(N_EXPERTS):
        st, en = starts[:, e:e + 1], ends[:, e:e + 1]
        s1 = s1 + jnp.where(e1 == float(e), st, 0.0)
        s2 = s2 + jnp.where(e2 == float(e), st, 0.0)
        mine = (pos >= st) & (pos < en)
        t_exp = t_exp + jnp.where(mine, float(e), 0.0)
        t_fill = t_fill + jnp.where(mine, st + cnt_first_ref[:, e:e + 1], 0.0)
    t_rows = jnp.where(tile < used, jnp.clip(t_fill - pos, 0.0, float(tms)), 0.0)
    slot_ref[...] = jnp.concatenate([s1, s2], axis=0).astype(jnp.int32)
    tile_ref[...] = jnp.concatenate(
        [t_exp, t_rows, jnp.broadcast_to(used, tile.shape), jnp.zeros((SUBLANES - 3, PLAN_TILE_LANES), F32)],
        axis=0).astype(jnp.int32)


def _moe_plan(route_t, counts, counts_first, tms, n_tiles):
    n = route_t.shape[1]
    assert n_tiles <= PLAN_TILE_LANES
    slots, tiles = pl.pallas_call(
        functools.partial(_moe_plan_kernel, tms=tms),
        out_shape=[jax.ShapeDtypeStruct((2, n), jnp.int32),
                   jax.ShapeDtypeStruct((SUBLANES, PLAN_TILE_LANES), jnp.int32)],
        compiler_params=pltpu.CompilerParams(vmem_limit_bytes=VMEM_LIMIT),
        name="moe_plan",
    )(route_t, counts, counts_first)
    return slots, tiles[PLAN_EXPERT, :n_tiles], tiles[PLAN_ROWS, :n_tiles], tiles[PLAN_USED, :1]


DMA_UNROLL = 8


HN_BUFFERS = 3


def _dispatch_kernel(tr_ref, slot_ref, hn_ref, *rest, tm, tms, n_tiles, n_steps, fresh):
    xs_ref, zbuf, hbuf, in_sem, out_sem, zsem = rest if fresh else rest[1:]
    i = pl.program_id(0)

    def fetch(t):
        b = lax.rem(t, HN_BUFFERS)
        first = pl.multiple_of(t * (tm * ROW_SUB), tm * ROW_SUB)
        return pltpu.make_async_copy(hn_ref.at[pl.ds(first, tm * ROW_SUB)], hbuf.at[b], in_sem.at[b])

    def drain_scatter(t):
        b = lax.rem(t, HN_BUFFERS)
        for k in range(2):
            pltpu.make_async_copy(hbuf.at[b], xs_ref.at[pl.ds(0, tm * ROW_SUB)], out_sem.at[b]).wait()

    @pl.when(i == 0)
    def _():
        fetch(i).start()

    @pl.when(jnp.logical_and(i == 0, fresh))
    def _():
        zbuf[...] = jnp.zeros_like(zbuf)

        def fill(t, carry):
            @pl.when(tr_ref[t] < tms)
            def _():
                first = pl.multiple_of(t * (tms * ROW_SUB), tms * ROW_SUB)
                pltpu.make_async_copy(zbuf, xs_ref.at[pl.ds(first, tms * ROW_SUB)], zsem).start()
            return carry

        def drain(t, carry):
            @pl.when(tr_ref[t] < tms)
            def _():
                pltpu.make_async_copy(zbuf, xs_ref.at[pl.ds(0, tms * ROW_SUB)], zsem).wait()
            return carry

        lax.fori_loop(0, n_tiles, fill, 0)
        lax.fori_loop(0, n_tiles, drain, 0)

    @pl.when(i + 1 < n_steps)
    def _():
        fetch(i + 1).start()

    fetch(i).wait()
    b = lax.rem(i, HN_BUFFERS)
    rows = hbuf.at[b]

    def issue(r, carry):
        for k in range(2):
            pltpu.make_async_copy(_one_row(rows, r), _one_row(xs_ref, slot_ref[k, r]),
                                  out_sem.at[b]).start(priority=k)
        return carry

    lax.fori_loop(0, tm, issue, 0, unroll=DMA_UNROLL)

    @pl.when(i >= 1)
    def _():
        drain_scatter(i - 1)

    @pl.when(i == n_steps - 1)
    def _():
        drain_scatter(i)


def _dispatch(hn, slots, tile_rows, tms, tm, into=None):
    n = hn.shape[0] // ROW_SUB
    n_tiles = tile_rows.shape[0]
    fresh = into is None
    any_spec = pl.BlockSpec(memory_space=pl.ANY)
    return pl.pallas_call(
        functools.partial(_dispatch_kernel, tm=tm, tms=tms, n_tiles=n_tiles, n_steps=n // tm, fresh=fresh),
        grid_spec=pltpu.PrefetchScalarGridSpec(
            num_scalar_prefetch=1, grid=(n // tm,),
            in_specs=[pl.BlockSpec((2, tm), lambda i, tr: (0, i), memory_space=pltpu.SMEM), any_spec]
                     + ([] if fresh else [any_spec]),
            out_specs=any_spec,
            scratch_shapes=[pltpu.VMEM(_tiled_rows(tms), F32), pltpu.VMEM((HN_BUFFERS,) + _tiled_rows(tm), F32),
                            pltpu.SemaphoreType.DMA((HN_BUFFERS,)), pltpu.SemaphoreType.DMA((HN_BUFFERS,)),
                            pltpu.SemaphoreType.DMA]),
        out_shape=jax.ShapeDtypeStruct(_tiled_rows(n_tiles * tms), F32),
        input_output_aliases={} if fresh else {3: 0},
        compiler_params=_params("arbitrary"),
        name="moe_dispatch",
    )(tile_rows, slots, hn, *([] if fresh else [into]))


def _moe_kernel(te_ref, nu_ref, x_ref, wg_ref, wu_ref, wd_ref, y_ref, wg_bf, wu_bf, wd_bf):
    i = pl.program_id(0)
    in_use = i < nu_ref[0]
    new_expert = (i == 0) | (te_ref[i] != te_ref[jnp.maximum(i - 1, 0)])

    @pl.when(in_use & new_expert)
    def _():
        wg_bf[...] = wg_ref[...].astype(BF16)
        wu_bf[...] = wu_ref[...].astype(BF16)
        wd_bf[...] = wd_ref[...].astype(BF16)

    @pl.when(in_use)
    def _():
        h = _load_row_tiles(x_ref).astype(BF16)
        hg = jnp.dot(h, wg_bf[...], preferred_element_type=F32)
        hu = jnp.dot(h, wu_bf[...], preferred_element_type=F32)
        _store_row_tiles(y_ref, _bdot(hg * jax.nn.sigmoid(hg) * hu, wd_bf[...]))

    @pl.when(jnp.logical_not(in_use))
    def _():
        y_ref[...] = jnp.zeros_like(y_ref)


def _moe(xs, tile_expert, n_used, wg, wu, wd, layer, tms):
    n_slots = xs.shape[0] // ROW_SUB
    rows = pl.BlockSpec(_tiled_rows(tms), lambda i, te, nu: (jnp.minimum(i, nu[0] - 1), 0))
    out_rows = pl.BlockSpec(_tiled_rows(tms), lambda i, te, nu: (i, 0))

    def wspec(a, b):
        return pl.BlockSpec((None, None, a, b), lambda i, te, nu: (layer, te[i], 0, 0))

    return pl.pallas_call(
        _moe_kernel,
        grid_spec=pltpu.PrefetchScalarGridSpec(
            num_scalar_prefetch=2, grid=(n_slots // tms,),
            in_specs=[rows, wspec(D_MODEL, D_EXPERT), wspec(D_MODEL, D_EXPERT), wspec(D_EXPERT, D_MODEL)],
            out_specs=out_rows,
            scratch_shapes=[pltpu.VMEM((D_MODEL, D_EXPERT), BF16), pltpu.VMEM((D_MODEL, D_EXPERT), BF16),
                            pltpu.VMEM((D_EXPERT, D_MODEL), BF16)]),
        out_shape=jax.ShapeDtypeStruct(xs.shape, F32),
        compiler_params=_params("arbitrary"),
        name="moe",
    )(tile_expert, n_used, xs, wg, wu, wd)


def _ple_kernel(slot_ref, next_slot_ref, x_ref, route_ref, p_ref, np_ref, wple_ref, wgate_ref, nfin_ref,
                ys_ref, o_ref, ybuf, sem, wple_bf, wgate_bf, *, tm, n_steps, final):
    i = pl.program_id(0)
    _cast_weight_once(wple_ref, wple_bf)
    _cast_weight_once(wgate_ref, wgate_bf)

    def gather(slots, b):
        def issue(r, carry):
            for k in range(2):
                pltpu.make_async_copy(_one_row(ys_ref, slots[k, r]), _one_row(ybuf.at[b, k], r),
                                      sem.at[b]).start(priority=k)
            return carry

        lax.fori_loop(0, tm, issue, 0, unroll=DMA_UNROLL)

    @pl.when(i == 0)
    def _():
        gather(slot_ref, 0)

    @pl.when(i + 1 < n_steps)
    def _():
        gather(next_slot_ref, lax.rem(i + 1, 2))

    pe = _pdot(p_ref[...], *_parts(wple_bf))
    b = lax.rem(i, 2)
    for k in range(2):
        pltpu.make_async_copy(ys_ref.at[pl.ds(0, tm * ROW_SUB)], ybuf.at[b, k], sem.at[b]).wait()
    route = route_ref[...]
    x = (x_ref[...] + route[:, ROUTE_W1:ROUTE_W1 + 1] * _load_row_tiles(ybuf.at[b, 0])
         + route[:, ROUTE_W2:ROUTE_W2 + 1] * _load_row_tiles(ybuf.at[b, 1]))
    gate = jax.nn.sigmoid(_pdot(_rms(x, np_ref[...]), *_parts(wgate_bf)))
    out = x + pe * gate
    if final:
        out = _rms(out, nfin_ref[...])
    o_ref[...] = out


def _ple(x, route, slots, ys, p, npl, wple, wgate, nfin, layer, tm, final):
    n = x.shape[0]
    n_steps = n // tm
    return pl.pallas_call(
        functools.partial(_ple_kernel, tm=tm, n_steps=n_steps, final=final),
        grid=(n_steps,),
        in_specs=[pl.BlockSpec((2, tm), lambda i: (0, i), memory_space=pltpu.SMEM),
                  pl.BlockSpec((2, tm), lambda i: (0, jnp.minimum(i + 1, n_steps - 1)),
                               memory_space=pltpu.SMEM),
                  _row_spec(tm, D_MODEL), _row_spec(tm, LANES),
                  pl.BlockSpec((None, tm, D_PLE), lambda i: (layer, i, 0)),
                  _layer_spec(layer, 1, D_MODEL), _layer_spec(layer, D_PLE, D_MODEL),
                  _layer_spec(layer, D_MODEL, D_MODEL), _const_spec((1, D_MODEL)),
                  pl.BlockSpec(memory_space=pl.ANY)],
        out_specs=_row_spec(tm, D_MODEL),
        out_shape=jax.ShapeDtypeStruct((n, D_MODEL), F32),
        scratch_shapes=[pltpu.VMEM((2, 2) + _tiled_rows(tm), F32), pltpu.SemaphoreType.DMA((2,)),
                        _weight_scratch(D_PLE, D_MODEL, False), _weight_scratch(D_MODEL, D_MODEL, False)],
        compiler_params=_params("arbitrary"),
        name="combine_ple",
    )(slots, slots, x, route, p, _vec(npl), wple, wgate, nfin.reshape(1, D_MODEL), ys)


def _mixers(x, h0_re, h0_im, conv_left, w, s5, router_w, router_b, counts0, layer, bsz, t_len, tm, tm_conv,
            tail):
    u, v = _inproj(x, w["norm_mix"], w["w_in"], layer, min(2 * tm, x.shape[0]), t_len, tail)
    ys, hf_re, hf_im = _s5_mixer(u, h0_re, h0_im, s5, bsz, t_len, tail)
    yb, conv_new = _conv_mixer(v, conv_left, w["conv_w"], w["conv_b"], w["conv_ln_g"], w["conv_ln_b"],
                               layer, bsz, t_len, tm_conv)
    x1, hn, route, route_t, counts = _outproj(x, ys, u, yb, w["ssm_d"], w["w_ssm_glu"], w["w_out"],
                                              w["norm_ffn"], router_w, router_b, counts0, layer, tm, t_len, tail)
    return dict(x1=x1, hn=hn, route=route, route_t=route_t, counts=counts, state=(hf_re, hf_im, conv_new))


def _moe_and_ple(sets, ps, tms_rows, w, layer, tms, final):
    sizes = [s["x1"].shape[0] for s in sets]
    n_tiles = 2 * sum(sizes) // tms + N_EXPERTS
    route_t = jnp.concatenate([s["route_t"] for s in sets], axis=1)
    slots, tile_expert, tile_rows, n_used = _moe_plan(route_t, sets[-1]["counts"], sets[0]["counts"], tms,
                                                     n_tiles)
    starts = [sum(sizes[:j]) for j in range(len(sets))]
    set_slots = [slots[:, a:a + n] for a, n in zip(starts, sizes)]
    xsort = None
    for s, sl, tm in zip(sets, set_slots, tms_rows):
        xsort = _dispatch(s["hn"], sl, tile_rows, tms, tm, into=xsort)
    ysort = _moe(xsort, tile_expert, n_used, w["expert_w_gate"], w["expert_w_up"], w["expert_w_down"],
                 layer, tms)
    return [_ple(s["x1"], s["route"], sl, ysort, p, w["norm_ple"], w["ple_w"], w["ple_gate_w"],
                 w["norm_final"], layer, tm, final)
            for s, sl, p, tm in zip(sets, set_slots, ps, tms_rows)]


def kernel(x_prompt, x_sample, p_prompt, p_sample, state_ssm_re, state_ssm_im, cache_conv, norm_mix, w_in, ssm_a_re, ssm_a_im, ssm_b_re, ssm_b_im, ssm_c_re, ssm_c_im, ssm_d, ssm_log_dt, w_ssm_glu, conv_w, conv_b, conv_ln_g, conv_ln_b, w_out, norm_ffn, router_group_w, router_group_b, router_expert_w, router_expert_b, expert_w_gate, expert_w_up, expert_w_down, norm_ple, ple_w, ple_gate_w, norm_final):
    depth = w_in.shape[0]
    bp, tp, _ = x_prompt.shape
    bs, ts, _ = x_sample.shape
    xp = x_prompt.reshape(bp * tp, D_MODEL)
    xs = x_sample.reshape(bs * ts, D_MODEL)
    pp = p_prompt.reshape(depth, bp * tp, D_PLE)
    ps = p_sample.reshape(depth, bs * ts, D_PLE)
    zero_state = jnp.zeros((bp, N_GROUPS, SSM_STATE), F32)
    zero_conv = jnp.zeros((bp, CONV_WIDTH - 1, D_CONV), F32)
    pad_lanes = LANES - N_EXPERTS - N_EXPERT_GROUPS
    w = {"norm_mix": norm_mix, "w_in": w_in, "ssm_d": ssm_d, "w_ssm_glu": w_ssm_glu, "conv_w": conv_w,
         "conv_b": conv_b, "conv_ln_g": conv_ln_g, "conv_ln_b": conv_ln_b, "w_out": w_out,
         "norm_ffn": norm_ffn, "expert_w_gate": expert_w_gate, "expert_w_up": expert_w_up,
         "expert_w_down": expert_w_down, "norm_ple": norm_ple, "ple_w": ple_w, "ple_gate_w": ple_gate_w,
         "norm_final": norm_final}
    no_picks = jnp.zeros((1, LANES), F32)
    outs = {k: [] for k in ("pr_re", "pr_im", "pr_conv", "sm_re", "sm_im", "sm_conv")}
    for i in range(depth):
        s5 = _s5_prep(ssm_a_re[i], ssm_a_im[i], ssm_log_dt[i], ssm_b_re[i], ssm_b_im[i], ssm_c_re[i],
                      ssm_c_im[i])
        router_w = jnp.pad(jnp.concatenate([router_expert_w[i], router_group_w[i]], axis=1),
                           ((0, 0), (0, pad_lanes)))
        router_b = jnp.pad(jnp.concatenate([router_expert_b[i], router_group_b[i]]),
                           (0, pad_lanes)).reshape(1, LANES)
        final = i == depth - 1
        tail = 0 if final else PRECISE_TAIL
        mp = _mixers(xp, zero_state, zero_state, zero_conv, w, s5, router_w, router_b, no_picks, i,
                     bp, tp, tm=TOKEN_TILE, tm_conv=TOKEN_TILE, tail=tail)
        ms = _mixers(xs, state_ssm_re[i], state_ssm_im[i], cache_conv[i], w, s5, router_w, router_b,
                     mp["counts"], i, bs, ts, tm=bs * ts, tm_conv=ts, tail=tail)
        xp, xs = _moe_and_ple([mp, ms], [pp, ps], [TOKEN_TILE, bs * ts], w, i, SLOT_TILE, final)
        for key, val in zip(("pr_re", "pr_im", "pr_conv"), mp["state"]):
            outs[key].append(val)
        for key, val in zip(("sm_re", "sm_im", "sm_conv"), ms["state"]):
            outs[key].append(val)
    return (xp.reshape(bp, tp, D_MODEL), xs.reshape(bs, ts, D_MODEL),
            jnp.stack(outs["pr_re"]), jnp.stack(outs["pr_im"]), jnp.stack(outs["pr_conv"]),
            jnp.stack(outs["sm_re"]), jnp.stack(outs["sm_im"]), jnp.stack(outs["sm_conv"]))
```

```python
import functools

import jax
import jax.numpy as jnp
from jax import lax
from jax.experimental import pallas as pl
from jax.experimental.pallas import tpu as pltpu

F32 = jnp.float32
BF16 = jnp.bfloat16

D_MODEL = 1024
D_SSM = 512
SSM_GROUP = 16
N_GROUPS = D_SSM // SSM_GROUP
N_PAIRS = N_GROUPS // 2
SSM_STATE = 64
D_CONV = 512
CONV_WIDTH = 31
CONV_HALO = 32
N_EXPERT_GROUPS = 4
EXPERTS_PER_GROUP = 8
N_EXPERTS = 32
D_EXPERT = 256
D_PLE = 256
EPS = 1e-6
S5_CHUNK = 16
LANES = 128
SUBLANES = 8
N_SLABS = D_SSM // LANES
GROUPS_PER_SLAB = LANES // SSM_GROUP
PAIRS_PER_SLAB = GROUPS_PER_SLAB // 2
VMEM_LIMIT = 56 * 1024 * 1024
TOKEN_TILE = 512
SLOT_TILE = 512
PRECISE_TAIL = 1024


def _params(*sem):
    return pltpu.CompilerParams(dimension_semantics=sem, vmem_limit_bytes=VMEM_LIMIT)


def _rms(x, g):
    return x * lax.rsqrt(jnp.mean(x * x, axis=-1, keepdims=True) + EPS) * g


def _bdot(a, b):
    return jnp.dot(a.astype(BF16), b, preferred_element_type=F32)


def _split_bf16(a):
    hi = a.astype(BF16)
    return hi, (a - hi.astype(F32)).astype(BF16)


def _pdot(a, w_hi, w_lo=None, dims=(((1,), (0,)), ((), ()))):
    def mm(x, w):
        return lax.dot_general(x, w, dims, preferred_element_type=F32)

    if w_lo is None:
        return mm(a.astype(BF16), w_hi)
    a_hi, a_lo = _split_bf16(a)
    return mm(a_hi, w_hi) + (mm(a_hi, w_lo) + mm(a_lo, w_hi))


def _pdot_lo_terms(a, w_hi, w_lo, dims=(((1,), (0,)), ((), ()))):
    a_hi, a_lo = _split_bf16(a)
    return (lax.dot_general(a_hi, w_lo, dims, preferred_element_type=F32)
            + lax.dot_general(a_lo, w_hi, dims, preferred_element_type=F32))


def _stream_tail_tiles(t_len, tm, tail):
    tiles = max(t_len // tm, 1)
    return tiles, (max(t_len - tail, 0) // tm if tm < t_len else 0)


def _pdot_tail(a, w_hi, w_lo, acc_ref, tail_tiles):
    if w_lo is None:
        return _pdot(a, w_hi)
    tiles, first = tail_tiles
    if first == 0:
        return _pdot(a, w_hi, w_lo)
    acc_ref[...] = _pdot(a, w_hi)

    @pl.when(lax.rem(pl.program_id(0), tiles) >= first)
    def _():
        acc_ref[...] += _pdot_lo_terms(a, w_hi, w_lo)

    return acc_ref[...]


def _parts(wbf_ref, rows=slice(None)):
    return wbf_ref[0, rows, :], (wbf_ref[1, rows, :] if wbf_ref.shape[0] == 2 else None)


def _row_spec(tm, width):
    return pl.BlockSpec((tm, width), lambda i: (i, 0))


ROW_SUB = D_MODEL // LANES


def _tiled_rows(n):
    return (n * ROW_SUB, LANES)


def _row_tile_spec(tm):
    return pl.BlockSpec(_tiled_rows(tm), lambda i, *_: (i, 0))


def _one_row(ref, r):
    return ref.at[pl.ds(pl.multiple_of(r * ROW_SUB, ROW_SUB), ROW_SUB)]


def _store_row_tiles(ref, rows):
    n = rows.shape[0]
    for s in range(ROW_SUB):
        ref[pl.ds(s, n, stride=ROW_SUB), :] = rows[:, s * LANES:(s + 1) * LANES]


def _load_row_tiles(ref):
    n = ref.shape[0] // ROW_SUB
    return jnp.concatenate([ref[pl.ds(s, n, stride=ROW_SUB), :] for s in range(ROW_SUB)], axis=1)


def _slab_spec(tm):
    return pl.BlockSpec((N_SLABS, tm, LANES), lambda i: (0, i, 0))


def _const_spec(shape):
    return pl.BlockSpec(shape, lambda i: (0,) * len(shape))


def _layer_spec(layer, *shape):
    return pl.BlockSpec((None,) + shape, lambda *_: (layer,) + (0,) * len(shape))


def _vec(stacked):
    return stacked.reshape(stacked.shape[0], 1, stacked.shape[1])


def _cast_weight_once(w_ref, wbf_ref):
    @pl.when(pl.program_id(0) == 0)
    def _():
        w = w_ref[...]
        hi = w.astype(BF16)
        wbf_ref[0] = hi
        if wbf_ref.shape[0] == 2:
            wbf_ref[1] = (w - hi.astype(F32)).astype(BF16)


def _weight_scratch(k, n, precise):
    return pltpu.VMEM((2 if precise else 1, k, n), BF16)


def _tail_acc(rows, n, precise):
    return pltpu.VMEM((rows, n) if precise else (SUBLANES, LANES), F32)


def _inproj_kernel(x_ref, g_ref, w_ref, u_ref, v_ref, wbf, acc, *, tail_tiles):
    _cast_weight_once(w_ref, wbf)
    hn = _rms(x_ref[...], g_ref[...])
    proj = _pdot_tail(hn, *_parts(wbf), acc, tail_tiles)
    for q in range(N_SLABS):
        u_ref[q] = proj[:, q * LANES:(q + 1) * LANES]
    v_ref[...] = proj[:, D_SSM:D_SSM + D_CONV] * jax.nn.sigmoid(proj[:, D_SSM + D_CONV:])


def _inproj(x, g, w, layer, tm, t_len, tail):
    n = x.shape[0]
    d_in = w.shape[2]
    precise = tail > 0
    return pl.pallas_call(
        functools.partial(_inproj_kernel, tail_tiles=_stream_tail_tiles(t_len, tm, tail)),
        grid=(n // tm,),
        in_specs=[_row_spec(tm, D_MODEL), _layer_spec(layer, 1, D_MODEL), _layer_spec(layer, D_MODEL, d_in)],
        out_specs=[_slab_spec(tm), _row_spec(tm, D_CONV)],
        out_shape=[jax.ShapeDtypeStruct((N_SLABS, n, LANES), F32), jax.ShapeDtypeStruct((n, D_CONV), F32)],
        scratch_shapes=[_weight_scratch(D_MODEL, d_in, precise), _tail_acc(tm, d_in, precise)],
        compiler_params=_params("arbitrary"),
        name="inproj",
    )(x, _vec(g), w)


def _s5_prep_kernel(ar_ref, ai_ref, ldt_ref, bre_ref, bim_ref, cre_ref, cim_ref,
                    m_ref, ws_ref, wot_ref, atab_ref, wt_re, wt_im, br_re, br_im):
    n_tap = S5_CHUNK * SSM_GROUP
    st = 4 * SSM_STATE
    nt = (((1,), (1,)), ((), ()))
    hi = lax.Precision.HIGHEST
    lane = lax.broadcasted_iota(jnp.int32, (SSM_GROUP, n_tap), 1)
    ws_rows, wot_rows, atab = [], [], jnp.zeros((2 * SUBLANES, st), F32)
    for gi in range(2):
        ar, ai = ar_ref[gi], ai_ref[gi]
        dt = jnp.exp(ldt_ref[gi])
        k = lax.broadcasted_iota(jnp.int32, (S5_CHUNK + SUBLANES, SSM_STATE), 0).astype(F32)
        mag = jnp.exp(k * (dt * ar))
        ang = k * (dt * ai)
        p_re, p_im = mag * jnp.cos(ang), mag * jnp.sin(ang)
        inv = 1.0 / (ar * ar + ai * ai)
        ab_re, ab_im = p_re[1:2], p_im[1:2]
        ia_re, ia_im = ar * inv, -ai * inv
        coef_re = (ab_re - 1.0) * ia_re - ab_im * ia_im
        coef_im = (ab_re - 1.0) * ia_im + ab_im * ia_re
        bre, bim = bre_ref[gi], bim_ref[gi]
        bb_re = coef_re * bre - coef_im * bim
        bb_im = coef_re * bim + coef_im * bre
        cre, cim = cre_ref[gi], cim_ref[gi]
        for kk in range(S5_CHUNK + 1):
            pr, pi = p_re[kk:kk + 1], p_im[kk:kk + 1]
            rows = slice(kk * SSM_GROUP, (kk + 1) * SSM_GROUP)
            wt_re[rows, :] = pr * cre - pi * cim
            wt_im[rows, :] = -pi * cre - pr * cim
            if kk < S5_CHUNK:
                back = slice((S5_CHUNK - 1 - kk) * SSM_GROUP, (S5_CHUNK - kk) * SSM_GROUP)
                br_re[back, :] = pr * bb_re - pi * bb_im
                br_im[back, :] = pi * bb_re + pr * bb_im
        kcat = (lax.dot_general(bb_re, wt_re[0:n_tap, :], nt, precision=hi, preferred_element_type=F32)
                + lax.dot_general(bb_im, wt_im[0:n_tap, :], nt, precision=hi, preferred_element_type=F32))
        for s in range(S5_CHUNK):
            shifted = kcat if s == 0 else pltpu.roll(kcat, s * SSM_GROUP, 1)
            rows = slice(s * SSM_GROUP, (s + 1) * SSM_GROUP)
            m_ref[0, gi, rows, :], m_ref[1, gi, rows, :] = _split_bf16(
                jnp.where(lane >= s * SSM_GROUP, shifted, 0.0))
        def place(v_re, v_im):
            zero = jnp.zeros_like(v_re)
            parts = [v_re, zero, v_im, zero] if gi == 0 else [zero, v_re, zero, v_im]
            return jnp.concatenate(parts, axis=1)

        ws_rows.append(place(br_re[...], br_im[...]))
        wot_rows.append(place(wt_re[SSM_GROUP:, :], wt_im[SSM_GROUP:, :]))
        kc = float(S5_CHUNK) * lax.broadcasted_iota(jnp.int32, (2 * SUBLANES, SSM_STATE), 0).astype(F32)
        magc = jnp.exp(kc * (dt * ar))
        angc = kc * (dt * ai)
        atab = atab + place(magc * jnp.cos(angc), magc * jnp.sin(angc))
    ws_ref[0], ws_ref[1] = _split_bf16(jnp.concatenate(ws_rows, axis=0))
    wot_ref[0], wot_ref[1] = _split_bf16(jnp.concatenate(wot_rows, axis=0))
    atab_ref[...] = atab


def _s5_prep(a_re, a_im, log_dt, b_re, b_im, c_re, c_im):
    p, n, c = N_PAIRS, SSM_STATE, SSM_GROUP
    n_tap = S5_CHUNK * c
    st = 4 * n

    def pspec(*shape):
        return pl.BlockSpec((None,) + shape, lambda i: (i,) + (0,) * len(shape))

    def pairs(a, *shape):
        return a.reshape((p, 2) + shape)

    return pl.pallas_call(
        _s5_prep_kernel,
        grid=(p,),
        in_specs=[pspec(2, 1, n), pspec(2, 1, n), pspec(2, 1, 1), pspec(2, c, n), pspec(2, c, n),
                  pspec(2, c, n), pspec(2, c, n)],
        out_specs=[pspec(2, 2, n_tap, n_tap), pspec(2, 2 * n_tap, st), pspec(2, 2 * n_tap, st),
                   pspec(2 * SUBLANES, st)],
        out_shape=[jax.ShapeDtypeStruct((p, 2, 2, n_tap, n_tap), BF16),
                   jax.ShapeDtypeStruct((p, 2, 2 * n_tap, st), BF16),
                   jax.ShapeDtypeStruct((p, 2, 2 * n_tap, st), BF16),
                   jax.ShapeDtypeStruct((p, 2 * SUBLANES, st), F32)],
        scratch_shapes=[pltpu.VMEM((n_tap + c, n), F32), pltpu.VMEM((n_tap + c, n), F32),
                        pltpu.VMEM((n_tap, n), F32), pltpu.VMEM((n_tap, n), F32)],
        compiler_params=_params("parallel"),
        name="s5_prep",
    )(pairs(a_re, 1, n), pairs(a_im, 1, n), pairs(log_dt, 1, 1),
      pairs(jnp.swapaxes(b_re, 1, 2), c, n), pairs(jnp.swapaxes(b_im, 1, 2), c, n),
      pairs(c_re, c, n), pairs(c_im, c, n))


def _block_transpose8(vs):
    lane = lax.broadcasted_iota(jnp.int32, vs[0].shape, 1)
    blk = lane >> 4
    for d in (4, 2, 1):
        keep = (blk & d) == 0
        new = list(vs)
        for i in range(GROUPS_PER_SLAB):
            if i & d == 0:
                a, b = vs[i], vs[i + d]
                new[i] = jnp.where(keep, a, pltpu.roll(b, d * SSM_GROUP, 1))
                new[i + d] = jnp.where(keep, pltpu.roll(a, LANES - d * SSM_GROUP, 1), b)
        vs = new
    return vs


def _cmul(ar, ai, xr, xi):
    return ar * xr - ai * xi, ar * xi + ai * xr


def _s5_kernel(u_ref, h0_ref, m_ref, ws_ref, wo_ref, a_ref, y_ref, hf_ref, x_scr, yg_scr, s_scr, hp_scr,
               *, rows, independent, tail_rows):
    half = 2 * SSM_STATE
    tail = pl.ds(rows - tail_rows, tail_rows)
    rt = min(rows, 16 * SUBLANES)
    half_chunk = S5_CHUNK // 2

    def gather_tile(t, carry):
        r0 = pl.multiple_of(t * rt, rt)
        for hf in range(2):
            vs = [u_ref[pl.ds(r0 * S5_CHUNK + hf * half_chunk + i, rt, stride=S5_CHUNK), :]
                  for i in range(half_chunk)]
            outs = _block_transpose8(vs)
            for g in range(GROUPS_PER_SLAB):
                x_scr[g, pl.ds(r0, rt), hf * LANES:(hf + 1) * LANES] = outs[g]
        return carry

    lax.fori_loop(0, rows // rt, gather_tile, 0)

    row = lax.broadcasted_iota(jnp.int32, (SUBLANES, half), 0)
    n_tap = S5_CHUNK * SSM_GROUP
    for pi in range(PAIRS_PER_SLAB):
        def part(w_ref, *idx):
            return w_ref[(pi, 0) + idx], w_ref[(pi, 1) + idx]

        x0 = x_scr[2 * pi]
        x1 = x_scr[2 * pi + 1]
        x01 = jnp.concatenate([x0, x1], axis=1)
        s_scr[...] = _pdot(x01, ws_ref[pi, 0])
        if tail_rows:
            s_scr[tail, :] += _pdot_lo_terms(x01[rows - tail_rows:], *part(ws_ref))
        ap = a_ref[pi]
        h0 = h0_ref[pi]
        if independent:
            hp_scr[...] = h0
            s = s_scr[...]
            n_re, n_im = _cmul(ap[1:2, :half], ap[1:2, half:], h0[:, :half], h0[:, half:])
            hf_ref[pi] = jnp.concatenate([n_re + s[:, :half], n_im + s[:, half:]], axis=1)
        else:
            pw_re, pw_im = ap[0:SUBLANES, :half], ap[0:SUBLANES, half:]

            def scan_tile(t, carry):
                h_re, h_im = carry
                r0 = pl.multiple_of(t * SUBLANES, SUBLANES)
                s = s_scr[pl.ds(r0, SUBLANES), :]
                t_re, t_im = s[:, :half], s[:, half:]
                for d in (1, 2, 4):
                    sh_re = jnp.where(row >= d, pltpu.roll(t_re, d, 0), 0.0)
                    sh_im = jnp.where(row >= d, pltpu.roll(t_im, d, 0), 0.0)
                    m_re, m_im = _cmul(ap[d:d + 1, :half], ap[d:d + 1, half:], sh_re, sh_im)
                    t_re, t_im = t_re + m_re, t_im + m_im
                e_re = jnp.where(row >= 1, pltpu.roll(t_re, 1, 0), 0.0)
                e_im = jnp.where(row >= 1, pltpu.roll(t_im, 1, 0), 0.0)
                c_re, c_im = _cmul(pw_re, pw_im, h_re, h_im)
                hp_scr[pl.ds(r0, SUBLANES), :] = jnp.concatenate([e_re + c_re, e_im + c_im], axis=1)
                o_re, o_im = _cmul(ap[SUBLANES:SUBLANES + 1, :half], ap[SUBLANES:SUBLANES + 1, half:],
                                   h_re, h_im)
                last = SUBLANES - 1
                n_re = jnp.broadcast_to(t_re[last:last + 1], h_re.shape) + o_re
                n_im = jnp.broadcast_to(t_im[last:last + 1], h_im.shape) + o_im
                return n_re, n_im

            init = (jnp.broadcast_to(h0[:, :half], (SUBLANES, half)),
                    jnp.broadcast_to(h0[:, half:], (SUBLANES, half)))
            h_re, h_im = lax.fori_loop(0, rows // SUBLANES, scan_tile, init, unroll=4)
            hf_ref[pi] = jnp.concatenate([h_re[0:1], h_im[0:1]], axis=1)
        nt = (((1,), (1,)), ((), ()))
        yc = _pdot(hp_scr[...], wo_ref[pi, 0], dims=nt)
        yg_scr[2 * pi] = _pdot(x0, m_ref[pi, 0, 0]) + yc[:, :n_tap]
        yg_scr[2 * pi + 1] = _pdot(x1, m_ref[pi, 0, 1]) + yc[:, n_tap:]
        if tail_rows:
            yc_lo = _pdot_lo_terms(hp_scr[tail, :], *part(wo_ref), dims=nt)
            yg_scr[2 * pi, tail, :] += _pdot_lo_terms(x0[rows - tail_rows:], *part(m_ref, 0)) + yc_lo[:, :n_tap]
            yg_scr[2 * pi + 1, tail, :] += (_pdot_lo_terms(x1[rows - tail_rows:], *part(m_ref, 1))
                                            + yc_lo[:, n_tap:])

    def scatter_tile(t, carry):
        r0 = pl.multiple_of(t * rt, rt)
        for hf in range(2):
            vs = [yg_scr[g, pl.ds(r0, rt), hf * LANES:(hf + 1) * LANES] for g in range(GROUPS_PER_SLAB)]
            outs = _block_transpose8(vs)
            for i in range(half_chunk):
                y_ref[pl.ds(r0 * S5_CHUNK + hf * half_chunk + i, rt, stride=S5_CHUNK), :] = outs[i]
        return carry

    lax.fori_loop(0, rows // rt, scatter_tile, 0)


def _s5_mixer(u, h0_re, h0_im, prep, bsz, t_len, tail):
    m, wsp, wop, a16 = prep
    parts = 2 if tail > 0 else 1
    n_tap = S5_CHUNK * SSM_GROUP
    st = 4 * SSM_STATE
    independent = t_len == S5_CHUNK
    if independent:
        nblk, rows, hrows = 1, bsz, bsz
    else:
        nblk, rows, hrows = bsz, t_len // S5_CHUNK, 1
    assert t_len % S5_CHUNK == 0 and rows % SUBLANES == 0, (bsz, t_len)
    h0p = jnp.concatenate([h0_re.reshape(bsz, N_PAIRS, 2 * SSM_STATE),
                           h0_im.reshape(bsz, N_PAIRS, 2 * SSM_STATE)], axis=2).astype(F32)
    h0p = h0p.transpose(1, 0, 2)[None] if independent else h0p[:, :, None, :]
    pp = PAIRS_PER_SLAB

    def wspec(*shape):
        return pl.BlockSpec((pp,) + shape, lambda q, b: (q,) + (0,) * len(shape))

    frames = rows * S5_CHUNK
    if tail <= 0:
        tail_rows = 0
    elif independent:
        tail_rows = rows
    else:
        tail_rows = min(rows, -(-tail // (S5_CHUNK * SUBLANES)) * SUBLANES)
    y, hf = pl.pallas_call(
        functools.partial(_s5_kernel, rows=rows, independent=independent, tail_rows=tail_rows),
        grid=(N_SLABS, nblk),
        in_specs=[pl.BlockSpec((None, frames, LANES), lambda q, b: (q, b, 0)),
                  pl.BlockSpec((None, pp, hrows, st), lambda q, b: (b, q, 0, 0)),
                  wspec(parts, 2, n_tap, n_tap), wspec(parts, 2 * n_tap, st), wspec(parts, 2 * n_tap, st),
                  wspec(2 * SUBLANES, st)],
        out_specs=[pl.BlockSpec((None, frames, LANES), lambda q, b: (q, b, 0)),
                   pl.BlockSpec((None, pp, hrows, st), lambda q, b: (b, q, 0, 0))],
        out_shape=[jax.ShapeDtypeStruct(u.shape, F32),
                   jax.ShapeDtypeStruct((nblk, N_PAIRS, hrows, st), F32)],
        scratch_shapes=[pltpu.VMEM((GROUPS_PER_SLAB, rows, n_tap), F32),
                        pltpu.VMEM((GROUPS_PER_SLAB, rows, n_tap), F32),
                        pltpu.VMEM((rows, st), F32), pltpu.VMEM((rows, st), F32)],
        compiler_params=_params("parallel", "parallel"),
        name="s5_core",
    )(u, h0p, m, wsp, wop, a16)
    hf = hf[0].transpose(1, 0, 2) if independent else hf[:, :, 0, :]
    hf_re = hf[:, :, :2 * SSM_STATE].reshape(bsz, N_GROUPS, SSM_STATE)
    hf_im = hf[:, :, 2 * SSM_STATE:].reshape(bsz, N_GROUPS, SSM_STATE)
    return y, hf_re, hf_im


def _conv_kernel(v_ref, left_ref, w_ref, b_ref, g_ref, beta_ref, y_ref, cn_ref, vbuf, shifted, *, tm):
    @pl.when(pl.program_id(1) == 0)
    def _():
        vbuf[0:CONV_HALO, :] = left_ref[...]

    vbuf[CONV_HALO:CONV_HALO + tm, :] = v_ref[...]
    first = CONV_HALO - (CONV_WIDTH - 1)
    span = tm + CONV_HALO - SUBLANES
    for r in range(1, SUBLANES):
        shifted[r - 1, 0:span, :] = vbuf[r:r + span, :]
    acc = jnp.zeros((tm, D_CONV), F32)
    for k in range(CONV_WIDTH):
        a, r = divmod(first + k, SUBLANES)
        src = vbuf if r == 0 else shifted.at[r - 1]
        acc = acc + w_ref[k:k + 1, :] * src[a * SUBLANES:a * SUBLANES + tm, :]
    y = acc + b_ref[...]
    mu = jnp.mean(y, axis=-1, keepdims=True)
    yc = y - mu
    var = jnp.mean(yc * yc, axis=-1, keepdims=True)
    yn = yc * lax.rsqrt(var + EPS) * g_ref[...] + beta_ref[...]
    y_ref[...] = yn * jax.nn.sigmoid(yn)
    cn_ref[...] = vbuf[tm + first:tm + CONV_HALO, :]
    vbuf[0:CONV_HALO, :] = vbuf[tm:tm + CONV_HALO, :]


def _conv_mixer(v, left, w, b, g, beta, layer, bsz, t_len, tm):
    left = jnp.pad(left.astype(F32), ((0, 0), (CONV_HALO - (CONV_WIDTH - 1), 0), (0, 0)))
    nt = t_len // tm
    vec = _layer_spec(layer, 1, D_CONV)
    return pl.pallas_call(
        functools.partial(_conv_kernel, tm=tm),
        grid=(bsz, nt),
        in_specs=[pl.BlockSpec((tm, D_CONV), lambda bi, j: (bi * nt + j, 0)),
                  pl.BlockSpec((None, CONV_HALO, D_CONV), lambda bi, j: (bi, 0, 0)),
                  _layer_spec(layer, CONV_WIDTH, D_CONV), vec, vec, vec],
        out_specs=[pl.BlockSpec((tm, D_CONV), lambda bi, j: (bi * nt + j, 0)),
                   pl.BlockSpec((None, CONV_WIDTH - 1, D_CONV), lambda bi, j: (bi, 0, 0))],
        out_shape=[jax.ShapeDtypeStruct((bsz * t_len, D_CONV), F32),
                   jax.ShapeDtypeStruct((bsz, CONV_WIDTH - 1, D_CONV), F32)],
        scratch_shapes=[pltpu.VMEM((tm + CONV_HALO, D_CONV), F32),
                        pltpu.VMEM((SUBLANES - 1, tm + CONV_HALO - SUBLANES, D_CONV), F32)],
        compiler_params=_params("parallel", "arbitrary"),
        name="conv_mixer",
    )(v, left, w, _vec(b), _vec(g), _vec(beta))


def _route(logits):
    lane_i = lax.broadcasted_iota(jnp.int32, logits.shape, 1)
    lane = lane_i.astype(F32)
    group_of_lane = (lane_i >> 3).astype(F32)
    neg = -jnp.inf
    far = float(LANES)
    is_g = (lane_i >= N_EXPERTS) & (lane_i < N_EXPERTS + N_EXPERT_GROUPS)
    gl = jnp.where(is_g, logits, neg)
    g_max = jnp.max(gl, axis=-1, keepdims=True)
    g_lane = jnp.min(jnp.where(gl == g_max, lane, far), axis=-1, keepdims=True)
    g_gate = 1.0 / jnp.sum(jnp.exp(gl - g_max), axis=-1, keepdims=True)
    g_idx = g_lane - float(N_EXPERTS)
    in_group = (lane_i < N_EXPERTS) & (group_of_lane == g_idx)
    el = jnp.where(in_group, logits, neg)
    v1 = jnp.max(el, axis=-1, keepdims=True)
    i1 = jnp.min(jnp.where(el == v1, lane, far), axis=-1, keepdims=True)
    el2 = jnp.where(lane == i1, neg, el)
    v2 = jnp.max(el2, axis=-1, keepdims=True)
    i2 = jnp.min(jnp.where(el2 == v2, lane, far), axis=-1, keepdims=True)
    e2 = jnp.exp(v2 - v1)
    w1 = g_gate / (1.0 + e2)
    w2 = g_gate * e2 / (1.0 + e2)
    return i1, i2, w1, w2


ROUTE_E1, ROUTE_E2, ROUTE_W1, ROUTE_W2, ROUTE_RANK1, ROUTE_RANK2 = range(6)


def _outproj_kernel(x_ref, ys_ref, u_ref, yb_ref, d_ref, wglu_ref, wout_ref, nf_ref, rw_ref, rb_ref, cnt0_ref,
                    x1_ref, hn_ref, route_ref, rt_ref, cnt_ref, wglu_bf, wout_bf, rw_bf, acc_glu, acc_mix,
                    *, tail_tiles):
    @pl.when(pl.program_id(0) == 0)
    def _():
        cnt_ref[...] = cnt0_ref[...]
        rw = rw_ref[...]
        rw_hi = rw.astype(BF16)
        rw_bf[:, 0:LANES] = rw_hi
        rw_bf[:, LANES:] = (rw - rw_hi.astype(F32)).astype(BF16)

    _cast_weight_once(wglu_ref, wglu_bf)
    _cast_weight_once(wout_ref, wout_bf)
    ys = jnp.concatenate([ys_ref[q] for q in range(N_SLABS)], axis=1)
    u = jnp.concatenate([u_ref[q] for q in range(N_SLABS)], axis=1)
    z = jax.nn.gelu(ys + d_ref[...] * u)
    ya = z * jax.nn.sigmoid(_pdot_tail(z, *_parts(wglu_bf), acc_glu, tail_tiles))
    mix = _pdot_tail(jnp.concatenate([ya, yb_ref[...]], axis=1), *_parts(wout_bf), acc_mix, tail_tiles)
    x1 = x_ref[...] + mix
    x1_ref[...] = x1
    hn = _rms(x1, nf_ref[...])
    _store_row_tiles(hn_ref, hn)
    h_hi = hn.astype(BF16)
    h_lo = (hn - h_hi.astype(F32)).astype(BF16)
    hw = jnp.dot(h_hi, rw_bf[...], preferred_element_type=F32)
    logits = (hw[:, :LANES] + hw[:, LANES:]
              + jnp.dot(h_lo, rw_bf[:, 0:LANES], preferred_element_type=F32) + rb_ref[...])
    i1, i2, w1, w2 = _route(logits)
    tm = logits.shape[0]
    lane_i = lax.broadcasted_iota(jnp.int32, logits.shape, 1)
    lane = lane_i.astype(F32)
    picked = jnp.where((lane == i1) | (lane == i2), 1.0, 0.0)
    earlier = (lax.broadcasted_iota(jnp.int32, (tm, tm), 0) > lax.broadcasted_iota(jnp.int32, (tm, tm), 1))
    prefix = _bdot(jnp.where(earlier, 1.0, 0.0), picked.astype(BF16)) + cnt_ref[...]
    rank1 = jnp.sum(jnp.where(lane == i1, prefix, 0.0), axis=-1, keepdims=True)
    rank2 = jnp.sum(jnp.where(lane == i2, prefix, 0.0), axis=-1, keepdims=True)
    cnt_ref[...] += jnp.sum(picked, axis=0, keepdims=True)
    rec = jnp.zeros_like(logits)
    for lane_id, val in ((ROUTE_E1, i1), (ROUTE_E2, i2), (ROUTE_W1, w1), (ROUTE_W2, w2),
                         (ROUTE_RANK1, rank1), (ROUTE_RANK2, rank2)):
        rec = jnp.where(lane_i == lane_id, val, rec)
    route_ref[...] = rec
    rt_ref[...] = rec.T[0:SUBLANES, :]


def _outproj(x, ys, u, yb, d, wglu, wout, nf, rw, rb, counts0, layer, tm, t_len, tail):
    n = x.shape[0]
    precise = tail > 0
    return pl.pallas_call(
        functools.partial(_outproj_kernel, tail_tiles=_stream_tail_tiles(t_len, tm, tail)),
        grid=(n // tm,),
        in_specs=[_row_spec(tm, D_MODEL), _slab_spec(tm), _slab_spec(tm), _row_spec(tm, D_CONV),
                  _layer_spec(layer, 1, D_SSM), _layer_spec(layer, D_SSM, D_SSM),
                  _layer_spec(layer, D_MODEL, D_MODEL), _layer_spec(layer, 1, D_MODEL),
                  _const_spec((D_MODEL, LANES)), _const_spec((1, LANES)), _const_spec((1, LANES))],
        out_specs=[_row_spec(tm, D_MODEL), _row_tile_spec(tm), _row_spec(tm, LANES),
                   pl.BlockSpec((SUBLANES, tm), lambda i: (0, i)), _const_spec((1, LANES))],
        out_shape=[jax.ShapeDtypeStruct((n, D_MODEL), F32), jax.ShapeDtypeStruct(_tiled_rows(n), F32),
                   jax.ShapeDtypeStruct((n, LANES), F32), jax.ShapeDtypeStruct((SUBLANES, n), F32),
                   jax.ShapeDtypeStruct((1, LANES), F32)],
        scratch_shapes=[_weight_scratch(D_SSM, D_SSM, precise), _weight_scratch(D_MODEL, D_MODEL, precise),
                        pltpu.VMEM((D_MODEL, 2 * LANES), BF16),
                        _tail_acc(tm, D_SSM, precise), _tail_acc(tm, D_MODEL, precise)],
        compiler_params=_params("arbitrary"),
        name="outproj_router",
    )(x, ys, u, yb, _vec(d), wglu, wout, _vec(nf), rw, rb, counts0)


PLAN_TILE_LANES = 2 * LANES
PLAN_EXPERT, PLAN_ROWS, PLAN_USED = range(3)


def _moe_plan_kernel(rt_ref, cnt_ref, cnt_first_ref, slot_ref, tile_ref, *, tms):
    cnt = cnt_ref[...]
    padded = jnp.ceil(cnt * (1.0 / tms)) * float(tms)
    r = lax.broadcasted_iota(jnp.int32, (LANES, LANES), 0)
    c = lax.broadcasted_iota(jnp.int32, (LANES, LANES), 1)
    ends = jnp.dot(padded, jnp.where(r <= c, 1.0, 0.0), precision=lax.Precision.HIGHEST,
                   preferred_element_type=F32)
    starts = ends - padded
    rt = rt_ref[...]
    e1, e2 = rt[ROUTE_E1:ROUTE_E1 + 1], rt[ROUTE_E2:ROUTE_E2 + 1]
    s1, s2 = rt[ROUTE_RANK1:ROUTE_RANK1 + 1], rt[ROUTE_RANK2:ROUTE_RANK2 + 1]
    tile = lax.broadcasted_iota(jnp.int32, (1, PLAN_TILE_LANES), 1).astype(F32)
    used = ends[:, N_EXPERTS - 1:N_EXPERTS] * (1.0 / tms)
    pos = jnp.minimum(tile, used - 1.0) * float(tms)
    t_exp = jnp.zeros_like(tile)
    t_fill = jnp.zeros_like(tile)
    for e in range(N_EXPERTS):
        st, en = starts[:, e:e + 1], ends[:, e:e + 1]
        s1 = s1 + jnp.where(e1 == float(e), st, 0.0)
        s2 = s2 + jnp.where(e2 == float(e), st, 0.0)
        mine = (pos >= st) & (pos < en)
        t_exp = t_exp + jnp.where(mine, float(e), 0.0)
        t_fill = t_fill + jnp.where(mine, st + cnt_first_ref[:, e:e + 1], 0.0)
    t_rows = jnp.where(tile < used, jnp.clip(t_fill - pos, 0.0, float(tms)), 0.0)
    slot_ref[...] = jnp.concatenate([s1, s2], axis=0).astype(jnp.int32)
    tile_ref[...] = jnp.concatenate(
        [t_exp, t_rows, jnp.broadcast_to(used, tile.shape), jnp.zeros((SUBLANES - 3, PLAN_TILE_LANES), F32)],
        axis=0).astype(jnp.int32)


def _moe_plan(route_t, counts, counts_first, tms, n_tiles):
    n = route_t.shape[1]
    assert n_tiles <= PLAN_TILE_LANES
    slots, tiles = pl.pallas_call(
        functools.partial(_moe_plan_kernel, tms=tms),
        out_shape=[jax.ShapeDtypeStruct((2, n), jnp.int32),
                   jax.ShapeDtypeStruct((SUBLANES, PLAN_TILE_LANES), jnp.int32)],
        compiler_params=pltpu.CompilerParams(vmem_limit_bytes=VMEM_LIMIT),
        name="moe_plan",
    )(route_t, counts, counts_first)
    return slots, tiles[PLAN_EXPERT, :n_tiles], tiles[PLAN_ROWS, :n_tiles], tiles[PLAN_USED, :1]


DMA_UNROLL = 8


HN_BUFFERS = 3
ZERO_FILL_ROWS = 128


def _dispatch_kernel(tr_ref, slot_ref, hn_ref, *rest, tm, tms, n_tiles, n_steps, fresh):
    xs_ref, zbuf, hbuf, in_sem, out_sem, zsem = rest if fresh else rest[1:]
    i = pl.program_id(0)

    def fetch(t):
        b = lax.rem(t, HN_BUFFERS)
        first = pl.multiple_of(t * (tm * ROW_SUB), tm * ROW_SUB)
        return pltpu.make_async_copy(hn_ref.at[pl.ds(first, tm * ROW_SUB)], hbuf.at[b], in_sem.at[b])

    def drain_scatter(t):
        b = lax.rem(t, HN_BUFFERS)
        for k in range(2):
            pltpu.make_async_copy(hbuf.at[b], xs_ref.at[pl.ds(0, tm * ROW_SUB)], out_sem.at[b]).wait()

    @pl.when(i == 0)
    def _():
        fetch(i).start()

    @pl.when(jnp.logical_and(i == 0, fresh))
    def _():
        zbuf[...] = jnp.zeros_like(zbuf)

        zrows = zbuf.shape[0] // ROW_SUB
        per_tile = tms // zrows

        def unfilled(p):
            return tr_ref[p // per_tile] < (lax.rem(p, per_tile) + 1) * zrows

        def fill(p, carry):
            @pl.when(unfilled(p))
            def _():
                first = pl.multiple_of(p * (zrows * ROW_SUB), zrows * ROW_SUB)
                pltpu.make_async_copy(zbuf, xs_ref.at[pl.ds(first, zrows * ROW_SUB)], zsem).start()
            return carry

        def drain(p, carry):
            @pl.when(unfilled(p))
            def _():
                pltpu.make_async_copy(zbuf, xs_ref.at[pl.ds(0, zrows * ROW_SUB)], zsem).wait()
            return carry

        lax.fori_loop(0, n_tiles * per_tile, fill, 0)
        lax.fori_loop(0, n_tiles * per_tile, drain, 0)

    @pl.when(i + 1 < n_steps)
    def _():
        fetch(i + 1).start()

    fetch(i).wait()
    b = lax.rem(i, HN_BUFFERS)
    rows = hbuf.at[b]

    def issue(r, carry):
        for k in range(2):
            pltpu.make_async_copy(_one_row(rows, r), _one_row(xs_ref, slot_ref[k, r]),
                                  out_sem.at[b]).start(priority=k)
        return carry

    lax.fori_loop(0, tm, issue, 0, unroll=DMA_UNROLL)

    @pl.when(i >= 1)
    def _():
        drain_scatter(i - 1)

    @pl.when(i == n_steps - 1)
    def _():
        drain_scatter(i)


def _dispatch(hn, slots, tile_rows, tms, tm, into=None):
    n = hn.shape[0] // ROW_SUB
    n_tiles = tile_rows.shape[0]
    fresh = into is None
    any_spec = pl.BlockSpec(memory_space=pl.ANY)
    return pl.pallas_call(
        functools.partial(_dispatch_kernel, tm=tm, tms=tms, n_tiles=n_tiles, n_steps=n // tm, fresh=fresh),
        grid_spec=pltpu.PrefetchScalarGridSpec(
            num_scalar_prefetch=1, grid=(n // tm,),
            in_specs=[pl.BlockSpec((2, tm), lambda i, tr: (0, i), memory_space=pltpu.SMEM), any_spec]
                     + ([] if fresh else [any_spec]),
            out_specs=any_spec,
            scratch_shapes=[pltpu.VMEM(_tiled_rows(min(tms, ZERO_FILL_ROWS)), F32),
                            pltpu.VMEM((HN_BUFFERS,) + _tiled_rows(tm), F32),
                            pltpu.SemaphoreType.DMA((HN_BUFFERS,)), pltpu.SemaphoreType.DMA((HN_BUFFERS,)),
                            pltpu.SemaphoreType.DMA]),
        out_shape=jax.ShapeDtypeStruct(_tiled_rows(n_tiles * tms), F32),
        input_output_aliases={} if fresh else {3: 0},
        compiler_params=_params("arbitrary"),
        name="moe_dispatch",
    )(tile_rows, slots, hn, *([] if fresh else [into]))


def _moe_kernel(te_ref, nu_ref, x_ref, wg_ref, wu_ref, wd_ref, y_ref, wg_bf, wu_bf, wd_bf):
    i = pl.program_id(0)
    in_use = i < nu_ref[0]
    new_expert = (i == 0) | (te_ref[i] != te_ref[jnp.maximum(i - 1, 0)])

    @pl.when(in_use & new_expert)
    def _():
        wg_bf[...] = wg_ref[...].astype(BF16)
        wu_bf[...] = wu_ref[...].astype(BF16)
        wd_bf[...] = wd_ref[...].astype(BF16)

    @pl.when(in_use)
    def _():
        h = _load_row_tiles(x_ref).astype(BF16)
        hg = jnp.dot(h, wg_bf[...], preferred_element_type=F32)
        hu = jnp.dot(h, wu_bf[...], preferred_element_type=F32)
        _store_row_tiles(y_ref, _bdot(hg * jax.nn.sigmoid(hg) * hu, wd_bf[...]))

    @pl.when(jnp.logical_not(in_use))
    def _():
        y_ref[...] = jnp.zeros_like(y_ref)


def _moe(xs, tile_expert, n_used, wg, wu, wd, layer, tms):
    n_slots = xs.shape[0] // ROW_SUB
    rows = pl.BlockSpec(_tiled_rows(tms), lambda i, te, nu: (jnp.minimum(i, nu[0] - 1), 0))
    out_rows = pl.BlockSpec(_tiled_rows(tms), lambda i, te, nu: (i, 0))

    def wspec(a, b):
        return pl.BlockSpec((None, None, a, b), lambda i, te, nu: (layer, te[i], 0, 0))

    return pl.pallas_call(
        _moe_kernel,
        grid_spec=pltpu.PrefetchScalarGridSpec(
            num_scalar_prefetch=2, grid=(n_slots // tms,),
            in_specs=[rows, wspec(D_MODEL, D_EXPERT), wspec(D_MODEL, D_EXPERT), wspec(D_EXPERT, D_MODEL)],
            out_specs=out_rows,
            scratch_shapes=[pltpu.VMEM((D_MODEL, D_EXPERT), BF16), pltpu.VMEM((D_MODEL, D_EXPERT), BF16),
                            pltpu.VMEM((D_EXPERT, D_MODEL), BF16)]),
        out_shape=jax.ShapeDtypeStruct(xs.shape, F32),
        compiler_params=_params("arbitrary"),
        name="moe",
    )(tile_expert, n_used, xs, wg, wu, wd)


def _ple_kernel(slot_ref, next_slot_ref, x_ref, route_ref, p_ref, np_ref, wple_ref, wgate_ref, nfin_ref,
                ys_ref, o_ref, ybuf, sem, wple_bf, wgate_bf, *, tm, n_steps, final):
    i = pl.program_id(0)
    _cast_weight_once(wple_ref, wple_bf)
    _cast_weight_once(wgate_ref, wgate_bf)

    def gather(slots, b):
        def issue(r, carry):
            for k in range(2):
                pltpu.make_async_copy(_one_row(ys_ref, slots[k, r]), _one_row(ybuf.at[b, k], r),
                                      sem.at[b]).start(priority=k)
            return carry

        lax.fori_loop(0, tm, issue, 0, unroll=DMA_UNROLL)

    @pl.when(i == 0)
    def _():
        gather(slot_ref, 0)

    @pl.when(i + 1 < n_steps)
    def _():
        gather(next_slot_ref, lax.rem(i + 1, 2))

    pe = _pdot(p_ref[...], *_parts(wple_bf))
    b = lax.rem(i, 2)
    for k in range(2):
        pltpu.make_async_copy(ys_ref.at[pl.ds(0, tm * ROW_SUB)], ybuf.at[b, k], sem.at[b]).wait()
    route = route_ref[...]
    x = (x_ref[...] + route[:, ROUTE_W1:ROUTE_W1 + 1] * _load_row_tiles(ybuf.at[b, 0])
         + route[:, ROUTE_W2:ROUTE_W2 + 1] * _load_row_tiles(ybuf.at[b, 1]))
    gate = jax.nn.sigmoid(_pdot(_rms(x, np_ref[...]), *_parts(wgate_bf)))
    out = x + pe * gate
    if final:
        out = _rms(out, nfin_ref[...])
    o_ref[...] = out


def _ple(x, route, slots, ys, p, npl, wple, wgate, nfin, layer, tm, final):
    n = x.shape[0]
    n_steps = n // tm
    return pl.pallas_call(
        functools.partial(_ple_kernel, tm=tm, n_steps=n_steps, final=final),
        grid=(n_steps,),
        in_specs=[pl.BlockSpec((2, tm), lambda i: (0, i), memory_space=pltpu.SMEM),
                  pl.BlockSpec((2, tm), lambda i: (0, jnp.minimum(i + 1, n_steps - 1)),
                               memory_space=pltpu.SMEM),
                  _row_spec(tm, D_MODEL), _row_spec(tm, LANES),
                  pl.BlockSpec((None, tm, D_PLE), lambda i: (layer, i, 0)),
                  _layer_spec(layer, 1, D_MODEL), _layer_spec(layer, D_PLE, D_MODEL),
                  _layer_spec(layer, D_MODEL, D_MODEL), _const_spec((1, D_MODEL)),
                  pl.BlockSpec(memory_space=pl.ANY)],
        out_specs=_row_spec(tm, D_MODEL),
        out_shape=jax.ShapeDtypeStruct((n, D_MODEL), F32),
        scratch_shapes=[pltpu.VMEM((2, 2) + _tiled_rows(tm), F32), pltpu.SemaphoreType.DMA((2,)),
                        _weight_scratch(D_PLE, D_MODEL, False), _weight_scratch(D_MODEL, D_MODEL, False)],
        compiler_params=_params("arbitrary"),
        name="combine_ple",
    )(slots, slots, x, route, p, _vec(npl), wple, wgate, nfin.reshape(1, D_MODEL), ys)


def _mixers(x, h0_re, h0_im, conv_left, w, s5, router_w, router_b, counts0, layer, bsz, t_len, tm, tm_conv,
            tail):
    u, v = _inproj(x, w["norm_mix"], w["w_in"], layer, min(2 * tm, x.shape[0]), t_len, tail)
    ys, hf_re, hf_im = _s5_mixer(u, h0_re, h0_im, s5, bsz, t_len, tail)
    yb, conv_new = _conv_mixer(v, conv_left, w["conv_w"], w["conv_b"], w["conv_ln_g"], w["conv_ln_b"],
                               layer, bsz, t_len, tm_conv)
    x1, hn, route, route_t, counts = _outproj(x, ys, u, yb, w["ssm_d"], w["w_ssm_glu"], w["w_out"],
                                              w["norm_ffn"], router_w, router_b, counts0, layer, tm, t_len, tail)
    return dict(x1=x1, hn=hn, route=route, route_t=route_t, counts=counts, state=(hf_re, hf_im, conv_new))


def _moe_and_ple(sets, ps, tms_rows, w, layer, tms, final):
    sizes = [s["x1"].shape[0] for s in sets]
    n_tiles = 2 * sum(sizes) // tms + N_EXPERTS
    route_t = jnp.concatenate([s["route_t"] for s in sets], axis=1)
    slots, tile_expert, tile_rows, n_used = _moe_plan(route_t, sets[-1]["counts"], sets[0]["counts"], tms,
                                                     n_tiles)
    starts = [sum(sizes[:j]) for j in range(len(sets))]
    set_slots = [slots[:, a:a + n] for a, n in zip(starts, sizes)]
    xsort = None
    for s, sl, tm in zip(sets, set_slots, tms_rows):
        xsort = _dispatch(s["hn"], sl, tile_rows, tms, tm, into=xsort)
    ysort = _moe(xsort, tile_expert, n_used, w["expert_w_gate"], w["expert_w_up"], w["expert_w_down"],
                 layer, tms)
    return [_ple(s["x1"], s["route"], sl, ysort, p, w["norm_ple"], w["ple_w"], w["ple_gate_w"],
                 w["norm_final"], layer, tm, final)
            for s, sl, p, tm in zip(sets, set_slots, ps, tms_rows)]


def kernel(x_prompt, x_sample, p_prompt, p_sample, state_ssm_re, state_ssm_im, cache_conv, norm_mix, w_in, ssm_a_re, ssm_a_im, ssm_b_re, ssm_b_im, ssm_c_re, ssm_c_im, ssm_d, ssm_log_dt, w_ssm_glu, conv_w, conv_b, conv_ln_g, conv_ln_b, w_out, norm_ffn, router_group_w, router_group_b, router_expert_w, router_expert_b, expert_w_gate, expert_w_up, expert_w_down, norm_ple, ple_w, ple_gate_w, norm_final):
    depth = w_in.shape[0]
    bp, tp, _ = x_prompt.shape
    bs, ts, _ = x_sample.shape
    xp = x_prompt.reshape(bp * tp, D_MODEL)
    xs = x_sample.reshape(bs * ts, D_MODEL)
    pp = p_prompt.reshape(depth, bp * tp, D_PLE)
    ps = p_sample.reshape(depth, bs * ts, D_PLE)
    zero_state = jnp.zeros((bp, N_GROUPS, SSM_STATE), F32)
    zero_conv = jnp.zeros((bp, CONV_WIDTH - 1, D_CONV), F32)
    pad_lanes = LANES - N_EXPERTS - N_EXPERT_GROUPS
    w = {"norm_mix": norm_mix, "w_in": w_in, "ssm_d": ssm_d, "w_ssm_glu": w_ssm_glu, "conv_w": conv_w,
         "conv_b": conv_b, "conv_ln_g": conv_ln_g, "conv_ln_b": conv_ln_b, "w_out": w_out,
         "norm_ffn": norm_ffn, "expert_w_gate": expert_w_gate, "expert_w_up": expert_w_up,
         "expert_w_down": expert_w_down, "norm_ple": norm_ple, "ple_w": ple_w, "ple_gate_w": ple_gate_w,
         "norm_final": norm_final}
    no_picks = jnp.zeros((1, LANES), F32)
    outs = {k: [] for k in ("pr_re", "pr_im", "pr_conv", "sm_re", "sm_im", "sm_conv")}
    for i in range(depth):
        s5 = _s5_prep(ssm_a_re[i], ssm_a_im[i], ssm_log_dt[i], ssm_b_re[i], ssm_b_im[i], ssm_c_re[i],
                      ssm_c_im[i])
        router_w = jnp.pad(jnp.concatenate([router_expert_w[i], router_group_w[i]], axis=1),
                           ((0, 0), (0, pad_lanes)))
        router_b = jnp.pad(jnp.concatenate([router_expert_b[i], router_group_b[i]]),
                           (0, pad_lanes)).reshape(1, LANES)
        final = i == depth - 1
        tail = 0 if final else PRECISE_TAIL
        mp = _mixers(xp, zero_state, zero_state, zero_conv, w, s5, router_w, router_b, no_picks, i,
                     bp, tp, tm=TOKEN_TILE, tm_conv=TOKEN_TILE, tail=tail)
        ms = _mixers(xs, state_ssm_re[i], state_ssm_im[i], cache_conv[i], w, s5, router_w, router_b,
                     mp["counts"], i, bs, ts, tm=bs * ts, tm_conv=ts, tail=tail)
        xp, xs = _moe_and_ple([mp, ms], [pp, ps], [TOKEN_TILE, bs * ts], w, i, SLOT_TILE, final)
        for key, val in zip(("pr_re", "pr_im", "pr_conv"), mp["state"]):
            outs[key].append(val)
        for key, val in zip(("sm_re", "sm_im", "sm_conv"), ms["state"]):
            outs[key].append(val)
    return (xp.reshape(bp, tp, D_MODEL), xs.reshape(bs, ts, D_MODEL),
            jnp.stack(outs["pr_re"]), jnp.stack(outs["pr_im"]), jnp.stack(outs["pr_conv"]),
            jnp.stack(outs["sm_re"]), jnp.stack(outs["sm_im"]), jnp.stack(outs["sm_conv"]))
```

```python
import functools

import jax
import jax.numpy as jnp
from jax import lax
from jax.experimental import pallas as pl
from jax.experimental.pallas import tpu as pltpu

F32 = jnp.float32
BF16 = jnp.bfloat16

D_MODEL = 1024
D_SSM = 512
SSM_GROUP = 16
N_GROUPS = D_SSM // SSM_GROUP
N_PAIRS = N_GROUPS // 2
SSM_STATE = 64
D_CONV = 512
CONV_WIDTH = 31
CONV_HALO = 32
N_EXPERT_GROUPS = 4
EXPERTS_PER_GROUP = 8
N_EXPERTS = 32
D_EXPERT = 256
D_PLE = 256
EPS = 1e-6
S5_CHUNK = 16
S5_RELAYOUT_ROWS = 128
S5_SCAN_UNROLL = 4
LANES = 128
SUBLANES = 8
N_SLABS = D_SSM // LANES
GROUPS_PER_SLAB = LANES // SSM_GROUP
PAIRS_PER_SLAB = GROUPS_PER_SLAB // 2
VMEM_LIMIT = 56 * 1024 * 1024
TOKEN_TILE = 512
SLOT_TILE = 512
PRECISE_TAIL = 1024


def _log2(n):
    assert n & (n - 1) == 0, n
    return n.bit_length() - 1


def _params(*sem):
    return pltpu.CompilerParams(dimension_semantics=sem, vmem_limit_bytes=VMEM_LIMIT)


def _rms(x, g):
    return x * lax.rsqrt(jnp.mean(x * x, axis=-1, keepdims=True) + EPS) * g


def _bdot(a, b):
    return jnp.dot(a.astype(BF16), b, preferred_element_type=F32)


def _split_bf16(a):
    hi = a.astype(BF16)
    return hi, (a - hi.astype(F32)).astype(BF16)


def _pdot(a, w_hi, w_lo=None, dims=(((1,), (0,)), ((), ()))):
    def mm(x, w):
        return lax.dot_general(x, w, dims, preferred_element_type=F32)

    if w_lo is None:
        return mm(a.astype(BF16), w_hi)
    a_hi, a_lo = _split_bf16(a)
    return mm(a_hi, w_hi) + (mm(a_hi, w_lo) + mm(a_lo, w_hi))


def _pdot_lo_terms(a, w_hi, w_lo, dims=(((1,), (0,)), ((), ()))):
    a_hi, a_lo = _split_bf16(a)
    return (lax.dot_general(a_hi, w_lo, dims, preferred_element_type=F32)
            + lax.dot_general(a_lo, w_hi, dims, preferred_element_type=F32))


def _stream_tail_tiles(t_len, tm, tail):
    tiles = max(t_len // tm, 1)
    return tiles, (max(t_len - tail, 0) // tm if tm < t_len else 0)


def _pdot_tail(a, w_hi, w_lo, acc_ref, tail_tiles):
    if w_lo is None:
        return _pdot(a, w_hi)
    tiles, first = tail_tiles
    if first == 0:
        return _pdot(a, w_hi, w_lo)
    acc_ref[...] = _pdot(a, w_hi)

    @pl.when(lax.rem(pl.program_id(0), tiles) >= first)
    def _():
        acc_ref[...] += _pdot_lo_terms(a, w_hi, w_lo)

    return acc_ref[...]


def _parts(wbf_ref, rows=slice(None)):
    return wbf_ref[0, rows, :], (wbf_ref[1, rows, :] if wbf_ref.shape[0] == 2 else None)


def _row_spec(tm, width):
    return pl.BlockSpec((tm, width), lambda i: (i, 0))


ROW_SUB = D_MODEL // LANES


def _tiled_rows(n):
    return (n * ROW_SUB, LANES)


def _row_tile_spec(tm):
    return pl.BlockSpec(_tiled_rows(tm), lambda i, *_: (i, 0))


def _one_row(ref, r):
    return ref.at[pl.ds(pl.multiple_of(r * ROW_SUB, ROW_SUB), ROW_SUB)]


def _store_row_tiles(ref, rows):
    n = rows.shape[0]
    for s in range(ROW_SUB):
        ref[pl.ds(s, n, stride=ROW_SUB), :] = rows[:, s * LANES:(s + 1) * LANES]


def _load_row_tiles(ref):
    n = ref.shape[0] // ROW_SUB
    return jnp.concatenate([ref[pl.ds(s, n, stride=ROW_SUB), :] for s in range(ROW_SUB)], axis=1)


def _slab_spec(tm):
    return pl.BlockSpec((N_SLABS, tm, LANES), lambda i: (0, i, 0))


def _const_spec(shape):
    return pl.BlockSpec(shape, lambda i: (0,) * len(shape))


def _layer_spec(layer, *shape):
    return pl.BlockSpec((None,) + shape, lambda *_: (layer,) + (0,) * len(shape))


def _vec(stacked):
    return stacked.reshape(stacked.shape[0], 1, stacked.shape[1])


def _cast_weight_once(w_ref, wbf_ref):
    @pl.when(pl.program_id(0) == 0)
    def _():
        w = w_ref[...]
        hi = w.astype(BF16)
        wbf_ref[0] = hi
        if wbf_ref.shape[0] == 2:
            wbf_ref[1] = (w - hi.astype(F32)).astype(BF16)


def _weight_scratch(k, n, precise):
    return pltpu.VMEM((2 if precise else 1, k, n), BF16)


def _tail_acc(rows, n, precise):
    return pltpu.VMEM((rows, n) if precise else (SUBLANES, LANES), F32)


def _inproj_kernel(x_ref, g_ref, w_ref, u_ref, v_ref, wbf, acc, *, tail_tiles):
    _cast_weight_once(w_ref, wbf)
    hn = _rms(x_ref[...], g_ref[...])
    proj = _pdot_tail(hn, *_parts(wbf), acc, tail_tiles)
    for q in range(N_SLABS):
        u_ref[q] = proj[:, q * LANES:(q + 1) * LANES]
    v_ref[...] = proj[:, D_SSM:D_SSM + D_CONV] * jax.nn.sigmoid(proj[:, D_SSM + D_CONV:])


def _inproj(x, g, w, layer, tm, t_len, tail):
    n = x.shape[0]
    d_in = w.shape[2]
    precise = tail > 0
    return pl.pallas_call(
        functools.partial(_inproj_kernel, tail_tiles=_stream_tail_tiles(t_len, tm, tail)),
        grid=(n // tm,),
        in_specs=[_row_spec(tm, D_MODEL), _layer_spec(layer, 1, D_MODEL), _layer_spec(layer, D_MODEL, d_in)],
        out_specs=[_slab_spec(tm), _row_spec(tm, D_CONV)],
        out_shape=[jax.ShapeDtypeStruct((N_SLABS, n, LANES), F32), jax.ShapeDtypeStruct((n, D_CONV), F32)],
        scratch_shapes=[_weight_scratch(D_MODEL, d_in, precise), _tail_acc(tm, d_in, precise)],
        compiler_params=_params("arbitrary"),
        name="inproj",
    )(x, _vec(g), w)


def _s5_prep_kernel(ar_ref, ai_ref, ldt_ref, bre_ref, bim_ref, cre_ref, cim_ref,
                    m_ref, ws_ref, wot_ref, atab_ref, wt_re, wt_im, br_re, br_im):
    n_tap = S5_CHUNK * SSM_GROUP
    st = 4 * SSM_STATE
    nt = (((1,), (1,)), ((), ()))
    hi = lax.Precision.HIGHEST
    lane = lax.broadcasted_iota(jnp.int32, (SSM_GROUP, n_tap), 1)
    ws_rows, wot_rows, atab = [], [], jnp.zeros((2 * SUBLANES, st), F32)
    for gi in range(2):
        ar, ai = ar_ref[gi], ai_ref[gi]
        dt = jnp.exp(ldt_ref[gi])
        k = lax.broadcasted_iota(jnp.int32, (S5_CHUNK + SUBLANES, SSM_STATE), 0).astype(F32)
        mag = jnp.exp(k * (dt * ar))
        ang = k * (dt * ai)
        p_re, p_im = mag * jnp.cos(ang), mag * jnp.sin(ang)
        inv = 1.0 / (ar * ar + ai * ai)
        ab_re, ab_im = p_re[1:2], p_im[1:2]
        ia_re, ia_im = ar * inv, -ai * inv
        coef_re = (ab_re - 1.0) * ia_re - ab_im * ia_im
        coef_im = (ab_re - 1.0) * ia_im + ab_im * ia_re
        bre, bim = bre_ref[gi], bim_ref[gi]
        bb_re = coef_re * bre - coef_im * bim
        bb_im = coef_re * bim + coef_im * bre
        cre, cim = cre_ref[gi], cim_ref[gi]
        for kk in range(S5_CHUNK + 1):
            pr, pi = p_re[kk:kk + 1], p_im[kk:kk + 1]
            rows = slice(kk * SSM_GROUP, (kk + 1) * SSM_GROUP)
            wt_re[rows, :] = pr * cre - pi * cim
            wt_im[rows, :] = -pi * cre - pr * cim
            if kk < S5_CHUNK:
                back = slice((S5_CHUNK - 1 - kk) * SSM_GROUP, (S5_CHUNK - kk) * SSM_GROUP)
                br_re[back, :] = pr * bb_re - pi * bb_im
                br_im[back, :] = pi * bb_re + pr * bb_im
        kcat = (lax.dot_general(bb_re, wt_re[0:n_tap, :], nt, precision=hi, preferred_element_type=F32)
                + lax.dot_general(bb_im, wt_im[0:n_tap, :], nt, precision=hi, preferred_element_type=F32))
        for s in range(S5_CHUNK):
            shifted = kcat if s == 0 else pltpu.roll(kcat, s * SSM_GROUP, 1)
            rows = slice(s * SSM_GROUP, (s + 1) * SSM_GROUP)
            m_ref[0, gi, rows, :], m_ref[1, gi, rows, :] = _split_bf16(
                jnp.where(lane >= s * SSM_GROUP, shifted, 0.0))
        def place(v_re, v_im):
            zero = jnp.zeros_like(v_re)
            parts = [v_re, zero, v_im, zero] if gi == 0 else [zero, v_re, zero, v_im]
            return jnp.concatenate(parts, axis=1)

        ws_rows.append(place(br_re[...], br_im[...]))
        wot_rows.append(place(wt_re[SSM_GROUP:, :], wt_im[SSM_GROUP:, :]))
        kc = float(S5_CHUNK) * lax.broadcasted_iota(jnp.int32, (2 * SUBLANES, SSM_STATE), 0).astype(F32)
        magc = jnp.exp(kc * (dt * ar))
        angc = kc * (dt * ai)
        atab = atab + place(magc * jnp.cos(angc), magc * jnp.sin(angc))
    ws_ref[0], ws_ref[1] = _split_bf16(jnp.concatenate(ws_rows, axis=0))
    wot_ref[0], wot_ref[1] = _split_bf16(jnp.concatenate(wot_rows, axis=0))
    atab_ref[...] = atab


def _s5_prep(a_re, a_im, log_dt, b_re, b_im, c_re, c_im):
    p, n, c = N_PAIRS, SSM_STATE, SSM_GROUP
    n_tap = S5_CHUNK * c
    st = 4 * n

    def pspec(*shape):
        return pl.BlockSpec((None,) + shape, lambda i: (i,) + (0,) * len(shape))

    def pairs(a, *shape):
        return a.reshape((p, 2) + shape)

    return pl.pallas_call(
        _s5_prep_kernel,
        grid=(p,),
        in_specs=[pspec(2, 1, n), pspec(2, 1, n), pspec(2, 1, 1), pspec(2, c, n), pspec(2, c, n),
                  pspec(2, c, n), pspec(2, c, n)],
        out_specs=[pspec(2, 2, n_tap, n_tap), pspec(2, 2 * n_tap, st), pspec(2, 2 * n_tap, st),
                   pspec(2 * SUBLANES, st)],
        out_shape=[jax.ShapeDtypeStruct((p, 2, 2, n_tap, n_tap), BF16),
                   jax.ShapeDtypeStruct((p, 2, 2 * n_tap, st), BF16),
                   jax.ShapeDtypeStruct((p, 2, 2 * n_tap, st), BF16),
                   jax.ShapeDtypeStruct((p, 2 * SUBLANES, st), F32)],
        scratch_shapes=[pltpu.VMEM((n_tap + c, n), F32), pltpu.VMEM((n_tap + c, n), F32),
                        pltpu.VMEM((n_tap, n), F32), pltpu.VMEM((n_tap, n), F32)],
        compiler_params=_params("parallel"),
        name="s5_prep",
    )(pairs(a_re, 1, n), pairs(a_im, 1, n), pairs(log_dt, 1, 1),
      pairs(jnp.swapaxes(b_re, 1, 2), c, n), pairs(jnp.swapaxes(b_im, 1, 2), c, n),
      pairs(c_re, c, n), pairs(c_im, c, n))


def _block_transpose8(vs):
    lane = lax.broadcasted_iota(jnp.int32, vs[0].shape, 1)
    blk = lane >> _log2(SSM_GROUP)
    for d in (GROUPS_PER_SLAB >> s for s in range(1, _log2(GROUPS_PER_SLAB) + 1)):
        keep = (blk & d) == 0
        new = list(vs)
        for i in range(GROUPS_PER_SLAB):
            if i & d == 0:
                a, b = vs[i], vs[i + d]
                new[i] = jnp.where(keep, a, pltpu.roll(b, d * SSM_GROUP, 1))
                new[i + d] = jnp.where(keep, pltpu.roll(a, LANES - d * SSM_GROUP, 1), b)
        vs = new
    return vs


def _cmul(ar, ai, xr, xi):
    return ar * xr - ai * xi, ar * xi + ai * xr


def _s5_kernel(u_ref, h0_ref, m_ref, ws_ref, wo_ref, a_ref, y_ref, hf_ref, x_scr, yg_scr, s_scr, hp_scr,
               *, rows, independent, tail_rows):
    half = 2 * SSM_STATE
    tail = pl.ds(rows - tail_rows, tail_rows)
    rt = min(rows, S5_RELAYOUT_ROWS)
    half_chunk = S5_CHUNK // 2

    def gather_tile(t, carry):
        r0 = pl.multiple_of(t * rt, rt)
        for hf in range(2):
            vs = [u_ref[pl.ds(r0 * S5_CHUNK + hf * half_chunk + i, rt, stride=S5_CHUNK), :]
                  for i in range(half_chunk)]
            outs = _block_transpose8(vs)
            for g in range(GROUPS_PER_SLAB):
                x_scr[g, pl.ds(r0, rt), hf * LANES:(hf + 1) * LANES] = outs[g]
        return carry

    lax.fori_loop(0, rows // rt, gather_tile, 0)

    row = lax.broadcasted_iota(jnp.int32, (SUBLANES, half), 0)
    n_tap = S5_CHUNK * SSM_GROUP
    for pi in range(PAIRS_PER_SLAB):
        def part(w_ref, *idx):
            return w_ref[(pi, 0) + idx], w_ref[(pi, 1) + idx]

        x0 = x_scr[2 * pi]
        x1 = x_scr[2 * pi + 1]
        x01 = jnp.concatenate([x0, x1], axis=1)
        s_scr[...] = _pdot(x01, ws_ref[pi, 0])
        if tail_rows:
            s_scr[tail, :] += _pdot_lo_terms(x01[rows - tail_rows:], *part(ws_ref))
        ap = a_ref[pi]
        h0 = h0_ref[pi]
        if independent:
            hp_scr[...] = h0
            s = s_scr[...]
            n_re, n_im = _cmul(ap[1:2, :half], ap[1:2, half:], h0[:, :half], h0[:, half:])
            hf_ref[pi] = jnp.concatenate([n_re + s[:, :half], n_im + s[:, half:]], axis=1)
        else:
            pw_re, pw_im = ap[0:SUBLANES, :half], ap[0:SUBLANES, half:]

            def scan_tile(t, carry):
                h_re, h_im = carry
                r0 = pl.multiple_of(t * SUBLANES, SUBLANES)
                s = s_scr[pl.ds(r0, SUBLANES), :]
                t_re, t_im = s[:, :half], s[:, half:]
                for d in (1, 2, 4):
                    sh_re = jnp.where(row >= d, pltpu.roll(t_re, d, 0), 0.0)
                    sh_im = jnp.where(row >= d, pltpu.roll(t_im, d, 0), 0.0)
                    m_re, m_im = _cmul(ap[d:d + 1, :half], ap[d:d + 1, half:], sh_re, sh_im)
                    t_re, t_im = t_re + m_re, t_im + m_im
                e_re = jnp.where(row >= 1, pltpu.roll(t_re, 1, 0), 0.0)
                e_im = jnp.where(row >= 1, pltpu.roll(t_im, 1, 0), 0.0)
                c_re, c_im = _cmul(pw_re, pw_im, h_re, h_im)
                hp_scr[pl.ds(r0, SUBLANES), :] = jnp.concatenate([e_re + c_re, e_im + c_im], axis=1)
                o_re, o_im = _cmul(ap[SUBLANES:SUBLANES + 1, :half], ap[SUBLANES:SUBLANES + 1, half:],
                                   h_re, h_im)
                last = SUBLANES - 1
                n_re = jnp.broadcast_to(t_re[last:last + 1], h_re.shape) + o_re
                n_im = jnp.broadcast_to(t_im[last:last + 1], h_im.shape) + o_im
                return n_re, n_im

            init = (jnp.broadcast_to(h0[:, :half], (SUBLANES, half)),
                    jnp.broadcast_to(h0[:, half:], (SUBLANES, half)))
            h_re, h_im = lax.fori_loop(0, rows // SUBLANES, scan_tile, init, unroll=S5_SCAN_UNROLL)
            hf_ref[pi] = jnp.concatenate([h_re[0:1], h_im[0:1]], axis=1)
        nt = (((1,), (1,)), ((), ()))
        yc = _pdot(hp_scr[...], wo_ref[pi, 0], dims=nt)
        yg_scr[2 * pi] = _pdot(x0, m_ref[pi, 0, 0]) + yc[:, :n_tap]
        yg_scr[2 * pi + 1] = _pdot(x1, m_ref[pi, 0, 1]) + yc[:, n_tap:]
        if tail_rows:
            yc_lo = _pdot_lo_terms(hp_scr[tail, :], *part(wo_ref), dims=nt)
            yg_scr[2 * pi, tail, :] += _pdot_lo_terms(x0[rows - tail_rows:], *part(m_ref, 0)) + yc_lo[:, :n_tap]
            yg_scr[2 * pi + 1, tail, :] += (_pdot_lo_terms(x1[rows - tail_rows:], *part(m_ref, 1))
                                            + yc_lo[:, n_tap:])

    def scatter_tile(t, carry):
        r0 = pl.multiple_of(t * rt, rt)
        for hf in range(2):
            vs = [yg_scr[g, pl.ds(r0, rt), hf * LANES:(hf + 1) * LANES] for g in range(GROUPS_PER_SLAB)]
            outs = _block_transpose8(vs)
            for i in range(half_chunk):
                y_ref[pl.ds(r0 * S5_CHUNK + hf * half_chunk + i, rt, stride=S5_CHUNK), :] = outs[i]
        return carry

    lax.fori_loop(0, rows // rt, scatter_tile, 0)


def _s5_mixer(u, h0_re, h0_im, prep, bsz, t_len, tail):
    m, wsp, wop, a16 = prep
    parts = 2 if tail > 0 else 1
    n_tap = S5_CHUNK * SSM_GROUP
    st = 4 * SSM_STATE
    independent = t_len == S5_CHUNK
    if independent:
        nblk, rows, hrows = 1, bsz, bsz
    else:
        nblk, rows, hrows = bsz, t_len // S5_CHUNK, 1
    assert t_len % S5_CHUNK == 0 and rows % SUBLANES == 0, (bsz, t_len)
    h0p = jnp.concatenate([h0_re.reshape(bsz, N_PAIRS, 2 * SSM_STATE),
                           h0_im.reshape(bsz, N_PAIRS, 2 * SSM_STATE)], axis=2).astype(F32)
    h0p = h0p.transpose(1, 0, 2)[None] if independent else h0p[:, :, None, :]
    pp = PAIRS_PER_SLAB

    def wspec(*shape):
        return pl.BlockSpec((pp,) + shape, lambda q, b: (q,) + (0,) * len(shape))

    frames = rows * S5_CHUNK
    if tail <= 0:
        tail_rows = 0
    elif independent:
        tail_rows = rows
    else:
        tail_rows = min(rows, -(-tail // (S5_CHUNK * SUBLANES)) * SUBLANES)
    y, hf = pl.pallas_call(
        functools.partial(_s5_kernel, rows=rows, independent=independent, tail_rows=tail_rows),
        grid=(N_SLABS, nblk),
        in_specs=[pl.BlockSpec((None, frames, LANES), lambda q, b: (q, b, 0)),
                  pl.BlockSpec((None, pp, hrows, st), lambda q, b: (b, q, 0, 0)),
                  wspec(parts, 2, n_tap, n_tap), wspec(parts, 2 * n_tap, st), wspec(parts, 2 * n_tap, st),
                  wspec(2 * SUBLANES, st)],
        out_specs=[pl.BlockSpec((None, frames, LANES), lambda q, b: (q, b, 0)),
                   pl.BlockSpec((None, pp, hrows, st), lambda q, b: (b, q, 0, 0))],
        out_shape=[jax.ShapeDtypeStruct(u.shape, F32),
                   jax.ShapeDtypeStruct((nblk, N_PAIRS, hrows, st), F32)],
        scratch_shapes=[pltpu.VMEM((GROUPS_PER_SLAB, rows, n_tap), F32),
                        pltpu.VMEM((GROUPS_PER_SLAB, rows, n_tap), F32),
                        pltpu.VMEM((rows, st), F32), pltpu.VMEM((rows, st), F32)],
        compiler_params=_params("parallel", "parallel"),
        name="s5_core",
    )(u, h0p, m, wsp, wop, a16)
    hf = hf[0].transpose(1, 0, 2) if independent else hf[:, :, 0, :]
    hf_re = hf[:, :, :2 * SSM_STATE].reshape(bsz, N_GROUPS, SSM_STATE)
    hf_im = hf[:, :, 2 * SSM_STATE:].reshape(bsz, N_GROUPS, SSM_STATE)
    return y, hf_re, hf_im


def _conv_kernel(v_ref, left_ref, w_ref, b_ref, g_ref, beta_ref, y_ref, cn_ref, vbuf, shifted, *, tm):
    @pl.when(pl.program_id(1) == 0)
    def _():
        vbuf[0:CONV_HALO, :] = left_ref[...]

    vbuf[CONV_HALO:CONV_HALO + tm, :] = v_ref[...]
    first = CONV_HALO - (CONV_WIDTH - 1)
    span = tm + CONV_HALO - SUBLANES
    for r in range(1, SUBLANES):
        shifted[r - 1, 0:span, :] = vbuf[r:r + span, :]
    acc = jnp.zeros((tm, D_CONV), F32)
    for k in range(CONV_WIDTH):
        a, r = divmod(first + k, SUBLANES)
        src = vbuf if r == 0 else shifted.at[r - 1]
        acc = acc + w_ref[k:k + 1, :] * src[a * SUBLANES:a * SUBLANES + tm, :]
    y = acc + b_ref[...]
    mu = jnp.mean(y, axis=-1, keepdims=True)
    yc = y - mu
    var = jnp.mean(yc * yc, axis=-1, keepdims=True)
    yn = yc * lax.rsqrt(var + EPS) * g_ref[...] + beta_ref[...]
    y_ref[...] = yn * jax.nn.sigmoid(yn)
    cn_ref[...] = vbuf[tm + first:tm + CONV_HALO, :]
    vbuf[0:CONV_HALO, :] = vbuf[tm:tm + CONV_HALO, :]


def _conv_mixer(v, left, w, b, g, beta, layer, bsz, t_len, tm):
    left = jnp.pad(left.astype(F32), ((0, 0), (CONV_HALO - (CONV_WIDTH - 1), 0), (0, 0)))
    nt = t_len // tm
    vec = _layer_spec(layer, 1, D_CONV)
    return pl.pallas_call(
        functools.partial(_conv_kernel, tm=tm),
        grid=(bsz, nt),
        in_specs=[pl.BlockSpec((tm, D_CONV), lambda bi, j: (bi * nt + j, 0)),
                  pl.BlockSpec((None, CONV_HALO, D_CONV), lambda bi, j: (bi, 0, 0)),
                  _layer_spec(layer, CONV_WIDTH, D_CONV), vec, vec, vec],
        out_specs=[pl.BlockSpec((tm, D_CONV), lambda bi, j: (bi * nt + j, 0)),
                   pl.BlockSpec((None, CONV_WIDTH - 1, D_CONV), lambda bi, j: (bi, 0, 0))],
        out_shape=[jax.ShapeDtypeStruct((bsz * t_len, D_CONV), F32),
                   jax.ShapeDtypeStruct((bsz, CONV_WIDTH - 1, D_CONV), F32)],
        scratch_shapes=[pltpu.VMEM((tm + CONV_HALO, D_CONV), F32),
                        pltpu.VMEM((SUBLANES - 1, tm + CONV_HALO - SUBLANES, D_CONV), F32)],
        compiler_params=_params("parallel", "arbitrary"),
        name="conv_mixer",
    )(v, left, w, _vec(b), _vec(g), _vec(beta))


def _route(logits):
    lane_i = lax.broadcasted_iota(jnp.int32, logits.shape, 1)
    lane = lane_i.astype(F32)
    group_of_lane = (lane_i >> _log2(EXPERTS_PER_GROUP)).astype(F32)
    neg = -jnp.inf
    far = float(LANES)
    is_g = (lane_i >= N_EXPERTS) & (lane_i < N_EXPERTS + N_EXPERT_GROUPS)
    gl = jnp.where(is_g, logits, neg)
    g_max = jnp.max(gl, axis=-1, keepdims=True)
    g_lane = jnp.min(jnp.where(gl == g_max, lane, far), axis=-1, keepdims=True)
    g_gate = 1.0 / jnp.sum(jnp.exp(gl - g_max), axis=-1, keepdims=True)
    g_idx = g_lane - float(N_EXPERTS)
    in_group = (lane_i < N_EXPERTS) & (group_of_lane == g_idx)
    el = jnp.where(in_group, logits, neg)
    v1 = jnp.max(el, axis=-1, keepdims=True)
    i1 = jnp.min(jnp.where(el == v1, lane, far), axis=-1, keepdims=True)
    el2 = jnp.where(lane == i1, neg, el)
    v2 = jnp.max(el2, axis=-1, keepdims=True)
    i2 = jnp.min(jnp.where(el2 == v2, lane, far), axis=-1, keepdims=True)
    e2 = jnp.exp(v2 - v1)
    w1 = g_gate / (1.0 + e2)
    w2 = g_gate * e2 / (1.0 + e2)
    return i1, i2, w1, w2


ROUTE_E1, ROUTE_E2, ROUTE_W1, ROUTE_W2, ROUTE_RANK1, ROUTE_RANK2 = range(6)


def _outproj_kernel(x_ref, ys_ref, u_ref, yb_ref, d_ref, wglu_ref, wout_ref, nf_ref, rw_ref, rb_ref, cnt0_ref,
                    x1_ref, hn_ref, route_ref, rt_ref, cnt_ref, wglu_bf, wout_bf, rw_bf, acc_glu, acc_mix,
                    *, tail_tiles):
    @pl.when(pl.program_id(0) == 0)
    def _():
        cnt_ref[...] = cnt0_ref[...]
        rw = rw_ref[...]
        rw_hi = rw.astype(BF16)
        rw_bf[:, 0:LANES] = rw_hi
        rw_bf[:, LANES:] = (rw - rw_hi.astype(F32)).astype(BF16)

    _cast_weight_once(wglu_ref, wglu_bf)
    _cast_weight_once(wout_ref, wout_bf)
    ys = jnp.concatenate([ys_ref[q] for q in range(N_SLABS)], axis=1)
    u = jnp.concatenate([u_ref[q] for q in range(N_SLABS)], axis=1)
    z = jax.nn.gelu(ys + d_ref[...] * u)
    ya = z * jax.nn.sigmoid(_pdot_tail(z, *_parts(wglu_bf), acc_glu, tail_tiles))
    mix = _pdot_tail(jnp.concatenate([ya, yb_ref[...]], axis=1), *_parts(wout_bf), acc_mix, tail_tiles)
    x1 = x_ref[...] + mix
    x1_ref[...] = x1
    hn = _rms(x1, nf_ref[...])
    _store_row_tiles(hn_ref, hn)
    h_hi = hn.astype(BF16)
    h_lo = (hn - h_hi.astype(F32)).astype(BF16)
    hw = jnp.dot(h_hi, rw_bf[...], preferred_element_type=F32)
    logits = (hw[:, :LANES] + hw[:, LANES:]
              + jnp.dot(h_lo, rw_bf[:, 0:LANES], preferred_element_type=F32) + rb_ref[...])
    i1, i2, w1, w2 = _route(logits)
    tm = logits.shape[0]
    lane_i = lax.broadcasted_iota(jnp.int32, logits.shape, 1)
    lane = lane_i.astype(F32)
    picked = jnp.where((lane == i1) | (lane == i2), 1.0, 0.0)
    earlier = (lax.broadcasted_iota(jnp.int32, (tm, tm), 0) > lax.broadcasted_iota(jnp.int32, (tm, tm), 1))
    prefix = _bdot(jnp.where(earlier, 1.0, 0.0), picked.astype(BF16)) + cnt_ref[...]
    rank1 = jnp.sum(jnp.where(lane == i1, prefix, 0.0), axis=-1, keepdims=True)
    rank2 = jnp.sum(jnp.where(lane == i2, prefix, 0.0), axis=-1, keepdims=True)
    cnt_ref[...] += jnp.sum(picked, axis=0, keepdims=True)
    rec = jnp.zeros_like(logits)
    for lane_id, val in ((ROUTE_E1, i1), (ROUTE_E2, i2), (ROUTE_W1, w1), (ROUTE_W2, w2),
                         (ROUTE_RANK1, rank1), (ROUTE_RANK2, rank2)):
        rec = jnp.where(lane_i == lane_id, val, rec)
    route_ref[...] = rec
    rt_ref[...] = rec.T[0:SUBLANES, :]


def _outproj(x, ys, u, yb, d, wglu, wout, nf, rw, rb, counts0, layer, tm, t_len, tail):
    n = x.shape[0]
    precise = tail > 0
    return pl.pallas_call(
        functools.partial(_outproj_kernel, tail_tiles=_stream_tail_tiles(t_len, tm, tail)),
        grid=(n // tm,),
        in_specs=[_row_spec(tm, D_MODEL), _slab_spec(tm), _slab_spec(tm), _row_spec(tm, D_CONV),
                  _layer_spec(layer, 1, D_SSM), _layer_spec(layer, D_SSM, D_SSM),
                  _layer_spec(layer, D_MODEL, D_MODEL), _layer_spec(layer, 1, D_MODEL),
                  _const_spec((D_MODEL, LANES)), _const_spec((1, LANES)), _const_spec((1, LANES))],
        out_specs=[_row_spec(tm, D_MODEL), _row_tile_spec(tm), _row_spec(tm, LANES),
                   pl.BlockSpec((SUBLANES, tm), lambda i: (0, i)), _const_spec((1, LANES))],
        out_shape=[jax.ShapeDtypeStruct((n, D_MODEL), F32), jax.ShapeDtypeStruct(_tiled_rows(n), F32),
                   jax.ShapeDtypeStruct((n, LANES), F32), jax.ShapeDtypeStruct((SUBLANES, n), F32),
                   jax.ShapeDtypeStruct((1, LANES), F32)],
        scratch_shapes=[_weight_scratch(D_SSM, D_SSM, precise), _weight_scratch(D_MODEL, D_MODEL, precise),
                        pltpu.VMEM((D_MODEL, 2 * LANES), BF16),
                        _tail_acc(tm, D_SSM, precise), _tail_acc(tm, D_MODEL, precise)],
        compiler_params=_params("arbitrary"),
        name="outproj_router",
    )(x, ys, u, yb, _vec(d), wglu, wout, _vec(nf), rw, rb, counts0)


PLAN_TILE_LANES = 2 * LANES
PLAN_EXPERT, PLAN_ROWS, PLAN_USED = range(3)


def _moe_plan_kernel(rt_ref, cnt_ref, cnt_first_ref, slot_ref, tile_ref, *, tms):
    cnt = cnt_ref[...]
    padded = jnp.ceil(cnt * (1.0 / tms)) * float(tms)
    r = lax.broadcasted_iota(jnp.int32, (LANES, LANES), 0)
    c = lax.broadcasted_iota(jnp.int32, (LANES, LANES), 1)
    ends = jnp.dot(padded, jnp.where(r <= c, 1.0, 0.0), precision=lax.Precision.HIGHEST,
                   preferred_element_type=F32)
    starts = ends - padded
    rt = rt_ref[...]
    e1, e2 = rt[ROUTE_E1:ROUTE_E1 + 1], rt[ROUTE_E2:ROUTE_E2 + 1]
    s1, s2 = rt[ROUTE_RANK1:ROUTE_RANK1 + 1], rt[ROUTE_RANK2:ROUTE_RANK2 + 1]
    tile = lax.broadcasted_iota(jnp.int32, (1, PLAN_TILE_LANES), 1).astype(F32)
    used = ends[:, N_EXPERTS - 1:N_EXPERTS] * (1.0 / tms)
    pos = jnp.minimum(tile, used - 1.0) * float(tms)
    t_exp = jnp.zeros_like(tile)
    t_fill = jnp.zeros_like(tile)
    for e in range(N_EXPERTS):
        st, en = starts[:, e:e + 1], ends[:, e:e + 1]
        s1 = s1 + jnp.where(e1 == float(e), st, 0.0)
        s2 = s2 + jnp.where(e2 == float(e), st, 0.0)
        mine = (pos >= st) & (pos < en)
        t_exp = t_exp + jnp.where(mine, float(e), 0.0)
        t_fill = t_fill + jnp.where(mine, st + cnt_first_ref[:, e:e + 1], 0.0)
    t_rows = jnp.where(tile < used, jnp.clip(t_fill - pos, 0.0, float(tms)), 0.0)
    slot_ref[...] = jnp.concatenate([s1, s2], axis=0).astype(jnp.int32)
    tile_ref[...] = jnp.concatenate(
        [t_exp, t_rows, jnp.broadcast_to(used, tile.shape), jnp.zeros((SUBLANES - 3, PLAN_TILE_LANES), F32)],
        axis=0).astype(jnp.int32)


def _moe_plan(route_t, counts, counts_first, tms, n_tiles):
    n = route_t.shape[1]
    assert n_tiles <= PLAN_TILE_LANES
    slots, tiles = pl.pallas_call(
        functools.partial(_moe_plan_kernel, tms=tms),
        out_shape=[jax.ShapeDtypeStruct((2, n), jnp.int32),
                   jax.ShapeDtypeStruct((SUBLANES, PLAN_TILE_LANES), jnp.int32)],
        compiler_params=pltpu.CompilerParams(vmem_limit_bytes=VMEM_LIMIT),
        name="moe_plan",
    )(route_t, counts, counts_first)
    return slots, tiles[PLAN_EXPERT, :n_tiles], tiles[PLAN_ROWS, :n_tiles], tiles[PLAN_USED, :1]


DMA_UNROLL = 8


HN_BUFFERS = 3
ZERO_FILL_ROWS = 128


def _dispatch_kernel(tr_ref, slot_ref, hn_ref, *rest, tm, tms, n_tiles, n_steps, fresh):
    xs_ref, zbuf, hbuf, in_sem, out_sem, zsem = rest if fresh else rest[1:]
    i = pl.program_id(0)

    def fetch(t):
        b = lax.rem(t, HN_BUFFERS)
        first = pl.multiple_of(t * (tm * ROW_SUB), tm * ROW_SUB)
        return pltpu.make_async_copy(hn_ref.at[pl.ds(first, tm * ROW_SUB)], hbuf.at[b], in_sem.at[b])

    def drain_scatter(t):
        b = lax.rem(t, HN_BUFFERS)
        for k in range(2):
            pltpu.make_async_copy(hbuf.at[b], xs_ref.at[pl.ds(0, tm * ROW_SUB)], out_sem.at[b]).wait()

    @pl.when(i == 0)
    def _():
        fetch(i).start()

    @pl.when(jnp.logical_and(i == 0, fresh))
    def _():
        zbuf[...] = jnp.zeros_like(zbuf)

        zrows = zbuf.shape[0] // ROW_SUB
        per_tile = tms // zrows

        def unfilled(p):
            return tr_ref[p // per_tile] < (lax.rem(p, per_tile) + 1) * zrows

        def fill(p, carry):
            @pl.when(unfilled(p))
            def _():
                first = pl.multiple_of(p * (zrows * ROW_SUB), zrows * ROW_SUB)
                pltpu.make_async_copy(zbuf, xs_ref.at[pl.ds(first, zrows * ROW_SUB)], zsem).start()
            return carry

        def drain(p, carry):
            @pl.when(unfilled(p))
            def _():
                pltpu.make_async_copy(zbuf, xs_ref.at[pl.ds(0, zrows * ROW_SUB)], zsem).wait()
            return carry

        lax.fori_loop(0, n_tiles * per_tile, fill, 0)
        lax.fori_loop(0, n_tiles * per_tile, drain, 0)

    @pl.when(i + 1 < n_steps)
    def _():
        fetch(i + 1).start()

    fetch(i).wait()
    b = lax.rem(i, HN_BUFFERS)
    rows = hbuf.at[b]

    def issue(r, carry):
        for k in range(2):
            pltpu.make_async_copy(_one_row(rows, r), _one_row(xs_ref, slot_ref[k, r]),
                                  out_sem.at[b]).start(priority=k)
        return carry

    lax.fori_loop(0, tm, issue, 0, unroll=DMA_UNROLL)

    @pl.when(i >= 1)
    def _():
        drain_scatter(i - 1)

    @pl.when(i == n_steps - 1)
    def _():
        drain_scatter(i)


def _dispatch(hn, slots, tile_rows, tms, tm, into=None):
    n = hn.shape[0] // ROW_SUB
    n_tiles = tile_rows.shape[0]
    fresh = into is None
    any_spec = pl.BlockSpec(memory_space=pl.ANY)
    return pl.pallas_call(
        functools.partial(_dispatch_kernel, tm=tm, tms=tms, n_tiles=n_tiles, n_steps=n // tm, fresh=fresh),
        grid_spec=pltpu.PrefetchScalarGridSpec(
            num_scalar_prefetch=1, grid=(n // tm,),
            in_specs=[pl.BlockSpec((2, tm), lambda i, tr: (0, i), memory_space=pltpu.SMEM), any_spec]
                     + ([] if fresh else [any_spec]),
            out_specs=any_spec,
            scratch_shapes=[pltpu.VMEM(_tiled_rows(min(tms, ZERO_FILL_ROWS)), F32),
                            pltpu.VMEM((HN_BUFFERS,) + _tiled_rows(tm), F32),
                            pltpu.SemaphoreType.DMA((HN_BUFFERS,)), pltpu.SemaphoreType.DMA((HN_BUFFERS,)),
                            pltpu.SemaphoreType.DMA]),
        out_shape=jax.ShapeDtypeStruct(_tiled_rows(n_tiles * tms), F32),
        input_output_aliases={} if fresh else {3: 0},
        compiler_params=_params("arbitrary"),
        name="moe_dispatch",
    )(tile_rows, slots, hn, *([] if fresh else [into]))


def _moe_kernel(te_ref, nu_ref, x_ref, wg_ref, wu_ref, wd_ref, y_ref, wg_bf, wu_bf, wd_bf):
    i = pl.program_id(0)
    in_use = i < nu_ref[0]
    new_expert = (i == 0) | (te_ref[i] != te_ref[jnp.maximum(i - 1, 0)])

    @pl.when(in_use & new_expert)
    def _():
        wg_bf[...] = wg_ref[...].astype(BF16)
        wu_bf[...] = wu_ref[...].astype(BF16)
        wd_bf[...] = wd_ref[...].astype(BF16)

    @pl.when(in_use)
    def _():
        h = _load_row_tiles(x_ref).astype(BF16)
        hg = jnp.dot(h, wg_bf[...], preferred_element_type=F32)
        hu = jnp.dot(h, wu_bf[...], preferred_element_type=F32)
        _store_row_tiles(y_ref, _bdot(hg * jax.nn.sigmoid(hg) * hu, wd_bf[...]))

    @pl.when(jnp.logical_not(in_use))
    def _():
        y_ref[...] = jnp.zeros_like(y_ref)


def _moe(xs, tile_expert, n_used, wg, wu, wd, layer, tms):
    n_slots = xs.shape[0] // ROW_SUB
    rows = pl.BlockSpec(_tiled_rows(tms), lambda i, te, nu: (jnp.minimum(i, nu[0] - 1), 0))
    out_rows = pl.BlockSpec(_tiled_rows(tms), lambda i, te, nu: (i, 0))

    def wspec(a, b):
        return pl.BlockSpec((None, None, a, b), lambda i, te, nu: (layer, te[i], 0, 0))

    return pl.pallas_call(
        _moe_kernel,
        grid_spec=pltpu.PrefetchScalarGridSpec(
            num_scalar_prefetch=2, grid=(n_slots // tms,),
            in_specs=[rows, wspec(D_MODEL, D_EXPERT), wspec(D_MODEL, D_EXPERT), wspec(D_EXPERT, D_MODEL)],
            out_specs=out_rows,
            scratch_shapes=[pltpu.VMEM((D_MODEL, D_EXPERT), BF16), pltpu.VMEM((D_MODEL, D_EXPERT), BF16),
                            pltpu.VMEM((D_EXPERT, D_MODEL), BF16)]),
        out_shape=jax.ShapeDtypeStruct(xs.shape, F32),
        compiler_params=_params("arbitrary"),
        name="moe",
    )(tile_expert, n_used, xs, wg, wu, wd)


def _ple_kernel(slot_ref, next_slot_ref, x_ref, route_ref, p_ref, np_ref, wple_ref, wgate_ref, nfin_ref,
                ys_ref, o_ref, ybuf, sem, wple_bf, wgate_bf, *, tm, n_steps, final):
    i = pl.program_id(0)
    _cast_weight_once(wple_ref, wple_bf)
    _cast_weight_once(wgate_ref, wgate_bf)

    def gather(slots, b):
        def issue(r, carry):
            for k in range(2):
                pltpu.make_async_copy(_one_row(ys_ref, slots[k, r]), _one_row(ybuf.at[b, k], r),
                                      sem.at[b]).start(priority=k)
            return carry

        lax.fori_loop(0, tm, issue, 0, unroll=DMA_UNROLL)

    @pl.when(i == 0)
    def _():
        gather(slot_ref, 0)

    @pl.when(i + 1 < n_steps)
    def _():
        gather(next_slot_ref, lax.rem(i + 1, 2))

    pe = _pdot(p_ref[...], *_parts(wple_bf))
    b = lax.rem(i, 2)
    for k in range(2):
        pltpu.make_async_copy(ys_ref.at[pl.ds(0, tm * ROW_SUB)], ybuf.at[b, k], sem.at[b]).wait()
    route = route_ref[...]
    x = (x_ref[...] + route[:, ROUTE_W1:ROUTE_W1 + 1] * _load_row_tiles(ybuf.at[b, 0])
         + route[:, ROUTE_W2:ROUTE_W2 + 1] * _load_row_tiles(ybuf.at[b, 1]))
    gate = jax.nn.sigmoid(_pdot(_rms(x, np_ref[...]), *_parts(wgate_bf)))
    out = x + pe * gate
    if final:
        out = _rms(out, nfin_ref[...])
    o_ref[...] = out


def _ple(x, route, slots, ys, p, npl, wple, wgate, nfin, layer, tm, final):
    n = x.shape[0]
    n_steps = n // tm
    return pl.pallas_call(
        functools.partial(_ple_kernel, tm=tm, n_steps=n_steps, final=final),
        grid=(n_steps,),
        in_specs=[pl.BlockSpec((2, tm), lambda i: (0, i), memory_space=pltpu.SMEM),
                  pl.BlockSpec((2, tm), lambda i: (0, jnp.minimum(i + 1, n_steps - 1)),
                               memory_space=pltpu.SMEM),
                  _row_spec(tm, D_MODEL), _row_spec(tm, LANES),
                  pl.BlockSpec((None, tm, D_PLE), lambda i: (layer, i, 0)),
                  _layer_spec(layer, 1, D_MODEL), _layer_spec(layer, D_PLE, D_MODEL),
                  _layer_spec(layer, D_MODEL, D_MODEL), _const_spec((1, D_MODEL)),
                  pl.BlockSpec(memory_space=pl.ANY)],
        out_specs=_row_spec(tm, D_MODEL),
        out_shape=jax.ShapeDtypeStruct((n, D_MODEL), F32),
        scratch_shapes=[pltpu.VMEM((2, 2) + _tiled_rows(tm), F32), pltpu.SemaphoreType.DMA((2,)),
                        _weight_scratch(D_PLE, D_MODEL, False), _weight_scratch(D_MODEL, D_MODEL, False)],
        compiler_params=_params("arbitrary"),
        name="combine_ple",
    )(slots, slots, x, route, p, _vec(npl), wple, wgate, nfin.reshape(1, D_MODEL), ys)


def _mixers(x, h0_re, h0_im, conv_left, w, s5, router_w, router_b, counts0, layer, bsz, t_len, tm, tm_conv,
            tail):
    u, v = _inproj(x, w["norm_mix"], w["w_in"], layer, min(2 * tm, x.shape[0]), t_len, tail)
    ys, hf_re, hf_im = _s5_mixer(u, h0_re, h0_im, s5, bsz, t_len, tail)
    yb, conv_new = _conv_mixer(v, conv_left, w["conv_w"], w["conv_b"], w["conv_ln_g"], w["conv_ln_b"],
                               layer, bsz, t_len, tm_conv)
    x1, hn, route, route_t, counts = _outproj(x, ys, u, yb, w["ssm_d"], w["w_ssm_glu"], w["w_out"],
                                              w["norm_ffn"], router_w, router_b, counts0, layer, tm, t_len, tail)
    return dict(x1=x1, hn=hn, route=route, route_t=route_t, counts=counts, state=(hf_re, hf_im, conv_new))


def _moe_and_ple(sets, ps, tms_rows, w, layer, tms, final):
    sizes = [s["x1"].shape[0] for s in sets]
    n_tiles = 2 * sum(sizes) // tms + N_EXPERTS
    route_t = jnp.concatenate([s["route_t"] for s in sets], axis=1)
    slots, tile_expert, tile_rows, n_used = _moe_plan(route_t, sets[-1]["counts"], sets[0]["counts"], tms,
                                                     n_tiles)
    starts = [sum(sizes[:j]) for j in range(len(sets))]
    set_slots = [slots[:, a:a + n] for a, n in zip(starts, sizes)]
    xsort = None
    for s, sl, tm in zip(sets, set_slots, tms_rows):
        xsort = _dispatch(s["hn"], sl, tile_rows, tms, tm, into=xsort)
    ysort = _moe(xsort, tile_expert, n_used, w["expert_w_gate"], w["expert_w_up"], w["expert_w_down"],
                 layer, tms)
    return [_ple(s["x1"], s["route"], sl, ysort, p, w["norm_ple"], w["ple_w"], w["ple_gate_w"],
                 w["norm_final"], layer, tm, final)
            for s, sl, p, tm in zip(sets, set_slots, ps, tms_rows)]


def kernel(x_prompt, x_sample, p_prompt, p_sample, state_ssm_re, state_ssm_im, cache_conv, norm_mix, w_in, ssm_a_re, ssm_a_im, ssm_b_re, ssm_b_im, ssm_c_re, ssm_c_im, ssm_d, ssm_log_dt, w_ssm_glu, conv_w, conv_b, conv_ln_g, conv_ln_b, w_out, norm_ffn, router_group_w, router_group_b, router_expert_w, router_expert_b, expert_w_gate, expert_w_up, expert_w_down, norm_ple, ple_w, ple_gate_w, norm_final):
    depth = w_in.shape[0]
    bp, tp, _ = x_prompt.shape
    bs, ts, _ = x_sample.shape
    xp = x_prompt.reshape(bp * tp, D_MODEL)
    xs = x_sample.reshape(bs * ts, D_MODEL)
    pp = p_prompt.reshape(depth, bp * tp, D_PLE)
    ps = p_sample.reshape(depth, bs * ts, D_PLE)
    zero_state = jnp.zeros((bp, N_GROUPS, SSM_STATE), F32)
    zero_conv = jnp.zeros((bp, CONV_WIDTH - 1, D_CONV), F32)
    pad_lanes = LANES - N_EXPERTS - N_EXPERT_GROUPS
    w = {"norm_mix": norm_mix, "w_in": w_in, "ssm_d": ssm_d, "w_ssm_glu": w_ssm_glu, "conv_w": conv_w,
         "conv_b": conv_b, "conv_ln_g": conv_ln_g, "conv_ln_b": conv_ln_b, "w_out": w_out,
         "norm_ffn": norm_ffn, "expert_w_gate": expert_w_gate, "expert_w_up": expert_w_up,
         "expert_w_down": expert_w_down, "norm_ple": norm_ple, "ple_w": ple_w, "ple_gate_w": ple_gate_w,
         "norm_final": norm_final}
    no_picks = jnp.zeros((1, LANES), F32)
    outs = {k: [] for k in ("pr_re", "pr_im", "pr_conv", "sm_re", "sm_im", "sm_conv")}
    for i in range(depth):
        s5 = _s5_prep(ssm_a_re[i], ssm_a_im[i], ssm_log_dt[i], ssm_b_re[i], ssm_b_im[i], ssm_c_re[i],
                      ssm_c_im[i])
        router_w = jnp.pad(jnp.concatenate([router_expert_w[i], router_group_w[i]], axis=1),
                           ((0, 0), (0, pad_lanes)))
        router_b = jnp.pad(jnp.concatenate([router_expert_b[i], router_group_b[i]]),
                           (0, pad_lanes)).reshape(1, LANES)
        final = i == depth - 1
        tail = 0 if final else PRECISE_TAIL
        mp = _mixers(xp, zero_state, zero_state, zero_conv, w, s5, router_w, router_b, no_picks, i,
                     bp, tp, tm=TOKEN_TILE, tm_conv=2 * TOKEN_TILE, tail=tail)
        ms = _mixers(xs, state_ssm_re[i], state_ssm_im[i], cache_conv[i], w, s5, router_w, router_b,
                     mp["counts"], i, bs, ts, tm=bs * ts, tm_conv=ts, tail=tail)
        xp, xs = _moe_and_ple([mp, ms], [pp, ps], [TOKEN_TILE, bs * ts], w, i, SLOT_TILE, final)
        for key, val in zip(("pr_re", "pr_im", "pr_conv"), mp["state"]):
            outs[key].append(val)
        for key, val in zip(("sm_re", "sm_im", "sm_conv"), ms["state"]):
            outs[key].append(val)
    return (xp.reshape(bp, tp, D_MODEL), xs.reshape(bs, ts, D_MODEL),
            jnp.stack(outs["pr_re"]), jnp.stack(outs["pr_im"]), jnp.stack(outs["pr_conv"]),
            jnp.stack(outs["sm_re"]), jnp.stack(outs["sm_im"]), jnp.stack(outs["sm_conv"]))
```

```python
import functools

import jax
import jax.numpy as jnp
from jax import lax
from jax.experimental import pallas as pl
from jax.experimental.pallas import tpu as pltpu

F32 = jnp.float32
BF16 = jnp.bfloat16

D_MODEL = 1024
D_SSM = 512
SSM_GROUP = 16
N_GROUPS = D_SSM // SSM_GROUP
N_PAIRS = N_GROUPS // 2
SSM_STATE = 64
D_CONV = 512
CONV_WIDTH = 31
CONV_HALO = 32
N_EXPERT_GROUPS = 4
EXPERTS_PER_GROUP = 8
N_EXPERTS = 32
D_EXPERT = 256
D_PLE = 256
EPS = 1e-6
S5_CHUNK = 16
S5_RELAYOUT_ROWS = 128
S5_SCAN_UNROLL = 4
LANES = 128
SUBLANES = 8
N_SLABS = D_SSM // LANES
GROUPS_PER_SLAB = LANES // SSM_GROUP
PAIRS_PER_SLAB = GROUPS_PER_SLAB // 2
VMEM_LIMIT = 56 * 1024 * 1024
TOKEN_TILE = 512
SLOT_TILE = 512
PRECISE_TAIL = 1024


def _log2(n):
    assert n & (n - 1) == 0, n
    return n.bit_length() - 1


def _params(*sem):
    return pltpu.CompilerParams(dimension_semantics=sem, vmem_limit_bytes=VMEM_LIMIT)


def _rms(x, g):
    return x * lax.rsqrt(jnp.mean(x * x, axis=-1, keepdims=True) + EPS) * g


def _bdot(a, b):
    return jnp.dot(a.astype(BF16), b, preferred_element_type=F32)


def _split_bf16(a):
    hi = a.astype(BF16)
    return hi, (a - hi.astype(F32)).astype(BF16)


def _pdot(a, w_hi, w_lo=None, dims=(((1,), (0,)), ((), ()))):
    def mm(x, w):
        return lax.dot_general(x, w, dims, preferred_element_type=F32)

    if w_lo is None:
        return mm(a.astype(BF16), w_hi)
    a_hi, a_lo = _split_bf16(a)
    return mm(a_hi, w_hi) + (mm(a_hi, w_lo) + mm(a_lo, w_hi))


def _pdot_lo_terms(a, w_hi, w_lo, dims=(((1,), (0,)), ((), ()))):
    a_hi, a_lo = _split_bf16(a)
    return (lax.dot_general(a_hi, w_lo, dims, preferred_element_type=F32)
            + lax.dot_general(a_lo, w_hi, dims, preferred_element_type=F32))


def _stream_tail_tiles(t_len, tm, tail):
    tiles = max(t_len // tm, 1)
    return tiles, (max(t_len - tail, 0) // tm if tm < t_len else 0)


def _per_tile_precision(tail_tiles, has_lo, body):
    tiles, first = tail_tiles
    if not has_lo or first == 0:
        body(has_lo)
        return
    in_tail = lax.rem(pl.program_id(0), tiles) >= first
    pl.when(in_tail)(functools.partial(body, True))
    pl.when(jnp.logical_not(in_tail))(functools.partial(body, False))


def _parts(wbf_ref, rows=slice(None), precise=True):
    return wbf_ref[0, rows, :], (wbf_ref[1, rows, :] if precise and wbf_ref.shape[0] == 2 else None)


def _row_spec(tm, width):
    return pl.BlockSpec((tm, width), lambda i: (i, 0))


ROW_SUB = D_MODEL // LANES


def _tiled_rows(n):
    return (n * ROW_SUB, LANES)


def _row_tile_spec(tm):
    return pl.BlockSpec(_tiled_rows(tm), lambda i, *_: (i, 0))


def _one_row(ref, r):
    return ref.at[pl.ds(pl.multiple_of(r * ROW_SUB, ROW_SUB), ROW_SUB)]


def _store_row_tiles(ref, rows):
    n = rows.shape[0]
    for s in range(ROW_SUB):
        ref[pl.ds(s, n, stride=ROW_SUB), :] = rows[:, s * LANES:(s + 1) * LANES]


def _load_row_tiles(ref):
    n = ref.shape[0] // ROW_SUB
    return jnp.concatenate([ref[pl.ds(s, n, stride=ROW_SUB), :] for s in range(ROW_SUB)], axis=1)


def _slab_spec(tm):
    return pl.BlockSpec((N_SLABS, tm, LANES), lambda i: (0, i, 0))


def _const_spec(shape):
    return pl.BlockSpec(shape, lambda i: (0,) * len(shape))


def _layer_spec(layer, *shape):
    return pl.BlockSpec((None,) + shape, lambda *_: (layer,) + (0,) * len(shape))


def _vec(stacked):
    return stacked.reshape(stacked.shape[0], 1, stacked.shape[1])


def _cast_weight_once(w_ref, wbf_ref):
    @pl.when(pl.program_id(0) == 0)
    def _():
        w = w_ref[...]
        hi = w.astype(BF16)
        wbf_ref[0] = hi
        if wbf_ref.shape[0] == 2:
            wbf_ref[1] = (w - hi.astype(F32)).astype(BF16)


def _weight_scratch(k, n, precise):
    return pltpu.VMEM((2 if precise else 1, k, n), BF16)


def _inproj_kernel(x_ref, g_ref, w_ref, u_ref, v_ref, wbf, *, tail_tiles):
    _cast_weight_once(w_ref, wbf)

    def body(precise):
        hn = _rms(x_ref[...], g_ref[...])
        proj = _pdot(hn, *_parts(wbf, precise=precise))
        for q in range(N_SLABS):
            u_ref[q] = proj[:, q * LANES:(q + 1) * LANES]
        v_ref[...] = proj[:, D_SSM:D_SSM + D_CONV] * jax.nn.sigmoid(proj[:, D_SSM + D_CONV:])

    _per_tile_precision(tail_tiles, wbf.shape[0] == 2, body)


def _inproj(x, g, w, layer, tm, t_len, tail):
    n = x.shape[0]
    d_in = w.shape[2]
    precise = tail > 0
    return pl.pallas_call(
        functools.partial(_inproj_kernel, tail_tiles=_stream_tail_tiles(t_len, tm, tail)),
        grid=(n // tm,),
        in_specs=[_row_spec(tm, D_MODEL), _layer_spec(layer, 1, D_MODEL), _layer_spec(layer, D_MODEL, d_in)],
        out_specs=[_slab_spec(tm), _row_spec(tm, D_CONV)],
        out_shape=[jax.ShapeDtypeStruct((N_SLABS, n, LANES), F32), jax.ShapeDtypeStruct((n, D_CONV), F32)],
        scratch_shapes=[_weight_scratch(D_MODEL, d_in, precise)],
        compiler_params=_params("arbitrary"),
        name="inproj",
    )(x, _vec(g), w)


def _s5_prep_kernel(ar_ref, ai_ref, ldt_ref, bre_ref, bim_ref, cre_ref, cim_ref,
                    m_ref, ws_ref, wot_ref, atab_ref, wt_re, wt_im, br_re, br_im):
    n_tap = S5_CHUNK * SSM_GROUP
    st = 4 * SSM_STATE
    nt = (((1,), (1,)), ((), ()))
    hi = lax.Precision.HIGHEST
    lane = lax.broadcasted_iota(jnp.int32, (SSM_GROUP, n_tap), 1)
    ws_rows, wot_rows, atab = [], [], jnp.zeros((2 * SUBLANES, st), F32)
    for gi in range(2):
        ar, ai = ar_ref[gi], ai_ref[gi]
        dt = jnp.exp(ldt_ref[gi])
        k = lax.broadcasted_iota(jnp.int32, (S5_CHUNK + SUBLANES, SSM_STATE), 0).astype(F32)
        mag = jnp.exp(k * (dt * ar))
        ang = k * (dt * ai)
        p_re, p_im = mag * jnp.cos(ang), mag * jnp.sin(ang)
        inv = 1.0 / (ar * ar + ai * ai)
        ab_re, ab_im = p_re[1:2], p_im[1:2]
        ia_re, ia_im = ar * inv, -ai * inv
        coef_re = (ab_re - 1.0) * ia_re - ab_im * ia_im
        coef_im = (ab_re - 1.0) * ia_im + ab_im * ia_re
        bre, bim = bre_ref[gi], bim_ref[gi]
        bb_re = coef_re * bre - coef_im * bim
        bb_im = coef_re * bim + coef_im * bre
        cre, cim = cre_ref[gi], cim_ref[gi]
        for kk in range(S5_CHUNK + 1):
            pr, pi = p_re[kk:kk + 1], p_im[kk:kk + 1]
            rows = slice(kk * SSM_GROUP, (kk + 1) * SSM_GROUP)
            wt_re[rows, :] = pr * cre - pi * cim
            wt_im[rows, :] = -pi * cre - pr * cim
            if kk < S5_CHUNK:
                back = slice((S5_CHUNK - 1 - kk) * SSM_GROUP, (S5_CHUNK - kk) * SSM_GROUP)
                br_re[back, :] = pr * bb_re - pi * bb_im
                br_im[back, :] = pi * bb_re + pr * bb_im
        kcat = (lax.dot_general(bb_re, wt_re[0:n_tap, :], nt, precision=hi, preferred_element_type=F32)
                + lax.dot_general(bb_im, wt_im[0:n_tap, :], nt, precision=hi, preferred_element_type=F32))
        for s in range(S5_CHUNK):
            shifted = kcat if s == 0 else pltpu.roll(kcat, s * SSM_GROUP, 1)
            rows = slice(s * SSM_GROUP, (s + 1) * SSM_GROUP)
            m_ref[0, gi, rows, :], m_ref[1, gi, rows, :] = _split_bf16(
                jnp.where(lane >= s * SSM_GROUP, shifted, 0.0))
        def place(v_re, v_im):
            zero = jnp.zeros_like(v_re)
            parts = [v_re, zero, v_im, zero] if gi == 0 else [zero, v_re, zero, v_im]
            return jnp.concatenate(parts, axis=1)

        ws_rows.append(place(br_re[...], br_im[...]))
        wot_rows.append(place(wt_re[SSM_GROUP:, :], wt_im[SSM_GROUP:, :]))
        kc = float(S5_CHUNK) * lax.broadcasted_iota(jnp.int32, (2 * SUBLANES, SSM_STATE), 0).astype(F32)
        magc = jnp.exp(kc * (dt * ar))
        angc = kc * (dt * ai)
        atab = atab + place(magc * jnp.cos(angc), magc * jnp.sin(angc))
    ws_ref[0], ws_ref[1] = _split_bf16(jnp.concatenate(ws_rows, axis=0))
    wot_ref[0], wot_ref[1] = _split_bf16(jnp.concatenate(wot_rows, axis=0))
    atab_ref[...] = atab


def _s5_prep(a_re, a_im, log_dt, b_re, b_im, c_re, c_im):
    p, n, c = N_PAIRS, SSM_STATE, SSM_GROUP
    n_tap = S5_CHUNK * c
    st = 4 * n

    def pspec(*shape):
        return pl.BlockSpec((None,) + shape, lambda i: (i,) + (0,) * len(shape))

    def pairs(a, *shape):
        return a.reshape((p, 2) + shape)

    return pl.pallas_call(
        _s5_prep_kernel,
        grid=(p,),
        in_specs=[pspec(2, 1, n), pspec(2, 1, n), pspec(2, 1, 1), pspec(2, c, n), pspec(2, c, n),
                  pspec(2, c, n), pspec(2, c, n)],
        out_specs=[pspec(2, 2, n_tap, n_tap), pspec(2, 2 * n_tap, st), pspec(2, 2 * n_tap, st),
                   pspec(2 * SUBLANES, st)],
        out_shape=[jax.ShapeDtypeStruct((p, 2, 2, n_tap, n_tap), BF16),
                   jax.ShapeDtypeStruct((p, 2, 2 * n_tap, st), BF16),
                   jax.ShapeDtypeStruct((p, 2, 2 * n_tap, st), BF16),
                   jax.ShapeDtypeStruct((p, 2 * SUBLANES, st), F32)],
        scratch_shapes=[pltpu.VMEM((n_tap + c, n), F32), pltpu.VMEM((n_tap + c, n), F32),
                        pltpu.VMEM((n_tap, n), F32), pltpu.VMEM((n_tap, n), F32)],
        compiler_params=_params("parallel"),
        name="s5_prep",
    )(pairs(a_re, 1, n), pairs(a_im, 1, n), pairs(log_dt, 1, 1),
      pairs(jnp.swapaxes(b_re, 1, 2), c, n), pairs(jnp.swapaxes(b_im, 1, 2), c, n),
      pairs(c_re, c, n), pairs(c_im, c, n))


def _block_transpose8(vs):
    lane = lax.broadcasted_iota(jnp.int32, vs[0].shape, 1)
    blk = lane >> _log2(SSM_GROUP)
    for d in (GROUPS_PER_SLAB >> s for s in range(1, _log2(GROUPS_PER_SLAB) + 1)):
        keep = (blk & d) == 0
        new = list(vs)
        for i in range(GROUPS_PER_SLAB):
            if i & d == 0:
                a, b = vs[i], vs[i + d]
                new[i] = jnp.where(keep, a, pltpu.roll(b, d * SSM_GROUP, 1))
                new[i + d] = jnp.where(keep, pltpu.roll(a, LANES - d * SSM_GROUP, 1), b)
        vs = new
    return vs


def _cmul(ar, ai, xr, xi):
    return ar * xr - ai * xi, ar * xi + ai * xr


def _s5_kernel(u_ref, h0_ref, m_ref, ws_ref, wo_ref, a_ref, y_ref, hf_ref, x_scr, yg_scr, s_scr, hp_scr,
               *, rows, independent, tail_rows):
    half = 2 * SSM_STATE
    tail = pl.ds(rows - tail_rows, tail_rows)
    rt = min(rows, S5_RELAYOUT_ROWS)
    half_chunk = S5_CHUNK // 2

    def gather_tile(t, carry):
        r0 = pl.multiple_of(t * rt, rt)
        for hf in range(2):
            vs = [u_ref[pl.ds(r0 * S5_CHUNK + hf * half_chunk + i, rt, stride=S5_CHUNK), :]
                  for i in range(half_chunk)]
            outs = _block_transpose8(vs)
            for g in range(GROUPS_PER_SLAB):
                x_scr[g, pl.ds(r0, rt), hf * LANES:(hf + 1) * LANES] = outs[g]
        return carry

    lax.fori_loop(0, rows // rt, gather_tile, 0)

    row = lax.broadcasted_iota(jnp.int32, (SUBLANES, half), 0)
    n_tap = S5_CHUNK * SSM_GROUP
    for pi in range(PAIRS_PER_SLAB):
        def part(w_ref, *idx):
            return w_ref[(pi, 0) + idx], w_ref[(pi, 1) + idx]

        x0 = x_scr[2 * pi]
        x1 = x_scr[2 * pi + 1]
        x01 = jnp.concatenate([x0, x1], axis=1)
        s_scr[...] = _pdot(x01, ws_ref[pi, 0])
        if tail_rows:
            s_scr[tail, :] += _pdot_lo_terms(x01[rows - tail_rows:], *part(ws_ref))
        ap = a_ref[pi]
        h0 = h0_ref[pi]
        if independent:
            hp_scr[...] = h0
            s = s_scr[...]
            n_re, n_im = _cmul(ap[1:2, :half], ap[1:2, half:], h0[:, :half], h0[:, half:])
            hf_ref[pi] = jnp.concatenate([n_re + s[:, :half], n_im + s[:, half:]], axis=1)
        else:
            pw_re, pw_im = ap[0:SUBLANES, :half], ap[0:SUBLANES, half:]

            def scan_tile(t, carry):
                h_re, h_im = carry
                r0 = pl.multiple_of(t * SUBLANES, SUBLANES)
                s = s_scr[pl.ds(r0, SUBLANES), :]
                t_re, t_im = s[:, :half], s[:, half:]
                for d in (1, 2, 4):
                    sh_re = jnp.where(row >= d, pltpu.roll(t_re, d, 0), 0.0)
                    sh_im = jnp.where(row >= d, pltpu.roll(t_im, d, 0), 0.0)
                    m_re, m_im = _cmul(ap[d:d + 1, :half], ap[d:d + 1, half:], sh_re, sh_im)
                    t_re, t_im = t_re + m_re, t_im + m_im
                e_re = jnp.where(row >= 1, pltpu.roll(t_re, 1, 0), 0.0)
                e_im = jnp.where(row >= 1, pltpu.roll(t_im, 1, 0), 0.0)
                c_re, c_im = _cmul(pw_re, pw_im, h_re, h_im)
                hp_scr[pl.ds(r0, SUBLANES), :] = jnp.concatenate([e_re + c_re, e_im + c_im], axis=1)
                o_re, o_im = _cmul(ap[SUBLANES:SUBLANES + 1, :half], ap[SUBLANES:SUBLANES + 1, half:],
                                   h_re, h_im)
                last = SUBLANES - 1
                n_re = jnp.broadcast_to(t_re[last:last + 1], h_re.shape) + o_re
                n_im = jnp.broadcast_to(t_im[last:last + 1], h_im.shape) + o_im
                return n_re, n_im

            init = (jnp.broadcast_to(h0[:, :half], (SUBLANES, half)),
                    jnp.broadcast_to(h0[:, half:], (SUBLANES, half)))
            h_re, h_im = lax.fori_loop(0, rows // SUBLANES, scan_tile, init, unroll=S5_SCAN_UNROLL)
            hf_ref[pi] = jnp.concatenate([h_re[0:1], h_im[0:1]], axis=1)
        nt = (((1,), (1,)), ((), ()))
        yc = _pdot(hp_scr[...], wo_ref[pi, 0], dims=nt)
        yg_scr[2 * pi] = _pdot(x0, m_ref[pi, 0, 0]) + yc[:, :n_tap]
        yg_scr[2 * pi + 1] = _pdot(x1, m_ref[pi, 0, 1]) + yc[:, n_tap:]
        if tail_rows:
            yc_lo = _pdot_lo_terms(hp_scr[tail, :], *part(wo_ref), dims=nt)
            yg_scr[2 * pi, tail, :] += _pdot_lo_terms(x0[rows - tail_rows:], *part(m_ref, 0)) + yc_lo[:, :n_tap]
            yg_scr[2 * pi + 1, tail, :] += (_pdot_lo_terms(x1[rows - tail_rows:], *part(m_ref, 1))
                                            + yc_lo[:, n_tap:])

    def scatter_tile(t, carry):
        r0 = pl.multiple_of(t * rt, rt)
        for hf in range(2):
            vs = [yg_scr[g, pl.ds(r0, rt), hf * LANES:(hf + 1) * LANES] for g in range(GROUPS_PER_SLAB)]
            outs = _block_transpose8(vs)
            for i in range(half_chunk):
                y_ref[pl.ds(r0 * S5_CHUNK + hf * half_chunk + i, rt, stride=S5_CHUNK), :] = outs[i]
        return carry

    lax.fori_loop(0, rows // rt, scatter_tile, 0)


def _s5_mixer(u, h0_re, h0_im, prep, bsz, t_len, tail):
    m, wsp, wop, a16 = prep
    parts = 2 if tail > 0 else 1
    n_tap = S5_CHUNK * SSM_GROUP
    st = 4 * SSM_STATE
    independent = t_len == S5_CHUNK
    if independent:
        nblk, rows, hrows = 1, bsz, bsz
    else:
        nblk, rows, hrows = bsz, t_len // S5_CHUNK, 1
    assert t_len % S5_CHUNK == 0 and rows % SUBLANES == 0, (bsz, t_len)
    h0p = jnp.concatenate([h0_re.reshape(bsz, N_PAIRS, 2 * SSM_STATE),
                           h0_im.reshape(bsz, N_PAIRS, 2 * SSM_STATE)], axis=2).astype(F32)
    h0p = h0p.transpose(1, 0, 2)[None] if independent else h0p[:, :, None, :]
    pp = PAIRS_PER_SLAB

    def wspec(*shape):
        return pl.BlockSpec((pp,) + shape, lambda q, b: (q,) + (0,) * len(shape))

    frames = rows * S5_CHUNK
    if tail <= 0:
        tail_rows = 0
    elif independent:
        tail_rows = rows
    else:
        tail_rows = min(rows, -(-tail // (S5_CHUNK * SUBLANES)) * SUBLANES)
    y, hf = pl.pallas_call(
        functools.partial(_s5_kernel, rows=rows, independent=independent, tail_rows=tail_rows),
        grid=(N_SLABS, nblk),
        in_specs=[pl.BlockSpec((None, frames, LANES), lambda q, b: (q, b, 0)),
                  pl.BlockSpec((None, pp, hrows, st), lambda q, b: (b, q, 0, 0)),
                  wspec(parts, 2, n_tap, n_tap), wspec(parts, 2 * n_tap, st), wspec(parts, 2 * n_tap, st),
                  wspec(2 * SUBLANES, st)],
        out_specs=[pl.BlockSpec((None, frames, LANES), lambda q, b: (q, b, 0)),
                   pl.BlockSpec((None, pp, hrows, st), lambda q, b: (b, q, 0, 0))],
        out_shape=[jax.ShapeDtypeStruct(u.shape, F32),
                   jax.ShapeDtypeStruct((nblk, N_PAIRS, hrows, st), F32)],
        scratch_shapes=[pltpu.VMEM((GROUPS_PER_SLAB, rows, n_tap), F32),
                        pltpu.VMEM((GROUPS_PER_SLAB, rows, n_tap), F32),
                        pltpu.VMEM((rows, st), F32), pltpu.VMEM((rows, st), F32)],
        compiler_params=_params("parallel", "parallel"),
        name="s5_core",
    )(u, h0p, m, wsp, wop, a16)
    hf = hf[0].transpose(1, 0, 2) if independent else hf[:, :, 0, :]
    hf_re = hf[:, :, :2 * SSM_STATE].reshape(bsz, N_GROUPS, SSM_STATE)
    hf_im = hf[:, :, 2 * SSM_STATE:].reshape(bsz, N_GROUPS, SSM_STATE)
    return y, hf_re, hf_im


def _conv_kernel(v_ref, left_ref, w_ref, b_ref, g_ref, beta_ref, y_ref, cn_ref, vbuf, shifted, *, tm):
    @pl.when(pl.program_id(1) == 0)
    def _():
        vbuf[0:CONV_HALO, :] = left_ref[...]

    vbuf[CONV_HALO:CONV_HALO + tm, :] = v_ref[...]
    first = CONV_HALO - (CONV_WIDTH - 1)
    span = tm + CONV_HALO - SUBLANES
    for r in range(1, SUBLANES):
        shifted[r - 1, 0:span, :] = vbuf[r:r + span, :]
    acc = jnp.zeros((tm, D_CONV), F32)
    for k in range(CONV_WIDTH):
        a, r = divmod(first + k, SUBLANES)
        src = vbuf if r == 0 else shifted.at[r - 1]
        acc = acc + w_ref[k:k + 1, :] * src[a * SUBLANES:a * SUBLANES + tm, :]
    y = acc + b_ref[...]
    mu = jnp.mean(y, axis=-1, keepdims=True)
    yc = y - mu
    var = jnp.mean(yc * yc, axis=-1, keepdims=True)
    yn = yc * lax.rsqrt(var + EPS) * g_ref[...] + beta_ref[...]
    y_ref[...] = yn * jax.nn.sigmoid(yn)
    cn_ref[...] = vbuf[tm + first:tm + CONV_HALO, :]
    vbuf[0:CONV_HALO, :] = vbuf[tm:tm + CONV_HALO, :]


def _conv_mixer(v, left, w, b, g, beta, layer, bsz, t_len, tm):
    left = jnp.pad(left.astype(F32), ((0, 0), (CONV_HALO - (CONV_WIDTH - 1), 0), (0, 0)))
    nt = t_len // tm
    vec = _layer_spec(layer, 1, D_CONV)
    return pl.pallas_call(
        functools.partial(_conv_kernel, tm=tm),
        grid=(bsz, nt),
        in_specs=[pl.BlockSpec((tm, D_CONV), lambda bi, j: (bi * nt + j, 0)),
                  pl.BlockSpec((None, CONV_HALO, D_CONV), lambda bi, j: (bi, 0, 0)),
                  _layer_spec(layer, CONV_WIDTH, D_CONV), vec, vec, vec],
        out_specs=[pl.BlockSpec((tm, D_CONV), lambda bi, j: (bi * nt + j, 0)),
                   pl.BlockSpec((None, CONV_WIDTH - 1, D_CONV), lambda bi, j: (bi, 0, 0))],
        out_shape=[jax.ShapeDtypeStruct((bsz * t_len, D_CONV), F32),
                   jax.ShapeDtypeStruct((bsz, CONV_WIDTH - 1, D_CONV), F32)],
        scratch_shapes=[pltpu.VMEM((tm + CONV_HALO, D_CONV), F32),
                        pltpu.VMEM((SUBLANES - 1, tm + CONV_HALO - SUBLANES, D_CONV), F32)],
        compiler_params=_params("parallel", "arbitrary"),
        name="conv_mixer",
    )(v, left, w, _vec(b), _vec(g), _vec(beta))


def _route(logits):
    lane_i = lax.broadcasted_iota(jnp.int32, logits.shape, 1)
    lane = lane_i.astype(F32)
    group_of_lane = (lane_i >> _log2(EXPERTS_PER_GROUP)).astype(F32)
    neg = -jnp.inf
    far = float(LANES)
    is_g = (lane_i >= N_EXPERTS) & (lane_i < N_EXPERTS + N_EXPERT_GROUPS)
    gl = jnp.where(is_g, logits, neg)
    g_max = jnp.max(gl, axis=-1, keepdims=True)
    g_lane = jnp.min(jnp.where(gl == g_max, lane, far), axis=-1, keepdims=True)
    g_gate = 1.0 / jnp.sum(jnp.exp(gl - g_max), axis=-1, keepdims=True)
    g_idx = g_lane - float(N_EXPERTS)
    in_group = (lane_i < N_EXPERTS) & (group_of_lane == g_idx)
    el = jnp.where(in_group, logits, neg)
    v1 = jnp.max(el, axis=-1, keepdims=True)
    i1 = jnp.min(jnp.where(el == v1, lane, far), axis=-1, keepdims=True)
    el2 = jnp.where(lane == i1, neg, el)
    v2 = jnp.max(el2, axis=-1, keepdims=True)
    i2 = jnp.min(jnp.where(el2 == v2, lane, far), axis=-1, keepdims=True)
    e2 = jnp.exp(v2 - v1)
    w1 = g_gate / (1.0 + e2)
    w2 = g_gate * e2 / (1.0 + e2)
    return i1, i2, w1, w2


ROUTE_E1, ROUTE_E2, ROUTE_W1, ROUTE_W2, ROUTE_RANK1, ROUTE_RANK2 = range(6)


def _outproj_kernel(x_ref, ys_ref, u_ref, yb_ref, d_ref, wglu_ref, wout_ref, nf_ref, rw_ref, rb_ref, cnt0_ref,
                    x1_ref, hn_ref, route_ref, rt_ref, cnt_ref, wglu_bf, wout_bf, rw_bf, *, tail_tiles):
    @pl.when(pl.program_id(0) == 0)
    def _():
        cnt_ref[...] = cnt0_ref[...]
        rw = rw_ref[...]
        rw_hi = rw.astype(BF16)
        rw_bf[:, 0:LANES] = rw_hi
        rw_bf[:, LANES:] = (rw - rw_hi.astype(F32)).astype(BF16)

    _cast_weight_once(wglu_ref, wglu_bf)
    _cast_weight_once(wout_ref, wout_bf)
    def mix_in(precise):
        ys = jnp.concatenate([ys_ref[q] for q in range(N_SLABS)], axis=1)
        u = jnp.concatenate([u_ref[q] for q in range(N_SLABS)], axis=1)
        z = jax.nn.gelu(ys + d_ref[...] * u)
        ya = z * jax.nn.sigmoid(_pdot(z, *_parts(wglu_bf, precise=precise)))
        mix = _pdot(jnp.concatenate([ya, yb_ref[...]], axis=1), *_parts(wout_bf, precise=precise))
        x1_ref[...] = x_ref[...] + mix

    _per_tile_precision(tail_tiles, wout_bf.shape[0] == 2, mix_in)
    x1 = x1_ref[...]
    hn = _rms(x1, nf_ref[...])
    _store_row_tiles(hn_ref, hn)
    h_hi = hn.astype(BF16)
    h_lo = (hn - h_hi.astype(F32)).astype(BF16)
    hw = jnp.dot(h_hi, rw_bf[...], preferred_element_type=F32)
    logits = (hw[:, :LANES] + hw[:, LANES:]
              + jnp.dot(h_lo, rw_bf[:, 0:LANES], preferred_element_type=F32) + rb_ref[...])
    i1, i2, w1, w2 = _route(logits)
    tm = logits.shape[0]
    lane_i = lax.broadcasted_iota(jnp.int32, logits.shape, 1)
    lane = lane_i.astype(F32)
    picked = jnp.where((lane == i1) | (lane == i2), 1.0, 0.0)
    earlier = (lax.broadcasted_iota(jnp.int32, (tm, tm), 0) > lax.broadcasted_iota(jnp.int32, (tm, tm), 1))
    prefix = _bdot(jnp.where(earlier, 1.0, 0.0), picked.astype(BF16)) + cnt_ref[...]
    rank1 = jnp.sum(jnp.where(lane == i1, prefix, 0.0), axis=-1, keepdims=True)
    rank2 = jnp.sum(jnp.where(lane == i2, prefix, 0.0), axis=-1, keepdims=True)
    cnt_ref[...] += jnp.sum(picked, axis=0, keepdims=True)
    rec = jnp.zeros_like(logits)
    for lane_id, val in ((ROUTE_E1, i1), (ROUTE_E2, i2), (ROUTE_W1, w1), (ROUTE_W2, w2),
                         (ROUTE_RANK1, rank1), (ROUTE_RANK2, rank2)):
        rec = jnp.where(lane_i == lane_id, val, rec)
    route_ref[...] = rec
    rt_ref[...] = rec.T[0:SUBLANES, :]


def _outproj(x, ys, u, yb, d, wglu, wout, nf, rw, rb, counts0, layer, tm, t_len, tail):
    n = x.shape[0]
    precise = tail > 0
    return pl.pallas_call(
        functools.partial(_outproj_kernel, tail_tiles=_stream_tail_tiles(t_len, tm, tail)),
        grid=(n // tm,),
        in_specs=[_row_spec(tm, D_MODEL), _slab_spec(tm), _slab_spec(tm), _row_spec(tm, D_CONV),
                  _layer_spec(layer, 1, D_SSM), _layer_spec(layer, D_SSM, D_SSM),
                  _layer_spec(layer, D_MODEL, D_MODEL), _layer_spec(layer, 1, D_MODEL),
                  _const_spec((D_MODEL, LANES)), _const_spec((1, LANES)), _const_spec((1, LANES))],
        out_specs=[_row_spec(tm, D_MODEL), _row_tile_spec(tm), _row_spec(tm, LANES),
                   pl.BlockSpec((SUBLANES, tm), lambda i: (0, i)), _const_spec((1, LANES))],
        out_shape=[jax.ShapeDtypeStruct((n, D_MODEL), F32), jax.ShapeDtypeStruct(_tiled_rows(n), F32),
                   jax.ShapeDtypeStruct((n, LANES), F32), jax.ShapeDtypeStruct((SUBLANES, n), F32),
                   jax.ShapeDtypeStruct((1, LANES), F32)],
        scratch_shapes=[_weight_scratch(D_SSM, D_SSM, precise), _weight_scratch(D_MODEL, D_MODEL, precise),
                        pltpu.VMEM((D_MODEL, 2 * LANES), BF16)],
        compiler_params=_params("arbitrary"),
        name="outproj_router",
    )(x, ys, u, yb, _vec(d), wglu, wout, _vec(nf), rw, rb, counts0)


PLAN_TILE_LANES = 2 * LANES
PLAN_EXPERT, PLAN_ROWS, PLAN_USED = range(3)


def _moe_plan_kernel(rt_ref, cnt_ref, cnt_first_ref, slot_ref, tile_ref, *, tms):
    cnt = cnt_ref[...]
    padded = jnp.ceil(cnt * (1.0 / tms)) * float(tms)
    r = lax.broadcasted_iota(jnp.int32, (LANES, LANES), 0)
    c = lax.broadcasted_iota(jnp.int32, (LANES, LANES), 1)
    ends = jnp.dot(padded, jnp.where(r <= c, 1.0, 0.0), precision=lax.Precision.HIGHEST,
                   preferred_element_type=F32)
    starts = ends - padded
    rt = rt_ref[...]
    e1, e2 = rt[ROUTE_E1:ROUTE_E1 + 1], rt[ROUTE_E2:ROUTE_E2 + 1]
    s1, s2 = rt[ROUTE_RANK1:ROUTE_RANK1 + 1], rt[ROUTE_RANK2:ROUTE_RANK2 + 1]
    tile = lax.broadcasted_iota(jnp.int32, (1, PLAN_TILE_LANES), 1).astype(F32)
    used = ends[:, N_EXPERTS - 1:N_EXPERTS] * (1.0 / tms)
    pos = jnp.minimum(tile, used - 1.0) * float(tms)
    t_exp = jnp.zeros_like(tile)
    t_fill = jnp.zeros_like(tile)
    for e in range(N_EXPERTS):
        st, en = starts[:, e:e + 1], ends[:, e:e + 1]
        s1 = s1 + jnp.where(e1 == float(e), st, 0.0)
        s2 = s2 + jnp.where(e2 == float(e), st, 0.0)
        mine = (pos >= st) & (pos < en)
        t_exp = t_exp + jnp.where(mine, float(e), 0.0)
        t_fill = t_fill + jnp.where(mine, st + cnt_first_ref[:, e:e + 1], 0.0)
    t_rows = jnp.where(tile < used, jnp.clip(t_fill - pos, 0.0, float(tms)), 0.0)
    slot_ref[...] = jnp.concatenate([s1, s2], axis=0).astype(jnp.int32)
    tile_ref[...] = jnp.concatenate(
        [t_exp, t_rows, jnp.broadcast_to(used, tile.shape), jnp.zeros((SUBLANES - 3, PLAN_TILE_LANES), F32)],
        axis=0).astype(jnp.int32)


def _moe_plan(route_t, counts, counts_first, tms, n_tiles):
    n = route_t.shape[1]
    assert n_tiles <= PLAN_TILE_LANES
    slots, tiles = pl.pallas_call(
        functools.partial(_moe_plan_kernel, tms=tms),
        out_shape=[jax.ShapeDtypeStruct((2, n), jnp.int32),
                   jax.ShapeDtypeStruct((SUBLANES, PLAN_TILE_LANES), jnp.int32)],
        compiler_params=pltpu.CompilerParams(vmem_limit_bytes=VMEM_LIMIT),
        name="moe_plan",
    )(route_t, counts, counts_first)
    return slots, tiles[PLAN_EXPERT, :n_tiles], tiles[PLAN_ROWS, :n_tiles], tiles[PLAN_USED, :1]


DMA_UNROLL = 8


HN_BUFFERS = 3
ZERO_FILL_ROWS = 128


def _dispatch_kernel(tr_ref, slot_ref, hn_ref, *rest, tm, tms, n_tiles, n_steps, fresh):
    xs_ref, zbuf, hbuf, in_sem, out_sem, zsem = rest if fresh else rest[1:]
    i = pl.program_id(0)

    def fetch(t):
        b = lax.rem(t, HN_BUFFERS)
        first = pl.multiple_of(t * (tm * ROW_SUB), tm * ROW_SUB)
        return pltpu.make_async_copy(hn_ref.at[pl.ds(first, tm * ROW_SUB)], hbuf.at[b], in_sem.at[b])

    def drain_scatter(t):
        b = lax.rem(t, HN_BUFFERS)
        for k in range(2):
            pltpu.make_async_copy(hbuf.at[b], xs_ref.at[pl.ds(0, tm * ROW_SUB)], out_sem.at[b]).wait()

    @pl.when(i == 0)
    def _():
        fetch(i).start()

    @pl.when(jnp.logical_and(i == 0, fresh))
    def _():
        zbuf[...] = jnp.zeros_like(zbuf)

        zrows = zbuf.shape[0] // ROW_SUB
        per_tile = tms // zrows

        def unfilled(p):
            return tr_ref[p // per_tile] < (lax.rem(p, per_tile) + 1) * zrows

        def fill(p, carry):
            @pl.when(unfilled(p))
            def _():
                first = pl.multiple_of(p * (zrows * ROW_SUB), zrows * ROW_SUB)
                pltpu.make_async_copy(zbuf, xs_ref.at[pl.ds(first, zrows * ROW_SUB)], zsem).start()
            return carry

        def drain(p, carry):
            @pl.when(unfilled(p))
            def _():
                pltpu.make_async_copy(zbuf, xs_ref.at[pl.ds(0, zrows * ROW_SUB)], zsem).wait()
            return carry

        lax.fori_loop(0, n_tiles * per_tile, fill, 0)
        lax.fori_loop(0, n_tiles * per_tile, drain, 0)

    @pl.when(i + 1 < n_steps)
    def _():
        fetch(i + 1).start()

    fetch(i).wait()
    b = lax.rem(i, HN_BUFFERS)
    rows = hbuf.at[b]

    def issue(r, carry):
        for k in range(2):
            pltpu.make_async_copy(_one_row(rows, r), _one_row(xs_ref, slot_ref[k, r]),
                                  out_sem.at[b]).start(priority=k)
        return carry

    lax.fori_loop(0, tm, issue, 0, unroll=DMA_UNROLL)

    @pl.when(i >= 1)
    def _():
        drain_scatter(i - 1)

    @pl.when(i == n_steps - 1)
    def _():
        drain_scatter(i)


def _dispatch(hn, slots, tile_rows, tms, tm, into=None):
    n = hn.shape[0] // ROW_SUB
    n_tiles = tile_rows.shape[0]
    fresh = into is None
    any_spec = pl.BlockSpec(memory_space=pl.ANY)
    return pl.pallas_call(
        functools.partial(_dispatch_kernel, tm=tm, tms=tms, n_tiles=n_tiles, n_steps=n // tm, fresh=fresh),
        grid_spec=pltpu.PrefetchScalarGridSpec(
            num_scalar_prefetch=1, grid=(n // tm,),
            in_specs=[pl.BlockSpec((2, tm), lambda i, tr: (0, i), memory_space=pltpu.SMEM), any_spec]
                     + ([] if fresh else [any_spec]),
            out_specs=any_spec,
            scratch_shapes=[pltpu.VMEM(_tiled_rows(min(tms, ZERO_FILL_ROWS)), F32),
                            pltpu.VMEM((HN_BUFFERS,) + _tiled_rows(tm), F32),
                            pltpu.SemaphoreType.DMA((HN_BUFFERS,)), pltpu.SemaphoreType.DMA((HN_BUFFERS,)),
                            pltpu.SemaphoreType.DMA]),
        out_shape=jax.ShapeDtypeStruct(_tiled_rows(n_tiles * tms), F32),
        input_output_aliases={} if fresh else {3: 0},
        compiler_params=_params("arbitrary"),
        name="moe_dispatch",
    )(tile_rows, slots, hn, *([] if fresh else [into]))


def _moe_kernel(te_ref, nu_ref, x_ref, wg_ref, wu_ref, wd_ref, y_ref, wg_bf, wu_bf, wd_bf):
    i = pl.program_id(0)
    in_use = i < nu_ref[0]
    new_expert = (i == 0) | (te_ref[i] != te_ref[jnp.maximum(i - 1, 0)])

    @pl.when(in_use & new_expert)
    def _():
        wg_bf[...] = wg_ref[...].astype(BF16)
        wu_bf[...] = wu_ref[...].astype(BF16)
        wd_bf[...] = wd_ref[...].astype(BF16)

    @pl.when(in_use)
    def _():
        h = _load_row_tiles(x_ref).astype(BF16)
        hg = jnp.dot(h, wg_bf[...], preferred_element_type=F32)
        hu = jnp.dot(h, wu_bf[...], preferred_element_type=F32)
        _store_row_tiles(y_ref, _bdot(hg * jax.nn.sigmoid(hg) * hu, wd_bf[...]))

    @pl.when(jnp.logical_not(in_use))
    def _():
        y_ref[...] = jnp.zeros_like(y_ref)


def _moe(xs, tile_expert, n_used, wg, wu, wd, layer, tms):
    n_slots = xs.shape[0] // ROW_SUB
    rows = pl.BlockSpec(_tiled_rows(tms), lambda i, te, nu: (jnp.minimum(i, nu[0] - 1), 0))
    out_rows = pl.BlockSpec(_tiled_rows(tms), lambda i, te, nu: (i, 0))

    def wspec(a, b):
        return pl.BlockSpec((None, None, a, b), lambda i, te, nu: (layer, te[i], 0, 0))

    return pl.pallas_call(
        _moe_kernel,
        grid_spec=pltpu.PrefetchScalarGridSpec(
            num_scalar_prefetch=2, grid=(n_slots // tms,),
            in_specs=[rows, wspec(D_MODEL, D_EXPERT), wspec(D_MODEL, D_EXPERT), wspec(D_EXPERT, D_MODEL)],
            out_specs=out_rows,
            scratch_shapes=[pltpu.VMEM((D_MODEL, D_EXPERT), BF16), pltpu.VMEM((D_MODEL, D_EXPERT), BF16),
                            pltpu.VMEM((D_EXPERT, D_MODEL), BF16)]),
        out_shape=jax.ShapeDtypeStruct(xs.shape, F32),
        compiler_params=_params("arbitrary"),
        name="moe",
    )(tile_expert, n_used, xs, wg, wu, wd)


def _ple_kernel(slot_ref, next_slot_ref, x_ref, route_ref, p_ref, np_ref, wple_ref, wgate_ref, nfin_ref,
                ys_ref, o_ref, ybuf, sem, wple_bf, wgate_bf, *, tm, n_steps, final):
    i = pl.program_id(0)
    _cast_weight_once(wple_ref, wple_bf)
    _cast_weight_once(wgate_ref, wgate_bf)

    def gather(slots, b):
        def issue(r, carry):
            for k in range(2):
                pltpu.make_async_copy(_one_row(ys_ref, slots[k, r]), _one_row(ybuf.at[b, k], r),
                                      sem.at[b]).start(priority=k)
            return carry

        lax.fori_loop(0, tm, issue, 0, unroll=DMA_UNROLL)

    @pl.when(i == 0)
    def _():
        gather(slot_ref, 0)

    @pl.when(i + 1 < n_steps)
    def _():
        gather(next_slot_ref, lax.rem(i + 1, 2))

    pe = _pdot(p_ref[...], *_parts(wple_bf))
    b = lax.rem(i, 2)
    for k in range(2):
        pltpu.make_async_copy(ys_ref.at[pl.ds(0, tm * ROW_SUB)], ybuf.at[b, k], sem.at[b]).wait()
    route = route_ref[...]
    x = (x_ref[...] + route[:, ROUTE_W1:ROUTE_W1 + 1] * _load_row_tiles(ybuf.at[b, 0])
         + route[:, ROUTE_W2:ROUTE_W2 + 1] * _load_row_tiles(ybuf.at[b, 1]))
    gate = jax.nn.sigmoid(_pdot(_rms(x, np_ref[...]), *_parts(wgate_bf)))
    out = x + pe * gate
    if final:
        out = _rms(out, nfin_ref[...])
    o_ref[...] = out


def _ple(x, route, slots, ys, p, npl, wple, wgate, nfin, layer, tm, final):
    n = x.shape[0]
    n_steps = n // tm
    return pl.pallas_call(
        functools.partial(_ple_kernel, tm=tm, n_steps=n_steps, final=final),
        grid=(n_steps,),
        in_specs=[pl.BlockSpec((2, tm), lambda i: (0, i), memory_space=pltpu.SMEM),
                  pl.BlockSpec((2, tm), lambda i: (0, jnp.minimum(i + 1, n_steps - 1)),
                               memory_space=pltpu.SMEM),
                  _row_spec(tm, D_MODEL), _row_spec(tm, LANES),
                  pl.BlockSpec((None, tm, D_PLE), lambda i: (layer, i, 0)),
                  _layer_spec(layer, 1, D_MODEL), _layer_spec(layer, D_PLE, D_MODEL),
                  _layer_spec(layer, D_MODEL, D_MODEL), _const_spec((1, D_MODEL)),
                  pl.BlockSpec(memory_space=pl.ANY)],
        out_specs=_row_spec(tm, D_MODEL),
        out_shape=jax.ShapeDtypeStruct((n, D_MODEL), F32),
        scratch_shapes=[pltpu.VMEM((2, 2) + _tiled_rows(tm), F32), pltpu.SemaphoreType.DMA((2,)),
                        _weight_scratch(D_PLE, D_MODEL, False), _weight_scratch(D_MODEL, D_MODEL, False)],
        compiler_params=_params("arbitrary"),
        name="combine_ple",
    )(slots, slots, x, route, p, _vec(npl), wple, wgate, nfin.reshape(1, D_MODEL), ys)


def _mixers(x, h0_re, h0_im, conv_left, w, s5, router_w, router_b, counts0, layer, bsz, t_len, tm, tm_conv,
            tail):
    u, v = _inproj(x, w["norm_mix"], w["w_in"], layer, min(2 * tm, x.shape[0]), t_len, tail)
    ys, hf_re, hf_im = _s5_mixer(u, h0_re, h0_im, s5, bsz, t_len, tail)
    yb, conv_new = _conv_mixer(v, conv_left, w["conv_w"], w["conv_b"], w["conv_ln_g"], w["conv_ln_b"],
                               layer, bsz, t_len, tm_conv)
    x1, hn, route, route_t, counts = _outproj(x, ys, u, yb, w["ssm_d"], w["w_ssm_glu"], w["w_out"],
                                              w["norm_ffn"], router_w, router_b, counts0, layer, tm, t_len, tail)
    return dict(x1=x1, hn=hn, route=route, route_t=route_t, counts=counts, state=(hf_re, hf_im, conv_new))


def _moe_and_ple(sets, ps, tms_rows, w, layer, tms, final):
    sizes = [s["x1"].shape[0] for s in sets]
    n_tiles = 2 * sum(sizes) // tms + N_EXPERTS
    route_t = jnp.concatenate([s["route_t"] for s in sets], axis=1)
    slots, tile_expert, tile_rows, n_used = _moe_plan(route_t, sets[-1]["counts"], sets[0]["counts"], tms,
                                                     n_tiles)
    starts = [sum(sizes[:j]) for j in range(len(sets))]
    set_slots = [slots[:, a:a + n] for a, n in zip(starts, sizes)]
    xsort = None
    for s, sl, tm in zip(sets, set_slots, tms_rows):
        xsort = _dispatch(s["hn"], sl, tile_rows, tms, tm, into=xsort)
    ysort = _moe(xsort, tile_expert, n_used, w["expert_w_gate"], w["expert_w_up"], w["expert_w_down"],
                 layer, tms)
    return [_ple(s["x1"], s["route"], sl, ysort, p, w["norm_ple"], w["ple_w"], w["ple_gate_w"],
                 w["norm_final"], layer, tm, final)
            for s, sl, p, tm in zip(sets, set_slots, ps, tms_rows)]


def kernel(x_prompt, x_sample, p_prompt, p_sample, state_ssm_re, state_ssm_im, cache_conv, norm_mix, w_in, ssm_a_re, ssm_a_im, ssm_b_re, ssm_b_im, ssm_c_re, ssm_c_im, ssm_d, ssm_log_dt, w_ssm_glu, conv_w, conv_b, conv_ln_g, conv_ln_b, w_out, norm_ffn, router_group_w, router_group_b, router_expert_w, router_expert_b, expert_w_gate, expert_w_up, expert_w_down, norm_ple, ple_w, ple_gate_w, norm_final):
    depth = w_in.shape[0]
    bp, tp, _ = x_prompt.shape
    bs, ts, _ = x_sample.shape
    xp = x_prompt.reshape(bp * tp, D_MODEL)
    xs = x_sample.reshape(bs * ts, D_MODEL)
    pp = p_prompt.reshape(depth, bp * tp, D_PLE)
    ps = p_sample.reshape(depth, bs * ts, D_PLE)
    zero_state = jnp.zeros((bp, N_GROUPS, SSM_STATE), F32)
    zero_conv = jnp.zeros((bp, CONV_WIDTH - 1, D_CONV), F32)
    pad_lanes = LANES - N_EXPERTS - N_EXPERT_GROUPS
    w = {"norm_mix": norm_mix, "w_in": w_in, "ssm_d": ssm_d, "w_ssm_glu": w_ssm_glu, "conv_w": conv_w,
         "conv_b": conv_b, "conv_ln_g": conv_ln_g, "conv_ln_b": conv_ln_b, "w_out": w_out,
         "norm_ffn": norm_ffn, "expert_w_gate": expert_w_gate, "expert_w_up": expert_w_up,
         "expert_w_down": expert_w_down, "norm_ple": norm_ple, "ple_w": ple_w, "ple_gate_w": ple_gate_w,
         "norm_final": norm_final}
    no_picks = jnp.zeros((1, LANES), F32)
    outs = {k: [] for k in ("pr_re", "pr_im", "pr_conv", "sm_re", "sm_im", "sm_conv")}
    for i in range(depth):
        s5 = _s5_prep(ssm_a_re[i], ssm_a_im[i], ssm_log_dt[i], ssm_b_re[i], ssm_b_im[i], ssm_c_re[i],
                      ssm_c_im[i])
        router_w = jnp.pad(jnp.concatenate([router_expert_w[i], router_group_w[i]], axis=1),
                           ((0, 0), (0, pad_lanes)))
        router_b = jnp.pad(jnp.concatenate([router_expert_b[i], router_group_b[i]]),
                           (0, pad_lanes)).reshape(1, LANES)
        final = i == depth - 1
        tail = 0 if final else PRECISE_TAIL
        mp = _mixers(xp, zero_state, zero_state, zero_conv, w, s5, router_w, router_b, no_picks, i,
                     bp, tp, tm=TOKEN_TILE, tm_conv=2 * TOKEN_TILE, tail=tail)
        ms = _mixers(xs, state_ssm_re[i], state_ssm_im[i], cache_conv[i], w, s5, router_w, router_b,
                     mp["counts"], i, bs, ts, tm=bs * ts, tm_conv=ts, tail=tail)
        xp, xs = _moe_and_ple([mp, ms], [pp, ps], [TOKEN_TILE, bs * ts], w, i, SLOT_TILE, final)
        for key, val in zip(("pr_re", "pr_im", "pr_conv"), mp["state"]):
            outs[key].append(val)
        for key, val in zip(("sm_re", "sm_im", "sm_conv"), ms["state"]):
            outs[key].append(val)
    return (xp.reshape(bp, tp, D_MODEL), xs.reshape(bs, ts, D_MODEL),
            jnp.stack(outs["pr_re"]), jnp.stack(outs["pr_im"]), jnp.stack(outs["pr_conv"]),
            jnp.stack(outs["sm_re"]), jnp.stack(outs["sm_im"]), jnp.stack(outs["sm_conv"]))
```

```python
import functools

import jax
import jax.numpy as jnp
from jax import lax
from jax.experimental import pallas as pl
from jax.experimental.pallas import tpu as pltpu

F32 = jnp.float32
BF16 = jnp.bfloat16

D_MODEL = 1024
D_SSM = 512
SSM_GROUP = 16
N_GROUPS = D_SSM // SSM_GROUP
N_PAIRS = N_GROUPS // 2
SSM_STATE = 64
D_CONV = 512
CONV_WIDTH = 31
CONV_HALO = 32
N_EXPERT_GROUPS = 4
EXPERTS_PER_GROUP = 8
N_EXPERTS = 32
D_EXPERT = 256
D_PLE = 256
EPS = 1e-6
S5_CHUNK = 16
S5_RELAYOUT_ROWS = 256
S5_SCAN_UNROLL = 4
LANES = 128
SUBLANES = 8
N_SLABS = D_SSM // LANES
GROUPS_PER_SLAB = LANES // SSM_GROUP
PAIRS_PER_SLAB = GROUPS_PER_SLAB // 2
VMEM_LIMIT = 56 * 1024 * 1024
TOKEN_TILE = 512
SLOT_TILE = 512
PRECISE_TAIL = 1024


def _log2(n):
    assert n & (n - 1) == 0, n
    return n.bit_length() - 1


def _params(*sem):
    return pltpu.CompilerParams(dimension_semantics=sem, vmem_limit_bytes=VMEM_LIMIT)


def _rms(x, g):
    return x * lax.rsqrt(jnp.mean(x * x, axis=-1, keepdims=True) + EPS) * g


def _bdot(a, b):
    return jnp.dot(a.astype(BF16), b, preferred_element_type=F32)


def _split_bf16(a):
    hi = a.astype(BF16)
    return hi, (a - hi.astype(F32)).astype(BF16)


def _pdot(a, w_hi, w_lo=None, dims=(((1,), (0,)), ((), ()))):
    def mm(x, w):
        return lax.dot_general(x, w, dims, preferred_element_type=F32)

    if w_lo is None:
        return mm(a.astype(BF16), w_hi)
    a_hi, a_lo = _split_bf16(a)
    return mm(a_hi, w_hi) + (mm(a_hi, w_lo) + mm(a_lo, w_hi))


def _pdot_lo_terms(a, w_hi, w_lo, dims=(((1,), (0,)), ((), ()))):
    a_hi, a_lo = _split_bf16(a)
    return (lax.dot_general(a_hi, w_lo, dims, preferred_element_type=F32)
            + lax.dot_general(a_lo, w_hi, dims, preferred_element_type=F32))


def _stream_tail_tiles(t_len, tm, tail):
    tiles = max(t_len // tm, 1)
    return tiles, (max(t_len - tail, 0) // tm if tm < t_len else 0)


def _per_tile_precision(tail_tiles, has_lo, body):
    tiles, first = tail_tiles
    if not has_lo or first == 0:
        body(has_lo)
        return
    in_tail = lax.rem(pl.program_id(0), tiles) >= first
    pl.when(in_tail)(functools.partial(body, True))
    pl.when(jnp.logical_not(in_tail))(functools.partial(body, False))


def _parts(wbf_ref, rows=slice(None), precise=True):
    return wbf_ref[0, rows, :], (wbf_ref[1, rows, :] if precise and wbf_ref.shape[0] == 2 else None)


def _row_spec(tm, width):
    return pl.BlockSpec((tm, width), lambda i: (i, 0))


ROW_SUB = D_MODEL // LANES


def _tiled_rows(n):
    return (n * ROW_SUB, LANES)


def _row_tile_spec(tm):
    return pl.BlockSpec(_tiled_rows(tm), lambda i, *_: (i, 0))


def _one_row(ref, r):
    return ref.at[pl.ds(pl.multiple_of(r * ROW_SUB, ROW_SUB), ROW_SUB)]


def _store_row_tiles(ref, rows):
    n = rows.shape[0]
    for s in range(ROW_SUB):
        ref[pl.ds(s, n, stride=ROW_SUB), :] = rows[:, s * LANES:(s + 1) * LANES]


def _load_row_tiles(ref):
    n = ref.shape[0] // ROW_SUB
    return jnp.concatenate([ref[pl.ds(s, n, stride=ROW_SUB), :] for s in range(ROW_SUB)], axis=1)


def _slab_spec(tm):
    return pl.BlockSpec((N_SLABS, tm, LANES), lambda i: (0, i, 0))


def _const_spec(shape):
    return pl.BlockSpec(shape, lambda i: (0,) * len(shape))


def _layer_spec(layer, *shape):
    return pl.BlockSpec((None,) + shape, lambda *_: (layer,) + (0,) * len(shape))


def _vec(stacked):
    return stacked.reshape(stacked.shape[0], 1, stacked.shape[1])


def _cast_weight_once(w_ref, wbf_ref):
    @pl.when(pl.program_id(0) == 0)
    def _():
        w = w_ref[...]
        hi = w.astype(BF16)
        wbf_ref[0] = hi
        if wbf_ref.shape[0] == 2:
            wbf_ref[1] = (w - hi.astype(F32)).astype(BF16)


def _weight_scratch(k, n, precise):
    return pltpu.VMEM((2 if precise else 1, k, n), BF16)


def _inproj_kernel(x_ref, g_ref, w_ref, u_ref, v_ref, wbf, *, tail_tiles):
    _cast_weight_once(w_ref, wbf)

    def body(precise):
        hn = _rms(x_ref[...], g_ref[...])
        proj = _pdot(hn, *_parts(wbf, precise=precise))
        for q in range(N_SLABS):
            u_ref[q] = proj[:, q * LANES:(q + 1) * LANES]
        v_ref[...] = proj[:, D_SSM:D_SSM + D_CONV] * jax.nn.sigmoid(proj[:, D_SSM + D_CONV:])

    _per_tile_precision(tail_tiles, wbf.shape[0] == 2, body)


def _inproj(x, g, w, layer, tm, t_len, tail):
    n = x.shape[0]
    d_in = w.shape[2]
    precise = tail > 0
    return pl.pallas_call(
        functools.partial(_inproj_kernel, tail_tiles=_stream_tail_tiles(t_len, tm, tail)),
        grid=(n // tm,),
        in_specs=[_row_spec(tm, D_MODEL), _layer_spec(layer, 1, D_MODEL), _layer_spec(layer, D_MODEL, d_in)],
        out_specs=[_slab_spec(tm), _row_spec(tm, D_CONV)],
        out_shape=[jax.ShapeDtypeStruct((N_SLABS, n, LANES), F32), jax.ShapeDtypeStruct((n, D_CONV), F32)],
        scratch_shapes=[_weight_scratch(D_MODEL, d_in, precise)],
        compiler_params=_params("arbitrary"),
        name="inproj",
    )(x, _vec(g), w)


def _s5_prep_kernel(ar_ref, ai_ref, ldt_ref, bre_ref, bim_ref, cre_ref, cim_ref,
                    m_ref, ws_ref, wot_ref, atab_ref, wt_re, wt_im, br_re, br_im):
    n_tap = S5_CHUNK * SSM_GROUP
    st = 4 * SSM_STATE
    nt = (((1,), (1,)), ((), ()))
    hi = lax.Precision.HIGHEST
    lane = lax.broadcasted_iota(jnp.int32, (SSM_GROUP, n_tap), 1)
    ws_rows, wot_rows, atab = [], [], jnp.zeros((2 * SUBLANES, st), F32)
    for gi in range(2):
        ar, ai = ar_ref[gi], ai_ref[gi]
        dt = jnp.exp(ldt_ref[gi])
        k = lax.broadcasted_iota(jnp.int32, (S5_CHUNK + SUBLANES, SSM_STATE), 0).astype(F32)
        mag = jnp.exp(k * (dt * ar))
        ang = k * (dt * ai)
        p_re, p_im = mag * jnp.cos(ang), mag * jnp.sin(ang)
        inv = 1.0 / (ar * ar + ai * ai)
        ab_re, ab_im = p_re[1:2], p_im[1:2]
        ia_re, ia_im = ar * inv, -ai * inv
        coef_re = (ab_re - 1.0) * ia_re - ab_im * ia_im
        coef_im = (ab_re - 1.0) * ia_im + ab_im * ia_re
        bre, bim = bre_ref[gi], bim_ref[gi]
        bb_re = coef_re * bre - coef_im * bim
        bb_im = coef_re * bim + coef_im * bre
        cre, cim = cre_ref[gi], cim_ref[gi]
        for kk in range(S5_CHUNK + 1):
            pr, pi = p_re[kk:kk + 1], p_im[kk:kk + 1]
            rows = slice(kk * SSM_GROUP, (kk + 1) * SSM_GROUP)
            wt_re[rows, :] = pr * cre - pi * cim
            wt_im[rows, :] = -pi * cre - pr * cim
            if kk < S5_CHUNK:
                back = slice((S5_CHUNK - 1 - kk) * SSM_GROUP, (S5_CHUNK - kk) * SSM_GROUP)
                br_re[back, :] = pr * bb_re - pi * bb_im
                br_im[back, :] = pi * bb_re + pr * bb_im
        kcat = (lax.dot_general(bb_re, wt_re[0:n_tap, :], nt, precision=hi, preferred_element_type=F32)
                + lax.dot_general(bb_im, wt_im[0:n_tap, :], nt, precision=hi, preferred_element_type=F32))
        for s in range(S5_CHUNK):
            shifted = kcat if s == 0 else pltpu.roll(kcat, s * SSM_GROUP, 1)
            rows = slice(s * SSM_GROUP, (s + 1) * SSM_GROUP)
            m_ref[0, gi, rows, :], m_ref[1, gi, rows, :] = _split_bf16(
                jnp.where(lane >= s * SSM_GROUP, shifted, 0.0))
        def place(v_re, v_im):
            zero = jnp.zeros_like(v_re)
            parts = [v_re, zero, v_im, zero] if gi == 0 else [zero, v_re, zero, v_im]
            return jnp.concatenate(parts, axis=1)

        ws_rows.append(place(br_re[...], br_im[...]))
        wot_rows.append(place(wt_re[SSM_GROUP:, :], wt_im[SSM_GROUP:, :]))
        kc = float(S5_CHUNK) * lax.broadcasted_iota(jnp.int32, (2 * SUBLANES, SSM_STATE), 0).astype(F32)
        magc = jnp.exp(kc * (dt * ar))
        angc = kc * (dt * ai)
        atab = atab + place(magc * jnp.cos(angc), magc * jnp.sin(angc))
    ws_ref[0], ws_ref[1] = _split_bf16(jnp.concatenate(ws_rows, axis=0))
    wot_ref[0], wot_ref[1] = _split_bf16(jnp.concatenate(wot_rows, axis=0))
    atab_ref[...] = atab


def _s5_prep(a_re, a_im, log_dt, b_re, b_im, c_re, c_im):
    p, n, c = N_PAIRS, SSM_STATE, SSM_GROUP
    n_tap = S5_CHUNK * c
    st = 4 * n

    def pspec(*shape):
        return pl.BlockSpec((None,) + shape, lambda i: (i,) + (0,) * len(shape))

    def pairs(a, *shape):
        return a.reshape((p, 2) + shape)

    return pl.pallas_call(
        _s5_prep_kernel,
        grid=(p,),
        in_specs=[pspec(2, 1, n), pspec(2, 1, n), pspec(2, 1, 1), pspec(2, c, n), pspec(2, c, n),
                  pspec(2, c, n), pspec(2, c, n)],
        out_specs=[pspec(2, 2, n_tap, n_tap), pspec(2, 2 * n_tap, st), pspec(2, 2 * n_tap, st),
                   pspec(2 * SUBLANES, st)],
        out_shape=[jax.ShapeDtypeStruct((p, 2, 2, n_tap, n_tap), BF16),
                   jax.ShapeDtypeStruct((p, 2, 2 * n_tap, st), BF16),
                   jax.ShapeDtypeStruct((p, 2, 2 * n_tap, st), BF16),
                   jax.ShapeDtypeStruct((p, 2 * SUBLANES, st), F32)],
        scratch_shapes=[pltpu.VMEM((n_tap + c, n), F32), pltpu.VMEM((n_tap + c, n), F32),
                        pltpu.VMEM((n_tap, n), F32), pltpu.VMEM((n_tap, n), F32)],
        compiler_params=_params("parallel"),
        name="s5_prep",
    )(pairs(a_re, 1, n), pairs(a_im, 1, n), pairs(log_dt, 1, 1),
      pairs(jnp.swapaxes(b_re, 1, 2), c, n), pairs(jnp.swapaxes(b_im, 1, 2), c, n),
      pairs(c_re, c, n), pairs(c_im, c, n))


def _block_transpose8(vs):
    lane = lax.broadcasted_iota(jnp.int32, vs[0].shape, 1)
    blk = lane >> _log2(SSM_GROUP)
    for d in (GROUPS_PER_SLAB >> s for s in range(1, _log2(GROUPS_PER_SLAB) + 1)):
        keep = (blk & d) == 0
        new = list(vs)
        for i in range(GROUPS_PER_SLAB):
            if i & d == 0:
                a, b = vs[i], vs[i + d]
                new[i] = jnp.where(keep, a, pltpu.roll(b, d * SSM_GROUP, 1))
                new[i + d] = jnp.where(keep, pltpu.roll(a, LANES - d * SSM_GROUP, 1), b)
        vs = new
    return vs


def _cmul(ar, ai, xr, xi):
    return ar * xr - ai * xi, ar * xi + ai * xr


def _s5_kernel(u_ref, h0_ref, m_ref, ws_ref, wo_ref, a_ref, y_ref, hf_ref, x_scr, yg_scr, s_scr, hp_scr,
               *, rows, independent, tail_rows):
    half = 2 * SSM_STATE
    tail = pl.ds(rows - tail_rows, tail_rows)
    rt = min(rows, S5_RELAYOUT_ROWS)
    half_chunk = S5_CHUNK // 2

    def gather_tile(t, carry):
        r0 = pl.multiple_of(t * rt, rt)
        for hf in range(2):
            vs = [u_ref[pl.ds(r0 * S5_CHUNK + hf * half_chunk + i, rt, stride=S5_CHUNK), :]
                  for i in range(half_chunk)]
            outs = _block_transpose8(vs)
            for g in range(GROUPS_PER_SLAB):
                x_scr[g, pl.ds(r0, rt), hf * LANES:(hf + 1) * LANES] = outs[g]
        return carry

    lax.fori_loop(0, rows // rt, gather_tile, 0)

    row = lax.broadcasted_iota(jnp.int32, (SUBLANES, half), 0)
    n_tap = S5_CHUNK * SSM_GROUP
    for pi in range(PAIRS_PER_SLAB):
        def part(w_ref, *idx):
            return w_ref[(pi, 0) + idx], w_ref[(pi, 1) + idx]

        x0 = x_scr[2 * pi]
        x1 = x_scr[2 * pi + 1]
        x01 = jnp.concatenate([x0, x1], axis=1)
        s_scr[...] = _pdot(x01, ws_ref[pi, 0])
        if tail_rows:
            s_scr[tail, :] += _pdot_lo_terms(x01[rows - tail_rows:], *part(ws_ref))
        ap = a_ref[pi]
        h0 = h0_ref[pi]
        if independent:
            hp_scr[...] = h0
            s = s_scr[...]
            n_re, n_im = _cmul(ap[1:2, :half], ap[1:2, half:], h0[:, :half], h0[:, half:])
            hf_ref[pi] = jnp.concatenate([n_re + s[:, :half], n_im + s[:, half:]], axis=1)
        else:
            pw_re, pw_im = ap[0:SUBLANES, :half], ap[0:SUBLANES, half:]

            def scan_tile(t, carry):
                h_re, h_im = carry
                r0 = pl.multiple_of(t * SUBLANES, SUBLANES)
                s = s_scr[pl.ds(r0, SUBLANES), :]
                t_re, t_im = s[:, :half], s[:, half:]
                for d in (1, 2, 4):
                    sh_re = jnp.where(row >= d, pltpu.roll(t_re, d, 0), 0.0)
                    sh_im = jnp.where(row >= d, pltpu.roll(t_im, d, 0), 0.0)
                    m_re, m_im = _cmul(ap[d:d + 1, :half], ap[d:d + 1, half:], sh_re, sh_im)
                    t_re, t_im = t_re + m_re, t_im + m_im
                e_re = jnp.where(row >= 1, pltpu.roll(t_re, 1, 0), 0.0)
                e_im = jnp.where(row >= 1, pltpu.roll(t_im, 1, 0), 0.0)
                c_re, c_im = _cmul(pw_re, pw_im, h_re, h_im)
                hp_scr[pl.ds(r0, SUBLANES), :] = jnp.concatenate([e_re + c_re, e_im + c_im], axis=1)
                o_re, o_im = _cmul(ap[SUBLANES:SUBLANES + 1, :half], ap[SUBLANES:SUBLANES + 1, half:],
                                   h_re, h_im)
                last = SUBLANES - 1
                n_re = jnp.broadcast_to(t_re[last:last + 1], h_re.shape) + o_re
                n_im = jnp.broadcast_to(t_im[last:last + 1], h_im.shape) + o_im
                return n_re, n_im

            init = (jnp.broadcast_to(h0[:, :half], (SUBLANES, half)),
                    jnp.broadcast_to(h0[:, half:], (SUBLANES, half)))
            h_re, h_im = lax.fori_loop(0, rows // SUBLANES, scan_tile, init, unroll=S5_SCAN_UNROLL)
            hf_ref[pi] = jnp.concatenate([h_re[0:1], h_im[0:1]], axis=1)
        nt = (((1,), (1,)), ((), ()))
        yc = _pdot(hp_scr[...], wo_ref[pi, 0], dims=nt)
        yg_scr[2 * pi] = _pdot(x0, m_ref[pi, 0, 0]) + yc[:, :n_tap]
        yg_scr[2 * pi + 1] = _pdot(x1, m_ref[pi, 0, 1]) + yc[:, n_tap:]
        if tail_rows:
            yc_lo = _pdot_lo_terms(hp_scr[tail, :], *part(wo_ref), dims=nt)
            yg_scr[2 * pi, tail, :] += _pdot_lo_terms(x0[rows - tail_rows:], *part(m_ref, 0)) + yc_lo[:, :n_tap]
            yg_scr[2 * pi + 1, tail, :] += (_pdot_lo_terms(x1[rows - tail_rows:], *part(m_ref, 1))
                                            + yc_lo[:, n_tap:])

    def scatter_tile(t, carry):
        r0 = pl.multiple_of(t * rt, rt)
        for hf in range(2):
            vs = [yg_scr[g, pl.ds(r0, rt), hf * LANES:(hf + 1) * LANES] for g in range(GROUPS_PER_SLAB)]
            outs = _block_transpose8(vs)
            for i in range(half_chunk):
                y_ref[pl.ds(r0 * S5_CHUNK + hf * half_chunk + i, rt, stride=S5_CHUNK), :] = outs[i]
        return carry

    lax.fori_loop(0, rows // rt, scatter_tile, 0)


def _s5_mixer(u, h0_re, h0_im, prep, bsz, t_len, tail):
    m, wsp, wop, a16 = prep
    parts = 2 if tail > 0 else 1
    n_tap = S5_CHUNK * SSM_GROUP
    st = 4 * SSM_STATE
    independent = t_len == S5_CHUNK
    if independent:
        nblk, rows, hrows = 1, bsz, bsz
    else:
        nblk, rows, hrows = bsz, t_len // S5_CHUNK, 1
    assert t_len % S5_CHUNK == 0 and rows % SUBLANES == 0, (bsz, t_len)
    h0p = jnp.concatenate([h0_re.reshape(bsz, N_PAIRS, 2 * SSM_STATE),
                           h0_im.reshape(bsz, N_PAIRS, 2 * SSM_STATE)], axis=2).astype(F32)
    h0p = h0p.transpose(1, 0, 2)[None] if independent else h0p[:, :, None, :]
    pp = PAIRS_PER_SLAB

    def wspec(*shape):
        return pl.BlockSpec((pp,) + shape, lambda q, b: (q,) + (0,) * len(shape))

    frames = rows * S5_CHUNK
    if tail <= 0:
        tail_rows = 0
    elif independent:
        tail_rows = rows
    else:
        tail_rows = min(rows, -(-tail // (S5_CHUNK * SUBLANES)) * SUBLANES)
    y, hf = pl.pallas_call(
        functools.partial(_s5_kernel, rows=rows, independent=independent, tail_rows=tail_rows),
        grid=(N_SLABS, nblk),
        in_specs=[pl.BlockSpec((None, frames, LANES), lambda q, b: (q, b, 0)),
                  pl.BlockSpec((None, pp, hrows, st), lambda q, b: (b, q, 0, 0)),
                  wspec(parts, 2, n_tap, n_tap), wspec(parts, 2 * n_tap, st), wspec(parts, 2 * n_tap, st),
                  wspec(2 * SUBLANES, st)],
        out_specs=[pl.BlockSpec((None, frames, LANES), lambda q, b: (q, b, 0)),
                   pl.BlockSpec((None, pp, hrows, st), lambda q, b: (b, q, 0, 0))],
        out_shape=[jax.ShapeDtypeStruct(u.shape, F32),
                   jax.ShapeDtypeStruct((nblk, N_PAIRS, hrows, st), F32)],
        scratch_shapes=[pltpu.VMEM((GROUPS_PER_SLAB, rows, n_tap), F32),
                        pltpu.VMEM((GROUPS_PER_SLAB, rows, n_tap), F32),
                        pltpu.VMEM((rows, st), F32), pltpu.VMEM((rows, st), F32)],
        compiler_params=_params("parallel", "parallel"),
        name="s5_core",
    )(u, h0p, m, wsp, wop, a16)
    hf = hf[0].transpose(1, 0, 2) if independent else hf[:, :, 0, :]
    hf_re = hf[:, :, :2 * SSM_STATE].reshape(bsz, N_GROUPS, SSM_STATE)
    hf_im = hf[:, :, 2 * SSM_STATE:].reshape(bsz, N_GROUPS, SSM_STATE)
    return y, hf_re, hf_im


def _conv_kernel(v_ref, left_ref, w_ref, b_ref, g_ref, beta_ref, y_ref, cn_ref, vbuf, shifted, *, tm):
    @pl.when(pl.program_id(1) == 0)
    def _():
        vbuf[0:CONV_HALO, :] = left_ref[...]

    vbuf[CONV_HALO:CONV_HALO + tm, :] = v_ref[...]
    first = CONV_HALO - (CONV_WIDTH - 1)
    span = tm + CONV_HALO - SUBLANES
    for r in range(1, SUBLANES):
        shifted[r - 1, 0:span, :] = vbuf[r:r + span, :]
    acc = jnp.zeros((tm, D_CONV), F32)
    for k in range(CONV_WIDTH):
        a, r = divmod(first + k, SUBLANES)
        src = vbuf if r == 0 else shifted.at[r - 1]
        acc = acc + w_ref[k:k + 1, :] * src[a * SUBLANES:a * SUBLANES + tm, :]
    y = acc + b_ref[...]
    mu = jnp.mean(y, axis=-1, keepdims=True)
    yc = y - mu
    var = jnp.mean(yc * yc, axis=-1, keepdims=True)
    yn = yc * lax.rsqrt(var + EPS) * g_ref[...] + beta_ref[...]
    y_ref[...] = yn * jax.nn.sigmoid(yn)
    cn_ref[...] = vbuf[tm + first:tm + CONV_HALO, :]
    vbuf[0:CONV_HALO, :] = vbuf[tm:tm + CONV_HALO, :]


def _conv_mixer(v, left, w, b, g, beta, layer, bsz, t_len, tm):
    left = jnp.pad(left.astype(F32), ((0, 0), (CONV_HALO - (CONV_WIDTH - 1), 0), (0, 0)))
    nt = t_len // tm
    vec = _layer_spec(layer, 1, D_CONV)
    return pl.pallas_call(
        functools.partial(_conv_kernel, tm=tm),
        grid=(bsz, nt),
        in_specs=[pl.BlockSpec((tm, D_CONV), lambda bi, j: (bi * nt + j, 0)),
                  pl.BlockSpec((None, CONV_HALO, D_CONV), lambda bi, j: (bi, 0, 0)),
                  _layer_spec(layer, CONV_WIDTH, D_CONV), vec, vec, vec],
        out_specs=[pl.BlockSpec((tm, D_CONV), lambda bi, j: (bi * nt + j, 0)),
                   pl.BlockSpec((None, CONV_WIDTH - 1, D_CONV), lambda bi, j: (bi, 0, 0))],
        out_shape=[jax.ShapeDtypeStruct((bsz * t_len, D_CONV), F32),
                   jax.ShapeDtypeStruct((bsz, CONV_WIDTH - 1, D_CONV), F32)],
        scratch_shapes=[pltpu.VMEM((tm + CONV_HALO, D_CONV), F32),
                        pltpu.VMEM((SUBLANES - 1, tm + CONV_HALO - SUBLANES, D_CONV), F32)],
        compiler_params=_params("parallel", "arbitrary"),
        name="conv_mixer",
    )(v, left, w, _vec(b), _vec(g), _vec(beta))


def _route(logits):
    lane_i = lax.broadcasted_iota(jnp.int32, logits.shape, 1)
    lane = lane_i.astype(F32)
    group_of_lane = (lane_i >> _log2(EXPERTS_PER_GROUP)).astype(F32)
    neg = -jnp.inf
    far = float(LANES)
    is_g = (lane_i >= N_EXPERTS) & (lane_i < N_EXPERTS + N_EXPERT_GROUPS)
    gl = jnp.where(is_g, logits, neg)
    g_max = jnp.max(gl, axis=-1, keepdims=True)
    g_lane = jnp.min(jnp.where(gl == g_max, lane, far), axis=-1, keepdims=True)
    g_gate = 1.0 / jnp.sum(jnp.exp(gl - g_max), axis=-1, keepdims=True)
    g_idx = g_lane - float(N_EXPERTS)
    in_group = (lane_i < N_EXPERTS) & (group_of_lane == g_idx)
    el = jnp.where(in_group, logits, neg)
    v1 = jnp.max(el, axis=-1, keepdims=True)
    i1 = jnp.min(jnp.where(el == v1, lane, far), axis=-1, keepdims=True)
    el2 = jnp.where(lane == i1, neg, el)
    v2 = jnp.max(el2, axis=-1, keepdims=True)
    i2 = jnp.min(jnp.where(el2 == v2, lane, far), axis=-1, keepdims=True)
    e2 = jnp.exp(v2 - v1)
    w1 = g_gate / (1.0 + e2)
    w2 = g_gate * e2 / (1.0 + e2)
    return i1, i2, w1, w2


ROUTE_E1, ROUTE_E2, ROUTE_W1, ROUTE_W2, ROUTE_RANK1, ROUTE_RANK2 = range(6)


def _outproj_kernel(x_ref, ys_ref, u_ref, yb_ref, d_ref, wglu_ref, wout_ref, nf_ref, rw_ref, rb_ref, cnt0_ref,
                    x1_ref, hn_ref, route_ref, rt_ref, cnt_ref, wglu_bf, wout_bf, rw_bf, *, tail_tiles):
    @pl.when(pl.program_id(0) == 0)
    def _():
        cnt_ref[...] = cnt0_ref[...]
        rw = rw_ref[...]
        rw_hi = rw.astype(BF16)
        rw_bf[:, 0:LANES] = rw_hi
        rw_bf[:, LANES:] = (rw - rw_hi.astype(F32)).astype(BF16)

    _cast_weight_once(wglu_ref, wglu_bf)
    _cast_weight_once(wout_ref, wout_bf)
    def mix_in(precise):
        ys = jnp.concatenate([ys_ref[q] for q in range(N_SLABS)], axis=1)
        u = jnp.concatenate([u_ref[q] for q in range(N_SLABS)], axis=1)
        z = jax.nn.gelu(ys + d_ref[...] * u)
        ya = z * jax.nn.sigmoid(_pdot(z, *_parts(wglu_bf, precise=precise)))
        mix = _pdot(jnp.concatenate([ya, yb_ref[...]], axis=1), *_parts(wout_bf, precise=precise))
        x1_ref[...] = x_ref[...] + mix

    _per_tile_precision(tail_tiles, wout_bf.shape[0] == 2, mix_in)
    x1 = x1_ref[...]
    hn = _rms(x1, nf_ref[...])
    _store_row_tiles(hn_ref, hn)
    h_hi = hn.astype(BF16)
    h_lo = (hn - h_hi.astype(F32)).astype(BF16)
    hw = jnp.dot(h_hi, rw_bf[...], preferred_element_type=F32)
    logits = (hw[:, :LANES] + hw[:, LANES:]
              + jnp.dot(h_lo, rw_bf[:, 0:LANES], preferred_element_type=F32) + rb_ref[...])
    i1, i2, w1, w2 = _route(logits)
    tm = logits.shape[0]
    lane_i = lax.broadcasted_iota(jnp.int32, logits.shape, 1)
    lane = lane_i.astype(F32)
    picked = jnp.where((lane == i1) | (lane == i2), 1.0, 0.0)
    earlier = (lax.broadcasted_iota(jnp.int32, (tm, tm), 0) > lax.broadcasted_iota(jnp.int32, (tm, tm), 1))
    prefix = _bdot(jnp.where(earlier, 1.0, 0.0), picked.astype(BF16)) + cnt_ref[...]
    rank1 = jnp.sum(jnp.where(lane == i1, prefix, 0.0), axis=-1, keepdims=True)
    rank2 = jnp.sum(jnp.where(lane == i2, prefix, 0.0), axis=-1, keepdims=True)
    cnt_ref[...] += jnp.sum(picked, axis=0, keepdims=True)
    rec = jnp.zeros_like(logits)
    for lane_id, val in ((ROUTE_E1, i1), (ROUTE_E2, i2), (ROUTE_W1, w1), (ROUTE_W2, w2),
                         (ROUTE_RANK1, rank1), (ROUTE_RANK2, rank2)):
        rec = jnp.where(lane_i == lane_id, val, rec)
    route_ref[...] = rec
    rt_ref[...] = rec.T[0:SUBLANES, :]


def _outproj(x, ys, u, yb, d, wglu, wout, nf, rw, rb, counts0, layer, tm, t_len, tail):
    n = x.shape[0]
    precise = tail > 0
    return pl.pallas_call(
        functools.partial(_outproj_kernel, tail_tiles=_stream_tail_tiles(t_len, tm, tail)),
        grid=(n // tm,),
        in_specs=[_row_spec(tm, D_MODEL), _slab_spec(tm), _slab_spec(tm), _row_spec(tm, D_CONV),
                  _layer_spec(layer, 1, D_SSM), _layer_spec(layer, D_SSM, D_SSM),
                  _layer_spec(layer, D_MODEL, D_MODEL), _layer_spec(layer, 1, D_MODEL),
                  _const_spec((D_MODEL, LANES)), _const_spec((1, LANES)), _const_spec((1, LANES))],
        out_specs=[_row_spec(tm, D_MODEL), _row_tile_spec(tm), _row_spec(tm, LANES),
                   pl.BlockSpec((SUBLANES, tm), lambda i: (0, i)), _const_spec((1, LANES))],
        out_shape=[jax.ShapeDtypeStruct((n, D_MODEL), F32), jax.ShapeDtypeStruct(_tiled_rows(n), F32),
                   jax.ShapeDtypeStruct((n, LANES), F32), jax.ShapeDtypeStruct((SUBLANES, n), F32),
                   jax.ShapeDtypeStruct((1, LANES), F32)],
        scratch_shapes=[_weight_scratch(D_SSM, D_SSM, precise), _weight_scratch(D_MODEL, D_MODEL, precise),
                        pltpu.VMEM((D_MODEL, 2 * LANES), BF16)],
        compiler_params=_params("arbitrary"),
        name="outproj_router",
    )(x, ys, u, yb, _vec(d), wglu, wout, _vec(nf), rw, rb, counts0)


PLAN_TILE_LANES = 2 * LANES
PLAN_EXPERT, PLAN_ROWS, PLAN_USED = range(3)


def _moe_plan_kernel(rt_ref, cnt_ref, cnt_first_ref, slot_ref, tile_ref, *, tms):
    cnt = cnt_ref[...]
    padded = jnp.ceil(cnt * (1.0 / tms)) * float(tms)
    r = lax.broadcasted_iota(jnp.int32, (LANES, LANES), 0)
    c = lax.broadcasted_iota(jnp.int32, (LANES, LANES), 1)
    ends = jnp.dot(padded, jnp.where(r <= c, 1.0, 0.0), precision=lax.Precision.HIGHEST,
                   preferred_element_type=F32)
    starts = ends - padded
    rt = rt_ref[...]
    e1, e2 = rt[ROUTE_E1:ROUTE_E1 + 1], rt[ROUTE_E2:ROUTE_E2 + 1]
    s1, s2 = rt[ROUTE_RANK1:ROUTE_RANK1 + 1], rt[ROUTE_RANK2:ROUTE_RANK2 + 1]
    tile = lax.broadcasted_iota(jnp.int32, (1, PLAN_TILE_LANES), 1).astype(F32)
    used = ends[:, N_EXPERTS - 1:N_EXPERTS] * (1.0 / tms)
    pos = jnp.minimum(tile, used - 1.0) * float(tms)
    t_exp = jnp.zeros_like(tile)
    t_fill = jnp.zeros_like(tile)
    for e in range(N_EXPERTS):
        st, en = starts[:, e:e + 1], ends[:, e:e + 1]
        s1 = s1 + jnp.where(e1 == float(e), st, 0.0)
        s2 = s2 + jnp.where(e2 == float(e), st, 0.0)
        mine = (pos >= st) & (pos < en)
        t_exp = t_exp + jnp.where(mine, float(e), 0.0)
        t_fill = t_fill + jnp.where(mine, st + cnt_first_ref[:, e:e + 1], 0.0)
    t_rows = jnp.where(tile < used, jnp.clip(t_fill - pos, 0.0, float(tms)), 0.0)
    slot_ref[...] = jnp.concatenate([s1, s2], axis=0).astype(jnp.int32)
    tile_ref[...] = jnp.concatenate(
        [t_exp, t_rows, jnp.broadcast_to(used, tile.shape), jnp.zeros((SUBLANES - 3, PLAN_TILE_LANES), F32)],
        axis=0).astype(jnp.int32)


def _moe_plan(route_t, counts, counts_first, tms, n_tiles):
    n = route_t.shape[1]
    assert n_tiles <= PLAN_TILE_LANES
    slots, tiles = pl.pallas_call(
        functools.partial(_moe_plan_kernel, tms=tms),
        out_shape=[jax.ShapeDtypeStruct((2, n), jnp.int32),
                   jax.ShapeDtypeStruct((SUBLANES, PLAN_TILE_LANES), jnp.int32)],
        compiler_params=pltpu.CompilerParams(vmem_limit_bytes=VMEM_LIMIT),
        name="moe_plan",
    )(route_t, counts, counts_first)
    return slots, tiles[PLAN_EXPERT, :n_tiles], tiles[PLAN_ROWS, :n_tiles], tiles[PLAN_USED, :1]


DMA_UNROLL = 8


HN_BUFFERS = 3
ZERO_FILL_ROWS = 128


def _dispatch_kernel(tr_ref, slot_ref, hn_ref, *rest, tm, tms, n_tiles, n_steps, fresh):
    xs_ref, zbuf, hbuf, in_sem, out_sem, zsem = rest if fresh else rest[1:]
    i = pl.program_id(0)

    def fetch(t):
        b = lax.rem(t, HN_BUFFERS)
        first = pl.multiple_of(t * (tm * ROW_SUB), tm * ROW_SUB)
        return pltpu.make_async_copy(hn_ref.at[pl.ds(first, tm * ROW_SUB)], hbuf.at[b], in_sem.at[b])

    def drain_scatter(t):
        b = lax.rem(t, HN_BUFFERS)
        for k in range(2):
            pltpu.make_async_copy(hbuf.at[b], xs_ref.at[pl.ds(0, tm * ROW_SUB)], out_sem.at[b]).wait()

    @pl.when(i == 0)
    def _():
        fetch(i).start()

    @pl.when(jnp.logical_and(i == 0, fresh))
    def _():
        zbuf[...] = jnp.zeros_like(zbuf)

        zrows = zbuf.shape[0] // ROW_SUB
        per_tile = tms // zrows

        def unfilled(p):
            return tr_ref[p // per_tile] < (lax.rem(p, per_tile) + 1) * zrows

        def fill(p, carry):
            @pl.when(unfilled(p))
            def _():
                first = pl.multiple_of(p * (zrows * ROW_SUB), zrows * ROW_SUB)
                pltpu.make_async_copy(zbuf, xs_ref.at[pl.ds(first, zrows * ROW_SUB)], zsem).start()
            return carry

        def drain(p, carry):
            @pl.when(unfilled(p))
            def _():
                pltpu.make_async_copy(zbuf, xs_ref.at[pl.ds(0, zrows * ROW_SUB)], zsem).wait()
            return carry

        lax.fori_loop(0, n_tiles * per_tile, fill, 0)
        lax.fori_loop(0, n_tiles * per_tile, drain, 0)

    @pl.when(i + 1 < n_steps)
    def _():
        fetch(i + 1).start()

    fetch(i).wait()
    b = lax.rem(i, HN_BUFFERS)
    rows = hbuf.at[b]

    def issue(r, carry):
        for k in range(2):
            pltpu.make_async_copy(_one_row(rows, r), _one_row(xs_ref, slot_ref[k, r]),
                                  out_sem.at[b]).start(priority=k)
        return carry

    lax.fori_loop(0, tm, issue, 0, unroll=DMA_UNROLL)

    @pl.when(i >= 1)
    def _():
        drain_scatter(i - 1)

    @pl.when(i == n_steps - 1)
    def _():
        drain_scatter(i)


def _dispatch(hn, slots, tile_rows, tms, tm, into=None):
    n = hn.shape[0] // ROW_SUB
    n_tiles = tile_rows.shape[0]
    fresh = into is None
    any_spec = pl.BlockSpec(memory_space=pl.ANY)
    return pl.pallas_call(
        functools.partial(_dispatch_kernel, tm=tm, tms=tms, n_tiles=n_tiles, n_steps=n // tm, fresh=fresh),
        grid_spec=pltpu.PrefetchScalarGridSpec(
            num_scalar_prefetch=1, grid=(n // tm,),
            in_specs=[pl.BlockSpec((2, tm), lambda i, tr: (0, i), memory_space=pltpu.SMEM), any_spec]
                     + ([] if fresh else [any_spec]),
            out_specs=any_spec,
            scratch_shapes=[pltpu.VMEM(_tiled_rows(min(tms, ZERO_FILL_ROWS)), F32),
                            pltpu.VMEM((HN_BUFFERS,) + _tiled_rows(tm), F32),
                            pltpu.SemaphoreType.DMA((HN_BUFFERS,)), pltpu.SemaphoreType.DMA((HN_BUFFERS,)),
                            pltpu.SemaphoreType.DMA]),
        out_shape=jax.ShapeDtypeStruct(_tiled_rows(n_tiles * tms), F32),
        input_output_aliases={} if fresh else {3: 0},
        compiler_params=_params("arbitrary"),
        name="moe_dispatch",
    )(tile_rows, slots, hn, *([] if fresh else [into]))


def _moe_kernel(te_ref, nu_ref, x_ref, wg_ref, wu_ref, wd_ref, y_ref, wg_bf, wu_bf, wd_bf):
    i = pl.program_id(0)
    in_use = i < nu_ref[0]
    new_expert = (i == 0) | (te_ref[i] != te_ref[jnp.maximum(i - 1, 0)])

    @pl.when(in_use & new_expert)
    def _():
        wg_bf[...] = wg_ref[...].astype(BF16)
        wu_bf[...] = wu_ref[...].astype(BF16)
        wd_bf[...] = wd_ref[...].astype(BF16)

    @pl.when(in_use)
    def _():
        h = _load_row_tiles(x_ref).astype(BF16)
        hg = jnp.dot(h, wg_bf[...], preferred_element_type=F32)
        hu = jnp.dot(h, wu_bf[...], preferred_element_type=F32)
        _store_row_tiles(y_ref, _bdot(hg * jax.nn.sigmoid(hg) * hu, wd_bf[...]))

    @pl.when(jnp.logical_not(in_use))
    def _():
        y_ref[...] = jnp.zeros_like(y_ref)


def _moe(xs, tile_expert, n_used, wg, wu, wd, layer, tms):
    n_slots = xs.shape[0] // ROW_SUB
    rows = pl.BlockSpec(_tiled_rows(tms), lambda i, te, nu: (jnp.minimum(i, nu[0] - 1), 0))
    out_rows = pl.BlockSpec(_tiled_rows(tms), lambda i, te, nu: (i, 0))

    def wspec(a, b):
        return pl.BlockSpec((None, None, a, b), lambda i, te, nu: (layer, te[i], 0, 0))

    return pl.pallas_call(
        _moe_kernel,
        grid_spec=pltpu.PrefetchScalarGridSpec(
            num_scalar_prefetch=2, grid=(n_slots // tms,),
            in_specs=[rows, wspec(D_MODEL, D_EXPERT), wspec(D_MODEL, D_EXPERT), wspec(D_EXPERT, D_MODEL)],
            out_specs=out_rows,
            scratch_shapes=[pltpu.VMEM((D_MODEL, D_EXPERT), BF16), pltpu.VMEM((D_MODEL, D_EXPERT), BF16),
                            pltpu.VMEM((D_EXPERT, D_MODEL), BF16)]),
        out_shape=jax.ShapeDtypeStruct(xs.shape, F32),
        compiler_params=_params("arbitrary"),
        name="moe",
    )(tile_expert, n_used, xs, wg, wu, wd)


def _ple_kernel(slot_ref, next_slot_ref, x_ref, route_ref, p_ref, np_ref, wple_ref, wgate_ref, nfin_ref,
                ys_ref, o_ref, ybuf, sem, wple_bf, wgate_bf, *, tm, n_steps, final):
    i = pl.program_id(0)
    _cast_weight_once(wple_ref, wple_bf)
    _cast_weight_once(wgate_ref, wgate_bf)

    def gather(slots, b):
        def issue(r, carry):
            for k in range(2):
                pltpu.make_async_copy(_one_row(ys_ref, slots[k, r]), _one_row(ybuf.at[b, k], r),
                                      sem.at[b]).start(priority=k)
            return carry

        lax.fori_loop(0, tm, issue, 0, unroll=DMA_UNROLL)

    @pl.when(i == 0)
    def _():
        gather(slot_ref, 0)

    @pl.when(i + 1 < n_steps)
    def _():
        gather(next_slot_ref, lax.rem(i + 1, 2))

    pe = _pdot(p_ref[...], *_parts(wple_bf))
    b = lax.rem(i, 2)
    for k in range(2):
        pltpu.make_async_copy(ys_ref.at[pl.ds(0, tm * ROW_SUB)], ybuf.at[b, k], sem.at[b]).wait()
    route = route_ref[...]
    x = (x_ref[...] + route[:, ROUTE_W1:ROUTE_W1 + 1] * _load_row_tiles(ybuf.at[b, 0])
         + route[:, ROUTE_W2:ROUTE_W2 + 1] * _load_row_tiles(ybuf.at[b, 1]))
    gate = jax.nn.sigmoid(_pdot(_rms(x, np_ref[...]), *_parts(wgate_bf)))
    out = x + pe * gate
    if final:
        out = _rms(out, nfin_ref[...])
    o_ref[...] = out


def _ple(x, route, slots, ys, p, npl, wple, wgate, nfin, layer, tm, final):
    n = x.shape[0]
    n_steps = n // tm
    return pl.pallas_call(
        functools.partial(_ple_kernel, tm=tm, n_steps=n_steps, final=final),
        grid=(n_steps,),
        in_specs=[pl.BlockSpec((2, tm), lambda i: (0, i), memory_space=pltpu.SMEM),
                  pl.BlockSpec((2, tm), lambda i: (0, jnp.minimum(i + 1, n_steps - 1)),
                               memory_space=pltpu.SMEM),
                  _row_spec(tm, D_MODEL), _row_spec(tm, LANES),
                  pl.BlockSpec((None, tm, D_PLE), lambda i: (layer, i, 0)),
                  _layer_spec(layer, 1, D_MODEL), _layer_spec(layer, D_PLE, D_MODEL),
                  _layer_spec(layer, D_MODEL, D_MODEL), _const_spec((1, D_MODEL)),
                  pl.BlockSpec(memory_space=pl.ANY)],
        out_specs=_row_spec(tm, D_MODEL),
        out_shape=jax.ShapeDtypeStruct((n, D_MODEL), F32),
        scratch_shapes=[pltpu.VMEM((2, 2) + _tiled_rows(tm), F32), pltpu.SemaphoreType.DMA((2,)),
                        _weight_scratch(D_PLE, D_MODEL, False), _weight_scratch(D_MODEL, D_MODEL, False)],
        compiler_params=_params("arbitrary"),
        name="combine_ple",
    )(slots, slots, x, route, p, _vec(npl), wple, wgate, nfin.reshape(1, D_MODEL), ys)


def _mixers(x, h0_re, h0_im, conv_left, w, s5, router_w, router_b, counts0, layer, bsz, t_len, tm, tm_conv,
            tail):
    u, v = _inproj(x, w["norm_mix"], w["w_in"], layer, min(2 * tm, x.shape[0]), t_len, tail)
    ys, hf_re, hf_im = _s5_mixer(u, h0_re, h0_im, s5, bsz, t_len, tail)
    yb, conv_new = _conv_mixer(v, conv_left, w["conv_w"], w["conv_b"], w["conv_ln_g"], w["conv_ln_b"],
                               layer, bsz, t_len, tm_conv)
    x1, hn, route, route_t, counts = _outproj(x, ys, u, yb, w["ssm_d"], w["w_ssm_glu"], w["w_out"],
                                              w["norm_ffn"], router_w, router_b, counts0, layer, tm, t_len, tail)
    return dict(x1=x1, hn=hn, route=route, route_t=route_t, counts=counts, state=(hf_re, hf_im, conv_new))


def _moe_and_ple(sets, ps, tms_rows, w, layer, tms, final):
    sizes = [s["x1"].shape[0] for s in sets]
    n_tiles = 2 * sum(sizes) // tms + N_EXPERTS
    route_t = jnp.concatenate([s["route_t"] for s in sets], axis=1)
    slots, tile_expert, tile_rows, n_used = _moe_plan(route_t, sets[-1]["counts"], sets[0]["counts"], tms,
                                                     n_tiles)
    starts = [sum(sizes[:j]) for j in range(len(sets))]
    set_slots = [slots[:, a:a + n] for a, n in zip(starts, sizes)]
    xsort = None
    for s, sl, tm in zip(sets, set_slots, tms_rows):
        xsort = _dispatch(s["hn"], sl, tile_rows, tms, tm, into=xsort)
    ysort = _moe(xsort, tile_expert, n_used, w["expert_w_gate"], w["expert_w_up"], w["expert_w_down"],
                 layer, tms)
    return [_ple(s["x1"], s["route"], sl, ysort, p, w["norm_ple"], w["ple_w"], w["ple_gate_w"],
                 w["norm_final"], layer, tm, final)
            for s, sl, p, tm in zip(sets, set_slots, ps, tms_rows)]


def kernel(x_prompt, x_sample, p_prompt, p_sample, state_ssm_re, state_ssm_im, cache_conv, norm_mix, w_in, ssm_a_re, ssm_a_im, ssm_b_re, ssm_b_im, ssm_c_re, ssm_c_im, ssm_d, ssm_log_dt, w_ssm_glu, conv_w, conv_b, conv_ln_g, conv_ln_b, w_out, norm_ffn, router_group_w, router_group_b, router_expert_w, router_expert_b, expert_w_gate, expert_w_up, expert_w_down, norm_ple, ple_w, ple_gate_w, norm_final):
    depth = w_in.shape[0]
    bp, tp, _ = x_prompt.shape
    bs, ts, _ = x_sample.shape
    xp = x_prompt.reshape(bp * tp, D_MODEL)
    xs = x_sample.reshape(bs * ts, D_MODEL)
    pp = p_prompt.reshape(depth, bp * tp, D_PLE)
    ps = p_sample.reshape(depth, bs * ts, D_PLE)
    zero_state = jnp.zeros((bp, N_GROUPS, SSM_STATE), F32)
    zero_conv = jnp.zeros((bp, CONV_WIDTH - 1, D_CONV), F32)
    pad_lanes = LANES - N_EXPERTS - N_EXPERT_GROUPS
    w = {"norm_mix": norm_mix, "w_in": w_in, "ssm_d": ssm_d, "w_ssm_glu": w_ssm_glu, "conv_w": conv_w,
         "conv_b": conv_b, "conv_ln_g": conv_ln_g, "conv_ln_b": conv_ln_b, "w_out": w_out,
         "norm_ffn": norm_ffn, "expert_w_gate": expert_w_gate, "expert_w_up": expert_w_up,
         "expert_w_down": expert_w_down, "norm_ple": norm_ple, "ple_w": ple_w, "ple_gate_w": ple_gate_w,
         "norm_final": norm_final}
    no_picks = jnp.zeros((1, LANES), F32)
    outs = {k: [] for k in ("pr_re", "pr_im", "pr_conv", "sm_re", "sm_im", "sm_conv")}
    for i in range(depth):
        s5 = _s5_prep(ssm_a_re[i], ssm_a_im[i], ssm_log_dt[i], ssm_b_re[i], ssm_b_im[i], ssm_c_re[i],
                      ssm_c_im[i])
        router_w = jnp.pad(jnp.concatenate([router_expert_w[i], router_group_w[i]], axis=1),
                           ((0, 0), (0, pad_lanes)))
        router_b = jnp.pad(jnp.concatenate([router_expert_b[i], router_group_b[i]]),
                           (0, pad_lanes)).reshape(1, LANES)
        final = i == depth - 1
        tail = 0 if final else PRECISE_TAIL
        mp = _mixers(xp, zero_state, zero_state, zero_conv, w, s5, router_w, router_b, no_picks, i,
                     bp, tp, tm=TOKEN_TILE, tm_conv=2 * TOKEN_TILE, tail=tail)
        ms = _mixers(xs, state_ssm_re[i], state_ssm_im[i], cache_conv[i], w, s5, router_w, router_b,
                     mp["counts"], i, bs, ts, tm=bs * ts, tm_conv=ts, tail=tail)
        xp, xs = _moe_and_ple([mp, ms], [pp, ps], [TOKEN_TILE, bs * ts], w, i, SLOT_TILE, final)
        for key, val in zip(("pr_re", "pr_im", "pr_conv"), mp["state"]):
            outs[key].append(val)
        for key, val in zip(("sm_re", "sm_im", "sm_conv"), ms["state"]):
            outs[key].append(val)
    return (xp.reshape(bp, tp, D_MODEL), xs.reshape(bs, ts, D_MODEL),
            jnp.stack(outs["pr_re"]), jnp.stack(outs["pr_im"]), jnp.stack(outs["pr_conv"]),
            jnp.stack(outs["sm_re"]), jnp.stack(outs["sm_im"]), jnp.stack(outs["sm_conv"]))
```

```python
import functools

import jax
import jax.numpy as jnp
from jax import lax
from jax.experimental import pallas as pl
from jax.experimental.pallas import tpu as pltpu

F32 = jnp.float32
BF16 = jnp.bfloat16

D_MODEL = 1024
D_SSM = 512
SSM_GROUP = 16
N_GROUPS = D_SSM // SSM_GROUP
N_PAIRS = N_GROUPS // 2
SSM_STATE = 64
D_CONV = 512
CONV_WIDTH = 31
CONV_HALO = 32
N_EXPERT_GROUPS = 4
EXPERTS_PER_GROUP = 8
N_EXPERTS = 32
D_EXPERT = 256
D_PLE = 256
EPS = 1e-6
S5_CHUNK = 16
S5_RELAYOUT_ROWS = 256
S5_SCAN_UNROLL = 4
LANES = 128
SUBLANES = 8
N_SLABS = D_SSM // LANES
GROUPS_PER_SLAB = LANES // SSM_GROUP
PAIRS_PER_SLAB = GROUPS_PER_SLAB // 2
VMEM_LIMIT = 56 * 1024 * 1024
TOKEN_TILE = 512
SLOT_TILE = 512
PRECISE_TAIL = 1024


def _log2(n):
    assert n & (n - 1) == 0, n
    return n.bit_length() - 1


def _params(*sem):
    return pltpu.CompilerParams(dimension_semantics=sem, vmem_limit_bytes=VMEM_LIMIT)


def _rms(x, g):
    return x * lax.rsqrt(jnp.mean(x * x, axis=-1, keepdims=True) + EPS) * g


def _bdot(a, b):
    return jnp.dot(a.astype(BF16), b, preferred_element_type=F32)


def _split_bf16(a):
    hi = a.astype(BF16)
    return hi, (a - hi.astype(F32)).astype(BF16)


def _pdot(a, w_hi, w_lo=None, dims=(((1,), (0,)), ((), ()))):
    def mm(x, w):
        return lax.dot_general(x, w, dims, preferred_element_type=F32)

    if w_lo is None:
        return mm(a.astype(BF16), w_hi)
    a_hi, a_lo = _split_bf16(a)
    return mm(a_hi, w_hi) + (mm(a_hi, w_lo) + mm(a_lo, w_hi))


def _pdot_lo_terms(a, w_hi, w_lo, dims=(((1,), (0,)), ((), ()))):
    a_hi, a_lo = _split_bf16(a)
    return (lax.dot_general(a_hi, w_lo, dims, preferred_element_type=F32)
            + lax.dot_general(a_lo, w_hi, dims, preferred_element_type=F32))


def _stream_tail_tiles(t_len, tm, tail):
    tiles = max(t_len // tm, 1)
    return tiles, (max(t_len - tail, 0) // tm if tm < t_len else 0)


def _per_tile_precision(tail_tiles, has_lo, body):
    tiles, first = tail_tiles
    if not has_lo or first == 0:
        body(has_lo)
        return
    in_tail = lax.rem(pl.program_id(0), tiles) >= first
    pl.when(in_tail)(functools.partial(body, True))
    pl.when(jnp.logical_not(in_tail))(functools.partial(body, False))


def _parts(wbf_ref, rows=slice(None), precise=True):
    return wbf_ref[0, rows, :], (wbf_ref[1, rows, :] if precise and wbf_ref.shape[0] == 2 else None)


def _row_spec(tm, width):
    return pl.BlockSpec((tm, width), lambda i: (i, 0))


ROW_SUB = D_MODEL // LANES


def _tiled_rows(n):
    return (n * ROW_SUB, LANES)


def _row_tile_spec(tm):
    return pl.BlockSpec(_tiled_rows(tm), lambda i, *_: (i, 0))


def _one_row(ref, r):
    return ref.at[pl.ds(pl.multiple_of(r * ROW_SUB, ROW_SUB), ROW_SUB)]


def _store_row_tiles(ref, rows):
    n = rows.shape[0]
    for s in range(ROW_SUB):
        ref[pl.ds(s, n, stride=ROW_SUB), :] = rows[:, s * LANES:(s + 1) * LANES]


def _load_row_tiles(ref):
    n = ref.shape[0] // ROW_SUB
    return jnp.concatenate([ref[pl.ds(s, n, stride=ROW_SUB), :] for s in range(ROW_SUB)], axis=1)


def _slab_spec(tm):
    return pl.BlockSpec((N_SLABS, tm, LANES), lambda i: (0, i, 0))


def _const_spec(shape):
    return pl.BlockSpec(shape, lambda i: (0,) * len(shape))


def _layer_spec(layer, *shape):
    return pl.BlockSpec((None,) + shape, lambda *_: (layer,) + (0,) * len(shape))


def _vec(stacked):
    return stacked.reshape(stacked.shape[0], 1, stacked.shape[1])


def _cast_weight_once(w_ref, wbf_ref):
    @pl.when(pl.program_id(0) == 0)
    def _():
        w = w_ref[...]
        hi = w.astype(BF16)
        wbf_ref[0] = hi
        if wbf_ref.shape[0] == 2:
            wbf_ref[1] = (w - hi.astype(F32)).astype(BF16)


def _weight_scratch(k, n, precise):
    return pltpu.VMEM((2 if precise else 1, k, n), BF16)


def _inproj_kernel(x_ref, g_ref, w_ref, u_ref, v_ref, wbf, *, tail_tiles):
    _cast_weight_once(w_ref, wbf)

    def body(precise):
        hn = _rms(x_ref[...], g_ref[...])
        proj = _pdot(hn, *_parts(wbf, precise=precise))
        for q in range(N_SLABS):
            u_ref[q] = proj[:, q * LANES:(q + 1) * LANES]
        v_ref[...] = proj[:, D_SSM:D_SSM + D_CONV] * jax.nn.sigmoid(proj[:, D_SSM + D_CONV:])

    _per_tile_precision(tail_tiles, wbf.shape[0] == 2, body)


def _inproj(x, g, w, layer, tm, t_len, tail):
    n = x.shape[0]
    d_in = w.shape[2]
    precise = tail > 0
    return pl.pallas_call(
        functools.partial(_inproj_kernel, tail_tiles=_stream_tail_tiles(t_len, tm, tail)),
        grid=(n // tm,),
        in_specs=[_row_spec(tm, D_MODEL), _layer_spec(layer, 1, D_MODEL), _layer_spec(layer, D_MODEL, d_in)],
        out_specs=[_slab_spec(tm), _row_spec(tm, D_CONV)],
        out_shape=[jax.ShapeDtypeStruct((N_SLABS, n, LANES), F32), jax.ShapeDtypeStruct((n, D_CONV), F32)],
        scratch_shapes=[_weight_scratch(D_MODEL, d_in, precise)],
        compiler_params=_params("arbitrary"),
        name="inproj",
    )(x, _vec(g), w)


def _s5_prep_kernel(ar_ref, ai_ref, ldt_ref, bre_ref, bim_ref, cre_ref, cim_ref,
                    m_ref, ws_ref, wot_ref, atab_ref, wt_re, wt_im, br_re, br_im):
    n_tap = S5_CHUNK * SSM_GROUP
    st = 4 * SSM_STATE
    nt = (((1,), (1,)), ((), ()))
    hi = lax.Precision.HIGHEST
    lane = lax.broadcasted_iota(jnp.int32, (SSM_GROUP, n_tap), 1)
    ws_rows, wot_rows, atab = [], [], jnp.zeros((2 * SUBLANES, st), F32)
    for gi in range(2):
        ar, ai = ar_ref[gi], ai_ref[gi]
        dt = jnp.exp(ldt_ref[gi])
        k = lax.broadcasted_iota(jnp.int32, (S5_CHUNK + SUBLANES, SSM_STATE), 0).astype(F32)
        mag = jnp.exp(k * (dt * ar))
        ang = k * (dt * ai)
        p_re, p_im = mag * jnp.cos(ang), mag * jnp.sin(ang)
        inv = 1.0 / (ar * ar + ai * ai)
        ab_re, ab_im = p_re[1:2], p_im[1:2]
        ia_re, ia_im = ar * inv, -ai * inv
        coef_re = (ab_re - 1.0) * ia_re - ab_im * ia_im
        coef_im = (ab_re - 1.0) * ia_im + ab_im * ia_re
        bre, bim = bre_ref[gi], bim_ref[gi]
        bb_re = coef_re * bre - coef_im * bim
        bb_im = coef_re * bim + coef_im * bre
        cre, cim = cre_ref[gi], cim_ref[gi]
        for kk in range(S5_CHUNK + 1):
            pr, pi = p_re[kk:kk + 1], p_im[kk:kk + 1]
            rows = slice(kk * SSM_GROUP, (kk + 1) * SSM_GROUP)
            wt_re[rows, :] = pr * cre - pi * cim
            wt_im[rows, :] = -pi * cre - pr * cim
            if kk < S5_CHUNK:
                back = slice((S5_CHUNK - 1 - kk) * SSM_GROUP, (S5_CHUNK - kk) * SSM_GROUP)
                br_re[back, :] = pr * bb_re - pi * bb_im
                br_im[back, :] = pi * bb_re + pr * bb_im
        kcat = (lax.dot_general(bb_re, wt_re[0:n_tap, :], nt, precision=hi, preferred_element_type=F32)
                + lax.dot_general(bb_im, wt_im[0:n_tap, :], nt, precision=hi, preferred_element_type=F32))
        for s in range(S5_CHUNK):
            shifted = kcat if s == 0 else pltpu.roll(kcat, s * SSM_GROUP, 1)
            rows = slice(s * SSM_GROUP, (s + 1) * SSM_GROUP)
            m_ref[0, gi, rows, :], m_ref[1, gi, rows, :] = _split_bf16(
                jnp.where(lane >= s * SSM_GROUP, shifted, 0.0))
        def place(v_re, v_im):
            zero = jnp.zeros_like(v_re)
            parts = [v_re, zero, v_im, zero] if gi == 0 else [zero, v_re, zero, v_im]
            return jnp.concatenate(parts, axis=1)

        ws_rows.append(place(br_re[...], br_im[...]))
        wot_rows.append(place(wt_re[SSM_GROUP:, :], wt_im[SSM_GROUP:, :]))
        kc = float(S5_CHUNK) * lax.broadcasted_iota(jnp.int32, (2 * SUBLANES, SSM_STATE), 0).astype(F32)
        magc = jnp.exp(kc * (dt * ar))
        angc = kc * (dt * ai)
        atab = atab + place(magc * jnp.cos(angc), magc * jnp.sin(angc))
    ws_ref[0], ws_ref[1] = _split_bf16(jnp.concatenate(ws_rows, axis=0))
    wot_ref[0], wot_ref[1] = _split_bf16(jnp.concatenate(wot_rows, axis=0))
    atab_ref[...] = atab


def _s5_prep(a_re, a_im, log_dt, b_re, b_im, c_re, c_im):
    p, n, c = N_PAIRS, SSM_STATE, SSM_GROUP
    n_tap = S5_CHUNK * c
    st = 4 * n

    def pspec(*shape):
        return pl.BlockSpec((None,) + shape, lambda i: (i,) + (0,) * len(shape))

    def pairs(a, *shape):
        return a.reshape((p, 2) + shape)

    return pl.pallas_call(
        _s5_prep_kernel,
        grid=(p,),
        in_specs=[pspec(2, 1, n), pspec(2, 1, n), pspec(2, 1, 1), pspec(2, c, n), pspec(2, c, n),
                  pspec(2, c, n), pspec(2, c, n)],
        out_specs=[pspec(2, 2, n_tap, n_tap), pspec(2, 2 * n_tap, st), pspec(2, 2 * n_tap, st),
                   pspec(2 * SUBLANES, st)],
        out_shape=[jax.ShapeDtypeStruct((p, 2, 2, n_tap, n_tap), BF16),
                   jax.ShapeDtypeStruct((p, 2, 2 * n_tap, st), BF16),
                   jax.ShapeDtypeStruct((p, 2, 2 * n_tap, st), BF16),
                   jax.ShapeDtypeStruct((p, 2 * SUBLANES, st), F32)],
        scratch_shapes=[pltpu.VMEM((n_tap + c, n), F32), pltpu.VMEM((n_tap + c, n), F32),
                        pltpu.VMEM((n_tap, n), F32), pltpu.VMEM((n_tap, n), F32)],
        compiler_params=_params("parallel"),
        name="s5_prep",
    )(pairs(a_re, 1, n), pairs(a_im, 1, n), pairs(log_dt, 1, 1),
      pairs(jnp.swapaxes(b_re, 1, 2), c, n), pairs(jnp.swapaxes(b_im, 1, 2), c, n),
      pairs(c_re, c, n), pairs(c_im, c, n))


def _block_transpose8(vs):
    lane = lax.broadcasted_iota(jnp.int32, vs[0].shape, 1)
    blk = lane >> _log2(SSM_GROUP)
    for d in (GROUPS_PER_SLAB >> s for s in range(1, _log2(GROUPS_PER_SLAB) + 1)):
        keep = (blk & d) == 0
        new = list(vs)
        for i in range(GROUPS_PER_SLAB):
            if i & d == 0:
                a, b = vs[i], vs[i + d]
                new[i] = jnp.where(keep, a, pltpu.roll(b, d * SSM_GROUP, 1))
                new[i + d] = jnp.where(keep, pltpu.roll(a, LANES - d * SSM_GROUP, 1), b)
        vs = new
    return vs


def _cmul(ar, ai, xr, xi):
    return ar * xr - ai * xi, ar * xi + ai * xr


def _s5_kernel(u_ref, h0_ref, m_ref, ws_ref, wo_ref, a_ref, y_ref, hf_ref, x_scr, yg_scr, s_scr, hp_scr,
               *, rows, independent, tail_rows):
    half = 2 * SSM_STATE
    tail = pl.ds(rows - tail_rows, tail_rows)
    rt = min(rows, S5_RELAYOUT_ROWS)
    half_chunk = S5_CHUNK // 2

    def gather_tile(t, carry):
        r0 = pl.multiple_of(t * rt, rt)
        for hf in range(2):
            vs = [u_ref[pl.ds(r0 * S5_CHUNK + hf * half_chunk + i, rt, stride=S5_CHUNK), :]
                  for i in range(half_chunk)]
            outs = _block_transpose8(vs)
            for g in range(GROUPS_PER_SLAB):
                x_scr[g, pl.ds(r0, rt), hf * LANES:(hf + 1) * LANES] = outs[g]
        return carry

    lax.fori_loop(0, rows // rt, gather_tile, 0)

    row = lax.broadcasted_iota(jnp.int32, (SUBLANES, half), 0)
    n_tap = S5_CHUNK * SSM_GROUP
    for pi in range(PAIRS_PER_SLAB):
        def part(w_ref, *idx):
            return w_ref[(pi, 0) + idx], w_ref[(pi, 1) + idx]

        x0 = x_scr[2 * pi]
        x1 = x_scr[2 * pi + 1]
        x01 = jnp.concatenate([x0, x1], axis=1)
        s_scr[...] = _pdot(x01, ws_ref[pi, 0])
        if tail_rows:
            s_scr[tail, :] += _pdot_lo_terms(x01[rows - tail_rows:], *part(ws_ref))
        ap = a_ref[pi]
        h0 = h0_ref[pi]
        if independent:
            hp_scr[...] = h0
            s = s_scr[...]
            n_re, n_im = _cmul(ap[1:2, :half], ap[1:2, half:], h0[:, :half], h0[:, half:])
            hf_ref[pi] = jnp.concatenate([n_re + s[:, :half], n_im + s[:, half:]], axis=1)
        else:
            pw_re, pw_im = ap[0:SUBLANES, :half], ap[0:SUBLANES, half:]

            def scan_tile(t, carry):
                h_re, h_im = carry
                r0 = pl.multiple_of(t * SUBLANES, SUBLANES)
                s = s_scr[pl.ds(r0, SUBLANES), :]
                t_re, t_im = s[:, :half], s[:, half:]
                for d in (1, 2, 4):
                    sh_re = jnp.where(row >= d, pltpu.roll(t_re, d, 0), 0.0)
                    sh_im = jnp.where(row >= d, pltpu.roll(t_im, d, 0), 0.0)
                    m_re, m_im = _cmul(ap[d:d + 1, :half], ap[d:d + 1, half:], sh_re, sh_im)
                    t_re, t_im = t_re + m_re, t_im + m_im
                e_re = jnp.where(row >= 1, pltpu.roll(t_re, 1, 0), 0.0)
                e_im = jnp.where(row >= 1, pltpu.roll(t_im, 1, 0), 0.0)
                c_re, c_im = _cmul(pw_re, pw_im, h_re, h_im)
                hp_scr[pl.ds(r0, SUBLANES), :] = jnp.concatenate([e_re + c_re, e_im + c_im], axis=1)
                o_re, o_im = _cmul(ap[SUBLANES:SUBLANES + 1, :half], ap[SUBLANES:SUBLANES + 1, half:],
                                   h_re, h_im)
                last = SUBLANES - 1
                n_re = jnp.broadcast_to(t_re[last:last + 1], h_re.shape) + o_re
                n_im = jnp.broadcast_to(t_im[last:last + 1], h_im.shape) + o_im
                return n_re, n_im

            init = (jnp.broadcast_to(h0[:, :half], (SUBLANES, half)),
                    jnp.broadcast_to(h0[:, half:], (SUBLANES, half)))
            h_re, h_im = lax.fori_loop(0, rows // SUBLANES, scan_tile, init, unroll=S5_SCAN_UNROLL)
            hf_ref[pi] = jnp.concatenate([h_re[0:1], h_im[0:1]], axis=1)
        nt = (((1,), (1,)), ((), ()))
        yc = _pdot(hp_scr[...], wo_ref[pi, 0], dims=nt)
        yg_scr[2 * pi] = _pdot(x0, m_ref[pi, 0, 0]) + yc[:, :n_tap]
        yg_scr[2 * pi + 1] = _pdot(x1, m_ref[pi, 0, 1]) + yc[:, n_tap:]
        if tail_rows:
            yc_lo = _pdot_lo_terms(hp_scr[tail, :], *part(wo_ref), dims=nt)
            yg_scr[2 * pi, tail, :] += _pdot_lo_terms(x0[rows - tail_rows:], *part(m_ref, 0)) + yc_lo[:, :n_tap]
            yg_scr[2 * pi + 1, tail, :] += (_pdot_lo_terms(x1[rows - tail_rows:], *part(m_ref, 1))
                                            + yc_lo[:, n_tap:])

    def scatter_tile(t, carry):
        r0 = pl.multiple_of(t * rt, rt)
        for hf in range(2):
            vs = [yg_scr[g, pl.ds(r0, rt), hf * LANES:(hf + 1) * LANES] for g in range(GROUPS_PER_SLAB)]
            outs = _block_transpose8(vs)
            for i in range(half_chunk):
                y_ref[pl.ds(r0 * S5_CHUNK + hf * half_chunk + i, rt, stride=S5_CHUNK), :] = outs[i]
        return carry

    lax.fori_loop(0, rows // rt, scatter_tile, 0)


def _s5_mixer(u, h0_re, h0_im, prep, bsz, t_len, tail):
    m, wsp, wop, a16 = prep
    parts = 2 if tail > 0 else 1
    n_tap = S5_CHUNK * SSM_GROUP
    st = 4 * SSM_STATE
    independent = t_len == S5_CHUNK
    if independent:
        nblk, rows, hrows = 1, bsz, bsz
    else:
        nblk, rows, hrows = bsz, t_len // S5_CHUNK, 1
    assert t_len % S5_CHUNK == 0 and rows % SUBLANES == 0, (bsz, t_len)
    h0p = jnp.concatenate([h0_re.reshape(bsz, N_PAIRS, 2 * SSM_STATE),
                           h0_im.reshape(bsz, N_PAIRS, 2 * SSM_STATE)], axis=2).astype(F32)
    h0p = h0p.transpose(1, 0, 2)[None] if independent else h0p[:, :, None, :]
    pp = PAIRS_PER_SLAB

    def wspec(*shape):
        return pl.BlockSpec((pp,) + shape, lambda q, b: (q,) + (0,) * len(shape))

    frames = rows * S5_CHUNK
    if tail <= 0:
        tail_rows = 0
    elif independent:
        tail_rows = rows
    else:
        tail_rows = min(rows, -(-tail // (S5_CHUNK * SUBLANES)) * SUBLANES)
    y, hf = pl.pallas_call(
        functools.partial(_s5_kernel, rows=rows, independent=independent, tail_rows=tail_rows),
        grid=(N_SLABS, nblk),
        in_specs=[pl.BlockSpec((None, frames, LANES), lambda q, b: (q, b, 0)),
                  pl.BlockSpec((None, pp, hrows, st), lambda q, b: (b, q, 0, 0)),
                  wspec(parts, 2, n_tap, n_tap), wspec(parts, 2 * n_tap, st), wspec(parts, 2 * n_tap, st),
                  wspec(2 * SUBLANES, st)],
        out_specs=[pl.BlockSpec((None, frames, LANES), lambda q, b: (q, b, 0)),
                   pl.BlockSpec((None, pp, hrows, st), lambda q, b: (b, q, 0, 0))],
        out_shape=[jax.ShapeDtypeStruct(u.shape, F32),
                   jax.ShapeDtypeStruct((nblk, N_PAIRS, hrows, st), F32)],
        scratch_shapes=[pltpu.VMEM((GROUPS_PER_SLAB, rows, n_tap), F32),
                        pltpu.VMEM((GROUPS_PER_SLAB, rows, n_tap), F32),
                        pltpu.VMEM((rows, st), F32), pltpu.VMEM((rows, st), F32)],
        compiler_params=_params("parallel", "parallel"),
        name="s5_core",
    )(u, h0p, m, wsp, wop, a16)
    hf = hf[0].transpose(1, 0, 2) if independent else hf[:, :, 0, :]
    hf_re = hf[:, :, :2 * SSM_STATE].reshape(bsz, N_GROUPS, SSM_STATE)
    hf_im = hf[:, :, 2 * SSM_STATE:].reshape(bsz, N_GROUPS, SSM_STATE)
    return y, hf_re, hf_im


def _conv_kernel(v_ref, left_ref, w_ref, b_ref, g_ref, beta_ref, y_ref, cn_ref, vbuf, shifted, *, tm):
    @pl.when(pl.program_id(1) == 0)
    def _():
        vbuf[0:CONV_HALO, :] = left_ref[...]

    vbuf[CONV_HALO:CONV_HALO + tm, :] = v_ref[...]
    first = CONV_HALO - (CONV_WIDTH - 1)
    span = tm + CONV_HALO - SUBLANES
    for r in range(1, SUBLANES):
        shifted[r - 1, 0:span, :] = vbuf[r:r + span, :]
    acc = jnp.zeros((tm, D_CONV), F32)
    for k in range(CONV_WIDTH):
        a, r = divmod(first + k, SUBLANES)
        src = vbuf if r == 0 else shifted.at[r - 1]
        acc = acc + w_ref[k:k + 1, :] * src[a * SUBLANES:a * SUBLANES + tm, :]
    y = acc + b_ref[...]
    mu = jnp.mean(y, axis=-1, keepdims=True)
    yc = y - mu
    var = jnp.mean(yc * yc, axis=-1, keepdims=True)
    yn = yc * lax.rsqrt(var + EPS) * g_ref[...] + beta_ref[...]
    y_ref[...] = yn * jax.nn.sigmoid(yn)
    cn_ref[...] = vbuf[tm + first:tm + CONV_HALO, :]
    vbuf[0:CONV_HALO, :] = vbuf[tm:tm + CONV_HALO, :]


def _conv_mixer(v, left, w, b, g, beta, layer, bsz, t_len, tm):
    left = jnp.pad(left.astype(F32), ((0, 0), (CONV_HALO - (CONV_WIDTH - 1), 0), (0, 0)))
    nt = t_len // tm
    vec = _layer_spec(layer, 1, D_CONV)
    return pl.pallas_call(
        functools.partial(_conv_kernel, tm=tm),
        grid=(bsz, nt),
        in_specs=[pl.BlockSpec((tm, D_CONV), lambda bi, j: (bi * nt + j, 0)),
                  pl.BlockSpec((None, CONV_HALO, D_CONV), lambda bi, j: (bi, 0, 0)),
                  _layer_spec(layer, CONV_WIDTH, D_CONV), vec, vec, vec],
        out_specs=[pl.BlockSpec((tm, D_CONV), lambda bi, j: (bi * nt + j, 0)),
                   pl.BlockSpec((None, CONV_WIDTH - 1, D_CONV), lambda bi, j: (bi, 0, 0))],
        out_shape=[jax.ShapeDtypeStruct((bsz * t_len, D_CONV), F32),
                   jax.ShapeDtypeStruct((bsz, CONV_WIDTH - 1, D_CONV), F32)],
        scratch_shapes=[pltpu.VMEM((tm + CONV_HALO, D_CONV), F32),
                        pltpu.VMEM((SUBLANES - 1, tm + CONV_HALO - SUBLANES, D_CONV), F32)],
        compiler_params=_params("parallel", "arbitrary"),
        name="conv_mixer",
    )(v, left, w, _vec(b), _vec(g), _vec(beta))


def _route(logits):
    lane_i = lax.broadcasted_iota(jnp.int32, logits.shape, 1)
    lane = lane_i.astype(F32)
    group_of_lane = (lane_i >> _log2(EXPERTS_PER_GROUP)).astype(F32)
    neg = -jnp.inf
    far = float(LANES)
    is_g = (lane_i >= N_EXPERTS) & (lane_i < N_EXPERTS + N_EXPERT_GROUPS)
    gl = jnp.where(is_g, logits, neg)
    g_max = jnp.max(gl, axis=-1, keepdims=True)
    g_lane = jnp.min(jnp.where(gl == g_max, lane, far), axis=-1, keepdims=True)
    g_gate = 1.0 / jnp.sum(jnp.exp(gl - g_max), axis=-1, keepdims=True)
    g_idx = g_lane - float(N_EXPERTS)
    in_group = (lane_i < N_EXPERTS) & (group_of_lane == g_idx)
    el = jnp.where(in_group, logits, neg)
    v1 = jnp.max(el, axis=-1, keepdims=True)
    i1 = jnp.min(jnp.where(el == v1, lane, far), axis=-1, keepdims=True)
    el2 = jnp.where(lane == i1, neg, el)
    v2 = jnp.max(el2, axis=-1, keepdims=True)
    i2 = jnp.min(jnp.where(el2 == v2, lane, far), axis=-1, keepdims=True)
    e2 = jnp.exp(v2 - v1)
    w1 = g_gate / (1.0 + e2)
    w2 = g_gate * e2 / (1.0 + e2)
    return i1, i2, w1, w2


ROUTE_E1, ROUTE_E2, ROUTE_W1, ROUTE_W2, ROUTE_RANK1, ROUTE_RANK2 = range(6)


def _outproj_kernel(x_ref, ys_ref, u_ref, yb_ref, d_ref, wglu_ref, wout_ref, nf_ref, rw_ref, rb_ref, cnt0_ref,
                    x1_ref, hn_ref, route_ref, rt_ref, cnt_ref, wglu_bf, wout_bf, rw_bf, *, tail_tiles):
    @pl.when(pl.program_id(0) == 0)
    def _():
        cnt_ref[...] = cnt0_ref[...]
        rw = rw_ref[...]
        rw_hi = rw.astype(BF16)
        rw_bf[:, 0:LANES] = rw_hi
        rw_bf[:, LANES:] = (rw - rw_hi.astype(F32)).astype(BF16)

    _cast_weight_once(wglu_ref, wglu_bf)
    _cast_weight_once(wout_ref, wout_bf)
    def mix_in(precise):
        ys = jnp.concatenate([ys_ref[q] for q in range(N_SLABS)], axis=1)
        u = jnp.concatenate([u_ref[q] for q in range(N_SLABS)], axis=1)
        z = jax.nn.gelu(ys + d_ref[...] * u)
        ya = z * jax.nn.sigmoid(_pdot(z, *_parts(wglu_bf, precise=precise)))
        mix = _pdot(jnp.concatenate([ya, yb_ref[...]], axis=1), *_parts(wout_bf, precise=precise))
        x1_ref[...] = x_ref[...] + mix

    _per_tile_precision(tail_tiles, wout_bf.shape[0] == 2, mix_in)
    x1 = x1_ref[...]
    hn = _rms(x1, nf_ref[...])
    _store_row_tiles(hn_ref, hn)
    h_hi = hn.astype(BF16)
    h_lo = (hn - h_hi.astype(F32)).astype(BF16)
    hw = jnp.dot(h_hi, rw_bf[...], preferred_element_type=F32)
    logits = (hw[:, :LANES] + hw[:, LANES:]
              + jnp.dot(h_lo, rw_bf[:, 0:LANES], preferred_element_type=F32) + rb_ref[...])
    i1, i2, w1, w2 = _route(logits)
    tm = logits.shape[0]
    lane_i = lax.broadcasted_iota(jnp.int32, logits.shape, 1)
    lane = lane_i.astype(F32)
    picked = jnp.where((lane == i1) | (lane == i2), 1.0, 0.0)
    earlier = (lax.broadcasted_iota(jnp.int32, (tm, tm), 0) > lax.broadcasted_iota(jnp.int32, (tm, tm), 1))
    prefix = _bdot(jnp.where(earlier, 1.0, 0.0), picked.astype(BF16)) + cnt_ref[...]
    rank1 = jnp.sum(jnp.where(lane == i1, prefix, 0.0), axis=-1, keepdims=True)
    rank2 = jnp.sum(jnp.where(lane == i2, prefix, 0.0), axis=-1, keepdims=True)
    cnt_ref[...] += jnp.sum(picked, axis=0, keepdims=True)
    rec = jnp.zeros_like(logits)
    for lane_id, val in ((ROUTE_E1, i1), (ROUTE_E2, i2), (ROUTE_W1, w1), (ROUTE_W2, w2),
                         (ROUTE_RANK1, rank1), (ROUTE_RANK2, rank2)):
        rec = jnp.where(lane_i == lane_id, val, rec)
    route_ref[...] = rec
    rt_ref[...] = rec.T[0:SUBLANES, :]


def _outproj(x, ys, u, yb, d, wglu, wout, nf, rw, rb, counts0, layer, tm, t_len, tail):
    n = x.shape[0]
    precise = tail > 0
    return pl.pallas_call(
        functools.partial(_outproj_kernel, tail_tiles=_stream_tail_tiles(t_len, tm, tail)),
        grid=(n // tm,),
        in_specs=[_row_spec(tm, D_MODEL), _slab_spec(tm), _slab_spec(tm), _row_spec(tm, D_CONV),
                  _layer_spec(layer, 1, D_SSM), _layer_spec(layer, D_SSM, D_SSM),
                  _layer_spec(layer, D_MODEL, D_MODEL), _layer_spec(layer, 1, D_MODEL),
                  _const_spec((D_MODEL, LANES)), _const_spec((1, LANES)), _const_spec((1, LANES))],
        out_specs=[_row_spec(tm, D_MODEL), _row_tile_spec(tm), _row_spec(tm, LANES),
                   pl.BlockSpec((SUBLANES, tm), lambda i: (0, i)), _const_spec((1, LANES))],
        out_shape=[jax.ShapeDtypeStruct((n, D_MODEL), F32), jax.ShapeDtypeStruct(_tiled_rows(n), F32),
                   jax.ShapeDtypeStruct((n, LANES), F32), jax.ShapeDtypeStruct((SUBLANES, n), F32),
                   jax.ShapeDtypeStruct((1, LANES), F32)],
        scratch_shapes=[_weight_scratch(D_SSM, D_SSM, precise), _weight_scratch(D_MODEL, D_MODEL, precise),
                        pltpu.VMEM((D_MODEL, 2 * LANES), BF16)],
        compiler_params=_params("arbitrary"),
        name="outproj_router",
    )(x, ys, u, yb, _vec(d), wglu, wout, _vec(nf), rw, rb, counts0)


PLAN_TILE_LANES = 2 * LANES
PLAN_EXPERT, PLAN_ROWS, PLAN_USED = range(3)


def _moe_plan_kernel(rt_ref, cnt_ref, cnt_first_ref, slot_ref, tile_ref, *, tms):
    cnt = cnt_ref[...]
    padded = jnp.ceil(cnt * (1.0 / tms)) * float(tms)
    r = lax.broadcasted_iota(jnp.int32, (LANES, LANES), 0)
    c = lax.broadcasted_iota(jnp.int32, (LANES, LANES), 1)
    ends = jnp.dot(padded, jnp.where(r <= c, 1.0, 0.0), precision=lax.Precision.HIGHEST,
                   preferred_element_type=F32)
    starts = ends - padded
    rt = rt_ref[...]
    e1, e2 = rt[ROUTE_E1:ROUTE_E1 + 1], rt[ROUTE_E2:ROUTE_E2 + 1]
    s1, s2 = rt[ROUTE_RANK1:ROUTE_RANK1 + 1], rt[ROUTE_RANK2:ROUTE_RANK2 + 1]
    tile = lax.broadcasted_iota(jnp.int32, (1, PLAN_TILE_LANES), 1).astype(F32)
    used = ends[:, N_EXPERTS - 1:N_EXPERTS] * (1.0 / tms)
    pos = jnp.minimum(tile, used - 1.0) * float(tms)
    t_exp = jnp.zeros_like(tile)
    t_fill = jnp.zeros_like(tile)
    for e in range(N_EXPERTS):
        st, en = starts[:, e:e + 1], ends[:, e:e + 1]
        s1 = s1 + jnp.where(e1 == float(e), st, 0.0)
        s2 = s2 + jnp.where(e2 == float(e), st, 0.0)
        mine = (pos >= st) & (pos < en)
        t_exp = t_exp + jnp.where(mine, float(e), 0.0)
        t_fill = t_fill + jnp.where(mine, st + cnt_first_ref[:, e:e + 1], 0.0)
    t_rows = jnp.where(tile < used, jnp.clip(t_fill - pos, 0.0, float(tms)), 0.0)
    slot_ref[...] = jnp.concatenate([s1, s2], axis=0).astype(jnp.int32)
    tile_ref[...] = jnp.concatenate(
        [t_exp, t_rows, jnp.broadcast_to(used, tile.shape), jnp.zeros((SUBLANES - 3, PLAN_TILE_LANES), F32)],
        axis=0).astype(jnp.int32)


def _moe_plan(route_t, counts, counts_first, tms, n_tiles):
    n = route_t.shape[1]
    assert n_tiles <= PLAN_TILE_LANES
    slots, tiles = pl.pallas_call(
        functools.partial(_moe_plan_kernel, tms=tms),
        out_shape=[jax.ShapeDtypeStruct((2, n), jnp.int32),
                   jax.ShapeDtypeStruct((SUBLANES, PLAN_TILE_LANES), jnp.int32)],
        compiler_params=pltpu.CompilerParams(vmem_limit_bytes=VMEM_LIMIT),
        name="moe_plan",
    )(route_t, counts, counts_first)
    return slots, tiles[PLAN_EXPERT, :n_tiles], tiles[PLAN_ROWS, :n_tiles], tiles[PLAN_USED, :1]


DMA_UNROLL = 16


HN_BUFFERS = 3
ZERO_FILL_ROWS = 128


def _dispatch_kernel(tr_ref, slot_ref, hn_ref, *rest, tm, tms, n_tiles, n_steps, fresh):
    xs_ref, zbuf, hbuf, in_sem, out_sem, zsem = rest if fresh else rest[1:]
    i = pl.program_id(0)

    def fetch(t):
        b = lax.rem(t, HN_BUFFERS)
        first = pl.multiple_of(t * (tm * ROW_SUB), tm * ROW_SUB)
        return pltpu.make_async_copy(hn_ref.at[pl.ds(first, tm * ROW_SUB)], hbuf.at[b], in_sem.at[b])

    def drain_scatter(t):
        b = lax.rem(t, HN_BUFFERS)
        for k in range(2):
            pltpu.make_async_copy(hbuf.at[b], xs_ref.at[pl.ds(0, tm * ROW_SUB)], out_sem.at[b]).wait()

    @pl.when(i == 0)
    def _():
        fetch(i).start()

    @pl.when(jnp.logical_and(i == 0, fresh))
    def _():
        zbuf[...] = jnp.zeros_like(zbuf)

        zrows = zbuf.shape[0] // ROW_SUB
        per_tile = tms // zrows

        def unfilled(p):
            return tr_ref[p // per_tile] < (lax.rem(p, per_tile) + 1) * zrows

        def fill(p, carry):
            @pl.when(unfilled(p))
            def _():
                first = pl.multiple_of(p * (zrows * ROW_SUB), zrows * ROW_SUB)
                pltpu.make_async_copy(zbuf, xs_ref.at[pl.ds(first, zrows * ROW_SUB)], zsem).start()
            return carry

        def drain(p, carry):
            @pl.when(unfilled(p))
            def _():
                pltpu.make_async_copy(zbuf, xs_ref.at[pl.ds(0, zrows * ROW_SUB)], zsem).wait()
            return carry

        lax.fori_loop(0, n_tiles * per_tile, fill, 0)
        lax.fori_loop(0, n_tiles * per_tile, drain, 0)

    @pl.when(i + 1 < n_steps)
    def _():
        fetch(i + 1).start()

    fetch(i).wait()
    b = lax.rem(i, HN_BUFFERS)
    rows = hbuf.at[b]

    def issue(r, carry):
        for k in range(2):
            pltpu.make_async_copy(_one_row(rows, r), _one_row(xs_ref, slot_ref[k, r]),
                                  out_sem.at[b]).start(priority=k)
        return carry

    lax.fori_loop(0, tm, issue, 0, unroll=DMA_UNROLL)

    @pl.when(i >= 1)
    def _():
        drain_scatter(i - 1)

    @pl.when(i == n_steps - 1)
    def _():
        drain_scatter(i)


def _dispatch(hn, slots, tile_rows, tms, tm, into=None):
    n = hn.shape[0] // ROW_SUB
    n_tiles = tile_rows.shape[0]
    fresh = into is None
    any_spec = pl.BlockSpec(memory_space=pl.ANY)
    return pl.pallas_call(
        functools.partial(_dispatch_kernel, tm=tm, tms=tms, n_tiles=n_tiles, n_steps=n // tm, fresh=fresh),
        grid_spec=pltpu.PrefetchScalarGridSpec(
            num_scalar_prefetch=1, grid=(n // tm,),
            in_specs=[pl.BlockSpec((2, tm), lambda i, tr: (0, i), memory_space=pltpu.SMEM), any_spec]
                     + ([] if fresh else [any_spec]),
            out_specs=any_spec,
            scratch_shapes=[pltpu.VMEM(_tiled_rows(min(tms, ZERO_FILL_ROWS)), F32),
                            pltpu.VMEM((HN_BUFFERS,) + _tiled_rows(tm), F32),
                            pltpu.SemaphoreType.DMA((HN_BUFFERS,)), pltpu.SemaphoreType.DMA((HN_BUFFERS,)),
                            pltpu.SemaphoreType.DMA]),
        out_shape=jax.ShapeDtypeStruct(_tiled_rows(n_tiles * tms), F32),
        input_output_aliases={} if fresh else {3: 0},
        compiler_params=_params("arbitrary"),
        name="moe_dispatch",
    )(tile_rows, slots, hn, *([] if fresh else [into]))


def _moe_kernel(te_ref, nu_ref, x_ref, wg_ref, wu_ref, wd_ref, y_ref, wg_bf, wu_bf, wd_bf):
    i = pl.program_id(0)
    in_use = i < nu_ref[0]
    new_expert = (i == 0) | (te_ref[i] != te_ref[jnp.maximum(i - 1, 0)])

    @pl.when(in_use & new_expert)
    def _():
        wg_bf[...] = wg_ref[...].astype(BF16)
        wu_bf[...] = wu_ref[...].astype(BF16)
        wd_bf[...] = wd_ref[...].astype(BF16)

    @pl.when(in_use)
    def _():
        h = _load_row_tiles(x_ref).astype(BF16)
        hg = jnp.dot(h, wg_bf[...], preferred_element_type=F32)
        hu = jnp.dot(h, wu_bf[...], preferred_element_type=F32)
        _store_row_tiles(y_ref, _bdot(hg * jax.nn.sigmoid(hg) * hu, wd_bf[...]))

    @pl.when(jnp.logical_not(in_use))
    def _():
        y_ref[...] = jnp.zeros_like(y_ref)


def _moe(xs, tile_expert, n_used, wg, wu, wd, layer, tms):
    n_slots = xs.shape[0] // ROW_SUB
    rows = pl.BlockSpec(_tiled_rows(tms), lambda i, te, nu: (jnp.minimum(i, nu[0] - 1), 0))
    out_rows = pl.BlockSpec(_tiled_rows(tms), lambda i, te, nu: (i, 0))

    def wspec(a, b):
        return pl.BlockSpec((None, None, a, b), lambda i, te, nu: (layer, te[i], 0, 0))

    return pl.pallas_call(
        _moe_kernel,
        grid_spec=pltpu.PrefetchScalarGridSpec(
            num_scalar_prefetch=2, grid=(n_slots // tms,),
            in_specs=[rows, wspec(D_MODEL, D_EXPERT), wspec(D_MODEL, D_EXPERT), wspec(D_EXPERT, D_MODEL)],
            out_specs=out_rows,
            scratch_shapes=[pltpu.VMEM((D_MODEL, D_EXPERT), BF16), pltpu.VMEM((D_MODEL, D_EXPERT), BF16),
                            pltpu.VMEM((D_EXPERT, D_MODEL), BF16)]),
        out_shape=jax.ShapeDtypeStruct(xs.shape, F32),
        compiler_params=_params("arbitrary"),
        name="moe",
    )(tile_expert, n_used, xs, wg, wu, wd)


def _ple_kernel(slot_ref, next_slot_ref, x_ref, route_ref, p_ref, np_ref, wple_ref, wgate_ref, nfin_ref,
                ys_ref, o_ref, ybuf, sem, wple_bf, wgate_bf, *, tm, n_steps, final):
    i = pl.program_id(0)
    _cast_weight_once(wple_ref, wple_bf)
    _cast_weight_once(wgate_ref, wgate_bf)

    def gather(slots, b):
        def issue(r, carry):
            for k in range(2):
                pltpu.make_async_copy(_one_row(ys_ref, slots[k, r]), _one_row(ybuf.at[b, k], r),
                                      sem.at[b]).start(priority=k)
            return carry

        lax.fori_loop(0, tm, issue, 0, unroll=DMA_UNROLL)

    @pl.when(i == 0)
    def _():
        gather(slot_ref, 0)

    @pl.when(i + 1 < n_steps)
    def _():
        gather(next_slot_ref, lax.rem(i + 1, 2))

    pe = _pdot(p_ref[...], *_parts(wple_bf))
    b = lax.rem(i, 2)
    for k in range(2):
        pltpu.make_async_copy(ys_ref.at[pl.ds(0, tm * ROW_SUB)], ybuf.at[b, k], sem.at[b]).wait()
    route = route_ref[...]
    x = (x_ref[...] + route[:, ROUTE_W1:ROUTE_W1 + 1] * _load_row_tiles(ybuf.at[b, 0])
         + route[:, ROUTE_W2:ROUTE_W2 + 1] * _load_row_tiles(ybuf.at[b, 1]))
    gate = jax.nn.sigmoid(_pdot(_rms(x, np_ref[...]), *_parts(wgate_bf)))
    out = x + pe * gate
    if final:
        out = _rms(out, nfin_ref[...])
    o_ref[...] = out


def _ple(x, route, slots, ys, p, npl, wple, wgate, nfin, layer, tm, final):
    n = x.shape[0]
    n_steps = n // tm
    return pl.pallas_call(
        functools.partial(_ple_kernel, tm=tm, n_steps=n_steps, final=final),
        grid=(n_steps,),
        in_specs=[pl.BlockSpec((2, tm), lambda i: (0, i), memory_space=pltpu.SMEM),
                  pl.BlockSpec((2, tm), lambda i: (0, jnp.minimum(i + 1, n_steps - 1)),
                               memory_space=pltpu.SMEM),
                  _row_spec(tm, D_MODEL), _row_spec(tm, LANES),
                  pl.BlockSpec((None, tm, D_PLE), lambda i: (layer, i, 0)),
                  _layer_spec(layer, 1, D_MODEL), _layer_spec(layer, D_PLE, D_MODEL),
                  _layer_spec(layer, D_MODEL, D_MODEL), _const_spec((1, D_MODEL)),
                  pl.BlockSpec(memory_space=pl.ANY)],
        out_specs=_row_spec(tm, D_MODEL),
        out_shape=jax.ShapeDtypeStruct((n, D_MODEL), F32),
        scratch_shapes=[pltpu.VMEM((2, 2) + _tiled_rows(tm), F32), pltpu.SemaphoreType.DMA((2,)),
                        _weight_scratch(D_PLE, D_MODEL, False), _weight_scratch(D_MODEL, D_MODEL, False)],
        compiler_params=_params("arbitrary"),
        name="combine_ple",
    )(slots, slots, x, route, p, _vec(npl), wple, wgate, nfin.reshape(1, D_MODEL), ys)


def _mixers(x, h0_re, h0_im, conv_left, w, s5, router_w, router_b, counts0, layer, bsz, t_len, tm, tm_conv,
            tail):
    u, v = _inproj(x, w["norm_mix"], w["w_in"], layer, min(2 * tm, x.shape[0]), t_len, tail)
    ys, hf_re, hf_im = _s5_mixer(u, h0_re, h0_im, s5, bsz, t_len, tail)
    yb, conv_new = _conv_mixer(v, conv_left, w["conv_w"], w["conv_b"], w["conv_ln_g"], w["conv_ln_b"],
                               layer, bsz, t_len, tm_conv)
    x1, hn, route, route_t, counts = _outproj(x, ys, u, yb, w["ssm_d"], w["w_ssm_glu"], w["w_out"],
                                              w["norm_ffn"], router_w, router_b, counts0, layer, tm, t_len, tail)
    return dict(x1=x1, hn=hn, route=route, route_t=route_t, counts=counts, state=(hf_re, hf_im, conv_new))


def _moe_and_ple(sets, ps, tms_rows, w, layer, tms, final):
    sizes = [s["x1"].shape[0] for s in sets]
    n_tiles = 2 * sum(sizes) // tms + N_EXPERTS
    route_t = jnp.concatenate([s["route_t"] for s in sets], axis=1)
    slots, tile_expert, tile_rows, n_used = _moe_plan(route_t, sets[-1]["counts"], sets[0]["counts"], tms,
                                                     n_tiles)
    starts = [sum(sizes[:j]) for j in range(len(sets))]
    set_slots = [slots[:, a:a + n] for a, n in zip(starts, sizes)]
    xsort = None
    for s, sl, tm in zip(sets, set_slots, tms_rows):
        xsort = _dispatch(s["hn"], sl, tile_rows, tms, min(2 * tm, s["x1"].shape[0]), into=xsort)
    ysort = _moe(xsort, tile_expert, n_used, w["expert_w_gate"], w["expert_w_up"], w["expert_w_down"],
                 layer, tms)
    return [_ple(s["x1"], s["route"], sl, ysort, p, w["norm_ple"], w["ple_w"], w["ple_gate_w"],
                 w["norm_final"], layer, tm, final)
            for s, sl, p, tm in zip(sets, set_slots, ps, tms_rows)]


def kernel(x_prompt, x_sample, p_prompt, p_sample, state_ssm_re, state_ssm_im, cache_conv, norm_mix, w_in, ssm_a_re, ssm_a_im, ssm_b_re, ssm_b_im, ssm_c_re, ssm_c_im, ssm_d, ssm_log_dt, w_ssm_glu, conv_w, conv_b, conv_ln_g, conv_ln_b, w_out, norm_ffn, router_group_w, router_group_b, router_expert_w, router_expert_b, expert_w_gate, expert_w_up, expert_w_down, norm_ple, ple_w, ple_gate_w, norm_final):
    depth = w_in.shape[0]
    bp, tp, _ = x_prompt.shape
    bs, ts, _ = x_sample.shape
    xp = x_prompt.reshape(bp * tp, D_MODEL)
    xs = x_sample.reshape(bs * ts, D_MODEL)
    pp = p_prompt.reshape(depth, bp * tp, D_PLE)
    ps = p_sample.reshape(depth, bs * ts, D_PLE)
    zero_state = jnp.zeros((bp, N_GROUPS, SSM_STATE), F32)
    zero_conv = jnp.zeros((bp, CONV_WIDTH - 1, D_CONV), F32)
    pad_lanes = LANES - N_EXPERTS - N_EXPERT_GROUPS
    w = {"norm_mix": norm_mix, "w_in": w_in, "ssm_d": ssm_d, "w_ssm_glu": w_ssm_glu, "conv_w": conv_w,
         "conv_b": conv_b, "conv_ln_g": conv_ln_g, "conv_ln_b": conv_ln_b, "w_out": w_out,
         "norm_ffn": norm_ffn, "expert_w_gate": expert_w_gate, "expert_w_up": expert_w_up,
         "expert_w_down": expert_w_down, "norm_ple": norm_ple, "ple_w": ple_w, "ple_gate_w": ple_gate_w,
         "norm_final": norm_final}
    no_picks = jnp.zeros((1, LANES), F32)
    outs = {k: [] for k in ("pr_re", "pr_im", "pr_conv", "sm_re", "sm_im", "sm_conv")}
    for i in range(depth):
        s5 = _s5_prep(ssm_a_re[i], ssm_a_im[i], ssm_log_dt[i], ssm_b_re[i], ssm_b_im[i], ssm_c_re[i],
                      ssm_c_im[i])
        router_w = jnp.pad(jnp.concatenate([router_expert_w[i], router_group_w[i]], axis=1),
                           ((0, 0), (0, pad_lanes)))
        router_b = jnp.pad(jnp.concatenate([router_expert_b[i], router_group_b[i]]),
                           (0, pad_lanes)).reshape(1, LANES)
        final = i == depth - 1
        tail = 0 if final else PRECISE_TAIL
        mp = _mixers(xp, zero_state, zero_state, zero_conv, w, s5, router_w, router_b, no_picks, i,
                     bp, tp, tm=TOKEN_TILE, tm_conv=2 * TOKEN_TILE, tail=tail)
        ms = _mixers(xs, state_ssm_re[i], state_ssm_im[i], cache_conv[i], w, s5, router_w, router_b,
                     mp["counts"], i, bs, ts, tm=bs * ts, tm_conv=ts, tail=tail)
        xp, xs = _moe_and_ple([mp, ms], [pp, ps], [TOKEN_TILE, bs * ts], w, i, SLOT_TILE, final)
        for key, val in zip(("pr_re", "pr_im", "pr_conv"), mp["state"]):
            outs[key].append(val)
        for key, val in zip(("sm_re", "sm_im", "sm_conv"), ms["state"]):
            outs[key].append(val)
    return (xp.reshape(bp, tp, D_MODEL), xs.reshape(bs, ts, D_MODEL),
            jnp.stack(outs["pr_re"]), jnp.stack(outs["pr_im"]), jnp.stack(outs["pr_conv"]),
            jnp.stack(outs["sm_re"]), jnp.stack(outs["sm_im"]), jnp.stack(outs["sm_conv"]))
```

```python
import functools

import jax
import jax.numpy as jnp
from jax import lax
from jax.experimental import pallas as pl
from jax.experimental.pallas import tpu as pltpu

F32 = jnp.float32
BF16 = jnp.bfloat16

D_MODEL = 1024
D_SSM = 512
SSM_GROUP = 16
N_GROUPS = D_SSM // SSM_GROUP
N_PAIRS = N_GROUPS // 2
SSM_STATE = 64
D_CONV = 512
CONV_WIDTH = 31
CONV_HALO = 32
N_EXPERT_GROUPS = 4
EXPERTS_PER_GROUP = 8
N_EXPERTS = 32
D_EXPERT = 256
D_PLE = 256
EPS = 1e-6
S5_CHUNK = 16
S5_RELAYOUT_ROWS = 256
S5_SCAN_UNROLL = 4
LANES = 128
SUBLANES = 8
N_SLABS = D_SSM // LANES
GROUPS_PER_SLAB = LANES // SSM_GROUP
PAIRS_PER_SLAB = GROUPS_PER_SLAB // 2
VMEM_LIMIT = 56 * 1024 * 1024
TOKEN_TILE = 512
SLOT_TILE = 512
PRECISE_TAIL = 1024


def _log2(n):
    assert n & (n - 1) == 0, n
    return n.bit_length() - 1


def _params(*sem):
    return pltpu.CompilerParams(dimension_semantics=sem, vmem_limit_bytes=VMEM_LIMIT)


def _rms(x, g):
    return x * lax.rsqrt(jnp.mean(x * x, axis=-1, keepdims=True) + EPS) * g


def _bdot(a, b):
    return jnp.dot(a.astype(BF16), b, preferred_element_type=F32)


def _split_bf16(a):
    hi = a.astype(BF16)
    return hi, (a - hi.astype(F32)).astype(BF16)


def _pdot(a, w_hi, w_lo=None, dims=(((1,), (0,)), ((), ()))):
    def mm(x, w):
        return lax.dot_general(x, w, dims, preferred_element_type=F32)

    if w_lo is None:
        return mm(a.astype(BF16), w_hi)
    a_hi, a_lo = _split_bf16(a)
    return mm(a_hi, w_hi) + (mm(a_hi, w_lo) + mm(a_lo, w_hi))


def _pdot_lo_terms(a, w_hi, w_lo, dims=(((1,), (0,)), ((), ()))):
    a_hi, a_lo = _split_bf16(a)
    return (lax.dot_general(a_hi, w_lo, dims, preferred_element_type=F32)
            + lax.dot_general(a_lo, w_hi, dims, preferred_element_type=F32))


def _stream_tail_tiles(t_len, tm, tail):
    tiles = max(t_len // tm, 1)
    return tiles, (max(t_len - tail, 0) // tm if tm < t_len else 0)


def _per_tile_precision(tail_tiles, has_lo, body):
    tiles, first = tail_tiles
    if not has_lo or first == 0:
        body(has_lo)
        return
    in_tail = lax.rem(pl.program_id(0), tiles) >= first
    pl.when(in_tail)(functools.partial(body, True))
    pl.when(jnp.logical_not(in_tail))(functools.partial(body, False))


def _parts(wbf_ref, rows=slice(None), precise=True):
    return wbf_ref[0, rows, :], (wbf_ref[1, rows, :] if precise and wbf_ref.shape[0] == 2 else None)


def _row_spec(tm, width):
    return pl.BlockSpec((tm, width), lambda i: (i, 0))


ROW_SUB = D_MODEL // LANES


def _tiled_rows(n):
    return (n * ROW_SUB, LANES)


def _row_tile_spec(tm):
    return pl.BlockSpec(_tiled_rows(tm), lambda i, *_: (i, 0))


def _one_row(ref, r):
    return ref.at[pl.ds(pl.multiple_of(r * ROW_SUB, ROW_SUB), ROW_SUB)]


def _store_row_tiles(ref, rows):
    n = rows.shape[0]
    for s in range(ROW_SUB):
        ref[pl.ds(s, n, stride=ROW_SUB), :] = rows[:, s * LANES:(s + 1) * LANES]


def _load_row_tiles(ref):
    n = ref.shape[0] // ROW_SUB
    return jnp.concatenate([ref[pl.ds(s, n, stride=ROW_SUB), :] for s in range(ROW_SUB)], axis=1)


def _slab_spec(tm):
    return pl.BlockSpec((N_SLABS, tm, LANES), lambda i: (0, i, 0))


def _const_spec(shape):
    return pl.BlockSpec(shape, lambda i: (0,) * len(shape))


def _layer_spec(layer, *shape):
    return pl.BlockSpec((None,) + shape, lambda *_: (layer,) + (0,) * len(shape))


def _vec(stacked):
    return stacked.reshape(stacked.shape[0], 1, stacked.shape[1])


def _cast_weight_once(w_ref, wbf_ref):
    @pl.when(pl.program_id(0) == 0)
    def _():
        w = w_ref[...]
        hi = w.astype(BF16)
        wbf_ref[0] = hi
        if wbf_ref.shape[0] == 2:
            wbf_ref[1] = (w - hi.astype(F32)).astype(BF16)


def _weight_scratch(k, n, precise):
    return pltpu.VMEM((2 if precise else 1, k, n), BF16)


def _inproj_kernel(x_ref, g_ref, w_ref, u_ref, v_ref, wbf, *, tail_tiles):
    _cast_weight_once(w_ref, wbf)

    def body(precise):
        hn = _rms(x_ref[...], g_ref[...])
        proj = _pdot(hn, *_parts(wbf, precise=precise))
        for q in range(N_SLABS):
            u_ref[q] = proj[:, q * LANES:(q + 1) * LANES]
        v_ref[...] = proj[:, D_SSM:D_SSM + D_CONV] * jax.nn.sigmoid(proj[:, D_SSM + D_CONV:])

    _per_tile_precision(tail_tiles, wbf.shape[0] == 2, body)


def _inproj(x, g, w, layer, tm, t_len, tail):
    n = x.shape[0]
    d_in = w.shape[2]
    precise = tail > 0
    return pl.pallas_call(
        functools.partial(_inproj_kernel, tail_tiles=_stream_tail_tiles(t_len, tm, tail)),
        grid=(n // tm,),
        in_specs=[_row_spec(tm, D_MODEL), _layer_spec(layer, 1, D_MODEL), _layer_spec(layer, D_MODEL, d_in)],
        out_specs=[_slab_spec(tm), _row_spec(tm, D_CONV)],
        out_shape=[jax.ShapeDtypeStruct((N_SLABS, n, LANES), F32), jax.ShapeDtypeStruct((n, D_CONV), F32)],
        scratch_shapes=[_weight_scratch(D_MODEL, d_in, precise)],
        compiler_params=_params("arbitrary"),
        name="inproj",
    )(x, _vec(g), w)


def _s5_prep_kernel(ar_ref, ai_ref, ldt_ref, bre_ref, bim_ref, cre_ref, cim_ref,
                    m_ref, ws_ref, wot_ref, atab_ref, wt_re, wt_im, br_re, br_im):
    n_tap = S5_CHUNK * SSM_GROUP
    st = 4 * SSM_STATE
    nt = (((1,), (1,)), ((), ()))
    hi = lax.Precision.HIGHEST
    lane = lax.broadcasted_iota(jnp.int32, (SSM_GROUP, n_tap), 1)
    ws_rows, wot_rows, atab = [], [], jnp.zeros((2 * SUBLANES, st), F32)
    for gi in range(2):
        ar, ai = ar_ref[gi], ai_ref[gi]
        dt = jnp.exp(ldt_ref[gi])
        k = lax.broadcasted_iota(jnp.int32, (S5_CHUNK + SUBLANES, SSM_STATE), 0).astype(F32)
        mag = jnp.exp(k * (dt * ar))
        ang = k * (dt * ai)
        p_re, p_im = mag * jnp.cos(ang), mag * jnp.sin(ang)
        inv = 1.0 / (ar * ar + ai * ai)
        ab_re, ab_im = p_re[1:2], p_im[1:2]
        ia_re, ia_im = ar * inv, -ai * inv
        coef_re = (ab_re - 1.0) * ia_re - ab_im * ia_im
        coef_im = (ab_re - 1.0) * ia_im + ab_im * ia_re
        bre, bim = bre_ref[gi], bim_ref[gi]
        bb_re = coef_re * bre - coef_im * bim
        bb_im = coef_re * bim + coef_im * bre
        cre, cim = cre_ref[gi], cim_ref[gi]
        for kk in range(S5_CHUNK + 1):
            pr, pi = p_re[kk:kk + 1], p_im[kk:kk + 1]
            rows = slice(kk * SSM_GROUP, (kk + 1) * SSM_GROUP)
            wt_re[rows, :] = pr * cre - pi * cim
            wt_im[rows, :] = -pi * cre - pr * cim
            if kk < S5_CHUNK:
                back = slice((S5_CHUNK - 1 - kk) * SSM_GROUP, (S5_CHUNK - kk) * SSM_GROUP)
                br_re[back, :] = pr * bb_re - pi * bb_im
                br_im[back, :] = pi * bb_re + pr * bb_im
        kcat = (lax.dot_general(bb_re, wt_re[0:n_tap, :], nt, precision=hi, preferred_element_type=F32)
                + lax.dot_general(bb_im, wt_im[0:n_tap, :], nt, precision=hi, preferred_element_type=F32))
        for s in range(S5_CHUNK):
            shifted = kcat if s == 0 else pltpu.roll(kcat, s * SSM_GROUP, 1)
            rows = slice(s * SSM_GROUP, (s + 1) * SSM_GROUP)
            m_ref[0, gi, rows, :], m_ref[1, gi, rows, :] = _split_bf16(
                jnp.where(lane >= s * SSM_GROUP, shifted, 0.0))
        def place(v_re, v_im):
            zero = jnp.zeros_like(v_re)
            parts = [v_re, zero, v_im, zero] if gi == 0 else [zero, v_re, zero, v_im]
            return jnp.concatenate(parts, axis=1)

        ws_rows.append(place(br_re[...], br_im[...]))
        wot_rows.append(place(wt_re[SSM_GROUP:, :], wt_im[SSM_GROUP:, :]))
        kc = float(S5_CHUNK) * lax.broadcasted_iota(jnp.int32, (2 * SUBLANES, SSM_STATE), 0).astype(F32)
        magc = jnp.exp(kc * (dt * ar))
        angc = kc * (dt * ai)
        atab = atab + place(magc * jnp.cos(angc), magc * jnp.sin(angc))
    ws_ref[0], ws_ref[1] = _split_bf16(jnp.concatenate(ws_rows, axis=0))
    wot_ref[0], wot_ref[1] = _split_bf16(jnp.concatenate(wot_rows, axis=0))
    atab_ref[...] = atab


def _s5_prep(a_re, a_im, log_dt, b_re, b_im, c_re, c_im):
    p, n, c = N_PAIRS, SSM_STATE, SSM_GROUP
    n_tap = S5_CHUNK * c
    st = 4 * n

    def pspec(*shape):
        return pl.BlockSpec((None,) + shape, lambda i: (i,) + (0,) * len(shape))

    def pairs(a, *shape):
        return a.reshape((p, 2) + shape)

    return pl.pallas_call(
        _s5_prep_kernel,
        grid=(p,),
        in_specs=[pspec(2, 1, n), pspec(2, 1, n), pspec(2, 1, 1), pspec(2, c, n), pspec(2, c, n),
                  pspec(2, c, n), pspec(2, c, n)],
        out_specs=[pspec(2, 2, n_tap, n_tap), pspec(2, 2 * n_tap, st), pspec(2, 2 * n_tap, st),
                   pspec(2 * SUBLANES, st)],
        out_shape=[jax.ShapeDtypeStruct((p, 2, 2, n_tap, n_tap), BF16),
                   jax.ShapeDtypeStruct((p, 2, 2 * n_tap, st), BF16),
                   jax.ShapeDtypeStruct((p, 2, 2 * n_tap, st), BF16),
                   jax.ShapeDtypeStruct((p, 2 * SUBLANES, st), F32)],
        scratch_shapes=[pltpu.VMEM((n_tap + c, n), F32), pltpu.VMEM((n_tap + c, n), F32),
                        pltpu.VMEM((n_tap, n), F32), pltpu.VMEM((n_tap, n), F32)],
        compiler_params=_params("parallel"),
        name="s5_prep",
    )(pairs(a_re, 1, n), pairs(a_im, 1, n), pairs(log_dt, 1, 1),
      pairs(jnp.swapaxes(b_re, 1, 2), c, n), pairs(jnp.swapaxes(b_im, 1, 2), c, n),
      pairs(c_re, c, n), pairs(c_im, c, n))


def _block_transpose8(vs):
    lane = lax.broadcasted_iota(jnp.int32, vs[0].shape, 1)
    blk = lane >> _log2(SSM_GROUP)
    for d in (GROUPS_PER_SLAB >> s for s in range(1, _log2(GROUPS_PER_SLAB) + 1)):
        keep = (blk & d) == 0
        new = list(vs)
        for i in range(GROUPS_PER_SLAB):
            if i & d == 0:
                a, b = vs[i], vs[i + d]
                new[i] = jnp.where(keep, a, pltpu.roll(b, d * SSM_GROUP, 1))
                new[i + d] = jnp.where(keep, pltpu.roll(a, LANES - d * SSM_GROUP, 1), b)
        vs = new
    return vs


def _cmul(ar, ai, xr, xi):
    return ar * xr - ai * xi, ar * xi + ai * xr


def _s5_kernel(u_ref, h0_ref, m_ref, ws_ref, wo_ref, a_ref, y_ref, hf_ref, x_scr, yg_scr, s_scr, hp_scr,
               *, rows, independent, tail_rows):
    half = 2 * SSM_STATE
    tail = pl.ds(rows - tail_rows, tail_rows)
    rt = min(rows, S5_RELAYOUT_ROWS)
    half_chunk = S5_CHUNK // 2

    def gather_tile(t, carry):
        r0 = pl.multiple_of(t * rt, rt)
        for hf in range(2):
            vs = [u_ref[pl.ds(r0 * S5_CHUNK + hf * half_chunk + i, rt, stride=S5_CHUNK), :]
                  for i in range(half_chunk)]
            outs = _block_transpose8(vs)
            for g in range(GROUPS_PER_SLAB):
                x_scr[g, pl.ds(r0, rt), hf * LANES:(hf + 1) * LANES] = outs[g]
        return carry

    lax.fori_loop(0, rows // rt, gather_tile, 0)

    row = lax.broadcasted_iota(jnp.int32, (SUBLANES, half), 0)
    n_tap = S5_CHUNK * SSM_GROUP
    for pi in range(PAIRS_PER_SLAB):
        def part(w_ref, *idx):
            return w_ref[(pi, 0) + idx], w_ref[(pi, 1) + idx]

        x0 = x_scr[2 * pi]
        x1 = x_scr[2 * pi + 1]
        x01 = jnp.concatenate([x0, x1], axis=1)
        s_scr[...] = _pdot(x01, ws_ref[pi, 0])
        if tail_rows:
            s_scr[tail, :] += _pdot_lo_terms(x01[rows - tail_rows:], *part(ws_ref))
        ap = a_ref[pi]
        h0 = h0_ref[pi]
        if independent:
            hp_scr[...] = h0
            s = s_scr[...]
            n_re, n_im = _cmul(ap[1:2, :half], ap[1:2, half:], h0[:, :half], h0[:, half:])
            hf_ref[pi] = jnp.concatenate([n_re + s[:, :half], n_im + s[:, half:]], axis=1)
        else:
            pw_re, pw_im = ap[0:SUBLANES, :half], ap[0:SUBLANES, half:]

            def scan_tile(t, carry):
                h_re, h_im = carry
                r0 = pl.multiple_of(t * SUBLANES, SUBLANES)
                s = s_scr[pl.ds(r0, SUBLANES), :]
                t_re, t_im = s[:, :half], s[:, half:]
                for d in (1, 2, 4):
                    sh_re = jnp.where(row >= d, pltpu.roll(t_re, d, 0), 0.0)
                    sh_im = jnp.where(row >= d, pltpu.roll(t_im, d, 0), 0.0)
                    m_re, m_im = _cmul(ap[d:d + 1, :half], ap[d:d + 1, half:], sh_re, sh_im)
                    t_re, t_im = t_re + m_re, t_im + m_im
                e_re = jnp.where(row >= 1, pltpu.roll(t_re, 1, 0), 0.0)
                e_im = jnp.where(row >= 1, pltpu.roll(t_im, 1, 0), 0.0)
                c_re, c_im = _cmul(pw_re, pw_im, h_re, h_im)
                hp_scr[pl.ds(r0, SUBLANES), :] = jnp.concatenate([e_re + c_re, e_im + c_im], axis=1)
                o_re, o_im = _cmul(ap[SUBLANES:SUBLANES + 1, :half], ap[SUBLANES:SUBLANES + 1, half:],
                                   h_re, h_im)
                last = SUBLANES - 1
                n_re = jnp.broadcast_to(t_re[last:last + 1], h_re.shape) + o_re
                n_im = jnp.broadcast_to(t_im[last:last + 1], h_im.shape) + o_im
                return n_re, n_im

            init = (jnp.broadcast_to(h0[:, :half], (SUBLANES, half)),
                    jnp.broadcast_to(h0[:, half:], (SUBLANES, half)))
            h_re, h_im = lax.fori_loop(0, rows // SUBLANES, scan_tile, init, unroll=S5_SCAN_UNROLL)
            hf_ref[pi] = jnp.concatenate([h_re[0:1], h_im[0:1]], axis=1)
        nt = (((1,), (1,)), ((), ()))
        yc = _pdot(hp_scr[...], wo_ref[pi, 0], dims=nt)
        yg_scr[2 * pi] = _pdot(x0, m_ref[pi, 0, 0]) + yc[:, :n_tap]
        yg_scr[2 * pi + 1] = _pdot(x1, m_ref[pi, 0, 1]) + yc[:, n_tap:]
        if tail_rows:
            yc_lo = _pdot_lo_terms(hp_scr[tail, :], *part(wo_ref), dims=nt)
            yg_scr[2 * pi, tail, :] += _pdot_lo_terms(x0[rows - tail_rows:], *part(m_ref, 0)) + yc_lo[:, :n_tap]
            yg_scr[2 * pi + 1, tail, :] += (_pdot_lo_terms(x1[rows - tail_rows:], *part(m_ref, 1))
                                            + yc_lo[:, n_tap:])

    def scatter_tile(t, carry):
        r0 = pl.multiple_of(t * rt, rt)
        for hf in range(2):
            vs = [yg_scr[g, pl.ds(r0, rt), hf * LANES:(hf + 1) * LANES] for g in range(GROUPS_PER_SLAB)]
            outs = _block_transpose8(vs)
            for i in range(half_chunk):
                y_ref[pl.ds(r0 * S5_CHUNK + hf * half_chunk + i, rt, stride=S5_CHUNK), :] = outs[i]
        return carry

    lax.fori_loop(0, rows // rt, scatter_tile, 0)


def _s5_mixer(u, h0_re, h0_im, prep, bsz, t_len, tail):
    m, wsp, wop, a16 = prep
    parts = 2 if tail > 0 else 1
    n_tap = S5_CHUNK * SSM_GROUP
    st = 4 * SSM_STATE
    independent = t_len == S5_CHUNK
    if independent:
        nblk, rows, hrows = 1, bsz, bsz
    else:
        nblk, rows, hrows = bsz, t_len // S5_CHUNK, 1
    assert t_len % S5_CHUNK == 0 and rows % SUBLANES == 0, (bsz, t_len)
    h0p = jnp.concatenate([h0_re.reshape(bsz, N_PAIRS, 2 * SSM_STATE),
                           h0_im.reshape(bsz, N_PAIRS, 2 * SSM_STATE)], axis=2).astype(F32)
    h0p = h0p.transpose(1, 0, 2)[None] if independent else h0p[:, :, None, :]
    pp = PAIRS_PER_SLAB

    def wspec(*shape):
        return pl.BlockSpec((pp,) + shape, lambda q, b: (q,) + (0,) * len(shape))

    frames = rows * S5_CHUNK
    if tail <= 0:
        tail_rows = 0
    elif independent:
        tail_rows = rows
    else:
        tail_rows = min(rows, -(-tail // (S5_CHUNK * SUBLANES)) * SUBLANES)
    y, hf = pl.pallas_call(
        functools.partial(_s5_kernel, rows=rows, independent=independent, tail_rows=tail_rows),
        grid=(N_SLABS, nblk),
        in_specs=[pl.BlockSpec((None, frames, LANES), lambda q, b: (q, b, 0)),
                  pl.BlockSpec((None, pp, hrows, st), lambda q, b: (b, q, 0, 0)),
                  wspec(parts, 2, n_tap, n_tap), wspec(parts, 2 * n_tap, st), wspec(parts, 2 * n_tap, st),
                  wspec(2 * SUBLANES, st)],
        out_specs=[pl.BlockSpec((None, frames, LANES), lambda q, b: (q, b, 0)),
                   pl.BlockSpec((None, pp, hrows, st), lambda q, b: (b, q, 0, 0))],
        out_shape=[jax.ShapeDtypeStruct(u.shape, F32),
                   jax.ShapeDtypeStruct((nblk, N_PAIRS, hrows, st), F32)],
        scratch_shapes=[pltpu.VMEM((GROUPS_PER_SLAB, rows, n_tap), F32),
                        pltpu.VMEM((GROUPS_PER_SLAB, rows, n_tap), F32),
                        pltpu.VMEM((rows, st), F32), pltpu.VMEM((rows, st), F32)],
        compiler_params=_params("parallel", "parallel"),
        name="s5_core",
    )(u, h0p, m, wsp, wop, a16)
    hf = hf[0].transpose(1, 0, 2) if independent else hf[:, :, 0, :]
    hf_re = hf[:, :, :2 * SSM_STATE].reshape(bsz, N_GROUPS, SSM_STATE)
    hf_im = hf[:, :, 2 * SSM_STATE:].reshape(bsz, N_GROUPS, SSM_STATE)
    return y, hf_re, hf_im


def _conv_kernel(v_ref, left_ref, w_ref, b_ref, g_ref, beta_ref, y_ref, cn_ref, vbuf, shifted, *, tm):
    @pl.when(pl.program_id(1) == 0)
    def _():
        vbuf[0:CONV_HALO, :] = left_ref[...]

    vbuf[CONV_HALO:CONV_HALO + tm, :] = v_ref[...]
    first = CONV_HALO - (CONV_WIDTH - 1)
    span = tm + CONV_HALO - SUBLANES
    for r in range(1, SUBLANES):
        shifted[r - 1, 0:span, :] = vbuf[r:r + span, :]
    acc = jnp.zeros((tm, D_CONV), F32)
    for k in range(CONV_WIDTH):
        a, r = divmod(first + k, SUBLANES)
        src = vbuf if r == 0 else shifted.at[r - 1]
        acc = acc + w_ref[k:k + 1, :] * src[a * SUBLANES:a * SUBLANES + tm, :]
    y = acc + b_ref[...]
    mu = jnp.mean(y, axis=-1, keepdims=True)
    yc = y - mu
    var = jnp.mean(yc * yc, axis=-1, keepdims=True)
    yn = yc * lax.rsqrt(var + EPS) * g_ref[...] + beta_ref[...]
    y_ref[...] = yn * jax.nn.sigmoid(yn)
    cn_ref[...] = vbuf[tm + first:tm + CONV_HALO, :]
    vbuf[0:CONV_HALO, :] = vbuf[tm:tm + CONV_HALO, :]


def _conv_mixer(v, left, w, b, g, beta, layer, bsz, t_len, tm):
    left = jnp.pad(left.astype(F32), ((0, 0), (CONV_HALO - (CONV_WIDTH - 1), 0), (0, 0)))
    nt = t_len // tm
    vec = _layer_spec(layer, 1, D_CONV)
    return pl.pallas_call(
        functools.partial(_conv_kernel, tm=tm),
        grid=(bsz, nt),
        in_specs=[pl.BlockSpec((tm, D_CONV), lambda bi, j: (bi * nt + j, 0)),
                  pl.BlockSpec((None, CONV_HALO, D_CONV), lambda bi, j: (bi, 0, 0)),
                  _layer_spec(layer, CONV_WIDTH, D_CONV), vec, vec, vec],
        out_specs=[pl.BlockSpec((tm, D_CONV), lambda bi, j: (bi * nt + j, 0)),
                   pl.BlockSpec((None, CONV_WIDTH - 1, D_CONV), lambda bi, j: (bi, 0, 0))],
        out_shape=[jax.ShapeDtypeStruct((bsz * t_len, D_CONV), F32),
                   jax.ShapeDtypeStruct((bsz, CONV_WIDTH - 1, D_CONV), F32)],
        scratch_shapes=[pltpu.VMEM((tm + CONV_HALO, D_CONV), F32),
                        pltpu.VMEM((SUBLANES - 1, tm + CONV_HALO - SUBLANES, D_CONV), F32)],
        compiler_params=_params("parallel", "arbitrary"),
        name="conv_mixer",
    )(v, left, w, _vec(b), _vec(g), _vec(beta))


def _route(logits):
    lane_i = lax.broadcasted_iota(jnp.int32, logits.shape, 1)
    lane = lane_i.astype(F32)
    group_of_lane = (lane_i >> _log2(EXPERTS_PER_GROUP)).astype(F32)
    neg = -jnp.inf
    far = float(LANES)
    is_g = (lane_i >= N_EXPERTS) & (lane_i < N_EXPERTS + N_EXPERT_GROUPS)
    gl = jnp.where(is_g, logits, neg)
    g_max = jnp.max(gl, axis=-1, keepdims=True)
    g_lane = jnp.min(jnp.where(gl == g_max, lane, far), axis=-1, keepdims=True)
    g_gate = 1.0 / jnp.sum(jnp.exp(gl - g_max), axis=-1, keepdims=True)
    g_idx = g_lane - float(N_EXPERTS)
    in_group = (lane_i < N_EXPERTS) & (group_of_lane == g_idx)
    el = jnp.where(in_group, logits, neg)
    v1 = jnp.max(el, axis=-1, keepdims=True)
    i1 = jnp.min(jnp.where(el == v1, lane, far), axis=-1, keepdims=True)
    el2 = jnp.where(lane == i1, neg, el)
    v2 = jnp.max(el2, axis=-1, keepdims=True)
    i2 = jnp.min(jnp.where(el2 == v2, lane, far), axis=-1, keepdims=True)
    e2 = jnp.exp(v2 - v1)
    w1 = g_gate / (1.0 + e2)
    w2 = g_gate * e2 / (1.0 + e2)
    return i1, i2, w1, w2


ROUTE_E1, ROUTE_E2, ROUTE_W1, ROUTE_W2, ROUTE_RANK1, ROUTE_RANK2 = range(6)


def _outproj_kernel(x_ref, ys_ref, u_ref, yb_ref, d_ref, wglu_ref, wout_ref, nf_ref, rw_ref, rb_ref, cnt0_ref,
                    x1_ref, hn_ref, route_ref, rt_ref, cnt_ref, wglu_bf, wout_bf, rw_bf, *, tail_tiles):
    @pl.when(pl.program_id(0) == 0)
    def _():
        cnt_ref[...] = cnt0_ref[...]
        rw = rw_ref[...]
        rw_hi = rw.astype(BF16)
        rw_bf[:, 0:LANES] = rw_hi
        rw_bf[:, LANES:] = (rw - rw_hi.astype(F32)).astype(BF16)

    _cast_weight_once(wglu_ref, wglu_bf)
    _cast_weight_once(wout_ref, wout_bf)
    def mix_in(precise):
        ys = jnp.concatenate([ys_ref[q] for q in range(N_SLABS)], axis=1)
        u = jnp.concatenate([u_ref[q] for q in range(N_SLABS)], axis=1)
        z = jax.nn.gelu(ys + d_ref[...] * u)
        ya = z * jax.nn.sigmoid(_pdot(z, *_parts(wglu_bf, precise=precise)))
        mix = _pdot(jnp.concatenate([ya, yb_ref[...]], axis=1), *_parts(wout_bf, precise=precise))
        x1_ref[...] = x_ref[...] + mix

    _per_tile_precision(tail_tiles, wout_bf.shape[0] == 2, mix_in)
    x1 = x1_ref[...]
    hn = _rms(x1, nf_ref[...])
    _store_row_tiles(hn_ref, hn)
    h_hi = hn.astype(BF16)
    h_lo = (hn - h_hi.astype(F32)).astype(BF16)
    hw = jnp.dot(h_hi, rw_bf[...], preferred_element_type=F32)
    logits = (hw[:, :LANES] + hw[:, LANES:]
              + jnp.dot(h_lo, rw_bf[:, 0:LANES], preferred_element_type=F32) + rb_ref[...])
    i1, i2, w1, w2 = _route(logits)
    tm = logits.shape[0]
    lane_i = lax.broadcasted_iota(jnp.int32, logits.shape, 1)
    lane = lane_i.astype(F32)
    picked = jnp.where((lane == i1) | (lane == i2), 1.0, 0.0)
    earlier = (lax.broadcasted_iota(jnp.int32, (tm, tm), 0) > lax.broadcasted_iota(jnp.int32, (tm, tm), 1))
    prefix = _bdot(jnp.where(earlier, 1.0, 0.0), picked.astype(BF16)) + cnt_ref[...]
    rank1 = jnp.sum(jnp.where(lane == i1, prefix, 0.0), axis=-1, keepdims=True)
    rank2 = jnp.sum(jnp.where(lane == i2, prefix, 0.0), axis=-1, keepdims=True)
    cnt_ref[...] += jnp.sum(picked, axis=0, keepdims=True)
    rec = jnp.zeros_like(logits)
    for lane_id, val in ((ROUTE_E1, i1), (ROUTE_E2, i2), (ROUTE_W1, w1), (ROUTE_W2, w2),
                         (ROUTE_RANK1, rank1), (ROUTE_RANK2, rank2)):
        rec = jnp.where(lane_i == lane_id, val, rec)
    route_ref[...] = rec
    rt_ref[...] = rec.T[0:SUBLANES, :]


def _outproj(x, ys, u, yb, d, wglu, wout, nf, rw, rb, counts0, layer, tm, t_len, tail):
    n = x.shape[0]
    precise = tail > 0
    return pl.pallas_call(
        functools.partial(_outproj_kernel, tail_tiles=_stream_tail_tiles(t_len, tm, tail)),
        grid=(n // tm,),
        in_specs=[_row_spec(tm, D_MODEL), _slab_spec(tm), _slab_spec(tm), _row_spec(tm, D_CONV),
                  _layer_spec(layer, 1, D_SSM), _layer_spec(layer, D_SSM, D_SSM),
                  _layer_spec(layer, D_MODEL, D_MODEL), _layer_spec(layer, 1, D_MODEL),
                  _const_spec((D_MODEL, LANES)), _const_spec((1, LANES)), _const_spec((1, LANES))],
        out_specs=[_row_spec(tm, D_MODEL), _row_tile_spec(tm), _row_spec(tm, LANES),
                   pl.BlockSpec((SUBLANES, tm), lambda i: (0, i)), _const_spec((1, LANES))],
        out_shape=[jax.ShapeDtypeStruct((n, D_MODEL), F32), jax.ShapeDtypeStruct(_tiled_rows(n), F32),
                   jax.ShapeDtypeStruct((n, LANES), F32), jax.ShapeDtypeStruct((SUBLANES, n), F32),
                   jax.ShapeDtypeStruct((1, LANES), F32)],
        scratch_shapes=[_weight_scratch(D_SSM, D_SSM, precise), _weight_scratch(D_MODEL, D_MODEL, precise),
                        pltpu.VMEM((D_MODEL, 2 * LANES), BF16)],
        compiler_params=_params("arbitrary"),
        name="outproj_router",
    )(x, ys, u, yb, _vec(d), wglu, wout, _vec(nf), rw, rb, counts0)


PLAN_TILE_LANES = 2 * LANES
PLAN_EXPERT, PLAN_ROWS, PLAN_USED = range(3)


def _moe_plan_kernel(rt_ref, cnt_ref, cnt_first_ref, slot_ref, tile_ref, *, tms):
    cnt = cnt_ref[...]
    padded = jnp.ceil(cnt * (1.0 / tms)) * float(tms)
    r = lax.broadcasted_iota(jnp.int32, (LANES, LANES), 0)
    c = lax.broadcasted_iota(jnp.int32, (LANES, LANES), 1)
    ends = jnp.dot(padded, jnp.where(r <= c, 1.0, 0.0), precision=lax.Precision.HIGHEST,
                   preferred_element_type=F32)
    starts = ends - padded
    rt = rt_ref[...]
    e1, e2 = rt[ROUTE_E1:ROUTE_E1 + 1], rt[ROUTE_E2:ROUTE_E2 + 1]
    s1, s2 = rt[ROUTE_RANK1:ROUTE_RANK1 + 1], rt[ROUTE_RANK2:ROUTE_RANK2 + 1]
    tile = lax.broadcasted_iota(jnp.int32, (1, PLAN_TILE_LANES), 1).astype(F32)
    used = ends[:, N_EXPERTS - 1:N_EXPERTS] * (1.0 / tms)
    pos = jnp.minimum(tile, used - 1.0) * float(tms)
    t_exp = jnp.zeros_like(tile)
    t_fill = jnp.zeros_like(tile)
    for e in range(N_EXPERTS):
        st, en = starts[:, e:e + 1], ends[:, e:e + 1]
        s1 = s1 + jnp.where(e1 == float(e), st, 0.0)
        s2 = s2 + jnp.where(e2 == float(e), st, 0.0)
        mine = (pos >= st) & (pos < en)
        t_exp = t_exp + jnp.where(mine, float(e), 0.0)
        t_fill = t_fill + jnp.where(mine, st + cnt_first_ref[:, e:e + 1], 0.0)
    t_rows = jnp.where(tile < used, jnp.clip(t_fill - pos, 0.0, float(tms)), 0.0)
    slot_ref[...] = jnp.concatenate([s1, s2], axis=0).astype(jnp.int32)
    tile_ref[...] = jnp.concatenate(
        [t_exp, t_rows, jnp.broadcast_to(used, tile.shape), jnp.zeros((SUBLANES - 3, PLAN_TILE_LANES), F32)],
        axis=0).astype(jnp.int32)


def _moe_plan(route_t, counts, counts_first, tms, n_tiles):
    n = route_t.shape[1]
    assert n_tiles <= PLAN_TILE_LANES
    slots, tiles = pl.pallas_call(
        functools.partial(_moe_plan_kernel, tms=tms),
        out_shape=[jax.ShapeDtypeStruct((2, n), jnp.int32),
                   jax.ShapeDtypeStruct((SUBLANES, PLAN_TILE_LANES), jnp.int32)],
        compiler_params=pltpu.CompilerParams(vmem_limit_bytes=VMEM_LIMIT),
        name="moe_plan",
    )(route_t, counts, counts_first)
    return slots, tiles[PLAN_EXPERT, :n_tiles], tiles[PLAN_ROWS, :n_tiles], tiles[PLAN_USED, :1]


DMA_UNROLL = 16


HN_BUFFERS = 3
ZERO_FILL_ROWS = 128


def _dispatch_kernel(tr_ref, slot_ref, hn_ref, *rest, tm, tms, n_tiles, n_steps, fresh):
    xs_ref, zbuf, hbuf, in_sem, out_sem, zsem = rest if fresh else rest[1:]
    i = pl.program_id(0)

    def fetch(t):
        b = lax.rem(t, HN_BUFFERS)
        first = pl.multiple_of(t * (tm * ROW_SUB), tm * ROW_SUB)
        return pltpu.make_async_copy(hn_ref.at[pl.ds(first, tm * ROW_SUB)], hbuf.at[b], in_sem.at[b])

    def drain_scatter(t):
        b = lax.rem(t, HN_BUFFERS)
        for k in range(2):
            pltpu.make_async_copy(hbuf.at[b], xs_ref.at[pl.ds(0, tm * ROW_SUB)], out_sem.at[b]).wait()

    @pl.when(i == 0)
    def _():
        fetch(i).start()

    @pl.when(jnp.logical_and(i == 0, fresh))
    def _():
        zbuf[...] = jnp.zeros_like(zbuf)

        zrows = zbuf.shape[0] // ROW_SUB
        per_tile = tms // zrows

        def unfilled(p):
            return tr_ref[p // per_tile] < (lax.rem(p, per_tile) + 1) * zrows

        def fill(p, carry):
            @pl.when(unfilled(p))
            def _():
                first = pl.multiple_of(p * (zrows * ROW_SUB), zrows * ROW_SUB)
                pltpu.make_async_copy(zbuf, xs_ref.at[pl.ds(first, zrows * ROW_SUB)], zsem).start()
            return carry

        def drain(p, carry):
            @pl.when(unfilled(p))
            def _():
                pltpu.make_async_copy(zbuf, xs_ref.at[pl.ds(0, zrows * ROW_SUB)], zsem).wait()
            return carry

        lax.fori_loop(0, n_tiles * per_tile, fill, 0)
        lax.fori_loop(0, n_tiles * per_tile, drain, 0)

    @pl.when(i + 1 < n_steps)
    def _():
        fetch(i + 1).start()

    fetch(i).wait()
    b = lax.rem(i, HN_BUFFERS)
    rows = hbuf.at[b]

    def issue(r, carry):
        for k in range(2):
            pltpu.make_async_copy(_one_row(rows, r), _one_row(xs_ref, slot_ref[k, r]),
                                  out_sem.at[b]).start(priority=k)
        return carry

    lax.fori_loop(0, tm, issue, 0, unroll=DMA_UNROLL)

    @pl.when(i >= 1)
    def _():
        drain_scatter(i - 1)

    @pl.when(i == n_steps - 1)
    def _():
        drain_scatter(i)


def _dispatch(hn, slots, tile_rows, tms, tm, into=None):
    n = hn.shape[0] // ROW_SUB
    n_tiles = tile_rows.shape[0]
    fresh = into is None
    any_spec = pl.BlockSpec(memory_space=pl.ANY)
    return pl.pallas_call(
        functools.partial(_dispatch_kernel, tm=tm, tms=tms, n_tiles=n_tiles, n_steps=n // tm, fresh=fresh),
        grid_spec=pltpu.PrefetchScalarGridSpec(
            num_scalar_prefetch=1, grid=(n // tm,),
            in_specs=[pl.BlockSpec((2, tm), lambda i, tr: (0, i), memory_space=pltpu.SMEM), any_spec]
                     + ([] if fresh else [any_spec]),
            out_specs=any_spec,
            scratch_shapes=[pltpu.VMEM(_tiled_rows(min(tms, ZERO_FILL_ROWS)), F32),
                            pltpu.VMEM((HN_BUFFERS,) + _tiled_rows(tm), F32),
                            pltpu.SemaphoreType.DMA((HN_BUFFERS,)), pltpu.SemaphoreType.DMA((HN_BUFFERS,)),
                            pltpu.SemaphoreType.DMA]),
        out_shape=jax.ShapeDtypeStruct(_tiled_rows(n_tiles * tms), F32),
        input_output_aliases={} if fresh else {3: 0},
        compiler_params=_params("arbitrary"),
        name="moe_dispatch",
    )(tile_rows, slots, hn, *([] if fresh else [into]))


def _moe_kernel(te_ref, nu_ref, x_ref, wg_ref, wu_ref, wd_ref, y_ref, wg_bf, wu_bf, wd_bf):
    i = pl.program_id(0)
    in_use = i < nu_ref[0]
    new_expert = (i == 0) | (te_ref[i] != te_ref[jnp.maximum(i - 1, 0)])

    @pl.when(in_use & new_expert)
    def _():
        wg_bf[...] = wg_ref[...].astype(BF16)
        wu_bf[...] = wu_ref[...].astype(BF16)
        wd_bf[...] = wd_ref[...].astype(BF16)

    @pl.when(in_use)
    def _():
        h = _load_row_tiles(x_ref).astype(BF16)
        hg = jnp.dot(h, wg_bf[...], preferred_element_type=F32)
        hu = jnp.dot(h, wu_bf[...], preferred_element_type=F32)
        _store_row_tiles(y_ref, _bdot(hg * jax.nn.sigmoid(hg) * hu, wd_bf[...]))

    @pl.when(jnp.logical_not(in_use))
    def _():
        y_ref[...] = jnp.zeros_like(y_ref)


def _moe(xs, tile_expert, n_used, wg, wu, wd, layer, tms):
    n_slots = xs.shape[0] // ROW_SUB
    rows = pl.BlockSpec(_tiled_rows(tms), lambda i, te, nu: (jnp.minimum(i, nu[0] - 1), 0))
    out_rows = pl.BlockSpec(_tiled_rows(tms), lambda i, te, nu: (i, 0))

    def wspec(a, b):
        return pl.BlockSpec((None, None, a, b), lambda i, te, nu: (layer, te[i], 0, 0))

    return pl.pallas_call(
        _moe_kernel,
        grid_spec=pltpu.PrefetchScalarGridSpec(
            num_scalar_prefetch=2, grid=(n_slots // tms,),
            in_specs=[rows, wspec(D_MODEL, D_EXPERT), wspec(D_MODEL, D_EXPERT), wspec(D_EXPERT, D_MODEL)],
            out_specs=out_rows,
            scratch_shapes=[pltpu.VMEM((D_MODEL, D_EXPERT), BF16), pltpu.VMEM((D_MODEL, D_EXPERT), BF16),
                            pltpu.VMEM((D_EXPERT, D_MODEL), BF16)]),
        out_shape=jax.ShapeDtypeStruct(xs.shape, F32),
        compiler_params=_params("arbitrary"),
        name="moe",
    )(tile_expert, n_used, xs, wg, wu, wd)


def _ple_kernel(slot_ref, next_slot_ref, x_ref, route_ref, p_ref, np_ref, wple_ref, wgate_ref, nfin_ref,
                ys_ref, o_ref, ybuf, sem, wple_bf, wgate_bf, *, tm, n_steps, final):
    i = pl.program_id(0)
    _cast_weight_once(wple_ref, wple_bf)
    _cast_weight_once(wgate_ref, wgate_bf)

    def gather(slots, b):
        def issue(r, carry):
            for k in range(2):
                pltpu.make_async_copy(_one_row(ys_ref, slots[k, r]), _one_row(ybuf.at[b, k], r),
                                      sem.at[b]).start(priority=k)
            return carry

        lax.fori_loop(0, tm, issue, 0, unroll=DMA_UNROLL)

    @pl.when(i == 0)
    def _():
        gather(slot_ref, 0)

    @pl.when(i + 1 < n_steps)
    def _():
        gather(next_slot_ref, lax.rem(i + 1, 2))

    pe = _pdot(p_ref[...], *_parts(wple_bf))
    b = lax.rem(i, 2)
    for k in range(2):
        pltpu.make_async_copy(ys_ref.at[pl.ds(0, tm * ROW_SUB)], ybuf.at[b, k], sem.at[b]).wait()
    route = route_ref[...]
    x = (x_ref[...] + route[:, ROUTE_W1:ROUTE_W1 + 1] * _load_row_tiles(ybuf.at[b, 0])
         + route[:, ROUTE_W2:ROUTE_W2 + 1] * _load_row_tiles(ybuf.at[b, 1]))
    gate = jax.nn.sigmoid(_pdot(_rms(x, np_ref[...]), *_parts(wgate_bf)))
    out = x + pe * gate
    if final:
        out = _rms(out, nfin_ref[...])
    o_ref[...] = out


def _ple(x, route, slots, ys, p, npl, wple, wgate, nfin, layer, tm, final):
    n = x.shape[0]
    n_steps = n // tm
    return pl.pallas_call(
        functools.partial(_ple_kernel, tm=tm, n_steps=n_steps, final=final),
        grid=(n_steps,),
        in_specs=[pl.BlockSpec((2, tm), lambda i: (0, i), memory_space=pltpu.SMEM),
                  pl.BlockSpec((2, tm), lambda i: (0, jnp.minimum(i + 1, n_steps - 1)),
                               memory_space=pltpu.SMEM),
                  _row_spec(tm, D_MODEL), _row_spec(tm, LANES),
                  pl.BlockSpec((None, tm, D_PLE), lambda i: (layer, i, 0)),
                  _layer_spec(layer, 1, D_MODEL), _layer_spec(layer, D_PLE, D_MODEL),
                  _layer_spec(layer, D_MODEL, D_MODEL), _const_spec((1, D_MODEL)),
                  pl.BlockSpec(memory_space=pl.ANY)],
        out_specs=_row_spec(tm, D_MODEL),
        out_shape=jax.ShapeDtypeStruct((n, D_MODEL), F32),
        scratch_shapes=[pltpu.VMEM((2, 2) + _tiled_rows(tm), F32), pltpu.SemaphoreType.DMA((2,)),
                        _weight_scratch(D_PLE, D_MODEL, False), _weight_scratch(D_MODEL, D_MODEL, False)],
        compiler_params=_params("arbitrary"),
        name="combine_ple",
    )(slots, slots, x, route, p, _vec(npl), wple, wgate, nfin.reshape(1, D_MODEL), ys)


def _mixers(x, h0_re, h0_im, conv_left, w, s5, router_w, router_b, counts0, layer, bsz, t_len, tm, tm_conv,
            tail):
    u, v = _inproj(x, w["norm_mix"], w["w_in"], layer, min(2 * tm, x.shape[0]), t_len, tail)
    ys, hf_re, hf_im = _s5_mixer(u, h0_re, h0_im, s5, bsz, t_len, tail)
    yb, conv_new = _conv_mixer(v, conv_left, w["conv_w"], w["conv_b"], w["conv_ln_g"], w["conv_ln_b"],
                               layer, bsz, t_len, tm_conv)
    x1, hn, route, route_t, counts = _outproj(x, ys, u, yb, w["ssm_d"], w["w_ssm_glu"], w["w_out"],
                                              w["norm_ffn"], router_w, router_b, counts0, layer, tm, t_len, tail)
    return dict(x1=x1, hn=hn, route=route, route_t=route_t, counts=counts, state=(hf_re, hf_im, conv_new))


def _moe_and_ple(sets, ps, tms_rows, w, layer, tms, final):
    sizes = [s["x1"].shape[0] for s in sets]
    n_tiles = 2 * sum(sizes) // tms + N_EXPERTS
    route_t = jnp.concatenate([s["route_t"] for s in sets], axis=1)
    slots, tile_expert, tile_rows, n_used = _moe_plan(route_t, sets[-1]["counts"], sets[0]["counts"], tms,
                                                     n_tiles)
    starts = [sum(sizes[:j]) for j in range(len(sets))]
    set_slots = [slots[:, a:a + n] for a, n in zip(starts, sizes)]
    xsort = None
    for s, sl, tm in zip(sets, set_slots, tms_rows):
        xsort = _dispatch(s["hn"], sl, tile_rows, tms, min(2 * tm, s["x1"].shape[0]), into=xsort)
    ysort = _moe(xsort, tile_expert, n_used, w["expert_w_gate"], w["expert_w_up"], w["expert_w_down"],
                 layer, tms)
    return [_ple(s["x1"], s["route"], sl, ysort, p, w["norm_ple"], w["ple_w"], w["ple_gate_w"],
                 w["norm_final"], layer, min(2 * tm, s["x1"].shape[0]), final)
            for s, sl, p, tm in zip(sets, set_slots, ps, tms_rows)]


def kernel(x_prompt, x_sample, p_prompt, p_sample, state_ssm_re, state_ssm_im, cache_conv, norm_mix, w_in, ssm_a_re, ssm_a_im, ssm_b_re, ssm_b_im, ssm_c_re, ssm_c_im, ssm_d, ssm_log_dt, w_ssm_glu, conv_w, conv_b, conv_ln_g, conv_ln_b, w_out, norm_ffn, router_group_w, router_group_b, router_expert_w, router_expert_b, expert_w_gate, expert_w_up, expert_w_down, norm_ple, ple_w, ple_gate_w, norm_final):
    depth = w_in.shape[0]
    bp, tp, _ = x_prompt.shape
    bs, ts, _ = x_sample.shape
    xp = x_prompt.reshape(bp * tp, D_MODEL)
    xs = x_sample.reshape(bs * ts, D_MODEL)
    pp = p_prompt.reshape(depth, bp * tp, D_PLE)
    ps = p_sample.reshape(depth, bs * ts, D_PLE)
    zero_state = jnp.zeros((bp, N_GROUPS, SSM_STATE), F32)
    zero_conv = jnp.zeros((bp, CONV_WIDTH - 1, D_CONV), F32)
    pad_lanes = LANES - N_EXPERTS - N_EXPERT_GROUPS
    w = {"norm_mix": norm_mix, "w_in": w_in, "ssm_d": ssm_d, "w_ssm_glu": w_ssm_glu, "conv_w": conv_w,
         "conv_b": conv_b, "conv_ln_g": conv_ln_g, "conv_ln_b": conv_ln_b, "w_out": w_out,
         "norm_ffn": norm_ffn, "expert_w_gate": expert_w_gate, "expert_w_up": expert_w_up,
         "expert_w_down": expert_w_down, "norm_ple": norm_ple, "ple_w": ple_w, "ple_gate_w": ple_gate_w,
         "norm_final": norm_final}
    no_picks = jnp.zeros((1, LANES), F32)
    outs = {k: [] for k in ("pr_re", "pr_im", "pr_conv", "sm_re", "sm_im", "sm_conv")}
    for i in range(depth):
        s5 = _s5_prep(ssm_a_re[i], ssm_a_im[i], ssm_log_dt[i], ssm_b_re[i], ssm_b_im[i], ssm_c_re[i],
                      ssm_c_im[i])
        router_w = jnp.pad(jnp.concatenate([router_expert_w[i], router_group_w[i]], axis=1),
                           ((0, 0), (0, pad_lanes)))
        router_b = jnp.pad(jnp.concatenate([router_expert_b[i], router_group_b[i]]),
                           (0, pad_lanes)).reshape(1, LANES)
        final = i == depth - 1
        tail = 0 if final else PRECISE_TAIL
        mp = _mixers(xp, zero_state, zero_state, zero_conv, w, s5, router_w, router_b, no_picks, i,
                     bp, tp, tm=TOKEN_TILE, tm_conv=2 * TOKEN_TILE, tail=tail)
        ms = _mixers(xs, state_ssm_re[i], state_ssm_im[i], cache_conv[i], w, s5, router_w, router_b,
                     mp["counts"], i, bs, ts, tm=bs * ts, tm_conv=ts, tail=tail)
        xp, xs = _moe_and_ple([mp, ms], [pp, ps], [TOKEN_TILE, bs * ts], w, i, SLOT_TILE, final)
        for key, val in zip(("pr_re", "pr_im", "pr_conv"), mp["state"]):
            outs[key].append(val)
        for key, val in zip(("sm_re", "sm_im", "sm_conv"), ms["state"]):
            outs[key].append(val)
    return (xp.reshape(bp, tp, D_MODEL), xs.reshape(bs, ts, D_MODEL),
            jnp.stack(outs["pr_re"]), jnp.stack(outs["pr_im"]), jnp.stack(outs["pr_conv"]),
            jnp.stack(outs["sm_re"]), jnp.stack(outs["sm_im"]), jnp.stack(outs["sm_conv"]))
```

```python
import functools

import jax
import jax.numpy as jnp
from jax import lax
from jax.experimental import pallas as pl
from jax.experimental.pallas import tpu as pltpu

F32 = jnp.float32
BF16 = jnp.bfloat16

D_MODEL = 1024
D_SSM = 512
SSM_GROUP = 16
N_GROUPS = D_SSM // SSM_GROUP
N_PAIRS = N_GROUPS // 2
SSM_STATE = 64
D_CONV = 512
CONV_WIDTH = 31
CONV_HALO = 32
N_EXPERT_GROUPS = 4
EXPERTS_PER_GROUP = 8
N_EXPERTS = 32
D_EXPERT = 256
D_PLE = 256
EPS = 1e-6
S5_CHUNK = 16
S5_RELAYOUT_ROWS = 256
S5_SCAN_UNROLL = 4
LANES = 128
SUBLANES = 8
N_SLABS = D_SSM // LANES
GROUPS_PER_SLAB = LANES // SSM_GROUP
PAIRS_PER_SLAB = GROUPS_PER_SLAB // 2
VMEM_LIMIT = 56 * 1024 * 1024
TOKEN_TILE = 512
SLOT_TILE = 512
PRECISE_TAIL = 1024


def _log2(n):
    assert n & (n - 1) == 0, n
    return n.bit_length() - 1


def _params(*sem):
    return pltpu.CompilerParams(dimension_semantics=sem, vmem_limit_bytes=VMEM_LIMIT)


def _rms(x, g):
    return x * lax.rsqrt(jnp.mean(x * x, axis=-1, keepdims=True) + EPS) * g


def _bdot(a, b):
    return jnp.dot(a.astype(BF16), b, preferred_element_type=F32)


def _split_bf16(a):
    hi = a.astype(BF16)
    return hi, (a - hi.astype(F32)).astype(BF16)


def _pdot(a, w_hi, w_lo=None, dims=(((1,), (0,)), ((), ()))):
    def mm(x, w):
        return lax.dot_general(x, w, dims, preferred_element_type=F32)

    if w_lo is None:
        return mm(a.astype(BF16), w_hi)
    a_hi, a_lo = _split_bf16(a)
    return mm(a_hi, w_hi) + (mm(a_hi, w_lo) + mm(a_lo, w_hi))


def _pdot_lo_terms(a, w_hi, w_lo, dims=(((1,), (0,)), ((), ()))):
    a_hi, a_lo = _split_bf16(a)
    return (lax.dot_general(a_hi, w_lo, dims, preferred_element_type=F32)
            + lax.dot_general(a_lo, w_hi, dims, preferred_element_type=F32))


def _stream_tail_tiles(t_len, tm, tail):
    tiles = max(t_len // tm, 1)
    return tiles, (max(t_len - tail, 0) // tm if tm < t_len else 0)


def _per_tile_precision(tail_tiles, has_lo, body):
    tiles, first = tail_tiles
    if not has_lo or first == 0:
        body(has_lo)
        return
    in_tail = lax.rem(pl.program_id(0), tiles) >= first
    pl.when(in_tail)(functools.partial(body, True))
    pl.when(jnp.logical_not(in_tail))(functools.partial(body, False))


def _parts(wbf_ref, rows=slice(None), precise=True):
    return wbf_ref[0, rows, :], (wbf_ref[1, rows, :] if precise and wbf_ref.shape[0] == 2 else None)


def _row_spec(tm, width):
    return pl.BlockSpec((tm, width), lambda i: (i, 0))


ROW_SUB = D_MODEL // LANES


def _tiled_rows(n):
    return (n * ROW_SUB, LANES)


def _row_tile_spec(tm):
    return pl.BlockSpec(_tiled_rows(tm), lambda i, *_: (i, 0))


def _one_row(ref, r):
    return ref.at[pl.ds(pl.multiple_of(r * ROW_SUB, ROW_SUB), ROW_SUB)]


def _store_row_tiles(ref, rows):
    n = rows.shape[0]
    for s in range(ROW_SUB):
        ref[pl.ds(s, n, stride=ROW_SUB), :] = rows[:, s * LANES:(s + 1) * LANES]


def _load_row_tiles(ref):
    n = ref.shape[0] // ROW_SUB
    return jnp.concatenate([ref[pl.ds(s, n, stride=ROW_SUB), :] for s in range(ROW_SUB)], axis=1)


def _slab_spec(tm):
    return pl.BlockSpec((N_SLABS, tm, LANES), lambda i: (0, i, 0))


def _const_spec(shape):
    return pl.BlockSpec(shape, lambda i: (0,) * len(shape))


def _layer_spec(layer, *shape):
    return pl.BlockSpec((None,) + shape, lambda *_: (layer,) + (0,) * len(shape))


def _vec(stacked):
    return stacked.reshape(stacked.shape[0], 1, stacked.shape[1])


def _cast_weight_once(w_ref, wbf_ref):
    @pl.when(pl.program_id(0) == 0)
    def _():
        w = w_ref[...]
        hi = w.astype(BF16)
        wbf_ref[0] = hi
        if wbf_ref.shape[0] == 2:
            wbf_ref[1] = (w - hi.astype(F32)).astype(BF16)


def _weight_scratch(k, n, precise):
    return pltpu.VMEM((2 if precise else 1, k, n), BF16)


def _inproj_kernel(x_ref, g_ref, w_ref, u_ref, v_ref, wbf, *, tail_tiles):
    _cast_weight_once(w_ref, wbf)

    def body(precise):
        hn = _rms(x_ref[...], g_ref[...])
        proj = _pdot(hn, *_parts(wbf, precise=precise))
        for q in range(N_SLABS):
            u_ref[q] = proj[:, q * LANES:(q + 1) * LANES]
        v_ref[...] = proj[:, D_SSM:D_SSM + D_CONV] * jax.nn.sigmoid(proj[:, D_SSM + D_CONV:])

    _per_tile_precision(tail_tiles, wbf.shape[0] == 2, body)


def _inproj(x, g, w, layer, tm, t_len, tail):
    n = x.shape[0]
    d_in = w.shape[2]
    precise = tail > 0
    return pl.pallas_call(
        functools.partial(_inproj_kernel, tail_tiles=_stream_tail_tiles(t_len, tm, tail)),
        grid=(n // tm,),
        in_specs=[_row_spec(tm, D_MODEL), _layer_spec(layer, 1, D_MODEL), _layer_spec(layer, D_MODEL, d_in)],
        out_specs=[_slab_spec(tm), _row_spec(tm, D_CONV)],
        out_shape=[jax.ShapeDtypeStruct((N_SLABS, n, LANES), F32), jax.ShapeDtypeStruct((n, D_CONV), F32)],
        scratch_shapes=[_weight_scratch(D_MODEL, d_in, precise)],
        compiler_params=_params("arbitrary"),
        name="inproj",
    )(x, _vec(g), w)


def _s5_prep_kernel(ar_ref, ai_ref, ldt_ref, bre_ref, bim_ref, cre_ref, cim_ref,
                    m_ref, ws_ref, wot_ref, atab_ref, wt_re, wt_im, br_re, br_im):
    n_tap = S5_CHUNK * SSM_GROUP
    st = 4 * SSM_STATE
    nt = (((1,), (1,)), ((), ()))
    hi = lax.Precision.HIGHEST
    lane = lax.broadcasted_iota(jnp.int32, (SSM_GROUP, n_tap), 1)
    ws_rows, wot_rows, atab = [], [], jnp.zeros((2 * SUBLANES, st), F32)
    for gi in range(2):
        ar, ai = ar_ref[gi], ai_ref[gi]
        dt = jnp.exp(ldt_ref[gi])
        k = lax.broadcasted_iota(jnp.int32, (S5_CHUNK + SUBLANES, SSM_STATE), 0).astype(F32)
        mag = jnp.exp(k * (dt * ar))
        ang = k * (dt * ai)
        p_re, p_im = mag * jnp.cos(ang), mag * jnp.sin(ang)
        inv = 1.0 / (ar * ar + ai * ai)
        ab_re, ab_im = p_re[1:2], p_im[1:2]
        ia_re, ia_im = ar * inv, -ai * inv
        coef_re = (ab_re - 1.0) * ia_re - ab_im * ia_im
        coef_im = (ab_re - 1.0) * ia_im + ab_im * ia_re
        bre, bim = bre_ref[gi], bim_ref[gi]
        bb_re = coef_re * bre - coef_im * bim
        bb_im = coef_re * bim + coef_im * bre
        cre, cim = cre_ref[gi], cim_ref[gi]
        for kk in range(S5_CHUNK + 1):
            pr, pi = p_re[kk:kk + 1], p_im[kk:kk + 1]
            rows = slice(kk * SSM_GROUP, (kk + 1) * SSM_GROUP)
            wt_re[rows, :] = pr * cre - pi * cim
            wt_im[rows, :] = -pi * cre - pr * cim
            if kk < S5_CHUNK:
                back = slice((S5_CHUNK - 1 - kk) * SSM_GROUP, (S5_CHUNK - kk) * SSM_GROUP)
                br_re[back, :] = pr * bb_re - pi * bb_im
                br_im[back, :] = pi * bb_re + pr * bb_im
        kcat = (lax.dot_general(bb_re, wt_re[0:n_tap, :], nt, precision=hi, preferred_element_type=F32)
                + lax.dot_general(bb_im, wt_im[0:n_tap, :], nt, precision=hi, preferred_element_type=F32))
        for s in range(S5_CHUNK):
            shifted = kcat if s == 0 else pltpu.roll(kcat, s * SSM_GROUP, 1)
            rows = slice(s * SSM_GROUP, (s + 1) * SSM_GROUP)
            m_ref[0, gi, rows, :], m_ref[1, gi, rows, :] = _split_bf16(
                jnp.where(lane >= s * SSM_GROUP, shifted, 0.0))
        def place(v_re, v_im):
            zero = jnp.zeros_like(v_re)
            parts = [v_re, zero, v_im, zero] if gi == 0 else [zero, v_re, zero, v_im]
            return jnp.concatenate(parts, axis=1)

        ws_rows.append(place(br_re[...], br_im[...]))
        wot_rows.append(place(wt_re[SSM_GROUP:, :], wt_im[SSM_GROUP:, :]))
        kc = float(S5_CHUNK) * lax.broadcasted_iota(jnp.int32, (2 * SUBLANES, SSM_STATE), 0).astype(F32)
        magc = jnp.exp(kc * (dt * ar))
        angc = kc * (dt * ai)
        atab = atab + place(magc * jnp.cos(angc), magc * jnp.sin(angc))
    ws_ref[0], ws_ref[1] = _split_bf16(jnp.concatenate(ws_rows, axis=0))
    wot_ref[0], wot_ref[1] = _split_bf16(jnp.concatenate(wot_rows, axis=0))
    atab_ref[...] = atab


def _s5_prep(a_re, a_im, log_dt, b_re, b_im, c_re, c_im):
    p, n, c = a_re.shape[0] * N_PAIRS, SSM_STATE, SSM_GROUP
    n_tap = S5_CHUNK * c
    st = 4 * n

    def pspec(*shape):
        return pl.BlockSpec((None,) + shape, lambda i: (i,) + (0,) * len(shape))

    def pairs(a, *shape):
        return a.reshape((p, 2) + shape)

    return pl.pallas_call(
        _s5_prep_kernel,
        grid=(p,),
        in_specs=[pspec(2, 1, n), pspec(2, 1, n), pspec(2, 1, 1), pspec(2, c, n), pspec(2, c, n),
                  pspec(2, c, n), pspec(2, c, n)],
        out_specs=[pspec(2, 2, n_tap, n_tap), pspec(2, 2 * n_tap, st), pspec(2, 2 * n_tap, st),
                   pspec(2 * SUBLANES, st)],
        out_shape=[jax.ShapeDtypeStruct((p, 2, 2, n_tap, n_tap), BF16),
                   jax.ShapeDtypeStruct((p, 2, 2 * n_tap, st), BF16),
                   jax.ShapeDtypeStruct((p, 2, 2 * n_tap, st), BF16),
                   jax.ShapeDtypeStruct((p, 2 * SUBLANES, st), F32)],
        scratch_shapes=[pltpu.VMEM((n_tap + c, n), F32), pltpu.VMEM((n_tap + c, n), F32),
                        pltpu.VMEM((n_tap, n), F32), pltpu.VMEM((n_tap, n), F32)],
        compiler_params=_params("parallel"),
        name="s5_prep",
    )(pairs(a_re, 1, n), pairs(a_im, 1, n), pairs(log_dt, 1, 1),
      pairs(jnp.swapaxes(b_re, -1, -2), c, n), pairs(jnp.swapaxes(b_im, -1, -2), c, n),
      pairs(c_re, c, n), pairs(c_im, c, n))


def _block_transpose8(vs):
    lane = lax.broadcasted_iota(jnp.int32, vs[0].shape, 1)
    blk = lane >> _log2(SSM_GROUP)
    for d in (GROUPS_PER_SLAB >> s for s in range(1, _log2(GROUPS_PER_SLAB) + 1)):
        keep = (blk & d) == 0
        new = list(vs)
        for i in range(GROUPS_PER_SLAB):
            if i & d == 0:
                a, b = vs[i], vs[i + d]
                new[i] = jnp.where(keep, a, pltpu.roll(b, d * SSM_GROUP, 1))
                new[i + d] = jnp.where(keep, pltpu.roll(a, LANES - d * SSM_GROUP, 1), b)
        vs = new
    return vs


def _cmul(ar, ai, xr, xi):
    return ar * xr - ai * xi, ar * xi + ai * xr


def _s5_kernel(u_ref, h0_ref, m_ref, ws_ref, wo_ref, a_ref, y_ref, hf_ref, x_scr, yg_scr, s_scr, hp_scr,
               *, rows, independent, tail_rows):
    half = 2 * SSM_STATE
    tail = pl.ds(rows - tail_rows, tail_rows)
    rt = min(rows, S5_RELAYOUT_ROWS)
    half_chunk = S5_CHUNK // 2

    def gather_tile(t, carry):
        r0 = pl.multiple_of(t * rt, rt)
        for hf in range(2):
            vs = [u_ref[pl.ds(r0 * S5_CHUNK + hf * half_chunk + i, rt, stride=S5_CHUNK), :]
                  for i in range(half_chunk)]
            outs = _block_transpose8(vs)
            for g in range(GROUPS_PER_SLAB):
                x_scr[g, pl.ds(r0, rt), hf * LANES:(hf + 1) * LANES] = outs[g]
        return carry

    lax.fori_loop(0, rows // rt, gather_tile, 0)

    row = lax.broadcasted_iota(jnp.int32, (SUBLANES, half), 0)
    n_tap = S5_CHUNK * SSM_GROUP
    for pi in range(PAIRS_PER_SLAB):
        def part(w_ref, *idx):
            return w_ref[(pi, 0) + idx], w_ref[(pi, 1) + idx]

        x0 = x_scr[2 * pi]
        x1 = x_scr[2 * pi + 1]
        x01 = jnp.concatenate([x0, x1], axis=1)
        s_scr[...] = _pdot(x01, ws_ref[pi, 0])
        if tail_rows:
            s_scr[tail, :] += _pdot_lo_terms(x01[rows - tail_rows:], *part(ws_ref))
        ap = a_ref[pi]
        h0 = h0_ref[pi]
        if independent:
            hp_scr[...] = h0
            s = s_scr[...]
            n_re, n_im = _cmul(ap[1:2, :half], ap[1:2, half:], h0[:, :half], h0[:, half:])
            hf_ref[pi] = jnp.concatenate([n_re + s[:, :half], n_im + s[:, half:]], axis=1)
        else:
            pw_re, pw_im = ap[0:SUBLANES, :half], ap[0:SUBLANES, half:]

            def scan_tile(t, carry):
                h_re, h_im = carry
                r0 = pl.multiple_of(t * SUBLANES, SUBLANES)
                s = s_scr[pl.ds(r0, SUBLANES), :]
                t_re, t_im = s[:, :half], s[:, half:]
                for d in (1, 2, 4):
                    sh_re = jnp.where(row >= d, pltpu.roll(t_re, d, 0), 0.0)
                    sh_im = jnp.where(row >= d, pltpu.roll(t_im, d, 0), 0.0)
                    m_re, m_im = _cmul(ap[d:d + 1, :half], ap[d:d + 1, half:], sh_re, sh_im)
                    t_re, t_im = t_re + m_re, t_im + m_im
                e_re = jnp.where(row >= 1, pltpu.roll(t_re, 1, 0), 0.0)
                e_im = jnp.where(row >= 1, pltpu.roll(t_im, 1, 0), 0.0)
                c_re, c_im = _cmul(pw_re, pw_im, h_re, h_im)
                hp_scr[pl.ds(r0, SUBLANES), :] = jnp.concatenate([e_re + c_re, e_im + c_im], axis=1)
                o_re, o_im = _cmul(ap[SUBLANES:SUBLANES + 1, :half], ap[SUBLANES:SUBLANES + 1, half:],
                                   h_re, h_im)
                last = SUBLANES - 1
                n_re = jnp.broadcast_to(t_re[last:last + 1], h_re.shape) + o_re
                n_im = jnp.broadcast_to(t_im[last:last + 1], h_im.shape) + o_im
                return n_re, n_im

            init = (jnp.broadcast_to(h0[:, :half], (SUBLANES, half)),
                    jnp.broadcast_to(h0[:, half:], (SUBLANES, half)))
            h_re, h_im = lax.fori_loop(0, rows // SUBLANES, scan_tile, init, unroll=S5_SCAN_UNROLL)
            hf_ref[pi] = jnp.concatenate([h_re[0:1], h_im[0:1]], axis=1)
        nt = (((1,), (1,)), ((), ()))
        yc = _pdot(hp_scr[...], wo_ref[pi, 0], dims=nt)
        yg_scr[2 * pi] = _pdot(x0, m_ref[pi, 0, 0]) + yc[:, :n_tap]
        yg_scr[2 * pi + 1] = _pdot(x1, m_ref[pi, 0, 1]) + yc[:, n_tap:]
        if tail_rows:
            yc_lo = _pdot_lo_terms(hp_scr[tail, :], *part(wo_ref), dims=nt)
            yg_scr[2 * pi, tail, :] += _pdot_lo_terms(x0[rows - tail_rows:], *part(m_ref, 0)) + yc_lo[:, :n_tap]
            yg_scr[2 * pi + 1, tail, :] += (_pdot_lo_terms(x1[rows - tail_rows:], *part(m_ref, 1))
                                            + yc_lo[:, n_tap:])

    def scatter_tile(t, carry):
        r0 = pl.multiple_of(t * rt, rt)
        for hf in range(2):
            vs = [yg_scr[g, pl.ds(r0, rt), hf * LANES:(hf + 1) * LANES] for g in range(GROUPS_PER_SLAB)]
            outs = _block_transpose8(vs)
            for i in range(half_chunk):
                y_ref[pl.ds(r0 * S5_CHUNK + hf * half_chunk + i, rt, stride=S5_CHUNK), :] = outs[i]
        return carry

    lax.fori_loop(0, rows // rt, scatter_tile, 0)


def _s5_mixer(u, h0_re, h0_im, prep, layer, bsz, t_len, tail):
    m, wsp, wop, a16 = prep
    parts = 2 if tail > 0 else 1
    n_tap = S5_CHUNK * SSM_GROUP
    st = 4 * SSM_STATE
    independent = t_len == S5_CHUNK
    if independent:
        nblk, rows, hrows = 1, bsz, bsz
    else:
        nblk, rows, hrows = bsz, t_len // S5_CHUNK, 1
    assert t_len % S5_CHUNK == 0 and rows % SUBLANES == 0, (bsz, t_len)
    h0p = jnp.concatenate([h0_re.reshape(bsz, N_PAIRS, 2 * SSM_STATE),
                           h0_im.reshape(bsz, N_PAIRS, 2 * SSM_STATE)], axis=2).astype(F32)
    h0p = h0p.transpose(1, 0, 2)[None] if independent else h0p[:, :, None, :]
    pp = PAIRS_PER_SLAB

    def wspec(*shape):
        return pl.BlockSpec((pp,) + shape, lambda q, b: (layer * N_SLABS + q,) + (0,) * len(shape))

    frames = rows * S5_CHUNK
    if tail <= 0:
        tail_rows = 0
    elif independent:
        tail_rows = rows
    else:
        tail_rows = min(rows, -(-tail // (S5_CHUNK * SUBLANES)) * SUBLANES)
    y, hf = pl.pallas_call(
        functools.partial(_s5_kernel, rows=rows, independent=independent, tail_rows=tail_rows),
        grid=(N_SLABS, nblk),
        in_specs=[pl.BlockSpec((None, frames, LANES), lambda q, b: (q, b, 0)),
                  pl.BlockSpec((None, pp, hrows, st), lambda q, b: (b, q, 0, 0)),
                  wspec(parts, 2, n_tap, n_tap), wspec(parts, 2 * n_tap, st), wspec(parts, 2 * n_tap, st),
                  wspec(2 * SUBLANES, st)],
        out_specs=[pl.BlockSpec((None, frames, LANES), lambda q, b: (q, b, 0)),
                   pl.BlockSpec((None, pp, hrows, st), lambda q, b: (b, q, 0, 0))],
        out_shape=[jax.ShapeDtypeStruct(u.shape, F32),
                   jax.ShapeDtypeStruct((nblk, N_PAIRS, hrows, st), F32)],
        scratch_shapes=[pltpu.VMEM((GROUPS_PER_SLAB, rows, n_tap), F32),
                        pltpu.VMEM((GROUPS_PER_SLAB, rows, n_tap), F32),
                        pltpu.VMEM((rows, st), F32), pltpu.VMEM((rows, st), F32)],
        compiler_params=_params("parallel", "parallel"),
        name="s5_core",
    )(u, h0p, m, wsp, wop, a16)
    hf = hf[0].transpose(1, 0, 2) if independent else hf[:, :, 0, :]
    hf_re = hf[:, :, :2 * SSM_STATE].reshape(bsz, N_GROUPS, SSM_STATE)
    hf_im = hf[:, :, 2 * SSM_STATE:].reshape(bsz, N_GROUPS, SSM_STATE)
    return y, hf_re, hf_im


def _conv_kernel(v_ref, left_ref, w_ref, b_ref, g_ref, beta_ref, y_ref, cn_ref, vbuf, shifted, *, tm):
    @pl.when(pl.program_id(1) == 0)
    def _():
        vbuf[0:CONV_HALO, :] = left_ref[...]

    vbuf[CONV_HALO:CONV_HALO + tm, :] = v_ref[...]
    first = CONV_HALO - (CONV_WIDTH - 1)
    span = tm + CONV_HALO - SUBLANES
    for r in range(1, SUBLANES):
        shifted[r - 1, 0:span, :] = vbuf[r:r + span, :]
    acc = jnp.zeros((tm, D_CONV), F32)
    for k in range(CONV_WIDTH):
        a, r = divmod(first + k, SUBLANES)
        src = vbuf if r == 0 else shifted.at[r - 1]
        acc = acc + w_ref[k:k + 1, :] * src[a * SUBLANES:a * SUBLANES + tm, :]
    y = acc + b_ref[...]
    mu = jnp.mean(y, axis=-1, keepdims=True)
    yc = y - mu
    var = jnp.mean(yc * yc, axis=-1, keepdims=True)
    yn = yc * lax.rsqrt(var + EPS) * g_ref[...] + beta_ref[...]
    y_ref[...] = yn * jax.nn.sigmoid(yn)
    cn_ref[...] = vbuf[tm + first:tm + CONV_HALO, :]
    vbuf[0:CONV_HALO, :] = vbuf[tm:tm + CONV_HALO, :]


def _conv_mixer(v, left, w, b, g, beta, layer, bsz, t_len, tm):
    left = jnp.pad(left.astype(F32), ((0, 0), (CONV_HALO - (CONV_WIDTH - 1), 0), (0, 0)))
    nt = t_len // tm
    vec = _layer_spec(layer, 1, D_CONV)
    return pl.pallas_call(
        functools.partial(_conv_kernel, tm=tm),
        grid=(bsz, nt),
        in_specs=[pl.BlockSpec((tm, D_CONV), lambda bi, j: (bi * nt + j, 0)),
                  pl.BlockSpec((None, CONV_HALO, D_CONV), lambda bi, j: (bi, 0, 0)),
                  _layer_spec(layer, CONV_WIDTH, D_CONV), vec, vec, vec],
        out_specs=[pl.BlockSpec((tm, D_CONV), lambda bi, j: (bi * nt + j, 0)),
                   pl.BlockSpec((None, CONV_WIDTH - 1, D_CONV), lambda bi, j: (bi, 0, 0))],
        out_shape=[jax.ShapeDtypeStruct((bsz * t_len, D_CONV), F32),
                   jax.ShapeDtypeStruct((bsz, CONV_WIDTH - 1, D_CONV), F32)],
        scratch_shapes=[pltpu.VMEM((tm + CONV_HALO, D_CONV), F32),
                        pltpu.VMEM((SUBLANES - 1, tm + CONV_HALO - SUBLANES, D_CONV), F32)],
        compiler_params=_params("parallel", "arbitrary"),
        name="conv_mixer",
    )(v, left, w, _vec(b), _vec(g), _vec(beta))


def _route(logits):
    lane_i = lax.broadcasted_iota(jnp.int32, logits.shape, 1)
    lane = lane_i.astype(F32)
    group_of_lane = (lane_i >> _log2(EXPERTS_PER_GROUP)).astype(F32)
    neg = -jnp.inf
    far = float(LANES)
    is_g = (lane_i >= N_EXPERTS) & (lane_i < N_EXPERTS + N_EXPERT_GROUPS)
    gl = jnp.where(is_g, logits, neg)
    g_max = jnp.max(gl, axis=-1, keepdims=True)
    g_lane = jnp.min(jnp.where(gl == g_max, lane, far), axis=-1, keepdims=True)
    g_gate = 1.0 / jnp.sum(jnp.exp(gl - g_max), axis=-1, keepdims=True)
    g_idx = g_lane - float(N_EXPERTS)
    in_group = (lane_i < N_EXPERTS) & (group_of_lane == g_idx)
    el = jnp.where(in_group, logits, neg)
    v1 = jnp.max(el, axis=-1, keepdims=True)
    i1 = jnp.min(jnp.where(el == v1, lane, far), axis=-1, keepdims=True)
    el2 = jnp.where(lane == i1, neg, el)
    v2 = jnp.max(el2, axis=-1, keepdims=True)
    i2 = jnp.min(jnp.where(el2 == v2, lane, far), axis=-1, keepdims=True)
    e2 = jnp.exp(v2 - v1)
    w1 = g_gate / (1.0 + e2)
    w2 = g_gate * e2 / (1.0 + e2)
    return i1, i2, w1, w2


ROUTE_E1, ROUTE_E2, ROUTE_W1, ROUTE_W2, ROUTE_RANK1, ROUTE_RANK2 = range(6)


def _outproj_kernel(x_ref, ys_ref, u_ref, yb_ref, d_ref, wglu_ref, wout_ref, nf_ref, rw_ref, rb_ref, cnt0_ref,
                    x1_ref, hn_ref, route_ref, rt_ref, cnt_ref, wglu_bf, wout_bf, rw_bf, *, tail_tiles):
    @pl.when(pl.program_id(0) == 0)
    def _():
        cnt_ref[...] = cnt0_ref[...]
        rw = rw_ref[...]
        rw_hi = rw.astype(BF16)
        rw_bf[:, 0:LANES] = rw_hi
        rw_bf[:, LANES:] = (rw - rw_hi.astype(F32)).astype(BF16)

    _cast_weight_once(wglu_ref, wglu_bf)
    _cast_weight_once(wout_ref, wout_bf)
    def mix_in(precise):
        ys = jnp.concatenate([ys_ref[q] for q in range(N_SLABS)], axis=1)
        u = jnp.concatenate([u_ref[q] for q in range(N_SLABS)], axis=1)
        z = jax.nn.gelu(ys + d_ref[...] * u)
        ya = z * jax.nn.sigmoid(_pdot(z, *_parts(wglu_bf, precise=precise)))
        mix = _pdot(jnp.concatenate([ya, yb_ref[...]], axis=1), *_parts(wout_bf, precise=precise))
        x1_ref[...] = x_ref[...] + mix

    _per_tile_precision(tail_tiles, wout_bf.shape[0] == 2, mix_in)
    x1 = x1_ref[...]
    hn = _rms(x1, nf_ref[...])
    _store_row_tiles(hn_ref, hn)
    h_hi = hn.astype(BF16)
    h_lo = (hn - h_hi.astype(F32)).astype(BF16)
    hw = jnp.dot(h_hi, rw_bf[...], preferred_element_type=F32)
    logits = (hw[:, :LANES] + hw[:, LANES:]
              + jnp.dot(h_lo, rw_bf[:, 0:LANES], preferred_element_type=F32) + rb_ref[...])
    i1, i2, w1, w2 = _route(logits)
    tm = logits.shape[0]
    lane_i = lax.broadcasted_iota(jnp.int32, logits.shape, 1)
    lane = lane_i.astype(F32)
    picked = jnp.where((lane == i1) | (lane == i2), 1.0, 0.0)
    earlier = (lax.broadcasted_iota(jnp.int32, (tm, tm), 0) > lax.broadcasted_iota(jnp.int32, (tm, tm), 1))
    prefix = _bdot(jnp.where(earlier, 1.0, 0.0), picked.astype(BF16)) + cnt_ref[...]
    rank1 = jnp.sum(jnp.where(lane == i1, prefix, 0.0), axis=-1, keepdims=True)
    rank2 = jnp.sum(jnp.where(lane == i2, prefix, 0.0), axis=-1, keepdims=True)
    cnt_ref[...] += jnp.sum(picked, axis=0, keepdims=True)
    rec = jnp.zeros_like(logits)
    for lane_id, val in ((ROUTE_E1, i1), (ROUTE_E2, i2), (ROUTE_W1, w1), (ROUTE_W2, w2),
                         (ROUTE_RANK1, rank1), (ROUTE_RANK2, rank2)):
        rec = jnp.where(lane_i == lane_id, val, rec)
    route_ref[...] = rec
    rt_ref[...] = rec.T[0:SUBLANES, :]


def _outproj(x, ys, u, yb, d, wglu, wout, nf, rw, rb, counts0, layer, tm, t_len, tail):
    n = x.shape[0]
    precise = tail > 0
    return pl.pallas_call(
        functools.partial(_outproj_kernel, tail_tiles=_stream_tail_tiles(t_len, tm, tail)),
        grid=(n // tm,),
        in_specs=[_row_spec(tm, D_MODEL), _slab_spec(tm), _slab_spec(tm), _row_spec(tm, D_CONV),
                  _layer_spec(layer, 1, D_SSM), _layer_spec(layer, D_SSM, D_SSM),
                  _layer_spec(layer, D_MODEL, D_MODEL), _layer_spec(layer, 1, D_MODEL),
                  _const_spec((D_MODEL, LANES)), _const_spec((1, LANES)), _const_spec((1, LANES))],
        out_specs=[_row_spec(tm, D_MODEL), _row_tile_spec(tm), _row_spec(tm, LANES),
                   pl.BlockSpec((SUBLANES, tm), lambda i: (0, i)), _const_spec((1, LANES))],
        out_shape=[jax.ShapeDtypeStruct((n, D_MODEL), F32), jax.ShapeDtypeStruct(_tiled_rows(n), F32),
                   jax.ShapeDtypeStruct((n, LANES), F32), jax.ShapeDtypeStruct((SUBLANES, n), F32),
                   jax.ShapeDtypeStruct((1, LANES), F32)],
        scratch_shapes=[_weight_scratch(D_SSM, D_SSM, precise), _weight_scratch(D_MODEL, D_MODEL, precise),
                        pltpu.VMEM((D_MODEL, 2 * LANES), BF16)],
        compiler_params=_params("arbitrary"),
        name="outproj_router",
    )(x, ys, u, yb, _vec(d), wglu, wout, _vec(nf), rw, rb, counts0)


PLAN_TILE_LANES = 2 * LANES
PLAN_EXPERT, PLAN_ROWS, PLAN_USED = range(3)


def _moe_plan_kernel(rt_ref, cnt_ref, cnt_first_ref, slot_ref, tile_ref, *, tms):
    cnt = cnt_ref[...]
    padded = jnp.ceil(cnt * (1.0 / tms)) * float(tms)
    r = lax.broadcasted_iota(jnp.int32, (LANES, LANES), 0)
    c = lax.broadcasted_iota(jnp.int32, (LANES, LANES), 1)
    ends = jnp.dot(padded, jnp.where(r <= c, 1.0, 0.0), precision=lax.Precision.HIGHEST,
                   preferred_element_type=F32)
    starts = ends - padded
    rt = rt_ref[...]
    e1, e2 = rt[ROUTE_E1:ROUTE_E1 + 1], rt[ROUTE_E2:ROUTE_E2 + 1]
    s1, s2 = rt[ROUTE_RANK1:ROUTE_RANK1 + 1], rt[ROUTE_RANK2:ROUTE_RANK2 + 1]
    tile = lax.broadcasted_iota(jnp.int32, (1, PLAN_TILE_LANES), 1).astype(F32)
    used = ends[:, N_EXPERTS - 1:N_EXPERTS] * (1.0 / tms)
    pos = jnp.minimum(tile, used - 1.0) * float(tms)
    t_exp = jnp.zeros_like(tile)
    t_fill = jnp.zeros_like(tile)
    for e in range(N_EXPERTS):
        st, en = starts[:, e:e + 1], ends[:, e:e + 1]
        s1 = s1 + jnp.where(e1 == float(e), st, 0.0)
        s2 = s2 + jnp.where(e2 == float(e), st, 0.0)
        mine = (pos >= st) & (pos < en)
        t_exp = t_exp + jnp.where(mine, float(e), 0.0)
        t_fill = t_fill + jnp.where(mine, st + cnt_first_ref[:, e:e + 1], 0.0)
    t_rows = jnp.where(tile < used, jnp.clip(t_fill - pos, 0.0, float(tms)), 0.0)
    slot_ref[...] = jnp.concatenate([s1, s2], axis=0).astype(jnp.int32)
    tile_ref[...] = jnp.concatenate(
        [t_exp, t_rows, jnp.broadcast_to(used, tile.shape), jnp.zeros((SUBLANES - 3, PLAN_TILE_LANES), F32)],
        axis=0).astype(jnp.int32)


def _moe_plan(route_t, counts, counts_first, tms, n_tiles):
    n = route_t.shape[1]
    assert n_tiles <= PLAN_TILE_LANES
    slots, tiles = pl.pallas_call(
        functools.partial(_moe_plan_kernel, tms=tms),
        out_shape=[jax.ShapeDtypeStruct((2, n), jnp.int32),
                   jax.ShapeDtypeStruct((SUBLANES, PLAN_TILE_LANES), jnp.int32)],
        compiler_params=pltpu.CompilerParams(vmem_limit_bytes=VMEM_LIMIT),
        name="moe_plan",
    )(route_t, counts, counts_first)
    return slots, tiles[PLAN_EXPERT, :n_tiles], tiles[PLAN_ROWS, :n_tiles], tiles[PLAN_USED, :1]


DMA_UNROLL = 16


HN_BUFFERS = 3
ZERO_FILL_ROWS = 128


def _dispatch_kernel(tr_ref, slot_ref, hn_ref, *rest, tm, tms, n_tiles, n_steps, fresh):
    xs_ref, zbuf, hbuf, in_sem, out_sem, zsem = rest if fresh else rest[1:]
    i = pl.program_id(0)

    def fetch(t):
        b = lax.rem(t, HN_BUFFERS)
        first = pl.multiple_of(t * (tm * ROW_SUB), tm * ROW_SUB)
        return pltpu.make_async_copy(hn_ref.at[pl.ds(first, tm * ROW_SUB)], hbuf.at[b], in_sem.at[b])

    def drain_scatter(t):
        b = lax.rem(t, HN_BUFFERS)
        for k in range(2):
            pltpu.make_async_copy(hbuf.at[b], xs_ref.at[pl.ds(0, tm * ROW_SUB)], out_sem.at[b]).wait()

    @pl.when(i == 0)
    def _():
        fetch(i).start()

    @pl.when(jnp.logical_and(i == 0, fresh))
    def _():
        zbuf[...] = jnp.zeros_like(zbuf)

        zrows = zbuf.shape[0] // ROW_SUB
        per_tile = tms // zrows

        def unfilled(p):
            return tr_ref[p // per_tile] < (lax.rem(p, per_tile) + 1) * zrows

        def fill(p, carry):
            @pl.when(unfilled(p))
            def _():
                first = pl.multiple_of(p * (zrows * ROW_SUB), zrows * ROW_SUB)
                pltpu.make_async_copy(zbuf, xs_ref.at[pl.ds(first, zrows * ROW_SUB)], zsem).start()
            return carry

        def drain(p, carry):
            @pl.when(unfilled(p))
            def _():
                pltpu.make_async_copy(zbuf, xs_ref.at[pl.ds(0, zrows * ROW_SUB)], zsem).wait()
            return carry

        lax.fori_loop(0, n_tiles * per_tile, fill, 0)
        lax.fori_loop(0, n_tiles * per_tile, drain, 0)

    @pl.when(i + 1 < n_steps)
    def _():
        fetch(i + 1).start()

    fetch(i).wait()
    b = lax.rem(i, HN_BUFFERS)
    rows = hbuf.at[b]

    def issue(r, carry):
        for k in range(2):
            pltpu.make_async_copy(_one_row(rows, r), _one_row(xs_ref, slot_ref[k, r]),
                                  out_sem.at[b]).start(priority=k)
        return carry

    lax.fori_loop(0, tm, issue, 0, unroll=DMA_UNROLL)

    @pl.when(i >= 1)
    def _():
        drain_scatter(i - 1)

    @pl.when(i == n_steps - 1)
    def _():
        drain_scatter(i)


def _dispatch(hn, slots, tile_rows, tms, tm, into=None):
    n = hn.shape[0] // ROW_SUB
    n_tiles = tile_rows.shape[0]
    fresh = into is None
    any_spec = pl.BlockSpec(memory_space=pl.ANY)
    return pl.pallas_call(
        functools.partial(_dispatch_kernel, tm=tm, tms=tms, n_tiles=n_tiles, n_steps=n // tm, fresh=fresh),
        grid_spec=pltpu.PrefetchScalarGridSpec(
            num_scalar_prefetch=1, grid=(n // tm,),
            in_specs=[pl.BlockSpec((2, tm), lambda i, tr: (0, i), memory_space=pltpu.SMEM), any_spec]
                     + ([] if fresh else [any_spec]),
            out_specs=any_spec,
            scratch_shapes=[pltpu.VMEM(_tiled_rows(min(tms, ZERO_FILL_ROWS)), F32),
                            pltpu.VMEM((HN_BUFFERS,) + _tiled_rows(tm), F32),
                            pltpu.SemaphoreType.DMA((HN_BUFFERS,)), pltpu.SemaphoreType.DMA((HN_BUFFERS,)),
                            pltpu.SemaphoreType.DMA]),
        out_shape=jax.ShapeDtypeStruct(_tiled_rows(n_tiles * tms), F32),
        input_output_aliases={} if fresh else {3: 0},
        compiler_params=_params("arbitrary"),
        name="moe_dispatch",
    )(tile_rows, slots, hn, *([] if fresh else [into]))


def _moe_kernel(te_ref, nu_ref, x_ref, wg_ref, wu_ref, wd_ref, y_ref, wg_bf, wu_bf, wd_bf):
    i = pl.program_id(0)
    in_use = i < nu_ref[0]
    new_expert = (i == 0) | (te_ref[i] != te_ref[jnp.maximum(i - 1, 0)])

    @pl.when(in_use & new_expert)
    def _():
        wg_bf[...] = wg_ref[...].astype(BF16)
        wu_bf[...] = wu_ref[...].astype(BF16)
        wd_bf[...] = wd_ref[...].astype(BF16)

    @pl.when(in_use)
    def _():
        h = _load_row_tiles(x_ref).astype(BF16)
        hg = jnp.dot(h, wg_bf[...], preferred_element_type=F32)
        hu = jnp.dot(h, wu_bf[...], preferred_element_type=F32)
        _store_row_tiles(y_ref, _bdot(hg * jax.nn.sigmoid(hg) * hu, wd_bf[...]))

    @pl.when(jnp.logical_not(in_use))
    def _():
        y_ref[...] = jnp.zeros_like(y_ref)


def _moe(xs, tile_expert, n_used, wg, wu, wd, layer, tms):
    n_slots = xs.shape[0] // ROW_SUB
    rows = pl.BlockSpec(_tiled_rows(tms), lambda i, te, nu: (jnp.minimum(i, nu[0] - 1), 0))
    out_rows = pl.BlockSpec(_tiled_rows(tms), lambda i, te, nu: (i, 0))

    def wspec(a, b):
        return pl.BlockSpec((None, None, a, b), lambda i, te, nu: (layer, te[i], 0, 0))

    return pl.pallas_call(
        _moe_kernel,
        grid_spec=pltpu.PrefetchScalarGridSpec(
            num_scalar_prefetch=2, grid=(n_slots // tms,),
            in_specs=[rows, wspec(D_MODEL, D_EXPERT), wspec(D_MODEL, D_EXPERT), wspec(D_EXPERT, D_MODEL)],
            out_specs=out_rows,
            scratch_shapes=[pltpu.VMEM((D_MODEL, D_EXPERT), BF16), pltpu.VMEM((D_MODEL, D_EXPERT), BF16),
                            pltpu.VMEM((D_EXPERT, D_MODEL), BF16)]),
        out_shape=jax.ShapeDtypeStruct(xs.shape, F32),
        compiler_params=_params("arbitrary"),
        name="moe",
    )(tile_expert, n_used, xs, wg, wu, wd)


def _ple_kernel(slot_ref, next_slot_ref, x_ref, route_ref, p_ref, np_ref, wple_ref, wgate_ref, nfin_ref,
                ys_ref, o_ref, ybuf, sem, wple_bf, wgate_bf, *, tm, n_steps, final):
    i = pl.program_id(0)
    _cast_weight_once(wple_ref, wple_bf)
    _cast_weight_once(wgate_ref, wgate_bf)

    def gather(slots, b):
        def issue(r, carry):
            for k in range(2):
                pltpu.make_async_copy(_one_row(ys_ref, slots[k, r]), _one_row(ybuf.at[b, k], r),
                                      sem.at[b]).start(priority=k)
            return carry

        lax.fori_loop(0, tm, issue, 0, unroll=DMA_UNROLL)

    @pl.when(i == 0)
    def _():
        gather(slot_ref, 0)

    @pl.when(i + 1 < n_steps)
    def _():
        gather(next_slot_ref, lax.rem(i + 1, 2))

    pe = _pdot(p_ref[...], *_parts(wple_bf))
    b = lax.rem(i, 2)
    for k in range(2):
        pltpu.make_async_copy(ys_ref.at[pl.ds(0, tm * ROW_SUB)], ybuf.at[b, k], sem.at[b]).wait()
    route = route_ref[...]
    x = (x_ref[...] + route[:, ROUTE_W1:ROUTE_W1 + 1] * _load_row_tiles(ybuf.at[b, 0])
         + route[:, ROUTE_W2:ROUTE_W2 + 1] * _load_row_tiles(ybuf.at[b, 1]))
    gate = jax.nn.sigmoid(_pdot(_rms(x, np_ref[...]), *_parts(wgate_bf)))
    out = x + pe * gate
    if final:
        out = _rms(out, nfin_ref[...])
    o_ref[...] = out


def _ple(x, route, slots, ys, p, npl, wple, wgate, nfin, layer, tm, final):
    n = x.shape[0]
    n_steps = n // tm
    return pl.pallas_call(
        functools.partial(_ple_kernel, tm=tm, n_steps=n_steps, final=final),
        grid=(n_steps,),
        in_specs=[pl.BlockSpec((2, tm), lambda i: (0, i), memory_space=pltpu.SMEM),
                  pl.BlockSpec((2, tm), lambda i: (0, jnp.minimum(i + 1, n_steps - 1)),
                               memory_space=pltpu.SMEM),
                  _row_spec(tm, D_MODEL), _row_spec(tm, LANES),
                  pl.BlockSpec((None, tm, D_PLE), lambda i: (layer, i, 0)),
                  _layer_spec(layer, 1, D_MODEL), _layer_spec(layer, D_PLE, D_MODEL),
                  _layer_spec(layer, D_MODEL, D_MODEL), _const_spec((1, D_MODEL)),
                  pl.BlockSpec(memory_space=pl.ANY)],
        out_specs=_row_spec(tm, D_MODEL),
        out_shape=jax.ShapeDtypeStruct((n, D_MODEL), F32),
        scratch_shapes=[pltpu.VMEM((2, 2) + _tiled_rows(tm), F32), pltpu.SemaphoreType.DMA((2,)),
                        _weight_scratch(D_PLE, D_MODEL, False), _weight_scratch(D_MODEL, D_MODEL, False)],
        compiler_params=_params("arbitrary"),
        name="combine_ple",
    )(slots, slots, x, route, p, _vec(npl), wple, wgate, nfin.reshape(1, D_MODEL), ys)


def _mixers(x, h0_re, h0_im, conv_left, w, s5, router_w, router_b, counts0, layer, bsz, t_len, tm, tm_conv,
            tail):
    u, v = _inproj(x, w["norm_mix"], w["w_in"], layer, min(2 * tm, x.shape[0]), t_len, tail)
    ys, hf_re, hf_im = _s5_mixer(u, h0_re, h0_im, s5, layer, bsz, t_len, tail)
    yb, conv_new = _conv_mixer(v, conv_left, w["conv_w"], w["conv_b"], w["conv_ln_g"], w["conv_ln_b"],
                               layer, bsz, t_len, tm_conv)
    x1, hn, route, route_t, counts = _outproj(x, ys, u, yb, w["ssm_d"], w["w_ssm_glu"], w["w_out"],
                                              w["norm_ffn"], router_w, router_b, counts0, layer, tm, t_len, tail)
    return dict(x1=x1, hn=hn, route=route, route_t=route_t, counts=counts, state=(hf_re, hf_im, conv_new))


def _moe_and_ple(sets, ps, tms_rows, w, layer, tms, final):
    sizes = [s["x1"].shape[0] for s in sets]
    n_tiles = 2 * sum(sizes) // tms + N_EXPERTS
    route_t = jnp.concatenate([s["route_t"] for s in sets], axis=1)
    slots, tile_expert, tile_rows, n_used = _moe_plan(route_t, sets[-1]["counts"], sets[0]["counts"], tms,
                                                     n_tiles)
    starts = [sum(sizes[:j]) for j in range(len(sets))]
    set_slots = [slots[:, a:a + n] for a, n in zip(starts, sizes)]
    xsort = None
    for s, sl, tm in zip(sets, set_slots, tms_rows):
        xsort = _dispatch(s["hn"], sl, tile_rows, tms, min(2 * tm, s["x1"].shape[0]), into=xsort)
    ysort = _moe(xsort, tile_expert, n_used, w["expert_w_gate"], w["expert_w_up"], w["expert_w_down"],
                 layer, tms)
    return [_ple(s["x1"], s["route"], sl, ysort, p, w["norm_ple"], w["ple_w"], w["ple_gate_w"],
                 w["norm_final"], layer, tm, final)
            for s, sl, p, tm in zip(sets, set_slots, ps, tms_rows)]


def kernel(x_prompt, x_sample, p_prompt, p_sample, state_ssm_re, state_ssm_im, cache_conv, norm_mix, w_in, ssm_a_re, ssm_a_im, ssm_b_re, ssm_b_im, ssm_c_re, ssm_c_im, ssm_d, ssm_log_dt, w_ssm_glu, conv_w, conv_b, conv_ln_g, conv_ln_b, w_out, norm_ffn, router_group_w, router_group_b, router_expert_w, router_expert_b, expert_w_gate, expert_w_up, expert_w_down, norm_ple, ple_w, ple_gate_w, norm_final):
    depth = w_in.shape[0]
    bp, tp, _ = x_prompt.shape
    bs, ts, _ = x_sample.shape
    xp = x_prompt.reshape(bp * tp, D_MODEL)
    xs = x_sample.reshape(bs * ts, D_MODEL)
    pp = p_prompt.reshape(depth, bp * tp, D_PLE)
    ps = p_sample.reshape(depth, bs * ts, D_PLE)
    zero_state = jnp.zeros((bp, N_GROUPS, SSM_STATE), F32)
    zero_conv = jnp.zeros((bp, CONV_WIDTH - 1, D_CONV), F32)
    pad_lanes = LANES - N_EXPERTS - N_EXPERT_GROUPS
    w = {"norm_mix": norm_mix, "w_in": w_in, "ssm_d": ssm_d, "w_ssm_glu": w_ssm_glu, "conv_w": conv_w,
         "conv_b": conv_b, "conv_ln_g": conv_ln_g, "conv_ln_b": conv_ln_b, "w_out": w_out,
         "norm_ffn": norm_ffn, "expert_w_gate": expert_w_gate, "expert_w_up": expert_w_up,
         "expert_w_down": expert_w_down, "norm_ple": norm_ple, "ple_w": ple_w, "ple_gate_w": ple_gate_w,
         "norm_final": norm_final}
    no_picks = jnp.zeros((1, LANES), F32)
    outs = {k: [] for k in ("pr_re", "pr_im", "pr_conv", "sm_re", "sm_im", "sm_conv")}
    s5 = _s5_prep(ssm_a_re, ssm_a_im, ssm_log_dt, ssm_b_re, ssm_b_im, ssm_c_re, ssm_c_im)
    for i in range(depth):
        router_w = jnp.pad(jnp.concatenate([router_expert_w[i], router_group_w[i]], axis=1),
                           ((0, 0), (0, pad_lanes)))
        router_b = jnp.pad(jnp.concatenate([router_expert_b[i], router_group_b[i]]),
                           (0, pad_lanes)).reshape(1, LANES)
        final = i == depth - 1
        tail = 0 if final else PRECISE_TAIL
        mp = _mixers(xp, zero_state, zero_state, zero_conv, w, s5, router_w, router_b, no_picks, i,
                     bp, tp, tm=TOKEN_TILE, tm_conv=2 * TOKEN_TILE, tail=tail)
        ms = _mixers(xs, state_ssm_re[i], state_ssm_im[i], cache_conv[i], w, s5, router_w, router_b,
                     mp["counts"], i, bs, ts, tm=bs * ts, tm_conv=ts, tail=tail)
        xp, xs = _moe_and_ple([mp, ms], [pp, ps], [TOKEN_TILE, bs * ts], w, i, SLOT_TILE, final)
        for key, val in zip(("pr_re", "pr_im", "pr_conv"), mp["state"]):
            outs[key].append(val)
        for key, val in zip(("sm_re", "sm_im", "sm_conv"), ms["state"]):
            outs[key].append(val)
    return (xp.reshape(bp, tp, D_MODEL), xs.reshape(bs, ts, D_MODEL),
            jnp.stack(outs["pr_re"]), jnp.stack(outs["pr_im"]), jnp.stack(outs["pr_conv"]),
            jnp.stack(outs["sm_re"]), jnp.stack(outs["sm_im"]), jnp.stack(outs["sm_conv"]))
```
